```python
import jax, jax.numpy as jnp
from jax import lax
import numpy as np

D_MODEL = 1024
BATCH = 16
SEQ = 2048
DEPTH = 2

N_MIXERS = 2
RMS_EPS = 1e-6

SB_HEADS = 8
SB_HEAD_DIM = D_MODEL // SB_HEADS
SB_WIDTH = SB_HEADS * SB_HEAD_DIM
Q_BLOCK = 128

HG_HEAD_K = 128
HG_HEADS = D_MODEL // HG_HEAD_K
HG_HEAD_V = D_MODEL // HG_HEADS
HG_KEY_WIDTH = HG_HEADS * HG_HEAD_K
HG_VALUE_WIDTH = HG_HEADS * HG_HEAD_V
CHUNK = 64

kernel_name = "hybrid_stickbreaking_hgrn2_trunk"


def rms_norm(x, gain):
    xf = x.astype(jnp.float32)
    y = xf * lax.rsqrt(jnp.mean(xf * xf, axis=-1, keepdims=True) + RMS_EPS)
    return (y * gain.astype(jnp.float32)).astype(x.dtype)


def split_heads(t, n_heads):
    b, s, _ = t.shape
    return t.reshape(b, s, n_heads, -1).transpose(0, 2, 1, 3)


def merge_heads(t):
    b, h, s, d = t.shape
    return t.transpose(0, 2, 1, 3).reshape(b, s, h * d)


def stick_breaking_attention(q, k, v):
    seq = q.shape[2]
    scale = q.shape[-1] ** -0.5
    outs = []
    for blk in range(seq // Q_BLOCK):
        start = blk * Q_BLOCK
        end = start + Q_BLOCK
        qb = q[:, :, start:end]
        kb = k[:, :, :end]
        vb = v[:, :, :end]
        z = jnp.einsum('bhtd,bhsd->bhts', qb, kb).astype(jnp.float32) * scale
        t_idx = start + jnp.arange(Q_BLOCK)[:, None]
        s_idx = jnp.arange(end)[None, :]
        causal = s_idx < t_idx
        log_beta = jax.nn.log_sigmoid(z)
        log_skip_j = jnp.where(causal, jax.nn.log_sigmoid(-z), 0.0)
        log_skip = lax.cumsum(log_skip_j, axis=3, reverse=True) - log_skip_j
        weights = jnp.where(causal, jnp.exp(log_beta + log_skip), 0.0)
        outs.append(jnp.einsum('bhts,bhsd->bhtd', weights.astype(vb.dtype), vb))
    return jnp.concatenate(outs, axis=2)


def hgrn2_chunkwise(q, log_f, k, i):
    b, h, s, dk = q.shape
    dv = i.shape[-1]
    n = s // CHUNK

    def to_chunks(a):
        return a.astype(jnp.float32).reshape(b, h, n, CHUNK, a.shape[-1]).transpose(2, 0, 1, 3, 4)

    qc, gc, kc, ic = to_chunks(q), to_chunks(log_f), to_chunks(k), to_chunks(i)
    incl = jnp.tril(jnp.ones((CHUNK, CHUNK), dtype=bool))[None, None, :, :, None]

    def step(state, xs):
        qb, gb, kb, ib = xs
        g_cum = jnp.cumsum(gb, axis=2)
        rel = g_cum[:, :, :, None, :] - g_cum[:, :, None, :, :]
        decay = jnp.exp(jnp.where(incl, rel, -jnp.inf))
        scores = jnp.einsum('bhtd,bhsd,bhtsd->bhts', qb, kb, decay)
        o_intra = jnp.einsum('bhts,bhsv->bhtv', scores, ib)
        o_inter = jnp.einsum('bhtd,bhdv->bhtv', qb * jnp.exp(g_cum), state)
        g_last = g_cum[:, :, -1:, :]
        new_state = (jnp.exp(g_last[:, :, 0, :])[..., None] * state
                     + jnp.einsum('bhsd,bhsv->bhdv', kb * jnp.exp(g_last - g_cum), ib))
        return new_state, o_intra + o_inter

    state0 = jnp.zeros((b, h, dk, dv), jnp.float32)
    _, oc = lax.scan(step, state0, (qc, gc, kc, ic))
    return oc.transpose(1, 2, 0, 3, 4).reshape(b, h, s, dv).astype(i.dtype)


def stick_breaking_layer(h, norm_gain, w_in, q_gain, k_gain, w_out):
    u = rms_norm(h, norm_gain)
    proj = u @ w_in
    q, k, v, gate = jnp.split(proj, 4, axis=-1)
    q = rms_norm(split_heads(q, SB_HEADS), q_gain)
    k = rms_norm(split_heads(k, SB_HEADS), k_gain)
    v = split_heads(v, SB_HEADS)
    o = merge_heads(stick_breaking_attention(q, k, v))
    return (o * jax.nn.silu(gate)) @ w_out


def hgrn2_layer(h, norm_gain, w_in, o_gain, w_out, lower_bound):
    u = rms_norm(h, norm_gain)
    proj = u @ w_in
    q, f_pre, i, gate = jnp.split(
        proj, [HG_KEY_WIDTH, 2 * HG_KEY_WIDTH, 2 * HG_KEY_WIDTH + HG_VALUE_WIDTH], axis=-1)
    q = split_heads(jax.nn.silu(q), HG_HEADS)
    f_pre = split_heads(f_pre, HG_HEADS).astype(jnp.float32)
    lb = lower_bound.astype(jnp.float32).reshape(HG_HEADS, 1, HG_HEAD_K)
    log_f = jnp.logaddexp(jnp.log(lb), jnp.log1p(-lb) + jax.nn.log_sigmoid(f_pre))
    k_in = -jnp.expm1(log_f)
    o = hgrn2_chunkwise(q, log_f, k_in, split_heads(i, HG_HEADS))
    o = merge_heads(rms_norm(o, o_gain))
    return (o * jax.nn.silu(gate)) @ w_out


def _fwd_setup_inputs(seed: int = 0) -> dict:
    key = jax.random.key(seed)
    ks = jax.random.split(key, 12)
    n_a = (DEPTH + 1) // 2
    n_b = DEPTH // 2

    def normal(k, shape, scale):
        return jax.random.normal(k, shape, jnp.float32) * scale

    return {
        "x": normal(ks[0], (BATCH, SEQ, D_MODEL), 1.0),
        "sb_norm": 1.0 + normal(ks[1], (n_a, D_MODEL), 0.02),
        "sb_w_in": normal(ks[2], (n_a, D_MODEL, 4 * SB_WIDTH), D_MODEL ** -0.5),
        "sb_q_gain": 1.0 + normal(ks[3], (n_a, SB_HEAD_DIM), 0.02),
        "sb_k_gain": 1.0 + normal(ks[4], (n_a, SB_HEAD_DIM), 0.02),
        "sb_w_out": normal(ks[5], (n_a, SB_WIDTH, D_MODEL), SB_WIDTH ** -0.5),
        "hg_norm": 1.0 + normal(ks[6], (n_b, D_MODEL), 0.02),
        "hg_w_in": normal(ks[7], (n_b, D_MODEL, 2 * HG_KEY_WIDTH + 2 * HG_VALUE_WIDTH), D_MODEL ** -0.5),
        "hg_o_gain": 1.0 + normal(ks[8], (n_b, HG_HEAD_V), 0.02),
        "hg_w_out": normal(ks[9], (n_b, HG_VALUE_WIDTH, D_MODEL), HG_VALUE_WIDTH ** -0.5),
        "hg_lb_logits": normal(ks[10], (DEPTH, HG_KEY_WIDTH), 0.1),
    }


def _fwd_reference(x, sb_norm, sb_w_in, sb_q_gain, sb_k_gain, sb_w_out,
              hg_norm, hg_w_in, hg_o_gain, hg_w_out, hg_lb_logits):
    lb_p = jax.nn.softmax(hg_lb_logits.astype(jnp.float32), axis=0)
    lb_table = jnp.cumsum(lb_p, axis=0) - lb_p[0]
    h = x
    for layer in range(DEPTH):
        j = layer // N_MIXERS
        if layer % N_MIXERS == 0:
            h = h + stick_breaking_layer(h, sb_norm[j], sb_w_in[j], sb_q_gain[j],
                                         sb_k_gain[j], sb_w_out[j])
        else:
            h = h + hgrn2_layer(h, hg_norm[j], hg_w_in[j], hg_o_gain[j], hg_w_out[j],
                                lb_table[layer])
    return h


import jax as _jax
import jax.numpy as _jnp

TWIN_FORMAT = 'train_step'
FWD_PARAMS = ['x', 'sb_norm', 'sb_w_in', 'sb_q_gain', 'sb_k_gain', 'sb_w_out', 'hg_norm', 'hg_w_in', 'hg_o_gain', 'hg_w_out', 'hg_lb_logits']
TWIN_WEIGHTS = ['sb_norm', 'sb_w_in', 'sb_q_gain', 'sb_k_gain', 'sb_w_out', 'hg_norm', 'hg_w_in', 'hg_o_gain', 'hg_w_out', 'hg_lb_logits']
TWIN_DIFF_INPUT = 'x'
TWIN_INPUTS = ['x', 'sb_norm', 'sb_w_in', 'sb_q_gain', 'sb_k_gain', 'sb_w_out', 'hg_norm', 'hg_w_in', 'hg_o_gain', 'hg_w_out', 'hg_lb_logits', 'loss_target', 'm_sb_norm', 'm_sb_w_in', 'm_sb_q_gain', 'm_sb_k_gain', 'm_sb_w_out', 'm_hg_norm', 'm_hg_w_in', 'm_hg_o_gain', 'm_hg_w_out', 'm_hg_lb_logits', 'v_sb_norm', 'v_sb_w_in', 'v_sb_q_gain', 'v_sb_k_gain', 'v_sb_w_out', 'v_hg_norm', 'v_hg_w_in', 'v_hg_o_gain', 'v_hg_w_out', 'v_hg_lb_logits']
TWIN_OUTPUTS = ['loss', 'grad_x', 'grad_sb_norm', 'grad_sb_w_in', 'grad_sb_q_gain', 'grad_sb_k_gain', 'grad_sb_w_out', 'grad_hg_norm', 'grad_hg_w_in', 'grad_hg_o_gain', 'grad_hg_w_out', 'grad_hg_lb_logits', 'delta_sb_norm', 'delta_sb_w_in', 'delta_sb_q_gain', 'delta_sb_k_gain', 'delta_sb_w_out', 'delta_hg_norm', 'delta_hg_w_in', 'delta_hg_o_gain', 'delta_hg_w_out', 'delta_hg_lb_logits', 'new_m_sb_norm', 'new_m_sb_w_in', 'new_m_sb_q_gain', 'new_m_sb_k_gain', 'new_m_sb_w_out', 'new_m_hg_norm', 'new_m_hg_w_in', 'new_m_hg_o_gain', 'new_m_hg_w_out', 'new_m_hg_lb_logits', 'new_v_sb_norm', 'new_v_sb_w_in', 'new_v_sb_q_gain', 'new_v_sb_k_gain', 'new_v_sb_w_out', 'new_v_hg_norm', 'new_v_hg_w_in', 'new_v_hg_o_gain', 'new_v_hg_w_out', 'new_v_hg_lb_logits']
TWIN_LEAF_KINDS = {'loss': 'loss', 'grad_x': 'grad_x', 'grad_sb_norm': 'grad_w', 'grad_sb_w_in': 'grad_w', 'grad_sb_q_gain': 'grad_w', 'grad_sb_k_gain': 'grad_w', 'grad_sb_w_out': 'grad_w', 'grad_hg_norm': 'grad_w', 'grad_hg_w_in': 'grad_w', 'grad_hg_o_gain': 'grad_w', 'grad_hg_w_out': 'grad_w', 'grad_hg_lb_logits': 'grad_w', 'delta_sb_norm': 'delta_w', 'delta_sb_w_in': 'delta_w', 'delta_sb_q_gain': 'delta_w', 'delta_sb_k_gain': 'delta_w', 'delta_sb_w_out': 'delta_w', 'delta_hg_norm': 'delta_w', 'delta_hg_w_in': 'delta_w', 'delta_hg_o_gain': 'delta_w', 'delta_hg_w_out': 'delta_w', 'delta_hg_lb_logits': 'delta_w', 'new_m_sb_norm': 'new_m', 'new_m_sb_w_in': 'new_m', 'new_m_sb_q_gain': 'new_m', 'new_m_sb_k_gain': 'new_m', 'new_m_sb_w_out': 'new_m', 'new_m_hg_norm': 'new_m', 'new_m_hg_w_in': 'new_m', 'new_m_hg_o_gain': 'new_m', 'new_m_hg_w_out': 'new_m', 'new_m_hg_lb_logits': 'new_m', 'new_v_sb_norm': 'new_v', 'new_v_sb_w_in': 'new_v', 'new_v_sb_q_gain': 'new_v', 'new_v_sb_k_gain': 'new_v', 'new_v_sb_w_out': 'new_v', 'new_v_hg_norm': 'new_v', 'new_v_hg_w_in': 'new_v', 'new_v_hg_o_gain': 'new_v', 'new_v_hg_w_out': 'new_v', 'new_v_hg_lb_logits': 'new_v'}


def _forward(args):
    return _fwd_reference(*[args[k] for k in FWD_PARAMS])


def _output_shape():
    out = _jax.eval_shape(lambda: _forward(_fwd_setup_inputs(0)))
    return out.shape, out.dtype

N_MICROBATCH = 1
ADAM_LR = 0.001
ADAM_B1 = 0.9
ADAM_B2 = 0.999
ADAM_EPS = 1e-08
ADAM_WD = 0.01
ADAM_STEP = 10
PER_EXAMPLE_BATCH_AXIS = {'x': 0, 'loss_target': 0}
SHARED_INPUTS = []
_WEIGHT_DTYPES = {'sb_norm': _jnp.float32, 'sb_w_in': _jnp.float32, 'sb_q_gain': _jnp.float32, 'sb_k_gain': _jnp.float32, 'sb_w_out': _jnp.float32, 'hg_norm': _jnp.float32, 'hg_w_in': _jnp.float32, 'hg_o_gain': _jnp.float32, 'hg_w_out': _jnp.float32, 'hg_lb_logits': _jnp.float32}
MOMENT_SCALE = {'sb_norm': 1.023175e+01, 'sb_w_in': 1.627280e-01, 'sb_q_gain': 5.603305e+00, 'sb_k_gain': 5.596565e+00, 'sb_w_out': 1.878093e-01, 'hg_norm': 1.323560e+01, 'hg_w_in': 2.100663e-01, 'hg_o_gain': 9.085012e+01, 'hg_w_out': 2.841639e-01, 'hg_lb_logits': 2.186299e-02}


def _to_microbatches(a, axis):
    t = _jnp.moveaxis(a, axis, 0)
    t = t.reshape((N_MICROBATCH, t.shape[0] // N_MICROBATCH) + t.shape[1:])
    return _jnp.moveaxis(t, 1, axis + 1)


def setup_inputs(seed: int = 0) -> dict:
    inp = _fwd_setup_inputs(seed)
    key = _jax.random.fold_in(_jax.random.key(seed), 7919)
    shape, _ = _output_shape()
    out = dict(inp)
    out["loss_target"] = _jax.random.normal(_jax.random.fold_in(key, 0), shape, _jnp.float32)
    for i, name in enumerate(TWIN_WEIGHTS):
        w = inp[name].astype(_jnp.float32)
        if MOMENT_SCALE is None:
            s = _jnp.sqrt(_jnp.mean(_jnp.square(w)) + 1e-30)
        else:
            s = MOMENT_SCALE[name]
        km, kv = _jax.random.split(_jax.random.fold_in(key, i + 1))
        out[name] = w
        out["m_" + name] = s * _jax.random.normal(km, w.shape, _jnp.float32)
        out["v_" + name] = (s * s) * _jax.random.uniform(kv, w.shape, _jnp.float32, 0.5, 1.5)
    if N_MICROBATCH > 1:
        for name, axis in PER_EXAMPLE_BATCH_AXIS.items():
            out[name] = _to_microbatches(out[name], axis)
    return {'x': out['x'], 'sb_norm': out['sb_norm'], 'sb_w_in': out['sb_w_in'], 'sb_q_gain': out['sb_q_gain'], 'sb_k_gain': out['sb_k_gain'], 'sb_w_out': out['sb_w_out'], 'hg_norm': out['hg_norm'], 'hg_w_in': out['hg_w_in'], 'hg_o_gain': out['hg_o_gain'], 'hg_w_out': out['hg_w_out'], 'hg_lb_logits': out['hg_lb_logits'], 'loss_target': out['loss_target'], 'm_sb_norm': out['m_sb_norm'], 'm_sb_w_in': out['m_sb_w_in'], 'm_sb_q_gain': out['m_sb_q_gain'], 'm_sb_k_gain': out['m_sb_k_gain'], 'm_sb_w_out': out['m_sb_w_out'], 'm_hg_norm': out['m_hg_norm'], 'm_hg_w_in': out['m_hg_w_in'], 'm_hg_o_gain': out['m_hg_o_gain'], 'm_hg_w_out': out['m_hg_w_out'], 'm_hg_lb_logits': out['m_hg_lb_logits'], 'v_sb_norm': out['v_sb_norm'], 'v_sb_w_in': out['v_sb_w_in'], 'v_sb_q_gain': out['v_sb_q_gain'], 'v_sb_k_gain': out['v_sb_k_gain'], 'v_sb_w_out': out['v_sb_w_out'], 'v_hg_norm': out['v_hg_norm'], 'v_hg_w_in': out['v_hg_w_in'], 'v_hg_o_gain': out['v_hg_o_gain'], 'v_hg_w_out': out['v_hg_w_out'], 'v_hg_lb_logits': out['v_hg_lb_logits']}


def _loss(weights, diff, rest, loss_target):
    with _jax.named_scope("forward"):
        args = {**rest, TWIN_DIFF_INPUT: diff, **{k: w.astype(_WEIGHT_DTYPES[k]) for k, w in weights.items()}}
        y = _forward(args)
    with _jax.named_scope("loss_head"):
        err = _jnp.square(y.astype(_jnp.float32) - loss_target)
        return 0.5 * _jnp.sum(_jnp.mean(err, axis=-1)) if err.ndim else 0.5 * err


def _adamw(w, g, m, v):
    m = ADAM_B1 * m + (1.0 - ADAM_B1) * g
    v = ADAM_B2 * v + (1.0 - ADAM_B2) * _jnp.square(g)
    m_hat = m / (1.0 - ADAM_B1 ** ADAM_STEP)
    v_hat = v / (1.0 - ADAM_B2 ** ADAM_STEP)
    delta = -ADAM_LR * (m_hat / (_jnp.sqrt(v_hat) + ADAM_EPS) + ADAM_WD * w)
    return delta, m, v


def reference(x, sb_norm, sb_w_in, sb_q_gain, sb_k_gain, sb_w_out, hg_norm, hg_w_in, hg_o_gain, hg_w_out, hg_lb_logits, loss_target, m_sb_norm, m_sb_w_in, m_sb_q_gain, m_sb_k_gain, m_sb_w_out, m_hg_norm, m_hg_w_in, m_hg_o_gain, m_hg_w_out, m_hg_lb_logits, v_sb_norm, v_sb_w_in, v_sb_q_gain, v_sb_k_gain, v_sb_w_out, v_hg_norm, v_hg_w_in, v_hg_o_gain, v_hg_w_out, v_hg_lb_logits):
    given = dict(x=x, sb_norm=sb_norm, sb_w_in=sb_w_in, sb_q_gain=sb_q_gain, sb_k_gain=sb_k_gain, sb_w_out=sb_w_out, hg_norm=hg_norm, hg_w_in=hg_w_in, hg_o_gain=hg_o_gain, hg_w_out=hg_w_out, hg_lb_logits=hg_lb_logits, loss_target=loss_target, m_sb_norm=m_sb_norm, m_sb_w_in=m_sb_w_in, m_sb_q_gain=m_sb_q_gain, m_sb_k_gain=m_sb_k_gain, m_sb_w_out=m_sb_w_out, m_hg_norm=m_hg_norm, m_hg_w_in=m_hg_w_in, m_hg_o_gain=m_hg_o_gain, m_hg_w_out=m_hg_w_out, m_hg_lb_logits=m_hg_lb_logits, v_sb_norm=v_sb_norm, v_sb_w_in=v_sb_w_in, v_sb_q_gain=v_sb_q_gain, v_sb_k_gain=v_sb_k_gain, v_sb_w_out=v_sb_w_out, v_hg_norm=v_hg_norm, v_hg_w_in=v_hg_w_in, v_hg_o_gain=v_hg_o_gain, v_hg_w_out=v_hg_w_out, v_hg_lb_logits=v_hg_lb_logits)
    weights = {n: given[n] for n in TWIN_WEIGHTS}
    shared = {n: given[n] for n in SHARED_INPUTS}
    per_example = {n: given[n] for n in ['x']}
    grad_fn = _jax.value_and_grad(_loss, argnums=(0, 1))

    def one_microbatch(ex, loss_target):
        ex = dict(ex)
        diff = ex.pop(TWIN_DIFF_INPUT)
        return grad_fn(weights, diff, {**shared, **ex}, loss_target)

    if N_MICROBATCH == 1:
        loss, (grad_w, grad_x) = one_microbatch(per_example, given["loss_target"])
    else:
        def body(carry, xs):
            loss_sum, grad_sum = carry
            l_k, (gw_k, gx_k) = one_microbatch(xs[0], xs[1])
            with _jax.named_scope("update"):
                return (loss_sum + l_k, _jax.tree.map(_jnp.add, grad_sum, gw_k)), gx_k

        init = (_jnp.zeros((), _jnp.float32), _jax.tree.map(_jnp.zeros_like, weights))
        (loss, grad_w), grad_x = _jax.lax.scan(body, init, (per_example, given["loss_target"]))
    with _jax.named_scope("update"):
        delta_w, new_m, new_v = {}, {}, {}
        for n in TWIN_WEIGHTS:
            delta_w[n], new_m[n], new_v[n] = _adamw(weights[n], grad_w[n], given["m_" + n], given["v_" + n])
    return (loss, grad_x, *[grad_w[n] for n in TWIN_WEIGHTS], *[delta_w[n] for n in TWIN_WEIGHTS],
            *[new_m[n] for n in TWIN_WEIGHTS], *[new_v[n] for n in TWIN_WEIGHTS])
```

```python
import functools

import jax
import jax.numpy as jnp
from jax import lax
from jax.experimental import pallas as pl
from jax.experimental.pallas import tpu as pltpu

F32 = jnp.float32
BF16 = jnp.bfloat16

N_DEV = 8
D_MODEL = 1024
N_HEADS = 8
HEAD_DIM = 128
RMS_EPS = 1e-6
ATTN_BLOCK = 128
HG_CHUNK = 64
HG_SUB = 16
EXP_CLAMP = 80.0
VMEM_LIMIT_BYTES = 48 * 1024 * 1024
W_COLS = 4 * D_MODEL // N_DEV
W_ROWS = D_MODEL // N_DEV

ADAM_LR = 0.001
ADAM_B1 = 0.9
ADAM_B2 = 0.999
ADAM_EPS = 1e-08
ADAM_WD = 0.01
ADAM_STEP = 10

NT = (((1,), (1,)), ((), ()))
TN = (((0,), (0,)), ((), ()))
NN = (((1,), (0,)), ((), ()))


def _pcall(body, *, name, **kw):
    return pl.pallas_call(body, name=name, **kw)


def _params(*sem):
    return pltpu.CompilerParams(dimension_semantics=sem, vmem_limit_bytes=VMEM_LIMIT_BYTES)


def _dot(a, b, dims=NN):
    return lax.dot_general(a, b, dims, preferred_element_type=F32)


def _dot_exact(a, m, dims=NN, left=False):
    hi = a.astype(BF16)
    lo = (a - hi.astype(F32)).astype(BF16)
    if left:
        return _dot(m, hi, dims) + _dot(m, lo, dims)
    return _dot(hi, m, dims) + _dot(lo, m, dims)


def _split(a):
    hi = a.astype(BF16)
    return hi, (a - hi.astype(F32)).astype(BF16)


def _dot3(a, b, dims=NN):
    return _dot(a[0], b[0], dims) + (_dot(a[0], b[1], dims) + _dot(a[1], b[0], dims))


def _sigmoid(x):
    return 1.0 / (1.0 + jnp.exp(-x))


def _sigmoid_pair(x):
    e = jnp.exp(-jnp.abs(x))
    big = 1.0 / (1.0 + e)
    small = e * big
    pos = x >= 0
    return jnp.where(pos, big, small), jnp.where(pos, small, big)


def _rms_scale(x):
    return lax.rsqrt(jnp.mean(x * x, axis=-1, keepdims=True) + RMS_EPS)


def _row_tile(t, want):
    return want if t % want == 0 else t


def rms_inproj(x2, gain, wg, name):
    t = x2.shape[0]
    tm = _row_tile(t, 512)

    def body(x_ref, g_ref, w_ref, proj_ref, u_ref):
        @pl.when(pl.program_id(1) == 0)
        def _():
            x = x_ref[...]
            u_ref[...] = (x * _rms_scale(x) * g_ref[...]).astype(BF16)

        proj_ref[0] = _dot(u_ref[...], w_ref[0])

    return _pcall(
        body, name=name,
        grid=(t // tm, N_DEV),
        in_specs=[pl.BlockSpec((tm, D_MODEL), lambda i, j: (i, 0)),
                  pl.BlockSpec((1, D_MODEL), lambda i, j: (0, 0)),
                  pl.BlockSpec((1, D_MODEL, W_COLS), lambda i, j: (j, 0, 0))],
        out_specs=[pl.BlockSpec((1, tm, W_COLS), lambda i, j: (j // 2, i, j % 2)),
                   pl.BlockSpec((tm, D_MODEL), lambda i, j: (i, 0))],
        out_shape=[jax.ShapeDtypeStruct((4, t, D_MODEL), F32),
                   jax.ShapeDtypeStruct((t, D_MODEL), BF16)],
        compiler_params=_params("parallel", "arbitrary"),
    )(x2, gain, wg)


def gate_outproj(o2, proj, w_out, resid, target, name):
    t = o2.shape[0]
    tm = _row_tile(t, 256)
    with_loss = target is not None

    def body(o_ref, gate_ref, w_ref, r_ref, *rest):
        g = gate_ref[0]
        og = (o_ref[...] * (g * _sigmoid(g))).astype(BF16)
        h = r_ref[...] + _dot(og, w_ref[...])
        if with_loss:
            t_ref, dh_ref, loss_ref = rest
            err = h - t_ref[...]
            dh_ref[...] = err * (1.0 / D_MODEL)
            part = 0.5 * jnp.sum(jnp.mean(err * err, axis=-1, keepdims=True))
            loss_ref[...] = jnp.full(loss_ref.shape, part, F32)
        else:
            (h_ref,) = rest
            h_ref[...] = h

    row = pl.BlockSpec((tm, D_MODEL), lambda i: (i, 0))
    in_specs = [row,
                pl.BlockSpec((1, tm, D_MODEL), lambda i: (3, i, 0)),
                pl.BlockSpec((D_MODEL, D_MODEL), lambda i: (0, 0)),
                row]
    args = [o2, proj, w_out, resid]
    if with_loss:
        in_specs.append(row)
        args.append(target)
        out_specs = [row, pl.BlockSpec((1, 8, 128), lambda i: (i, 0, 0))]
        out_shape = [jax.ShapeDtypeStruct((t, D_MODEL), F32),
                     jax.ShapeDtypeStruct((t // tm, 8, 128), F32)]
    else:
        out_specs = row
        out_shape = jax.ShapeDtypeStruct((t, D_MODEL), F32)
    return _pcall(body, name=name, grid=(t // tm,), in_specs=in_specs, out_specs=out_specs,
                  out_shape=out_shape, compiler_params=_params("parallel"))(*args)


def _head_spec(s, part):
    return pl.BlockSpec((1, 1, s, HEAD_DIM), lambda b, h: (part, b, 0, h))


def _seq_spec(s):
    return pl.BlockSpec((1, s, HEAD_DIM), lambda b, h: (b, 0, h))


_GAIN_SPEC = pl.BlockSpec((1, HEAD_DIM), lambda b, h: (0, 0))
_HEAD_ROW_SPEC = pl.BlockSpec((1, 1, 1, HEAD_DIM), lambda b, h: (b, h, 0, 0))


def _sb_scores(qi, kj, diag, tri_lt):
    z = _dot(qi, kj, NT) * (HEAD_DIM ** -0.5)
    soft = jnp.log1p(jnp.exp(-jnp.abs(z)))
    valid = jnp.logical_or(jnp.logical_not(diag), tri_lt)
    log_skip = jnp.where(valid, -(jnp.maximum(z, 0.0) + soft), 0.0)
    log_beta = jnp.minimum(z, 0.0) - soft
    return log_skip, log_beta, valid


def sb_attn_fwd(proj4, q_gain, k_gain):
    _, b, s, _ = proj4.shape
    nq = s // ATTN_BLOCK
    blk = ATTN_BLOCK

    def body(q_ref, k_ref, v_ref, qg_ref, kg_ref, o_ref, qb, kb, vb):
        q = q_ref[0, 0]
        qb[...] = (q * _rms_scale(q) * qg_ref[...]).astype(BF16)
        k = k_ref[0, 0]
        kb[...] = (k * _rms_scale(k) * kg_ref[...]).astype(BF16)
        vb[...] = v_ref[0, 0].astype(BF16)
        row = lax.broadcasted_iota(jnp.int32, (blk, blk), 0)
        col = lax.broadcasted_iota(jnp.int32, (blk, blk), 1)
        tri_lt = col < row
        suffix = (row > col).astype(BF16)

        def q_block(i, _):
            qi = qb[pl.ds(pl.multiple_of(i * blk, blk), blk), :]

            def k_block(jj, carry):
                c, acc = carry
                j = i - jj
                rows = pl.ds(pl.multiple_of(j * blk, blk), blk)
                log_skip, log_beta, valid = _sb_scores(qi, kb[rows, :], jj == 0, tri_lt)
                after = _dot_exact(log_skip, suffix)
                w = jnp.where(valid, jnp.exp(log_beta + after + c), 0.0)
                acc = acc + _dot(w.astype(BF16), vb[rows, :])
                c = c + jnp.sum(log_skip, axis=1, keepdims=True)
                return c, acc

            _, acc = lax.fori_loop(0, i + 1, k_block,
                                   (jnp.zeros((blk, 1), F32), jnp.zeros((blk, HEAD_DIM), F32)))
            o_ref[0, pl.ds(pl.multiple_of(i * blk, blk), blk), :] = acc
            return 0

        lax.fori_loop(0, nq, q_block, 0)

    return _pcall(
        body, name="sb_attn_fwd", grid=(b, N_HEADS),
        in_specs=[_head_spec(s, 0), _head_spec(s, 1), _head_spec(s, 2), _GAIN_SPEC, _GAIN_SPEC],
        out_specs=_seq_spec(s),
        out_shape=jax.ShapeDtypeStruct((b, s, D_MODEL), F32),
        scratch_shapes=[pltpu.VMEM((s, HEAD_DIM), BF16)] * 3,
        compiler_params=_params("parallel", "parallel"),
    )(proj4, proj4, proj4, q_gain, k_gain)


def _hg_masks():
    c = HG_CHUNK
    row = lax.broadcasted_iota(jnp.int32, (c, c), 0)
    col = lax.broadcasted_iota(jnp.int32, (c, c), 1)
    incl = (col <= row)
    lower = incl.astype(BF16)
    before_sub = (col < (row // HG_SUB) * HG_SUB).astype(BF16)
    upper = (col >= row).astype(BF16)
    return incl, lower, before_sub, upper


def _hg_lower_bound(lbl_ref):
    l0 = lbl_ref[0, 0]
    l1 = lbl_ref[1, 0]
    d = l1 - l0
    return _sigmoid_pair(d)


def _hg_chunk_fwd(qp, fp, lb, oml, lower, before_sub):
    sq = _sigmoid(qp)
    q = qp * sq
    sf, sfn = _sigmoid_pair(fp)
    f = lb + oml * sf
    k = oml * sfn
    logf = jnp.log(f)
    gc = _dot_exact(logf, lower, left=True)
    gr = _dot_exact(logf, before_sub, left=True)
    return dict(sq=sq, q=q, sf=sf, sfn=sfn, f=f, k=k, gc=gc, gr=gr)


def _hg_intra(qd, k, gc, gr, incl):
    ks, es, rows = [], [], []
    for sub in range(HG_CHUNK // HG_SUB):
        lo = sub * HG_SUB
        e = jnp.exp(jnp.minimum(gr[lo:lo + 1, :] - gc, EXP_CLAMP))
        es.append(e)
        ks.append(k * e)
        rows.append(_dot(qd[lo:lo + HG_SUB, :].astype(BF16), ks[-1].astype(BF16), NT))
    a = jnp.where(incl, jnp.concatenate(rows, axis=0), 0.0)
    return a, ks, es


def _hg_intra3(qd, k, gc, gr, incl):
    qs = _split(qd)
    ks, es, rows = [], [], []
    for sub in range(HG_CHUNK // HG_SUB):
        lo = sub * HG_SUB
        e = jnp.exp(jnp.minimum(gr[lo:lo + 1, :] - gc, EXP_CLAMP))
        es.append(e)
        ks.append(_split(k * e))
        rows.append(_dot3((qs[0][lo:lo + HG_SUB, :], qs[1][lo:lo + HG_SUB, :]), ks[-1], NT))
    a = jnp.where(incl, jnp.concatenate(rows, axis=0), 0.0)
    return a, qs, ks, es


def hgrn2_fwd(proj4, lbl4, o_gain):
    _, b, s, _ = proj4.shape
    nchunk = s // HG_CHUNK
    c = HG_CHUNK

    def body(q_ref, f_ref, i_ref, lbl_ref, og_ref, o_ref, st_ref):
        incl, lower, before_sub, _ = _hg_masks()
        lb, oml = _hg_lower_bound(lbl_ref)
        st_ref[...] = jnp.zeros_like(st_ref)

        def chunk(n, _):
            rows = pl.ds(pl.multiple_of(n * c, c), c)
            v = _hg_chunk_fwd(q_ref[0, 0, rows, :], f_ref[0, 0, rows, :], lb, oml, lower, before_sub)
            inp = i_ref[0, 0, rows, :].astype(BF16)
            gc = v["gc"]
            qd = v["q"] * jnp.exp(gc - v["gr"])
            a, _, _ = _hg_intra(qd, v["k"], gc, v["gr"], incl)
            st = st_ref[...]
            o = _dot(a.astype(BF16), inp) + _dot((v["q"] * jnp.exp(gc)).astype(BF16), st.astype(BF16), NT)
            gl = gc[c - 1:c, :]
            kk = v["k"] * jnp.exp(gl - gc)
            st_ref[...] = st * jnp.exp(gl) + _dot(inp, kk.astype(BF16), TN)
            o_ref[0, rows, :] = o * _rms_scale(o) * og_ref[...]
            return 0

        lax.fori_loop(0, nchunk, chunk, 0)

    return _pcall(
        body, name="hgrn2_fwd", grid=(b, N_HEADS),
        in_specs=[_head_spec(s, 0), _head_spec(s, 1), _head_spec(s, 2),
                  pl.BlockSpec((2, 1, 1, HEAD_DIM), lambda b, h: (0, h, 0, 0)), _GAIN_SPEC],
        out_specs=_seq_spec(s),
        out_shape=jax.ShapeDtypeStruct((b, s, D_MODEL), F32),
        scratch_shapes=[pltpu.VMEM((HEAD_DIM, HEAD_DIM), F32)],
        compiler_params=_params("parallel", "parallel"),
    )(proj4, proj4, proj4, lbl4, o_gain)


def outproj_bwd(dh, w_out, o2, proj, name):
    t = dh.shape[0]
    tm = _row_tile(t, 256)

    def body(dh_ref, w_ref, o_ref, gate_ref, do_ref, dproj_ref, dw_ref):
        dhb = dh_ref[...].astype(BF16)
        dog = _dot(dhb, w_ref[...], NT)
        g = gate_ref[0]
        sg = _sigmoid(g)
        silu = g * sg
        o = o_ref[...]
        do_ref[...] = dog * silu
        dproj_ref[0] = dog * o * (sg * (1.0 + g * (1.0 - sg)))
        part = _dot((o * silu).astype(BF16), dhb, TN)

        @pl.when(pl.program_id(0) == 0)
        def _():
            dw_ref[...] = part

        @pl.when(pl.program_id(0) > 0)
        def _():
            dw_ref[...] += part

    row = pl.BlockSpec((tm, D_MODEL), lambda i: (i, 0))
    full = pl.BlockSpec((D_MODEL, D_MODEL), lambda i: (0, 0))
    return _pcall(
        body, name=name, grid=(t // tm,),
        in_specs=[row, full, row, pl.BlockSpec((1, tm, D_MODEL), lambda i: (3, i, 0))],
        out_specs=[row, pl.BlockSpec((1, tm, D_MODEL), lambda i: (3, i, 0)), full],
        out_shape=[jax.ShapeDtypeStruct((t, D_MODEL), F32),
                   jax.ShapeDtypeStruct((4, t, D_MODEL), F32),
                   jax.ShapeDtypeStruct((D_MODEL, D_MODEL), F32)],
        compiler_params=_params("arbitrary"),
    )(dh, w_out, o2, proj)


def inproj_bwd_dx(dproj, wg, x2, gain, dres, name):
    t = x2.shape[0]
    tm = _row_tile(t, 256)

    def body(d_ref, w_ref, x_ref, g_ref, r_ref, dx_ref, dg_ref):
        du = jnp.zeros((tm, D_MODEL), F32)
        for p in range(N_DEV):
            cols = slice((p % 2) * W_COLS, (p % 2 + 1) * W_COLS)
            du = du + _dot(d_ref[p // 2, :, cols].astype(BF16), w_ref[p], NT)
        x = x_ref[...]
        r = _rms_scale(x)
        xh = x * r
        a = du * g_ref[...]
        dx_ref[...] = r_ref[...] + r * (a - xh * jnp.mean(a * xh, axis=-1, keepdims=True))
        part = jnp.sum(du * xh, axis=0, keepdims=True)

        @pl.when(pl.program_id(0) == 0)
        def _():
            dg_ref[...] = part

        @pl.when(pl.program_id(0) > 0)
        def _():
            dg_ref[...] += part

    row = pl.BlockSpec((tm, D_MODEL), lambda i: (i, 0))
    vec = pl.BlockSpec((1, D_MODEL), lambda i: (0, 0))
    return _pcall(
        body, name=name, grid=(t // tm,),
        in_specs=[pl.BlockSpec((4, tm, D_MODEL), lambda i: (0, i, 0)),
                  pl.BlockSpec((N_DEV, D_MODEL, W_COLS), lambda i: (0, 0, 0)),
                  row, vec, row],
        out_specs=[row, vec],
        out_shape=[jax.ShapeDtypeStruct((t, D_MODEL), F32), jax.ShapeDtypeStruct((1, D_MODEL), F32)],
        compiler_params=_params("arbitrary"),
    )(dproj, wg, x2, gain, dres)


def inproj_bwd_dw(u, dproj, name):
    t = u.shape[0]
    tm = _row_tile(t, 512)

    def body(u_ref, d_ref, dw_ref):
        part = _dot(u_ref[...], d_ref[0].astype(BF16), TN)

        @pl.when(pl.program_id(1) == 0)
        def _():
            dw_ref[0] = part

        @pl.when(pl.program_id(1) > 0)
        def _():
            dw_ref[0] += part

    return _pcall(
        body, name=name, grid=(N_DEV, t // tm),
        in_specs=[pl.BlockSpec((tm, D_MODEL), lambda j, i: (i, 0)),
                  pl.BlockSpec((1, tm, W_COLS), lambda j, i: (j // 2, i, j % 2))],
        out_specs=pl.BlockSpec((1, D_MODEL, W_COLS), lambda j, i: (j, 0, 0)),
        out_shape=jax.ShapeDtypeStruct((N_DEV, D_MODEL, W_COLS), F32),
        compiler_params=_params("parallel", "arbitrary"),
    )(u, dproj)


def _rms_bwd(x, gain, dy):
    r = _rms_scale(x)
    xh = x * r
    a = dy * gain
    return r * (a - xh * jnp.mean(a * xh, axis=-1, keepdims=True)), dy * xh


def sb_attn_bwd(proj4, do3, o3, q_gain, k_gain, dproj4):
    _, b, s, _ = proj4.shape
    nq = s // ATTN_BLOCK
    blk = ATTN_BLOCK

    def body(q_ref, k_ref, v_ref, do_ref, o_ref, qg_ref, kg_ref, _alias, d_ref, dqg_ref, dkg_ref, qb, kb, vb, dob):
        q = q_ref[0, 0]
        qb[...] = (q * _rms_scale(q) * qg_ref[...]).astype(BF16)
        k = k_ref[0, 0]
        kb[...] = (k * _rms_scale(k) * kg_ref[...]).astype(BF16)
        vb[...] = v_ref[0, 0].astype(BF16)
        dob[...] = do_ref[0].astype(BF16)
        d_ref[...] = jnp.zeros_like(d_ref)
        row = lax.broadcasted_iota(jnp.int32, (blk, blk), 0)
        col = lax.broadcasted_iota(jnp.int32, (blk, blk), 1)
        tri_lt = col < row
        suffix = (row > col).astype(BF16)
        suffix_incl = (row >= col).astype(BF16)

        def q_block(i, _):
            rows_i = pl.ds(pl.multiple_of(i * blk, blk), blk)
            qi = qb[rows_i, :]
            doi = dob[rows_i, :]
            delta = jnp.sum(doi.astype(F32) * o_ref[0, rows_i, :], axis=1, keepdims=True)

            def k_block(jj, carry):
                c, cg, dq = carry
                j = i - jj
                rows = pl.ds(pl.multiple_of(j * blk, blk), blk)
                kj = kb[rows, :]
                vj = vb[rows, :]
                log_skip, log_beta, valid = _sb_scores(qi, kj, jj == 0, tri_lt)
                after = _dot_exact(log_skip, suffix)
                w = jnp.where(valid, jnp.exp(log_beta + after + c), 0.0)
                beta = jnp.exp(log_beta)
                wb = w.astype(BF16)
                g = _dot(doi, vj, NT) * wb.astype(F32)
                d_ref[2, 0, rows, :] += _dot(wb, doi, TN)
                before = delta - (_dot_exact(g, suffix_incl) + cg)
                dz = jnp.where(valid, g * (1.0 - beta) - before * beta, 0.0) * (HEAD_DIM ** -0.5)
                dzb = dz.astype(BF16)
                dq = dq + _dot(dzb, kj)
                d_ref[1, 0, rows, :] += _dot(dzb, qi, TN)
                c = c + jnp.sum(log_skip, axis=1, keepdims=True)
                cg = cg + jnp.sum(g, axis=1, keepdims=True)
                return c, cg, dq

            zero = jnp.zeros((blk, 1), F32)
            _, _, dq = lax.fori_loop(0, i + 1, k_block, (zero, zero, jnp.zeros((blk, HEAD_DIM), F32)))
            d_ref[0, 0, rows_i, :] = dq
            return 0

        lax.fori_loop(0, nq, q_block, 0)

        def norm_block(i, carry):
            gq, gk = carry
            rows = pl.ds(pl.multiple_of(i * blk, blk), blk)
            dq, pq = _rms_bwd(q_ref[0, 0, rows, :], qg_ref[...], d_ref[0, 0, rows, :])
            d_ref[0, 0, rows, :] = dq
            dk, pk = _rms_bwd(k_ref[0, 0, rows, :], kg_ref[...], d_ref[1, 0, rows, :])
            d_ref[1, 0, rows, :] = dk
            return gq + jnp.sum(pq, axis=0, keepdims=True), gk + jnp.sum(pk, axis=0, keepdims=True)

        zero = jnp.zeros((1, HEAD_DIM), F32)
        gq, gk = lax.fori_loop(0, nq, norm_block, (zero, zero))
        dqg_ref[0, 0] = gq
        dkg_ref[0, 0] = gk

    head_row = jax.ShapeDtypeStruct((b, N_HEADS, 1, HEAD_DIM), F32)
    return _pcall(
        body, name="sb_attn_bwd", grid=(b, N_HEADS),
        in_specs=[_head_spec(s, 0), _head_spec(s, 1), _head_spec(s, 2), _seq_spec(s), _seq_spec(s),
                  _GAIN_SPEC, _GAIN_SPEC, pl.BlockSpec(memory_space=pl.ANY)],
        out_specs=[pl.BlockSpec((3, 1, s, HEAD_DIM), lambda b, h: (0, b, 0, h)), _HEAD_ROW_SPEC, _HEAD_ROW_SPEC],
        out_shape=[jax.ShapeDtypeStruct(dproj4.shape, F32), head_row, head_row],
        scratch_shapes=[pltpu.VMEM((s, HEAD_DIM), BF16)] * 4,
        input_output_aliases={7: 0},
        compiler_params=_params("parallel", "parallel"),
    )(proj4, proj4, proj4, do3, o3, q_gain, k_gain, dproj4)


def hgrn2_bwd(proj4, don3, lbl4, o_gain, dproj4):
    _, b, s, _ = proj4.shape
    nchunk = s // HG_CHUNK
    c = HG_CHUNK
    nsub = HG_CHUNK // HG_SUB

    def body(q_ref, f_ref, i_ref, don_ref, lbl_ref, og_ref, _alias, d_ref, dog_ref, dlb_ref, st_all, o_all, dst_ref):
        incl, lower, before_sub, upper = _hg_masks()
        lb, oml = _hg_lower_bound(lbl_ref)
        last_row = lax.broadcasted_iota(jnp.int32, (c, HEAD_DIM), 0) == c - 1
        st_all[0] = jnp.zeros((HEAD_DIM, HEAD_DIM), F32)

        def fwd_chunk(n, _):
            rows = pl.ds(pl.multiple_of(n * c, c), c)
            v = _hg_chunk_fwd(q_ref[0, 0, rows, :], f_ref[0, 0, rows, :], lb, oml, lower, before_sub)
            inp = _split(i_ref[0, 0, rows, :])
            gc = v["gc"]
            qd = v["q"] * jnp.exp(gc - v["gr"])
            a, _, _, _ = _hg_intra3(qd, v["k"], gc, v["gr"], incl)
            st = st_all[n]
            o_all[rows, :] = (_dot3(_split(a), inp)
                              + _dot3(_split(v["q"] * jnp.exp(gc)), _split(st), NT))
            gl = gc[c - 1:c, :]
            kk = v["k"] * jnp.exp(gl - gc)

            @pl.when(n + 1 < nchunk)
            def _():
                st_all[n + 1] = st * jnp.exp(gl) + _dot3(inp, _split(kk), TN)

            return 0

        lax.fori_loop(0, nchunk, fwd_chunk, 0)
        dst_ref[...] = jnp.zeros_like(dst_ref)

        def bwd_chunk(m, carry):
            dog_acc, dlb_acc = carry
            n = nchunk - 1 - m
            rows = pl.ds(pl.multiple_of(n * c, c), c)
            qp = q_ref[0, 0, rows, :]
            v = _hg_chunk_fwd(qp, f_ref[0, 0, rows, :], lb, oml, lower, before_sub)
            inp = _split(i_ref[0, 0, rows, :])
            q, k, gc = v["q"], v["k"], v["gc"]
            e_q = jnp.exp(gc - v["gr"])
            a, qds, ks, es = _hg_intra3(q * e_q, k, gc, v["gr"], incl)
            e_gc = jnp.exp(gc)
            gl = gc[c - 1:c, :]
            e_gl = jnp.exp(gl)
            e_k = jnp.exp(gl - gc)
            st = st_all[n]
            dst = dst_ref[...]
            dsts = _split(dst)
            don = don_ref[0, rows, :]
            do, pg = _rms_bwd(o_all[rows, :], og_ref[...], don)
            dog_acc = dog_acc + jnp.sum(pg, axis=0, keepdims=True)
            dos = _split(do)
            das = _split(jnp.where(incl, _dot3(dos, inp, NT), 0.0))
            di = _dot3(_split(a), dos, TN) + _dot3(_split(k * e_k), dsts, NT)
            dqd, dk_intra = [], jnp.zeros((c, HEAD_DIM), F32)
            for sub in range(nsub):
                sl = slice(sub * HG_SUB, (sub + 1) * HG_SUB)
                da_sub = (das[0][sl, :], das[1][sl, :])
                dqd.append(_dot3(da_sub, ks[sub]))
                dk_intra = dk_intra + _dot3(da_sub, (qds[0][sl, :], qds[1][sl, :]), TN) * es[sub]
            dq = jnp.concatenate(dqd, axis=0) * e_q + _dot3(dos, _split(st)) * e_gc
            dk_inter = _dot3(inp, dsts) * e_k
            dk = dk_intra + dk_inter
            at_last = (jnp.sum(k * dk_inter, axis=0, keepdims=True)
                       + e_gl * jnp.sum(st * dst, axis=0, keepdims=True))
            dgc = q * dq - k * dk + jnp.where(last_row, at_last, 0.0)
            dlogf = _dot_exact(dgc, upper, left=True)
            dst_ref[...] = dst * e_gl + _dot3(dos, _split(q * e_gc), TN)
            sq, sf, sfn, f = v["sq"], v["sf"], v["sfn"], v["f"]
            dlf_f = dlogf / f
            d_ref[0, 0, rows, :] = dq * (sq * (1.0 + qp * (1.0 - sq)))
            d_ref[1, 0, rows, :] = (dlf_f - dk) * (oml * sf * sfn)
            d_ref[2, 0, rows, :] = di
            dlb_acc = dlb_acc + jnp.sum((dlf_f - dk) * sfn, axis=0, keepdims=True)
            return dog_acc, dlb_acc

        zero = jnp.zeros((1, HEAD_DIM), F32)
        dog, dlb = lax.fori_loop(0, nchunk, bwd_chunk, (zero, zero))
        dog_ref[0, 0] = dog
        dlb_ref[0, 0] = dlb

    head_row = jax.ShapeDtypeStruct((b, N_HEADS, 1, HEAD_DIM), F32)
    return _pcall(
        body, name="hgrn2_bwd", grid=(b, N_HEADS),
        in_specs=[_head_spec(s, 0), _head_spec(s, 1), _head_spec(s, 2), _seq_spec(s),
                  pl.BlockSpec((2, 1, 1, HEAD_DIM), lambda b, h: (0, h, 0, 0)), _GAIN_SPEC,
                  pl.BlockSpec(memory_space=pl.ANY)],
        out_specs=[pl.BlockSpec((3, 1, s, HEAD_DIM), lambda b, h: (0, b, 0, h)), _HEAD_ROW_SPEC, _HEAD_ROW_SPEC],
        out_shape=[jax.ShapeDtypeStruct(dproj4.shape, F32), head_row, head_row],
        scratch_shapes=[pltpu.VMEM((nchunk, HEAD_DIM, HEAD_DIM), F32), pltpu.VMEM((s, HEAD_DIM), F32),
                        pltpu.VMEM((HEAD_DIM, HEAD_DIM), F32)],
        input_output_aliases={6: 0},
        compiler_params=_params("parallel", "parallel"),
    )(proj4, proj4, proj4, don3, lbl4, o_gain, dproj4)


def local_step(x, target, sb_norm, wsi, sb_q_gain, sb_k_gain, wso, hg_norm_full, whi, hg_o_gain, who, hg_lb_logits):
    b, s, _ = x.shape
    t = b * s
    x2 = x.reshape(t, D_MODEL)
    tg2 = target.reshape(t, D_MODEL)
    lbl4 = hg_lb_logits.reshape(2, N_HEADS, 1, HEAD_DIM)
    four = (4, b, s, D_MODEL)
    three = (b, s, D_MODEL)

    proj0, u0 = rms_inproj(x2, sb_norm, wsi, "sb_inproj")
    o0 = sb_attn_fwd(proj0.reshape(four), sb_q_gain, sb_k_gain).reshape(t, D_MODEL)
    h1 = gate_outproj(o0, proj0, wso, x2, None, "sb_outproj")
    proj1, u1 = rms_inproj(h1, hg_norm_full, whi, "hg_inproj")
    o1 = hgrn2_fwd(proj1.reshape(four), lbl4, hg_o_gain).reshape(t, D_MODEL)
    dh2, loss_parts = gate_outproj(o1, proj1, who, h1, tg2, "hg_outproj_loss")

    do1, dproj1, g_who = outproj_bwd(dh2, who, o1, proj1, "hg_outproj_bwd")
    dproj1, g_og, g_lb = hgrn2_bwd(proj1.reshape(four), do1.reshape(three), lbl4, hg_o_gain, dproj1.reshape(four))
    dproj1 = dproj1.reshape(4, t, D_MODEL)
    dh1, g_hgn = inproj_bwd_dx(dproj1, whi, h1, hg_norm_full, dh2, "hg_inproj_bwd_dx")
    g_whi = inproj_bwd_dw(u1, dproj1, "hg_inproj_bwd_dw")

    do0, dproj0, g_wso = outproj_bwd(dh1, wso, o0, proj0, "sb_outproj_bwd")
    dproj0, g_qg, g_kg = sb_attn_bwd(proj0.reshape(four), do0.reshape(three), o0.reshape(three),
                                     sb_q_gain, sb_k_gain, dproj0.reshape(four))
    dproj0 = dproj0.reshape(4, t, D_MODEL)
    gx, g_sbn = inproj_bwd_dx(dproj0, wsi, x2, sb_norm, dh1, "sb_inproj_bwd_dx")
    g_wsi = inproj_bwd_dw(u0, dproj0, "sb_inproj_bwd_dw")
    return dict(loss_parts=loss_parts, gx=gx.reshape(three), g_wsi=g_wsi, g_wso=g_wso, g_whi=g_whi, g_who=g_who,
                g_sbn=g_sbn, g_hgn=g_hgn, g_qg=g_qg, g_kg=g_kg, g_og=g_og, g_lb=g_lb)


MESH = pl.DeviceIdType.MESH
N_PEERS = N_DEV - 1
_ANY = pl.BlockSpec(memory_space=pl.ANY)
_VMEM = pl.BlockSpec(memory_space=pltpu.VMEM)


def _mesh_pos():
    return lax.axis_index("x"), lax.axis_index("y"), lax.axis_index("c")


def _linear(pos):
    return 4 * pos[0] + 2 * pos[1] + pos[2]


def _peer(pos, k):
    flips = ((k + 1) >> 2 & 1, (k + 1) >> 1 & 1, (k + 1) & 1)
    return tuple(1 - p if f else p for p, f in zip(pos, flips))


def _exchange(pairs, send_sems, recv_sems, local_sems, pos):
    me = _linear(pos)
    started = []
    for a, (src_of, dst) in enumerate(pairs):
        loc = pltpu.make_async_copy(src_of(me), dst.at[me], local_sems.at[a])
        loc.start()
        started.append(loc)
        for k in range(N_PEERS):
            peer = _peer(pos, k)
            pltpu.make_async_remote_copy(
                src_ref=src_of(_linear(peer)), dst_ref=dst.at[me], send_sem=send_sems.at[a, k],
                recv_sem=recv_sems.at[a, k], device_id=peer, device_id_type=MESH).start()
    for a, (src_of, dst) in enumerate(pairs):
        for k in range(N_PEERS):
            peer = _peer(pos, k)
            landed = pltpu.make_async_remote_copy(
                src_ref=src_of(_linear(peer)), dst_ref=dst.at[_linear(peer)], send_sem=send_sems.at[a, k],
                recv_sem=recv_sems.at[a, k], device_id=peer, device_id_type=MESH)
            landed.wait_recv()
            landed.wait_send()
    for loc in started:
        loc.wait()


def _exchange_sems(n):
    return [pltpu.SemaphoreType.DMA((n, N_PEERS)), pltpu.SemaphoreType.DMA((n, N_PEERS)),
            pltpu.SemaphoreType.DMA((n,))]


def all_gather_weights(w_si, w_so, w_hi, w_ho, hg_norm):
    shards = [w_si, w_so, w_hi, w_ho]

    def body(si_ref, so_ref, hi_ref, ho_ref, hn_ref, o_si, o_so, o_hi, o_ho, o_hn,
             b_si, b_so, b_hi, b_ho, b_hn, send_sems, recv_sems, local_sems):
        for src, buf in ((si_ref, b_si), (so_ref, b_so), (hi_ref, b_hi), (ho_ref, b_ho)):
            buf[...] = src[...].astype(BF16)
        b_hn[...] = jnp.broadcast_to(hn_ref[...], b_hn.shape)
        pairs = [((lambda p, buf=buf: buf), out) for buf, out in
                 ((b_si, o_si), (b_so, o_so), (b_hi, o_hi), (b_ho, o_ho), (b_hn, o_hn))]
        _exchange(pairs, send_sems, recv_sems, local_sems, _mesh_pos())

    out_shape = [jax.ShapeDtypeStruct((N_DEV,) + w.shape, BF16) for w in shards]
    out_shape.append(jax.ShapeDtypeStruct((N_DEV, 8, HEAD_DIM), F32))
    scratch = [pltpu.VMEM(w.shape, BF16) for w in shards] + [pltpu.VMEM((8, HEAD_DIM), F32)]
    return _pcall(
        body, name="all_gather_weights",
        in_specs=[_VMEM] * 5, out_specs=[_ANY] * 5, out_shape=out_shape,
        scratch_shapes=scratch + _exchange_sems(5),
        compiler_params=pltpu.CompilerParams(vmem_limit_bytes=VMEM_LIMIT_BYTES),
    )(w_si, w_so, w_hi, w_ho, hg_norm)


def exchange_grads(grads):
    def body(*refs):
        n = len(grads)
        ins, outs = refs[:n], refs[n:2 * n]
        send_sems, recv_sems, local_sems = refs[2 * n:]
        pairs = [((lambda p, g=g: g.at[p]), r) for g, r in zip(ins, outs)]
        _exchange(pairs, send_sems, recv_sems, local_sems, _mesh_pos())

    return _pcall(
        body, name="exchange_grads",
        in_specs=[_ANY] * len(grads), out_specs=[_ANY] * len(grads),
        out_shape=[jax.ShapeDtypeStruct(g.shape, F32) for g in grads],
        scratch_shapes=_exchange_sems(len(grads)),
    )(*grads)


def _adamw(w, g, m, v):
    m = ADAM_B1 * m + (1.0 - ADAM_B1) * g
    v = ADAM_B2 * v + (1.0 - ADAM_B2) * (g * g)
    m_hat = m / (1.0 - ADAM_B1 ** ADAM_STEP)
    v_hat = v / (1.0 - ADAM_B2 ** ADAM_STEP)
    delta = -ADAM_LR * (m_hat / (jnp.sqrt(v_hat) + ADAM_EPS) + ADAM_WD * w)
    return delta, m, v


def reduce_adamw(parts, w, m, v, name):
    _, r, c = parts.shape
    tr = _row_tile(r, 256)

    def body(p_ref, w_ref, m_ref, v_ref, g_ref, d_ref, m2_ref, v2_ref):
        g = p_ref[0]
        for dev in range(1, N_DEV):
            g = g + p_ref[dev]
        g_ref[...] = g
        d_ref[...], m2_ref[...], v2_ref[...] = _adamw(w_ref[...], g, m_ref[...], v_ref[...])

    tile = pl.BlockSpec((tr, c), lambda i: (i, 0))
    return _pcall(
        body, name=name, grid=(r // tr,),
        in_specs=[pl.BlockSpec((N_DEV, tr, c), lambda i: (0, i, 0)), tile, tile, tile],
        out_specs=[tile] * 4, out_shape=[jax.ShapeDtypeStruct((r, c), F32)] * 4,
        compiler_params=_params("parallel"),
    )(parts, w, m, v)


PACK_ROWS = 32
ROW_SBN, ROW_HGN, ROW_LB, ROW_QG, ROW_KG, ROW_OG, ROW_LOSS = 0, 8, 16, 24, 25, 26, 27


def small_update(g_sbn, g_hgn, g_lb, g_qg, g_kg, g_og, loss_parts, small):
    n_in = 7 + len(small)

    def body(*refs):
        sbn_ref, hgn_ref, lb_ref, qg_ref, kg_ref, og_ref, loss_ref = refs[:7]
        wmv = refs[7:n_in]
        outs = refs[n_in:n_in + 25]
        pack, gath, tot, send_sems, recv_sems, local_sems = refs[n_in + 25:]
        pos = _mesh_pos()
        me = _linear(pos)
        pack[...] = jnp.zeros_like(pack)
        pack[ROW_SBN:ROW_SBN + 8, :] = sbn_ref[...]
        pack[ROW_HGN:ROW_HGN + 8, :] = hgn_ref[...]
        pack[ROW_LB:ROW_LB + 8, :] = jnp.sum(lb_ref[...], axis=0)
        pack[ROW_QG:ROW_QG + 1, :] = jnp.sum(qg_ref[...], axis=0, keepdims=True)
        pack[ROW_KG:ROW_KG + 1, :] = jnp.sum(kg_ref[...], axis=0, keepdims=True)
        pack[ROW_OG:ROW_OG + 1, :] = jnp.sum(og_ref[...], axis=0, keepdims=True)
        pack[ROW_LOSS:ROW_LOSS + 1, :] = jnp.sum(loss_ref[...], axis=0)[0:1, :]
        _exchange([((lambda p: pack), gath)], send_sems, recv_sems, local_sems, pos)
        total = gath[0]
        for dev in range(1, N_DEV):
            total = total + gath[dev]
        tot[...] = total
        outs[0][...] = jnp.broadcast_to(tot[ROW_LOSS:ROW_LOSS + 1, :], (8, HEAD_DIM))
        l0 = wmv[15][0:8, :]
        l1 = wmv[15][8:16, :]
        p1, p0 = _sigmoid_pair(l1 - l0)
        d_l1 = p0 * p1 * tot[ROW_LB:ROW_LB + 8, :]
        grads = [tot[ROW_SBN:ROW_SBN + 8, :], tot[ROW_QG:ROW_QG + 1, :], tot[ROW_KG:ROW_KG + 1, :],
                 tot[pl.ds(ROW_HGN + me, 1), :], tot[ROW_OG:ROW_OG + 1, :],
                 jnp.concatenate([-d_l1, d_l1], axis=0)]
        for i, g in enumerate(grads):
            w_ref, m_ref, v_ref = wmv[3 * i:3 * i + 3]
            o = outs[1 + 4 * i:5 + 4 * i]
            o[0][...] = g
            o[1][...], o[2][...], o[3][...] = _adamw(w_ref[...], g, m_ref[...], v_ref[...])

    out_shape = [jax.ShapeDtypeStruct((8, HEAD_DIM), F32)]
    for i in range(6):
        out_shape += [jax.ShapeDtypeStruct(small[3 * i].shape, F32)] * 4
    return _pcall(
        body, name="small_update",
        in_specs=[_VMEM] * n_in, out_specs=[_VMEM] * 25, out_shape=out_shape,
        scratch_shapes=[pltpu.VMEM((PACK_ROWS, HEAD_DIM), F32), pltpu.VMEM((N_DEV, PACK_ROWS, HEAD_DIM), F32),
                        pltpu.VMEM((PACK_ROWS, HEAD_DIM), F32)] + _exchange_sems(1),
    )(g_sbn, g_hgn, g_lb, g_qg, g_kg, g_og, loss_parts, *small)


def kernel(x, sb_norm, sb_w_in, sb_q_gain, sb_k_gain, sb_w_out, hg_norm, hg_w_in, hg_o_gain, hg_w_out, hg_lb_logits, loss_target, m_sb_norm, m_sb_w_in, m_sb_q_gain, m_sb_k_gain, m_sb_w_out, m_hg_norm, m_hg_w_in, m_hg_o_gain, m_hg_w_out, m_hg_lb_logits, v_sb_norm, v_sb_w_in, v_sb_q_gain, v_sb_k_gain, v_sb_w_out, v_hg_norm, v_hg_w_in, v_hg_o_gain, v_hg_w_out, v_hg_lb_logits):
    b = x.shape[0]
    wsi, wso, whi, who, hgn = all_gather_weights(sb_w_in[0], sb_w_out[0], hg_w_in[0], hg_w_out[0], hg_norm)
    hg_norm_full = hgn[:, 0, :].reshape(1, D_MODEL)
    r = local_step(x, loss_target, sb_norm, wsi, sb_q_gain, sb_k_gain, wso.reshape(D_MODEL, D_MODEL),
                   hg_norm_full, whi, hg_o_gain, who.reshape(D_MODEL, D_MODEL), hg_lb_logits)

    parts = exchange_grads([r["g_wsi"], r["g_wso"].reshape(N_DEV, W_ROWS, D_MODEL),
                            r["g_whi"], r["g_who"].reshape(N_DEV, W_ROWS, D_MODEL)])
    big = {}
    for name, p, w, m, v in (("sb_w_in", parts[0], sb_w_in, m_sb_w_in, v_sb_w_in),
                             ("sb_w_out", parts[1], sb_w_out, m_sb_w_out, v_sb_w_out),
                             ("hg_w_in", parts[2], hg_w_in, m_hg_w_in, v_hg_w_in),
                             ("hg_w_out", parts[3], hg_w_out, m_hg_w_out, v_hg_w_out)):
        big[name] = [o[None] for o in reduce_adamw(p, w[0], m[0], v[0], "adamw_" + name)]

    def rows8(a):
        return a.reshape(8, HEAD_DIM)

    def rows16(a):
        return a.reshape(16, HEAD_DIM)

    small_in = [rows8(sb_norm), rows8(m_sb_norm), rows8(v_sb_norm),
                sb_q_gain, m_sb_q_gain, v_sb_q_gain,
                sb_k_gain, m_sb_k_gain, v_sb_k_gain,
                hg_norm, m_hg_norm, v_hg_norm,
                hg_o_gain, m_hg_o_gain, v_hg_o_gain,
                rows16(hg_lb_logits), rows16(m_hg_lb_logits), rows16(v_hg_lb_logits)]
    so = small_update(rows8(r["g_sbn"]), rows8(r["g_hgn"]), r["g_lb"].reshape(b, N_HEADS, HEAD_DIM),
                      r["g_qg"].reshape(b * N_HEADS, HEAD_DIM), r["g_kg"].reshape(b * N_HEADS, HEAD_DIM),
                      r["g_og"].reshape(b * N_HEADS, HEAD_DIM), r["loss_parts"], small_in)
    loss = so[0][0, 0]
    shapes = {"sb_norm": (1, D_MODEL), "sb_q_gain": (1, HEAD_DIM), "sb_k_gain": (1, HEAD_DIM),
              "hg_norm": (1, HEAD_DIM), "hg_o_gain": (1, HEAD_DIM), "hg_lb_logits": (2, D_MODEL)}
    small = {}
    for i, name in enumerate(("sb_norm", "sb_q_gain", "sb_k_gain", "hg_norm", "hg_o_gain", "hg_lb_logits")):
        small[name] = [o.reshape(shapes[name]) for o in so[1 + 4 * i:5 + 4 * i]]
    order = ("sb_norm", "sb_w_in", "sb_q_gain", "sb_k_gain", "sb_w_out",
             "hg_norm", "hg_w_in", "hg_o_gain", "hg_w_out", "hg_lb_logits")
    res = {**big, **small}
    return (loss, r["gx"]) + tuple(res[n][j] for j in range(4) for n in order)
```

```python
import functools

import jax
import jax.numpy as jnp
from jax import lax
from jax.experimental import pallas as pl
from jax.experimental.pallas import tpu as pltpu

F32 = jnp.float32
BF16 = jnp.bfloat16

N_DEV = 8
D_MODEL = 1024
N_HEADS = 8
HEAD_DIM = 128
RMS_EPS = 1e-6
ATTN_BLOCK = 128
HG_CHUNK = 64
HG_SUB = 16
EXP_CLAMP = 80.0
SB_LOG_WEIGHT_FLOOR = -104.0
VMEM_LIMIT_BYTES = 48 * 1024 * 1024
W_COLS = 4 * D_MODEL // N_DEV
W_ROWS = D_MODEL // N_DEV

ADAM_LR = 0.001
ADAM_B1 = 0.9
ADAM_B2 = 0.999
ADAM_EPS = 1e-08
ADAM_WD = 0.01
ADAM_STEP = 10

NT = (((1,), (1,)), ((), ()))
TN = (((0,), (0,)), ((), ()))
NN = (((1,), (0,)), ((), ()))


def _pcall(body, *, name, **kw):
    return pl.pallas_call(body, name=name, **kw)


def _params(*sem):
    return pltpu.CompilerParams(dimension_semantics=sem, vmem_limit_bytes=VMEM_LIMIT_BYTES)


def _dot(a, b, dims=NN):
    return lax.dot_general(a, b, dims, preferred_element_type=F32)


def _dot_exact(a, m, dims=NN, left=False):
    hi = a.astype(BF16)
    lo = (a - hi.astype(F32)).astype(BF16)
    if left:
        return _dot(m, hi, dims) + _dot(m, lo, dims)
    return _dot(hi, m, dims) + _dot(lo, m, dims)


def _split(a):
    hi = a.astype(BF16)
    return hi, (a - hi.astype(F32)).astype(BF16)


def _dot3(a, b, dims=NN):
    return _dot(a[0], b[0], dims) + (_dot(a[0], b[1], dims) + _dot(a[1], b[0], dims))


def _sigmoid(x):
    return 1.0 / (1.0 + jnp.exp(-x))


def _sigmoid_pair(x):
    e = jnp.exp(-jnp.abs(x))
    big = 1.0 / (1.0 + e)
    small = e * big
    pos = x >= 0
    return jnp.where(pos, big, small), jnp.where(pos, small, big)


def _rms_scale(x):
    return lax.rsqrt(jnp.mean(x * x, axis=-1, keepdims=True) + RMS_EPS)


def _row_tile(t, want):
    return want if t % want == 0 else t


def rms_inproj(x2, gain, wg, name):
    t = x2.shape[0]
    tm = _row_tile(t, 512)

    def body(x_ref, g_ref, w_ref, proj_ref, u_ref):
        @pl.when(pl.program_id(1) == 0)
        def _():
            x = x_ref[...]
            u_ref[...] = (x * _rms_scale(x) * g_ref[...]).astype(BF16)

        proj_ref[0] = _dot(u_ref[...], w_ref[0])

    return _pcall(
        body, name=name,
        grid=(t // tm, N_DEV),
        in_specs=[pl.BlockSpec((tm, D_MODEL), lambda i, j: (i, 0)),
                  pl.BlockSpec((1, D_MODEL), lambda i, j: (0, 0)),
                  pl.BlockSpec((1, D_MODEL, W_COLS), lambda i, j: (j, 0, 0))],
        out_specs=[pl.BlockSpec((1, tm, W_COLS), lambda i, j: (j // 2, i, j % 2)),
                   pl.BlockSpec((tm, D_MODEL), lambda i, j: (i, 0))],
        out_shape=[jax.ShapeDtypeStruct((4, t, D_MODEL), F32),
                   jax.ShapeDtypeStruct((t, D_MODEL), BF16)],
        compiler_params=_params("parallel", "arbitrary"),
    )(x2, gain, wg)


def gate_outproj(o2, proj, w_out, resid, target, name):
    t = o2.shape[0]
    tm = _row_tile(t, 256)
    with_loss = target is not None

    def body(o_ref, gate_ref, w_ref, r_ref, *rest):
        g = gate_ref[0]
        og = (o_ref[...] * (g * _sigmoid(g))).astype(BF16)
        h = r_ref[...] + _dot(og, w_ref[...])
        if with_loss:
            t_ref, dh_ref, loss_ref = rest
            err = h - t_ref[...]
            dh_ref[...] = err * (1.0 / D_MODEL)
            part = 0.5 * jnp.sum(jnp.mean(err * err, axis=-1, keepdims=True))
            loss_ref[...] = jnp.full(loss_ref.shape, part, F32)
        else:
            (h_ref,) = rest
            h_ref[...] = h

    row = pl.BlockSpec((tm, D_MODEL), lambda i: (i, 0))
    in_specs = [row,
                pl.BlockSpec((1, tm, D_MODEL), lambda i: (3, i, 0)),
                pl.BlockSpec((D_MODEL, D_MODEL), lambda i: (0, 0)),
                row]
    args = [o2, proj, w_out, resid]
    if with_loss:
        in_specs.append(row)
        args.append(target)
        out_specs = [row, pl.BlockSpec((1, 8, 128), lambda i: (i, 0, 0))]
        out_shape = [jax.ShapeDtypeStruct((t, D_MODEL), F32),
                     jax.ShapeDtypeStruct((t // tm, 8, 128), F32)]
    else:
        out_specs = row
        out_shape = jax.ShapeDtypeStruct((t, D_MODEL), F32)
    return _pcall(body, name=name, grid=(t // tm,), in_specs=in_specs, out_specs=out_specs,
                  out_shape=out_shape, compiler_params=_params("parallel"))(*args)


def _head_spec(s, part):
    return pl.BlockSpec((1, 1, s, HEAD_DIM), lambda b, h: (part, b, 0, h))


def _seq_spec(s):
    return pl.BlockSpec((1, s, HEAD_DIM), lambda b, h: (b, 0, h))


_GAIN_SPEC = pl.BlockSpec((1, HEAD_DIM), lambda b, h: (0, 0))
_HEAD_ROW_SPEC = pl.BlockSpec((1, 1, 1, HEAD_DIM), lambda b, h: (b, h, 0, 0))


def _sb_scores(qi, kj, diag, tri_lt):
    z = _dot(qi, kj, NT) * (HEAD_DIM ** -0.5)
    soft = jnp.log1p(jnp.exp(-jnp.abs(z)))
    valid = jnp.logical_or(jnp.logical_not(diag), tri_lt)
    log_skip = jnp.where(valid, -(jnp.maximum(z, 0.0) + soft), 0.0)
    log_beta = jnp.minimum(z, 0.0) - soft
    return log_skip, log_beta, valid


def _sb_keys_left(i, state):
    done, carry = state[0], state[1]
    return jnp.logical_and(done <= i, jnp.logical_or(done == 0, jnp.max(carry) > SB_LOG_WEIGHT_FLOOR))


def sb_attn_fwd(proj4, q_gain, k_gain):
    _, b, s, _ = proj4.shape
    nq = s // ATTN_BLOCK
    blk = ATTN_BLOCK

    def body(q_ref, k_ref, v_ref, qg_ref, kg_ref, o_ref, qb, kb, vb):
        q = q_ref[0, 0]
        qb[...] = (q * _rms_scale(q) * qg_ref[...]).astype(BF16)
        k = k_ref[0, 0]
        kb[...] = (k * _rms_scale(k) * kg_ref[...]).astype(BF16)
        vb[...] = v_ref[0, 0].astype(BF16)
        row = lax.broadcasted_iota(jnp.int32, (blk, blk), 0)
        col = lax.broadcasted_iota(jnp.int32, (blk, blk), 1)
        tri_lt = col < row
        suffix = (row > col).astype(BF16)

        def q_block(i, _):
            qi = qb[pl.ds(pl.multiple_of(i * blk, blk), blk), :]

            def k_block(state):
                jj, c, acc = state
                j = i - jj
                rows = pl.ds(pl.multiple_of(j * blk, blk), blk)
                log_skip, log_beta, valid = _sb_scores(qi, kb[rows, :], jj == 0, tri_lt)
                after = _dot_exact(log_skip, suffix)
                w = jnp.where(valid, jnp.exp(log_beta + after + c), 0.0)
                acc = acc + _dot(w.astype(BF16), vb[rows, :])
                c = c + jnp.sum(log_skip, axis=1, keepdims=True)
                return jj + 1, c, acc

            _, _, acc = lax.while_loop(
                functools.partial(_sb_keys_left, i), k_block,
                (jnp.int32(0), jnp.zeros((blk, 1), F32), jnp.zeros((blk, HEAD_DIM), F32)))
            o_ref[0, pl.ds(pl.multiple_of(i * blk, blk), blk), :] = acc
            return 0

        lax.fori_loop(0, nq, q_block, 0)

    return _pcall(
        body, name="sb_attn_fwd", grid=(b, N_HEADS),
        in_specs=[_head_spec(s, 0), _head_spec(s, 1), _head_spec(s, 2), _GAIN_SPEC, _GAIN_SPEC],
        out_specs=_seq_spec(s),
        out_shape=jax.ShapeDtypeStruct((b, s, D_MODEL), F32),
        scratch_shapes=[pltpu.VMEM((s, HEAD_DIM), BF16)] * 3,
        compiler_params=_params("parallel", "parallel"),
    )(proj4, proj4, proj4, q_gain, k_gain)


def _hg_masks():
    c = HG_CHUNK
    row = lax.broadcasted_iota(jnp.int32, (c, c), 0)
    col = lax.broadcasted_iota(jnp.int32, (c, c), 1)
    incl = (col <= row)
    lower = incl.astype(BF16)
    before_sub = (col < (row // HG_SUB) * HG_SUB).astype(BF16)
    upper = (col >= row).astype(BF16)
    return incl, lower, before_sub, upper


def _hg_lower_bound(lbl_ref):
    l0 = lbl_ref[0, 0]
    l1 = lbl_ref[1, 0]
    d = l1 - l0
    return _sigmoid_pair(d)


def _hg_chunk_fwd(qp, fp, lb, oml, lower, before_sub):
    sq = _sigmoid(qp)
    q = qp * sq
    sf, sfn = _sigmoid_pair(fp)
    f = lb + oml * sf
    k = oml * sfn
    logf = jnp.log(f)
    gc = _dot_exact(logf, lower, left=True)
    gr = _dot_exact(logf, before_sub, left=True)
    return dict(sq=sq, q=q, sf=sf, sfn=sfn, f=f, k=k, gc=gc, gr=gr)


def _hg_intra(qd, k, gc, gr, incl):
    ks, es, rows = [], [], []
    for sub in range(HG_CHUNK // HG_SUB):
        lo = sub * HG_SUB
        e = jnp.exp(jnp.minimum(gr[lo:lo + 1, :] - gc, EXP_CLAMP))
        es.append(e)
        ks.append(k * e)
        rows.append(_dot(qd[lo:lo + HG_SUB, :].astype(BF16), ks[-1].astype(BF16), NT))
    a = jnp.where(incl, jnp.concatenate(rows, axis=0), 0.0)
    return a, ks, es


def _hg_intra3(qd, k, gc, gr, incl):
    qs = _split(qd)
    ks, es, rows = [], [], []
    for sub in range(HG_CHUNK // HG_SUB):
        lo = sub * HG_SUB
        e = jnp.exp(jnp.minimum(gr[lo:lo + 1, :] - gc, EXP_CLAMP))
        es.append(e)
        ks.append(_split(k * e))
        rows.append(_dot3((qs[0][lo:lo + HG_SUB, :], qs[1][lo:lo + HG_SUB, :]), ks[-1], NT))
    a = jnp.where(incl, jnp.concatenate(rows, axis=0), 0.0)
    return a, qs, ks, es


def hgrn2_fwd(proj4, lbl4, o_gain):
    _, b, s, _ = proj4.shape
    nchunk = s // HG_CHUNK
    c = HG_CHUNK

    def body(q_ref, f_ref, i_ref, lbl_ref, og_ref, o_ref, st_ref):
        incl, lower, before_sub, _ = _hg_masks()
        lb, oml = _hg_lower_bound(lbl_ref)
        st_ref[...] = jnp.zeros_like(st_ref)

        def chunk(n, _):
            rows = pl.ds(pl.multiple_of(n * c, c), c)
            v = _hg_chunk_fwd(q_ref[0, 0, rows, :], f_ref[0, 0, rows, :], lb, oml, lower, before_sub)
            inp = i_ref[0, 0, rows, :].astype(BF16)
            gc = v["gc"]
            qd = v["q"] * jnp.exp(gc - v["gr"])
            a, _, _ = _hg_intra(qd, v["k"], gc, v["gr"], incl)
            st = st_ref[...]
            o = _dot(a.astype(BF16), inp) + _dot((v["q"] * jnp.exp(gc)).astype(BF16), st.astype(BF16), NT)
            gl = gc[c - 1:c, :]
            kk = v["k"] * jnp.exp(gl - gc)
            st_ref[...] = st * jnp.exp(gl) + _dot(inp, kk.astype(BF16), TN)
            o_ref[0, rows, :] = o * _rms_scale(o) * og_ref[...]
            return 0

        lax.fori_loop(0, nchunk, chunk, 0)

    return _pcall(
        body, name="hgrn2_fwd", grid=(b, N_HEADS),
        in_specs=[_head_spec(s, 0), _head_spec(s, 1), _head_spec(s, 2),
                  pl.BlockSpec((2, 1, 1, HEAD_DIM), lambda b, h: (0, h, 0, 0)), _GAIN_SPEC],
        out_specs=_seq_spec(s),
        out_shape=jax.ShapeDtypeStruct((b, s, D_MODEL), F32),
        scratch_shapes=[pltpu.VMEM((HEAD_DIM, HEAD_DIM), F32)],
        compiler_params=_params("parallel", "parallel"),
    )(proj4, proj4, proj4, lbl4, o_gain)


def outproj_bwd(dh, w_out, o2, proj, name):
    t = dh.shape[0]
    tm = _row_tile(t, 256)

    def body(dh_ref, w_ref, o_ref, gate_ref, do_ref, dproj_ref, dw_ref):
        dhb = dh_ref[...].astype(BF16)
        dog = _dot(dhb, w_ref[...], NT)
        g = gate_ref[0]
        sg = _sigmoid(g)
        silu = g * sg
        o = o_ref[...]
        do_ref[...] = dog * silu
        dproj_ref[0] = dog * o * (sg * (1.0 + g * (1.0 - sg)))
        part = _dot((o * silu).astype(BF16), dhb, TN)

        @pl.when(pl.program_id(0) == 0)
        def _():
            dw_ref[...] = part

        @pl.when(pl.program_id(0) > 0)
        def _():
            dw_ref[...] += part

    row = pl.BlockSpec((tm, D_MODEL), lambda i: (i, 0))
    full = pl.BlockSpec((D_MODEL, D_MODEL), lambda i: (0, 0))
    return _pcall(
        body, name=name, grid=(t // tm,),
        in_specs=[row, full, row, pl.BlockSpec((1, tm, D_MODEL), lambda i: (3, i, 0))],
        out_specs=[row, pl.BlockSpec((1, tm, D_MODEL), lambda i: (3, i, 0)), full],
        out_shape=[jax.ShapeDtypeStruct((t, D_MODEL), F32),
                   jax.ShapeDtypeStruct((4, t, D_MODEL), F32),
                   jax.ShapeDtypeStruct((D_MODEL, D_MODEL), F32)],
        compiler_params=_params("arbitrary"),
    )(dh, w_out, o2, proj)


def inproj_bwd_dx(dproj, wg, x2, gain, dres, name):
    t = x2.shape[0]
    tm = _row_tile(t, 256)

    def body(d_ref, w_ref, x_ref, g_ref, r_ref, dx_ref, dg_ref):
        du = jnp.zeros((tm, D_MODEL), F32)
        for p in range(N_DEV):
            cols = slice((p % 2) * W_COLS, (p % 2 + 1) * W_COLS)
            du = du + _dot(d_ref[p // 2, :, cols].astype(BF16), w_ref[p], NT)
        x = x_ref[...]
        r = _rms_scale(x)
        xh = x * r
        a = du * g_ref[...]
        dx_ref[...] = r_ref[...] + r * (a - xh * jnp.mean(a * xh, axis=-1, keepdims=True))
        part = jnp.sum(du * xh, axis=0, keepdims=True)

        @pl.when(pl.program_id(0) == 0)
        def _():
            dg_ref[...] = part

        @pl.when(pl.program_id(0) > 0)
        def _():
            dg_ref[...] += part

    row = pl.BlockSpec((tm, D_MODEL), lambda i: (i, 0))
    vec = pl.BlockSpec((1, D_MODEL), lambda i: (0, 0))
    return _pcall(
        body, name=name, grid=(t // tm,),
        in_specs=[pl.BlockSpec((4, tm, D_MODEL), lambda i: (0, i, 0)),
                  pl.BlockSpec((N_DEV, D_MODEL, W_COLS), lambda i: (0, 0, 0)),
                  row, vec, row],
        out_specs=[row, vec],
        out_shape=[jax.ShapeDtypeStruct((t, D_MODEL), F32), jax.ShapeDtypeStruct((1, D_MODEL), F32)],
        compiler_params=_params("arbitrary"),
    )(dproj, wg, x2, gain, dres)


def inproj_bwd_dw(u, dproj, name):
    t = u.shape[0]
    tm = _row_tile(t, 512)

    def body(u_ref, d_ref, dw_ref):
        part = _dot(u_ref[...], d_ref[0].astype(BF16), TN)

        @pl.when(pl.program_id(1) == 0)
        def _():
            dw_ref[0] = part

        @pl.when(pl.program_id(1) > 0)
        def _():
            dw_ref[0] += part

    return _pcall(
        body, name=name, grid=(N_DEV, t // tm),
        in_specs=[pl.BlockSpec((tm, D_MODEL), lambda j, i: (i, 0)),
                  pl.BlockSpec((1, tm, W_COLS), lambda j, i: (j // 2, i, j % 2))],
        out_specs=pl.BlockSpec((1, D_MODEL, W_COLS), lambda j, i: (j, 0, 0)),
        out_shape=jax.ShapeDtypeStruct((N_DEV, D_MODEL, W_COLS), F32),
        compiler_params=_params("parallel", "arbitrary"),
    )(u, dproj)


def _rms_bwd(x, gain, dy):
    r = _rms_scale(x)
    xh = x * r
    a = dy * gain
    return r * (a - xh * jnp.mean(a * xh, axis=-1, keepdims=True)), dy * xh


def sb_attn_bwd(proj4, do3, o3, q_gain, k_gain, dproj4):
    _, b, s, _ = proj4.shape
    nq = s // ATTN_BLOCK
    blk = ATTN_BLOCK

    def body(q_ref, k_ref, v_ref, do_ref, o_ref, qg_ref, kg_ref, _alias, d_ref, dqg_ref, dkg_ref, qb, kb, vb, dob):
        q = q_ref[0, 0]
        qb[...] = (q * _rms_scale(q) * qg_ref[...]).astype(BF16)
        k = k_ref[0, 0]
        kb[...] = (k * _rms_scale(k) * kg_ref[...]).astype(BF16)
        vb[...] = v_ref[0, 0].astype(BF16)
        dob[...] = do_ref[0].astype(BF16)
        d_ref[...] = jnp.zeros_like(d_ref)
        row = lax.broadcasted_iota(jnp.int32, (blk, blk), 0)
        col = lax.broadcasted_iota(jnp.int32, (blk, blk), 1)
        tri_lt = col < row
        suffix = (row > col).astype(BF16)
        suffix_incl = (row >= col).astype(BF16)

        def q_block(i, _):
            rows_i = pl.ds(pl.multiple_of(i * blk, blk), blk)
            qi = qb[rows_i, :]
            doi = dob[rows_i, :]
            delta = jnp.sum(doi.astype(F32) * o_ref[0, rows_i, :], axis=1, keepdims=True)

            def k_block(state):
                jj, c, cg, dq = state
                j = i - jj
                rows = pl.ds(pl.multiple_of(j * blk, blk), blk)
                kj = kb[rows, :]
                vj = vb[rows, :]
                log_skip, log_beta, valid = _sb_scores(qi, kj, jj == 0, tri_lt)
                after = _dot_exact(log_skip, suffix)
                w = jnp.where(valid, jnp.exp(log_beta + after + c), 0.0)
                beta = jnp.exp(log_beta)
                wb = w.astype(BF16)
                g = _dot(doi, vj, NT) * wb.astype(F32)
                d_ref[2, 0, rows, :] += _dot(wb, doi, TN)
                before = delta - (_dot_exact(g, suffix_incl) + cg)
                dz = jnp.where(valid, g * (1.0 - beta) - before * beta, 0.0) * (HEAD_DIM ** -0.5)
                dzb = dz.astype(BF16)
                dq = dq + _dot(dzb, kj)
                d_ref[1, 0, rows, :] += _dot(dzb, qi, TN)
                c = c + jnp.sum(log_skip, axis=1, keepdims=True)
                cg = cg + jnp.sum(g, axis=1, keepdims=True)
                return jj + 1, c, cg, dq

            zero = jnp.zeros((blk, 1), F32)
            _, _, _, dq = lax.while_loop(functools.partial(_sb_keys_left, i), k_block,
                                         (jnp.int32(0), zero, zero, jnp.zeros((blk, HEAD_DIM), F32)))
            d_ref[0, 0, rows_i, :] = dq
            return 0

        lax.fori_loop(0, nq, q_block, 0)

        def norm_block(i, carry):
            gq, gk = carry
            rows = pl.ds(pl.multiple_of(i * blk, blk), blk)
            dq, pq = _rms_bwd(q_ref[0, 0, rows, :], qg_ref[...], d_ref[0, 0, rows, :])
            d_ref[0, 0, rows, :] = dq
            dk, pk = _rms_bwd(k_ref[0, 0, rows, :], kg_ref[...], d_ref[1, 0, rows, :])
            d_ref[1, 0, rows, :] = dk
            return gq + jnp.sum(pq, axis=0, keepdims=True), gk + jnp.sum(pk, axis=0, keepdims=True)

        zero = jnp.zeros((1, HEAD_DIM), F32)
        gq, gk = lax.fori_loop(0, nq, norm_block, (zero, zero))
        dqg_ref[0, 0] = gq
        dkg_ref[0, 0] = gk

    head_row = jax.ShapeDtypeStruct((b, N_HEADS, 1, HEAD_DIM), F32)
    return _pcall(
        body, name="sb_attn_bwd", grid=(b, N_HEADS),
        in_specs=[_head_spec(s, 0), _head_spec(s, 1), _head_spec(s, 2), _seq_spec(s), _seq_spec(s),
                  _GAIN_SPEC, _GAIN_SPEC, pl.BlockSpec(memory_space=pl.ANY)],
        out_specs=[pl.BlockSpec((3, 1, s, HEAD_DIM), lambda b, h: (0, b, 0, h)), _HEAD_ROW_SPEC, _HEAD_ROW_SPEC],
        out_shape=[jax.ShapeDtypeStruct(dproj4.shape, F32), head_row, head_row],
        scratch_shapes=[pltpu.VMEM((s, HEAD_DIM), BF16)] * 4,
        input_output_aliases={7: 0},
        compiler_params=_params("parallel", "parallel"),
    )(proj4, proj4, proj4, do3, o3, q_gain, k_gain, dproj4)


def hgrn2_bwd(proj4, don3, lbl4, o_gain, dproj4):
    _, b, s, _ = proj4.shape
    nchunk = s // HG_CHUNK
    c = HG_CHUNK
    nsub = HG_CHUNK // HG_SUB

    def body(q_ref, f_ref, i_ref, don_ref, lbl_ref, og_ref, _alias, d_ref, dog_ref, dlb_ref, st_all, o_all, dst_ref):
        incl, lower, before_sub, upper = _hg_masks()
        lb, oml = _hg_lower_bound(lbl_ref)
        last_row = lax.broadcasted_iota(jnp.int32, (c, HEAD_DIM), 0) == c - 1
        st_all[0] = jnp.zeros((HEAD_DIM, HEAD_DIM), F32)

        def fwd_chunk(n, _):
            rows = pl.ds(pl.multiple_of(n * c, c), c)
            v = _hg_chunk_fwd(q_ref[0, 0, rows, :], f_ref[0, 0, rows, :], lb, oml, lower, before_sub)
            inp = _split(i_ref[0, 0, rows, :])
            gc = v["gc"]
            qd = v["q"] * jnp.exp(gc - v["gr"])
            a, _, _, _ = _hg_intra3(qd, v["k"], gc, v["gr"], incl)
            st = st_all[n]
            o_all[rows, :] = (_dot3(_split(a), inp)
                              + _dot3(_split(v["q"] * jnp.exp(gc)), _split(st), NT))
            gl = gc[c - 1:c, :]
            kk = v["k"] * jnp.exp(gl - gc)

            @pl.when(n + 1 < nchunk)
            def _():
                st_all[n + 1] = st * jnp.exp(gl) + _dot3(inp, _split(kk), TN)

            return 0

        lax.fori_loop(0, nchunk, fwd_chunk, 0)
        dst_ref[...] = jnp.zeros_like(dst_ref)

        def bwd_chunk(m, carry):
            dog_acc, dlb_acc = carry
            n = nchunk - 1 - m
            rows = pl.ds(pl.multiple_of(n * c, c), c)
            qp = q_ref[0, 0, rows, :]
            v = _hg_chunk_fwd(qp, f_ref[0, 0, rows, :], lb, oml, lower, before_sub)
            inp = _split(i_ref[0, 0, rows, :])
            q, k, gc = v["q"], v["k"], v["gc"]
            e_q = jnp.exp(gc - v["gr"])
            a, qds, ks, es = _hg_intra3(q * e_q, k, gc, v["gr"], incl)
            e_gc = jnp.exp(gc)
            gl = gc[c - 1:c, :]
            e_gl = jnp.exp(gl)
            e_k = jnp.exp(gl - gc)
            st = st_all[n]
            dst = dst_ref[...]
            dsts = _split(dst)
            don = don_ref[0, rows, :]
            do, pg = _rms_bwd(o_all[rows, :], og_ref[...], don)
            dog_acc = dog_acc + jnp.sum(pg, axis=0, keepdims=True)
            dos = _split(do)
            das = _split(jnp.where(incl, _dot3(dos, inp, NT), 0.0))
            di = _dot3(_split(a), dos, TN) + _dot3(_split(k * e_k), dsts, NT)
            dqd, dk_intra = [], jnp.zeros((c, HEAD_DIM), F32)
            for sub in range(nsub):
                sl = slice(sub * HG_SUB, (sub + 1) * HG_SUB)
                da_sub = (das[0][sl, :], das[1][sl, :])
                dqd.append(_dot3(da_sub, ks[sub]))
                dk_intra = dk_intra + _dot3(da_sub, (qds[0][sl, :], qds[1][sl, :]), TN) * es[sub]
            dq = jnp.concatenate(dqd, axis=0) * e_q + _dot3(dos, _split(st)) * e_gc
            dk_inter = _dot3(inp, dsts) * e_k
            dk = dk_intra + dk_inter
            at_last = (jnp.sum(k * dk_inter, axis=0, keepdims=True)
                       + e_gl * jnp.sum(st * dst, axis=0, keepdims=True))
            dgc = q * dq - k * dk + jnp.where(last_row, at_last, 0.0)
            dlogf = _dot_exact(dgc, upper, left=True)
            dst_ref[...] = dst * e_gl + _dot3(dos, _split(q * e_gc), TN)
            sq, sf, sfn, f = v["sq"], v["sf"], v["sfn"], v["f"]
            dlf_f = dlogf / f
            d_ref[0, 0, rows, :] = dq * (sq * (1.0 + qp * (1.0 - sq)))
            d_ref[1, 0, rows, :] = (dlf_f - dk) * (oml * sf * sfn)
            d_ref[2, 0, rows, :] = di
            dlb_acc = dlb_acc + jnp.sum((dlf_f - dk) * sfn, axis=0, keepdims=True)
            return dog_acc, dlb_acc

        zero = jnp.zeros((1, HEAD_DIM), F32)
        dog, dlb = lax.fori_loop(0, nchunk, bwd_chunk, (zero, zero))
        dog_ref[0, 0] = dog
        dlb_ref[0, 0] = dlb

    head_row = jax.ShapeDtypeStruct((b, N_HEADS, 1, HEAD_DIM), F32)
    return _pcall(
        body, name="hgrn2_bwd", grid=(b, N_HEADS),
        in_specs=[_head_spec(s, 0), _head_spec(s, 1), _head_spec(s, 2), _seq_spec(s),
                  pl.BlockSpec((2, 1, 1, HEAD_DIM), lambda b, h: (0, h, 0, 0)), _GAIN_SPEC,
                  pl.BlockSpec(memory_space=pl.ANY)],
        out_specs=[pl.BlockSpec((3, 1, s, HEAD_DIM), lambda b, h: (0, b, 0, h)), _HEAD_ROW_SPEC, _HEAD_ROW_SPEC],
        out_shape=[jax.ShapeDtypeStruct(dproj4.shape, F32), head_row, head_row],
        scratch_shapes=[pltpu.VMEM((nchunk, HEAD_DIM, HEAD_DIM), F32), pltpu.VMEM((s, HEAD_DIM), F32),
                        pltpu.VMEM((HEAD_DIM, HEAD_DIM), F32)],
        input_output_aliases={6: 0},
        compiler_params=_params("parallel", "parallel"),
    )(proj4, proj4, proj4, don3, lbl4, o_gain, dproj4)


def local_step(x, target, sb_norm, wsi, sb_q_gain, sb_k_gain, wso, hg_norm_full, whi, hg_o_gain, who, hg_lb_logits):
    b, s, _ = x.shape
    t = b * s
    x2 = x.reshape(t, D_MODEL)
    tg2 = target.reshape(t, D_MODEL)
    lbl4 = hg_lb_logits.reshape(2, N_HEADS, 1, HEAD_DIM)
    four = (4, b, s, D_MODEL)
    three = (b, s, D_MODEL)

    proj0, u0 = rms_inproj(x2, sb_norm, wsi, "sb_inproj")
    o0 = sb_attn_fwd(proj0.reshape(four), sb_q_gain, sb_k_gain).reshape(t, D_MODEL)
    h1 = gate_outproj(o0, proj0, wso, x2, None, "sb_outproj")
    proj1, u1 = rms_inproj(h1, hg_norm_full, whi, "hg_inproj")
    o1 = hgrn2_fwd(proj1.reshape(four), lbl4, hg_o_gain).reshape(t, D_MODEL)
    dh2, loss_parts = gate_outproj(o1, proj1, who, h1, tg2, "hg_outproj_loss")

    do1, dproj1, g_who = outproj_bwd(dh2, who, o1, proj1, "hg_outproj_bwd")
    dproj1, g_og, g_lb = hgrn2_bwd(proj1.reshape(four), do1.reshape(three), lbl4, hg_o_gain, dproj1.reshape(four))
    dproj1 = dproj1.reshape(4, t, D_MODEL)
    dh1, g_hgn = inproj_bwd_dx(dproj1, whi, h1, hg_norm_full, dh2, "hg_inproj_bwd_dx")
    g_whi = inproj_bwd_dw(u1, dproj1, "hg_inproj_bwd_dw")

    do0, dproj0, g_wso = outproj_bwd(dh1, wso, o0, proj0, "sb_outproj_bwd")
    dproj0, g_qg, g_kg = sb_attn_bwd(proj0.reshape(four), do0.reshape(three), o0.reshape(three),
                                     sb_q_gain, sb_k_gain, dproj0.reshape(four))
    dproj0 = dproj0.reshape(4, t, D_MODEL)
    gx, g_sbn = inproj_bwd_dx(dproj0, wsi, x2, sb_norm, dh1, "sb_inproj_bwd_dx")
    g_wsi = inproj_bwd_dw(u0, dproj0, "sb_inproj_bwd_dw")
    return dict(loss_parts=loss_parts, gx=gx.reshape(three), g_wsi=g_wsi, g_wso=g_wso, g_whi=g_whi, g_who=g_who,
                g_sbn=g_sbn, g_hgn=g_hgn, g_qg=g_qg, g_kg=g_kg, g_og=g_og, g_lb=g_lb)


MESH = pl.DeviceIdType.MESH
N_PEERS = N_DEV - 1
_ANY = pl.BlockSpec(memory_space=pl.ANY)
_VMEM = pl.BlockSpec(memory_space=pltpu.VMEM)


def _mesh_pos():
    return lax.axis_index("x"), lax.axis_index("y"), lax.axis_index("c")


def _linear(pos):
    return 4 * pos[0] + 2 * pos[1] + pos[2]


def _peer(pos, k):
    flips = ((k + 1) >> 2 & 1, (k + 1) >> 1 & 1, (k + 1) & 1)
    return tuple(1 - p if f else p for p, f in zip(pos, flips))


def _exchange(pairs, send_sems, recv_sems, local_sems, pos):
    me = _linear(pos)
    started = []
    for a, (src_of, dst) in enumerate(pairs):
        loc = pltpu.make_async_copy(src_of(me), dst.at[me], local_sems.at[a])
        loc.start()
        started.append(loc)
        for k in range(N_PEERS):
            peer = _peer(pos, k)
            pltpu.make_async_remote_copy(
                src_ref=src_of(_linear(peer)), dst_ref=dst.at[me], send_sem=send_sems.at[a, k],
                recv_sem=recv_sems.at[a, k], device_id=peer, device_id_type=MESH).start()
    for a, (src_of, dst) in enumerate(pairs):
        for k in range(N_PEERS):
            peer = _peer(pos, k)
            landed = pltpu.make_async_remote_copy(
                src_ref=src_of(_linear(peer)), dst_ref=dst.at[_linear(peer)], send_sem=send_sems.at[a, k],
                recv_sem=recv_sems.at[a, k], device_id=peer, device_id_type=MESH)
            landed.wait_recv()
            landed.wait_send()
    for loc in started:
        loc.wait()


def _exchange_sems(n):
    return [pltpu.SemaphoreType.DMA((n, N_PEERS)), pltpu.SemaphoreType.DMA((n, N_PEERS)),
            pltpu.SemaphoreType.DMA((n,))]


def all_gather_weights(w_si, w_so, w_hi, w_ho, hg_norm):
    shards = [w_si, w_so, w_hi, w_ho]

    def body(si_ref, so_ref, hi_ref, ho_ref, hn_ref, o_si, o_so, o_hi, o_ho, o_hn,
             b_si, b_so, b_hi, b_ho, b_hn, send_sems, recv_sems, local_sems):
        for src, buf in ((si_ref, b_si), (so_ref, b_so), (hi_ref, b_hi), (ho_ref, b_ho)):
            buf[...] = src[...].astype(BF16)
        b_hn[...] = jnp.broadcast_to(hn_ref[...], b_hn.shape)
        pairs = [((lambda p, buf=buf: buf), out) for buf, out in
                 ((b_si, o_si), (b_so, o_so), (b_hi, o_hi), (b_ho, o_ho), (b_hn, o_hn))]
        _exchange(pairs, send_sems, recv_sems, local_sems, _mesh_pos())

    out_shape = [jax.ShapeDtypeStruct((N_DEV,) + w.shape, BF16) for w in shards]
    out_shape.append(jax.ShapeDtypeStruct((N_DEV, 8, HEAD_DIM), F32))
    scratch = [pltpu.VMEM(w.shape, BF16) for w in shards] + [pltpu.VMEM((8, HEAD_DIM), F32)]
    return _pcall(
        body, name="all_gather_weights",
        in_specs=[_VMEM] * 5, out_specs=[_ANY] * 5, out_shape=out_shape,
        scratch_shapes=scratch + _exchange_sems(5),
        compiler_params=pltpu.CompilerParams(vmem_limit_bytes=VMEM_LIMIT_BYTES),
    )(w_si, w_so, w_hi, w_ho, hg_norm)


def exchange_grads(grads):
    def body(*refs):
        n = len(grads)
        ins, outs = refs[:n], refs[n:2 * n]
        send_sems, recv_sems, local_sems = refs[2 * n:]
        pairs = [((lambda p, g=g: g.at[p]), r) for g, r in zip(ins, outs)]
        _exchange(pairs, send_sems, recv_sems, local_sems, _mesh_pos())

    return _pcall(
        body, name="exchange_grads",
        in_specs=[_ANY] * len(grads), out_specs=[_ANY] * len(grads),
        out_shape=[jax.ShapeDtypeStruct(g.shape, F32) for g in grads],
        scratch_shapes=_exchange_sems(len(grads)),
    )(*grads)


def _adamw(w, g, m, v):
    m = ADAM_B1 * m + (1.0 - ADAM_B1) * g
    v = ADAM_B2 * v + (1.0 - ADAM_B2) * (g * g)
    m_hat = m / (1.0 - ADAM_B1 ** ADAM_STEP)
    v_hat = v / (1.0 - ADAM_B2 ** ADAM_STEP)
    delta = -ADAM_LR * (m_hat / (jnp.sqrt(v_hat) + ADAM_EPS) + ADAM_WD * w)
    return delta, m, v


def reduce_adamw(parts, w, m, v, name):
    _, r, c = parts.shape
    tr = _row_tile(r, 256)

    def body(p_ref, w_ref, m_ref, v_ref, g_ref, d_ref, m2_ref, v2_ref):
        g = p_ref[0]
        for dev in range(1, N_DEV):
            g = g + p_ref[dev]
        g_ref[...] = g
        d_ref[...], m2_ref[...], v2_ref[...] = _adamw(w_ref[...], g, m_ref[...], v_ref[...])

    tile = pl.BlockSpec((tr, c), lambda i: (i, 0))
    return _pcall(
        body, name=name, grid=(r // tr,),
        in_specs=[pl.BlockSpec((N_DEV, tr, c), lambda i: (0, i, 0)), tile, tile, tile],
        out_specs=[tile] * 4, out_shape=[jax.ShapeDtypeStruct((r, c), F32)] * 4,
        compiler_params=_params("parallel"),
    )(parts, w, m, v)


PACK_ROWS = 32
ROW_SBN, ROW_HGN, ROW_LB, ROW_QG, ROW_KG, ROW_OG, ROW_LOSS = 0, 8, 16, 24, 25, 26, 27


def small_update(g_sbn, g_hgn, g_lb, g_qg, g_kg, g_og, loss_parts, small):
    n_in = 7 + len(small)

    def body(*refs):
        sbn_ref, hgn_ref, lb_ref, qg_ref, kg_ref, og_ref, loss_ref = refs[:7]
        wmv = refs[7:n_in]
        outs = refs[n_in:n_in + 25]
        pack, gath, tot, send_sems, recv_sems, local_sems = refs[n_in + 25:]
        pos = _mesh_pos()
        me = _linear(pos)
        pack[...] = jnp.zeros_like(pack)
        pack[ROW_SBN:ROW_SBN + 8, :] = sbn_ref[...]
        pack[ROW_HGN:ROW_HGN + 8, :] = hgn_ref[...]
        pack[ROW_LB:ROW_LB + 8, :] = jnp.sum(lb_ref[...], axis=0)
        pack[ROW_QG:ROW_QG + 1, :] = jnp.sum(qg_ref[...], axis=0, keepdims=True)
        pack[ROW_KG:ROW_KG + 1, :] = jnp.sum(kg_ref[...], axis=0, keepdims=True)
        pack[ROW_OG:ROW_OG + 1, :] = jnp.sum(og_ref[...], axis=0, keepdims=True)
        pack[ROW_LOSS:ROW_LOSS + 1, :] = jnp.sum(loss_ref[...], axis=0)[0:1, :]
        _exchange([((lambda p: pack), gath)], send_sems, recv_sems, local_sems, pos)
        total = gath[0]
        for dev in range(1, N_DEV):
            total = total + gath[dev]
        tot[...] = total
        outs[0][...] = jnp.broadcast_to(tot[ROW_LOSS:ROW_LOSS + 1, :], (8, HEAD_DIM))
        l0 = wmv[15][0:8, :]
        l1 = wmv[15][8:16, :]
        p1, p0 = _sigmoid_pair(l1 - l0)
        d_l1 = p0 * p1 * tot[ROW_LB:ROW_LB + 8, :]
        grads = [tot[ROW_SBN:ROW_SBN + 8, :], tot[ROW_QG:ROW_QG + 1, :], tot[ROW_KG:ROW_KG + 1, :],
                 tot[pl.ds(ROW_HGN + me, 1), :], tot[ROW_OG:ROW_OG + 1, :],
                 jnp.concatenate([-d_l1, d_l1], axis=0)]
        for i, g in enumerate(grads):
            w_ref, m_ref, v_ref = wmv[3 * i:3 * i + 3]
            o = outs[1 + 4 * i:5 + 4 * i]
            o[0][...] = g
            o[1][...], o[2][...], o[3][...] = _adamw(w_ref[...], g, m_ref[...], v_ref[...])

    out_shape = [jax.ShapeDtypeStruct((8, HEAD_DIM), F32)]
    for i in range(6):
        out_shape += [jax.ShapeDtypeStruct(small[3 * i].shape, F32)] * 4
    return _pcall(
        body, name="small_update",
        in_specs=[_VMEM] * n_in, out_specs=[_VMEM] * 25, out_shape=out_shape,
        scratch_shapes=[pltpu.VMEM((PACK_ROWS, HEAD_DIM), F32), pltpu.VMEM((N_DEV, PACK_ROWS, HEAD_DIM), F32),
                        pltpu.VMEM((PACK_ROWS, HEAD_DIM), F32)] + _exchange_sems(1),
    )(g_sbn, g_hgn, g_lb, g_qg, g_kg, g_og, loss_parts, *small)


def kernel(x, sb_norm, sb_w_in, sb_q_gain, sb_k_gain, sb_w_out, hg_norm, hg_w_in, hg_o_gain, hg_w_out, hg_lb_logits, loss_target, m_sb_norm, m_sb_w_in, m_sb_q_gain, m_sb_k_gain, m_sb_w_out, m_hg_norm, m_hg_w_in, m_hg_o_gain, m_hg_w_out, m_hg_lb_logits, v_sb_norm, v_sb_w_in, v_sb_q_gain, v_sb_k_gain, v_sb_w_out, v_hg_norm, v_hg_w_in, v_hg_o_gain, v_hg_w_out, v_hg_lb_logits):
    b = x.shape[0]
    wsi, wso, whi, who, hgn = all_gather_weights(sb_w_in[0], sb_w_out[0], hg_w_in[0], hg_w_out[0], hg_norm)
    hg_norm_full = hgn[:, 0, :].reshape(1, D_MODEL)
    r = local_step(x, loss_target, sb_norm, wsi, sb_q_gain, sb_k_gain, wso.reshape(D_MODEL, D_MODEL),
                   hg_norm_full, whi, hg_o_gain, who.reshape(D_MODEL, D_MODEL), hg_lb_logits)

    parts = exchange_grads([r["g_wsi"], r["g_wso"].reshape(N_DEV, W_ROWS, D_MODEL),
                            r["g_whi"], r["g_who"].reshape(N_DEV, W_ROWS, D_MODEL)])
    big = {}
    for name, p, w, m, v in (("sb_w_in", parts[0], sb_w_in, m_sb_w_in, v_sb_w_in),
                             ("sb_w_out", parts[1], sb_w_out, m_sb_w_out, v_sb_w_out),
                             ("hg_w_in", parts[2], hg_w_in, m_hg_w_in, v_hg_w_in),
                             ("hg_w_out", parts[3], hg_w_out, m_hg_w_out, v_hg_w_out)):
        big[name] = [o[None] for o in reduce_adamw(p, w[0], m[0], v[0], "adamw_" + name)]

    def rows8(a):
        return a.reshape(8, HEAD_DIM)

    def rows16(a):
        return a.reshape(16, HEAD_DIM)

    small_in = [rows8(sb_norm), rows8(m_sb_norm), rows8(v_sb_norm),
                sb_q_gain, m_sb_q_gain, v_sb_q_gain,
                sb_k_gain, m_sb_k_gain, v_sb_k_gain,
                hg_norm, m_hg_norm, v_hg_norm,
                hg_o_gain, m_hg_o_gain, v_hg_o_gain,
                rows16(hg_lb_logits), rows16(m_hg_lb_logits), rows16(v_hg_lb_logits)]
    so = small_update(rows8(r["g_sbn"]), rows8(r["g_hgn"]), r["g_lb"].reshape(b, N_HEADS, HEAD_DIM),
                      r["g_qg"].reshape(b * N_HEADS, HEAD_DIM), r["g_kg"].reshape(b * N_HEADS, HEAD_DIM),
                      r["g_og"].reshape(b * N_HEADS, HEAD_DIM), r["loss_parts"], small_in)
    loss = so[0][0, 0]
    shapes = {"sb_norm": (1, D_MODEL), "sb_q_gain": (1, HEAD_DIM), "sb_k_gain": (1, HEAD_DIM),
              "hg_norm": (1, HEAD_DIM), "hg_o_gain": (1, HEAD_DIM), "hg_lb_logits": (2, D_MODEL)}
    small = {}
    for i, name in enumerate(("sb_norm", "sb_q_gain", "sb_k_gain", "hg_norm", "hg_o_gain", "hg_lb_logits")):
        small[name] = [o.reshape(shapes[name]) for o in so[1 + 4 * i:5 + 4 * i]]
    order = ("sb_norm", "sb_w_in", "sb_q_gain", "sb_k_gain", "sb_w_out",
             "hg_norm", "hg_w_in", "hg_o_gain", "hg_w_out", "hg_lb_logits")
    res = {**big, **small}
    return (loss, r["gx"]) + tuple(res[n][j] for j in range(4) for n in order)
```

```python
import functools

import jax
import jax.numpy as jnp
from jax import lax
from jax.experimental import pallas as pl
from jax.experimental.pallas import tpu as pltpu

F32 = jnp.float32
BF16 = jnp.bfloat16

N_DEV = 8
D_MODEL = 1024
N_HEADS = 8
HEAD_DIM = 128
RMS_EPS = 1e-6
ATTN_BLOCK = 128
HG_CHUNK = 64
HG_SUB = 16
HG_UNROLL = 4
EXP_CLAMP = 80.0
SB_LOG_WEIGHT_FLOOR = -104.0
VMEM_LIMIT_BYTES = 48 * 1024 * 1024
W_COLS = 4 * D_MODEL // N_DEV
W_ROWS = D_MODEL // N_DEV

ADAM_LR = 0.001
ADAM_B1 = 0.9
ADAM_B2 = 0.999
ADAM_EPS = 1e-08
ADAM_WD = 0.01
ADAM_STEP = 10

NT = (((1,), (1,)), ((), ()))
TN = (((0,), (0,)), ((), ()))
NN = (((1,), (0,)), ((), ()))


def _pcall(body, *, name, **kw):
    return pl.pallas_call(body, name=name, **kw)


def _params(*sem):
    return pltpu.CompilerParams(dimension_semantics=sem, vmem_limit_bytes=VMEM_LIMIT_BYTES)


def _dot(a, b, dims=NN):
    return lax.dot_general(a, b, dims, preferred_element_type=F32)


def _dot_exact(a, m, dims=NN, left=False):
    hi = a.astype(BF16)
    lo = (a - hi.astype(F32)).astype(BF16)
    if left:
        return _dot(m, hi, dims) + _dot(m, lo, dims)
    return _dot(hi, m, dims) + _dot(lo, m, dims)


def _split(a):
    hi = a.astype(BF16)
    return hi, (a - hi.astype(F32)).astype(BF16)


def _dot3(a, b, dims=NN):
    return _dot(a[0], b[0], dims) + (_dot(a[0], b[1], dims) + _dot(a[1], b[0], dims))


def _sigmoid(x):
    return 1.0 / (1.0 + jnp.exp(-x))


def _sigmoid_pair(x):
    e = jnp.exp(-jnp.abs(x))
    big = 1.0 / (1.0 + e)
    small = e * big
    pos = x >= 0
    return jnp.where(pos, big, small), jnp.where(pos, small, big)


def _rms_scale(x):
    return lax.rsqrt(jnp.mean(x * x, axis=-1, keepdims=True) + RMS_EPS)


def _row_tile(t, want):
    return want if t % want == 0 else t


def rms_inproj(x2, gain, wg, name):
    t = x2.shape[0]
    tm = _row_tile(t, 512)

    def body(x_ref, g_ref, w_ref, proj_ref, u_ref):
        @pl.when(pl.program_id(1) == 0)
        def _():
            x = x_ref[...]
            u_ref[...] = (x * _rms_scale(x) * g_ref[...]).astype(BF16)

        proj_ref[0] = _dot(u_ref[...], w_ref[0])

    return _pcall(
        body, name=name,
        grid=(t // tm, N_DEV),
        in_specs=[pl.BlockSpec((tm, D_MODEL), lambda i, j: (i, 0)),
                  pl.BlockSpec((1, D_MODEL), lambda i, j: (0, 0)),
                  pl.BlockSpec((1, D_MODEL, W_COLS), lambda i, j: (j, 0, 0))],
        out_specs=[pl.BlockSpec((1, tm, W_COLS), lambda i, j: (j // 2, i, j % 2)),
                   pl.BlockSpec((tm, D_MODEL), lambda i, j: (i, 0))],
        out_shape=[jax.ShapeDtypeStruct((4, t, D_MODEL), F32),
                   jax.ShapeDtypeStruct((t, D_MODEL), BF16)],
        compiler_params=_params("parallel", "arbitrary"),
    )(x2, gain, wg)


def gate_outproj(o2, proj, w_out, resid, target, name):
    t = o2.shape[0]
    tm = _row_tile(t, 256)
    with_loss = target is not None

    def body(o_ref, gate_ref, w_ref, r_ref, *rest):
        g = gate_ref[0]
        og = (o_ref[...] * (g * _sigmoid(g))).astype(BF16)
        h = r_ref[...] + _dot(og, w_ref[...])
        if with_loss:
            t_ref, dh_ref, loss_ref = rest
            err = h - t_ref[...]
            dh_ref[...] = err * (1.0 / D_MODEL)
            part = 0.5 * jnp.sum(jnp.mean(err * err, axis=-1, keepdims=True))
            loss_ref[...] = jnp.full(loss_ref.shape, part, F32)
        else:
            (h_ref,) = rest
            h_ref[...] = h

    row = pl.BlockSpec((tm, D_MODEL), lambda i: (i, 0))
    in_specs = [row,
                pl.BlockSpec((1, tm, D_MODEL), lambda i: (3, i, 0)),
                pl.BlockSpec((D_MODEL, D_MODEL), lambda i: (0, 0)),
                row]
    args = [o2, proj, w_out, resid]
    if with_loss:
        in_specs.append(row)
        args.append(target)
        out_specs = [row, pl.BlockSpec((1, 8, 128), lambda i: (i, 0, 0))]
        out_shape = [jax.ShapeDtypeStruct((t, D_MODEL), F32),
                     jax.ShapeDtypeStruct((t // tm, 8, 128), F32)]
    else:
        out_specs = row
        out_shape = jax.ShapeDtypeStruct((t, D_MODEL), F32)
    return _pcall(body, name=name, grid=(t // tm,), in_specs=in_specs, out_specs=out_specs,
                  out_shape=out_shape, compiler_params=_params("parallel"))(*args)


def _head_spec(s, part):
    return pl.BlockSpec((1, 1, s, HEAD_DIM), lambda b, h: (part, b, 0, h))


def _seq_spec(s):
    return pl.BlockSpec((1, s, HEAD_DIM), lambda b, h: (b, 0, h))


_GAIN_SPEC = pl.BlockSpec((1, HEAD_DIM), lambda b, h: (0, 0))
_HEAD_ROW_SPEC = pl.BlockSpec((1, 1, 1, HEAD_DIM), lambda b, h: (b, h, 0, 0))


def _sb_scores(qi, kj, diag, tri_lt):
    z = _dot(qi, kj, NT) * (HEAD_DIM ** -0.5)
    soft = jnp.log1p(jnp.exp(-jnp.abs(z)))
    valid = jnp.logical_or(jnp.logical_not(diag), tri_lt)
    log_skip = jnp.where(valid, -(jnp.maximum(z, 0.0) + soft), 0.0)
    log_beta = jnp.minimum(z, 0.0) - soft
    return log_skip, log_beta, valid


def _sb_keys_left(i, state):
    done, carry = state[0], state[1]
    return jnp.logical_and(done <= i, jnp.logical_or(done == 0, jnp.max(carry) > SB_LOG_WEIGHT_FLOOR))


def sb_attn_fwd(proj4, q_gain, k_gain):
    _, b, s, _ = proj4.shape
    nq = s // ATTN_BLOCK
    blk = ATTN_BLOCK

    def body(q_ref, k_ref, v_ref, qg_ref, kg_ref, o_ref, qb, kb, vb):
        q = q_ref[0, 0]
        qb[...] = (q * _rms_scale(q) * qg_ref[...]).astype(BF16)
        k = k_ref[0, 0]
        kb[...] = (k * _rms_scale(k) * kg_ref[...]).astype(BF16)
        vb[...] = v_ref[0, 0].astype(BF16)
        row = lax.broadcasted_iota(jnp.int32, (blk, blk), 0)
        col = lax.broadcasted_iota(jnp.int32, (blk, blk), 1)
        tri_lt = col < row
        suffix = (row > col).astype(BF16)

        def q_block(i, _):
            qi = qb[pl.ds(pl.multiple_of(i * blk, blk), blk), :]

            def k_block(state):
                jj, c, acc = state
                j = i - jj
                rows = pl.ds(pl.multiple_of(j * blk, blk), blk)
                log_skip, log_beta, valid = _sb_scores(qi, kb[rows, :], jj == 0, tri_lt)
                after = _dot_exact(log_skip, suffix)
                w = jnp.where(valid, jnp.exp(log_beta + after + c), 0.0)
                acc = acc + _dot(w.astype(BF16), vb[rows, :])
                c = c + jnp.sum(log_skip, axis=1, keepdims=True)
                return jj + 1, c, acc

            _, _, acc = lax.while_loop(
                functools.partial(_sb_keys_left, i), k_block,
                (jnp.int32(0), jnp.zeros((blk, 1), F32), jnp.zeros((blk, HEAD_DIM), F32)))
            o_ref[0, pl.ds(pl.multiple_of(i * blk, blk), blk), :] = acc
            return 0

        lax.fori_loop(0, nq, q_block, 0)

    return _pcall(
        body, name="sb_attn_fwd", grid=(b, N_HEADS),
        in_specs=[_head_spec(s, 0), _head_spec(s, 1), _head_spec(s, 2), _GAIN_SPEC, _GAIN_SPEC],
        out_specs=_seq_spec(s),
        out_shape=jax.ShapeDtypeStruct((b, s, D_MODEL), F32),
        scratch_shapes=[pltpu.VMEM((s, HEAD_DIM), BF16)] * 3,
        compiler_params=_params("parallel", "parallel"),
    )(proj4, proj4, proj4, q_gain, k_gain)


def _hg_masks():
    c = HG_CHUNK
    row = lax.broadcasted_iota(jnp.int32, (c, c), 0)
    col = lax.broadcasted_iota(jnp.int32, (c, c), 1)
    incl = (col <= row)
    lower = incl.astype(BF16)
    before_sub = (col < (row // HG_SUB) * HG_SUB).astype(BF16)
    upper = (col >= row).astype(BF16)
    return incl, lower, before_sub, upper


def _hg_lower_bound(lbl_ref):
    l0 = lbl_ref[0, 0]
    l1 = lbl_ref[1, 0]
    d = l1 - l0
    return _sigmoid_pair(d)


def _hg_chunk_fwd(qp, fp, lb, oml, lower, before_sub):
    sq = _sigmoid(qp)
    q = qp * sq
    sf, sfn = _sigmoid_pair(fp)
    f = lb + oml * sf
    k = oml * sfn
    logf = jnp.log(f)
    gc = _dot_exact(logf, lower, left=True)
    gr = _dot_exact(logf, before_sub, left=True)
    return dict(sq=sq, q=q, sf=sf, sfn=sfn, f=f, k=k, gc=gc, gr=gr)


def _hg_intra(qd, k, gc, gr, incl):
    qdb = qd.astype(BF16)
    ksb, es, rows = [], [], []
    for sub in range(HG_CHUNK // HG_SUB):
        lo = sub * HG_SUB
        e = jnp.exp(jnp.minimum(gr[lo:lo + 1, :] - gc, EXP_CLAMP))
        es.append(e)
        ksb.append((k * e).astype(BF16))
        rows.append(_dot(qdb[lo:lo + HG_SUB, :], ksb[-1], NT))
    a = jnp.where(incl, jnp.concatenate(rows, axis=0), 0.0)
    return a, qdb, ksb, es


def _unrolled_loop(n, step, init):
    assert n % HG_UNROLL == 0

    def body(outer, carry):
        for u in range(HG_UNROLL):
            carry = step(outer * HG_UNROLL + u, carry)
        return carry

    return lax.fori_loop(0, n // HG_UNROLL, body, init)


def _state_spec(nchunk):
    return pl.BlockSpec((1, 1, nchunk, HEAD_DIM, HEAD_DIM), lambda b, h: (b, h, 0, 0, 0))


def hgrn2_fwd(proj4, lbl4, o_gain):
    _, b, s, _ = proj4.shape
    nchunk = s // HG_CHUNK
    c = HG_CHUNK

    def body(q_ref, f_ref, i_ref, lbl_ref, og_ref, o_ref, oraw_ref, st_ref):
        incl, lower, before_sub, _ = _hg_masks()
        lb, oml = _hg_lower_bound(lbl_ref)

        def chunk(n, st):
            rows = pl.ds(pl.multiple_of(n * c, c), c)
            v = _hg_chunk_fwd(q_ref[0, 0, rows, :], f_ref[0, 0, rows, :], lb, oml, lower, before_sub)
            inp = i_ref[0, 0, rows, :].astype(BF16)
            gc = v["gc"]
            a, _, _, _ = _hg_intra(v["q"] * jnp.exp(gc - v["gr"]), v["k"], gc, v["gr"], incl)
            st_ref[0, 0, n] = st
            o = _dot(a.astype(BF16), inp) + _dot((v["q"] * jnp.exp(gc)).astype(BF16), st.astype(BF16), NT)
            oraw_ref[0, rows, :] = o
            o_ref[0, rows, :] = o * _rms_scale(o) * og_ref[...]
            gl = gc[c - 1:c, :]
            kk = v["k"] * jnp.exp(gl - gc)
            return st * jnp.exp(gl) + _dot(inp, kk.astype(BF16), TN)

        _unrolled_loop(nchunk, chunk, jnp.zeros((HEAD_DIM, HEAD_DIM), F32))

    seq = jax.ShapeDtypeStruct((b, s, D_MODEL), F32)
    return _pcall(
        body, name="hgrn2_fwd", grid=(b, N_HEADS),
        in_specs=[_head_spec(s, 0), _head_spec(s, 1), _head_spec(s, 2),
                  pl.BlockSpec((2, 1, 1, HEAD_DIM), lambda b, h: (0, h, 0, 0)), _GAIN_SPEC],
        out_specs=[_seq_spec(s), _seq_spec(s), _state_spec(nchunk)],
        out_shape=[seq, seq, jax.ShapeDtypeStruct((b, N_HEADS, nchunk, HEAD_DIM, HEAD_DIM), F32)],
        compiler_params=_params("parallel", "parallel"),
    )(proj4, proj4, proj4, lbl4, o_gain)


def outproj_bwd(dh, w_out, o2, proj, name):
    t = dh.shape[0]
    tm = _row_tile(t, 256)

    def body(dh_ref, w_ref, o_ref, gate_ref, do_ref, dproj_ref, dw_ref):
        dhb = dh_ref[...].astype(BF16)
        dog = _dot(dhb, w_ref[...], NT)
        g = gate_ref[0]
        sg = _sigmoid(g)
        silu = g * sg
        o = o_ref[...]
        do_ref[...] = dog * silu
        dproj_ref[0] = dog * o * (sg * (1.0 + g * (1.0 - sg)))
        part = _dot((o * silu).astype(BF16), dhb, TN)

        @pl.when(pl.program_id(0) == 0)
        def _():
            dw_ref[...] = part

        @pl.when(pl.program_id(0) > 0)
        def _():
            dw_ref[...] += part

    row = pl.BlockSpec((tm, D_MODEL), lambda i: (i, 0))
    full = pl.BlockSpec((D_MODEL, D_MODEL), lambda i: (0, 0))
    return _pcall(
        body, name=name, grid=(t // tm,),
        in_specs=[row, full, row, pl.BlockSpec((1, tm, D_MODEL), lambda i: (3, i, 0))],
        out_specs=[row, pl.BlockSpec((1, tm, D_MODEL), lambda i: (3, i, 0)), full],
        out_shape=[jax.ShapeDtypeStruct((t, D_MODEL), F32),
                   jax.ShapeDtypeStruct((4, t, D_MODEL), F32),
                   jax.ShapeDtypeStruct((D_MODEL, D_MODEL), F32)],
        compiler_params=_params("arbitrary"),
    )(dh, w_out, o2, proj)


def inproj_bwd_dx(dproj, wg, x2, gain, dres, name):
    t = x2.shape[0]
    tm = _row_tile(t, 256)

    def body(d_ref, w_ref, x_ref, g_ref, r_ref, dx_ref, dg_ref):
        du = jnp.zeros((tm, D_MODEL), F32)
        for p in range(N_DEV):
            cols = slice((p % 2) * W_COLS, (p % 2 + 1) * W_COLS)
            du = du + _dot(d_ref[p // 2, :, cols].astype(BF16), w_ref[p], NT)
        x = x_ref[...]
        r = _rms_scale(x)
        xh = x * r
        a = du * g_ref[...]
        dx_ref[...] = r_ref[...] + r * (a - xh * jnp.mean(a * xh, axis=-1, keepdims=True))
        part = jnp.sum(du * xh, axis=0, keepdims=True)

        @pl.when(pl.program_id(0) == 0)
        def _():
            dg_ref[...] = part

        @pl.when(pl.program_id(0) > 0)
        def _():
            dg_ref[...] += part

    row = pl.BlockSpec((tm, D_MODEL), lambda i: (i, 0))
    vec = pl.BlockSpec((1, D_MODEL), lambda i: (0, 0))
    return _pcall(
        body, name=name, grid=(t // tm,),
        in_specs=[pl.BlockSpec((4, tm, D_MODEL), lambda i: (0, i, 0)),
                  pl.BlockSpec((N_DEV, D_MODEL, W_COLS), lambda i: (0, 0, 0)),
                  row, vec, row],
        out_specs=[row, vec],
        out_shape=[jax.ShapeDtypeStruct((t, D_MODEL), F32), jax.ShapeDtypeStruct((1, D_MODEL), F32)],
        compiler_params=_params("arbitrary"),
    )(dproj, wg, x2, gain, dres)


def inproj_bwd_dw(u, dproj, name):
    t = u.shape[0]
    tm = _row_tile(t, 512)

    def body(u_ref, d_ref, dw_ref):
        part = _dot(u_ref[...], d_ref[0].astype(BF16), TN)

        @pl.when(pl.program_id(1) == 0)
        def _():
            dw_ref[0] = part

        @pl.when(pl.program_id(1) > 0)
        def _():
            dw_ref[0] += part

    return _pcall(
        body, name=name, grid=(N_DEV, t // tm),
        in_specs=[pl.BlockSpec((tm, D_MODEL), lambda j, i: (i, 0)),
                  pl.BlockSpec((1, tm, W_COLS), lambda j, i: (j // 2, i, j % 2))],
        out_specs=pl.BlockSpec((1, D_MODEL, W_COLS), lambda j, i: (j, 0, 0)),
        out_shape=jax.ShapeDtypeStruct((N_DEV, D_MODEL, W_COLS), F32),
        compiler_params=_params("parallel", "arbitrary"),
    )(u, dproj)


def _rms_bwd(x, gain, dy):
    r = _rms_scale(x)
    xh = x * r
    a = dy * gain
    return r * (a - xh * jnp.mean(a * xh, axis=-1, keepdims=True)), dy * xh


def sb_attn_bwd(proj4, do3, o3, q_gain, k_gain, dproj4):
    _, b, s, _ = proj4.shape
    nq = s // ATTN_BLOCK
    blk = ATTN_BLOCK

    def body(q_ref, k_ref, v_ref, do_ref, o_ref, qg_ref, kg_ref, _alias, d_ref, dqg_ref, dkg_ref, qb, kb, vb, dob):
        q = q_ref[0, 0]
        qb[...] = (q * _rms_scale(q) * qg_ref[...]).astype(BF16)
        k = k_ref[0, 0]
        kb[...] = (k * _rms_scale(k) * kg_ref[...]).astype(BF16)
        vb[...] = v_ref[0, 0].astype(BF16)
        dob[...] = do_ref[0].astype(BF16)
        d_ref[...] = jnp.zeros_like(d_ref)
        row = lax.broadcasted_iota(jnp.int32, (blk, blk), 0)
        col = lax.broadcasted_iota(jnp.int32, (blk, blk), 1)
        tri_lt = col < row
        suffix = (row > col).astype(BF16)
        suffix_incl = (row >= col).astype(BF16)

        def q_block(i, _):
            rows_i = pl.ds(pl.multiple_of(i * blk, blk), blk)
            qi = qb[rows_i, :]
            doi = dob[rows_i, :]
            delta = jnp.sum(doi.astype(F32) * o_ref[0, rows_i, :], axis=1, keepdims=True)

            def k_block(state):
                jj, c, cg, dq = state
                j = i - jj
                rows = pl.ds(pl.multiple_of(j * blk, blk), blk)
                kj = kb[rows, :]
                vj = vb[rows, :]
                log_skip, log_beta, valid = _sb_scores(qi, kj, jj == 0, tri_lt)
                after = _dot_exact(log_skip, suffix)
                w = jnp.where(valid, jnp.exp(log_beta + after + c), 0.0)
                beta = jnp.exp(log_beta)
                wb = w.astype(BF16)
                g = _dot(doi, vj, NT) * wb.astype(F32)
                d_ref[2, 0, rows, :] += _dot(wb, doi, TN)
                before = delta - (_dot_exact(g, suffix_incl) + cg)
                dz = jnp.where(valid, g * (1.0 - beta) - before * beta, 0.0) * (HEAD_DIM ** -0.5)
                dzb = dz.astype(BF16)
                dq = dq + _dot(dzb, kj)
                d_ref[1, 0, rows, :] += _dot(dzb, qi, TN)
                c = c + jnp.sum(log_skip, axis=1, keepdims=True)
                cg = cg + jnp.sum(g, axis=1, keepdims=True)
                return jj + 1, c, cg, dq

            zero = jnp.zeros((blk, 1), F32)
            _, _, _, dq = lax.while_loop(functools.partial(_sb_keys_left, i), k_block,
                                         (jnp.int32(0), zero, zero, jnp.zeros((blk, HEAD_DIM), F32)))
            d_ref[0, 0, rows_i, :] = dq
            return 0

        lax.fori_loop(0, nq, q_block, 0)

        def norm_block(i, carry):
            gq, gk = carry
            rows = pl.ds(pl.multiple_of(i * blk, blk), blk)
            dq, pq = _rms_bwd(q_ref[0, 0, rows, :], qg_ref[...], d_ref[0, 0, rows, :])
            d_ref[0, 0, rows, :] = dq
            dk, pk = _rms_bwd(k_ref[0, 0, rows, :], kg_ref[...], d_ref[1, 0, rows, :])
            d_ref[1, 0, rows, :] = dk
            return gq + jnp.sum(pq, axis=0, keepdims=True), gk + jnp.sum(pk, axis=0, keepdims=True)

        zero = jnp.zeros((1, HEAD_DIM), F32)
        gq, gk = lax.fori_loop(0, nq, norm_block, (zero, zero))
        dqg_ref[0, 0] = gq
        dkg_ref[0, 0] = gk

    head_row = jax.ShapeDtypeStruct((b, N_HEADS, 1, HEAD_DIM), F32)
    return _pcall(
        body, name="sb_attn_bwd", grid=(b, N_HEADS),
        in_specs=[_head_spec(s, 0), _head_spec(s, 1), _head_spec(s, 2), _seq_spec(s), _seq_spec(s),
                  _GAIN_SPEC, _GAIN_SPEC, pl.BlockSpec(memory_space=pl.ANY)],
        out_specs=[pl.BlockSpec((3, 1, s, HEAD_DIM), lambda b, h: (0, b, 0, h)), _HEAD_ROW_SPEC, _HEAD_ROW_SPEC],
        out_shape=[jax.ShapeDtypeStruct(dproj4.shape, F32), head_row, head_row],
        scratch_shapes=[pltpu.VMEM((s, HEAD_DIM), BF16)] * 4,
        input_output_aliases={7: 0},
        compiler_params=_params("parallel", "parallel"),
    )(proj4, proj4, proj4, do3, o3, q_gain, k_gain, dproj4)


def hgrn2_bwd(proj4, don3, oraw3, states, lbl4, o_gain, dproj4):
    _, b, s, _ = proj4.shape
    nchunk = s // HG_CHUNK
    c = HG_CHUNK
    nsub = HG_CHUNK // HG_SUB

    def body(q_ref, f_ref, i_ref, don_ref, oraw_ref, st_ref, lbl_ref, og_ref, _alias, d_ref, dog_ref, dlb_ref):
        incl, lower, before_sub, upper = _hg_masks()
        lb, oml = _hg_lower_bound(lbl_ref)
        last_row = lax.broadcasted_iota(jnp.int32, (c, HEAD_DIM), 0) == c - 1

        def chunk(m, carry):
            dst, dog_acc, dlb_acc = carry
            n = nchunk - 1 - m
            rows = pl.ds(pl.multiple_of(n * c, c), c)
            qp = q_ref[0, 0, rows, :]
            v = _hg_chunk_fwd(qp, f_ref[0, 0, rows, :], lb, oml, lower, before_sub)
            inp = i_ref[0, 0, rows, :].astype(BF16)
            q, k, gc = v["q"], v["k"], v["gc"]
            e_q = jnp.exp(gc - v["gr"])
            a, qdb, ksb, es = _hg_intra(q * e_q, k, gc, v["gr"], incl)
            e_gc = jnp.exp(gc)
            gl = gc[c - 1:c, :]
            e_gl = jnp.exp(gl)
            e_k = jnp.exp(gl - gc)
            st = st_ref[0, 0, n]
            dstb = dst.astype(BF16)
            do, pg = _rms_bwd(oraw_ref[0, rows, :], og_ref[...], don_ref[0, rows, :])
            dog_acc = dog_acc + jnp.sum(pg, axis=0, keepdims=True)
            dob = do.astype(BF16)
            dab = jnp.where(incl, _dot(dob, inp, NT), 0.0).astype(BF16)
            di = _dot(a.astype(BF16), dob, TN) + _dot((k * e_k).astype(BF16), dstb, NT)
            dq_inter = _dot(dob, st.astype(BF16)) * e_gc
            dk_inter = _dot(inp, dstb) * e_k
            dqd, dk, dgc_k = [], dk_inter, jnp.zeros((c, HEAD_DIM), F32)
            for sub in range(nsub):
                sl = slice(sub * HG_SUB, (sub + 1) * HG_SUB)
                dqd.append(_dot(dab[sl, :], ksb[sub]))
                dks = _dot(dab[sl, :], qdb[sl, :], TN)
                dk = dk + dks * es[sub]
                dgc_k = dgc_k + dks * ksb[sub].astype(F32)
            dqd = jnp.concatenate(dqd, axis=0)
            dq = dqd * e_q + dq_inter
            at_last = (jnp.sum(k * dk_inter, axis=0, keepdims=True)
                       + e_gl * jnp.sum(st * dst, axis=0, keepdims=True))
            dgc = (qdb.astype(F32) * dqd - dgc_k) + (q * dq_inter - k * dk_inter) + jnp.where(last_row, at_last, 0.0)
            dlogf = _dot_exact(dgc, upper, left=True)
            sq, sf, sfn, f = v["sq"], v["sf"], v["sfn"], v["f"]
            dlf_f = dlogf / f
            d_ref[0, 0, rows, :] = dq * (sq * (1.0 + qp * (1.0 - sq)))
            d_ref[1, 0, rows, :] = (dlf_f - dk) * (oml * sf * sfn)
            d_ref[2, 0, rows, :] = di
            dlb_acc = dlb_acc + jnp.sum((dlf_f - dk) * sfn, axis=0, keepdims=True)
            return dst * e_gl + _dot(dob, (q * e_gc).astype(BF16), TN), dog_acc, dlb_acc

        zero = jnp.zeros((1, HEAD_DIM), F32)
        _, dog, dlb = _unrolled_loop(nchunk, chunk, (jnp.zeros((HEAD_DIM, HEAD_DIM), F32), zero, zero))
        dog_ref[0, 0] = dog
        dlb_ref[0, 0] = dlb

    head_row = jax.ShapeDtypeStruct((b, N_HEADS, 1, HEAD_DIM), F32)
    return _pcall(
        body, name="hgrn2_bwd", grid=(b, N_HEADS),
        in_specs=[_head_spec(s, 0), _head_spec(s, 1), _head_spec(s, 2), _seq_spec(s), _seq_spec(s),
                  _state_spec(nchunk), pl.BlockSpec((2, 1, 1, HEAD_DIM), lambda b, h: (0, h, 0, 0)), _GAIN_SPEC,
                  pl.BlockSpec(memory_space=pl.ANY)],
        out_specs=[pl.BlockSpec((3, 1, s, HEAD_DIM), lambda b, h: (0, b, 0, h)), _HEAD_ROW_SPEC, _HEAD_ROW_SPEC],
        out_shape=[jax.ShapeDtypeStruct(dproj4.shape, F32), head_row, head_row],
        input_output_aliases={8: 0},
        compiler_params=_params("parallel", "parallel"),
    )(proj4, proj4, proj4, don3, oraw3, states, lbl4, o_gain, dproj4)


def local_step(x, target, sb_norm, wsi, sb_q_gain, sb_k_gain, wso, hg_norm_full, whi, hg_o_gain, who, hg_lb_logits):
    b, s, _ = x.shape
    t = b * s
    x2 = x.reshape(t, D_MODEL)
    tg2 = target.reshape(t, D_MODEL)
    lbl4 = hg_lb_logits.reshape(2, N_HEADS, 1, HEAD_DIM)
    four = (4, b, s, D_MODEL)
    three = (b, s, D_MODEL)

    proj0, u0 = rms_inproj(x2, sb_norm, wsi, "sb_inproj")
    o0 = sb_attn_fwd(proj0.reshape(four), sb_q_gain, sb_k_gain).reshape(t, D_MODEL)
    h1 = gate_outproj(o0, proj0, wso, x2, None, "sb_outproj")
    proj1, u1 = rms_inproj(h1, hg_norm_full, whi, "hg_inproj")
    o1, o1_raw, states = hgrn2_fwd(proj1.reshape(four), lbl4, hg_o_gain)
    o1 = o1.reshape(t, D_MODEL)
    dh2, loss_parts = gate_outproj(o1, proj1, who, h1, tg2, "hg_outproj_loss")

    do1, dproj1, g_who = outproj_bwd(dh2, who, o1, proj1, "hg_outproj_bwd")
    dproj1, g_og, g_lb = hgrn2_bwd(proj1.reshape(four), do1.reshape(three), o1_raw, states, lbl4, hg_o_gain,
                                   dproj1.reshape(four))
    dproj1 = dproj1.reshape(4, t, D_MODEL)
    dh1, g_hgn = inproj_bwd_dx(dproj1, whi, h1, hg_norm_full, dh2, "hg_inproj_bwd_dx")
    g_whi = inproj_bwd_dw(u1, dproj1, "hg_inproj_bwd_dw")

    do0, dproj0, g_wso = outproj_bwd(dh1, wso, o0, proj0, "sb_outproj_bwd")
    dproj0, g_qg, g_kg = sb_attn_bwd(proj0.reshape(four), do0.reshape(three), o0.reshape(three),
                                     sb_q_gain, sb_k_gain, dproj0.reshape(four))
    dproj0 = dproj0.reshape(4, t, D_MODEL)
    gx, g_sbn = inproj_bwd_dx(dproj0, wsi, x2, sb_norm, dh1, "sb_inproj_bwd_dx")
    g_wsi = inproj_bwd_dw(u0, dproj0, "sb_inproj_bwd_dw")
    return dict(loss_parts=loss_parts, gx=gx.reshape(three), g_wsi=g_wsi, g_wso=g_wso, g_whi=g_whi, g_who=g_who,
                g_sbn=g_sbn, g_hgn=g_hgn, g_qg=g_qg, g_kg=g_kg, g_og=g_og, g_lb=g_lb)


MESH = pl.DeviceIdType.MESH
N_PEERS = N_DEV - 1
_ANY = pl.BlockSpec(memory_space=pl.ANY)
_VMEM = pl.BlockSpec(memory_space=pltpu.VMEM)


def _mesh_pos():
    return lax.axis_index("x"), lax.axis_index("y"), lax.axis_index("c")


def _linear(pos):
    return 4 * pos[0] + 2 * pos[1] + pos[2]


def _peer(pos, k):
    flips = ((k + 1) >> 2 & 1, (k + 1) >> 1 & 1, (k + 1) & 1)
    return tuple(1 - p if f else p for p, f in zip(pos, flips))


def _exchange(pairs, send_sems, recv_sems, local_sems, pos):
    me = _linear(pos)
    started = []
    for a, (src_of, dst) in enumerate(pairs):
        loc = pltpu.make_async_copy(src_of(me), dst.at[me], local_sems.at[a])
        loc.start()
        started.append(loc)
        for k in range(N_PEERS):
            peer = _peer(pos, k)
            pltpu.make_async_remote_copy(
                src_ref=src_of(_linear(peer)), dst_ref=dst.at[me], send_sem=send_sems.at[a, k],
                recv_sem=recv_sems.at[a, k], device_id=peer, device_id_type=MESH).start()
    for a, (src_of, dst) in enumerate(pairs):
        for k in range(N_PEERS):
            peer = _peer(pos, k)
            landed = pltpu.make_async_remote_copy(
                src_ref=src_of(_linear(peer)), dst_ref=dst.at[_linear(peer)], send_sem=send_sems.at[a, k],
                recv_sem=recv_sems.at[a, k], device_id=peer, device_id_type=MESH)
            landed.wait_recv()
            landed.wait_send()
    for loc in started:
        loc.wait()


def _exchange_sems(n):
    return [pltpu.SemaphoreType.DMA((n, N_PEERS)), pltpu.SemaphoreType.DMA((n, N_PEERS)),
            pltpu.SemaphoreType.DMA((n,))]


def all_gather_weights(w_si, w_so, w_hi, w_ho, hg_norm):
    shards = [w_si, w_so, w_hi, w_ho]

    def body(si_ref, so_ref, hi_ref, ho_ref, hn_ref, o_si, o_so, o_hi, o_ho, o_hn,
             b_si, b_so, b_hi, b_ho, b_hn, send_sems, recv_sems, local_sems):
        for src, buf in ((si_ref, b_si), (so_ref, b_so), (hi_ref, b_hi), (ho_ref, b_ho)):
            buf[...] = src[...].astype(BF16)
        b_hn[...] = jnp.broadcast_to(hn_ref[...], b_hn.shape)
        pairs = [((lambda p, buf=buf: buf), out) for buf, out in
                 ((b_si, o_si), (b_so, o_so), (b_hi, o_hi), (b_ho, o_ho), (b_hn, o_hn))]
        _exchange(pairs, send_sems, recv_sems, local_sems, _mesh_pos())

    out_shape = [jax.ShapeDtypeStruct((N_DEV,) + w.shape, BF16) for w in shards]
    out_shape.append(jax.ShapeDtypeStruct((N_DEV, 8, HEAD_DIM), F32))
    scratch = [pltpu.VMEM(w.shape, BF16) for w in shards] + [pltpu.VMEM((8, HEAD_DIM), F32)]
    return _pcall(
        body, name="all_gather_weights",
        in_specs=[_VMEM] * 5, out_specs=[_ANY] * 5, out_shape=out_shape,
        scratch_shapes=scratch + _exchange_sems(5),
        compiler_params=pltpu.CompilerParams(vmem_limit_bytes=VMEM_LIMIT_BYTES),
    )(w_si, w_so, w_hi, w_ho, hg_norm)


def exchange_grads(grads):
    def body(*refs):
        n = len(grads)
        ins, outs = refs[:n], refs[n:2 * n]
        send_sems, recv_sems, local_sems = refs[2 * n:]
        pairs = [((lambda p, g=g: g.at[p]), r) for g, r in zip(ins, outs)]
        _exchange(pairs, send_sems, recv_sems, local_sems, _mesh_pos())

    return _pcall(
        body, name="exchange_grads",
        in_specs=[_ANY] * len(grads), out_specs=[_ANY] * len(grads),
        out_shape=[jax.ShapeDtypeStruct(g.shape, F32) for g in grads],
        scratch_shapes=_exchange_sems(len(grads)),
    )(*grads)


def _adamw(w, g, m, v):
    m = ADAM_B1 * m + (1.0 - ADAM_B1) * g
    v = ADAM_B2 * v + (1.0 - ADAM_B2) * (g * g)
    m_hat = m / (1.0 - ADAM_B1 ** ADAM_STEP)
    v_hat = v / (1.0 - ADAM_B2 ** ADAM_STEP)
    delta = -ADAM_LR * (m_hat / (jnp.sqrt(v_hat) + ADAM_EPS) + ADAM_WD * w)
    return delta, m, v


def reduce_adamw(parts, w, m, v, name):
    _, r, c = parts.shape
    tr = _row_tile(r, 256)

    def body(p_ref, w_ref, m_ref, v_ref, g_ref, d_ref, m2_ref, v2_ref):
        g = p_ref[0]
        for dev in range(1, N_DEV):
            g = g + p_ref[dev]
        g_ref[...] = g
        d_ref[...], m2_ref[...], v2_ref[...] = _adamw(w_ref[...], g, m_ref[...], v_ref[...])

    tile = pl.BlockSpec((tr, c), lambda i: (i, 0))
    return _pcall(
        body, name=name, grid=(r // tr,),
        in_specs=[pl.BlockSpec((N_DEV, tr, c), lambda i: (0, i, 0)), tile, tile, tile],
        out_specs=[tile] * 4, out_shape=[jax.ShapeDtypeStruct((r, c), F32)] * 4,
        compiler_params=_params("parallel"),
    )(parts, w, m, v)


PACK_ROWS = 32
ROW_SBN, ROW_HGN, ROW_LB, ROW_QG, ROW_KG, ROW_OG, ROW_LOSS = 0, 8, 16, 24, 25, 26, 27


def small_update(g_sbn, g_hgn, g_lb, g_qg, g_kg, g_og, loss_parts, small):
    n_in = 7 + len(small)

    def body(*refs):
        sbn_ref, hgn_ref, lb_ref, qg_ref, kg_ref, og_ref, loss_ref = refs[:7]
        wmv = refs[7:n_in]
        outs = refs[n_in:n_in + 25]
        pack, gath, tot, send_sems, recv_sems, local_sems = refs[n_in + 25:]
        pos = _mesh_pos()
        me = _linear(pos)
        pack[...] = jnp.zeros_like(pack)
        pack[ROW_SBN:ROW_SBN + 8, :] = sbn_ref[...]
        pack[ROW_HGN:ROW_HGN + 8, :] = hgn_ref[...]
        pack[ROW_LB:ROW_LB + 8, :] = jnp.sum(lb_ref[...], axis=0)
        pack[ROW_QG:ROW_QG + 1, :] = jnp.sum(qg_ref[...], axis=0, keepdims=True)
        pack[ROW_KG:ROW_KG + 1, :] = jnp.sum(kg_ref[...], axis=0, keepdims=True)
        pack[ROW_OG:ROW_OG + 1, :] = jnp.sum(og_ref[...], axis=0, keepdims=True)
        pack[ROW_LOSS:ROW_LOSS + 1, :] = jnp.sum(loss_ref[...], axis=0)[0:1, :]
        _exchange([((lambda p: pack), gath)], send_sems, recv_sems, local_sems, pos)
        total = gath[0]
        for dev in range(1, N_DEV):
            total = total + gath[dev]
        tot[...] = total
        outs[0][...] = jnp.broadcast_to(tot[ROW_LOSS:ROW_LOSS + 1, :], (8, HEAD_DIM))
        l0 = wmv[15][0:8, :]
        l1 = wmv[15][8:16, :]
        p1, p0 = _sigmoid_pair(l1 - l0)
        d_l1 = p0 * p1 * tot[ROW_LB:ROW_LB + 8, :]
        grads = [tot[ROW_SBN:ROW_SBN + 8, :], tot[ROW_QG:ROW_QG + 1, :], tot[ROW_KG:ROW_KG + 1, :],
                 tot[pl.ds(ROW_HGN + me, 1), :], tot[ROW_OG:ROW_OG + 1, :],
                 jnp.concatenate([-d_l1, d_l1], axis=0)]
        for i, g in enumerate(grads):
            w_ref, m_ref, v_ref = wmv[3 * i:3 * i + 3]
            o = outs[1 + 4 * i:5 + 4 * i]
            o[0][...] = g
            o[1][...], o[2][...], o[3][...] = _adamw(w_ref[...], g, m_ref[...], v_ref[...])

    out_shape = [jax.ShapeDtypeStruct((8, HEAD_DIM), F32)]
    for i in range(6):
        out_shape += [jax.ShapeDtypeStruct(small[3 * i].shape, F32)] * 4
    return _pcall(
        body, name="small_update",
        in_specs=[_VMEM] * n_in, out_specs=[_VMEM] * 25, out_shape=out_shape,
        scratch_shapes=[pltpu.VMEM((PACK_ROWS, HEAD_DIM), F32), pltpu.VMEM((N_DEV, PACK_ROWS, HEAD_DIM), F32),
                        pltpu.VMEM((PACK_ROWS, HEAD_DIM), F32)] + _exchange_sems(1),
    )(g_sbn, g_hgn, g_lb, g_qg, g_kg, g_og, loss_parts, *small)


def kernel(x, sb_norm, sb_w_in, sb_q_gain, sb_k_gain, sb_w_out, hg_norm, hg_w_in, hg_o_gain, hg_w_out, hg_lb_logits, loss_target, m_sb_norm, m_sb_w_in, m_sb_q_gain, m_sb_k_gain, m_sb_w_out, m_hg_norm, m_hg_w_in, m_hg_o_gain, m_hg_w_out, m_hg_lb_logits, v_sb_norm, v_sb_w_in, v_sb_q_gain, v_sb_k_gain, v_sb_w_out, v_hg_norm, v_hg_w_in, v_hg_o_gain, v_hg_w_out, v_hg_lb_logits):
    b = x.shape[0]
    wsi, wso, whi, who, hgn = all_gather_weights(sb_w_in[0], sb_w_out[0], hg_w_in[0], hg_w_out[0], hg_norm)
    hg_norm_full = hgn[:, 0, :].reshape(1, D_MODEL)
    r = local_step(x, loss_target, sb_norm, wsi, sb_q_gain, sb_k_gain, wso.reshape(D_MODEL, D_MODEL),
                   hg_norm_full, whi, hg_o_gain, who.reshape(D_MODEL, D_MODEL), hg_lb_logits)

    parts = exchange_grads([r["g_wsi"], r["g_wso"].reshape(N_DEV, W_ROWS, D_MODEL),
                            r["g_whi"], r["g_who"].reshape(N_DEV, W_ROWS, D_MODEL)])
    big = {}
    for name, p, w, m, v in (("sb_w_in", parts[0], sb_w_in, m_sb_w_in, v_sb_w_in),
                             ("sb_w_out", parts[1], sb_w_out, m_sb_w_out, v_sb_w_out),
                             ("hg_w_in", parts[2], hg_w_in, m_hg_w_in, v_hg_w_in),
                             ("hg_w_out", parts[3], hg_w_out, m_hg_w_out, v_hg_w_out)):
        big[name] = [o[None] for o in reduce_adamw(p, w[0], m[0], v[0], "adamw_" + name)]

    def rows8(a):
        return a.reshape(8, HEAD_DIM)

    def rows16(a):
        return a.reshape(16, HEAD_DIM)

    small_in = [rows8(sb_norm), rows8(m_sb_norm), rows8(v_sb_norm),
                sb_q_gain, m_sb_q_gain, v_sb_q_gain,
                sb_k_gain, m_sb_k_gain, v_sb_k_gain,
                hg_norm, m_hg_norm, v_hg_norm,
                hg_o_gain, m_hg_o_gain, v_hg_o_gain,
                rows16(hg_lb_logits), rows16(m_hg_lb_logits), rows16(v_hg_lb_logits)]
    so = small_update(rows8(r["g_sbn"]), rows8(r["g_hgn"]), r["g_lb"].reshape(b, N_HEADS, HEAD_DIM),
                      r["g_qg"].reshape(b * N_HEADS, HEAD_DIM), r["g_kg"].reshape(b * N_HEADS, HEAD_DIM),
                      r["g_og"].reshape(b * N_HEADS, HEAD_DIM), r["loss_parts"], small_in)
    loss = so[0][0, 0]
    shapes = {"sb_norm": (1, D_MODEL), "sb_q_gain": (1, HEAD_DIM), "sb_k_gain": (1, HEAD_DIM),
              "hg_norm": (1, HEAD_DIM), "hg_o_gain": (1, HEAD_DIM), "hg_lb_logits": (2, D_MODEL)}
    small = {}
    for i, name in enumerate(("sb_norm", "sb_q_gain", "sb_k_gain", "hg_norm", "hg_o_gain", "hg_lb_logits")):
        small[name] = [o.reshape(shapes[name]) for o in so[1 + 4 * i:5 + 4 * i]]
    order = ("sb_norm", "sb_w_in", "sb_q_gain", "sb_k_gain", "sb_w_out",
             "hg_norm", "hg_w_in", "hg_o_gain", "hg_w_out", "hg_lb_logits")
    res = {**big, **small}
    return (loss, r["gx"]) + tuple(res[n][j] for j in range(4) for n in order)
```

```python
import functools

import jax
import jax.numpy as jnp
from jax import lax
from jax.experimental import pallas as pl
from jax.experimental.pallas import tpu as pltpu

F32 = jnp.float32
BF16 = jnp.bfloat16

N_DEV = 8
D_MODEL = 1024
N_HEADS = 8
HEAD_DIM = 128
RMS_EPS = 1e-6
ATTN_BLOCK = 128
HG_CHUNK = 64
HG_SUB = 16
HG_UNROLL = 8
EXP_CLAMP = 80.0
SB_HEADS_PER_STEP = 2
SB_QBLOCKS_PER_STEP = 4
SB_GROUP_COLS = SB_HEADS_PER_STEP * 128
SB_LOG_WEIGHT_FLOOR = -104.0
VMEM_LIMIT_BYTES = 48 * 1024 * 1024
W_COLS = 4 * D_MODEL // N_DEV
W_ROWS = D_MODEL // N_DEV

ADAM_LR = 0.001
ADAM_B1 = 0.9
ADAM_B2 = 0.999
ADAM_EPS = 1e-08
ADAM_WD = 0.01
ADAM_STEP = 10

NT = (((1,), (1,)), ((), ()))
TN = (((0,), (0,)), ((), ()))
NN = (((1,), (0,)), ((), ()))


def _pcall(body, *, name, **kw):
    return pl.pallas_call(body, name=name, **kw)


def _params(*sem):
    return pltpu.CompilerParams(dimension_semantics=sem, vmem_limit_bytes=VMEM_LIMIT_BYTES)


def _dot(a, b, dims=NN):
    return lax.dot_general(a, b, dims, preferred_element_type=F32)


def _dot_exact(a, m, dims=NN, left=False):
    hi = a.astype(BF16)
    lo = (a - hi.astype(F32)).astype(BF16)
    if left:
        return _dot(m, hi, dims) + _dot(m, lo, dims)
    return _dot(hi, m, dims) + _dot(lo, m, dims)


def _split(a):
    hi = a.astype(BF16)
    return hi, (a - hi.astype(F32)).astype(BF16)


def _dot3(a, b, dims=NN):
    return _dot(a[0], b[0], dims) + (_dot(a[0], b[1], dims) + _dot(a[1], b[0], dims))


def _sigmoid(x):
    return 1.0 / (1.0 + jnp.exp(-x))


def _sigmoid_pair(x):
    e = jnp.exp(-jnp.abs(x))
    big = 1.0 / (1.0 + e)
    small = e * big
    pos = x >= 0
    return jnp.where(pos, big, small), jnp.where(pos, small, big)


def _rms_scale(x):
    return lax.rsqrt(jnp.mean(x * x, axis=-1, keepdims=True) + RMS_EPS)


def _row_tile(t, want):
    return want if t % want == 0 else t


def rms_inproj(x2, gain, wg, name):
    t = x2.shape[0]
    tm = _row_tile(t, 512)

    def body(x_ref, g_ref, w_ref, proj_ref, u_ref):
        @pl.when(pl.program_id(1) == 0)
        def _():
            x = x_ref[...]
            u_ref[...] = (x * _rms_scale(x) * g_ref[...]).astype(BF16)

        proj_ref[0] = _dot(u_ref[...], w_ref[0])

    return _pcall(
        body, name=name,
        grid=(t // tm, N_DEV),
        in_specs=[pl.BlockSpec((tm, D_MODEL), lambda i, j: (i, 0)),
                  pl.BlockSpec((1, D_MODEL), lambda i, j: (0, 0)),
                  pl.BlockSpec((1, D_MODEL, W_COLS), lambda i, j: (j, 0, 0))],
        out_specs=[pl.BlockSpec((1, tm, W_COLS), lambda i, j: (j // 2, i, j % 2)),
                   pl.BlockSpec((tm, D_MODEL), lambda i, j: (i, 0))],
        out_shape=[jax.ShapeDtypeStruct((4, t, D_MODEL), F32),
                   jax.ShapeDtypeStruct((t, D_MODEL), BF16)],
        compiler_params=_params("parallel", "arbitrary"),
    )(x2, gain, wg)


def gate_outproj(o2, proj, w_out, resid, target, name):
    t = o2.shape[0]
    tm = _row_tile(t, 256)
    with_loss = target is not None

    def body(o_ref, gate_ref, w_ref, r_ref, *rest):
        g = gate_ref[0]
        og = (o_ref[...] * (g * _sigmoid(g))).astype(BF16)
        h = r_ref[...] + _dot(og, w_ref[...])
        if with_loss:
            t_ref, dh_ref, loss_ref = rest
            err = h - t_ref[...]
            dh_ref[...] = err * (1.0 / D_MODEL)
            part = 0.5 * jnp.sum(jnp.mean(err * err, axis=-1, keepdims=True))
            loss_ref[...] = jnp.full(loss_ref.shape, part, F32)
        else:
            (h_ref,) = rest
            h_ref[...] = h

    row = pl.BlockSpec((tm, D_MODEL), lambda i: (i, 0))
    in_specs = [row,
                pl.BlockSpec((1, tm, D_MODEL), lambda i: (3, i, 0)),
                pl.BlockSpec((D_MODEL, D_MODEL), lambda i: (0, 0)),
                row]
    args = [o2, proj, w_out, resid]
    if with_loss:
        in_specs.append(row)
        args.append(target)
        out_specs = [row, pl.BlockSpec((1, 8, 128), lambda i: (i, 0, 0))]
        out_shape = [jax.ShapeDtypeStruct((t, D_MODEL), F32),
                     jax.ShapeDtypeStruct((t // tm, 8, 128), F32)]
    else:
        out_specs = row
        out_shape = jax.ShapeDtypeStruct((t, D_MODEL), F32)
    return _pcall(body, name=name, grid=(t // tm,), in_specs=in_specs, out_specs=out_specs,
                  out_shape=out_shape, compiler_params=_params("parallel"))(*args)


def _head_spec(s, part):
    return pl.BlockSpec((1, 1, s, HEAD_DIM), lambda b, h: (part, b, 0, h))


def _seq_spec(s):
    return pl.BlockSpec((1, s, HEAD_DIM), lambda b, h: (b, 0, h))


_GAIN_SPEC = pl.BlockSpec((1, HEAD_DIM), lambda b, h: (0, 0))
_HEAD_ROW_SPEC = pl.BlockSpec((1, 1, 1, HEAD_DIM), lambda b, h: (b, h, 0, 0))


def _sb_group_spec(s, part):
    return pl.BlockSpec((1, 1, s, SB_GROUP_COLS), lambda b, g: (part, b, 0, g))


def _sb_seq_group_spec(s):
    return pl.BlockSpec((1, s, SB_GROUP_COLS), lambda b, g: (b, 0, g))


_SB_GROUP_ROW_SPEC = pl.BlockSpec((1, SB_HEADS_PER_STEP, 1, HEAD_DIM), lambda b, g: (b, g, 0, 0))


def _sb_chains(m):
    return [(h, m * SB_QBLOCKS_PER_STEP + r) for h in range(SB_HEADS_PER_STEP) for r in range(SB_QBLOCKS_PER_STEP)]


def _sb_logits(qi, kj):
    return _dot(qi, kj, NT) * (HEAD_DIM ** -0.5)


def _sb_scores(z, diag, live, tri_lt):
    soft = jnp.log1p(jnp.exp(-jnp.abs(z)))
    valid = jnp.logical_and(live, jnp.logical_or(jnp.logical_not(diag), tri_lt))
    log_skip = jnp.where(valid, -(jnp.maximum(z, 0.0) + soft), 0.0)
    log_beta = jnp.minimum(z, 0.0) - soft
    return log_skip, log_beta, valid


def _sb_keys_left(chains, state):
    done, carries = state[0], state[1]
    worst = None
    for (_, i), c in zip(chains, carries):
        c = jnp.where(done <= i, c, -jnp.inf)
        worst = c if worst is None else jnp.maximum(worst, c)
    return jnp.logical_and(done <= chains[-1][1],
                           jnp.logical_or(done == 0, jnp.max(worst) > SB_LOG_WEIGHT_FLOOR))


def _sb_key_rows(i, done):
    j = i - done
    return pl.ds(pl.multiple_of(jnp.maximum(j, 0) * ATTN_BLOCK, ATTN_BLOCK), ATTN_BLOCK), j >= 0


def _sb_load_normed(src_ref, gain_ref, dst):
    for h in range(SB_HEADS_PER_STEP):
        x = src_ref[0, 0, :, h * HEAD_DIM:(h + 1) * HEAD_DIM]
        dst[h] = (x * _rms_scale(x) * gain_ref[...]).astype(BF16)


def sb_attn_fwd(proj4, q_gain, k_gain):
    _, b, s, _ = proj4.shape
    blk = ATTN_BLOCK
    nq = s // blk

    def body(q_ref, k_ref, v_ref, qg_ref, kg_ref, o_ref, qb, kb, vb):
        _sb_load_normed(q_ref, qg_ref, qb)
        _sb_load_normed(k_ref, kg_ref, kb)
        for h in range(SB_HEADS_PER_STEP):
            vb[h] = v_ref[0, 0, :, h * HEAD_DIM:(h + 1) * HEAD_DIM].astype(BF16)
        row = lax.broadcasted_iota(jnp.int32, (blk, blk), 0)
        col = lax.broadcasted_iota(jnp.int32, (blk, blk), 1)
        tri_lt = col < row
        suffix = (row > col).astype(BF16)

        def q_group(m, _):
            chains = _sb_chains(m)
            qis = [qb[h, pl.ds(pl.multiple_of(i * blk, blk), blk), :] for h, i in chains]

            def k_step(state):
                done, cs, accs = state
                where = [_sb_key_rows(i, done) for _, i in chains]
                zs = [_sb_logits(qi, kb[h, rows, :]) for (h, _), qi, (rows, _) in zip(chains, qis, where)]
                scored = [_sb_scores(z, done == 0, live, tri_lt) for z, (_, live) in zip(zs, where)]
                afters = [_dot_exact(log_skip, suffix) for log_skip, _, _ in scored]
                ws = [jnp.where(valid, jnp.exp(log_beta + after + c), 0.0).astype(BF16)
                      for (_, log_beta, valid), after, c in zip(scored, afters, cs)]
                new_accs = [acc + _dot(w, vb[h, rows, :]) for (h, _), (rows, _), w, acc in zip(chains, where, ws, accs)]
                new_cs = [c + jnp.sum(log_skip, axis=1, keepdims=True) for (log_skip, _, _), c in zip(scored, cs)]
                return done + 1, tuple(new_cs), tuple(new_accs)

            n = len(chains)
            _, _, accs = lax.while_loop(
                functools.partial(_sb_keys_left, chains), k_step,
                (jnp.int32(0), (jnp.zeros((blk, 1), F32),) * n, (jnp.zeros((blk, HEAD_DIM), F32),) * n))
            for (h, i), acc in zip(chains, accs):
                o_ref[0, pl.ds(pl.multiple_of(i * blk, blk), blk), h * HEAD_DIM:(h + 1) * HEAD_DIM] = acc
            return 0

        lax.fori_loop(0, nq // SB_QBLOCKS_PER_STEP, q_group, 0)

    return _pcall(
        body, name="sb_attn_fwd", grid=(b, N_HEADS // SB_HEADS_PER_STEP),
        in_specs=[_sb_group_spec(s, 0), _sb_group_spec(s, 1), _sb_group_spec(s, 2), _GAIN_SPEC, _GAIN_SPEC],
        out_specs=_sb_seq_group_spec(s),
        out_shape=jax.ShapeDtypeStruct((b, s, D_MODEL), F32),
        scratch_shapes=[pltpu.VMEM((SB_HEADS_PER_STEP, s, HEAD_DIM), BF16)] * 3,
        compiler_params=_params("parallel", "parallel"),
    )(proj4, proj4, proj4, q_gain, k_gain)


def _hg_masks():
    c = HG_CHUNK
    row = lax.broadcasted_iota(jnp.int32, (c, c), 0)
    col = lax.broadcasted_iota(jnp.int32, (c, c), 1)
    incl = (col <= row)
    lower = incl.astype(BF16)
    before_sub = (col < (row // HG_SUB) * HG_SUB).astype(BF16)
    upper = (col >= row).astype(BF16)
    return incl, lower, before_sub, upper


def _hg_lower_bound(lbl_ref):
    l0 = lbl_ref[0, 0]
    l1 = lbl_ref[1, 0]
    d = l1 - l0
    return _sigmoid_pair(d)


def _hg_gates(qp, fp, lb, oml):
    sq = _sigmoid(qp)
    sf, sfn = _sigmoid_pair(fp)
    f = lb + oml * sf
    return dict(qp=qp, sq=sq, q=qp * sq, sf=sf, sfn=sfn, f=f, k=oml * sfn, logf=jnp.log(f))


def _hg_intra(qds, ks, gcs, grs, incl):
    subs = range(HG_CHUNK // HG_SUB)
    qdbs = [qd.astype(BF16) for qd in qds]
    ess = [[jnp.exp(jnp.minimum(gr[sub * HG_SUB:sub * HG_SUB + 1, :] - gc, EXP_CLAMP)) for sub in subs]
           for gc, gr in zip(gcs, grs)]
    ksbs = [[(k * e).astype(BF16) for e in es] for k, es in zip(ks, ess)]
    rows = [[_dot(qdb[sub * HG_SUB:(sub + 1) * HG_SUB, :], ksb[sub], NT) for sub in subs]
            for qdb, ksb in zip(qdbs, ksbs)]
    a_s = [jnp.where(incl, jnp.concatenate(r, axis=0), 0.0) for r in rows]
    return a_s, qdbs, ksbs, ess


def _hg_group_rows(outer):
    ns = [outer * HG_UNROLL + u for u in range(HG_UNROLL)]
    return ns, [pl.ds(pl.multiple_of(n * HG_CHUNK, HG_CHUNK), HG_CHUNK) for n in ns]


def _state_spec(nchunk):
    return pl.BlockSpec((1, 1, nchunk, HEAD_DIM, HEAD_DIM), lambda b, h: (b, h, 0, 0, 0))


def hgrn2_fwd(proj4, lbl4, o_gain):
    _, b, s, _ = proj4.shape
    nchunk = s // HG_CHUNK
    c = HG_CHUNK
    assert nchunk % HG_UNROLL == 0

    def body(q_ref, f_ref, i_ref, lbl_ref, og_ref, o_ref, oraw_ref, st_ref):
        incl, lower, before_sub, _ = _hg_masks()
        lb, oml = _hg_lower_bound(lbl_ref)

        def group(outer, st):
            ns, rows = _hg_group_rows(outer)
            vs = [_hg_gates(q_ref[0, 0, r, :], f_ref[0, 0, r, :], lb, oml) for r in rows]
            inps = [i_ref[0, 0, r, :].astype(BF16) for r in rows]
            gcs = [_dot_exact(v["logf"], lower, left=True) for v in vs]
            grs = [_dot_exact(v["logf"], before_sub, left=True) for v in vs]
            a_s, _, _, _ = _hg_intra([v["q"] * jnp.exp(gc - gr) for v, gc, gr in zip(vs, gcs, grs)],
                                     [v["k"] for v in vs], gcs, grs, incl)
            gls = [gc[c - 1:c, :] for gc in gcs]
            adds = [_dot(inp, (v["k"] * jnp.exp(gl - gc)).astype(BF16), TN)
                    for inp, v, gl, gc in zip(inps, vs, gls, gcs)]
            o_intra = [_dot(a.astype(BF16), inp) for a, inp in zip(a_s, inps)]
            sts = []
            for gl, add in zip(gls, adds):
                sts.append(st)
                st = st * jnp.exp(gl) + add
            outs = [oi + _dot((v["q"] * jnp.exp(gc)).astype(BF16), s0.astype(BF16), NT)
                    for oi, v, gc, s0 in zip(o_intra, vs, gcs, sts)]
            for n, r, s0, o in zip(ns, rows, sts, outs):
                st_ref[0, 0, n] = s0
                oraw_ref[0, r, :] = o
                o_ref[0, r, :] = o * _rms_scale(o) * og_ref[...]
            return st

        lax.fori_loop(0, nchunk // HG_UNROLL, group, jnp.zeros((HEAD_DIM, HEAD_DIM), F32))

    seq = jax.ShapeDtypeStruct((b, s, D_MODEL), F32)
    return _pcall(
        body, name="hgrn2_fwd", grid=(b, N_HEADS),
        in_specs=[_head_spec(s, 0), _head_spec(s, 1), _head_spec(s, 2),
                  pl.BlockSpec((2, 1, 1, HEAD_DIM), lambda b, h: (0, h, 0, 0)), _GAIN_SPEC],
        out_specs=[_seq_spec(s), _seq_spec(s), _state_spec(nchunk)],
        out_shape=[seq, seq, jax.ShapeDtypeStruct((b, N_HEADS, nchunk, HEAD_DIM, HEAD_DIM), F32)],
        compiler_params=_params("parallel", "parallel"),
    )(proj4, proj4, proj4, lbl4, o_gain)


def outproj_bwd(dh, w_out, o2, proj, name):
    t = dh.shape[0]
    tm = _row_tile(t, 256)

    def body(dh_ref, w_ref, o_ref, gate_ref, do_ref, dproj_ref, dw_ref):
        dhb = dh_ref[...].astype(BF16)
        dog = _dot(dhb, w_ref[...], NT)
        g = gate_ref[0]
        sg = _sigmoid(g)
        silu = g * sg
        o = o_ref[...]
        do_ref[...] = dog * silu
        dproj_ref[0] = dog * o * (sg * (1.0 + g * (1.0 - sg)))
        part = _dot((o * silu).astype(BF16), dhb, TN)

        @pl.when(pl.program_id(0) == 0)
        def _():
            dw_ref[...] = part

        @pl.when(pl.program_id(0) > 0)
        def _():
            dw_ref[...] += part

    row = pl.BlockSpec((tm, D_MODEL), lambda i: (i, 0))
    full = pl.BlockSpec((D_MODEL, D_MODEL), lambda i: (0, 0))
    return _pcall(
        body, name=name, grid=(t // tm,),
        in_specs=[row, full, row, pl.BlockSpec((1, tm, D_MODEL), lambda i: (3, i, 0))],
        out_specs=[row, pl.BlockSpec((1, tm, D_MODEL), lambda i: (3, i, 0)), full],
        out_shape=[jax.ShapeDtypeStruct((t, D_MODEL), F32),
                   jax.ShapeDtypeStruct((4, t, D_MODEL), F32),
                   jax.ShapeDtypeStruct((D_MODEL, D_MODEL), F32)],
        compiler_params=_params("arbitrary"),
    )(dh, w_out, o2, proj)


def inproj_bwd_dx(dproj, wg, x2, gain, dres, name):
    t = x2.shape[0]
    tm = _row_tile(t, 256)

    def body(d_ref, w_ref, x_ref, g_ref, r_ref, dx_ref, dg_ref):
        du = jnp.zeros((tm, D_MODEL), F32)
        for p in range(N_DEV):
            cols = slice((p % 2) * W_COLS, (p % 2 + 1) * W_COLS)
            du = du + _dot(d_ref[p // 2, :, cols].astype(BF16), w_ref[p], NT)
        x = x_ref[...]
        r = _rms_scale(x)
        xh = x * r
        a = du * g_ref[...]
        dx_ref[...] = r_ref[...] + r * (a - xh * jnp.mean(a * xh, axis=-1, keepdims=True))
        part = jnp.sum(du * xh, axis=0, keepdims=True)

        @pl.when(pl.program_id(0) == 0)
        def _():
            dg_ref[...] = part

        @pl.when(pl.program_id(0) > 0)
        def _():
            dg_ref[...] += part

    row = pl.BlockSpec((tm, D_MODEL), lambda i: (i, 0))
    vec = pl.BlockSpec((1, D_MODEL), lambda i: (0, 0))
    return _pcall(
        body, name=name, grid=(t // tm,),
        in_specs=[pl.BlockSpec((4, tm, D_MODEL), lambda i: (0, i, 0)),
                  pl.BlockSpec((N_DEV, D_MODEL, W_COLS), lambda i: (0, 0, 0)),
                  row, vec, row],
        out_specs=[row, vec],
        out_shape=[jax.ShapeDtypeStruct((t, D_MODEL), F32), jax.ShapeDtypeStruct((1, D_MODEL), F32)],
        compiler_params=_params("arbitrary"),
    )(dproj, wg, x2, gain, dres)


def inproj_bwd_dw(u, dproj, name):
    t = u.shape[0]
    tm = _row_tile(t, 512)

    def body(u_ref, d_ref, dw_ref):
        part = _dot(u_ref[...], d_ref[0].astype(BF16), TN)

        @pl.when(pl.program_id(1) == 0)
        def _():
            dw_ref[0] = part

        @pl.when(pl.program_id(1) > 0)
        def _():
            dw_ref[0] += part

    return _pcall(
        body, name=name, grid=(N_DEV, t // tm),
        in_specs=[pl.BlockSpec((tm, D_MODEL), lambda j, i: (i, 0)),
                  pl.BlockSpec((1, tm, W_COLS), lambda j, i: (j // 2, i, j % 2))],
        out_specs=pl.BlockSpec((1, D_MODEL, W_COLS), lambda j, i: (j, 0, 0)),
        out_shape=jax.ShapeDtypeStruct((N_DEV, D_MODEL, W_COLS), F32),
        compiler_params=_params("parallel", "arbitrary"),
    )(u, dproj)


def _rms_bwd(x, gain, dy):
    r = _rms_scale(x)
    xh = x * r
    a = dy * gain
    return r * (a - xh * jnp.mean(a * xh, axis=-1, keepdims=True)), dy * xh


def sb_attn_bwd(proj4, do3, o3, q_gain, k_gain, dproj4):
    _, b, s, _ = proj4.shape
    blk = ATTN_BLOCK
    nq = s // blk

    def body(q_ref, k_ref, v_ref, do_ref, o_ref, qg_ref, kg_ref, _alias, d_ref, dqg_ref, dkg_ref, qb, kb, vb, dob):
        _sb_load_normed(q_ref, qg_ref, qb)
        _sb_load_normed(k_ref, kg_ref, kb)
        for h in range(SB_HEADS_PER_STEP):
            cols = slice(h * HEAD_DIM, (h + 1) * HEAD_DIM)
            vb[h] = v_ref[0, 0, :, cols].astype(BF16)
            dob[h] = do_ref[0, :, cols].astype(BF16)
        d_ref[...] = jnp.zeros_like(d_ref)
        row = lax.broadcasted_iota(jnp.int32, (blk, blk), 0)
        col = lax.broadcasted_iota(jnp.int32, (blk, blk), 1)
        tri_lt = col < row
        suffix = (row > col).astype(BF16)
        suffix_incl = (row >= col).astype(BF16)

        def q_group(m, _):
            chains = _sb_chains(m)
            qis, dois, deltas = [], [], []
            for h, i in chains:
                rows_i = pl.ds(pl.multiple_of(i * blk, blk), blk)
                qis.append(qb[h, rows_i, :])
                dois.append(dob[h, rows_i, :])
                deltas.append(jnp.sum(dois[-1].astype(F32) * o_ref[0, rows_i, h * HEAD_DIM:(h + 1) * HEAD_DIM],
                                      axis=1, keepdims=True))

            def k_step(state):
                done, cs, cgs, dqs = state
                where = [_sb_key_rows(i, done) for _, i in chains]
                kjs = [kb[h, rows, :] for (h, _), (rows, _) in zip(chains, where)]
                zs = [_sb_logits(qi, kj) for qi, kj in zip(qis, kjs)]
                dws = [_dot(doi, vb[h, rows, :], NT) for (h, _), (rows, _), doi in zip(chains, where, dois)]
                scored = [_sb_scores(z, done == 0, live, tri_lt) for z, (_, live) in zip(zs, where)]
                afters = [_dot_exact(log_skip, suffix) for log_skip, _, _ in scored]
                wbs = [jnp.where(valid, jnp.exp(log_beta + after + c), 0.0).astype(BF16)
                       for (_, log_beta, valid), after, c in zip(scored, afters, cs)]
                gs = [dw * wb.astype(F32) for dw, wb in zip(dws, wbs)]
                befores = [delta - (_dot_exact(g, suffix_incl) + cg) for g, delta, cg in zip(gs, deltas, cgs)]
                dzbs = []
                for (_, log_beta, valid), g, before in zip(scored, gs, befores):
                    beta = jnp.exp(log_beta)
                    dz = jnp.where(valid, g * (1.0 - beta) - before * beta, 0.0) * (HEAD_DIM ** -0.5)
                    dzbs.append(dz.astype(BF16))
                new_dqs = [dq + _dot(dzb, kj) for dq, dzb, kj in zip(dqs, dzbs, kjs)]
                for (h, _), (rows, _), wb, doi, dzb, qi in zip(chains, where, wbs, dois, dzbs, qis):
                    cols = slice(h * HEAD_DIM, (h + 1) * HEAD_DIM)
                    d_ref[2, 0, rows, cols] += _dot(wb, doi, TN)
                    d_ref[1, 0, rows, cols] += _dot(dzb, qi, TN)
                new_cs = [c + jnp.sum(log_skip, axis=1, keepdims=True) for (log_skip, _, _), c in zip(scored, cs)]
                new_cgs = [cg + jnp.sum(g, axis=1, keepdims=True) for g, cg in zip(gs, cgs)]
                return done + 1, tuple(new_cs), tuple(new_cgs), tuple(new_dqs)

            n = len(chains)
            zero = (jnp.zeros((blk, 1), F32),) * n
            _, _, _, dqs = lax.while_loop(functools.partial(_sb_keys_left, chains), k_step,
                                          (jnp.int32(0), zero, zero, (jnp.zeros((blk, HEAD_DIM), F32),) * n))
            for (h, i), dq in zip(chains, dqs):
                d_ref[0, 0, pl.ds(pl.multiple_of(i * blk, blk), blk), h * HEAD_DIM:(h + 1) * HEAD_DIM] = dq
            return 0

        lax.fori_loop(0, nq // SB_QBLOCKS_PER_STEP, q_group, 0)

        def norm_block(i, carry):
            rows = pl.ds(pl.multiple_of(i * blk, blk), blk)
            out = []
            for h in range(SB_HEADS_PER_STEP):
                cols = slice(h * HEAD_DIM, (h + 1) * HEAD_DIM)
                for part, src_ref, gain_ref in ((0, q_ref, qg_ref), (1, k_ref, kg_ref)):
                    dx, pg = _rms_bwd(src_ref[0, 0, rows, cols], gain_ref[...], d_ref[part, 0, rows, cols])
                    d_ref[part, 0, rows, cols] = dx
                    out.append(carry[len(out)] + jnp.sum(pg, axis=0, keepdims=True))
            return tuple(out)

        sums = lax.fori_loop(0, nq, norm_block, (jnp.zeros((1, HEAD_DIM), F32),) * (2 * SB_HEADS_PER_STEP))
        for h in range(SB_HEADS_PER_STEP):
            dqg_ref[0, h] = sums[2 * h]
            dkg_ref[0, h] = sums[2 * h + 1]

    head_row = jax.ShapeDtypeStruct((b, N_HEADS, 1, HEAD_DIM), F32)
    return _pcall(
        body, name="sb_attn_bwd", grid=(b, N_HEADS // SB_HEADS_PER_STEP),
        in_specs=[_sb_group_spec(s, 0), _sb_group_spec(s, 1), _sb_group_spec(s, 2),
                  _sb_seq_group_spec(s), _sb_seq_group_spec(s), _GAIN_SPEC, _GAIN_SPEC,
                  pl.BlockSpec(memory_space=pl.ANY)],
        out_specs=[pl.BlockSpec((3, 1, s, SB_GROUP_COLS), lambda b, g: (0, b, 0, g)),
                   _SB_GROUP_ROW_SPEC, _SB_GROUP_ROW_SPEC],
        out_shape=[jax.ShapeDtypeStruct(dproj4.shape, F32), head_row, head_row],
        scratch_shapes=[pltpu.VMEM((SB_HEADS_PER_STEP, s, HEAD_DIM), BF16)] * 4,
        input_output_aliases={7: 0},
        compiler_params=_params("parallel", "parallel"),
    )(proj4, proj4, proj4, do3, o3, q_gain, k_gain, dproj4)


def hgrn2_bwd(proj4, don3, oraw3, states, lbl4, o_gain, dproj4):
    _, b, s, _ = proj4.shape
    nchunk = s // HG_CHUNK
    c = HG_CHUNK
    subs = range(HG_CHUNK // HG_SUB)
    ngroup = nchunk // HG_UNROLL

    def body(q_ref, f_ref, i_ref, don_ref, oraw_ref, st_ref, lbl_ref, og_ref, _alias, d_ref, dog_ref, dlb_ref):
        incl, lower, before_sub, upper = _hg_masks()
        lb, oml = _hg_lower_bound(lbl_ref)
        last_row = lax.broadcasted_iota(jnp.int32, (c, HEAD_DIM), 0) == c - 1

        def group(m, carry):
            dst, dog_acc, dlb_acc = carry
            ns, rows = _hg_group_rows(ngroup - 1 - m)
            ns, rows = ns[::-1], rows[::-1]
            vs = [_hg_gates(q_ref[0, 0, r, :], f_ref[0, 0, r, :], lb, oml) for r in rows]
            inps = [i_ref[0, 0, r, :].astype(BF16) for r in rows]
            sts = [st_ref[0, 0, n] for n in ns]
            gcs = [_dot_exact(v["logf"], lower, left=True) for v in vs]
            grs = [_dot_exact(v["logf"], before_sub, left=True) for v in vs]
            e_qs = [jnp.exp(gc - gr) for gc, gr in zip(gcs, grs)]
            a_s, qdbs, ksbs, ess = _hg_intra([v["q"] * e for v, e in zip(vs, e_qs)], [v["k"] for v in vs],
                                             gcs, grs, incl)
            e_gcs = [jnp.exp(gc) for gc in gcs]
            gls = [gc[c - 1:c, :] for gc in gcs]
            e_gls = [jnp.exp(gl) for gl in gls]
            e_ks = [jnp.exp(gl - gc) for gl, gc in zip(gls, gcs)]
            normed = [_rms_bwd(oraw_ref[0, r, :], og_ref[...], don_ref[0, r, :]) for r in rows]
            dobs = [do.astype(BF16) for do, _ in normed]
            dabs = [jnp.where(incl, _dot(dob, inp, NT), 0.0).astype(BF16) for dob, inp in zip(dobs, inps)]
            adds = [_dot(dob, (v["q"] * e).astype(BF16), TN) for dob, v, e in zip(dobs, vs, e_gcs)]
            dq_inters = [_dot(dob, st.astype(BF16)) * e for dob, st, e in zip(dobs, sts, e_gcs)]
            dqds = [jnp.concatenate([_dot(dab[sub * HG_SUB:(sub + 1) * HG_SUB, :], ksb[sub]) for sub in subs], axis=0)
                    for dab, ksb in zip(dabs, ksbs)]
            dkss = [[_dot(dab[sub * HG_SUB:(sub + 1) * HG_SUB, :], qdb[sub * HG_SUB:(sub + 1) * HG_SUB, :], TN)
                     for sub in subs] for dab, qdb in zip(dabs, qdbs)]
            dsts = []
            for e_gl, add in zip(e_gls, adds):
                dsts.append(dst)
                dst = dst * e_gl + add
            dstbs = [d.astype(BF16) for d in dsts]
            dis = [_dot(a.astype(BF16), dob, TN) + _dot((v["k"] * e_k).astype(BF16), dstb, NT)
                   for a, dob, v, e_k, dstb in zip(a_s, dobs, vs, e_ks, dstbs)]
            dk_inters = [_dot(inp, dstb) * e_k for inp, dstb, e_k in zip(inps, dstbs, e_ks)]
            for u, r in enumerate(rows):
                v, q, k = vs[u], vs[u]["q"], vs[u]["k"]
                dk, dgc_k = dk_inters[u], jnp.zeros((c, HEAD_DIM), F32)
                for sub in subs:
                    dk = dk + dkss[u][sub] * ess[u][sub]
                    dgc_k = dgc_k + dkss[u][sub] * ksbs[u][sub].astype(F32)
                dq = dqds[u] * e_qs[u] + dq_inters[u]
                at_last = (jnp.sum(k * dk_inters[u], axis=0, keepdims=True)
                           + e_gls[u] * jnp.sum(sts[u] * dsts[u], axis=0, keepdims=True))
                dgc = ((qdbs[u].astype(F32) * dqds[u] - dgc_k) + (q * dq_inters[u] - k * dk_inters[u])
                       + jnp.where(last_row, at_last, 0.0))
                dlf_f = _dot_exact(dgc, upper, left=True) / v["f"]
                d_ref[0, 0, r, :] = dq * (v["sq"] * (1.0 + v["qp"] * (1.0 - v["sq"])))
                d_ref[1, 0, r, :] = (dlf_f - dk) * (oml * v["sf"] * v["sfn"])
                d_ref[2, 0, r, :] = dis[u]
                dlb_acc = dlb_acc + jnp.sum((dlf_f - dk) * v["sfn"], axis=0, keepdims=True)
                dog_acc = dog_acc + jnp.sum(normed[u][1], axis=0, keepdims=True)
            return dst, dog_acc, dlb_acc

        zero = jnp.zeros((1, HEAD_DIM), F32)
        _, dog, dlb = lax.fori_loop(0, ngroup, group, (jnp.zeros((HEAD_DIM, HEAD_DIM), F32), zero, zero))
        dog_ref[0, 0] = dog
        dlb_ref[0, 0] = dlb

    head_row = jax.ShapeDtypeStruct((b, N_HEADS, 1, HEAD_DIM), F32)
    return _pcall(
        body, name="hgrn2_bwd", grid=(b, N_HEADS),
        in_specs=[_head_spec(s, 0), _head_spec(s, 1), _head_spec(s, 2), _seq_spec(s), _seq_spec(s),
                  _state_spec(nchunk), pl.BlockSpec((2, 1, 1, HEAD_DIM), lambda b, h: (0, h, 0, 0)), _GAIN_SPEC,
                  pl.BlockSpec(memory_space=pl.ANY)],
        out_specs=[pl.BlockSpec((3, 1, s, HEAD_DIM), lambda b, h: (0, b, 0, h)), _HEAD_ROW_SPEC, _HEAD_ROW_SPEC],
        out_shape=[jax.ShapeDtypeStruct(dproj4.shape, F32), head_row, head_row],
        input_output_aliases={8: 0},
        compiler_params=_params("parallel", "parallel"),
    )(proj4, proj4, proj4, don3, oraw3, states, lbl4, o_gain, dproj4)


def local_step(x, target, sb_norm, wsi, sb_q_gain, sb_k_gain, wso, hg_norm_full, whi, hg_o_gain, who, hg_lb_logits):
    b, s, _ = x.shape
    t = b * s
    x2 = x.reshape(t, D_MODEL)
    tg2 = target.reshape(t, D_MODEL)
    lbl4 = hg_lb_logits.reshape(2, N_HEADS, 1, HEAD_DIM)
    four = (4, b, s, D_MODEL)
    three = (b, s, D_MODEL)

    proj0, u0 = rms_inproj(x2, sb_norm, wsi, "sb_inproj")
    o0 = sb_attn_fwd(proj0.reshape(four), sb_q_gain, sb_k_gain).reshape(t, D_MODEL)
    h1 = gate_outproj(o0, proj0, wso, x2, None, "sb_outproj")
    proj1, u1 = rms_inproj(h1, hg_norm_full, whi, "hg_inproj")
    o1, o1_raw, states = hgrn2_fwd(proj1.reshape(four), lbl4, hg_o_gain)
    o1 = o1.reshape(t, D_MODEL)
    dh2, loss_parts = gate_outproj(o1, proj1, who, h1, tg2, "hg_outproj_loss")

    do1, dproj1, g_who = outproj_bwd(dh2, who, o1, proj1, "hg_outproj_bwd")
    dproj1, g_og, g_lb = hgrn2_bwd(proj1.reshape(four), do1.reshape(three), o1_raw, states, lbl4, hg_o_gain,
                                   dproj1.reshape(four))
    dproj1 = dproj1.reshape(4, t, D_MODEL)
    dh1, g_hgn = inproj_bwd_dx(dproj1, whi, h1, hg_norm_full, dh2, "hg_inproj_bwd_dx")
    g_whi = inproj_bwd_dw(u1, dproj1, "hg_inproj_bwd_dw")

    do0, dproj0, g_wso = outproj_bwd(dh1, wso, o0, proj0, "sb_outproj_bwd")
    dproj0, g_qg, g_kg = sb_attn_bwd(proj0.reshape(four), do0.reshape(three), o0.reshape(three),
                                     sb_q_gain, sb_k_gain, dproj0.reshape(four))
    dproj0 = dproj0.reshape(4, t, D_MODEL)
    gx, g_sbn = inproj_bwd_dx(dproj0, wsi, x2, sb_norm, dh1, "sb_inproj_bwd_dx")
    g_wsi = inproj_bwd_dw(u0, dproj0, "sb_inproj_bwd_dw")
    return dict(loss_parts=loss_parts, gx=gx.reshape(three), g_wsi=g_wsi, g_wso=g_wso, g_whi=g_whi, g_who=g_who,
                g_sbn=g_sbn, g_hgn=g_hgn, g_qg=g_qg, g_kg=g_kg, g_og=g_og, g_lb=g_lb)


MESH = pl.DeviceIdType.MESH
N_PEERS = N_DEV - 1
_ANY = pl.BlockSpec(memory_space=pl.ANY)
_VMEM = pl.BlockSpec(memory_space=pltpu.VMEM)


def _mesh_pos():
    return lax.axis_index("x"), lax.axis_index("y"), lax.axis_index("c")


def _linear(pos):
    return 4 * pos[0] + 2 * pos[1] + pos[2]


def _peer(pos, k):
    flips = ((k + 1) >> 2 & 1, (k + 1) >> 1 & 1, (k + 1) & 1)
    return tuple(1 - p if f else p for p, f in zip(pos, flips))


def _exchange(pairs, send_sems, recv_sems, local_sems, pos):
    me = _linear(pos)
    started = []
    for a, (src_of, dst) in enumerate(pairs):
        loc = pltpu.make_async_copy(src_of(me), dst.at[me], local_sems.at[a])
        loc.start()
        started.append(loc)
        for k in range(N_PEERS):
            peer = _peer(pos, k)
            pltpu.make_async_remote_copy(
                src_ref=src_of(_linear(peer)), dst_ref=dst.at[me], send_sem=send_sems.at[a, k],
                recv_sem=recv_sems.at[a, k], device_id=peer, device_id_type=MESH).start()
    for a, (src_of, dst) in enumerate(pairs):
        for k in range(N_PEERS):
            peer = _peer(pos, k)
            landed = pltpu.make_async_remote_copy(
                src_ref=src_of(_linear(peer)), dst_ref=dst.at[_linear(peer)], send_sem=send_sems.at[a, k],
                recv_sem=recv_sems.at[a, k], device_id=peer, device_id_type=MESH)
            landed.wait_recv()
            landed.wait_send()
    for loc in started:
        loc.wait()


def _exchange_sems(n):
    return [pltpu.SemaphoreType.DMA((n, N_PEERS)), pltpu.SemaphoreType.DMA((n, N_PEERS)),
            pltpu.SemaphoreType.DMA((n,))]


def all_gather_weights(w_si, w_so, w_hi, w_ho, hg_norm):
    shards = [w_si, w_so, w_hi, w_ho]

    def body(si_ref, so_ref, hi_ref, ho_ref, hn_ref, o_si, o_so, o_hi, o_ho, o_hn,
             b_si, b_so, b_hi, b_ho, b_hn, send_sems, recv_sems, local_sems):
        for src, buf in ((si_ref, b_si), (so_ref, b_so), (hi_ref, b_hi), (ho_ref, b_ho)):
            buf[...] = src[...].astype(BF16)
        b_hn[...] = jnp.broadcast_to(hn_ref[...], b_hn.shape)
        pairs = [((lambda p, buf=buf: buf), out) for buf, out in
                 ((b_si, o_si), (b_so, o_so), (b_hi, o_hi), (b_ho, o_ho), (b_hn, o_hn))]
        _exchange(pairs, send_sems, recv_sems, local_sems, _mesh_pos())

    out_shape = [jax.ShapeDtypeStruct((N_DEV,) + w.shape, BF16) for w in shards]
    out_shape.append(jax.ShapeDtypeStruct((N_DEV, 8, HEAD_DIM), F32))
    scratch = [pltpu.VMEM(w.shape, BF16) for w in shards] + [pltpu.VMEM((8, HEAD_DIM), F32)]
    return _pcall(
        body, name="all_gather_weights",
        in_specs=[_VMEM] * 5, out_specs=[_ANY] * 5, out_shape=out_shape,
        scratch_shapes=scratch + _exchange_sems(5),
        compiler_params=pltpu.CompilerParams(vmem_limit_bytes=VMEM_LIMIT_BYTES),
    )(w_si, w_so, w_hi, w_ho, hg_norm)


def exchange_grads(grads):
    def body(*refs):
        n = len(grads)
        ins, outs = refs[:n], refs[n:2 * n]
        send_sems, recv_sems, local_sems = refs[2 * n:]
        pairs = [((lambda p, g=g: g.at[p]), r) for g, r in zip(ins, outs)]
        _exchange(pairs, send_sems, recv_sems, local_sems, _mesh_pos())

    return _pcall(
        body, name="exchange_grads",
        in_specs=[_ANY] * len(grads), out_specs=[_ANY] * len(grads),
        out_shape=[jax.ShapeDtypeStruct(g.shape, F32) for g in grads],
        scratch_shapes=_exchange_sems(len(grads)),
    )(*grads)


def _adamw(w, g, m, v):
    m = ADAM_B1 * m + (1.0 - ADAM_B1) * g
    v = ADAM_B2 * v + (1.0 - ADAM_B2) * (g * g)
    m_hat = m / (1.0 - ADAM_B1 ** ADAM_STEP)
    v_hat = v / (1.0 - ADAM_B2 ** ADAM_STEP)
    delta = -ADAM_LR * (m_hat / (jnp.sqrt(v_hat) + ADAM_EPS) + ADAM_WD * w)
    return delta, m, v


def reduce_adamw(parts, w, m, v, name):
    _, r, c = parts.shape
    tr = _row_tile(r, 256)

    def body(p_ref, w_ref, m_ref, v_ref, g_ref, d_ref, m2_ref, v2_ref):
        g = p_ref[0]
        for dev in range(1, N_DEV):
            g = g + p_ref[dev]
        g_ref[...] = g
        d_ref[...], m2_ref[...], v2_ref[...] = _adamw(w_ref[...], g, m_ref[...], v_ref[...])

    tile = pl.BlockSpec((tr, c), lambda i: (i, 0))
    return _pcall(
        body, name=name, grid=(r // tr,),
        in_specs=[pl.BlockSpec((N_DEV, tr, c), lambda i: (0, i, 0)), tile, tile, tile],
        out_specs=[tile] * 4, out_shape=[jax.ShapeDtypeStruct((r, c), F32)] * 4,
        compiler_params=_params("parallel"),
    )(parts, w, m, v)


PACK_ROWS = 32
ROW_SBN, ROW_HGN, ROW_LB, ROW_QG, ROW_KG, ROW_OG, ROW_LOSS = 0, 8, 16, 24, 25, 26, 27


def small_update(g_sbn, g_hgn, g_lb, g_qg, g_kg, g_og, loss_parts, small):
    n_in = 7 + len(small)

    def body(*refs):
        sbn_ref, hgn_ref, lb_ref, qg_ref, kg_ref, og_ref, loss_ref = refs[:7]
        wmv = refs[7:n_in]
        outs = refs[n_in:n_in + 25]
        pack, gath, tot, send_sems, recv_sems, local_sems = refs[n_in + 25:]
        pos = _mesh_pos()
        me = _linear(pos)
        pack[...] = jnp.zeros_like(pack)
        pack[ROW_SBN:ROW_SBN + 8, :] = sbn_ref[...]
        pack[ROW_HGN:ROW_HGN + 8, :] = hgn_ref[...]
        pack[ROW_LB:ROW_LB + 8, :] = jnp.sum(lb_ref[...], axis=0)
        pack[ROW_QG:ROW_QG + 1, :] = jnp.sum(qg_ref[...], axis=0, keepdims=True)
        pack[ROW_KG:ROW_KG + 1, :] = jnp.sum(kg_ref[...], axis=0, keepdims=True)
        pack[ROW_OG:ROW_OG + 1, :] = jnp.sum(og_ref[...], axis=0, keepdims=True)
        pack[ROW_LOSS:ROW_LOSS + 1, :] = jnp.sum(loss_ref[...], axis=0)[0:1, :]
        _exchange([((lambda p: pack), gath)], send_sems, recv_sems, local_sems, pos)
        total = gath[0]
        for dev in range(1, N_DEV):
            total = total + gath[dev]
        tot[...] = total
        outs[0][...] = jnp.broadcast_to(tot[ROW_LOSS:ROW_LOSS + 1, :], (8, HEAD_DIM))
        l0 = wmv[15][0:8, :]
        l1 = wmv[15][8:16, :]
        p1, p0 = _sigmoid_pair(l1 - l0)
        d_l1 = p0 * p1 * tot[ROW_LB:ROW_LB + 8, :]
        grads = [tot[ROW_SBN:ROW_SBN + 8, :], tot[ROW_QG:ROW_QG + 1, :], tot[ROW_KG:ROW_KG + 1, :],
                 tot[pl.ds(ROW_HGN + me, 1), :], tot[ROW_OG:ROW_OG + 1, :],
                 jnp.concatenate([-d_l1, d_l1], axis=0)]
        for i, g in enumerate(grads):
            w_ref, m_ref, v_ref = wmv[3 * i:3 * i + 3]
            o = outs[1 + 4 * i:5 + 4 * i]
            o[0][...] = g
            o[1][...], o[2][...], o[3][...] = _adamw(w_ref[...], g, m_ref[...], v_ref[...])

    out_shape = [jax.ShapeDtypeStruct((8, HEAD_DIM), F32)]
    for i in range(6):
        out_shape += [jax.ShapeDtypeStruct(small[3 * i].shape, F32)] * 4
    return _pcall(
        body, name="small_update",
        in_specs=[_VMEM] * n_in, out_specs=[_VMEM] * 25, out_shape=out_shape,
        scratch_shapes=[pltpu.VMEM((PACK_ROWS, HEAD_DIM), F32), pltpu.VMEM((N_DEV, PACK_ROWS, HEAD_DIM), F32),
                        pltpu.VMEM((PACK_ROWS, HEAD_DIM), F32)] + _exchange_sems(1),
    )(g_sbn, g_hgn, g_lb, g_qg, g_kg, g_og, loss_parts, *small)


def kernel(x, sb_norm, sb_w_in, sb_q_gain, sb_k_gain, sb_w_out, hg_norm, hg_w_in, hg_o_gain, hg_w_out, hg_lb_logits, loss_target, m_sb_norm, m_sb_w_in, m_sb_q_gain, m_sb_k_gain, m_sb_w_out, m_hg_norm, m_hg_w_in, m_hg_o_gain, m_hg_w_out, m_hg_lb_logits, v_sb_norm, v_sb_w_in, v_sb_q_gain, v_sb_k_gain, v_sb_w_out, v_hg_norm, v_hg_w_in, v_hg_o_gain, v_hg_w_out, v_hg_lb_logits):
    b = x.shape[0]
    wsi, wso, whi, who, hgn = all_gather_weights(sb_w_in[0], sb_w_out[0], hg_w_in[0], hg_w_out[0], hg_norm)
    hg_norm_full = hgn[:, 0, :].reshape(1, D_MODEL)
    r = local_step(x, loss_target, sb_norm, wsi, sb_q_gain, sb_k_gain, wso.reshape(D_MODEL, D_MODEL),
                   hg_norm_full, whi, hg_o_gain, who.reshape(D_MODEL, D_MODEL), hg_lb_logits)

    parts = exchange_grads([r["g_wsi"], r["g_wso"].reshape(N_DEV, W_ROWS, D_MODEL),
                            r["g_whi"], r["g_who"].reshape(N_DEV, W_ROWS, D_MODEL)])
    big = {}
    for name, p, w, m, v in (("sb_w_in", parts[0], sb_w_in, m_sb_w_in, v_sb_w_in),
                             ("sb_w_out", parts[1], sb_w_out, m_sb_w_out, v_sb_w_out),
                             ("hg_w_in", parts[2], hg_w_in, m_hg_w_in, v_hg_w_in),
                             ("hg_w_out", parts[3], hg_w_out, m_hg_w_out, v_hg_w_out)):
        big[name] = [o[None] for o in reduce_adamw(p, w[0], m[0], v[0], "adamw_" + name)]

    def rows8(a):
        return a.reshape(8, HEAD_DIM)

    def rows16(a):
        return a.reshape(16, HEAD_DIM)

    small_in = [rows8(sb_norm), rows8(m_sb_norm), rows8(v_sb_norm),
                sb_q_gain, m_sb_q_gain, v_sb_q_gain,
                sb_k_gain, m_sb_k_gain, v_sb_k_gain,
                hg_norm, m_hg_norm, v_hg_norm,
                hg_o_gain, m_hg_o_gain, v_hg_o_gain,
                rows16(hg_lb_logits), rows16(m_hg_lb_logits), rows16(v_hg_lb_logits)]
    so = small_update(rows8(r["g_sbn"]), rows8(r["g_hgn"]), r["g_lb"].reshape(b, N_HEADS, HEAD_DIM),
                      r["g_qg"].reshape(b * N_HEADS, HEAD_DIM), r["g_kg"].reshape(b * N_HEADS, HEAD_DIM),
                      r["g_og"].reshape(b * N_HEADS, HEAD_DIM), r["loss_parts"], small_in)
    loss = so[0][0, 0]
    shapes = {"sb_norm": (1, D_MODEL), "sb_q_gain": (1, HEAD_DIM), "sb_k_gain": (1, HEAD_DIM),
              "hg_norm": (1, HEAD_DIM), "hg_o_gain": (1, HEAD_DIM), "hg_lb_logits": (2, D_MODEL)}
    small = {}
    for i, name in enumerate(("sb_norm", "sb_q_gain", "sb_k_gain", "hg_norm", "hg_o_gain", "hg_lb_logits")):
        small[name] = [o.reshape(shapes[name]) for o in so[1 + 4 * i:5 + 4 * i]]
    order = ("sb_norm", "sb_w_in", "sb_q_gain", "sb_k_gain", "sb_w_out",
             "hg_norm", "hg_w_in", "hg_o_gain", "hg_w_out", "hg_lb_logits")
    res = {**big, **small}
    return (loss, r["gx"]) + tuple(res[n][j] for j in range(4) for n in order)
```

```python
import functools

import jax
import jax.numpy as jnp
from jax import lax
from jax.experimental import pallas as pl
from jax.experimental.pallas import tpu as pltpu

F32 = jnp.float32
BF16 = jnp.bfloat16

N_DEV = 8
D_MODEL = 1024
N_HEADS = 8
HEAD_DIM = 128
RMS_EPS = 1e-6
ATTN_BLOCK = 128
HG_CHUNK = 64
HG_SUB = 16
HG_UNROLL = 8
EXP_CLAMP = 80.0
SB_HEADS_PER_STEP = 2
SB_QBLOCKS_PER_STEP = 4
SB_GROUP_COLS = SB_HEADS_PER_STEP * 128
SB_LOG_WEIGHT_FLOOR = -104.0
VMEM_LIMIT_BYTES = 48 * 1024 * 1024
W_COLS = 4 * D_MODEL // N_DEV
W_ROWS = D_MODEL // N_DEV

ADAM_LR = 0.001
ADAM_B1 = 0.9
ADAM_B2 = 0.999
ADAM_EPS = 1e-08
ADAM_WD = 0.01
ADAM_STEP = 10

NT = (((1,), (1,)), ((), ()))
TN = (((0,), (0,)), ((), ()))
NN = (((1,), (0,)), ((), ()))


def _pcall(body, *, name, **kw):
    return pl.pallas_call(body, name=name, **kw)


def _params(*sem):
    return pltpu.CompilerParams(dimension_semantics=sem, vmem_limit_bytes=VMEM_LIMIT_BYTES)


def _dot(a, b, dims=NN):
    return lax.dot_general(a, b, dims, preferred_element_type=F32)


def _dot_exact(a, m, dims=NN, left=False):
    hi = a.astype(BF16)
    lo = (a - hi.astype(F32)).astype(BF16)
    if left:
        return _dot(m, hi, dims) + _dot(m, lo, dims)
    return _dot(hi, m, dims) + _dot(lo, m, dims)


def _split(a):
    hi = a.astype(BF16)
    return hi, (a - hi.astype(F32)).astype(BF16)


def _dot3(a, b, dims=NN):
    return _dot(a[0], b[0], dims) + (_dot(a[0], b[1], dims) + _dot(a[1], b[0], dims))


def _sigmoid(x):
    return 1.0 / (1.0 + jnp.exp(-x))


def _sigmoid_pair(x):
    e = jnp.exp(-jnp.abs(x))
    big = 1.0 / (1.0 + e)
    small = e * big
    pos = x >= 0
    return jnp.where(pos, big, small), jnp.where(pos, small, big)


def _rms_scale(x):
    return lax.rsqrt(jnp.mean(x * x, axis=-1, keepdims=True) + RMS_EPS)


def _row_tile(t, want):
    return want if t % want == 0 else t


MESH = pl.DeviceIdType.MESH
N_PEERS = N_DEV - 1
_ANY = pl.BlockSpec(memory_space=pl.ANY)
_VMEM = pl.BlockSpec(memory_space=pltpu.VMEM)


def _mesh_pos():
    return lax.axis_index("x"), lax.axis_index("y"), lax.axis_index("c")


def _linear(pos):
    return 4 * pos[0] + 2 * pos[1] + pos[2]


def _peer(pos, k):
    flips = ((k + 1) >> 2 & 1, (k + 1) >> 1 & 1, (k + 1) & 1)
    return tuple(1 - p if f else p for p, f in zip(pos, flips))


def _exchange_copies(pairs, send_sems, recv_sems, local_sems, pos, landing):
    me = _linear(pos)
    local, remote = [], []
    for a, (src_of, dst) in enumerate(pairs):
        local.append(pltpu.make_async_copy(src_of(me), dst.at[me], local_sems.at[a]))
        for k in range(N_PEERS):
            peer = _peer(pos, k)
            remote.append(pltpu.make_async_remote_copy(
                src_ref=src_of(_linear(peer)), dst_ref=dst.at[_linear(peer) if landing else me],
                send_sem=send_sems.at[a, k], recv_sem=recv_sems.at[a, k], device_id=peer, device_id_type=MESH))
    return local, remote


def _exchange_start(pairs, send_sems, recv_sems, local_sems, pos):
    local, sent = _exchange_copies(pairs, send_sems, recv_sems, local_sems, pos, landing=False)
    for copy in local + sent:
        copy.start()


def _exchange_wait(pairs, send_sems, recv_sems, local_sems, pos):
    local, landed = _exchange_copies(pairs, send_sems, recv_sems, local_sems, pos, landing=True)
    for copy in landed:
        copy.wait_recv()
        copy.wait_send()
    for copy in local:
        copy.wait()


def _exchange_sems(n):
    return [pltpu.SemaphoreType.DMA((n, N_PEERS)), pltpu.SemaphoreType.DMA((n, N_PEERS)),
            pltpu.SemaphoreType.DMA((n,))]


class _Rider:
    def __init__(self, arrays, scatter):
        self.arrays = list(arrays)
        self.scatter = scatter
        self.out_shapes = [jax.ShapeDtypeStruct(a.shape if scatter else (N_DEV,) + a.shape, a.dtype)
                           for a in self.arrays]

    def pairs(self, in_refs, out_refs):
        if self.scatter:
            return [((lambda p, r=r: r.at[p]), o) for r, o in zip(in_refs, out_refs)]
        return [((lambda p, r=r: r), o) for r, o in zip(in_refs, out_refs)]


def _pcall_riding(body, rider, args, *, name, grid, in_specs, out_specs, out_shape, semantics, scratch_shapes=(),
                  input_output_aliases=None):
    aliases = input_output_aliases or {}
    if rider is None:
        outs = _pcall(body, name=name, grid=grid, in_specs=list(in_specs), out_specs=list(out_specs),
                      out_shape=list(out_shape), scratch_shapes=list(scratch_shapes), input_output_aliases=aliases,
                      compiler_params=_params(*semantics))(*args)
        return list(outs), []
    n_in, n_out, n_scr, n_r = len(in_specs), len(out_specs), len(scratch_shapes), len(rider.arrays)

    def riding(*refs):
        ins, refs = refs[:n_in], refs[n_in:]
        rider_in, refs = refs[:n_r], refs[n_r:]
        outs, refs = refs[:n_out], refs[n_out:]
        rider_out, refs = refs[:n_r], refs[n_r:]
        scratch, sems = refs[:n_scr], refs[n_scr:]
        pairs = rider.pairs(rider_in, rider_out)
        first = functools.reduce(jnp.logical_and, [pl.program_id(a) == 0 for a in range(len(grid))])
        last = functools.reduce(jnp.logical_and, [pl.program_id(a) == g - 1 for a, g in enumerate(grid)])

        @pl.when(first)
        def _():
            _exchange_start(pairs, *sems, _mesh_pos())

        body(*ins, *outs, *scratch)

        @pl.when(last)
        def _():
            _exchange_wait(pairs, *sems, _mesh_pos())

    outs = _pcall(riding, name=name, grid=grid, in_specs=list(in_specs) + [_ANY] * n_r,
                  out_specs=list(out_specs) + [_ANY] * n_r, out_shape=list(out_shape) + rider.out_shapes,
                  scratch_shapes=list(scratch_shapes) + _exchange_sems(n_r), input_output_aliases=aliases,
                  compiler_params=_params(*(("arbitrary",) * len(grid))))(*args, *rider.arrays)
    return list(outs[:n_out]), list(outs[n_out:])


def rms_inproj(x2, gain, wg, name, rider=None):
    t = x2.shape[0]
    tm = _row_tile(t, 512)

    def body(x_ref, g_ref, w_ref, proj_ref, u_ref):
        @pl.when(pl.program_id(1) == 0)
        def _():
            x = x_ref[...]
            u_ref[...] = (x * _rms_scale(x) * g_ref[...]).astype(BF16)

        proj_ref[0] = _dot(u_ref[...], w_ref[0])

    return _pcall_riding(
        body, rider, (x2, gain, wg), name=name,
        grid=(t // tm, N_DEV),
        in_specs=[pl.BlockSpec((tm, D_MODEL), lambda i, j: (i, 0)),
                  pl.BlockSpec((1, D_MODEL), lambda i, j: (0, 0)),
                  pl.BlockSpec((1, D_MODEL, W_COLS), lambda i, j: (j, 0, 0))],
        out_specs=[pl.BlockSpec((1, tm, W_COLS), lambda i, j: (j // 2, i, j % 2)),
                   pl.BlockSpec((tm, D_MODEL), lambda i, j: (i, 0))],
        out_shape=[jax.ShapeDtypeStruct((4, t, D_MODEL), F32),
                   jax.ShapeDtypeStruct((t, D_MODEL), BF16)],
        semantics=("parallel", "arbitrary"))


def gate_outproj(o2, proj, w_out, resid, target, name):
    t = o2.shape[0]
    tm = _row_tile(t, 256)
    with_loss = target is not None

    def body(o_ref, gate_ref, w_ref, r_ref, *rest):
        g = gate_ref[0]
        og = (o_ref[...] * (g * _sigmoid(g))).astype(BF16)
        h = r_ref[...] + _dot(og, w_ref[...])
        if with_loss:
            t_ref, dh_ref, loss_ref = rest
            err = h - t_ref[...]
            dh_ref[...] = err * (1.0 / D_MODEL)
            part = 0.5 * jnp.sum(jnp.mean(err * err, axis=-1, keepdims=True))
            loss_ref[...] = jnp.full(loss_ref.shape, part, F32)
        else:
            (h_ref,) = rest
            h_ref[...] = h

    row = pl.BlockSpec((tm, D_MODEL), lambda i: (i, 0))
    in_specs = [row,
                pl.BlockSpec((1, tm, D_MODEL), lambda i: (3, i, 0)),
                pl.BlockSpec((D_MODEL, D_MODEL), lambda i: (0, 0)),
                row]
    args = [o2, proj, w_out, resid]
    if with_loss:
        in_specs.append(row)
        args.append(target)
        out_specs = [row, pl.BlockSpec((1, 8, 128), lambda i: (i, 0, 0))]
        out_shape = [jax.ShapeDtypeStruct((t, D_MODEL), F32),
                     jax.ShapeDtypeStruct((t // tm, 8, 128), F32)]
    else:
        out_specs = row
        out_shape = jax.ShapeDtypeStruct((t, D_MODEL), F32)
    return _pcall(body, name=name, grid=(t // tm,), in_specs=in_specs, out_specs=out_specs,
                  out_shape=out_shape, compiler_params=_params("parallel"))(*args)


def _head_spec(s, part):
    return pl.BlockSpec((1, 1, s, HEAD_DIM), lambda b, h: (part, b, 0, h))


def _seq_spec(s):
    return pl.BlockSpec((1, s, HEAD_DIM), lambda b, h: (b, 0, h))


_GAIN_SPEC = pl.BlockSpec((1, HEAD_DIM), lambda b, h: (0, 0))
_HEAD_ROW_SPEC = pl.BlockSpec((1, 1, 1, HEAD_DIM), lambda b, h: (b, h, 0, 0))


def _sb_group_spec(s, part):
    return pl.BlockSpec((1, 1, s, SB_GROUP_COLS), lambda b, g: (part, b, 0, g))


def _sb_seq_group_spec(s):
    return pl.BlockSpec((1, s, SB_GROUP_COLS), lambda b, g: (b, 0, g))


_SB_GROUP_ROW_SPEC = pl.BlockSpec((1, SB_HEADS_PER_STEP, 1, HEAD_DIM), lambda b, g: (b, g, 0, 0))


def _sb_chains(m):
    return [(h, m * SB_QBLOCKS_PER_STEP + r) for h in range(SB_HEADS_PER_STEP) for r in range(SB_QBLOCKS_PER_STEP)]


def _sb_logits(qi, kj):
    return _dot(qi, kj, NT) * (HEAD_DIM ** -0.5)


def _sb_scores(z, diag, live, tri_lt):
    soft = jnp.log1p(jnp.exp(-jnp.abs(z)))
    valid = jnp.logical_and(live, jnp.logical_or(jnp.logical_not(diag), tri_lt))
    log_skip = jnp.where(valid, -(jnp.maximum(z, 0.0) + soft), 0.0)
    log_beta = jnp.minimum(z, 0.0) - soft
    return log_skip, log_beta, valid


def _sb_keys_left(chains, state):
    done, carries = state[0], state[1]
    worst = None
    for (_, i), c in zip(chains, carries):
        c = jnp.where(done <= i, c, -jnp.inf)
        worst = c if worst is None else jnp.maximum(worst, c)
    return jnp.logical_and(done <= chains[-1][1],
                           jnp.logical_or(done == 0, jnp.max(worst) > SB_LOG_WEIGHT_FLOOR))


def _sb_key_rows(i, done):
    j = i - done
    return pl.ds(pl.multiple_of(jnp.maximum(j, 0) * ATTN_BLOCK, ATTN_BLOCK), ATTN_BLOCK), j >= 0


def _sb_load_normed(src_ref, gain_ref, dst):
    for h in range(SB_HEADS_PER_STEP):
        x = src_ref[0, 0, :, h * HEAD_DIM:(h + 1) * HEAD_DIM]
        dst[h] = (x * _rms_scale(x) * gain_ref[...]).astype(BF16)


def sb_attn_fwd(proj4, q_gain, k_gain, rider=None):
    _, b, s, _ = proj4.shape
    blk = ATTN_BLOCK
    nq = s // blk

    def body(q_ref, k_ref, v_ref, qg_ref, kg_ref, o_ref, qb, kb, vb):
        _sb_load_normed(q_ref, qg_ref, qb)
        _sb_load_normed(k_ref, kg_ref, kb)
        for h in range(SB_HEADS_PER_STEP):
            vb[h] = v_ref[0, 0, :, h * HEAD_DIM:(h + 1) * HEAD_DIM].astype(BF16)
        row = lax.broadcasted_iota(jnp.int32, (blk, blk), 0)
        col = lax.broadcasted_iota(jnp.int32, (blk, blk), 1)
        tri_lt = col < row
        suffix = (row > col).astype(BF16)

        def q_group(m, _):
            chains = _sb_chains(m)
            qis = [qb[h, pl.ds(pl.multiple_of(i * blk, blk), blk), :] for h, i in chains]

            def k_step(state):
                done, cs, accs = state
                where = [_sb_key_rows(i, done) for _, i in chains]
                zs = [_sb_logits(qi, kb[h, rows, :]) for (h, _), qi, (rows, _) in zip(chains, qis, where)]
                scored = [_sb_scores(z, done == 0, live, tri_lt) for z, (_, live) in zip(zs, where)]
                afters = [_dot_exact(log_skip, suffix) for log_skip, _, _ in scored]
                ws = [jnp.where(valid, jnp.exp(log_beta + after + c), 0.0).astype(BF16)
                      for (_, log_beta, valid), after, c in zip(scored, afters, cs)]
                new_accs = [acc + _dot(w, vb[h, rows, :]) for (h, _), (rows, _), w, acc in zip(chains, where, ws, accs)]
                new_cs = [c + jnp.sum(log_skip, axis=1, keepdims=True) for (log_skip, _, _), c in zip(scored, cs)]
                return done + 1, tuple(new_cs), tuple(new_accs)

            n = len(chains)
            _, _, accs = lax.while_loop(
                functools.partial(_sb_keys_left, chains), k_step,
                (jnp.int32(0), (jnp.zeros((blk, 1), F32),) * n, (jnp.zeros((blk, HEAD_DIM), F32),) * n))
            for (h, i), acc in zip(chains, accs):
                o_ref[0, pl.ds(pl.multiple_of(i * blk, blk), blk), h * HEAD_DIM:(h + 1) * HEAD_DIM] = acc
            return 0

        lax.fori_loop(0, nq // SB_QBLOCKS_PER_STEP, q_group, 0)

    (o,), extra = _pcall_riding(
        body, rider, (proj4, proj4, proj4, q_gain, k_gain),
        name="sb_attn_fwd", grid=(b, N_HEADS // SB_HEADS_PER_STEP),
        in_specs=[_sb_group_spec(s, 0), _sb_group_spec(s, 1), _sb_group_spec(s, 2), _GAIN_SPEC, _GAIN_SPEC],
        out_specs=[_sb_seq_group_spec(s)],
        out_shape=[jax.ShapeDtypeStruct((b, s, D_MODEL), F32)],
        scratch_shapes=[pltpu.VMEM((SB_HEADS_PER_STEP, s, HEAD_DIM), BF16)] * 3,
        semantics=("parallel", "parallel"))
    return o, extra


def _hg_masks():
    c = HG_CHUNK
    row = lax.broadcasted_iota(jnp.int32, (c, c), 0)
    col = lax.broadcasted_iota(jnp.int32, (c, c), 1)
    incl = (col <= row)
    lower = incl.astype(BF16)
    before_sub = (col < (row // HG_SUB) * HG_SUB).astype(BF16)
    upper = (col >= row).astype(BF16)
    return incl, lower, before_sub, upper


def _hg_lower_bound(lbl_ref):
    l0 = lbl_ref[0, 0]
    l1 = lbl_ref[1, 0]
    d = l1 - l0
    return _sigmoid_pair(d)


def _hg_gates(qp, fp, lb, oml):
    sq = _sigmoid(qp)
    sf, sfn = _sigmoid_pair(fp)
    f = lb + oml * sf
    return dict(qp=qp, sq=sq, q=qp * sq, sf=sf, sfn=sfn, f=f, k=oml * sfn, logf=jnp.log(f))


def _hg_intra(qds, ks, gcs, grs, incl):
    subs = range(HG_CHUNK // HG_SUB)
    qdbs = [qd.astype(BF16) for qd in qds]
    ess = [[jnp.exp(jnp.minimum(gr[sub * HG_SUB:sub * HG_SUB + 1, :] - gc, EXP_CLAMP)) for sub in subs]
           for gc, gr in zip(gcs, grs)]
    ksbs = [[(k * e).astype(BF16) for e in es] for k, es in zip(ks, ess)]
    rows = [[_dot(qdb[sub * HG_SUB:(sub + 1) * HG_SUB, :], ksb[sub], NT) for sub in subs]
            for qdb, ksb in zip(qdbs, ksbs)]
    a_s = [jnp.where(incl, jnp.concatenate(r, axis=0), 0.0) for r in rows]
    return a_s, qdbs, ksbs, ess


def _hg_group_rows(outer):
    ns = [outer * HG_UNROLL + u for u in range(HG_UNROLL)]
    return ns, [pl.ds(pl.multiple_of(n * HG_CHUNK, HG_CHUNK), HG_CHUNK) for n in ns]


def _state_spec(nchunk):
    return pl.BlockSpec((1, 1, nchunk, HEAD_DIM, HEAD_DIM), lambda b, h: (b, h, 0, 0, 0))


def hgrn2_fwd(proj4, lbl4, o_gain):
    _, b, s, _ = proj4.shape
    nchunk = s // HG_CHUNK
    c = HG_CHUNK
    assert nchunk % HG_UNROLL == 0

    def body(q_ref, f_ref, i_ref, lbl_ref, og_ref, o_ref, oraw_ref, st_ref):
        incl, lower, before_sub, _ = _hg_masks()
        lb, oml = _hg_lower_bound(lbl_ref)

        def group(outer, st):
            ns, rows = _hg_group_rows(outer)
            vs = [_hg_gates(q_ref[0, 0, r, :], f_ref[0, 0, r, :], lb, oml) for r in rows]
            inps = [i_ref[0, 0, r, :].astype(BF16) for r in rows]
            gcs = [_dot_exact(v["logf"], lower, left=True) for v in vs]
            grs = [_dot_exact(v["logf"], before_sub, left=True) for v in vs]
            a_s, _, _, _ = _hg_intra([v["q"] * jnp.exp(gc - gr) for v, gc, gr in zip(vs, gcs, grs)],
                                     [v["k"] for v in vs], gcs, grs, incl)
            gls = [gc[c - 1:c, :] for gc in gcs]
            adds = [_dot(inp, (v["k"] * jnp.exp(gl - gc)).astype(BF16), TN)
                    for inp, v, gl, gc in zip(inps, vs, gls, gcs)]
            o_intra = [_dot(a.astype(BF16), inp) for a, inp in zip(a_s, inps)]
            sts = []
            for gl, add in zip(gls, adds):
                sts.append(st)
                st = st * jnp.exp(gl) + add
            outs = [oi + _dot((v["q"] * jnp.exp(gc)).astype(BF16), s0.astype(BF16), NT)
                    for oi, v, gc, s0 in zip(o_intra, vs, gcs, sts)]
            for n, r, s0, o in zip(ns, rows, sts, outs):
                st_ref[0, 0, n] = s0
                oraw_ref[0, r, :] = o
                o_ref[0, r, :] = o * _rms_scale(o) * og_ref[...]
            return st

        lax.fori_loop(0, nchunk // HG_UNROLL, group, jnp.zeros((HEAD_DIM, HEAD_DIM), F32))

    seq = jax.ShapeDtypeStruct((b, s, D_MODEL), F32)
    return _pcall(
        body, name="hgrn2_fwd", grid=(b, N_HEADS),
        in_specs=[_head_spec(s, 0), _head_spec(s, 1), _head_spec(s, 2),
                  pl.BlockSpec((2, 1, 1, HEAD_DIM), lambda b, h: (0, h, 0, 0)), _GAIN_SPEC],
        out_specs=[_seq_spec(s), _seq_spec(s), _state_spec(nchunk)],
        out_shape=[seq, seq, jax.ShapeDtypeStruct((b, N_HEADS, nchunk, HEAD_DIM, HEAD_DIM), F32)],
        compiler_params=_params("parallel", "parallel"),
    )(proj4, proj4, proj4, lbl4, o_gain)


def outproj_bwd(dh, w_out, o2, proj, name):
    t = dh.shape[0]
    tm = _row_tile(t, 256)

    def body(dh_ref, w_ref, o_ref, gate_ref, do_ref, dproj_ref, dw_ref):
        dhb = dh_ref[...].astype(BF16)
        dog = _dot(dhb, w_ref[...], NT)
        g = gate_ref[0]
        sg = _sigmoid(g)
        silu = g * sg
        o = o_ref[...]
        do_ref[...] = dog * silu
        dproj_ref[0] = dog * o * (sg * (1.0 + g * (1.0 - sg)))
        part = _dot((o * silu).astype(BF16), dhb, TN)

        @pl.when(pl.program_id(0) == 0)
        def _():
            dw_ref[...] = part

        @pl.when(pl.program_id(0) > 0)
        def _():
            dw_ref[...] += part

    row = pl.BlockSpec((tm, D_MODEL), lambda i: (i, 0))
    full = pl.BlockSpec((D_MODEL, D_MODEL), lambda i: (0, 0))
    return _pcall(
        body, name=name, grid=(t // tm,),
        in_specs=[row, full, row, pl.BlockSpec((1, tm, D_MODEL), lambda i: (3, i, 0))],
        out_specs=[row, pl.BlockSpec((1, tm, D_MODEL), lambda i: (3, i, 0)), full],
        out_shape=[jax.ShapeDtypeStruct((t, D_MODEL), F32),
                   jax.ShapeDtypeStruct((4, t, D_MODEL), F32),
                   jax.ShapeDtypeStruct((D_MODEL, D_MODEL), F32)],
        compiler_params=_params("arbitrary"),
    )(dh, w_out, o2, proj)


def inproj_bwd_dx(dproj, wg, x2, gain, dres, name, rider=None):
    t = x2.shape[0]
    tm = _row_tile(t, 256)

    def body(d_ref, w_ref, x_ref, g_ref, r_ref, dx_ref, dg_ref):
        du = jnp.zeros((tm, D_MODEL), F32)
        for p in range(N_DEV):
            cols = slice((p % 2) * W_COLS, (p % 2 + 1) * W_COLS)
            du = du + _dot(d_ref[p // 2, :, cols].astype(BF16), w_ref[p], NT)
        x = x_ref[...]
        r = _rms_scale(x)
        xh = x * r
        a = du * g_ref[...]
        dx_ref[...] = r_ref[...] + r * (a - xh * jnp.mean(a * xh, axis=-1, keepdims=True))
        part = jnp.sum(du * xh, axis=0, keepdims=True)

        @pl.when(pl.program_id(0) == 0)
        def _():
            dg_ref[...] = part

        @pl.when(pl.program_id(0) > 0)
        def _():
            dg_ref[...] += part

    row = pl.BlockSpec((tm, D_MODEL), lambda i: (i, 0))
    vec = pl.BlockSpec((1, D_MODEL), lambda i: (0, 0))
    return _pcall_riding(
        body, rider, (dproj, wg, x2, gain, dres), name=name, grid=(t // tm,),
        in_specs=[pl.BlockSpec((4, tm, D_MODEL), lambda i: (0, i, 0)),
                  pl.BlockSpec((N_DEV, D_MODEL, W_COLS), lambda i: (0, 0, 0)),
                  row, vec, row],
        out_specs=[row, vec],
        out_shape=[jax.ShapeDtypeStruct((t, D_MODEL), F32), jax.ShapeDtypeStruct((1, D_MODEL), F32)],
        semantics=("arbitrary",))


def inproj_bwd_dw(u, dproj, name):
    t = u.shape[0]
    tm = _row_tile(t, 512)

    def body(u_ref, d_ref, dw_ref):
        part = _dot(u_ref[...], d_ref[0].astype(BF16), TN)

        @pl.when(pl.program_id(1) == 0)
        def _():
            dw_ref[0] = part

        @pl.when(pl.program_id(1) > 0)
        def _():
            dw_ref[0] += part

    return _pcall(
        body, name=name, grid=(N_DEV, t // tm),
        in_specs=[pl.BlockSpec((tm, D_MODEL), lambda j, i: (i, 0)),
                  pl.BlockSpec((1, tm, W_COLS), lambda j, i: (j // 2, i, j % 2))],
        out_specs=pl.BlockSpec((1, D_MODEL, W_COLS), lambda j, i: (j, 0, 0)),
        out_shape=jax.ShapeDtypeStruct((N_DEV, D_MODEL, W_COLS), F32),
        compiler_params=_params("parallel", "arbitrary"),
    )(u, dproj)


def _rms_bwd(x, gain, dy):
    r = _rms_scale(x)
    xh = x * r
    a = dy * gain
    return r * (a - xh * jnp.mean(a * xh, axis=-1, keepdims=True)), dy * xh


def sb_attn_bwd(proj4, do3, o3, q_gain, k_gain, dproj4, rider=None):
    _, b, s, _ = proj4.shape
    blk = ATTN_BLOCK
    nq = s // blk

    def body(q_ref, k_ref, v_ref, do_ref, o_ref, qg_ref, kg_ref, _alias, d_ref, dqg_ref, dkg_ref, qb, kb, vb, dob):
        _sb_load_normed(q_ref, qg_ref, qb)
        _sb_load_normed(k_ref, kg_ref, kb)
        for h in range(SB_HEADS_PER_STEP):
            cols = slice(h * HEAD_DIM, (h + 1) * HEAD_DIM)
            vb[h] = v_ref[0, 0, :, cols].astype(BF16)
            dob[h] = do_ref[0, :, cols].astype(BF16)
        d_ref[...] = jnp.zeros_like(d_ref)
        row = lax.broadcasted_iota(jnp.int32, (blk, blk), 0)
        col = lax.broadcasted_iota(jnp.int32, (blk, blk), 1)
        tri_lt = col < row
        suffix = (row > col).astype(BF16)
        suffix_incl = (row >= col).astype(BF16)

        def q_group(m, _):
            chains = _sb_chains(m)
            qis, dois, deltas = [], [], []
            for h, i in chains:
                rows_i = pl.ds(pl.multiple_of(i * blk, blk), blk)
                qis.append(qb[h, rows_i, :])
                dois.append(dob[h, rows_i, :])
                deltas.append(jnp.sum(dois[-1].astype(F32) * o_ref[0, rows_i, h * HEAD_DIM:(h + 1) * HEAD_DIM],
                                      axis=1, keepdims=True))

            def k_step(state):
                done, cs, cgs, dqs = state
                where = [_sb_key_rows(i, done) for _, i in chains]
                kjs = [kb[h, rows, :] for (h, _), (rows, _) in zip(chains, where)]
                zs = [_sb_logits(qi, kj) for qi, kj in zip(qis, kjs)]
                dws = [_dot(doi, vb[h, rows, :], NT) for (h, _), (rows, _), doi in zip(chains, where, dois)]
                scored = [_sb_scores(z, done == 0, live, tri_lt) for z, (_, live) in zip(zs, where)]
                afters = [_dot_exact(log_skip, suffix) for log_skip, _, _ in scored]
                wbs = [jnp.where(valid, jnp.exp(log_beta + after + c), 0.0).astype(BF16)
                       for (_, log_beta, valid), after, c in zip(scored, afters, cs)]
                gs = [dw * wb.astype(F32) for dw, wb in zip(dws, wbs)]
                befores = [delta - (_dot_exact(g, suffix_incl) + cg) for g, delta, cg in zip(gs, deltas, cgs)]
                dzbs = []
                for (_, log_beta, valid), g, before in zip(scored, gs, befores):
                    beta = jnp.exp(log_beta)
                    dz = jnp.where(valid, g * (1.0 - beta) - before * beta, 0.0) * (HEAD_DIM ** -0.5)
                    dzbs.append(dz.astype(BF16))
                new_dqs = [dq + _dot(dzb, kj) for dq, dzb, kj in zip(dqs, dzbs, kjs)]
                for (h, _), (rows, _), wb, doi, dzb, qi in zip(chains, where, wbs, dois, dzbs, qis):
                    cols = slice(h * HEAD_DIM, (h + 1) * HEAD_DIM)
                    d_ref[2, 0, rows, cols] += _dot(wb, doi, TN)
                    d_ref[1, 0, rows, cols] += _dot(dzb, qi, TN)
                new_cs = [c + jnp.sum(log_skip, axis=1, keepdims=True) for (log_skip, _, _), c in zip(scored, cs)]
                new_cgs = [cg + jnp.sum(g, axis=1, keepdims=True) for g, cg in zip(gs, cgs)]
                return done + 1, tuple(new_cs), tuple(new_cgs), tuple(new_dqs)

            n = len(chains)
            zero = (jnp.zeros((blk, 1), F32),) * n
            _, _, _, dqs = lax.while_loop(functools.partial(_sb_keys_left, chains), k_step,
                                          (jnp.int32(0), zero, zero, (jnp.zeros((blk, HEAD_DIM), F32),) * n))
            for (h, i), dq in zip(chains, dqs):
                d_ref[0, 0, pl.ds(pl.multiple_of(i * blk, blk), blk), h * HEAD_DIM:(h + 1) * HEAD_DIM] = dq
            return 0

        lax.fori_loop(0, nq // SB_QBLOCKS_PER_STEP, q_group, 0)

        def norm_block(i, carry):
            rows = pl.ds(pl.multiple_of(i * blk, blk), blk)
            out = []
            for h in range(SB_HEADS_PER_STEP):
                cols = slice(h * HEAD_DIM, (h + 1) * HEAD_DIM)
                for part, src_ref, gain_ref in ((0, q_ref, qg_ref), (1, k_ref, kg_ref)):
                    dx, pg = _rms_bwd(src_ref[0, 0, rows, cols], gain_ref[...], d_ref[part, 0, rows, cols])
                    d_ref[part, 0, rows, cols] = dx
                    out.append(carry[len(out)] + jnp.sum(pg, axis=0, keepdims=True))
            return tuple(out)

        sums = lax.fori_loop(0, nq, norm_block, (jnp.zeros((1, HEAD_DIM), F32),) * (2 * SB_HEADS_PER_STEP))
        for h in range(SB_HEADS_PER_STEP):
            dqg_ref[0, h] = sums[2 * h]
            dkg_ref[0, h] = sums[2 * h + 1]

    head_row = jax.ShapeDtypeStruct((b, N_HEADS, 1, HEAD_DIM), F32)
    return _pcall_riding(
        body, rider, (proj4, proj4, proj4, do3, o3, q_gain, k_gain, dproj4),
        name="sb_attn_bwd", grid=(b, N_HEADS // SB_HEADS_PER_STEP),
        in_specs=[_sb_group_spec(s, 0), _sb_group_spec(s, 1), _sb_group_spec(s, 2),
                  _sb_seq_group_spec(s), _sb_seq_group_spec(s), _GAIN_SPEC, _GAIN_SPEC,
                  pl.BlockSpec(memory_space=pl.ANY)],
        out_specs=[pl.BlockSpec((3, 1, s, SB_GROUP_COLS), lambda b, g: (0, b, 0, g)),
                   _SB_GROUP_ROW_SPEC, _SB_GROUP_ROW_SPEC],
        out_shape=[jax.ShapeDtypeStruct(dproj4.shape, F32), head_row, head_row],
        scratch_shapes=[pltpu.VMEM((SB_HEADS_PER_STEP, s, HEAD_DIM), BF16)] * 4,
        input_output_aliases={7: 0}, semantics=("parallel", "parallel"))


def hgrn2_bwd(proj4, don3, oraw3, states, lbl4, o_gain, dproj4, rider=None):
    _, b, s, _ = proj4.shape
    nchunk = s // HG_CHUNK
    c = HG_CHUNK
    subs = range(HG_CHUNK // HG_SUB)
    ngroup = nchunk // HG_UNROLL

    def body(q_ref, f_ref, i_ref, don_ref, oraw_ref, st_ref, lbl_ref, og_ref, _alias, d_ref, dog_ref, dlb_ref):
        incl, lower, before_sub, upper = _hg_masks()
        lb, oml = _hg_lower_bound(lbl_ref)
        last_row = lax.broadcasted_iota(jnp.int32, (c, HEAD_DIM), 0) == c - 1

        def group(m, carry):
            dst, dog_acc, dlb_acc = carry
            ns, rows = _hg_group_rows(ngroup - 1 - m)
            ns, rows = ns[::-1], rows[::-1]
            vs = [_hg_gates(q_ref[0, 0, r, :], f_ref[0, 0, r, :], lb, oml) for r in rows]
            inps = [i_ref[0, 0, r, :].astype(BF16) for r in rows]
            sts = [st_ref[0, 0, n] for n in ns]
            gcs = [_dot_exact(v["logf"], lower, left=True) for v in vs]
            grs = [_dot_exact(v["logf"], before_sub, left=True) for v in vs]
            e_qs = [jnp.exp(gc - gr) for gc, gr in zip(gcs, grs)]
            a_s, qdbs, ksbs, ess = _hg_intra([v["q"] * e for v, e in zip(vs, e_qs)], [v["k"] for v in vs],
                                             gcs, grs, incl)
            e_gcs = [jnp.exp(gc) for gc in gcs]
            gls = [gc[c - 1:c, :] for gc in gcs]
            e_gls = [jnp.exp(gl) for gl in gls]
            e_ks = [jnp.exp(gl - gc) for gl, gc in zip(gls, gcs)]
            normed = [_rms_bwd(oraw_ref[0, r, :], og_ref[...], don_ref[0, r, :]) for r in rows]
            dobs = [do.astype(BF16) for do, _ in normed]
            dabs = [jnp.where(incl, _dot(dob, inp, NT), 0.0).astype(BF16) for dob, inp in zip(dobs, inps)]
            adds = [_dot(dob, (v["q"] * e).astype(BF16), TN) for dob, v, e in zip(dobs, vs, e_gcs)]
            dq_inters = [_dot(dob, st.astype(BF16)) * e for dob, st, e in zip(dobs, sts, e_gcs)]
            dqds = [jnp.concatenate([_dot(dab[sub * HG_SUB:(sub + 1) * HG_SUB, :], ksb[sub]) for sub in subs], axis=0)
                    for dab, ksb in zip(dabs, ksbs)]
            dkss = [[_dot(dab[sub * HG_SUB:(sub + 1) * HG_SUB, :], qdb[sub * HG_SUB:(sub + 1) * HG_SUB, :], TN)
                     for sub in subs] for dab, qdb in zip(dabs, qdbs)]
            dsts = []
            for e_gl, add in zip(e_gls, adds):
                dsts.append(dst)
                dst = dst * e_gl + add
            dstbs = [d.astype(BF16) for d in dsts]
            dis = [_dot(a.astype(BF16), dob, TN) + _dot((v["k"] * e_k).astype(BF16), dstb, NT)
                   for a, dob, v, e_k, dstb in zip(a_s, dobs, vs, e_ks, dstbs)]
            dk_inters = [_dot(inp, dstb) * e_k for inp, dstb, e_k in zip(inps, dstbs, e_ks)]
            for u, r in enumerate(rows):
                v, q, k = vs[u], vs[u]["q"], vs[u]["k"]
                dk, dgc_k = dk_inters[u], jnp.zeros((c, HEAD_DIM), F32)
                for sub in subs:
                    dk = dk + dkss[u][sub] * ess[u][sub]
                    dgc_k = dgc_k + dkss[u][sub] * ksbs[u][sub].astype(F32)
                dq = dqds[u] * e_qs[u] + dq_inters[u]
                at_last = (jnp.sum(k * dk_inters[u], axis=0, keepdims=True)
                           + e_gls[u] * jnp.sum(sts[u] * dsts[u], axis=0, keepdims=True))
                dgc = ((qdbs[u].astype(F32) * dqds[u] - dgc_k) + (q * dq_inters[u] - k * dk_inters[u])
                       + jnp.where(last_row, at_last, 0.0))
                dlf_f = _dot_exact(dgc, upper, left=True) / v["f"]
                d_ref[0, 0, r, :] = dq * (v["sq"] * (1.0 + v["qp"] * (1.0 - v["sq"])))
                d_ref[1, 0, r, :] = (dlf_f - dk) * (oml * v["sf"] * v["sfn"])
                d_ref[2, 0, r, :] = dis[u]
                dlb_acc = dlb_acc + jnp.sum((dlf_f - dk) * v["sfn"], axis=0, keepdims=True)
                dog_acc = dog_acc + jnp.sum(normed[u][1], axis=0, keepdims=True)
            return dst, dog_acc, dlb_acc

        zero = jnp.zeros((1, HEAD_DIM), F32)
        _, dog, dlb = lax.fori_loop(0, ngroup, group, (jnp.zeros((HEAD_DIM, HEAD_DIM), F32), zero, zero))
        dog_ref[0, 0] = dog
        dlb_ref[0, 0] = dlb

    head_row = jax.ShapeDtypeStruct((b, N_HEADS, 1, HEAD_DIM), F32)
    return _pcall_riding(
        body, rider, (proj4, proj4, proj4, don3, oraw3, states, lbl4, o_gain, dproj4),
        name="hgrn2_bwd", grid=(b, N_HEADS),
        in_specs=[_head_spec(s, 0), _head_spec(s, 1), _head_spec(s, 2), _seq_spec(s), _seq_spec(s),
                  _state_spec(nchunk), pl.BlockSpec((2, 1, 1, HEAD_DIM), lambda b, h: (0, h, 0, 0)), _GAIN_SPEC,
                  pl.BlockSpec(memory_space=pl.ANY)],
        out_specs=[pl.BlockSpec((3, 1, s, HEAD_DIM), lambda b, h: (0, b, 0, h)), _HEAD_ROW_SPEC, _HEAD_ROW_SPEC],
        out_shape=[jax.ShapeDtypeStruct(dproj4.shape, F32), head_row, head_row],
        input_output_aliases={8: 0}, semantics=("parallel", "parallel"))


def local_step(x, target, sb_norm, wsi, sb_q_gain, sb_k_gain, hg_o_gain, hg_lb_logits, wso_mine, whi_mine, who_mine,
               hg_norm_mine):
    b, s, _ = x.shape
    t = b * s
    x2 = x.reshape(t, D_MODEL)
    tg2 = target.reshape(t, D_MODEL)
    lbl4 = hg_lb_logits.reshape(2, N_HEADS, 1, HEAD_DIM)
    four = (4, b, s, D_MODEL)
    three = (b, s, D_MODEL)
    rows8 = (N_DEV, W_ROWS, D_MODEL)

    (proj0, u0), (wso,) = rms_inproj(x2, sb_norm, wsi, "sb_inproj", _Rider([wso_mine], scatter=False))
    wso = wso.reshape(D_MODEL, D_MODEL)
    o0, (whi, who, hgn) = sb_attn_fwd(proj0.reshape(four), sb_q_gain, sb_k_gain,
                                      _Rider([whi_mine, who_mine, hg_norm_mine], scatter=False))
    who = who.reshape(D_MODEL, D_MODEL)
    hg_norm_full = hgn[:, 0, :].reshape(1, D_MODEL)
    o0 = o0.reshape(t, D_MODEL)
    h1 = gate_outproj(o0, proj0, wso, x2, None, "sb_outproj")
    (proj1, u1), _ = rms_inproj(h1, hg_norm_full, whi, "hg_inproj")
    o1, o1_raw, states = hgrn2_fwd(proj1.reshape(four), lbl4, hg_o_gain)
    o1 = o1.reshape(t, D_MODEL)
    dh2, loss_parts = gate_outproj(o1, proj1, who, h1, tg2, "hg_outproj_loss")

    do1, dproj1, g_who = outproj_bwd(dh2, who, o1, proj1, "hg_outproj_bwd")
    (dproj1, g_og, g_lb), (p_who,) = hgrn2_bwd(proj1.reshape(four), do1.reshape(three), o1_raw, states, lbl4,
                                               hg_o_gain, dproj1.reshape(four),
                                               _Rider([g_who.reshape(rows8)], scatter=True))
    dproj1 = dproj1.reshape(4, t, D_MODEL)
    (dh1, g_hgn), _ = inproj_bwd_dx(dproj1, whi, h1, hg_norm_full, dh2, "hg_inproj_bwd_dx")
    g_whi = inproj_bwd_dw(u1, dproj1, "hg_inproj_bwd_dw")

    do0, dproj0, g_wso = outproj_bwd(dh1, wso, o0, proj0, "sb_outproj_bwd")
    (dproj0, g_qg, g_kg), (p_whi, p_wso) = sb_attn_bwd(proj0.reshape(four), do0.reshape(three), o0.reshape(three),
                                                       sb_q_gain, sb_k_gain, dproj0.reshape(four),
                                                       _Rider([g_whi, g_wso.reshape(rows8)], scatter=True))
    dproj0 = dproj0.reshape(4, t, D_MODEL)
    g_wsi = inproj_bwd_dw(u0, dproj0, "sb_inproj_bwd_dw")
    (gx, g_sbn), (p_wsi,) = inproj_bwd_dx(dproj0, wsi, x2, sb_norm, dh1, "sb_inproj_bwd_dx",
                                          _Rider([g_wsi], scatter=True))
    return dict(loss_parts=loss_parts, gx=gx.reshape(three), p_wsi=p_wsi, p_wso=p_wso, p_whi=p_whi, p_who=p_who,
                g_sbn=g_sbn, g_hgn=g_hgn, g_qg=g_qg, g_kg=g_kg, g_og=g_og, g_lb=g_lb)


def gather_first_weights(w_si, w_so, w_hi, w_ho, hg_norm):
    def body(si_ref, so_ref, hi_ref, ho_ref, hn_ref, o_si, so_b, hi_b, ho_b, hn_b, si_b, send_sems, recv_sems, local_sems):
        for src, buf in ((si_ref, si_b), (so_ref, so_b), (hi_ref, hi_b), (ho_ref, ho_b)):
            buf[...] = src[...].astype(BF16)
        hn_b[...] = jnp.broadcast_to(hn_ref[...], hn_b.shape)
        pairs = [((lambda p: si_b), o_si)]
        pos = _mesh_pos()
        _exchange_start(pairs, send_sems, recv_sems, local_sems, pos)
        _exchange_wait(pairs, send_sems, recv_sems, local_sems, pos)

    return _pcall(
        body, name="gather_first_weights",
        in_specs=[_VMEM] * 5, out_specs=[_ANY] + [_VMEM] * 4,
        out_shape=[jax.ShapeDtypeStruct((N_DEV,) + w_si.shape, BF16), jax.ShapeDtypeStruct(w_so.shape, BF16),
                   jax.ShapeDtypeStruct(w_hi.shape, BF16), jax.ShapeDtypeStruct(w_ho.shape, BF16),
                   jax.ShapeDtypeStruct((8, HEAD_DIM), F32)],
        scratch_shapes=[pltpu.VMEM(w_si.shape, BF16)] + _exchange_sems(1),
        compiler_params=pltpu.CompilerParams(vmem_limit_bytes=VMEM_LIMIT_BYTES),
    )(w_si, w_so, w_hi, w_ho, hg_norm)


def _adamw(w, g, m, v):
    m = ADAM_B1 * m + (1.0 - ADAM_B1) * g
    v = ADAM_B2 * v + (1.0 - ADAM_B2) * (g * g)
    m_hat = m / (1.0 - ADAM_B1 ** ADAM_STEP)
    v_hat = v / (1.0 - ADAM_B2 ** ADAM_STEP)
    delta = -ADAM_LR * (m_hat / (jnp.sqrt(v_hat) + ADAM_EPS) + ADAM_WD * w)
    return delta, m, v


def reduce_adamw(parts, w, m, v, name):
    _, r, c = parts.shape
    tr = _row_tile(r, 256)

    def body(p_ref, w_ref, m_ref, v_ref, g_ref, d_ref, m2_ref, v2_ref):
        g = p_ref[0]
        for dev in range(1, N_DEV):
            g = g + p_ref[dev]
        g_ref[...] = g
        d_ref[...], m2_ref[...], v2_ref[...] = _adamw(w_ref[...], g, m_ref[...], v_ref[...])

    tile = pl.BlockSpec((tr, c), lambda i: (i, 0))
    return _pcall(
        body, name=name, grid=(r // tr,),
        in_specs=[pl.BlockSpec((N_DEV, tr, c), lambda i: (0, i, 0)), tile, tile, tile],
        out_specs=[tile] * 4, out_shape=[jax.ShapeDtypeStruct((r, c), F32)] * 4,
        compiler_params=_params("parallel"),
    )(parts, w, m, v)


PACK_ROWS = 32
ROW_SBN, ROW_HGN, ROW_LB, ROW_QG, ROW_KG, ROW_OG, ROW_LOSS = 0, 8, 16, 24, 25, 26, 27


def small_update(g_sbn, g_hgn, g_lb, g_qg, g_kg, g_og, loss_parts, small):
    n_in = 7 + len(small)

    def body(*refs):
        sbn_ref, hgn_ref, lb_ref, qg_ref, kg_ref, og_ref, loss_ref = refs[:7]
        wmv = refs[7:n_in]
        outs = refs[n_in:n_in + 25]
        pack, gath, tot, send_sems, recv_sems, local_sems = refs[n_in + 25:]
        pos = _mesh_pos()
        me = _linear(pos)
        pack[...] = jnp.zeros_like(pack)
        pack[ROW_SBN:ROW_SBN + 8, :] = sbn_ref[...]
        pack[ROW_HGN:ROW_HGN + 8, :] = hgn_ref[...]
        pack[ROW_LB:ROW_LB + 8, :] = jnp.sum(lb_ref[...], axis=0)
        pack[ROW_QG:ROW_QG + 1, :] = jnp.sum(qg_ref[...], axis=0, keepdims=True)
        pack[ROW_KG:ROW_KG + 1, :] = jnp.sum(kg_ref[...], axis=0, keepdims=True)
        pack[ROW_OG:ROW_OG + 1, :] = jnp.sum(og_ref[...], axis=0, keepdims=True)
        pack[ROW_LOSS:ROW_LOSS + 1, :] = jnp.sum(loss_ref[...], axis=0)[0:1, :]
        _exchange_start([((lambda p: pack), gath)], send_sems, recv_sems, local_sems, pos)
        _exchange_wait([((lambda p: pack), gath)], send_sems, recv_sems, local_sems, pos)
        total = gath[0]
        for dev in range(1, N_DEV):
            total = total + gath[dev]
        tot[...] = total
        outs[0][...] = jnp.broadcast_to(tot[ROW_LOSS:ROW_LOSS + 1, :], (8, HEAD_DIM))
        l0 = wmv[15][0:8, :]
        l1 = wmv[15][8:16, :]
        p1, p0 = _sigmoid_pair(l1 - l0)
        d_l1 = p0 * p1 * tot[ROW_LB:ROW_LB + 8, :]
        grads = [tot[ROW_SBN:ROW_SBN + 8, :], tot[ROW_QG:ROW_QG + 1, :], tot[ROW_KG:ROW_KG + 1, :],
                 tot[pl.ds(ROW_HGN + me, 1), :], tot[ROW_OG:ROW_OG + 1, :],
                 jnp.concatenate([-d_l1, d_l1], axis=0)]
        for i, g in enumerate(grads):
            w_ref, m_ref, v_ref = wmv[3 * i:3 * i + 3]
            o = outs[1 + 4 * i:5 + 4 * i]
            o[0][...] = g
            o[1][...], o[2][...], o[3][...] = _adamw(w_ref[...], g, m_ref[...], v_ref[...])

    out_shape = [jax.ShapeDtypeStruct((8, HEAD_DIM), F32)]
    for i in range(6):
        out_shape += [jax.ShapeDtypeStruct(small[3 * i].shape, F32)] * 4
    return _pcall(
        body, name="small_update",
        in_specs=[_VMEM] * n_in, out_specs=[_VMEM] * 25, out_shape=out_shape,
        scratch_shapes=[pltpu.VMEM((PACK_ROWS, HEAD_DIM), F32), pltpu.VMEM((N_DEV, PACK_ROWS, HEAD_DIM), F32),
                        pltpu.VMEM((PACK_ROWS, HEAD_DIM), F32)] + _exchange_sems(1),
    )(g_sbn, g_hgn, g_lb, g_qg, g_kg, g_og, loss_parts, *small)


def kernel(x, sb_norm, sb_w_in, sb_q_gain, sb_k_gain, sb_w_out, hg_norm, hg_w_in, hg_o_gain, hg_w_out, hg_lb_logits, loss_target, m_sb_norm, m_sb_w_in, m_sb_q_gain, m_sb_k_gain, m_sb_w_out, m_hg_norm, m_hg_w_in, m_hg_o_gain, m_hg_w_out, m_hg_lb_logits, v_sb_norm, v_sb_w_in, v_sb_q_gain, v_sb_k_gain, v_sb_w_out, v_hg_norm, v_hg_w_in, v_hg_o_gain, v_hg_w_out, v_hg_lb_logits):
    b = x.shape[0]
    wsi, wso_mine, whi_mine, who_mine, hg_norm_mine = gather_first_weights(
        sb_w_in[0], sb_w_out[0], hg_w_in[0], hg_w_out[0], hg_norm)
    r = local_step(x, loss_target, sb_norm, wsi, sb_q_gain, sb_k_gain, hg_o_gain, hg_lb_logits,
                   wso_mine, whi_mine, who_mine, hg_norm_mine)
    big = {}
    for name, p, w, m, v in (("sb_w_in", r["p_wsi"], sb_w_in, m_sb_w_in, v_sb_w_in),
                             ("sb_w_out", r["p_wso"], sb_w_out, m_sb_w_out, v_sb_w_out),
                             ("hg_w_in", r["p_whi"], hg_w_in, m_hg_w_in, v_hg_w_in),
                             ("hg_w_out", r["p_who"], hg_w_out, m_hg_w_out, v_hg_w_out)):
        big[name] = [o[None] for o in reduce_adamw(p, w[0], m[0], v[0], "adamw_" + name)]

    def rows8(a):
        return a.reshape(8, HEAD_DIM)

    def rows16(a):
        return a.reshape(16, HEAD_DIM)

    small_in = [rows8(sb_norm), rows8(m_sb_norm), rows8(v_sb_norm),
                sb_q_gain, m_sb_q_gain, v_sb_q_gain,
                sb_k_gain, m_sb_k_gain, v_sb_k_gain,
                hg_norm, m_hg_norm, v_hg_norm,
                hg_o_gain, m_hg_o_gain, v_hg_o_gain,
                rows16(hg_lb_logits), rows16(m_hg_lb_logits), rows16(v_hg_lb_logits)]
    so = small_update(rows8(r["g_sbn"]), rows8(r["g_hgn"]), r["g_lb"].reshape(b, N_HEADS, HEAD_DIM),
                      r["g_qg"].reshape(b * N_HEADS, HEAD_DIM), r["g_kg"].reshape(b * N_HEADS, HEAD_DIM),
                      r["g_og"].reshape(b * N_HEADS, HEAD_DIM), r["loss_parts"], small_in)
    loss = so[0][0, 0]
    shapes = {"sb_norm": (1, D_MODEL), "sb_q_gain": (1, HEAD_DIM), "sb_k_gain": (1, HEAD_DIM),
              "hg_norm": (1, HEAD_DIM), "hg_o_gain": (1, HEAD_DIM), "hg_lb_logits": (2, D_MODEL)}
    small = {}
    for i, name in enumerate(("sb_norm", "sb_q_gain", "sb_k_gain", "hg_norm", "hg_o_gain", "hg_lb_logits")):
        small[name] = [o.reshape(shapes[name]) for o in so[1 + 4 * i:5 + 4 * i]]
    order = ("sb_norm", "sb_w_in", "sb_q_gain", "sb_k_gain", "sb_w_out",
             "hg_norm", "hg_w_in", "hg_o_gain", "hg_w_out", "hg_lb_logits")
    res = {**big, **small}
    return (loss, r["gx"]) + tuple(res[n][j] for j in range(4) for n in order)
```

```python
import functools

import jax
import jax.numpy as jnp
from jax import lax
from jax.experimental import pallas as pl
from jax.experimental.pallas import tpu as pltpu

F32 = jnp.float32
BF16 = jnp.bfloat16

N_DEV = 8
D_MODEL = 1024
N_HEADS = 8
HEAD_DIM = 128
RMS_EPS = 1e-6
ATTN_BLOCK = 128
HG_CHUNK = 64
HG_SUB = 16
HG_UNROLL = 8
EXP_CLAMP = 80.0
SB_HEADS_PER_STEP = 2
SB_QBLOCKS_PER_STEP = 4
SB_GROUP_COLS = SB_HEADS_PER_STEP * 128
SB_LOG_WEIGHT_FLOOR = -104.0
VMEM_LIMIT_BYTES = 48 * 1024 * 1024
W_COLS = 4 * D_MODEL // N_DEV
W_ROWS = D_MODEL // N_DEV

ADAM_LR = 0.001
ADAM_B1 = 0.9
ADAM_B2 = 0.999
ADAM_EPS = 1e-08
ADAM_WD = 0.01
ADAM_STEP = 10

NT = (((1,), (1,)), ((), ()))
TN = (((0,), (0,)), ((), ()))
NN = (((1,), (0,)), ((), ()))


def _pcall(body, *, name, **kw):
    return pl.pallas_call(body, name=name, **kw)


def _params(*sem):
    return pltpu.CompilerParams(dimension_semantics=sem, vmem_limit_bytes=VMEM_LIMIT_BYTES)


def _dot(a, b, dims=NN):
    return lax.dot_general(a, b, dims, preferred_element_type=F32)


def _dot_exact(a, m, dims=NN, left=False):
    hi = a.astype(BF16)
    lo = (a - hi.astype(F32)).astype(BF16)
    if left:
        return _dot(m, hi, dims) + _dot(m, lo, dims)
    return _dot(hi, m, dims) + _dot(lo, m, dims)


def _split(a):
    hi = a.astype(BF16)
    return hi, (a - hi.astype(F32)).astype(BF16)


def _dot3(a, b, dims=NN):
    return _dot(a[0], b[0], dims) + (_dot(a[0], b[1], dims) + _dot(a[1], b[0], dims))


def _sigmoid(x):
    return 1.0 / (1.0 + jnp.exp(-x))


def _sigmoid_pair(x):
    e = jnp.exp(-jnp.abs(x))
    big = 1.0 / (1.0 + e)
    small = e * big
    pos = x >= 0
    return jnp.where(pos, big, small), jnp.where(pos, small, big)


def _rms_scale(x):
    return lax.rsqrt(jnp.mean(x * x, axis=-1, keepdims=True) + RMS_EPS)


def _row_tile(t, want):
    return want if t % want == 0 else t


MESH = pl.DeviceIdType.MESH
N_PEERS = N_DEV - 1
_ANY = pl.BlockSpec(memory_space=pl.ANY)
_VMEM = pl.BlockSpec(memory_space=pltpu.VMEM)


def _mesh_pos():
    return lax.axis_index("x"), lax.axis_index("y"), lax.axis_index("c")


def _linear(pos):
    return 4 * pos[0] + 2 * pos[1] + pos[2]


def _peer(pos, k):
    flips = ((k + 1) >> 2 & 1, (k + 1) >> 1 & 1, (k + 1) & 1)
    return tuple(1 - p if f else p for p, f in zip(pos, flips))


def _exchange_copies(pairs, send_sems, recv_sems, local_sems, pos, landing):
    me = _linear(pos)
    local, remote = [], []
    for a, (src_of, dst) in enumerate(pairs):
        local.append(pltpu.make_async_copy(src_of(me), dst.at[me], local_sems.at[a]))
        for k in range(N_PEERS):
            peer = _peer(pos, k)
            remote.append(pltpu.make_async_remote_copy(
                src_ref=src_of(_linear(peer)), dst_ref=dst.at[_linear(peer) if landing else me],
                send_sem=send_sems.at[a, k], recv_sem=recv_sems.at[a, k], device_id=peer, device_id_type=MESH))
    return local, remote


def _exchange_start(pairs, send_sems, recv_sems, local_sems, pos):
    local, sent = _exchange_copies(pairs, send_sems, recv_sems, local_sems, pos, landing=False)
    for copy in local + sent:
        copy.start()


def _exchange_wait(pairs, send_sems, recv_sems, local_sems, pos):
    local, landed = _exchange_copies(pairs, send_sems, recv_sems, local_sems, pos, landing=True)
    for copy in landed:
        copy.wait_recv()
        copy.wait_send()
    for copy in local:
        copy.wait()


def _exchange_sems(n):
    return [pltpu.SemaphoreType.DMA((n, N_PEERS)), pltpu.SemaphoreType.DMA((n, N_PEERS)),
            pltpu.SemaphoreType.DMA((n,))]


class _Rider:
    def __init__(self, arrays, scatter):
        self.arrays = list(arrays)
        self.scatter = scatter
        self.out_shapes = [jax.ShapeDtypeStruct(a.shape if scatter else (N_DEV,) + a.shape, a.dtype)
                           for a in self.arrays]

    def pairs(self, in_refs, out_refs):
        if self.scatter:
            return [((lambda p, r=r: r.at[p]), o) for r, o in zip(in_refs, out_refs)]
        return [((lambda p, r=r: r), o) for r, o in zip(in_refs, out_refs)]


def _pcall_riding(body, rider, args, *, name, grid, in_specs, out_specs, out_shape, semantics, scratch_shapes=(),
                  input_output_aliases=None):
    aliases = input_output_aliases or {}
    if rider is None:
        outs = _pcall(body, name=name, grid=grid, in_specs=list(in_specs), out_specs=list(out_specs),
                      out_shape=list(out_shape), scratch_shapes=list(scratch_shapes), input_output_aliases=aliases,
                      compiler_params=_params(*semantics))(*args)
        return list(outs), []
    n_in, n_out, n_scr, n_r = len(in_specs), len(out_specs), len(scratch_shapes), len(rider.arrays)

    def riding(*refs):
        ins, refs = refs[:n_in], refs[n_in:]
        rider_in, refs = refs[:n_r], refs[n_r:]
        outs, refs = refs[:n_out], refs[n_out:]
        rider_out, refs = refs[:n_r], refs[n_r:]
        scratch, sems = refs[:n_scr], refs[n_scr:]
        pairs = rider.pairs(rider_in, rider_out)
        first = functools.reduce(jnp.logical_and, [pl.program_id(a) == 0 for a in range(len(grid))])
        last = functools.reduce(jnp.logical_and, [pl.program_id(a) == g - 1 for a, g in enumerate(grid)])

        @pl.when(first)
        def _():
            _exchange_start(pairs, *sems, _mesh_pos())

        body(*ins, *outs, *scratch)

        @pl.when(last)
        def _():
            _exchange_wait(pairs, *sems, _mesh_pos())

    outs = _pcall(riding, name=name, grid=grid, in_specs=list(in_specs) + [_ANY] * n_r,
                  out_specs=list(out_specs) + [_ANY] * n_r, out_shape=list(out_shape) + rider.out_shapes,
                  scratch_shapes=list(scratch_shapes) + _exchange_sems(n_r), input_output_aliases=aliases,
                  compiler_params=_params(*(("arbitrary",) * len(grid))))(*args, *rider.arrays)
    return list(outs[:n_out]), list(outs[n_out:])


def rms_inproj(x2, gain, wg, name, rider=None):
    t = x2.shape[0]
    tm = _row_tile(t, 256)

    def body(x_ref, g_ref, w_ref, proj_ref, u_ref):
        x = x_ref[...]
        u = (x * _rms_scale(x) * g_ref[...]).astype(BF16)
        u_ref[...] = u
        for p in range(N_DEV):
            proj_ref[p // 2, :, (p % 2) * W_COLS:(p % 2 + 1) * W_COLS] = _dot(u, w_ref[p])

    return _pcall_riding(
        body, rider, (x2, gain, wg), name=name,
        grid=(t // tm,),
        in_specs=[pl.BlockSpec((tm, D_MODEL), lambda i: (i, 0)),
                  pl.BlockSpec((1, D_MODEL), lambda i: (0, 0)),
                  pl.BlockSpec((N_DEV, D_MODEL, W_COLS), lambda i: (0, 0, 0))],
        out_specs=[pl.BlockSpec((4, tm, D_MODEL), lambda i: (0, i, 0)),
                   pl.BlockSpec((tm, D_MODEL), lambda i: (i, 0))],
        out_shape=[jax.ShapeDtypeStruct((4, t, D_MODEL), F32),
                   jax.ShapeDtypeStruct((t, D_MODEL), BF16)],
        semantics=("parallel",))


def gate_outproj(o2, proj, w_out, resid, target, name):
    t = o2.shape[0]
    tm = _row_tile(t, 256)
    with_loss = target is not None

    def body(o_ref, gate_ref, w_ref, r_ref, *rest):
        g = gate_ref[0]
        og = (o_ref[...] * (g * _sigmoid(g))).astype(BF16)
        h = r_ref[...] + _dot(og, w_ref[...])
        if with_loss:
            t_ref, dh_ref, loss_ref = rest
            err = h - t_ref[...]
            dh_ref[...] = err * (1.0 / D_MODEL)
            part = 0.5 * jnp.sum(jnp.mean(err * err, axis=-1, keepdims=True))
            loss_ref[...] = jnp.full(loss_ref.shape, part, F32)
        else:
            (h_ref,) = rest
            h_ref[...] = h

    row = pl.BlockSpec((tm, D_MODEL), lambda i: (i, 0))
    in_specs = [row,
                pl.BlockSpec((1, tm, D_MODEL), lambda i: (3, i, 0)),
                pl.BlockSpec((D_MODEL, D_MODEL), lambda i: (0, 0)),
                row]
    args = [o2, proj, w_out, resid]
    if with_loss:
        in_specs.append(row)
        args.append(target)
        out_specs = [row, pl.BlockSpec((1, 8, 128), lambda i: (i, 0, 0))]
        out_shape = [jax.ShapeDtypeStruct((t, D_MODEL), F32),
                     jax.ShapeDtypeStruct((t // tm, 8, 128), F32)]
    else:
        out_specs = row
        out_shape = jax.ShapeDtypeStruct((t, D_MODEL), F32)
    return _pcall(body, name=name, grid=(t // tm,), in_specs=in_specs, out_specs=out_specs,
                  out_shape=out_shape, compiler_params=_params("parallel"))(*args)


def _head_spec(s, part):
    return pl.BlockSpec((1, 1, s, HEAD_DIM), lambda b, h: (part, b, 0, h))


def _seq_spec(s):
    return pl.BlockSpec((1, s, HEAD_DIM), lambda b, h: (b, 0, h))


_GAIN_SPEC = pl.BlockSpec((1, HEAD_DIM), lambda b, h: (0, 0))
_HEAD_ROW_SPEC = pl.BlockSpec((1, 1, 1, HEAD_DIM), lambda b, h: (b, h, 0, 0))


def _sb_group_spec(s, part):
    return pl.BlockSpec((1, 1, s, SB_GROUP_COLS), lambda b, g: (part, b, 0, g))


def _sb_seq_group_spec(s):
    return pl.BlockSpec((1, s, SB_GROUP_COLS), lambda b, g: (b, 0, g))


_SB_GROUP_ROW_SPEC = pl.BlockSpec((1, SB_HEADS_PER_STEP, 1, HEAD_DIM), lambda b, g: (b, g, 0, 0))


def _sb_chains(m):
    return [(h, m * SB_QBLOCKS_PER_STEP + r) for h in range(SB_HEADS_PER_STEP) for r in range(SB_QBLOCKS_PER_STEP)]


def _sb_logits(qi, kj):
    return _dot(qi, kj, NT) * (HEAD_DIM ** -0.5)


def _sb_scores(z, diag, live, tri_lt):
    soft = jnp.log1p(jnp.exp(-jnp.abs(z)))
    valid = jnp.logical_and(live, jnp.logical_or(jnp.logical_not(diag), tri_lt))
    log_skip = jnp.where(valid, -(jnp.maximum(z, 0.0) + soft), 0.0)
    log_beta = jnp.minimum(z, 0.0) - soft
    return log_skip, log_beta, valid


def _sb_keys_left(chains, state):
    done, carries = state[0], state[1]
    worst = None
    for (_, i), c in zip(chains, carries):
        c = jnp.where(done <= i, c, -jnp.inf)
        worst = c if worst is None else jnp.maximum(worst, c)
    return jnp.logical_and(done <= chains[-1][1],
                           jnp.logical_or(done == 0, jnp.max(worst) > SB_LOG_WEIGHT_FLOOR))


def _sb_key_rows(i, done):
    j = i - done
    return pl.ds(pl.multiple_of(jnp.maximum(j, 0) * ATTN_BLOCK, ATTN_BLOCK), ATTN_BLOCK), j >= 0


def _sb_load_normed(src_ref, gain_ref, dst):
    for h in range(SB_HEADS_PER_STEP):
        x = src_ref[0, 0, :, h * HEAD_DIM:(h + 1) * HEAD_DIM]
        dst[h] = (x * _rms_scale(x) * gain_ref[...]).astype(BF16)


def sb_attn_fwd(proj4, q_gain, k_gain, rider=None):
    _, b, s, _ = proj4.shape
    blk = ATTN_BLOCK
    nq = s // blk

    def body(q_ref, k_ref, v_ref, qg_ref, kg_ref, o_ref, qb, kb, vb):
        _sb_load_normed(q_ref, qg_ref, qb)
        _sb_load_normed(k_ref, kg_ref, kb)
        for h in range(SB_HEADS_PER_STEP):
            vb[h] = v_ref[0, 0, :, h * HEAD_DIM:(h + 1) * HEAD_DIM].astype(BF16)
        row = lax.broadcasted_iota(jnp.int32, (blk, blk), 0)
        col = lax.broadcasted_iota(jnp.int32, (blk, blk), 1)
        tri_lt = col < row
        suffix = (row > col).astype(BF16)

        def q_group(m, _):
            chains = _sb_chains(m)
            qis = [qb[h, pl.ds(pl.multiple_of(i * blk, blk), blk), :] for h, i in chains]

            def k_step(state):
                done, cs, accs = state
                where = [_sb_key_rows(i, done) for _, i in chains]
                zs = [_sb_logits(qi, kb[h, rows, :]) for (h, _), qi, (rows, _) in zip(chains, qis, where)]
                scored = [_sb_scores(z, done == 0, live, tri_lt) for z, (_, live) in zip(zs, where)]
                afters = [_dot_exact(log_skip, suffix) for log_skip, _, _ in scored]
                ws = [jnp.where(valid, jnp.exp(log_beta + after + c), 0.0).astype(BF16)
                      for (_, log_beta, valid), after, c in zip(scored, afters, cs)]
                new_accs = [acc + _dot(w, vb[h, rows, :]) for (h, _), (rows, _), w, acc in zip(chains, where, ws, accs)]
                new_cs = [c + jnp.sum(log_skip, axis=1, keepdims=True) for (log_skip, _, _), c in zip(scored, cs)]
                return done + 1, tuple(new_cs), tuple(new_accs)

            n = len(chains)
            _, _, accs = lax.while_loop(
                functools.partial(_sb_keys_left, chains), k_step,
                (jnp.int32(0), (jnp.zeros((blk, 1), F32),) * n, (jnp.zeros((blk, HEAD_DIM), F32),) * n))
            for (h, i), acc in zip(chains, accs):
                o_ref[0, pl.ds(pl.multiple_of(i * blk, blk), blk), h * HEAD_DIM:(h + 1) * HEAD_DIM] = acc
            return 0

        lax.fori_loop(0, nq // SB_QBLOCKS_PER_STEP, q_group, 0)

    (o,), extra = _pcall_riding(
        body, rider, (proj4, proj4, proj4, q_gain, k_gain),
        name="sb_attn_fwd", grid=(b, N_HEADS // SB_HEADS_PER_STEP),
        in_specs=[_sb_group_spec(s, 0), _sb_group_spec(s, 1), _sb_group_spec(s, 2), _GAIN_SPEC, _GAIN_SPEC],
        out_specs=[_sb_seq_group_spec(s)],
        out_shape=[jax.ShapeDtypeStruct((b, s, D_MODEL), F32)],
        scratch_shapes=[pltpu.VMEM((SB_HEADS_PER_STEP, s, HEAD_DIM), BF16)] * 3,
        semantics=("parallel", "parallel"))
    return o, extra


def _hg_masks():
    c = HG_CHUNK
    row = lax.broadcasted_iota(jnp.int32, (c, c), 0)
    col = lax.broadcasted_iota(jnp.int32, (c, c), 1)
    incl = (col <= row)
    lower = incl.astype(BF16)
    before_sub = (col < (row // HG_SUB) * HG_SUB).astype(BF16)
    upper = (col >= row).astype(BF16)
    return incl, lower, before_sub, upper


def _hg_lower_bound(lbl_ref):
    l0 = lbl_ref[0, 0]
    l1 = lbl_ref[1, 0]
    d = l1 - l0
    return _sigmoid_pair(d)


def _hg_gates(qp, fp, lb, oml):
    sq = _sigmoid(qp)
    sf, sfn = _sigmoid_pair(fp)
    f = lb + oml * sf
    return dict(qp=qp, sq=sq, q=qp * sq, sf=sf, sfn=sfn, f=f, k=oml * sfn, logf=jnp.log(f))


def _hg_intra(qds, ks, gcs, grs, incl):
    subs = range(HG_CHUNK // HG_SUB)
    qdbs = [qd.astype(BF16) for qd in qds]
    ess = [[jnp.exp(jnp.minimum(gr[sub * HG_SUB:sub * HG_SUB + 1, :] - gc, EXP_CLAMP)) for sub in subs]
           for gc, gr in zip(gcs, grs)]
    ksbs = [[(k * e).astype(BF16) for e in es] for k, es in zip(ks, ess)]
    rows = [[_dot(qdb[sub * HG_SUB:(sub + 1) * HG_SUB, :], ksb[sub], NT) for sub in subs]
            for qdb, ksb in zip(qdbs, ksbs)]
    a_s = [jnp.where(incl, jnp.concatenate(r, axis=0), 0.0) for r in rows]
    return a_s, qdbs, ksbs, ess


def _hg_group_rows(outer):
    ns = [outer * HG_UNROLL + u for u in range(HG_UNROLL)]
    return ns, [pl.ds(pl.multiple_of(n * HG_CHUNK, HG_CHUNK), HG_CHUNK) for n in ns]


def _state_spec(nchunk):
    return pl.BlockSpec((1, 1, nchunk, HEAD_DIM, HEAD_DIM), lambda b, h: (b, h, 0, 0, 0))


def hgrn2_fwd(proj4, lbl4, o_gain):
    _, b, s, _ = proj4.shape
    nchunk = s // HG_CHUNK
    c = HG_CHUNK
    assert nchunk % HG_UNROLL == 0

    def body(q_ref, f_ref, i_ref, lbl_ref, og_ref, o_ref, oraw_ref, st_ref):
        incl, lower, before_sub, _ = _hg_masks()
        lb, oml = _hg_lower_bound(lbl_ref)

        def group(outer, st):
            ns, rows = _hg_group_rows(outer)
            vs = [_hg_gates(q_ref[0, 0, r, :], f_ref[0, 0, r, :], lb, oml) for r in rows]
            inps = [i_ref[0, 0, r, :].astype(BF16) for r in rows]
            gcs = [_dot_exact(v["logf"], lower, left=True) for v in vs]
            grs = [_dot_exact(v["logf"], before_sub, left=True) for v in vs]
            a_s, _, _, _ = _hg_intra([v["q"] * jnp.exp(gc - gr) for v, gc, gr in zip(vs, gcs, grs)],
                                     [v["k"] for v in vs], gcs, grs, incl)
            gls = [gc[c - 1:c, :] for gc in gcs]
            adds = [_dot(inp, (v["k"] * jnp.exp(gl - gc)).astype(BF16), TN)
                    for inp, v, gl, gc in zip(inps, vs, gls, gcs)]
            o_intra = [_dot(a.astype(BF16), inp) for a, inp in zip(a_s, inps)]
            sts = []
            for gl, add in zip(gls, adds):
                sts.append(st)
                st = st * jnp.exp(gl) + add
            outs = [oi + _dot((v["q"] * jnp.exp(gc)).astype(BF16), s0.astype(BF16), NT)
                    for oi, v, gc, s0 in zip(o_intra, vs, gcs, sts)]
            for n, r, s0, o in zip(ns, rows, sts, outs):
                st_ref[0, 0, n] = s0
                oraw_ref[0, r, :] = o
                o_ref[0, r, :] = o * _rms_scale(o) * og_ref[...]
            return st

        lax.fori_loop(0, nchunk // HG_UNROLL, group, jnp.zeros((HEAD_DIM, HEAD_DIM), F32))

    seq = jax.ShapeDtypeStruct((b, s, D_MODEL), F32)
    return _pcall(
        body, name="hgrn2_fwd", grid=(b, N_HEADS),
        in_specs=[_head_spec(s, 0), _head_spec(s, 1), _head_spec(s, 2),
                  pl.BlockSpec((2, 1, 1, HEAD_DIM), lambda b, h: (0, h, 0, 0)), _GAIN_SPEC],
        out_specs=[_seq_spec(s), _seq_spec(s), _state_spec(nchunk)],
        out_shape=[seq, seq, jax.ShapeDtypeStruct((b, N_HEADS, nchunk, HEAD_DIM, HEAD_DIM), F32)],
        compiler_params=_params("parallel", "parallel"),
    )(proj4, proj4, proj4, lbl4, o_gain)


def outproj_bwd(dh, w_out, o2, proj, name):
    t = dh.shape[0]
    tm = _row_tile(t, 256)

    def body(dh_ref, w_ref, o_ref, gate_ref, do_ref, dproj_ref, dw_ref):
        dhb = dh_ref[...].astype(BF16)
        dog = _dot(dhb, w_ref[...], NT)
        g = gate_ref[0]
        sg = _sigmoid(g)
        silu = g * sg
        o = o_ref[...]
        do_ref[...] = dog * silu
        dproj_ref[0] = dog * o * (sg * (1.0 + g * (1.0 - sg)))
        part = _dot((o * silu).astype(BF16), dhb, TN)

        @pl.when(pl.program_id(0) == 0)
        def _():
            dw_ref[...] = part

        @pl.when(pl.program_id(0) > 0)
        def _():
            dw_ref[...] += part

    row = pl.BlockSpec((tm, D_MODEL), lambda i: (i, 0))
    full = pl.BlockSpec((D_MODEL, D_MODEL), lambda i: (0, 0))
    return _pcall(
        body, name=name, grid=(t // tm,),
        in_specs=[row, full, row, pl.BlockSpec((1, tm, D_MODEL), lambda i: (3, i, 0))],
        out_specs=[row, pl.BlockSpec((1, tm, D_MODEL), lambda i: (3, i, 0)), full],
        out_shape=[jax.ShapeDtypeStruct((t, D_MODEL), F32),
                   jax.ShapeDtypeStruct((4, t, D_MODEL), F32),
                   jax.ShapeDtypeStruct((D_MODEL, D_MODEL), F32)],
        compiler_params=_params("arbitrary"),
    )(dh, w_out, o2, proj)


def inproj_bwd_dx(dproj, wg, x2, gain, dres, name, rider=None):
    t = x2.shape[0]
    tm = _row_tile(t, 256)

    def body(d_ref, w_ref, x_ref, g_ref, r_ref, dx_ref, dg_ref):
        du = jnp.zeros((tm, D_MODEL), F32)
        for p in range(N_DEV):
            cols = slice((p % 2) * W_COLS, (p % 2 + 1) * W_COLS)
            du = du + _dot(d_ref[p // 2, :, cols].astype(BF16), w_ref[p], NT)
        x = x_ref[...]
        r = _rms_scale(x)
        xh = x * r
        a = du * g_ref[...]
        dx_ref[...] = r_ref[...] + r * (a - xh * jnp.mean(a * xh, axis=-1, keepdims=True))
        part = jnp.sum(du * xh, axis=0, keepdims=True)

        @pl.when(pl.program_id(0) == 0)
        def _():
            dg_ref[...] = part

        @pl.when(pl.program_id(0) > 0)
        def _():
            dg_ref[...] += part

    row = pl.BlockSpec((tm, D_MODEL), lambda i: (i, 0))
    vec = pl.BlockSpec((1, D_MODEL), lambda i: (0, 0))
    return _pcall_riding(
        body, rider, (dproj, wg, x2, gain, dres), name=name, grid=(t // tm,),
        in_specs=[pl.BlockSpec((4, tm, D_MODEL), lambda i: (0, i, 0)),
                  pl.BlockSpec((N_DEV, D_MODEL, W_COLS), lambda i: (0, 0, 0)),
                  row, vec, row],
        out_specs=[row, vec],
        out_shape=[jax.ShapeDtypeStruct((t, D_MODEL), F32), jax.ShapeDtypeStruct((1, D_MODEL), F32)],
        semantics=("arbitrary",))


def inproj_bwd_dw(u, dproj, name, out_dtype=F32):
    t = u.shape[0]
    tm = _row_tile(t, 512)
    half = N_DEV // 2
    steps = t // tm

    def body(u_ref, d_ref, dw_ref, acc_ref):
        ub = u_ref[...]
        for q in range(half):
            part = _dot(ub, d_ref[q // 2, :, (q % 2) * W_COLS:(q % 2 + 1) * W_COLS].astype(BF16), TN)

            @pl.when(pl.program_id(1) == 0)
            def _():
                acc_ref[q] = part

            @pl.when(pl.program_id(1) > 0)
            def _():
                acc_ref[q] += part

        @pl.when(pl.program_id(1) == steps - 1)
        def _():
            dw_ref[...] = acc_ref[...].astype(dw_ref.dtype)

    return _pcall(
        body, name=name, grid=(2, steps),
        in_specs=[pl.BlockSpec((tm, D_MODEL), lambda h, i: (i, 0)),
                  pl.BlockSpec((2, tm, D_MODEL), lambda h, i: (h, i, 0))],
        out_specs=pl.BlockSpec((half, D_MODEL, W_COLS), lambda h, i: (h, 0, 0)),
        out_shape=jax.ShapeDtypeStruct((N_DEV, D_MODEL, W_COLS), out_dtype),
        scratch_shapes=[pltpu.VMEM((half, D_MODEL, W_COLS), F32)],
        compiler_params=_params("parallel", "arbitrary"),
    )(u, dproj)


def _rms_bwd(x, gain, dy):
    r = _rms_scale(x)
    xh = x * r
    a = dy * gain
    return r * (a - xh * jnp.mean(a * xh, axis=-1, keepdims=True)), dy * xh


def sb_attn_bwd(proj4, do3, o3, q_gain, k_gain, dproj4, rider=None):
    _, b, s, _ = proj4.shape
    blk = ATTN_BLOCK
    nq = s // blk

    def body(q_ref, k_ref, v_ref, do_ref, o_ref, qg_ref, kg_ref, _alias, d_ref, dqg_ref, dkg_ref, qb, kb, vb, dob):
        _sb_load_normed(q_ref, qg_ref, qb)
        _sb_load_normed(k_ref, kg_ref, kb)
        for h in range(SB_HEADS_PER_STEP):
            cols = slice(h * HEAD_DIM, (h + 1) * HEAD_DIM)
            vb[h] = v_ref[0, 0, :, cols].astype(BF16)
            dob[h] = do_ref[0, :, cols].astype(BF16)
        d_ref[...] = jnp.zeros_like(d_ref)
        row = lax.broadcasted_iota(jnp.int32, (blk, blk), 0)
        col = lax.broadcasted_iota(jnp.int32, (blk, blk), 1)
        tri_lt = col < row
        suffix = (row > col).astype(BF16)
        suffix_incl = (row >= col).astype(BF16)

        def q_group(m, _):
            chains = _sb_chains(m)
            qis, dois, deltas = [], [], []
            for h, i in chains:
                rows_i = pl.ds(pl.multiple_of(i * blk, blk), blk)
                qis.append(qb[h, rows_i, :])
                dois.append(dob[h, rows_i, :])
                deltas.append(jnp.sum(dois[-1].astype(F32) * o_ref[0, rows_i, h * HEAD_DIM:(h + 1) * HEAD_DIM],
                                      axis=1, keepdims=True))

            def k_step(state):
                done, cs, cgs, dqs = state
                where = [_sb_key_rows(i, done) for _, i in chains]
                kjs = [kb[h, rows, :] for (h, _), (rows, _) in zip(chains, where)]
                zs = [_sb_logits(qi, kj) for qi, kj in zip(qis, kjs)]
                dws = [_dot(doi, vb[h, rows, :], NT) for (h, _), (rows, _), doi in zip(chains, where, dois)]
                scored = [_sb_scores(z, done == 0, live, tri_lt) for z, (_, live) in zip(zs, where)]
                afters = [_dot_exact(log_skip, suffix) for log_skip, _, _ in scored]
                wbs = [jnp.where(valid, jnp.exp(log_beta + after + c), 0.0).astype(BF16)
                       for (_, log_beta, valid), after, c in zip(scored, afters, cs)]
                gs = [dw * wb.astype(F32) for dw, wb in zip(dws, wbs)]
                befores = [delta - (_dot_exact(g, suffix_incl) + cg) for g, delta, cg in zip(gs, deltas, cgs)]
                dzbs = []
                for (_, log_beta, valid), g, before in zip(scored, gs, befores):
                    beta = jnp.exp(log_beta)
                    dz = jnp.where(valid, g * (1.0 - beta) - before * beta, 0.0) * (HEAD_DIM ** -0.5)
                    dzbs.append(dz.astype(BF16))
                new_dqs = [dq + _dot(dzb, kj) for dq, dzb, kj in zip(dqs, dzbs, kjs)]
                for (h, _), (rows, _), wb, doi, dzb, qi in zip(chains, where, wbs, dois, dzbs, qis):
                    cols = slice(h * HEAD_DIM, (h + 1) * HEAD_DIM)
                    d_ref[2, 0, rows, cols] += _dot(wb, doi, TN)
                    d_ref[1, 0, rows, cols] += _dot(dzb, qi, TN)
                new_cs = [c + jnp.sum(log_skip, axis=1, keepdims=True) for (log_skip, _, _), c in zip(scored, cs)]
                new_cgs = [cg + jnp.sum(g, axis=1, keepdims=True) for g, cg in zip(gs, cgs)]
                return done + 1, tuple(new_cs), tuple(new_cgs), tuple(new_dqs)

            n = len(chains)
            zero = (jnp.zeros((blk, 1), F32),) * n
            _, _, _, dqs = lax.while_loop(functools.partial(_sb_keys_left, chains), k_step,
                                          (jnp.int32(0), zero, zero, (jnp.zeros((blk, HEAD_DIM), F32),) * n))
            for (h, i), dq in zip(chains, dqs):
                d_ref[0, 0, pl.ds(pl.multiple_of(i * blk, blk), blk), h * HEAD_DIM:(h + 1) * HEAD_DIM] = dq
            return 0

        lax.fori_loop(0, nq // SB_QBLOCKS_PER_STEP, q_group, 0)

        def norm_block(i, carry):
            rows = pl.ds(pl.multiple_of(i * blk, blk), blk)
            out = []
            for h in range(SB_HEADS_PER_STEP):
                cols = slice(h * HEAD_DIM, (h + 1) * HEAD_DIM)
                for part, src_ref, gain_ref in ((0, q_ref, qg_ref), (1, k_ref, kg_ref)):
                    dx, pg = _rms_bwd(src_ref[0, 0, rows, cols], gain_ref[...], d_ref[part, 0, rows, cols])
                    d_ref[part, 0, rows, cols] = dx
                    out.append(carry[len(out)] + jnp.sum(pg, axis=0, keepdims=True))
            return tuple(out)

        sums = lax.fori_loop(0, nq, norm_block, (jnp.zeros((1, HEAD_DIM), F32),) * (2 * SB_HEADS_PER_STEP))
        for h in range(SB_HEADS_PER_STEP):
            dqg_ref[0, h] = sums[2 * h]
            dkg_ref[0, h] = sums[2 * h + 1]

    head_row = jax.ShapeDtypeStruct((b, N_HEADS, 1, HEAD_DIM), F32)
    return _pcall_riding(
        body, rider, (proj4, proj4, proj4, do3, o3, q_gain, k_gain, dproj4),
        name="sb_attn_bwd", grid=(b, N_HEADS // SB_HEADS_PER_STEP),
        in_specs=[_sb_group_spec(s, 0), _sb_group_spec(s, 1), _sb_group_spec(s, 2),
                  _sb_seq_group_spec(s), _sb_seq_group_spec(s), _GAIN_SPEC, _GAIN_SPEC,
                  pl.BlockSpec(memory_space=pl.ANY)],
        out_specs=[pl.BlockSpec((3, 1, s, SB_GROUP_COLS), lambda b, g: (0, b, 0, g)),
                   _SB_GROUP_ROW_SPEC, _SB_GROUP_ROW_SPEC],
        out_shape=[jax.ShapeDtypeStruct(dproj4.shape, F32), head_row, head_row],
        scratch_shapes=[pltpu.VMEM((SB_HEADS_PER_STEP, s, HEAD_DIM), BF16)] * 4,
        input_output_aliases={7: 0}, semantics=("parallel", "parallel"))


def hgrn2_bwd(proj4, don3, oraw3, states, lbl4, o_gain, dproj4, rider=None):
    _, b, s, _ = proj4.shape
    nchunk = s // HG_CHUNK
    c = HG_CHUNK
    subs = range(HG_CHUNK // HG_SUB)
    ngroup = nchunk // HG_UNROLL

    def body(q_ref, f_ref, i_ref, don_ref, oraw_ref, st_ref, lbl_ref, og_ref, _alias, d_ref, dog_ref, dlb_ref):
        incl, lower, before_sub, upper = _hg_masks()
        lb, oml = _hg_lower_bound(lbl_ref)
        last_row = lax.broadcasted_iota(jnp.int32, (c, HEAD_DIM), 0) == c - 1

        def group(m, carry):
            dst, dog_acc, dlb_acc = carry
            ns, rows = _hg_group_rows(ngroup - 1 - m)
            ns, rows = ns[::-1], rows[::-1]
            vs = [_hg_gates(q_ref[0, 0, r, :], f_ref[0, 0, r, :], lb, oml) for r in rows]
            inps = [i_ref[0, 0, r, :].astype(BF16) for r in rows]
            sts = [st_ref[0, 0, n] for n in ns]
            gcs = [_dot_exact(v["logf"], lower, left=True) for v in vs]
            grs = [_dot_exact(v["logf"], before_sub, left=True) for v in vs]
            e_qs = [jnp.exp(gc - gr) for gc, gr in zip(gcs, grs)]
            a_s, qdbs, ksbs, ess = _hg_intra([v["q"] * e for v, e in zip(vs, e_qs)], [v["k"] for v in vs],
                                             gcs, grs, incl)
            e_gcs = [jnp.exp(gc) for gc in gcs]
            gls = [gc[c - 1:c, :] for gc in gcs]
            e_gls = [jnp.exp(gl) for gl in gls]
            e_ks = [jnp.exp(gl - gc) for gl, gc in zip(gls, gcs)]
            normed = [_rms_bwd(oraw_ref[0, r, :], og_ref[...], don_ref[0, r, :]) for r in rows]
            dobs = [do.astype(BF16) for do, _ in normed]
            dabs = [jnp.where(incl, _dot(dob, inp, NT), 0.0).astype(BF16) for dob, inp in zip(dobs, inps)]
            adds = [_dot(dob, (v["q"] * e).astype(BF16), TN) for dob, v, e in zip(dobs, vs, e_gcs)]
            dq_inters = [_dot(dob, st.astype(BF16)) * e for dob, st, e in zip(dobs, sts, e_gcs)]
            dqds = [jnp.concatenate([_dot(dab[sub * HG_SUB:(sub + 1) * HG_SUB, :], ksb[sub]) for sub in subs], axis=0)
                    for dab, ksb in zip(dabs, ksbs)]
            dkss = [[_dot(dab[sub * HG_SUB:(sub + 1) * HG_SUB, :], qdb[sub * HG_SUB:(sub + 1) * HG_SUB, :], TN)
                     for sub in subs] for dab, qdb in zip(dabs, qdbs)]
            dsts = []
            for e_gl, add in zip(e_gls, adds):
                dsts.append(dst)
                dst = dst * e_gl + add
            dstbs = [d.astype(BF16) for d in dsts]
            dis = [_dot(a.astype(BF16), dob, TN) + _dot((v["k"] * e_k).astype(BF16), dstb, NT)
                   for a, dob, v, e_k, dstb in zip(a_s, dobs, vs, e_ks, dstbs)]
            dk_inters = [_dot(inp, dstb) * e_k for inp, dstb, e_k in zip(inps, dstbs, e_ks)]
            for u, r in enumerate(rows):
                v, q, k = vs[u], vs[u]["q"], vs[u]["k"]
                dk, dgc_k = dk_inters[u], jnp.zeros((c, HEAD_DIM), F32)
                for sub in subs:
                    dk = dk + dkss[u][sub] * ess[u][sub]
                    dgc_k = dgc_k + dkss[u][sub] * ksbs[u][sub].astype(F32)
                dq = dqds[u] * e_qs[u] + dq_inters[u]
                at_last = (jnp.sum(k * dk_inters[u], axis=0, keepdims=True)
                           + e_gls[u] * jnp.sum(sts[u] * dsts[u], axis=0, keepdims=True))
                dgc = ((qdbs[u].astype(F32) * dqds[u] - dgc_k) + (q * dq_inters[u] - k * dk_inters[u])
                       + jnp.where(last_row, at_last, 0.0))
                dlf_f = _dot_exact(dgc, upper, left=True) / v["f"]
                d_ref[0, 0, r, :] = dq * (v["sq"] * (1.0 + v["qp"] * (1.0 - v["sq"])))
                d_ref[1, 0, r, :] = (dlf_f - dk) * (oml * v["sf"] * v["sfn"])
                d_ref[2, 0, r, :] = dis[u]
                dlb_acc = dlb_acc + jnp.sum((dlf_f - dk) * v["sfn"], axis=0, keepdims=True)
                dog_acc = dog_acc + jnp.sum(normed[u][1], axis=0, keepdims=True)
            return dst, dog_acc, dlb_acc

        zero = jnp.zeros((1, HEAD_DIM), F32)
        _, dog, dlb = lax.fori_loop(0, ngroup, group, (jnp.zeros((HEAD_DIM, HEAD_DIM), F32), zero, zero))
        dog_ref[0, 0] = dog
        dlb_ref[0, 0] = dlb

    head_row = jax.ShapeDtypeStruct((b, N_HEADS, 1, HEAD_DIM), F32)
    return _pcall_riding(
        body, rider, (proj4, proj4, proj4, don3, oraw3, states, lbl4, o_gain, dproj4),
        name="hgrn2_bwd", grid=(b, N_HEADS),
        in_specs=[_head_spec(s, 0), _head_spec(s, 1), _head_spec(s, 2), _seq_spec(s), _seq_spec(s),
                  _state_spec(nchunk), pl.BlockSpec((2, 1, 1, HEAD_DIM), lambda b, h: (0, h, 0, 0)), _GAIN_SPEC,
                  pl.BlockSpec(memory_space=pl.ANY)],
        out_specs=[pl.BlockSpec((3, 1, s, HEAD_DIM), lambda b, h: (0, b, 0, h)), _HEAD_ROW_SPEC, _HEAD_ROW_SPEC],
        out_shape=[jax.ShapeDtypeStruct(dproj4.shape, F32), head_row, head_row],
        input_output_aliases={8: 0}, semantics=("parallel", "parallel"))


def local_step(x, target, sb_norm, wsi, sb_q_gain, sb_k_gain, hg_o_gain, hg_lb_logits, wso_mine, whi_mine, who_mine,
               hg_norm_mine):
    b, s, _ = x.shape
    t = b * s
    x2 = x.reshape(t, D_MODEL)
    tg2 = target.reshape(t, D_MODEL)
    lbl4 = hg_lb_logits.reshape(2, N_HEADS, 1, HEAD_DIM)
    four = (4, b, s, D_MODEL)
    three = (b, s, D_MODEL)
    rows8 = (N_DEV, W_ROWS, D_MODEL)

    (proj0, u0), (wso,) = rms_inproj(x2, sb_norm, wsi, "sb_inproj", _Rider([wso_mine], scatter=False))
    wso = wso.reshape(D_MODEL, D_MODEL)
    o0, (whi, who, hgn) = sb_attn_fwd(proj0.reshape(four), sb_q_gain, sb_k_gain,
                                      _Rider([whi_mine, who_mine, hg_norm_mine], scatter=False))
    who = who.reshape(D_MODEL, D_MODEL)
    hg_norm_full = hgn[:, 0, :].reshape(1, D_MODEL)
    o0 = o0.reshape(t, D_MODEL)
    h1 = gate_outproj(o0, proj0, wso, x2, None, "sb_outproj")
    (proj1, u1), _ = rms_inproj(h1, hg_norm_full, whi, "hg_inproj")
    o1, o1_raw, states = hgrn2_fwd(proj1.reshape(four), lbl4, hg_o_gain)
    o1 = o1.reshape(t, D_MODEL)
    dh2, loss_parts = gate_outproj(o1, proj1, who, h1, tg2, "hg_outproj_loss")

    do1, dproj1, g_who = outproj_bwd(dh2, who, o1, proj1, "hg_outproj_bwd")
    (dproj1, g_og, g_lb), (p_who,) = hgrn2_bwd(proj1.reshape(four), do1.reshape(three), o1_raw, states, lbl4,
                                               hg_o_gain, dproj1.reshape(four),
                                               _Rider([g_who.reshape(rows8)], scatter=True))
    dproj1 = dproj1.reshape(4, t, D_MODEL)
    (dh1, g_hgn), _ = inproj_bwd_dx(dproj1, whi, h1, hg_norm_full, dh2, "hg_inproj_bwd_dx")
    g_whi = inproj_bwd_dw(u1, dproj1, "hg_inproj_bwd_dw")

    do0, dproj0, g_wso = outproj_bwd(dh1, wso, o0, proj0, "sb_outproj_bwd")
    (dproj0, g_qg, g_kg), (p_whi, p_wso) = sb_attn_bwd(proj0.reshape(four), do0.reshape(three), o0.reshape(three),
                                                       sb_q_gain, sb_k_gain, dproj0.reshape(four),
                                                       _Rider([g_whi, g_wso.reshape(rows8)], scatter=True))
    dproj0 = dproj0.reshape(4, t, D_MODEL)
    g_wsi = inproj_bwd_dw(u0, dproj0, "sb_inproj_bwd_dw", out_dtype=BF16)
    (gx, g_sbn), (p_wsi,) = inproj_bwd_dx(dproj0, wsi, x2, sb_norm, dh1, "sb_inproj_bwd_dx",
                                          _Rider([g_wsi], scatter=True))
    return dict(loss_parts=loss_parts, gx=gx.reshape(three), p_wsi=p_wsi, p_wso=p_wso, p_whi=p_whi, p_who=p_who,
                g_sbn=g_sbn, g_hgn=g_hgn, g_qg=g_qg, g_kg=g_kg, g_og=g_og, g_lb=g_lb)


def gather_first_weights(w_si, w_so, w_hi, w_ho, hg_norm):
    def body(si_ref, so_ref, hi_ref, ho_ref, hn_ref, o_si, so_b, hi_b, ho_b, hn_b, si_b, send_sems, recv_sems, local_sems):
        for src, buf in ((si_ref, si_b), (so_ref, so_b), (hi_ref, hi_b), (ho_ref, ho_b)):
            buf[...] = src[...].astype(BF16)
        hn_b[...] = jnp.broadcast_to(hn_ref[...], hn_b.shape)
        pairs = [((lambda p: si_b), o_si)]
        pos = _mesh_pos()
        _exchange_start(pairs, send_sems, recv_sems, local_sems, pos)
        _exchange_wait(pairs, send_sems, recv_sems, local_sems, pos)

    return _pcall(
        body, name="gather_first_weights",
        in_specs=[_VMEM] * 5, out_specs=[_ANY] + [_VMEM] * 4,
        out_shape=[jax.ShapeDtypeStruct((N_DEV,) + w_si.shape, BF16), jax.ShapeDtypeStruct(w_so.shape, BF16),
                   jax.ShapeDtypeStruct(w_hi.shape, BF16), jax.ShapeDtypeStruct(w_ho.shape, BF16),
                   jax.ShapeDtypeStruct((8, HEAD_DIM), F32)],
        scratch_shapes=[pltpu.VMEM(w_si.shape, BF16)] + _exchange_sems(1),
        compiler_params=pltpu.CompilerParams(vmem_limit_bytes=VMEM_LIMIT_BYTES),
    )(w_si, w_so, w_hi, w_ho, hg_norm)


def _adamw(w, g, m, v):
    m = ADAM_B1 * m + (1.0 - ADAM_B1) * g
    v = ADAM_B2 * v + (1.0 - ADAM_B2) * (g * g)
    m_hat = m / (1.0 - ADAM_B1 ** ADAM_STEP)
    v_hat = v / (1.0 - ADAM_B2 ** ADAM_STEP)
    delta = -ADAM_LR * (m_hat / (jnp.sqrt(v_hat) + ADAM_EPS) + ADAM_WD * w)
    return delta, m, v


def reduce_adamw(parts, w, m, v, name):
    _, r, c = parts.shape
    tr = _row_tile(r, 256)

    def body(p_ref, w_ref, m_ref, v_ref, g_ref, d_ref, m2_ref, v2_ref):
        g = p_ref[0].astype(F32)
        for dev in range(1, N_DEV):
            g = g + p_ref[dev].astype(F32)
        g_ref[...] = g
        d_ref[...], m2_ref[...], v2_ref[...] = _adamw(w_ref[...], g, m_ref[...], v_ref[...])

    tile = pl.BlockSpec((tr, c), lambda i: (i, 0))
    return _pcall(
        body, name=name, grid=(r // tr,),
        in_specs=[pl.BlockSpec((N_DEV, tr, c), lambda i: (0, i, 0)), tile, tile, tile],
        out_specs=[tile] * 4, out_shape=[jax.ShapeDtypeStruct((r, c), F32)] * 4,
        compiler_params=_params("parallel"),
    )(parts, w, m, v)


PACK_ROWS = 32
ROW_SBN, ROW_HGN, ROW_LB, ROW_QG, ROW_KG, ROW_OG, ROW_LOSS = 0, 8, 16, 24, 25, 26, 27


def small_update(g_sbn, g_hgn, g_lb, g_qg, g_kg, g_og, loss_parts, small):
    n_in = 7 + len(small)

    def body(*refs):
        sbn_ref, hgn_ref, lb_ref, qg_ref, kg_ref, og_ref, loss_ref = refs[:7]
        wmv = refs[7:n_in]
        outs = refs[n_in:n_in + 25]
        pack, gath, tot, send_sems, recv_sems, local_sems = refs[n_in + 25:]
        pos = _mesh_pos()
        me = _linear(pos)
        pack[...] = jnp.zeros_like(pack)
        pack[ROW_SBN:ROW_SBN + 8, :] = sbn_ref[...]
        pack[ROW_HGN:ROW_HGN + 8, :] = hgn_ref[...]
        pack[ROW_LB:ROW_LB + 8, :] = jnp.sum(lb_ref[...], axis=0)
        pack[ROW_QG:ROW_QG + 1, :] = jnp.sum(qg_ref[...], axis=0, keepdims=True)
        pack[ROW_KG:ROW_KG + 1, :] = jnp.sum(kg_ref[...], axis=0, keepdims=True)
        pack[ROW_OG:ROW_OG + 1, :] = jnp.sum(og_ref[...], axis=0, keepdims=True)
        pack[ROW_LOSS:ROW_LOSS + 1, :] = jnp.sum(loss_ref[...], axis=0)[0:1, :]
        _exchange_start([((lambda p: pack), gath)], send_sems, recv_sems, local_sems, pos)
        _exchange_wait([((lambda p: pack), gath)], send_sems, recv_sems, local_sems, pos)
        total = gath[0]
        for dev in range(1, N_DEV):
            total = total + gath[dev]
        tot[...] = total
        outs[0][...] = jnp.broadcast_to(tot[ROW_LOSS:ROW_LOSS + 1, :], (8, HEAD_DIM))
        l0 = wmv[15][0:8, :]
        l1 = wmv[15][8:16, :]
        p1, p0 = _sigmoid_pair(l1 - l0)
        d_l1 = p0 * p1 * tot[ROW_LB:ROW_LB + 8, :]
        grads = [tot[ROW_SBN:ROW_SBN + 8, :], tot[ROW_QG:ROW_QG + 1, :], tot[ROW_KG:ROW_KG + 1, :],
                 tot[pl.ds(ROW_HGN + me, 1), :], tot[ROW_OG:ROW_OG + 1, :],
                 jnp.concatenate([-d_l1, d_l1], axis=0)]
        for i, g in enumerate(grads):
            w_ref, m_ref, v_ref = wmv[3 * i:3 * i + 3]
            o = outs[1 + 4 * i:5 + 4 * i]
            o[0][...] = g
            o[1][...], o[2][...], o[3][...] = _adamw(w_ref[...], g, m_ref[...], v_ref[...])

    out_shape = [jax.ShapeDtypeStruct((8, HEAD_DIM), F32)]
    for i in range(6):
        out_shape += [jax.ShapeDtypeStruct(small[3 * i].shape, F32)] * 4
    return _pcall(
        body, name="small_update",
        in_specs=[_VMEM] * n_in, out_specs=[_VMEM] * 25, out_shape=out_shape,
        scratch_shapes=[pltpu.VMEM((PACK_ROWS, HEAD_DIM), F32), pltpu.VMEM((N_DEV, PACK_ROWS, HEAD_DIM), F32),
                        pltpu.VMEM((PACK_ROWS, HEAD_DIM), F32)] + _exchange_sems(1),
    )(g_sbn, g_hgn, g_lb, g_qg, g_kg, g_og, loss_parts, *small)


def kernel(x, sb_norm, sb_w_in, sb_q_gain, sb_k_gain, sb_w_out, hg_norm, hg_w_in, hg_o_gain, hg_w_out, hg_lb_logits, loss_target, m_sb_norm, m_sb_w_in, m_sb_q_gain, m_sb_k_gain, m_sb_w_out, m_hg_norm, m_hg_w_in, m_hg_o_gain, m_hg_w_out, m_hg_lb_logits, v_sb_norm, v_sb_w_in, v_sb_q_gain, v_sb_k_gain, v_sb_w_out, v_hg_norm, v_hg_w_in, v_hg_o_gain, v_hg_w_out, v_hg_lb_logits):
    b = x.shape[0]
    wsi, wso_mine, whi_mine, who_mine, hg_norm_mine = gather_first_weights(
        sb_w_in[0], sb_w_out[0], hg_w_in[0], hg_w_out[0], hg_norm)
    r = local_step(x, loss_target, sb_norm, wsi, sb_q_gain, sb_k_gain, hg_o_gain, hg_lb_logits,
                   wso_mine, whi_mine, who_mine, hg_norm_mine)
    big = {}
    for name, p, w, m, v in (("sb_w_in", r["p_wsi"], sb_w_in, m_sb_w_in, v_sb_w_in),
                             ("sb_w_out", r["p_wso"], sb_w_out, m_sb_w_out, v_sb_w_out),
                             ("hg_w_in", r["p_whi"], hg_w_in, m_hg_w_in, v_hg_w_in),
                             ("hg_w_out", r["p_who"], hg_w_out, m_hg_w_out, v_hg_w_out)):
        big[name] = [o[None] for o in reduce_adamw(p, w[0], m[0], v[0], "adamw_" + name)]

    def rows8(a):
        return a.reshape(8, HEAD_DIM)

    def rows16(a):
        return a.reshape(16, HEAD_DIM)

    small_in = [rows8(sb_norm), rows8(m_sb_norm), rows8(v_sb_norm),
                sb_q_gain, m_sb_q_gain, v_sb_q_gain,
                sb_k_gain, m_sb_k_gain, v_sb_k_gain,
                hg_norm, m_hg_norm, v_hg_norm,
                hg_o_gain, m_hg_o_gain, v_hg_o_gain,
                rows16(hg_lb_logits), rows16(m_hg_lb_logits), rows16(v_hg_lb_logits)]
    so = small_update(rows8(r["g_sbn"]), rows8(r["g_hgn"]), r["g_lb"].reshape(b, N_HEADS, HEAD_DIM),
                      r["g_qg"].reshape(b * N_HEADS, HEAD_DIM), r["g_kg"].reshape(b * N_HEADS, HEAD_DIM),
                      r["g_og"].reshape(b * N_HEADS, HEAD_DIM), r["loss_parts"], small_in)
    loss = so[0][0, 0]
    shapes = {"sb_norm": (1, D_MODEL), "sb_q_gain": (1, HEAD_DIM), "sb_k_gain": (1, HEAD_DIM),
              "hg_norm": (1, HEAD_DIM), "hg_o_gain": (1, HEAD_DIM), "hg_lb_logits": (2, D_MODEL)}
    small = {}
    for i, name in enumerate(("sb_norm", "sb_q_gain", "sb_k_gain", "hg_norm", "hg_o_gain", "hg_lb_logits")):
        small[name] = [o.reshape(shapes[name]) for o in so[1 + 4 * i:5 + 4 * i]]
    order = ("sb_norm", "sb_w_in", "sb_q_gain", "sb_k_gain", "sb_w_out",
             "hg_norm", "hg_w_in", "hg_o_gain", "hg_w_out", "hg_lb_logits")
    res = {**big, **small}
    return (loss, r["gx"]) + tuple(res[n][j] for j in range(4) for n in order)
```

```python
import functools

import jax
import jax.numpy as jnp
from jax import lax
from jax.experimental import pallas as pl
from jax.experimental.pallas import tpu as pltpu

F32 = jnp.float32
BF16 = jnp.bfloat16

N_DEV = 8
D_MODEL = 1024
N_HEADS = 8
HEAD_DIM = 128
RMS_EPS = 1e-6
ATTN_BLOCK = 128
HG_CHUNK = 64
HG_SUB = 16
HG_UNROLL = 8
EXP_CLAMP = 80.0
SB_HEADS_PER_STEP = 2
SB_QBLOCKS_PER_STEP = 4
SB_GROUP_COLS = SB_HEADS_PER_STEP * 128
SB_LOG_WEIGHT_FLOOR = -104.0
VMEM_LIMIT_BYTES = 48 * 1024 * 1024
W_COLS = 4 * D_MODEL // N_DEV
W_ROWS = D_MODEL // N_DEV

ADAM_LR = 0.001
ADAM_B1 = 0.9
ADAM_B2 = 0.999
ADAM_EPS = 1e-08
ADAM_WD = 0.01
ADAM_STEP = 10

NT = (((1,), (1,)), ((), ()))
TN = (((0,), (0,)), ((), ()))
NN = (((1,), (0,)), ((), ()))


def _pcall(body, *, name, **kw):
    return pl.pallas_call(body, name=name, **kw)


def _params(*sem):
    return pltpu.CompilerParams(dimension_semantics=sem, vmem_limit_bytes=VMEM_LIMIT_BYTES)


def _dot(a, b, dims=NN):
    return lax.dot_general(a, b, dims, preferred_element_type=F32)


def _dot_exact(a, m, dims=NN, left=False):
    hi = a.astype(BF16)
    lo = (a - hi.astype(F32)).astype(BF16)
    if left:
        return _dot(m, hi, dims) + _dot(m, lo, dims)
    return _dot(hi, m, dims) + _dot(lo, m, dims)


def _split(a):
    hi = a.astype(BF16)
    return hi, (a - hi.astype(F32)).astype(BF16)


def _dot3(a, b, dims=NN):
    return _dot(a[0], b[0], dims) + (_dot(a[0], b[1], dims) + _dot(a[1], b[0], dims))


def _sigmoid(x):
    return 1.0 / (1.0 + jnp.exp(-x))


def _sigmoid_pair(x):
    e = jnp.exp(-jnp.abs(x))
    big = 1.0 / (1.0 + e)
    small = e * big
    pos = x >= 0
    return jnp.where(pos, big, small), jnp.where(pos, small, big)


def _rms_scale(x):
    return lax.rsqrt(jnp.mean(x * x, axis=-1, keepdims=True) + RMS_EPS)


def _row_tile(t, want):
    return want if t % want == 0 else t


MESH = pl.DeviceIdType.MESH
N_PEERS = N_DEV - 1
_ANY = pl.BlockSpec(memory_space=pl.ANY)
_VMEM = pl.BlockSpec(memory_space=pltpu.VMEM)


def _mesh_pos():
    return lax.axis_index("x"), lax.axis_index("y"), lax.axis_index("c")


def _linear(pos):
    return 4 * pos[0] + 2 * pos[1] + pos[2]


def _peer(pos, k):
    flips = ((k + 1) >> 2 & 1, (k + 1) >> 1 & 1, (k + 1) & 1)
    return tuple(1 - p if f else p for p, f in zip(pos, flips))


def _exchange_copies(pairs, send_sems, recv_sems, local_sems, pos, landing):
    me = _linear(pos)
    local, remote = [], []
    for a, (src_of, dst) in enumerate(pairs):
        local.append(pltpu.make_async_copy(src_of(me), dst.at[me], local_sems.at[a]))
        for k in range(N_PEERS):
            peer = _peer(pos, k)
            remote.append(pltpu.make_async_remote_copy(
                src_ref=src_of(_linear(peer)), dst_ref=dst.at[_linear(peer) if landing else me],
                send_sem=send_sems.at[a, k], recv_sem=recv_sems.at[a, k], device_id=peer, device_id_type=MESH))
    return local, remote


def _exchange_start(pairs, send_sems, recv_sems, local_sems, pos):
    local, sent = _exchange_copies(pairs, send_sems, recv_sems, local_sems, pos, landing=False)
    for copy in local + sent:
        copy.start()


def _exchange_wait(pairs, send_sems, recv_sems, local_sems, pos):
    local, landed = _exchange_copies(pairs, send_sems, recv_sems, local_sems, pos, landing=True)
    for copy in landed:
        copy.wait_recv()
        copy.wait_send()
    for copy in local:
        copy.wait()


def _exchange_sems(n):
    return [pltpu.SemaphoreType.DMA((n, N_PEERS)), pltpu.SemaphoreType.DMA((n, N_PEERS)),
            pltpu.SemaphoreType.DMA((n,))]


class _Rider:
    def __init__(self, arrays, scatter):
        self.arrays = list(arrays)
        self.scatter = scatter
        self.out_shapes = [jax.ShapeDtypeStruct(a.shape if scatter else (N_DEV,) + a.shape, a.dtype)
                           for a in self.arrays]

    def pairs(self, in_refs, out_refs):
        if self.scatter:
            return [((lambda p, r=r: r.at[p]), o) for r, o in zip(in_refs, out_refs)]
        return [((lambda p, r=r: r), o) for r, o in zip(in_refs, out_refs)]


def _pcall_riding(body, rider, args, *, name, grid, in_specs, out_specs, out_shape, semantics, scratch_shapes=(),
                  input_output_aliases=None):
    aliases = input_output_aliases or {}
    if rider is None:
        outs = _pcall(body, name=name, grid=grid, in_specs=list(in_specs), out_specs=list(out_specs),
                      out_shape=list(out_shape), scratch_shapes=list(scratch_shapes), input_output_aliases=aliases,
                      compiler_params=_params(*semantics))(*args)
        return list(outs), []
    n_in, n_out, n_scr, n_r = len(in_specs), len(out_specs), len(scratch_shapes), len(rider.arrays)

    def riding(*refs):
        ins, refs = refs[:n_in], refs[n_in:]
        rider_in, refs = refs[:n_r], refs[n_r:]
        outs, refs = refs[:n_out], refs[n_out:]
        rider_out, refs = refs[:n_r], refs[n_r:]
        scratch, sems = refs[:n_scr], refs[n_scr:]
        pairs = rider.pairs(rider_in, rider_out)
        first = functools.reduce(jnp.logical_and, [pl.program_id(a) == 0 for a in range(len(grid))])
        last = functools.reduce(jnp.logical_and, [pl.program_id(a) == g - 1 for a, g in enumerate(grid)])

        @pl.when(first)
        def _():
            _exchange_start(pairs, *sems, _mesh_pos())

        body(*ins, *outs, *scratch)

        @pl.when(last)
        def _():
            _exchange_wait(pairs, *sems, _mesh_pos())

    outs = _pcall(riding, name=name, grid=grid, in_specs=list(in_specs) + [_ANY] * n_r,
                  out_specs=list(out_specs) + [_ANY] * n_r, out_shape=list(out_shape) + rider.out_shapes,
                  scratch_shapes=list(scratch_shapes) + _exchange_sems(n_r), input_output_aliases=aliases,
                  compiler_params=_params(*(("arbitrary",) * len(grid))))(*args, *rider.arrays)
    return list(outs[:n_out]), list(outs[n_out:])


def rms_inproj(x2, gain, wg, name, rider=None):
    t = x2.shape[0]
    tm = _row_tile(t, 256)

    def body(x_ref, g_ref, w_ref, proj_ref, ut_ref):
        x = x_ref[...]
        u = x * _rms_scale(x) * g_ref[...]
        ut_ref[...] = u.T.astype(BF16)
        u = u.astype(BF16)
        for p in range(N_DEV):
            proj_ref[p // 2, :, (p % 2) * W_COLS:(p % 2 + 1) * W_COLS] = _dot(u, w_ref[p])

    return _pcall_riding(
        body, rider, (x2, gain, wg), name=name,
        grid=(t // tm,),
        in_specs=[pl.BlockSpec((tm, D_MODEL), lambda i: (i, 0)),
                  pl.BlockSpec((1, D_MODEL), lambda i: (0, 0)),
                  pl.BlockSpec((N_DEV, D_MODEL, W_COLS), lambda i: (0, 0, 0))],
        out_specs=[pl.BlockSpec((4, tm, D_MODEL), lambda i: (0, i, 0)),
                   pl.BlockSpec((D_MODEL, tm), lambda i: (0, i))],
        out_shape=[jax.ShapeDtypeStruct((4, t, D_MODEL), F32),
                   jax.ShapeDtypeStruct((D_MODEL, t), BF16)],
        semantics=("parallel",))


def gate_outproj(o2, proj, w_out, resid, target, name):
    t = o2.shape[0]
    tm = _row_tile(t, 256)
    with_loss = target is not None

    def body(o_ref, gate_ref, w_ref, r_ref, *rest):
        g = gate_ref[0]
        og = (o_ref[...] * (g * _sigmoid(g))).astype(BF16)
        h = r_ref[...] + _dot(og, w_ref[...])
        if with_loss:
            t_ref, dh_ref, loss_ref = rest
            err = h - t_ref[...]
            dh_ref[...] = err * (1.0 / D_MODEL)
            part = 0.5 * jnp.sum(jnp.mean(err * err, axis=-1, keepdims=True))
            loss_ref[...] = jnp.full(loss_ref.shape, part, F32)
        else:
            (h_ref,) = rest
            h_ref[...] = h

    row = pl.BlockSpec((tm, D_MODEL), lambda i: (i, 0))
    in_specs = [row,
                pl.BlockSpec((1, tm, D_MODEL), lambda i: (3, i, 0)),
                pl.BlockSpec((D_MODEL, D_MODEL), lambda i: (0, 0)),
                row]
    args = [o2, proj, w_out, resid]
    if with_loss:
        in_specs.append(row)
        args.append(target)
        out_specs = [row, pl.BlockSpec((1, 8, 128), lambda i: (i, 0, 0))]
        out_shape = [jax.ShapeDtypeStruct((t, D_MODEL), F32),
                     jax.ShapeDtypeStruct((t // tm, 8, 128), F32)]
    else:
        out_specs = row
        out_shape = jax.ShapeDtypeStruct((t, D_MODEL), F32)
    return _pcall(body, name=name, grid=(t // tm,), in_specs=in_specs, out_specs=out_specs,
                  out_shape=out_shape, compiler_params=_params("parallel"))(*args)


def _head_spec(s, part):
    return pl.BlockSpec((1, 1, s, HEAD_DIM), lambda b, h: (part, b, 0, h))


def _seq_spec(s):
    return pl.BlockSpec((1, s, HEAD_DIM), lambda b, h: (b, 0, h))


_GAIN_SPEC = pl.BlockSpec((1, HEAD_DIM), lambda b, h: (0, 0))
_HEAD_ROW_SPEC = pl.BlockSpec((1, 1, 1, HEAD_DIM), lambda b, h: (b, h, 0, 0))


def _sb_group_spec(s, part):
    return pl.BlockSpec((1, 1, s, SB_GROUP_COLS), lambda b, g: (part, b, 0, g))


def _sb_seq_group_spec(s):
    return pl.BlockSpec((1, s, SB_GROUP_COLS), lambda b, g: (b, 0, g))


_SB_GROUP_ROW_SPEC = pl.BlockSpec((1, SB_HEADS_PER_STEP, 1, HEAD_DIM), lambda b, g: (b, g, 0, 0))


def _sb_chains(m):
    return [(h, m * SB_QBLOCKS_PER_STEP + r) for h in range(SB_HEADS_PER_STEP) for r in range(SB_QBLOCKS_PER_STEP)]


def _sb_logits(qi, kj):
    return _dot(qi, kj, NT) * (HEAD_DIM ** -0.5)


def _sb_scores(z, diag, live, tri_lt):
    soft = jnp.log1p(jnp.exp(-jnp.abs(z)))
    valid = jnp.logical_and(live, jnp.logical_or(jnp.logical_not(diag), tri_lt))
    log_skip = jnp.where(valid, -(jnp.maximum(z, 0.0) + soft), 0.0)
    log_beta = jnp.minimum(z, 0.0) - soft
    return log_skip, log_beta, valid


def _sb_keys_left(chains, state):
    done, carries = state[0], state[1]
    worst = None
    for (_, i), c in zip(chains, carries):
        c = jnp.where(done <= i, c, -jnp.inf)
        worst = c if worst is None else jnp.maximum(worst, c)
    return jnp.logical_and(done <= chains[-1][1],
                           jnp.logical_or(done == 0, jnp.max(worst) > SB_LOG_WEIGHT_FLOOR))


def _sb_key_rows(i, done):
    j = i - done
    return pl.ds(pl.multiple_of(jnp.maximum(j, 0) * ATTN_BLOCK, ATTN_BLOCK), ATTN_BLOCK), j >= 0


def _sb_load_normed(src_ref, gain_ref, dst):
    for h in range(SB_HEADS_PER_STEP):
        x = src_ref[0, 0, :, h * HEAD_DIM:(h + 1) * HEAD_DIM]
        dst[h] = (x * _rms_scale(x) * gain_ref[...]).astype(BF16)


def sb_attn_fwd(proj4, q_gain, k_gain, rider=None):
    _, b, s, _ = proj4.shape
    blk = ATTN_BLOCK
    nq = s // blk

    def body(q_ref, k_ref, v_ref, qg_ref, kg_ref, o_ref, qb, kb, vb):
        _sb_load_normed(q_ref, qg_ref, qb)
        _sb_load_normed(k_ref, kg_ref, kb)
        for h in range(SB_HEADS_PER_STEP):
            vb[h] = v_ref[0, 0, :, h * HEAD_DIM:(h + 1) * HEAD_DIM].astype(BF16)
        row = lax.broadcasted_iota(jnp.int32, (blk, blk), 0)
        col = lax.broadcasted_iota(jnp.int32, (blk, blk), 1)
        tri_lt = col < row
        suffix = (row > col).astype(BF16)

        def q_group(m, _):
            chains = _sb_chains(m)
            qis = [qb[h, pl.ds(pl.multiple_of(i * blk, blk), blk), :] for h, i in chains]

            def k_step(state):
                done, cs, accs = state
                where = [_sb_key_rows(i, done) for _, i in chains]
                zs = [_sb_logits(qi, kb[h, rows, :]) for (h, _), qi, (rows, _) in zip(chains, qis, where)]
                scored = [_sb_scores(z, done == 0, live, tri_lt) for z, (_, live) in zip(zs, where)]
                afters = [_dot_exact(log_skip, suffix) for log_skip, _, _ in scored]
                ws = [jnp.where(valid, jnp.exp(log_beta + after + c), 0.0).astype(BF16)
                      for (_, log_beta, valid), after, c in zip(scored, afters, cs)]
                new_accs = [acc + _dot(w, vb[h, rows, :]) for (h, _), (rows, _), w, acc in zip(chains, where, ws, accs)]
                new_cs = [c + jnp.sum(log_skip, axis=1, keepdims=True) for (log_skip, _, _), c in zip(scored, cs)]
                return done + 1, tuple(new_cs), tuple(new_accs)

            n = len(chains)
            _, _, accs = lax.while_loop(
                functools.partial(_sb_keys_left, chains), k_step,
                (jnp.int32(0), (jnp.zeros((blk, 1), F32),) * n, (jnp.zeros((blk, HEAD_DIM), F32),) * n))
            for (h, i), acc in zip(chains, accs):
                o_ref[0, pl.ds(pl.multiple_of(i * blk, blk), blk), h * HEAD_DIM:(h + 1) * HEAD_DIM] = acc
            return 0

        lax.fori_loop(0, nq // SB_QBLOCKS_PER_STEP, q_group, 0)

    (o,), extra = _pcall_riding(
        body, rider, (proj4, proj4, proj4, q_gain, k_gain),
        name="sb_attn_fwd", grid=(b, N_HEADS // SB_HEADS_PER_STEP),
        in_specs=[_sb_group_spec(s, 0), _sb_group_spec(s, 1), _sb_group_spec(s, 2), _GAIN_SPEC, _GAIN_SPEC],
        out_specs=[_sb_seq_group_spec(s)],
        out_shape=[jax.ShapeDtypeStruct((b, s, D_MODEL), F32)],
        scratch_shapes=[pltpu.VMEM((SB_HEADS_PER_STEP, s, HEAD_DIM), BF16)] * 3,
        semantics=("parallel", "parallel"))
    return o, extra


def _hg_masks():
    c = HG_CHUNK
    row = lax.broadcasted_iota(jnp.int32, (c, c), 0)
    col = lax.broadcasted_iota(jnp.int32, (c, c), 1)
    incl = (col <= row)
    lower = incl.astype(BF16)
    before_sub = (col < (row // HG_SUB) * HG_SUB).astype(BF16)
    upper = (col >= row).astype(BF16)
    return incl, lower, before_sub, upper


def _hg_lower_bound(lbl_ref):
    l0 = lbl_ref[0, 0]
    l1 = lbl_ref[1, 0]
    d = l1 - l0
    return _sigmoid_pair(d)


def _hg_gates(qp, fp, lb, oml):
    sq = _sigmoid(qp)
    sf, sfn = _sigmoid_pair(fp)
    f = lb + oml * sf
    return dict(qp=qp, sq=sq, q=qp * sq, sf=sf, sfn=sfn, f=f, k=oml * sfn, logf=jnp.log(f))


def _hg_intra(qds, ks, gcs, grs, incl):
    subs = range(HG_CHUNK // HG_SUB)
    qdbs = [qd.astype(BF16) for qd in qds]
    ess = [[jnp.exp(jnp.minimum(gr[sub * HG_SUB:sub * HG_SUB + 1, :] - gc, EXP_CLAMP)) for sub in subs]
           for gc, gr in zip(gcs, grs)]
    ksbs = [[(k * e).astype(BF16) for e in es] for k, es in zip(ks, ess)]
    rows = [[_dot(qdb[sub * HG_SUB:(sub + 1) * HG_SUB, :], ksb[sub], NT) for sub in subs]
            for qdb, ksb in zip(qdbs, ksbs)]
    a_s = [jnp.where(incl, jnp.concatenate(r, axis=0), 0.0) for r in rows]
    return a_s, qdbs, ksbs, ess


def _hg_group_rows(outer):
    ns = [outer * HG_UNROLL + u for u in range(HG_UNROLL)]
    return ns, [pl.ds(pl.multiple_of(n * HG_CHUNK, HG_CHUNK), HG_CHUNK) for n in ns]


def _state_spec(nchunk):
    return pl.BlockSpec((1, 1, nchunk, HEAD_DIM, HEAD_DIM), lambda b, h: (b, h, 0, 0, 0))


def hgrn2_fwd(proj4, lbl4, o_gain):
    _, b, s, _ = proj4.shape
    nchunk = s // HG_CHUNK
    c = HG_CHUNK
    assert nchunk % HG_UNROLL == 0

    def body(q_ref, f_ref, i_ref, lbl_ref, og_ref, o_ref, oraw_ref, st_ref):
        incl, lower, before_sub, _ = _hg_masks()
        lb, oml = _hg_lower_bound(lbl_ref)

        def group(outer, st):
            ns, rows = _hg_group_rows(outer)
            vs = [_hg_gates(q_ref[0, 0, r, :], f_ref[0, 0, r, :], lb, oml) for r in rows]
            inps = [i_ref[0, 0, r, :].astype(BF16) for r in rows]
            gcs = [_dot_exact(v["logf"], lower, left=True) for v in vs]
            grs = [_dot_exact(v["logf"], before_sub, left=True) for v in vs]
            a_s, _, _, _ = _hg_intra([v["q"] * jnp.exp(gc - gr) for v, gc, gr in zip(vs, gcs, grs)],
                                     [v["k"] for v in vs], gcs, grs, incl)
            gls = [gc[c - 1:c, :] for gc in gcs]
            adds = [_dot(inp, (v["k"] * jnp.exp(gl - gc)).astype(BF16), TN)
                    for inp, v, gl, gc in zip(inps, vs, gls, gcs)]
            o_intra = [_dot(a.astype(BF16), inp) for a, inp in zip(a_s, inps)]
            sts = []
            for gl, add in zip(gls, adds):
                sts.append(st)
                st = st * jnp.exp(gl) + add
            outs = [oi + _dot((v["q"] * jnp.exp(gc)).astype(BF16), s0.astype(BF16), NT)
                    for oi, v, gc, s0 in zip(o_intra, vs, gcs, sts)]
            for n, r, s0, o in zip(ns, rows, sts, outs):
                st_ref[0, 0, n] = s0
                oraw_ref[0, r, :] = o
                o_ref[0, r, :] = o * _rms_scale(o) * og_ref[...]
            return st

        lax.fori_loop(0, nchunk // HG_UNROLL, group, jnp.zeros((HEAD_DIM, HEAD_DIM), F32))

    seq = jax.ShapeDtypeStruct((b, s, D_MODEL), F32)
    return _pcall(
        body, name="hgrn2_fwd", grid=(b, N_HEADS),
        in_specs=[_head_spec(s, 0), _head_spec(s, 1), _head_spec(s, 2),
                  pl.BlockSpec((2, 1, 1, HEAD_DIM), lambda b, h: (0, h, 0, 0)), _GAIN_SPEC],
        out_specs=[_seq_spec(s), _seq_spec(s), _state_spec(nchunk)],
        out_shape=[seq, seq, jax.ShapeDtypeStruct((b, N_HEADS, nchunk, HEAD_DIM, HEAD_DIM), F32)],
        compiler_params=_params("parallel", "parallel"),
    )(proj4, proj4, proj4, lbl4, o_gain)


def outproj_bwd(dh, w_out, o2, proj, name):
    t = dh.shape[0]
    tm = _row_tile(t, 256)

    def body(dh_ref, w_ref, o_ref, gate_ref, do_ref, dproj_ref, dw_ref):
        dhb = dh_ref[...].astype(BF16)
        dog = _dot(dhb, w_ref[...], NT)
        g = gate_ref[0]
        sg = _sigmoid(g)
        silu = g * sg
        o = o_ref[...]
        do_ref[...] = dog * silu
        dproj_ref[0] = dog * o * (sg * (1.0 + g * (1.0 - sg)))
        part = _dot((o * silu).astype(BF16), dhb, TN)

        @pl.when(pl.program_id(0) == 0)
        def _():
            dw_ref[...] = part

        @pl.when(pl.program_id(0) > 0)
        def _():
            dw_ref[...] += part

    row = pl.BlockSpec((tm, D_MODEL), lambda i: (i, 0))
    full = pl.BlockSpec((D_MODEL, D_MODEL), lambda i: (0, 0))
    return _pcall(
        body, name=name, grid=(t // tm,),
        in_specs=[row, full, row, pl.BlockSpec((1, tm, D_MODEL), lambda i: (3, i, 0))],
        out_specs=[row, pl.BlockSpec((1, tm, D_MODEL), lambda i: (3, i, 0)), full],
        out_shape=[jax.ShapeDtypeStruct((t, D_MODEL), F32),
                   jax.ShapeDtypeStruct((4, t, D_MODEL), F32),
                   jax.ShapeDtypeStruct((D_MODEL, D_MODEL), F32)],
        compiler_params=_params("arbitrary"),
    )(dh, w_out, o2, proj)


def inproj_bwd_dx(dproj, wg, x2, gain, dres, name, rider=None):
    t = x2.shape[0]
    tm = _row_tile(t, 256)

    def body(d_ref, w_ref, x_ref, g_ref, r_ref, dx_ref, dg_ref):
        du = jnp.zeros((tm, D_MODEL), F32)
        for p in range(N_DEV):
            cols = slice((p % 2) * W_COLS, (p % 2 + 1) * W_COLS)
            du = du + _dot(d_ref[p // 2, :, cols].astype(BF16), w_ref[p], NT)
        x = x_ref[...]
        r = _rms_scale(x)
        xh = x * r
        a = du * g_ref[...]
        dx_ref[...] = r_ref[...] + r * (a - xh * jnp.mean(a * xh, axis=-1, keepdims=True))
        part = jnp.sum(du * xh, axis=0, keepdims=True)

        @pl.when(pl.program_id(0) == 0)
        def _():
            dg_ref[...] = part

        @pl.when(pl.program_id(0) > 0)
        def _():
            dg_ref[...] += part

    row = pl.BlockSpec((tm, D_MODEL), lambda i: (i, 0))
    vec = pl.BlockSpec((1, D_MODEL), lambda i: (0, 0))
    return _pcall_riding(
        body, rider, (dproj, wg, x2, gain, dres), name=name, grid=(t // tm,),
        in_specs=[pl.BlockSpec((4, tm, D_MODEL), lambda i: (0, i, 0)),
                  pl.BlockSpec((N_DEV, D_MODEL, W_COLS), lambda i: (0, 0, 0)),
                  row, vec, row],
        out_specs=[row, vec],
        out_shape=[jax.ShapeDtypeStruct((t, D_MODEL), F32), jax.ShapeDtypeStruct((1, D_MODEL), F32)],
        semantics=("arbitrary",))


def inproj_bwd_dw(ut, dproj, name, out_dtype=F32):
    t = ut.shape[1]

    def body(ut_ref, d_ref, dw_ref):
        dw_ref[0] = _dot(ut_ref[...], d_ref[0].astype(BF16)).astype(dw_ref.dtype)

    return _pcall(
        body, name=name, grid=(N_DEV,),
        in_specs=[pl.BlockSpec((D_MODEL, t), lambda j: (0, 0)),
                  pl.BlockSpec((1, t, W_COLS), lambda j: (j // 2, 0, j % 2))],
        out_specs=pl.BlockSpec((1, D_MODEL, W_COLS), lambda j: (j, 0, 0)),
        out_shape=jax.ShapeDtypeStruct((N_DEV, D_MODEL, W_COLS), out_dtype),
        compiler_params=_params("parallel"),
    )(ut, dproj)


def _rms_bwd(x, gain, dy):
    r = _rms_scale(x)
    xh = x * r
    a = dy * gain
    return r * (a - xh * jnp.mean(a * xh, axis=-1, keepdims=True)), dy * xh


def sb_attn_bwd(proj4, do3, o3, q_gain, k_gain, dproj4, rider=None):
    _, b, s, _ = proj4.shape
    blk = ATTN_BLOCK
    nq = s // blk

    def body(q_ref, k_ref, v_ref, do_ref, o_ref, qg_ref, kg_ref, _alias, d_ref, dqg_ref, dkg_ref, qb, kb, vb, dob):
        _sb_load_normed(q_ref, qg_ref, qb)
        _sb_load_normed(k_ref, kg_ref, kb)
        for h in range(SB_HEADS_PER_STEP):
            cols = slice(h * HEAD_DIM, (h + 1) * HEAD_DIM)
            vb[h] = v_ref[0, 0, :, cols].astype(BF16)
            dob[h] = do_ref[0, :, cols].astype(BF16)
        d_ref[...] = jnp.zeros_like(d_ref)
        row = lax.broadcasted_iota(jnp.int32, (blk, blk), 0)
        col = lax.broadcasted_iota(jnp.int32, (blk, blk), 1)
        tri_lt = col < row
        suffix = (row > col).astype(BF16)
        suffix_incl = (row >= col).astype(BF16)

        def q_group(m, _):
            chains = _sb_chains(m)
            qis, dois, deltas = [], [], []
            for h, i in chains:
                rows_i = pl.ds(pl.multiple_of(i * blk, blk), blk)
                qis.append(qb[h, rows_i, :])
                dois.append(dob[h, rows_i, :])
                deltas.append(jnp.sum(dois[-1].astype(F32) * o_ref[0, rows_i, h * HEAD_DIM:(h + 1) * HEAD_DIM],
                                      axis=1, keepdims=True))

            def k_step(state):
                done, cs, cgs, dqs = state
                where = [_sb_key_rows(i, done) for _, i in chains]
                kjs = [kb[h, rows, :] for (h, _), (rows, _) in zip(chains, where)]
                zs = [_sb_logits(qi, kj) for qi, kj in zip(qis, kjs)]
                dws = [_dot(doi, vb[h, rows, :], NT) for (h, _), (rows, _), doi in zip(chains, where, dois)]
                scored = [_sb_scores(z, done == 0, live, tri_lt) for z, (_, live) in zip(zs, where)]
                afters = [_dot_exact(log_skip, suffix) for log_skip, _, _ in scored]
                wbs = [jnp.where(valid, jnp.exp(log_beta + after + c), 0.0).astype(BF16)
                       for (_, log_beta, valid), after, c in zip(scored, afters, cs)]
                gs = [dw * wb.astype(F32) for dw, wb in zip(dws, wbs)]
                befores = [delta - (_dot_exact(g, suffix_incl) + cg) for g, delta, cg in zip(gs, deltas, cgs)]
                dzbs = []
                for (_, log_beta, valid), g, before in zip(scored, gs, befores):
                    beta = jnp.exp(log_beta)
                    dz = jnp.where(valid, g * (1.0 - beta) - before * beta, 0.0) * (HEAD_DIM ** -0.5)
                    dzbs.append(dz.astype(BF16))
                new_dqs = [dq + _dot(dzb, kj) for dq, dzb, kj in zip(dqs, dzbs, kjs)]
                for (h, _), (rows, _), wb, doi, dzb, qi in zip(chains, where, wbs, dois, dzbs, qis):
                    cols = slice(h * HEAD_DIM, (h + 1) * HEAD_DIM)
                    d_ref[2, 0, rows, cols] += _dot(wb, doi, TN)
                    d_ref[1, 0, rows, cols] += _dot(dzb, qi, TN)
                new_cs = [c + jnp.sum(log_skip, axis=1, keepdims=True) for (log_skip, _, _), c in zip(scored, cs)]
                new_cgs = [cg + jnp.sum(g, axis=1, keepdims=True) for g, cg in zip(gs, cgs)]
                return done + 1, tuple(new_cs), tuple(new_cgs), tuple(new_dqs)

            n = len(chains)
            zero = (jnp.zeros((blk, 1), F32),) * n
            _, _, _, dqs = lax.while_loop(functools.partial(_sb_keys_left, chains), k_step,
                                          (jnp.int32(0), zero, zero, (jnp.zeros((blk, HEAD_DIM), F32),) * n))
            for (h, i), dq in zip(chains, dqs):
                d_ref[0, 0, pl.ds(pl.multiple_of(i * blk, blk), blk), h * HEAD_DIM:(h + 1) * HEAD_DIM] = dq
            return 0

        lax.fori_loop(0, nq // SB_QBLOCKS_PER_STEP, q_group, 0)

        def norm_block(i, carry):
            rows = pl.ds(pl.multiple_of(i * blk, blk), blk)
            out = []
            for h in range(SB_HEADS_PER_STEP):
                cols = slice(h * HEAD_DIM, (h + 1) * HEAD_DIM)
                for part, src_ref, gain_ref in ((0, q_ref, qg_ref), (1, k_ref, kg_ref)):
                    dx, pg = _rms_bwd(src_ref[0, 0, rows, cols], gain_ref[...], d_ref[part, 0, rows, cols])
                    d_ref[part, 0, rows, cols] = dx
                    out.append(carry[len(out)] + jnp.sum(pg, axis=0, keepdims=True))
            return tuple(out)

        sums = lax.fori_loop(0, nq, norm_block, (jnp.zeros((1, HEAD_DIM), F32),) * (2 * SB_HEADS_PER_STEP))
        for h in range(SB_HEADS_PER_STEP):
            dqg_ref[0, h] = sums[2 * h]
            dkg_ref[0, h] = sums[2 * h + 1]

    head_row = jax.ShapeDtypeStruct((b, N_HEADS, 1, HEAD_DIM), F32)
    return _pcall_riding(
        body, rider, (proj4, proj4, proj4, do3, o3, q_gain, k_gain, dproj4),
        name="sb_attn_bwd", grid=(b, N_HEADS // SB_HEADS_PER_STEP),
        in_specs=[_sb_group_spec(s, 0), _sb_group_spec(s, 1), _sb_group_spec(s, 2),
                  _sb_seq_group_spec(s), _sb_seq_group_spec(s), _GAIN_SPEC, _GAIN_SPEC,
                  pl.BlockSpec(memory_space=pl.ANY)],
        out_specs=[pl.BlockSpec((3, 1, s, SB_GROUP_COLS), lambda b, g: (0, b, 0, g)),
                   _SB_GROUP_ROW_SPEC, _SB_GROUP_ROW_SPEC],
        out_shape=[jax.ShapeDtypeStruct(dproj4.shape, F32), head_row, head_row],
        scratch_shapes=[pltpu.VMEM((SB_HEADS_PER_STEP, s, HEAD_DIM), BF16)] * 4,
        input_output_aliases={7: 0}, semantics=("parallel", "parallel"))


def hgrn2_bwd(proj4, don3, oraw3, states, lbl4, o_gain, dproj4, rider=None):
    _, b, s, _ = proj4.shape
    nchunk = s // HG_CHUNK
    c = HG_CHUNK
    subs = range(HG_CHUNK // HG_SUB)
    ngroup = nchunk // HG_UNROLL

    def body(q_ref, f_ref, i_ref, don_ref, oraw_ref, st_ref, lbl_ref, og_ref, _alias, d_ref, dog_ref, dlb_ref):
        incl, lower, before_sub, upper = _hg_masks()
        lb, oml = _hg_lower_bound(lbl_ref)
        last_row = lax.broadcasted_iota(jnp.int32, (c, HEAD_DIM), 0) == c - 1

        def group(m, carry):
            dst, dog_acc, dlb_acc = carry
            ns, rows = _hg_group_rows(ngroup - 1 - m)
            ns, rows = ns[::-1], rows[::-1]
            vs = [_hg_gates(q_ref[0, 0, r, :], f_ref[0, 0, r, :], lb, oml) for r in rows]
            inps = [i_ref[0, 0, r, :].astype(BF16) for r in rows]
            sts = [st_ref[0, 0, n] for n in ns]
            gcs = [_dot_exact(v["logf"], lower, left=True) for v in vs]
            grs = [_dot_exact(v["logf"], before_sub, left=True) for v in vs]
            e_qs = [jnp.exp(gc - gr) for gc, gr in zip(gcs, grs)]
            a_s, qdbs, ksbs, ess = _hg_intra([v["q"] * e for v, e in zip(vs, e_qs)], [v["k"] for v in vs],
                                             gcs, grs, incl)
            e_gcs = [jnp.exp(gc) for gc in gcs]
            gls = [gc[c - 1:c, :] for gc in gcs]
            e_gls = [jnp.exp(gl) for gl in gls]
            e_ks = [jnp.exp(gl - gc) for gl, gc in zip(gls, gcs)]
            normed = [_rms_bwd(oraw_ref[0, r, :], og_ref[...], don_ref[0, r, :]) for r in rows]
            dobs = [do.astype(BF16) for do, _ in normed]
            dabs = [jnp.where(incl, _dot(dob, inp, NT), 0.0).astype(BF16) for dob, inp in zip(dobs, inps)]
            adds = [_dot(dob, (v["q"] * e).astype(BF16), TN) for dob, v, e in zip(dobs, vs, e_gcs)]
            dq_inters = [_dot(dob, st.astype(BF16)) * e for dob, st, e in zip(dobs, sts, e_gcs)]
            dqds = [jnp.concatenate([_dot(dab[sub * HG_SUB:(sub + 1) * HG_SUB, :], ksb[sub]) for sub in subs], axis=0)
                    for dab, ksb in zip(dabs, ksbs)]
            dkss = [[_dot(dab[sub * HG_SUB:(sub + 1) * HG_SUB, :], qdb[sub * HG_SUB:(sub + 1) * HG_SUB, :], TN)
                     for sub in subs] for dab, qdb in zip(dabs, qdbs)]
            dsts = []
            for e_gl, add in zip(e_gls, adds):
                dsts.append(dst)
                dst = dst * e_gl + add
            dstbs = [d.astype(BF16) for d in dsts]
            dis = [_dot(a.astype(BF16), dob, TN) + _dot((v["k"] * e_k).astype(BF16), dstb, NT)
                   for a, dob, v, e_k, dstb in zip(a_s, dobs, vs, e_ks, dstbs)]
            dk_inters = [_dot(inp, dstb) * e_k for inp, dstb, e_k in zip(inps, dstbs, e_ks)]
            for u, r in enumerate(rows):
                v, q, k = vs[u], vs[u]["q"], vs[u]["k"]
                dk, dgc_k = dk_inters[u], jnp.zeros((c, HEAD_DIM), F32)
                for sub in subs:
                    dk = dk + dkss[u][sub] * ess[u][sub]
                    dgc_k = dgc_k + dkss[u][sub] * ksbs[u][sub].astype(F32)
                dq = dqds[u] * e_qs[u] + dq_inters[u]
                at_last = (jnp.sum(k * dk_inters[u], axis=0, keepdims=True)
                           + e_gls[u] * jnp.sum(sts[u] * dsts[u], axis=0, keepdims=True))
                dgc = ((qdbs[u].astype(F32) * dqds[u] - dgc_k) + (q * dq_inters[u] - k * dk_inters[u])
                       + jnp.where(last_row, at_last, 0.0))
                dlf_f = _dot_exact(dgc, upper, left=True) / v["f"]
                d_ref[0, 0, r, :] = dq * (v["sq"] * (1.0 + v["qp"] * (1.0 - v["sq"])))
                d_ref[1, 0, r, :] = (dlf_f - dk) * (oml * v["sf"] * v["sfn"])
                d_ref[2, 0, r, :] = dis[u]
                dlb_acc = dlb_acc + jnp.sum((dlf_f - dk) * v["sfn"], axis=0, keepdims=True)
                dog_acc = dog_acc + jnp.sum(normed[u][1], axis=0, keepdims=True)
            return dst, dog_acc, dlb_acc

        zero = jnp.zeros((1, HEAD_DIM), F32)
        _, dog, dlb = lax.fori_loop(0, ngroup, group, (jnp.zeros((HEAD_DIM, HEAD_DIM), F32), zero, zero))
        dog_ref[0, 0] = dog
        dlb_ref[0, 0] = dlb

    head_row = jax.ShapeDtypeStruct((b, N_HEADS, 1, HEAD_DIM), F32)
    return _pcall_riding(
        body, rider, (proj4, proj4, proj4, don3, oraw3, states, lbl4, o_gain, dproj4),
        name="hgrn2_bwd", grid=(b, N_HEADS),
        in_specs=[_head_spec(s, 0), _head_spec(s, 1), _head_spec(s, 2), _seq_spec(s), _seq_spec(s),
                  _state_spec(nchunk), pl.BlockSpec((2, 1, 1, HEAD_DIM), lambda b, h: (0, h, 0, 0)), _GAIN_SPEC,
                  pl.BlockSpec(memory_space=pl.ANY)],
        out_specs=[pl.BlockSpec((3, 1, s, HEAD_DIM), lambda b, h: (0, b, 0, h)), _HEAD_ROW_SPEC, _HEAD_ROW_SPEC],
        out_shape=[jax.ShapeDtypeStruct(dproj4.shape, F32), head_row, head_row],
        input_output_aliases={8: 0}, semantics=("parallel", "parallel"))


def local_step(x, target, sb_norm, wsi, sb_q_gain, sb_k_gain, hg_o_gain, hg_lb_logits, wso_mine, whi_mine, who_mine,
               hg_norm_mine):
    b, s, _ = x.shape
    t = b * s
    x2 = x.reshape(t, D_MODEL)
    tg2 = target.reshape(t, D_MODEL)
    lbl4 = hg_lb_logits.reshape(2, N_HEADS, 1, HEAD_DIM)
    four = (4, b, s, D_MODEL)
    three = (b, s, D_MODEL)
    rows8 = (N_DEV, W_ROWS, D_MODEL)

    (proj0, u0), (wso,) = rms_inproj(x2, sb_norm, wsi, "sb_inproj", _Rider([wso_mine], scatter=False))
    wso = wso.reshape(D_MODEL, D_MODEL)
    o0, (whi, who, hgn) = sb_attn_fwd(proj0.reshape(four), sb_q_gain, sb_k_gain,
                                      _Rider([whi_mine, who_mine, hg_norm_mine], scatter=False))
    who = who.reshape(D_MODEL, D_MODEL)
    hg_norm_full = hgn[:, 0, :].reshape(1, D_MODEL)
    o0 = o0.reshape(t, D_MODEL)
    h1 = gate_outproj(o0, proj0, wso, x2, None, "sb_outproj")
    (proj1, u1), _ = rms_inproj(h1, hg_norm_full, whi, "hg_inproj")
    o1, o1_raw, states = hgrn2_fwd(proj1.reshape(four), lbl4, hg_o_gain)
    o1 = o1.reshape(t, D_MODEL)
    dh2, loss_parts = gate_outproj(o1, proj1, who, h1, tg2, "hg_outproj_loss")

    do1, dproj1, g_who = outproj_bwd(dh2, who, o1, proj1, "hg_outproj_bwd")
    (dproj1, g_og, g_lb), (p_who,) = hgrn2_bwd(proj1.reshape(four), do1.reshape(three), o1_raw, states, lbl4,
                                               hg_o_gain, dproj1.reshape(four),
                                               _Rider([g_who.reshape(rows8)], scatter=True))
    dproj1 = dproj1.reshape(4, t, D_MODEL)
    (dh1, g_hgn), _ = inproj_bwd_dx(dproj1, whi, h1, hg_norm_full, dh2, "hg_inproj_bwd_dx")
    g_whi = inproj_bwd_dw(u1, dproj1, "hg_inproj_bwd_dw")

    do0, dproj0, g_wso = outproj_bwd(dh1, wso, o0, proj0, "sb_outproj_bwd")
    (dproj0, g_qg, g_kg), (p_whi, p_wso) = sb_attn_bwd(proj0.reshape(four), do0.reshape(three), o0.reshape(three),
                                                       sb_q_gain, sb_k_gain, dproj0.reshape(four),
                                                       _Rider([g_whi, g_wso.reshape(rows8)], scatter=True))
    dproj0 = dproj0.reshape(4, t, D_MODEL)
    g_wsi = inproj_bwd_dw(u0, dproj0, "sb_inproj_bwd_dw", out_dtype=BF16)
    (gx, g_sbn), (p_wsi,) = inproj_bwd_dx(dproj0, wsi, x2, sb_norm, dh1, "sb_inproj_bwd_dx",
                                          _Rider([g_wsi], scatter=True))
    return dict(loss_parts=loss_parts, gx=gx.reshape(three), p_wsi=p_wsi, p_wso=p_wso, p_whi=p_whi, p_who=p_who,
                g_sbn=g_sbn, g_hgn=g_hgn, g_qg=g_qg, g_kg=g_kg, g_og=g_og, g_lb=g_lb)


def _two_level_gather(src, out, send_sems, recv_sems, local_sem, pos):
    x, y, c = pos
    me, sibling = (x, y, c), (x, y, 1 - c)
    chips = [(1 - x, y), (x, 1 - y), (1 - x, 1 - y)]

    def copy(k, block, to, source=None):
        slot = out.at[_linear(block)]
        return pltpu.make_async_remote_copy(
            src_ref=slot if source is None else source, dst_ref=slot, send_sem=send_sems.at[k],
            recv_sem=recv_sems.at[k], device_id=to, device_id_type=MESH)

    mine = pltpu.make_async_copy(src, out.at[_linear(me)], local_sem)
    mine.start()
    first = [copy(0, me, sibling, src)] + [copy(1 + j, me, (*chip, c), src) for j, chip in enumerate(chips)]
    for cp in first:
        cp.start()
    passed = [copy(4 + j, (*chip, c), sibling) for j, chip in enumerate(chips)]
    for j, chip in enumerate(chips):
        copy(1 + j, (*chip, c), me).wait_recv()
        passed[j].start()
    copy(0, sibling, me).wait_recv()
    for j, chip in enumerate(chips):
        copy(4 + j, (*chip, 1 - c), me).wait_recv()
    for cp in first + passed:
        cp.wait_send()
    mine.wait()


def gather_first_weights(w_si, w_so, w_hi, w_ho, hg_norm):
    def body(si_ref, so_ref, hi_ref, ho_ref, hn_ref, o_si, so_b, hi_b, ho_b, hn_b, si_b, send_sems, recv_sems, local_sem):
        for src, buf in ((si_ref, si_b), (so_ref, so_b), (hi_ref, hi_b), (ho_ref, ho_b)):
            buf[...] = src[...].astype(BF16)
        hn_b[...] = jnp.broadcast_to(hn_ref[...], hn_b.shape)
        _two_level_gather(si_b, o_si, send_sems, recv_sems, local_sem, _mesh_pos())

    return _pcall(
        body, name="gather_first_weights",
        in_specs=[_VMEM] * 5, out_specs=[_ANY] + [_VMEM] * 4,
        out_shape=[jax.ShapeDtypeStruct((N_DEV,) + w_si.shape, BF16), jax.ShapeDtypeStruct(w_so.shape, BF16),
                   jax.ShapeDtypeStruct(w_hi.shape, BF16), jax.ShapeDtypeStruct(w_ho.shape, BF16),
                   jax.ShapeDtypeStruct((8, HEAD_DIM), F32)],
        scratch_shapes=[pltpu.VMEM(w_si.shape, BF16), pltpu.SemaphoreType.DMA((N_PEERS,)),
                        pltpu.SemaphoreType.DMA((N_PEERS,)), pltpu.SemaphoreType.DMA],
        compiler_params=pltpu.CompilerParams(vmem_limit_bytes=VMEM_LIMIT_BYTES),
    )(w_si, w_so, w_hi, w_ho, hg_norm)


def _adamw(w, g, m, v):
    m = ADAM_B1 * m + (1.0 - ADAM_B1) * g
    v = ADAM_B2 * v + (1.0 - ADAM_B2) * (g * g)
    m_hat = m / (1.0 - ADAM_B1 ** ADAM_STEP)
    v_hat = v / (1.0 - ADAM_B2 ** ADAM_STEP)
    delta = -ADAM_LR * (m_hat / (jnp.sqrt(v_hat) + ADAM_EPS) + ADAM_WD * w)
    return delta, m, v


def reduce_adamw(parts, w, m, v, name):
    _, r, c = parts.shape
    tr = _row_tile(r, 256)

    def body(p_ref, w_ref, m_ref, v_ref, g_ref, d_ref, m2_ref, v2_ref):
        g = p_ref[0].astype(F32)
        for dev in range(1, N_DEV):
            g = g + p_ref[dev].astype(F32)
        g_ref[...] = g
        d_ref[...], m2_ref[...], v2_ref[...] = _adamw(w_ref[...], g, m_ref[...], v_ref[...])

    tile = pl.BlockSpec((tr, c), lambda i: (i, 0))
    return _pcall(
        body, name=name, grid=(r // tr,),
        in_specs=[pl.BlockSpec((N_DEV, tr, c), lambda i: (0, i, 0)), tile, tile, tile],
        out_specs=[tile] * 4, out_shape=[jax.ShapeDtypeStruct((r, c), F32)] * 4,
        compiler_params=_params("parallel"),
    )(parts, w, m, v)


PACK_ROWS = 32
ROW_SBN, ROW_HGN, ROW_LB, ROW_QG, ROW_KG, ROW_OG, ROW_LOSS = 0, 8, 16, 24, 25, 26, 27


def small_update(g_sbn, g_hgn, g_lb, g_qg, g_kg, g_og, loss_parts, small):
    n_in = 7 + len(small)

    def body(*refs):
        sbn_ref, hgn_ref, lb_ref, qg_ref, kg_ref, og_ref, loss_ref = refs[:7]
        wmv = refs[7:n_in]
        outs = refs[n_in:n_in + 25]
        pack, gath, tot, send_sems, recv_sems, local_sems = refs[n_in + 25:]
        pos = _mesh_pos()
        me = _linear(pos)
        pack[...] = jnp.zeros_like(pack)
        pack[ROW_SBN:ROW_SBN + 8, :] = sbn_ref[...]
        pack[ROW_HGN:ROW_HGN + 8, :] = hgn_ref[...]
        pack[ROW_LB:ROW_LB + 8, :] = jnp.sum(lb_ref[...], axis=0)
        pack[ROW_QG:ROW_QG + 1, :] = jnp.sum(qg_ref[...], axis=0, keepdims=True)
        pack[ROW_KG:ROW_KG + 1, :] = jnp.sum(kg_ref[...], axis=0, keepdims=True)
        pack[ROW_OG:ROW_OG + 1, :] = jnp.sum(og_ref[...], axis=0, keepdims=True)
        pack[ROW_LOSS:ROW_LOSS + 1, :] = jnp.sum(loss_ref[...], axis=0)[0:1, :]
        _exchange_start([((lambda p: pack), gath)], send_sems, recv_sems, local_sems, pos)
        _exchange_wait([((lambda p: pack), gath)], send_sems, recv_sems, local_sems, pos)
        total = gath[0]
        for dev in range(1, N_DEV):
            total = total + gath[dev]
        tot[...] = total
        outs[0][...] = jnp.broadcast_to(tot[ROW_LOSS:ROW_LOSS + 1, :], (8, HEAD_DIM))
        l0 = wmv[15][0:8, :]
        l1 = wmv[15][8:16, :]
        p1, p0 = _sigmoid_pair(l1 - l0)
        d_l1 = p0 * p1 * tot[ROW_LB:ROW_LB + 8, :]
        grads = [tot[ROW_SBN:ROW_SBN + 8, :], tot[ROW_QG:ROW_QG + 1, :], tot[ROW_KG:ROW_KG + 1, :],
                 tot[pl.ds(ROW_HGN + me, 1), :], tot[ROW_OG:ROW_OG + 1, :],
                 jnp.concatenate([-d_l1, d_l1], axis=0)]
        for i, g in enumerate(grads):
            w_ref, m_ref, v_ref = wmv[3 * i:3 * i + 3]
            o = outs[1 + 4 * i:5 + 4 * i]
            o[0][...] = g
            o[1][...], o[2][...], o[3][...] = _adamw(w_ref[...], g, m_ref[...], v_ref[...])

    out_shape = [jax.ShapeDtypeStruct((8, HEAD_DIM), F32)]
    for i in range(6):
        out_shape += [jax.ShapeDtypeStruct(small[3 * i].shape, F32)] * 4
    return _pcall(
        body, name="small_update",
        in_specs=[_VMEM] * n_in, out_specs=[_VMEM] * 25, out_shape=out_shape,
        scratch_shapes=[pltpu.VMEM((PACK_ROWS, HEAD_DIM), F32), pltpu.VMEM((N_DEV, PACK_ROWS, HEAD_DIM), F32),
                        pltpu.VMEM((PACK_ROWS, HEAD_DIM), F32)] + _exchange_sems(1),
    )(g_sbn, g_hgn, g_lb, g_qg, g_kg, g_og, loss_parts, *small)


def kernel(x, sb_norm, sb_w_in, sb_q_gain, sb_k_gain, sb_w_out, hg_norm, hg_w_in, hg_o_gain, hg_w_out, hg_lb_logits, loss_target, m_sb_norm, m_sb_w_in, m_sb_q_gain, m_sb_k_gain, m_sb_w_out, m_hg_norm, m_hg_w_in, m_hg_o_gain, m_hg_w_out, m_hg_lb_logits, v_sb_norm, v_sb_w_in, v_sb_q_gain, v_sb_k_gain, v_sb_w_out, v_hg_norm, v_hg_w_in, v_hg_o_gain, v_hg_w_out, v_hg_lb_logits):
    b = x.shape[0]
    wsi, wso_mine, whi_mine, who_mine, hg_norm_mine = gather_first_weights(
        sb_w_in[0], sb_w_out[0], hg_w_in[0], hg_w_out[0], hg_norm)
    r = local_step(x, loss_target, sb_norm, wsi, sb_q_gain, sb_k_gain, hg_o_gain, hg_lb_logits,
                   wso_mine, whi_mine, who_mine, hg_norm_mine)
    big = {}
    for name, p, w, m, v in (("sb_w_in", r["p_wsi"], sb_w_in, m_sb_w_in, v_sb_w_in),
                             ("sb_w_out", r["p_wso"], sb_w_out, m_sb_w_out, v_sb_w_out),
                             ("hg_w_in", r["p_whi"], hg_w_in, m_hg_w_in, v_hg_w_in),
                             ("hg_w_out", r["p_who"], hg_w_out, m_hg_w_out, v_hg_w_out)):
        big[name] = [o[None] for o in reduce_adamw(p, w[0], m[0], v[0], "adamw_" + name)]

    def rows8(a):
        return a.reshape(8, HEAD_DIM)

    def rows16(a):
        return a.reshape(16, HEAD_DIM)

    small_in = [rows8(sb_norm), rows8(m_sb_norm), rows8(v_sb_norm),
                sb_q_gain, m_sb_q_gain, v_sb_q_gain,
                sb_k_gain, m_sb_k_gain, v_sb_k_gain,
                hg_norm, m_hg_norm, v_hg_norm,
                hg_o_gain, m_hg_o_gain, v_hg_o_gain,
                rows16(hg_lb_logits), rows16(m_hg_lb_logits), rows16(v_hg_lb_logits)]
    so = small_update(rows8(r["g_sbn"]), rows8(r["g_hgn"]), r["g_lb"].reshape(b, N_HEADS, HEAD_DIM),
                      r["g_qg"].reshape(b * N_HEADS, HEAD_DIM), r["g_kg"].reshape(b * N_HEADS, HEAD_DIM),
                      r["g_og"].reshape(b * N_HEADS, HEAD_DIM), r["loss_parts"], small_in)
    loss = so[0][0, 0]
    shapes = {"sb_norm": (1, D_MODEL), "sb_q_gain": (1, HEAD_DIM), "sb_k_gain": (1, HEAD_DIM),
              "hg_norm": (1, HEAD_DIM), "hg_o_gain": (1, HEAD_DIM), "hg_lb_logits": (2, D_MODEL)}
    small = {}
    for i, name in enumerate(("sb_norm", "sb_q_gain", "sb_k_gain", "hg_norm", "hg_o_gain", "hg_lb_logits")):
        small[name] = [o.reshape(shapes[name]) for o in so[1 + 4 * i:5 + 4 * i]]
    order = ("sb_norm", "sb_w_in", "sb_q_gain", "sb_k_gain", "sb_w_out",
             "hg_norm", "hg_w_in", "hg_o_gain", "hg_w_out", "hg_lb_logits")
    res = {**big, **small}
    return (loss, r["gx"]) + tuple(res[n][j] for j in range(4) for n in order)
```

```python
import functools

import jax
import jax.numpy as jnp
from jax import lax
from jax.experimental import pallas as pl
from jax.experimental.pallas import tpu as pltpu

F32 = jnp.float32
BF16 = jnp.bfloat16

N_DEV = 8
D_MODEL = 1024
N_HEADS = 8
HEAD_DIM = 128
RMS_EPS = 1e-6
ATTN_BLOCK = 128
HG_CHUNK = 64
HG_SUB = 16
HG_UNROLL = 8
EXP_CLAMP = 80.0
SB_HEADS_PER_STEP = 2
SB_QBLOCKS_PER_STEP = 4
SB_GROUP_COLS = SB_HEADS_PER_STEP * 128
SB_TOP_ROWS = 32
SB_LOG_WEIGHT_FLOOR = -104.0
VMEM_LIMIT_BYTES = 48 * 1024 * 1024
W_COLS = 4 * D_MODEL // N_DEV
W_ROWS = D_MODEL // N_DEV

ADAM_LR = 0.001
ADAM_B1 = 0.9
ADAM_B2 = 0.999
ADAM_EPS = 1e-08
ADAM_WD = 0.01
ADAM_STEP = 10

NT = (((1,), (1,)), ((), ()))
TN = (((0,), (0,)), ((), ()))
NN = (((1,), (0,)), ((), ()))


def _pcall(body, *, name, **kw):
    return pl.pallas_call(body, name=name, **kw)


def _params(*sem):
    return pltpu.CompilerParams(dimension_semantics=sem, vmem_limit_bytes=VMEM_LIMIT_BYTES)


def _dot(a, b, dims=NN):
    return lax.dot_general(a, b, dims, preferred_element_type=F32)


def _dot_exact(a, m, dims=NN, left=False):
    hi = a.astype(BF16)
    lo = (a - hi.astype(F32)).astype(BF16)
    if left:
        return _dot(m, hi, dims) + _dot(m, lo, dims)
    return _dot(hi, m, dims) + _dot(lo, m, dims)


def _split(a):
    hi = a.astype(BF16)
    return hi, (a - hi.astype(F32)).astype(BF16)


def _dot3(a, b, dims=NN):
    return _dot(a[0], b[0], dims) + (_dot(a[0], b[1], dims) + _dot(a[1], b[0], dims))


def _sigmoid(x):
    return 1.0 / (1.0 + jnp.exp(-x))


def _sigmoid_pair(x):
    e = jnp.exp(-jnp.abs(x))
    big = 1.0 / (1.0 + e)
    small = e * big
    pos = x >= 0
    return jnp.where(pos, big, small), jnp.where(pos, small, big)


def _rms_scale(x):
    return lax.rsqrt(jnp.mean(x * x, axis=-1, keepdims=True) + RMS_EPS)


def _row_tile(t, want):
    return want if t % want == 0 else t


MESH = pl.DeviceIdType.MESH
N_PEERS = N_DEV - 1
_ANY = pl.BlockSpec(memory_space=pl.ANY)
_VMEM = pl.BlockSpec(memory_space=pltpu.VMEM)


def _mesh_pos():
    return lax.axis_index("x"), lax.axis_index("y"), lax.axis_index("c")


def _linear(pos):
    return 4 * pos[0] + 2 * pos[1] + pos[2]


def _peer(pos, k):
    flips = ((k + 1) >> 2 & 1, (k + 1) >> 1 & 1, (k + 1) & 1)
    return tuple(1 - p if f else p for p, f in zip(pos, flips))


def _exchange_copies(pairs, send_sems, recv_sems, local_sems, pos, landing):
    me = _linear(pos)
    local, remote = [], []
    for a, (src_of, dst) in enumerate(pairs):
        local.append(pltpu.make_async_copy(src_of(me), dst.at[me], local_sems.at[a]))
        for k in range(N_PEERS):
            peer = _peer(pos, k)
            remote.append(pltpu.make_async_remote_copy(
                src_ref=src_of(_linear(peer)), dst_ref=dst.at[_linear(peer) if landing else me],
                send_sem=send_sems.at[a, k], recv_sem=recv_sems.at[a, k], device_id=peer, device_id_type=MESH))
    return local, remote


def _exchange_start(pairs, send_sems, recv_sems, local_sems, pos):
    local, sent = _exchange_copies(pairs, send_sems, recv_sems, local_sems, pos, landing=False)
    for copy in local + sent:
        copy.start()


def _exchange_wait(pairs, send_sems, recv_sems, local_sems, pos):
    local, landed = _exchange_copies(pairs, send_sems, recv_sems, local_sems, pos, landing=True)
    for copy in landed:
        copy.wait_recv()
        copy.wait_send()
    for copy in local:
        copy.wait()


def _exchange_sems(n):
    return [pltpu.SemaphoreType.DMA((n, N_PEERS)), pltpu.SemaphoreType.DMA((n, N_PEERS)),
            pltpu.SemaphoreType.DMA((n,))]


class _Rider:
    def __init__(self, arrays, scatter):
        self.arrays = list(arrays)
        self.scatter = scatter
        self.out_shapes = [jax.ShapeDtypeStruct(a.shape if scatter else (N_DEV,) + a.shape, a.dtype)
                           for a in self.arrays]

    def pairs(self, in_refs, out_refs):
        if self.scatter:
            return [((lambda p, r=r: r.at[p]), o) for r, o in zip(in_refs, out_refs)]
        return [((lambda p, r=r: r), o) for r, o in zip(in_refs, out_refs)]


def _pcall_riding(body, rider, args, *, name, grid, in_specs, out_specs, out_shape, semantics, scratch_shapes=(),
                  input_output_aliases=None):
    aliases = input_output_aliases or {}
    if rider is None:
        outs = _pcall(body, name=name, grid=grid, in_specs=list(in_specs), out_specs=list(out_specs),
                      out_shape=list(out_shape), scratch_shapes=list(scratch_shapes), input_output_aliases=aliases,
                      compiler_params=_params(*semantics))(*args)
        return list(outs), []
    n_in, n_out, n_scr, n_r = len(in_specs), len(out_specs), len(scratch_shapes), len(rider.arrays)

    def riding(*refs):
        ins, refs = refs[:n_in], refs[n_in:]
        rider_in, refs = refs[:n_r], refs[n_r:]
        outs, refs = refs[:n_out], refs[n_out:]
        rider_out, refs = refs[:n_r], refs[n_r:]
        scratch, sems = refs[:n_scr], refs[n_scr:]
        pairs = rider.pairs(rider_in, rider_out)
        first = functools.reduce(jnp.logical_and, [pl.program_id(a) == 0 for a in range(len(grid))])
        last = functools.reduce(jnp.logical_and, [pl.program_id(a) == g - 1 for a, g in enumerate(grid)])

        @pl.when(first)
        def _():
            _exchange_start(pairs, *sems, _mesh_pos())

        body(*ins, *outs, *scratch)

        @pl.when(last)
        def _():
            _exchange_wait(pairs, *sems, _mesh_pos())

    outs = _pcall(riding, name=name, grid=grid, in_specs=list(in_specs) + [_ANY] * n_r,
                  out_specs=list(out_specs) + [_ANY] * n_r, out_shape=list(out_shape) + rider.out_shapes,
                  scratch_shapes=list(scratch_shapes) + _exchange_sems(n_r), input_output_aliases=aliases,
                  compiler_params=_params(*(("arbitrary",) * len(grid))))(*args, *rider.arrays)
    return list(outs[:n_out]), list(outs[n_out:])


def rms_inproj(x2, gain, wg, name, rider=None):
    t = x2.shape[0]
    tm = _row_tile(t, 256)

    def body(x_ref, g_ref, w_ref, proj_ref, ut_ref):
        x = x_ref[...]
        u = x * _rms_scale(x) * g_ref[...]
        ut_ref[...] = u.T.astype(BF16)
        u = u.astype(BF16)
        for p in range(N_DEV):
            proj_ref[p // 2, :, (p % 2) * W_COLS:(p % 2 + 1) * W_COLS] = _dot(u, w_ref[p])

    return _pcall_riding(
        body, rider, (x2, gain, wg), name=name,
        grid=(t // tm,),
        in_specs=[pl.BlockSpec((tm, D_MODEL), lambda i: (i, 0)),
                  pl.BlockSpec((1, D_MODEL), lambda i: (0, 0)),
                  pl.BlockSpec((N_DEV, D_MODEL, W_COLS), lambda i: (0, 0, 0))],
        out_specs=[pl.BlockSpec((4, tm, D_MODEL), lambda i: (0, i, 0)),
                   pl.BlockSpec((D_MODEL, tm), lambda i: (0, i))],
        out_shape=[jax.ShapeDtypeStruct((4, t, D_MODEL), F32),
                   jax.ShapeDtypeStruct((D_MODEL, t), BF16)],
        semantics=("parallel",))


def gate_outproj(o2, proj, w_out, resid, target, name):
    t = o2.shape[0]
    tm = _row_tile(t, 256)
    with_loss = target is not None

    def body(o_ref, gate_ref, w_ref, r_ref, *rest):
        g = gate_ref[0]
        og = (o_ref[...] * (g * _sigmoid(g))).astype(BF16)
        h = r_ref[...] + _dot(og, w_ref[...])
        if with_loss:
            t_ref, dh_ref, loss_ref = rest
            err = h - t_ref[...]
            dh_ref[...] = err * (1.0 / D_MODEL)
            part = 0.5 * jnp.sum(jnp.mean(err * err, axis=-1, keepdims=True))
            loss_ref[...] = jnp.full(loss_ref.shape, part, F32)
        else:
            (h_ref,) = rest
            h_ref[...] = h

    row = pl.BlockSpec((tm, D_MODEL), lambda i: (i, 0))
    in_specs = [row,
                pl.BlockSpec((1, tm, D_MODEL), lambda i: (3, i, 0)),
                pl.BlockSpec((D_MODEL, D_MODEL), lambda i: (0, 0)),
                row]
    args = [o2, proj, w_out, resid]
    if with_loss:
        in_specs.append(row)
        args.append(target)
        out_specs = [row, pl.BlockSpec((1, 8, 128), lambda i: (i, 0, 0))]
        out_shape = [jax.ShapeDtypeStruct((t, D_MODEL), F32),
                     jax.ShapeDtypeStruct((t // tm, 8, 128), F32)]
    else:
        out_specs = row
        out_shape = jax.ShapeDtypeStruct((t, D_MODEL), F32)
    return _pcall(body, name=name, grid=(t // tm,), in_specs=in_specs, out_specs=out_specs,
                  out_shape=out_shape, compiler_params=_params("parallel"))(*args)


def _head_spec(s, part):
    return pl.BlockSpec((1, 1, s, HEAD_DIM), lambda b, h: (part, b, 0, h))


def _seq_spec(s):
    return pl.BlockSpec((1, s, HEAD_DIM), lambda b, h: (b, 0, h))


_GAIN_SPEC = pl.BlockSpec((1, HEAD_DIM), lambda b, h: (0, 0))
_HEAD_ROW_SPEC = pl.BlockSpec((1, 1, 1, HEAD_DIM), lambda b, h: (b, h, 0, 0))


def _sb_group_spec(s, part):
    return pl.BlockSpec((1, 1, s, SB_GROUP_COLS), lambda b, g: (part, b, 0, g))


def _sb_seq_group_spec(s):
    return pl.BlockSpec((1, s, SB_GROUP_COLS), lambda b, g: (b, 0, g))


_SB_GROUP_ROW_SPEC = pl.BlockSpec((1, SB_HEADS_PER_STEP, 1, HEAD_DIM), lambda b, g: (b, g, 0, 0))


def _sb_chains(m):
    return [(h, m * SB_QBLOCKS_PER_STEP + r) for h in range(SB_HEADS_PER_STEP) for r in range(SB_QBLOCKS_PER_STEP)]


def _sb_logits(qi, kj):
    return _dot(qi, kj, NT) * (HEAD_DIM ** -0.5)


def _sb_scores(z, diag, live, tri_lt):
    soft = jnp.log1p(jnp.exp(-jnp.abs(z)))
    valid = jnp.logical_and(live, jnp.logical_or(jnp.logical_not(diag), tri_lt))
    log_skip = jnp.where(valid, -(jnp.maximum(z, 0.0) + soft), 0.0)
    log_beta = jnp.minimum(z, 0.0) - soft
    return log_skip, log_beta, valid


def _sb_keys_left(chains, watch, state):
    done, carries = state[0], state[1]
    worst = None
    for (_, i), c in zip(chains, carries):
        c = jnp.where(done <= i, c[watch], -jnp.inf)
        worst = c if worst is None else jnp.maximum(worst, c)
    return jnp.logical_and(done <= chains[-1][1],
                           jnp.logical_or(done == 0, jnp.max(worst) > SB_LOG_WEIGHT_FLOOR))


def _sb_key_rows(i, done):
    j = i - done
    return pl.ds(pl.multiple_of(jnp.maximum(j, 0) * ATTN_BLOCK, ATTN_BLOCK), ATTN_BLOCK), j >= 0


def _sb_load_normed(src_ref, gain_ref, dst):
    for h in range(SB_HEADS_PER_STEP):
        x = src_ref[0, 0, :, h * HEAD_DIM:(h + 1) * HEAD_DIM]
        dst[h] = (x * _rms_scale(x) * gain_ref[...]).astype(BF16)


def sb_attn_fwd(proj4, q_gain, k_gain, rider=None):
    _, b, s, _ = proj4.shape
    blk = ATTN_BLOCK
    nq = s // blk

    def body(q_ref, k_ref, v_ref, qg_ref, kg_ref, o_ref, qb, kb, vb):
        _sb_load_normed(q_ref, qg_ref, qb)
        _sb_load_normed(k_ref, kg_ref, kb)
        for h in range(SB_HEADS_PER_STEP):
            vb[h] = v_ref[0, 0, :, h * HEAD_DIM:(h + 1) * HEAD_DIM].astype(BF16)
        row = lax.broadcasted_iota(jnp.int32, (blk, blk), 0)
        col = lax.broadcasted_iota(jnp.int32, (blk, blk), 1)
        tri_lt = col < row
        suffix = (row > col).astype(BF16)

        def q_group(m, _):
            chains = _sb_chains(m)
            qis = [qb[h, pl.ds(pl.multiple_of(i * blk, blk), blk), :] for h, i in chains]

            def walk(top, watch, state):
                qs = [qi[:top, :] for qi in qis]
                mask = tri_lt[:top, :]

                def k_step(state):
                    done, cs, accs = state
                    where = [_sb_key_rows(i, done) for _, i in chains]
                    zs = [_sb_logits(q, kb[h, rows, :]) for (h, _), q, (rows, _) in zip(chains, qs, where)]
                    scored = [_sb_scores(z, done == 0, live, mask) for z, (_, live) in zip(zs, where)]
                    afters = [_dot_exact(log_skip, suffix) for log_skip, _, _ in scored]
                    ws = [jnp.where(valid, jnp.exp(log_beta + after + c), 0.0).astype(BF16)
                          for (_, log_beta, valid), after, c in zip(scored, afters, cs)]
                    new_accs = [acc + _dot(w, vb[h, rows, :])
                                for (h, _), (rows, _), w, acc in zip(chains, where, ws, accs)]
                    new_cs = [c + jnp.sum(log_skip, axis=1, keepdims=True) for (log_skip, _, _), c in zip(scored, cs)]
                    return done + 1, tuple(new_cs), tuple(new_accs)

                return lax.while_loop(functools.partial(_sb_keys_left, chains, watch), k_step, state)

            n = len(chains)
            done, cs, accs = walk(blk, slice(SB_TOP_ROWS, blk),
                                  (jnp.int32(0), (jnp.zeros((blk, 1), F32),) * n, (jnp.zeros((blk, HEAD_DIM), F32),) * n))
            _, _, tops = walk(SB_TOP_ROWS, slice(0, SB_TOP_ROWS),
                              (done, tuple(c[:SB_TOP_ROWS] for c in cs), tuple(a[:SB_TOP_ROWS] for a in accs)))
            for (h, i), acc, top in zip(chains, accs, tops):
                o_ref[0, pl.ds(pl.multiple_of(i * blk, blk), blk), h * HEAD_DIM:(h + 1) * HEAD_DIM] = (
                    jnp.concatenate([top, acc[SB_TOP_ROWS:]], axis=0))
            return 0

        lax.fori_loop(0, nq // SB_QBLOCKS_PER_STEP, q_group, 0)

    (o,), extra = _pcall_riding(
        body, rider, (proj4, proj4, proj4, q_gain, k_gain),
        name="sb_attn_fwd", grid=(b, N_HEADS // SB_HEADS_PER_STEP),
        in_specs=[_sb_group_spec(s, 0), _sb_group_spec(s, 1), _sb_group_spec(s, 2), _GAIN_SPEC, _GAIN_SPEC],
        out_specs=[_sb_seq_group_spec(s)],
        out_shape=[jax.ShapeDtypeStruct((b, s, D_MODEL), F32)],
        scratch_shapes=[pltpu.VMEM((SB_HEADS_PER_STEP, s, HEAD_DIM), BF16)] * 3,
        semantics=("parallel", "parallel"))
    return o, extra


def _hg_masks():
    c = HG_CHUNK
    row = lax.broadcasted_iota(jnp.int32, (c, c), 0)
    col = lax.broadcasted_iota(jnp.int32, (c, c), 1)
    incl = (col <= row)
    lower = incl.astype(BF16)
    before_sub = (col < (row // HG_SUB) * HG_SUB).astype(BF16)
    upper = (col >= row).astype(BF16)
    return incl, lower, before_sub, upper


def _hg_lower_bound(lbl_ref):
    l0 = lbl_ref[0, 0]
    l1 = lbl_ref[1, 0]
    d = l1 - l0
    return _sigmoid_pair(d)


def _hg_gates(qp, fp, lb, oml):
    sq = _sigmoid(qp)
    sf, sfn = _sigmoid_pair(fp)
    f = lb + oml * sf
    return dict(qp=qp, sq=sq, q=qp * sq, sf=sf, sfn=sfn, f=f, k=oml * sfn, logf=jnp.log(f))


def _hg_intra(qds, ks, gcs, grs, incl):
    subs = range(HG_CHUNK // HG_SUB)
    qdbs = [qd.astype(BF16) for qd in qds]
    ess = [[jnp.exp(jnp.minimum(gr[sub * HG_SUB:sub * HG_SUB + 1, :] - gc, EXP_CLAMP)) for sub in subs]
           for gc, gr in zip(gcs, grs)]
    ksbs = [[(k * e).astype(BF16) for e in es] for k, es in zip(ks, ess)]
    rows = [[_dot(qdb[sub * HG_SUB:(sub + 1) * HG_SUB, :], ksb[sub], NT) for sub in subs]
            for qdb, ksb in zip(qdbs, ksbs)]
    a_s = [jnp.where(incl, jnp.concatenate(r, axis=0), 0.0) for r in rows]
    return a_s, qdbs, ksbs, ess


def _hg_group_rows(outer):
    ns = [outer * HG_UNROLL + u for u in range(HG_UNROLL)]
    return ns, [pl.ds(pl.multiple_of(n * HG_CHUNK, HG_CHUNK), HG_CHUNK) for n in ns]


def _state_spec(nchunk):
    return pl.BlockSpec((1, 1, nchunk, HEAD_DIM, HEAD_DIM), lambda b, h: (b, h, 0, 0, 0))


def hgrn2_fwd(proj4, lbl4, o_gain):
    _, b, s, _ = proj4.shape
    nchunk = s // HG_CHUNK
    c = HG_CHUNK
    assert nchunk % HG_UNROLL == 0

    def body(q_ref, f_ref, i_ref, lbl_ref, og_ref, o_ref, oraw_ref, st_ref):
        incl, lower, before_sub, _ = _hg_masks()
        lb, oml = _hg_lower_bound(lbl_ref)

        def group(outer, st):
            ns, rows = _hg_group_rows(outer)
            vs = [_hg_gates(q_ref[0, 0, r, :], f_ref[0, 0, r, :], lb, oml) for r in rows]
            inps = [i_ref[0, 0, r, :].astype(BF16) for r in rows]
            gcs = [_dot_exact(v["logf"], lower, left=True) for v in vs]
            grs = [_dot_exact(v["logf"], before_sub, left=True) for v in vs]
            a_s, _, _, _ = _hg_intra([v["q"] * jnp.exp(gc - gr) for v, gc, gr in zip(vs, gcs, grs)],
                                     [v["k"] for v in vs], gcs, grs, incl)
            gls = [gc[c - 1:c, :] for gc in gcs]
            adds = [_dot(inp, (v["k"] * jnp.exp(gl - gc)).astype(BF16), TN)
                    for inp, v, gl, gc in zip(inps, vs, gls, gcs)]
            o_intra = [_dot(a.astype(BF16), inp) for a, inp in zip(a_s, inps)]
            sts = []
            for gl, add in zip(gls, adds):
                sts.append(st)
                st = st * jnp.exp(gl) + add
            outs = [oi + _dot((v["q"] * jnp.exp(gc)).astype(BF16), s0.astype(BF16), NT)
                    for oi, v, gc, s0 in zip(o_intra, vs, gcs, sts)]
            for n, r, s0, o in zip(ns, rows, sts, outs):
                st_ref[0, 0, n] = s0
                oraw_ref[0, r, :] = o
                o_ref[0, r, :] = o * _rms_scale(o) * og_ref[...]
            return st

        lax.fori_loop(0, nchunk // HG_UNROLL, group, jnp.zeros((HEAD_DIM, HEAD_DIM), F32))

    seq = jax.ShapeDtypeStruct((b, s, D_MODEL), F32)
    return _pcall(
        body, name="hgrn2_fwd", grid=(b, N_HEADS),
        in_specs=[_head_spec(s, 0), _head_spec(s, 1), _head_spec(s, 2),
                  pl.BlockSpec((2, 1, 1, HEAD_DIM), lambda b, h: (0, h, 0, 0)), _GAIN_SPEC],
        out_specs=[_seq_spec(s), _seq_spec(s), _state_spec(nchunk)],
        out_shape=[seq, seq, jax.ShapeDtypeStruct((b, N_HEADS, nchunk, HEAD_DIM, HEAD_DIM), F32)],
        compiler_params=_params("parallel", "parallel"),
    )(proj4, proj4, proj4, lbl4, o_gain)


def outproj_bwd(dh, w_out, o2, proj, name):
    t = dh.shape[0]
    tm = _row_tile(t, 256)

    def body(dh_ref, w_ref, o_ref, gate_ref, do_ref, dproj_ref, dw_ref):
        dhb = dh_ref[...].astype(BF16)
        dog = _dot(dhb, w_ref[...], NT)
        g = gate_ref[0]
        sg = _sigmoid(g)
        silu = g * sg
        o = o_ref[...]
        do_ref[...] = dog * silu
        dproj_ref[0] = dog * o * (sg * (1.0 + g * (1.0 - sg)))
        part = _dot((o * silu).astype(BF16), dhb, TN)

        @pl.when(pl.program_id(0) == 0)
        def _():
            dw_ref[...] = part

        @pl.when(pl.program_id(0) > 0)
        def _():
            dw_ref[...] += part

    row = pl.BlockSpec((tm, D_MODEL), lambda i: (i, 0))
    full = pl.BlockSpec((D_MODEL, D_MODEL), lambda i: (0, 0))
    return _pcall(
        body, name=name, grid=(t // tm,),
        in_specs=[row, full, row, pl.BlockSpec((1, tm, D_MODEL), lambda i: (3, i, 0))],
        out_specs=[row, pl.BlockSpec((1, tm, D_MODEL), lambda i: (3, i, 0)), full],
        out_shape=[jax.ShapeDtypeStruct((t, D_MODEL), F32),
                   jax.ShapeDtypeStruct((4, t, D_MODEL), F32),
                   jax.ShapeDtypeStruct((D_MODEL, D_MODEL), F32)],
        compiler_params=_params("arbitrary"),
    )(dh, w_out, o2, proj)


def inproj_bwd_dx(dproj, wg, x2, gain, dres, name, rider=None):
    t = x2.shape[0]
    tm = _row_tile(t, 256)

    def body(d_ref, w_ref, x_ref, g_ref, r_ref, dx_ref, dg_ref):
        du = jnp.zeros((tm, D_MODEL), F32)
        for p in range(N_DEV):
            cols = slice((p % 2) * W_COLS, (p % 2 + 1) * W_COLS)
            du = du + _dot(d_ref[p // 2, :, cols].astype(BF16), w_ref[p], NT)
        x = x_ref[...]
        r = _rms_scale(x)
        xh = x * r
        a = du * g_ref[...]
        dx_ref[...] = r_ref[...] + r * (a - xh * jnp.mean(a * xh, axis=-1, keepdims=True))
        part = jnp.sum(du * xh, axis=0, keepdims=True)

        @pl.when(pl.program_id(0) == 0)
        def _():
            dg_ref[...] = part

        @pl.when(pl.program_id(0) > 0)
        def _():
            dg_ref[...] += part

    row = pl.BlockSpec((tm, D_MODEL), lambda i: (i, 0))
    vec = pl.BlockSpec((1, D_MODEL), lambda i: (0, 0))
    return _pcall_riding(
        body, rider, (dproj, wg, x2, gain, dres), name=name, grid=(t // tm,),
        in_specs=[pl.BlockSpec((4, tm, D_MODEL), lambda i: (0, i, 0)),
                  pl.BlockSpec((N_DEV, D_MODEL, W_COLS), lambda i: (0, 0, 0)),
                  row, vec, row],
        out_specs=[row, vec],
        out_shape=[jax.ShapeDtypeStruct((t, D_MODEL), F32), jax.ShapeDtypeStruct((1, D_MODEL), F32)],
        semantics=("arbitrary",))


def inproj_bwd_dw(ut, dproj, name, out_dtype=F32):
    t = ut.shape[1]

    def body(ut_ref, d_ref, dw_ref):
        dw_ref[0] = _dot(ut_ref[...], d_ref[0].astype(BF16)).astype(dw_ref.dtype)

    return _pcall(
        body, name=name, grid=(N_DEV,),
        in_specs=[pl.BlockSpec((D_MODEL, t), lambda j: (0, 0)),
                  pl.BlockSpec((1, t, W_COLS), lambda j: (j // 2, 0, j % 2))],
        out_specs=pl.BlockSpec((1, D_MODEL, W_COLS), lambda j: (j, 0, 0)),
        out_shape=jax.ShapeDtypeStruct((N_DEV, D_MODEL, W_COLS), out_dtype),
        compiler_params=_params("parallel"),
    )(ut, dproj)


def _rms_bwd(x, gain, dy):
    r = _rms_scale(x)
    xh = x * r
    a = dy * gain
    return r * (a - xh * jnp.mean(a * xh, axis=-1, keepdims=True)), dy * xh


def sb_attn_bwd(proj4, do3, o3, q_gain, k_gain, dproj4, rider=None):
    _, b, s, _ = proj4.shape
    blk = ATTN_BLOCK
    nq = s // blk

    def body(q_ref, k_ref, v_ref, do_ref, o_ref, qg_ref, kg_ref, _alias, d_ref, dqg_ref, dkg_ref, qb, kb, vb, dob):
        _sb_load_normed(q_ref, qg_ref, qb)
        _sb_load_normed(k_ref, kg_ref, kb)
        for h in range(SB_HEADS_PER_STEP):
            cols = slice(h * HEAD_DIM, (h + 1) * HEAD_DIM)
            vb[h] = v_ref[0, 0, :, cols].astype(BF16)
            dob[h] = do_ref[0, :, cols].astype(BF16)
        d_ref[...] = jnp.zeros_like(d_ref)
        row = lax.broadcasted_iota(jnp.int32, (blk, blk), 0)
        col = lax.broadcasted_iota(jnp.int32, (blk, blk), 1)
        tri_lt = col < row
        suffix = (row > col).astype(BF16)
        suffix_incl = (row >= col).astype(BF16)

        def q_group(m, _):
            chains = _sb_chains(m)
            qis, dois, deltas = [], [], []
            for h, i in chains:
                rows_i = pl.ds(pl.multiple_of(i * blk, blk), blk)
                qis.append(qb[h, rows_i, :])
                dois.append(dob[h, rows_i, :])
                deltas.append(jnp.sum(dois[-1].astype(F32) * o_ref[0, rows_i, h * HEAD_DIM:(h + 1) * HEAD_DIM],
                                      axis=1, keepdims=True))

            def walk(top, watch, state):
                qs = [qi[:top, :] for qi in qis]
                dos = [doi[:top, :] for doi in dois]
                dels = [delta[:top, :] for delta in deltas]
                mask = tri_lt[:top, :]

                def k_step(state):
                    done, cs, cgs, dqs = state
                    where = [_sb_key_rows(i, done) for _, i in chains]
                    kjs = [kb[h, rows, :] for (h, _), (rows, _) in zip(chains, where)]
                    zs = [_sb_logits(q, kj) for q, kj in zip(qs, kjs)]
                    dws = [_dot(do, vb[h, rows, :], NT) for (h, _), (rows, _), do in zip(chains, where, dos)]
                    scored = [_sb_scores(z, done == 0, live, mask) for z, (_, live) in zip(zs, where)]
                    afters = [_dot_exact(log_skip, suffix) for log_skip, _, _ in scored]
                    wbs = [jnp.where(valid, jnp.exp(log_beta + after + c), 0.0).astype(BF16)
                           for (_, log_beta, valid), after, c in zip(scored, afters, cs)]
                    gs = [dw * wb.astype(F32) for dw, wb in zip(dws, wbs)]
                    befores = [delta - (_dot_exact(g, suffix_incl) + cg) for g, delta, cg in zip(gs, dels, cgs)]
                    dzbs = []
                    for (_, log_beta, valid), g, before in zip(scored, gs, befores):
                        beta = jnp.exp(log_beta)
                        dz = jnp.where(valid, g * (1.0 - beta) - before * beta, 0.0) * (HEAD_DIM ** -0.5)
                        dzbs.append(dz.astype(BF16))
                    new_dqs = [dq + _dot(dzb, kj) for dq, dzb, kj in zip(dqs, dzbs, kjs)]
                    for (h, _), (rows, _), wb, do, dzb, q in zip(chains, where, wbs, dos, dzbs, qs):
                        cols = slice(h * HEAD_DIM, (h + 1) * HEAD_DIM)
                        d_ref[2, 0, rows, cols] += _dot(wb, do, TN)
                        d_ref[1, 0, rows, cols] += _dot(dzb, q, TN)
                    new_cs = [c + jnp.sum(log_skip, axis=1, keepdims=True) for (log_skip, _, _), c in zip(scored, cs)]
                    new_cgs = [cg + jnp.sum(g, axis=1, keepdims=True) for g, cg in zip(gs, cgs)]
                    return done + 1, tuple(new_cs), tuple(new_cgs), tuple(new_dqs)

                return lax.while_loop(functools.partial(_sb_keys_left, chains, watch), k_step, state)

            n = len(chains)
            zero = (jnp.zeros((blk, 1), F32),) * n
            done, cs, cgs, dqs = walk(blk, slice(SB_TOP_ROWS, blk),
                                      (jnp.int32(0), zero, zero, (jnp.zeros((blk, HEAD_DIM), F32),) * n))
            top = slice(0, SB_TOP_ROWS)
            _, _, _, tops = walk(SB_TOP_ROWS, top, (done, tuple(c[top] for c in cs), tuple(cg[top] for cg in cgs),
                                                    tuple(dq[top] for dq in dqs)))
            for (h, i), dq, dq_top in zip(chains, dqs, tops):
                d_ref[0, 0, pl.ds(pl.multiple_of(i * blk, blk), blk), h * HEAD_DIM:(h + 1) * HEAD_DIM] = (
                    jnp.concatenate([dq_top, dq[SB_TOP_ROWS:]], axis=0))
            return 0

        lax.fori_loop(0, nq // SB_QBLOCKS_PER_STEP, q_group, 0)

        def norm_block(i, carry):
            rows = pl.ds(pl.multiple_of(i * blk, blk), blk)
            out = []
            for h in range(SB_HEADS_PER_STEP):
                cols = slice(h * HEAD_DIM, (h + 1) * HEAD_DIM)
                for part, src_ref, gain_ref in ((0, q_ref, qg_ref), (1, k_ref, kg_ref)):
                    dx, pg = _rms_bwd(src_ref[0, 0, rows, cols], gain_ref[...], d_ref[part, 0, rows, cols])
                    d_ref[part, 0, rows, cols] = dx
                    out.append(carry[len(out)] + jnp.sum(pg, axis=0, keepdims=True))
            return tuple(out)

        sums = lax.fori_loop(0, nq, norm_block, (jnp.zeros((1, HEAD_DIM), F32),) * (2 * SB_HEADS_PER_STEP))
        for h in range(SB_HEADS_PER_STEP):
            dqg_ref[0, h] = sums[2 * h]
            dkg_ref[0, h] = sums[2 * h + 1]

    head_row = jax.ShapeDtypeStruct((b, N_HEADS, 1, HEAD_DIM), F32)
    return _pcall_riding(
        body, rider, (proj4, proj4, proj4, do3, o3, q_gain, k_gain, dproj4),
        name="sb_attn_bwd", grid=(b, N_HEADS // SB_HEADS_PER_STEP),
        in_specs=[_sb_group_spec(s, 0), _sb_group_spec(s, 1), _sb_group_spec(s, 2),
                  _sb_seq_group_spec(s), _sb_seq_group_spec(s), _GAIN_SPEC, _GAIN_SPEC,
                  pl.BlockSpec(memory_space=pl.ANY)],
        out_specs=[pl.BlockSpec((3, 1, s, SB_GROUP_COLS), lambda b, g: (0, b, 0, g)),
                   _SB_GROUP_ROW_SPEC, _SB_GROUP_ROW_SPEC],
        out_shape=[jax.ShapeDtypeStruct(dproj4.shape, F32), head_row, head_row],
        scratch_shapes=[pltpu.VMEM((SB_HEADS_PER_STEP, s, HEAD_DIM), BF16)] * 4,
        input_output_aliases={7: 0}, semantics=("parallel", "parallel"))


def hgrn2_bwd(proj4, don3, oraw3, states, lbl4, o_gain, dproj4, rider=None):
    _, b, s, _ = proj4.shape
    nchunk = s // HG_CHUNK
    c = HG_CHUNK
    subs = range(HG_CHUNK // HG_SUB)
    ngroup = nchunk // HG_UNROLL

    def body(q_ref, f_ref, i_ref, don_ref, oraw_ref, st_ref, lbl_ref, og_ref, _alias, d_ref, dog_ref, dlb_ref):
        incl, lower, before_sub, upper = _hg_masks()
        lb, oml = _hg_lower_bound(lbl_ref)
        last_row = lax.broadcasted_iota(jnp.int32, (c, HEAD_DIM), 0) == c - 1

        def group(m, carry):
            dst, dog_acc, dlb_acc = carry
            ns, rows = _hg_group_rows(ngroup - 1 - m)
            ns, rows = ns[::-1], rows[::-1]
            vs = [_hg_gates(q_ref[0, 0, r, :], f_ref[0, 0, r, :], lb, oml) for r in rows]
            inps = [i_ref[0, 0, r, :].astype(BF16) for r in rows]
            sts = [st_ref[0, 0, n] for n in ns]
            gcs = [_dot_exact(v["logf"], lower, left=True) for v in vs]
            grs = [_dot_exact(v["logf"], before_sub, left=True) for v in vs]
            e_qs = [jnp.exp(gc - gr) for gc, gr in zip(gcs, grs)]
            a_s, qdbs, ksbs, ess = _hg_intra([v["q"] * e for v, e in zip(vs, e_qs)], [v["k"] for v in vs],
                                             gcs, grs, incl)
            e_gcs = [jnp.exp(gc) for gc in gcs]
            gls = [gc[c - 1:c, :] for gc in gcs]
            e_gls = [jnp.exp(gl) for gl in gls]
            e_ks = [jnp.exp(gl - gc) for gl, gc in zip(gls, gcs)]
            normed = [_rms_bwd(oraw_ref[0, r, :], og_ref[...], don_ref[0, r, :]) for r in rows]
            dobs = [do.astype(BF16) for do, _ in normed]
            dabs = [jnp.where(incl, _dot(dob, inp, NT), 0.0).astype(BF16) for dob, inp in zip(dobs, inps)]
            adds = [_dot(dob, (v["q"] * e).astype(BF16), TN) for dob, v, e in zip(dobs, vs, e_gcs)]
            dq_inters = [_dot(dob, st.astype(BF16)) * e for dob, st, e in zip(dobs, sts, e_gcs)]
            dqds = [jnp.concatenate([_dot(dab[sub * HG_SUB:(sub + 1) * HG_SUB, :], ksb[sub]) for sub in subs], axis=0)
                    for dab, ksb in zip(dabs, ksbs)]
            dkss = [[_dot(dab[sub * HG_SUB:(sub + 1) * HG_SUB, :], qdb[sub * HG_SUB:(sub + 1) * HG_SUB, :], TN)
                     for sub in subs] for dab, qdb in zip(dabs, qdbs)]
            dsts = []
            for e_gl, add in zip(e_gls, adds):
                dsts.append(dst)
                dst = dst * e_gl + add
            dstbs = [d.astype(BF16) for d in dsts]
            dis = [_dot(a.astype(BF16), dob, TN) + _dot((v["k"] * e_k).astype(BF16), dstb, NT)
                   for a, dob, v, e_k, dstb in zip(a_s, dobs, vs, e_ks, dstbs)]
            dk_inters = [_dot(inp, dstb) * e_k for inp, dstb, e_k in zip(inps, dstbs, e_ks)]
            for u, r in enumerate(rows):
                v, q, k = vs[u], vs[u]["q"], vs[u]["k"]
                dk, dgc_k = dk_inters[u], jnp.zeros((c, HEAD_DIM), F32)
                for sub in subs:
                    dk = dk + dkss[u][sub] * ess[u][sub]
                    dgc_k = dgc_k + dkss[u][sub] * ksbs[u][sub].astype(F32)
                dq = dqds[u] * e_qs[u] + dq_inters[u]
                at_last = (jnp.sum(k * dk_inters[u], axis=0, keepdims=True)
                           + e_gls[u] * jnp.sum(sts[u] * dsts[u], axis=0, keepdims=True))
                dgc = ((qdbs[u].astype(F32) * dqds[u] - dgc_k) + (q * dq_inters[u] - k * dk_inters[u])
                       + jnp.where(last_row, at_last, 0.0))
                dlf_f = _dot_exact(dgc, upper, left=True) / v["f"]
                d_ref[0, 0, r, :] = dq * (v["sq"] * (1.0 + v["qp"] * (1.0 - v["sq"])))
                d_ref[1, 0, r, :] = (dlf_f - dk) * (oml * v["sf"] * v["sfn"])
                d_ref[2, 0, r, :] = dis[u]
                dlb_acc = dlb_acc + jnp.sum((dlf_f - dk) * v["sfn"], axis=0, keepdims=True)
                dog_acc = dog_acc + jnp.sum(normed[u][1], axis=0, keepdims=True)
            return dst, dog_acc, dlb_acc

        zero = jnp.zeros((1, HEAD_DIM), F32)
        _, dog, dlb = lax.fori_loop(0, ngroup, group, (jnp.zeros((HEAD_DIM, HEAD_DIM), F32), zero, zero))
        dog_ref[0, 0] = dog
        dlb_ref[0, 0] = dlb

    head_row = jax.ShapeDtypeStruct((b, N_HEADS, 1, HEAD_DIM), F32)
    return _pcall_riding(
        body, rider, (proj4, proj4, proj4, don3, oraw3, states, lbl4, o_gain, dproj4),
        name="hgrn2_bwd", grid=(b, N_HEADS),
        in_specs=[_head_spec(s, 0), _head_spec(s, 1), _head_spec(s, 2), _seq_spec(s), _seq_spec(s),
                  _state_spec(nchunk), pl.BlockSpec((2, 1, 1, HEAD_DIM), lambda b, h: (0, h, 0, 0)), _GAIN_SPEC,
                  pl.BlockSpec(memory_space=pl.ANY)],
        out_specs=[pl.BlockSpec((3, 1, s, HEAD_DIM), lambda b, h: (0, b, 0, h)), _HEAD_ROW_SPEC, _HEAD_ROW_SPEC],
        out_shape=[jax.ShapeDtypeStruct(dproj4.shape, F32), head_row, head_row],
        input_output_aliases={8: 0}, semantics=("parallel", "parallel"))


def local_step(x, target, sb_norm, wsi, sb_q_gain, sb_k_gain, hg_o_gain, hg_lb_logits, wso_mine, whi_mine, who_mine,
               hg_norm_mine):
    b, s, _ = x.shape
    t = b * s
    x2 = x.reshape(t, D_MODEL)
    tg2 = target.reshape(t, D_MODEL)
    lbl4 = hg_lb_logits.reshape(2, N_HEADS, 1, HEAD_DIM)
    four = (4, b, s, D_MODEL)
    three = (b, s, D_MODEL)
    rows8 = (N_DEV, W_ROWS, D_MODEL)

    (proj0, u0), (wso,) = rms_inproj(x2, sb_norm, wsi, "sb_inproj", _Rider([wso_mine], scatter=False))
    wso = wso.reshape(D_MODEL, D_MODEL)
    o0, (whi, who, hgn) = sb_attn_fwd(proj0.reshape(four), sb_q_gain, sb_k_gain,
                                      _Rider([whi_mine, who_mine, hg_norm_mine], scatter=False))
    who = who.reshape(D_MODEL, D_MODEL)
    hg_norm_full = hgn[:, 0, :].reshape(1, D_MODEL)
    o0 = o0.reshape(t, D_MODEL)
    h1 = gate_outproj(o0, proj0, wso, x2, None, "sb_outproj")
    (proj1, u1), _ = rms_inproj(h1, hg_norm_full, whi, "hg_inproj")
    o1, o1_raw, states = hgrn2_fwd(proj1.reshape(four), lbl4, hg_o_gain)
    o1 = o1.reshape(t, D_MODEL)
    dh2, loss_parts = gate_outproj(o1, proj1, who, h1, tg2, "hg_outproj_loss")

    do1, dproj1, g_who = outproj_bwd(dh2, who, o1, proj1, "hg_outproj_bwd")
    (dproj1, g_og, g_lb), (p_who,) = hgrn2_bwd(proj1.reshape(four), do1.reshape(three), o1_raw, states, lbl4,
                                               hg_o_gain, dproj1.reshape(four),
                                               _Rider([g_who.reshape(rows8)], scatter=True))
    dproj1 = dproj1.reshape(4, t, D_MODEL)
    (dh1, g_hgn), _ = inproj_bwd_dx(dproj1, whi, h1, hg_norm_full, dh2, "hg_inproj_bwd_dx")
    g_whi = inproj_bwd_dw(u1, dproj1, "hg_inproj_bwd_dw")

    do0, dproj0, g_wso = outproj_bwd(dh1, wso, o0, proj0, "sb_outproj_bwd")
    (dproj0, g_qg, g_kg), (p_whi, p_wso) = sb_attn_bwd(proj0.reshape(four), do0.reshape(three), o0.reshape(three),
                                                       sb_q_gain, sb_k_gain, dproj0.reshape(four),
                                                       _Rider([g_whi, g_wso.reshape(rows8)], scatter=True))
    dproj0 = dproj0.reshape(4, t, D_MODEL)
    g_wsi = inproj_bwd_dw(u0, dproj0, "sb_inproj_bwd_dw", out_dtype=BF16)
    (gx, g_sbn), (p_wsi,) = inproj_bwd_dx(dproj0, wsi, x2, sb_norm, dh1, "sb_inproj_bwd_dx",
                                          _Rider([g_wsi], scatter=True))
    return dict(loss_parts=loss_parts, gx=gx.reshape(three), p_wsi=p_wsi, p_wso=p_wso, p_whi=p_whi, p_who=p_who,
                g_sbn=g_sbn, g_hgn=g_hgn, g_qg=g_qg, g_kg=g_kg, g_og=g_og, g_lb=g_lb)


def _two_level_gather(src, out, send_sems, recv_sems, local_sem, pos):
    x, y, c = pos
    me, sibling = (x, y, c), (x, y, 1 - c)
    chips = [(1 - x, y), (x, 1 - y), (1 - x, 1 - y)]

    def copy(k, block, to, source=None):
        slot = out.at[_linear(block)]
        return pltpu.make_async_remote_copy(
            src_ref=slot if source is None else source, dst_ref=slot, send_sem=send_sems.at[k],
            recv_sem=recv_sems.at[k], device_id=to, device_id_type=MESH)

    mine = pltpu.make_async_copy(src, out.at[_linear(me)], local_sem)
    mine.start()
    first = [copy(0, me, sibling, src)] + [copy(1 + j, me, (*chip, c), src) for j, chip in enumerate(chips)]
    for cp in first:
        cp.start()
    passed = [copy(4 + j, (*chip, c), sibling) for j, chip in enumerate(chips)]
    for j, chip in enumerate(chips):
        copy(1 + j, (*chip, c), me).wait_recv()
        passed[j].start()
    copy(0, sibling, me).wait_recv()
    for j, chip in enumerate(chips):
        copy(4 + j, (*chip, 1 - c), me).wait_recv()
    for cp in first + passed:
        cp.wait_send()
    mine.wait()


def gather_first_weights(w_si, w_so, w_hi, w_ho, hg_norm):
    def body(si_ref, so_ref, hi_ref, ho_ref, hn_ref, o_si, so_b, hi_b, ho_b, hn_b, si_b, send_sems, recv_sems, local_sem):
        for src, buf in ((si_ref, si_b), (so_ref, so_b), (hi_ref, hi_b), (ho_ref, ho_b)):
            buf[...] = src[...].astype(BF16)
        hn_b[...] = jnp.broadcast_to(hn_ref[...], hn_b.shape)
        _two_level_gather(si_b, o_si, send_sems, recv_sems, local_sem, _mesh_pos())

    return _pcall(
        body, name="gather_first_weights",
        in_specs=[_VMEM] * 5, out_specs=[_ANY] + [_VMEM] * 4,
        out_shape=[jax.ShapeDtypeStruct((N_DEV,) + w_si.shape, BF16), jax.ShapeDtypeStruct(w_so.shape, BF16),
                   jax.ShapeDtypeStruct(w_hi.shape, BF16), jax.ShapeDtypeStruct(w_ho.shape, BF16),
                   jax.ShapeDtypeStruct((8, HEAD_DIM), F32)],
        scratch_shapes=[pltpu.VMEM(w_si.shape, BF16), pltpu.SemaphoreType.DMA((N_PEERS,)),
                        pltpu.SemaphoreType.DMA((N_PEERS,)), pltpu.SemaphoreType.DMA],
        compiler_params=pltpu.CompilerParams(vmem_limit_bytes=VMEM_LIMIT_BYTES),
    )(w_si, w_so, w_hi, w_ho, hg_norm)


def _adamw(w, g, m, v):
    m = ADAM_B1 * m + (1.0 - ADAM_B1) * g
    v = ADAM_B2 * v + (1.0 - ADAM_B2) * (g * g)
    m_hat = m / (1.0 - ADAM_B1 ** ADAM_STEP)
    v_hat = v / (1.0 - ADAM_B2 ** ADAM_STEP)
    delta = -ADAM_LR * (m_hat / (jnp.sqrt(v_hat) + ADAM_EPS) + ADAM_WD * w)
    return delta, m, v


def reduce_adamw(parts, w, m, v, name):
    _, r, c = parts.shape
    tr = _row_tile(r, 256)

    def body(p_ref, w_ref, m_ref, v_ref, g_ref, d_ref, m2_ref, v2_ref):
        g = p_ref[0].astype(F32)
        for dev in range(1, N_DEV):
            g = g + p_ref[dev].astype(F32)
        g_ref[...] = g
        d_ref[...], m2_ref[...], v2_ref[...] = _adamw(w_ref[...], g, m_ref[...], v_ref[...])

    tile = pl.BlockSpec((tr, c), lambda i: (i, 0))
    return _pcall(
        body, name=name, grid=(r // tr,),
        in_specs=[pl.BlockSpec((N_DEV, tr, c), lambda i: (0, i, 0)), tile, tile, tile],
        out_specs=[tile] * 4, out_shape=[jax.ShapeDtypeStruct((r, c), F32)] * 4,
        compiler_params=_params("parallel"),
    )(parts, w, m, v)


PACK_ROWS = 32
ROW_SBN, ROW_HGN, ROW_LB, ROW_QG, ROW_KG, ROW_OG, ROW_LOSS = 0, 8, 16, 24, 25, 26, 27


def small_update(g_sbn, g_hgn, g_lb, g_qg, g_kg, g_og, loss_parts, small):
    n_in = 7 + len(small)

    def body(*refs):
        sbn_ref, hgn_ref, lb_ref, qg_ref, kg_ref, og_ref, loss_ref = refs[:7]
        wmv = refs[7:n_in]
        outs = refs[n_in:n_in + 25]
        pack, gath, tot, send_sems, recv_sems, local_sems = refs[n_in + 25:]
        pos = _mesh_pos()
        me = _linear(pos)
        pack[...] = jnp.zeros_like(pack)
        pack[ROW_SBN:ROW_SBN + 8, :] = sbn_ref[...]
        pack[ROW_HGN:ROW_HGN + 8, :] = hgn_ref[...]
        pack[ROW_LB:ROW_LB + 8, :] = jnp.sum(lb_ref[...], axis=0)
        pack[ROW_QG:ROW_QG + 1, :] = jnp.sum(qg_ref[...], axis=0, keepdims=True)
        pack[ROW_KG:ROW_KG + 1, :] = jnp.sum(kg_ref[...], axis=0, keepdims=True)
        pack[ROW_OG:ROW_OG + 1, :] = jnp.sum(og_ref[...], axis=0, keepdims=True)
        pack[ROW_LOSS:ROW_LOSS + 1, :] = jnp.sum(loss_ref[...], axis=0)[0:1, :]
        _exchange_start([((lambda p: pack), gath)], send_sems, recv_sems, local_sems, pos)
        _exchange_wait([((lambda p: pack), gath)], send_sems, recv_sems, local_sems, pos)
        total = gath[0]
        for dev in range(1, N_DEV):
            total = total + gath[dev]
        tot[...] = total
        outs[0][...] = jnp.broadcast_to(tot[ROW_LOSS:ROW_LOSS + 1, :], (8, HEAD_DIM))
        l0 = wmv[15][0:8, :]
        l1 = wmv[15][8:16, :]
        p1, p0 = _sigmoid_pair(l1 - l0)
        d_l1 = p0 * p1 * tot[ROW_LB:ROW_LB + 8, :]
        grads = [tot[ROW_SBN:ROW_SBN + 8, :], tot[ROW_QG:ROW_QG + 1, :], tot[ROW_KG:ROW_KG + 1, :],
                 tot[pl.ds(ROW_HGN + me, 1), :], tot[ROW_OG:ROW_OG + 1, :],
                 jnp.concatenate([-d_l1, d_l1], axis=0)]
        for i, g in enumerate(grads):
            w_ref, m_ref, v_ref = wmv[3 * i:3 * i + 3]
            o = outs[1 + 4 * i:5 + 4 * i]
            o[0][...] = g
            o[1][...], o[2][...], o[3][...] = _adamw(w_ref[...], g, m_ref[...], v_ref[...])

    out_shape = [jax.ShapeDtypeStruct((8, HEAD_DIM), F32)]
    for i in range(6):
        out_shape += [jax.ShapeDtypeStruct(small[3 * i].shape, F32)] * 4
    return _pcall(
        body, name="small_update",
        in_specs=[_VMEM] * n_in, out_specs=[_VMEM] * 25, out_shape=out_shape,
        scratch_shapes=[pltpu.VMEM((PACK_ROWS, HEAD_DIM), F32), pltpu.VMEM((N_DEV, PACK_ROWS, HEAD_DIM), F32),
                        pltpu.VMEM((PACK_ROWS, HEAD_DIM), F32)] + _exchange_sems(1),
    )(g_sbn, g_hgn, g_lb, g_qg, g_kg, g_og, loss_parts, *small)


def kernel(x, sb_norm, sb_w_in, sb_q_gain, sb_k_gain, sb_w_out, hg_norm, hg_w_in, hg_o_gain, hg_w_out, hg_lb_logits, loss_target, m_sb_norm, m_sb_w_in, m_sb_q_gain, m_sb_k_gain, m_sb_w_out, m_hg_norm, m_hg_w_in, m_hg_o_gain, m_hg_w_out, m_hg_lb_logits, v_sb_norm, v_sb_w_in, v_sb_q_gain, v_sb_k_gain, v_sb_w_out, v_hg_norm, v_hg_w_in, v_hg_o_gain, v_hg_w_out, v_hg_lb_logits):
    b = x.shape[0]
    wsi, wso_mine, whi_mine, who_mine, hg_norm_mine = gather_first_weights(
        sb_w_in[0], sb_w_out[0], hg_w_in[0], hg_w_out[0], hg_norm)
    r = local_step(x, loss_target, sb_norm, wsi, sb_q_gain, sb_k_gain, hg_o_gain, hg_lb_logits,
                   wso_mine, whi_mine, who_mine, hg_norm_mine)
    big = {}
    for name, p, w, m, v in (("sb_w_in", r["p_wsi"], sb_w_in, m_sb_w_in, v_sb_w_in),
                             ("sb_w_out", r["p_wso"], sb_w_out, m_sb_w_out, v_sb_w_out),
                             ("hg_w_in", r["p_whi"], hg_w_in, m_hg_w_in, v_hg_w_in),
                             ("hg_w_out", r["p_who"], hg_w_out, m_hg_w_out, v_hg_w_out)):
        big[name] = [o[None] for o in reduce_adamw(p, w[0], m[0], v[0], "adamw_" + name)]

    def rows8(a):
        return a.reshape(8, HEAD_DIM)

    def rows16(a):
        return a.reshape(16, HEAD_DIM)

    small_in = [rows8(sb_norm), rows8(m_sb_norm), rows8(v_sb_norm),
                sb_q_gain, m_sb_q_gain, v_sb_q_gain,
                sb_k_gain, m_sb_k_gain, v_sb_k_gain,
                hg_norm, m_hg_norm, v_hg_norm,
                hg_o_gain, m_hg_o_gain, v_hg_o_gain,
                rows16(hg_lb_logits), rows16(m_hg_lb_logits), rows16(v_hg_lb_logits)]
    so = small_update(rows8(r["g_sbn"]), rows8(r["g_hgn"]), r["g_lb"].reshape(b, N_HEADS, HEAD_DIM),
                      r["g_qg"].reshape(b * N_HEADS, HEAD_DIM), r["g_kg"].reshape(b * N_HEADS, HEAD_DIM),
                      r["g_og"].reshape(b * N_HEADS, HEAD_DIM), r["loss_parts"], small_in)
    loss = so[0][0, 0]
    shapes = {"sb_norm": (1, D_MODEL), "sb_q_gain": (1, HEAD_DIM), "sb_k_gain": (1, HEAD_DIM),
              "hg_norm": (1, HEAD_DIM), "hg_o_gain": (1, HEAD_DIM), "hg_lb_logits": (2, D_MODEL)}
    small = {}
    for i, name in enumerate(("sb_norm", "sb_q_gain", "sb_k_gain", "hg_norm", "hg_o_gain", "hg_lb_logits")):
        small[name] = [o.reshape(shapes[name]) for o in so[1 + 4 * i:5 + 4 * i]]
    order = ("sb_norm", "sb_w_in", "sb_q_gain", "sb_k_gain", "sb_w_out",
             "hg_norm", "hg_w_in", "hg_o_gain", "hg_w_out", "hg_lb_logits")
    res = {**big, **small}
    return (loss, r["gx"]) + tuple(res[n][j] for j in range(4) for n in order)
```

```python
import functools
import math

import jax
import jax.numpy as jnp
from jax import lax
from jax.experimental import pallas as pl
from jax.experimental.pallas import tpu as pltpu

F32 = jnp.float32
BF16 = jnp.bfloat16

N_DEV = 8
D_MODEL = 1024
N_HEADS = 8
HEAD_DIM = 128
RMS_EPS = 1e-6
ATTN_BLOCK = 128
HG_CHUNK = 64
HG_SUB = 16
HG_UNROLL_FWD = 16
HG_UNROLL_BWD = 8
EXP_CLAMP = 80.0
SB_HEADS_PER_STEP = 2
SB_QBLOCKS_PER_STEP = 4
SB_GROUP_COLS = SB_HEADS_PER_STEP * 128
SB_TOP_ROWS = 32
SB_LOG_WEIGHT_FLOOR = -104.0
VMEM_LIMIT_BYTES = 48 * 1024 * 1024
W_COLS = 4 * D_MODEL // N_DEV
W_ROWS = D_MODEL // N_DEV

ADAM_LR = 0.001
ADAM_B1 = 0.9
ADAM_B2 = 0.999
ADAM_EPS = 1e-08
ADAM_WD = 0.01
ADAM_STEP = 10

NT = (((1,), (1,)), ((), ()))
TN = (((0,), (0,)), ((), ()))
NN = (((1,), (0,)), ((), ()))


def _pcall(body, *, name, **kw):
    return pl.pallas_call(body, name=name, **kw)


def _params(*sem):
    return pltpu.CompilerParams(dimension_semantics=sem, vmem_limit_bytes=VMEM_LIMIT_BYTES)


def _dot(a, b, dims=NN):
    return lax.dot_general(a, b, dims, preferred_element_type=F32)


def _dot_exact(a, m, dims=NN, left=False):
    hi = a.astype(BF16)
    lo = (a - hi.astype(F32)).astype(BF16)
    if left:
        return _dot(m, hi, dims) + _dot(m, lo, dims)
    return _dot(hi, m, dims) + _dot(lo, m, dims)


def _split(a):
    hi = a.astype(BF16)
    return hi, (a - hi.astype(F32)).astype(BF16)


def _dot3(a, b, dims=NN):
    return _dot(a[0], b[0], dims) + (_dot(a[0], b[1], dims) + _dot(a[1], b[0], dims))


def _sigmoid(x):
    return 1.0 / (1.0 + jnp.exp(-x))


def _sigmoid_pair(x):
    e = jnp.exp(-jnp.abs(x))
    big = 1.0 / (1.0 + e)
    small = e * big
    pos = x >= 0
    return jnp.where(pos, big, small), jnp.where(pos, small, big)


def _rms_scale(x):
    return lax.rsqrt(jnp.mean(x * x, axis=-1, keepdims=True) + RMS_EPS)


def _row_tile(t, want):
    return want if t % want == 0 else t


MESH = pl.DeviceIdType.MESH
N_PEERS = N_DEV - 1
_ANY = pl.BlockSpec(memory_space=pl.ANY)
_VMEM = pl.BlockSpec(memory_space=pltpu.VMEM)


def _mesh_pos():
    return lax.axis_index("x"), lax.axis_index("y"), lax.axis_index("c")


def _linear(pos):
    return 4 * pos[0] + 2 * pos[1] + pos[2]


def _peer(pos, k):
    flips = ((k + 1) >> 2 & 1, (k + 1) >> 1 & 1, (k + 1) & 1)
    return tuple(1 - p if f else p for p, f in zip(pos, flips))


def _exchange_copies(pairs, send_sems, recv_sems, local_sems, pos, landing):
    me = _linear(pos)
    local, remote = [], []
    for a, (src_of, dst) in enumerate(pairs):
        local.append(pltpu.make_async_copy(src_of(me), dst.at[me], local_sems.at[a]))
        for k in range(N_PEERS):
            peer = _peer(pos, k)
            remote.append(pltpu.make_async_remote_copy(
                src_ref=src_of(_linear(peer)), dst_ref=dst.at[_linear(peer) if landing else me],
                send_sem=send_sems.at[a, k], recv_sem=recv_sems.at[a, k], device_id=peer, device_id_type=MESH))
    return local, remote


def _exchange_start(pairs, send_sems, recv_sems, local_sems, pos):
    local, sent = _exchange_copies(pairs, send_sems, recv_sems, local_sems, pos, landing=False)
    for copy in local + sent:
        copy.start()


def _exchange_wait(pairs, send_sems, recv_sems, local_sems, pos):
    local, landed = _exchange_copies(pairs, send_sems, recv_sems, local_sems, pos, landing=True)
    for copy in landed:
        copy.wait_recv()
        copy.wait_send()
    for copy in local:
        copy.wait()


def _exchange_sems(n):
    return [pltpu.SemaphoreType.DMA((n, N_PEERS)), pltpu.SemaphoreType.DMA((n, N_PEERS)),
            pltpu.SemaphoreType.DMA((n,))]


class _Rider:
    def __init__(self, arrays, scatter):
        self.arrays = list(arrays)
        self.scatter = scatter
        self.out_shapes = [jax.ShapeDtypeStruct(a.shape if scatter else (N_DEV,) + a.shape, a.dtype)
                           for a in self.arrays]

    def pairs(self, in_refs, out_refs):
        if self.scatter:
            return [((lambda p, r=r: r.at[p]), o) for r, o in zip(in_refs, out_refs)]
        return [((lambda p, r=r: r), o) for r, o in zip(in_refs, out_refs)]


def _pcall_riding(body, rider, args, *, name, grid, in_specs, out_specs, out_shape, semantics, scratch_shapes=(),
                  input_output_aliases=None):
    aliases = input_output_aliases or {}
    if rider is None:
        outs = _pcall(body, name=name, grid=grid, in_specs=list(in_specs), out_specs=list(out_specs),
                      out_shape=list(out_shape), scratch_shapes=list(scratch_shapes), input_output_aliases=aliases,
                      compiler_params=_params(*semantics))(*args)
        return list(outs), []
    n_in, n_out, n_scr, n_r = len(in_specs), len(out_specs), len(scratch_shapes), len(rider.arrays)

    def riding(*refs):
        ins, refs = refs[:n_in], refs[n_in:]
        rider_in, refs = refs[:n_r], refs[n_r:]
        outs, refs = refs[:n_out], refs[n_out:]
        rider_out, refs = refs[:n_r], refs[n_r:]
        scratch, sems = refs[:n_scr], refs[n_scr:]
        pairs = rider.pairs(rider_in, rider_out)
        first = functools.reduce(jnp.logical_and, [pl.program_id(a) == 0 for a in range(len(grid))])
        last = functools.reduce(jnp.logical_and, [pl.program_id(a) == g - 1 for a, g in enumerate(grid)])

        @pl.when(first)
        def _():
            _exchange_start(pairs, *sems, _mesh_pos())

        body(*ins, *outs, *scratch)

        @pl.when(last)
        def _():
            _exchange_wait(pairs, *sems, _mesh_pos())

    outs = _pcall(riding, name=name, grid=grid, in_specs=list(in_specs) + [_ANY] * n_r,
                  out_specs=list(out_specs) + [_ANY] * n_r, out_shape=list(out_shape) + rider.out_shapes,
                  scratch_shapes=list(scratch_shapes) + _exchange_sems(n_r), input_output_aliases=aliases,
                  compiler_params=_params(*(("arbitrary",) * len(grid))))(*args, *rider.arrays)
    return list(outs[:n_out]), list(outs[n_out:])


def rms_inproj(x2, gain, wg, name, rider=None, qk_gains=None):
    t = x2.shape[0]
    tm = _row_tile(t, 256)
    with_qkv = qk_gains is not None

    def body(x_ref, g_ref, w_ref, *rest):
        if with_qkv:
            qg_ref, kg_ref, proj_ref, ut_ref, qkv_ref = rest
            head_gain = (qg_ref, kg_ref)
        else:
            proj_ref, ut_ref = rest
        x = x_ref[...]
        u = x * _rms_scale(x) * g_ref[...]
        ut_ref[...] = u.T.astype(BF16)
        u = u.astype(BF16)
        for p in range(N_DEV):
            part, lo = p // 2, (p % 2) * W_COLS
            res = _dot(u, w_ref[p])
            proj_ref[part, :, lo:lo + W_COLS] = res
            if with_qkv and part < 3:
                for h in range(W_COLS // HEAD_DIM):
                    y = res[:, h * HEAD_DIM:(h + 1) * HEAD_DIM]
                    if part < 2:
                        y = y * _rms_scale(y) * head_gain[part][...]
                    qkv_ref[part, :, lo + h * HEAD_DIM:lo + (h + 1) * HEAD_DIM] = y.astype(BF16)

    vec = pl.BlockSpec((1, D_MODEL), lambda i: (0, 0))
    in_specs = [pl.BlockSpec((tm, D_MODEL), lambda i: (i, 0)), vec,
                pl.BlockSpec((N_DEV, D_MODEL, W_COLS), lambda i: (0, 0, 0))]
    out_specs = [pl.BlockSpec((4, tm, D_MODEL), lambda i: (0, i, 0)), pl.BlockSpec((D_MODEL, tm), lambda i: (0, i))]
    out_shape = [jax.ShapeDtypeStruct((4, t, D_MODEL), F32), jax.ShapeDtypeStruct((D_MODEL, t), BF16)]
    args = (x2, gain, wg)
    if with_qkv:
        in_specs += [pl.BlockSpec((1, HEAD_DIM), lambda i: (0, 0))] * 2
        out_specs.append(pl.BlockSpec((3, tm, D_MODEL), lambda i: (0, i, 0)))
        out_shape.append(jax.ShapeDtypeStruct((3, t, D_MODEL), BF16))
        args += tuple(qk_gains)
    return _pcall_riding(body, rider, args, name=name, grid=(t // tm,), in_specs=in_specs, out_specs=out_specs,
                         out_shape=out_shape, semantics=("parallel",))


def gate_outproj(o2, proj, w_out, resid, target, name):
    t = o2.shape[0]
    tm = _row_tile(t, 256)
    with_loss = target is not None

    def body(o_ref, gate_ref, w_ref, r_ref, *rest):
        g = gate_ref[0]
        og = (o_ref[...] * (g * _sigmoid(g))).astype(BF16)
        h = r_ref[...] + _dot(og, w_ref[...])
        if with_loss:
            t_ref, dh_ref, loss_ref = rest
            err = h - t_ref[...]
            dh_ref[...] = err * (1.0 / D_MODEL)
            part = 0.5 * jnp.sum(jnp.mean(err * err, axis=-1, keepdims=True))
            loss_ref[...] = jnp.full(loss_ref.shape, part, F32)
        else:
            (h_ref,) = rest
            h_ref[...] = h

    row = pl.BlockSpec((tm, D_MODEL), lambda i: (i, 0))
    in_specs = [row,
                pl.BlockSpec((1, tm, D_MODEL), lambda i: (3, i, 0)),
                pl.BlockSpec((D_MODEL, D_MODEL), lambda i: (0, 0)),
                row]
    args = [o2, proj, w_out, resid]
    if with_loss:
        in_specs.append(row)
        args.append(target)
        out_specs = [row, pl.BlockSpec((1, 8, 128), lambda i: (i, 0, 0))]
        out_shape = [jax.ShapeDtypeStruct((t, D_MODEL), F32),
                     jax.ShapeDtypeStruct((t // tm, 8, 128), F32)]
    else:
        out_specs = row
        out_shape = jax.ShapeDtypeStruct((t, D_MODEL), F32)
    return _pcall(body, name=name, grid=(t // tm,), in_specs=in_specs, out_specs=out_specs,
                  out_shape=out_shape, compiler_params=_params("parallel"))(*args)


def _head_spec(s, part):
    return pl.BlockSpec((1, 1, s, HEAD_DIM), lambda b, h: (part, b, 0, h))


def _seq_spec(s):
    return pl.BlockSpec((1, s, HEAD_DIM), lambda b, h: (b, 0, h))


_GAIN_SPEC = pl.BlockSpec((1, HEAD_DIM), lambda b, h: (0, 0))
_HEAD_ROW_SPEC = pl.BlockSpec((1, 1, 1, HEAD_DIM), lambda b, h: (b, h, 0, 0))


def _sb_group_spec(s, part):
    return pl.BlockSpec((1, 1, s, SB_GROUP_COLS), lambda b, g: (part, b, 0, g))


def _sb_seq_group_spec(s):
    return pl.BlockSpec((1, s, SB_GROUP_COLS), lambda b, g: (b, 0, g))


_SB_GROUP_ROW_SPEC = pl.BlockSpec((1, SB_HEADS_PER_STEP, 1, HEAD_DIM), lambda b, g: (b, g, 0, 0))


def _sb_chains(m):
    return [(h, m * SB_QBLOCKS_PER_STEP + r) for h in range(SB_HEADS_PER_STEP) for r in range(SB_QBLOCKS_PER_STEP)]


def _sb_logits(qi, kj):
    return _dot(qi, kj, NT) * (HEAD_DIM ** -0.5)


def _sb_scores(z, diag, live, tri_lt):
    soft = jnp.log(1.0 + jnp.exp(-jnp.abs(z)))
    valid = jnp.logical_and(live, jnp.logical_or(jnp.logical_not(diag), tri_lt))
    log_skip = jnp.where(valid, -(jnp.maximum(z, 0.0) + soft), 0.0)
    log_beta = jnp.minimum(z, 0.0) - soft
    return log_skip, log_beta, valid


def _sb_keys_left(chains, watch, state):
    done, carries = state[0], state[1]
    worst = None
    for (_, i), c in zip(chains, carries):
        c = jnp.where(done <= i, c[watch], -jnp.inf)
        worst = c if worst is None else jnp.maximum(worst, c)
    return jnp.logical_and(done <= chains[-1][1],
                           jnp.logical_or(done == 0, jnp.max(worst) > SB_LOG_WEIGHT_FLOOR))


def _sb_key_rows(i, done):
    j = i - done
    return pl.ds(pl.multiple_of(jnp.maximum(j, 0) * ATTN_BLOCK, ATTN_BLOCK), ATTN_BLOCK), j >= 0


def _sb_head(ref, h, rows):
    return ref[0, 0, rows, h * HEAD_DIM:(h + 1) * HEAD_DIM]


def sb_attn_fwd(qkv4, rider=None):
    _, b, s, _ = qkv4.shape
    blk = ATTN_BLOCK
    nq = s // blk

    def body(q_ref, k_ref, v_ref, o_ref):
        row = lax.broadcasted_iota(jnp.int32, (blk, blk), 0)
        col = lax.broadcasted_iota(jnp.int32, (blk, blk), 1)
        tri_lt = col < row
        suffix = (row > col).astype(BF16)

        def q_group(m, _):
            chains = _sb_chains(m)
            qis = [_sb_head(q_ref, h, pl.ds(pl.multiple_of(i * blk, blk), blk)) for h, i in chains]

            def walk(top, watch, state):
                qs = [qi[:top, :] for qi in qis]
                mask = tri_lt[:top, :]

                def k_step(state):
                    done, cs, accs = state
                    where = [_sb_key_rows(i, done) for _, i in chains]
                    zs = [_sb_logits(q, _sb_head(k_ref, h, rows)) for (h, _), q, (rows, _) in zip(chains, qs, where)]
                    scored = [_sb_scores(z, done == 0, live, mask) for z, (_, live) in zip(zs, where)]
                    afters = [_dot_exact(log_skip, suffix) for log_skip, _, _ in scored]
                    ws = [jnp.where(valid, jnp.exp(log_beta + after + c), 0.0).astype(BF16)
                          for (_, log_beta, valid), after, c in zip(scored, afters, cs)]
                    new_accs = [acc + _dot(w, _sb_head(v_ref, h, rows))
                                for (h, _), (rows, _), w, acc in zip(chains, where, ws, accs)]
                    new_cs = [c + jnp.sum(log_skip, axis=1, keepdims=True) for (log_skip, _, _), c in zip(scored, cs)]
                    return done + 1, tuple(new_cs), tuple(new_accs)

                return lax.while_loop(functools.partial(_sb_keys_left, chains, watch), k_step, state)

            n = len(chains)
            done, cs, accs = walk(blk, slice(SB_TOP_ROWS, blk),
                                  (jnp.int32(0), (jnp.zeros((blk, 1), F32),) * n, (jnp.zeros((blk, HEAD_DIM), F32),) * n))
            _, _, tops = walk(SB_TOP_ROWS, slice(0, SB_TOP_ROWS),
                              (done, tuple(c[:SB_TOP_ROWS] for c in cs), tuple(a[:SB_TOP_ROWS] for a in accs)))
            for (h, i), acc, top in zip(chains, accs, tops):
                o_ref[0, pl.ds(pl.multiple_of(i * blk, blk), blk), h * HEAD_DIM:(h + 1) * HEAD_DIM] = (
                    jnp.concatenate([top, acc[SB_TOP_ROWS:]], axis=0))
            return 0

        lax.fori_loop(0, nq // SB_QBLOCKS_PER_STEP, q_group, 0)

    (o,), extra = _pcall_riding(
        body, rider, (qkv4, qkv4, qkv4),
        name="sb_attn_fwd", grid=(b, N_HEADS // SB_HEADS_PER_STEP),
        in_specs=[_sb_group_spec(s, 0), _sb_group_spec(s, 1), _sb_group_spec(s, 2)],
        out_specs=[_sb_seq_group_spec(s)],
        out_shape=[jax.ShapeDtypeStruct((b, s, D_MODEL), F32)],
        semantics=("parallel", "parallel"))
    return o, extra


def _hg_masks():
    c = HG_CHUNK
    row = lax.broadcasted_iota(jnp.int32, (c, c), 0)
    col = lax.broadcasted_iota(jnp.int32, (c, c), 1)
    incl = (col <= row)
    lower = incl.astype(BF16)
    before_sub = (col < (row // HG_SUB) * HG_SUB).astype(BF16)
    upper = (col >= row).astype(BF16)
    return incl, lower, before_sub, upper


def _hg_lower_bound(lbl_ref):
    l0 = lbl_ref[0, 0]
    l1 = lbl_ref[1, 0]
    d = l1 - l0
    return _sigmoid_pair(d)


def _hg_gates(qp, fp, lb, oml):
    sq = _sigmoid(qp)
    sf, sfn = _sigmoid_pair(fp)
    f = lb + oml * sf
    return dict(qp=qp, sq=sq, q=qp * sq, sf=sf, sfn=sfn, f=f, k=oml * sfn, logf=jnp.log(f))


def _hg_intra(qds, ks, gcs, grs, incl):
    subs = range(HG_CHUNK // HG_SUB)
    qdbs = [qd.astype(BF16) for qd in qds]
    ess = [[jnp.exp(jnp.minimum(gr[sub * HG_SUB:sub * HG_SUB + 1, :] - gc, EXP_CLAMP)) for sub in subs]
           for gc, gr in zip(gcs, grs)]
    ksbs = [[(k * e).astype(BF16) for e in es] for k, es in zip(ks, ess)]
    rows = [[_dot(qdb[sub * HG_SUB:(sub + 1) * HG_SUB, :], ksb[sub], NT) for sub in subs]
            for qdb, ksb in zip(qdbs, ksbs)]
    a_s = [jnp.where(incl, jnp.concatenate(r, axis=0), 0.0) for r in rows]
    return a_s, qdbs, ksbs, ess


def _hg_group_rows(outer, unroll):
    ns = [outer * unroll + u for u in range(unroll)]
    return ns, [pl.ds(pl.multiple_of(n * HG_CHUNK, HG_CHUNK), HG_CHUNK) for n in ns]


def _state_spec(nchunk):
    return pl.BlockSpec((1, 1, nchunk, HEAD_DIM, HEAD_DIM), lambda b, h: (b, h, 0, 0, 0))


def hgrn2_fwd(proj4, lbl4, o_gain):
    _, b, s, _ = proj4.shape
    nchunk = s // HG_CHUNK
    c = HG_CHUNK
    unroll = math.gcd(nchunk, HG_UNROLL_FWD)

    def body(q_ref, f_ref, i_ref, lbl_ref, og_ref, o_ref, oraw_ref, st_ref):
        incl, lower, before_sub, _ = _hg_masks()
        lb, oml = _hg_lower_bound(lbl_ref)

        def group(outer, st):
            ns, rows = _hg_group_rows(outer, unroll)
            vs = [_hg_gates(q_ref[0, 0, r, :], f_ref[0, 0, r, :], lb, oml) for r in rows]
            inps = [i_ref[0, 0, r, :].astype(BF16) for r in rows]
            gcs = [_dot_exact(v["logf"], lower, left=True) for v in vs]
            grs = [_dot_exact(v["logf"], before_sub, left=True) for v in vs]
            a_s, _, _, _ = _hg_intra([v["q"] * jnp.exp(gc - gr) for v, gc, gr in zip(vs, gcs, grs)],
                                     [v["k"] for v in vs], gcs, grs, incl)
            gls = [gc[c - 1:c, :] for gc in gcs]
            adds = [_dot(inp, (v["k"] * jnp.exp(gl - gc)).astype(BF16), TN)
                    for inp, v, gl, gc in zip(inps, vs, gls, gcs)]
            o_intra = [_dot(a.astype(BF16), inp) for a, inp in zip(a_s, inps)]
            sts = []
            for gl, add in zip(gls, adds):
                sts.append(st)
                st = st * jnp.exp(gl) + add
            outs = [oi + _dot((v["q"] * jnp.exp(gc)).astype(BF16), s0.astype(BF16), NT)
                    for oi, v, gc, s0 in zip(o_intra, vs, gcs, sts)]
            for n, r, s0, o in zip(ns, rows, sts, outs):
                st_ref[0, 0, n] = s0
                oraw_ref[0, r, :] = o
                o_ref[0, r, :] = o * _rms_scale(o) * og_ref[...]
            return st

        lax.fori_loop(0, nchunk // unroll, group, jnp.zeros((HEAD_DIM, HEAD_DIM), F32))

    seq = jax.ShapeDtypeStruct((b, s, D_MODEL), F32)
    return _pcall(
        body, name="hgrn2_fwd", grid=(b, N_HEADS),
        in_specs=[_head_spec(s, 0), _head_spec(s, 1), _head_spec(s, 2),
                  pl.BlockSpec((2, 1, 1, HEAD_DIM), lambda b, h: (0, h, 0, 0)), _GAIN_SPEC],
        out_specs=[_seq_spec(s), _seq_spec(s), _state_spec(nchunk)],
        out_shape=[seq, seq, jax.ShapeDtypeStruct((b, N_HEADS, nchunk, HEAD_DIM, HEAD_DIM), F32)],
        compiler_params=_params("parallel", "parallel"),
    )(proj4, proj4, proj4, lbl4, o_gain)


def outproj_bwd(dh, w_out, o2, proj, name):
    t = dh.shape[0]
    tm = _row_tile(t, 256)

    def body(dh_ref, w_ref, o_ref, gate_ref, do_ref, dproj_ref, dw_ref):
        dhb = dh_ref[...].astype(BF16)
        dog = _dot(dhb, w_ref[...], NT)
        g = gate_ref[0]
        sg = _sigmoid(g)
        silu = g * sg
        o = o_ref[...]
        do_ref[...] = dog * silu
        dproj_ref[0] = dog * o * (sg * (1.0 + g * (1.0 - sg)))
        part = _dot((o * silu).astype(BF16), dhb, TN)

        @pl.when(pl.program_id(0) == 0)
        def _():
            dw_ref[...] = part

        @pl.when(pl.program_id(0) > 0)
        def _():
            dw_ref[...] += part

    row = pl.BlockSpec((tm, D_MODEL), lambda i: (i, 0))
    full = pl.BlockSpec((D_MODEL, D_MODEL), lambda i: (0, 0))
    return _pcall(
        body, name=name, grid=(t // tm,),
        in_specs=[row, full, row, pl.BlockSpec((1, tm, D_MODEL), lambda i: (3, i, 0))],
        out_specs=[row, pl.BlockSpec((1, tm, D_MODEL), lambda i: (3, i, 0)), full],
        out_shape=[jax.ShapeDtypeStruct((t, D_MODEL), F32),
                   jax.ShapeDtypeStruct((4, t, D_MODEL), F32),
                   jax.ShapeDtypeStruct((D_MODEL, D_MODEL), F32)],
        compiler_params=_params("arbitrary"),
    )(dh, w_out, o2, proj)


def inproj_bwd_dx(dproj, wg, x2, gain, dres, name, rider=None):
    t = x2.shape[0]
    tm = _row_tile(t, 256)

    def body(d_ref, w_ref, x_ref, g_ref, r_ref, dx_ref, dg_ref):
        du = jnp.zeros((tm, D_MODEL), F32)
        for p in range(N_DEV):
            cols = slice((p % 2) * W_COLS, (p % 2 + 1) * W_COLS)
            du = du + _dot(d_ref[p // 2, :, cols].astype(BF16), w_ref[p], NT)
        x = x_ref[...]
        r = _rms_scale(x)
        xh = x * r
        a = du * g_ref[...]
        dx_ref[...] = r_ref[...] + r * (a - xh * jnp.mean(a * xh, axis=-1, keepdims=True))
        part = jnp.sum(du * xh, axis=0, keepdims=True)

        @pl.when(pl.program_id(0) == 0)
        def _():
            dg_ref[...] = part

        @pl.when(pl.program_id(0) > 0)
        def _():
            dg_ref[...] += part

    row = pl.BlockSpec((tm, D_MODEL), lambda i: (i, 0))
    vec = pl.BlockSpec((1, D_MODEL), lambda i: (0, 0))
    return _pcall_riding(
        body, rider, (dproj, wg, x2, gain, dres), name=name, grid=(t // tm,),
        in_specs=[pl.BlockSpec((4, tm, D_MODEL), lambda i: (0, i, 0)),
                  pl.BlockSpec((N_DEV, D_MODEL, W_COLS), lambda i: (0, 0, 0)),
                  row, vec, row],
        out_specs=[row, vec],
        out_shape=[jax.ShapeDtypeStruct((t, D_MODEL), F32), jax.ShapeDtypeStruct((1, D_MODEL), F32)],
        semantics=("arbitrary",))


def inproj_bwd_dw(ut, dproj, name, out_dtype=F32):
    t = ut.shape[1]

    def body(ut_ref, d_ref, dw_ref):
        dw_ref[0] = _dot(ut_ref[...], d_ref[0].astype(BF16)).astype(dw_ref.dtype)

    return _pcall(
        body, name=name, grid=(N_DEV,),
        in_specs=[pl.BlockSpec((D_MODEL, t), lambda j: (0, 0)),
                  pl.BlockSpec((1, t, W_COLS), lambda j: (j // 2, 0, j % 2))],
        out_specs=pl.BlockSpec((1, D_MODEL, W_COLS), lambda j: (j, 0, 0)),
        out_shape=jax.ShapeDtypeStruct((N_DEV, D_MODEL, W_COLS), out_dtype),
        compiler_params=_params("parallel"),
    )(ut, dproj)


def _rms_bwd(x, gain, dy):
    r = _rms_scale(x)
    xh = x * r
    a = dy * gain
    return r * (a - xh * jnp.mean(a * xh, axis=-1, keepdims=True)), dy * xh


def sb_attn_bwd(qkv4, proj4, do3, o3, q_gain, k_gain, dproj4, rider=None):
    _, b, s, _ = proj4.shape
    blk = ATTN_BLOCK
    nq = s // blk
    scale = HEAD_DIM ** -0.5

    def body(qn_ref, kn_ref, v_ref, q_ref, k_ref, do_ref, o_ref, qg_ref, kg_ref, _alias, d_ref, dqg_ref, dkg_ref, dob):
        for h in range(SB_HEADS_PER_STEP):
            dob[h] = do_ref[0, :, h * HEAD_DIM:(h + 1) * HEAD_DIM].astype(BF16)
        d_ref[...] = jnp.zeros_like(d_ref)
        row = lax.broadcasted_iota(jnp.int32, (blk, blk), 0)
        col = lax.broadcasted_iota(jnp.int32, (blk, blk), 1)
        tri_lt = col < row
        suffix = (row > col).astype(BF16)
        suffix_incl = (row >= col).astype(BF16)

        def q_group(m, _):
            chains = _sb_chains(m)
            qis, dois, deltas = [], [], []
            for h, i in chains:
                rows_i = pl.ds(pl.multiple_of(i * blk, blk), blk)
                qis.append(_sb_head(qn_ref, h, rows_i))
                dois.append(dob[h, rows_i, :])
                deltas.append(jnp.sum(dois[-1].astype(F32) * o_ref[0, rows_i, h * HEAD_DIM:(h + 1) * HEAD_DIM],
                                      axis=1, keepdims=True))

            def walk(top, watch, state):
                qs = [qi[:top, :] for qi in qis]
                dos = [doi[:top, :] for doi in dois]
                dels = [delta[:top, :] for delta in deltas]
                mask = tri_lt[:top, :]

                def k_step(state):
                    done, cs, cgs, dqs = state
                    where = [_sb_key_rows(i, done) for _, i in chains]
                    kjs = [_sb_head(kn_ref, h, rows) for (h, _), (rows, _) in zip(chains, where)]
                    zs = [_sb_logits(q, kj) for q, kj in zip(qs, kjs)]
                    dws = [_dot(do, _sb_head(v_ref, h, rows), NT) for (h, _), (rows, _), do in zip(chains, where, dos)]
                    scored = [_sb_scores(z, done == 0, live, mask) for z, (_, live) in zip(zs, where)]
                    afters = [_dot_exact(log_skip, suffix) for log_skip, _, _ in scored]
                    wbs = [jnp.where(valid, jnp.exp(log_beta + after + c), 0.0).astype(BF16)
                           for (_, log_beta, valid), after, c in zip(scored, afters, cs)]
                    gs = [dw * wb.astype(F32) for dw, wb in zip(dws, wbs)]
                    befores = [delta - (_dot_exact(g, suffix_incl) + cg) for g, delta, cg in zip(gs, dels, cgs)]
                    dzbs = [jnp.where(valid, g - jnp.exp(log_beta) * (g + before), 0.0).astype(BF16)
                            for (_, log_beta, valid), g, before in zip(scored, gs, befores)]
                    new_dqs = [dq + _dot(dzb, kj) for dq, dzb, kj in zip(dqs, dzbs, kjs)]
                    for (h, _), (rows, _), wb, do, dzb, q in zip(chains, where, wbs, dos, dzbs, qs):
                        cols = slice(h * HEAD_DIM, (h + 1) * HEAD_DIM)
                        d_ref[2, 0, rows, cols] += _dot(wb, do, TN)
                        d_ref[1, 0, rows, cols] += _dot(dzb, q, TN)
                    new_cs = [c + jnp.sum(log_skip, axis=1, keepdims=True) for (log_skip, _, _), c in zip(scored, cs)]
                    new_cgs = [cg + jnp.sum(g, axis=1, keepdims=True) for g, cg in zip(gs, cgs)]
                    return done + 1, tuple(new_cs), tuple(new_cgs), tuple(new_dqs)

                return lax.while_loop(functools.partial(_sb_keys_left, chains, watch), k_step, state)

            n = len(chains)
            zero = (jnp.zeros((blk, 1), F32),) * n
            done, cs, cgs, dqs = walk(blk, slice(SB_TOP_ROWS, blk),
                                      (jnp.int32(0), zero, zero, (jnp.zeros((blk, HEAD_DIM), F32),) * n))
            top = slice(0, SB_TOP_ROWS)
            _, _, _, tops = walk(SB_TOP_ROWS, top, (done, tuple(c[top] for c in cs), tuple(cg[top] for cg in cgs),
                                                    tuple(dq[top] for dq in dqs)))
            for (h, i), dq, dq_top in zip(chains, dqs, tops):
                d_ref[0, 0, pl.ds(pl.multiple_of(i * blk, blk), blk), h * HEAD_DIM:(h + 1) * HEAD_DIM] = (
                    jnp.concatenate([dq_top, dq[SB_TOP_ROWS:]], axis=0) * scale)
            return 0

        lax.fori_loop(0, nq // SB_QBLOCKS_PER_STEP, q_group, 0)

        def norm_block(i, carry):
            rows = pl.ds(pl.multiple_of(i * blk, blk), blk)
            out = []
            for h in range(SB_HEADS_PER_STEP):
                cols = slice(h * HEAD_DIM, (h + 1) * HEAD_DIM)
                for part, src_ref, gain_ref in ((0, q_ref, qg_ref), (1, k_ref, kg_ref)):
                    dy = d_ref[part, 0, rows, cols] * (scale if part == 1 else 1.0)
                    dx, pg = _rms_bwd(src_ref[0, 0, rows, cols], gain_ref[...], dy)
                    d_ref[part, 0, rows, cols] = dx
                    out.append(carry[len(out)] + jnp.sum(pg, axis=0, keepdims=True))
            return tuple(out)

        sums = lax.fori_loop(0, nq, norm_block, (jnp.zeros((1, HEAD_DIM), F32),) * (2 * SB_HEADS_PER_STEP))
        for h in range(SB_HEADS_PER_STEP):
            dqg_ref[0, h] = sums[2 * h]
            dkg_ref[0, h] = sums[2 * h + 1]

    head_row = jax.ShapeDtypeStruct((b, N_HEADS, 1, HEAD_DIM), F32)
    return _pcall_riding(
        body, rider, (qkv4, qkv4, qkv4, proj4, proj4, do3, o3, q_gain, k_gain, dproj4),
        name="sb_attn_bwd", grid=(b, N_HEADS // SB_HEADS_PER_STEP),
        in_specs=[_sb_group_spec(s, 0), _sb_group_spec(s, 1), _sb_group_spec(s, 2),
                  _sb_group_spec(s, 0), _sb_group_spec(s, 1),
                  _sb_seq_group_spec(s), _sb_seq_group_spec(s), _GAIN_SPEC, _GAIN_SPEC,
                  pl.BlockSpec(memory_space=pl.ANY)],
        out_specs=[pl.BlockSpec((3, 1, s, SB_GROUP_COLS), lambda b, g: (0, b, 0, g)),
                   _SB_GROUP_ROW_SPEC, _SB_GROUP_ROW_SPEC],
        out_shape=[jax.ShapeDtypeStruct(dproj4.shape, F32), head_row, head_row],
        scratch_shapes=[pltpu.VMEM((SB_HEADS_PER_STEP, s, HEAD_DIM), BF16)],
        input_output_aliases={9: 0}, semantics=("parallel", "parallel"))


def hgrn2_bwd(proj4, don3, oraw3, states, lbl4, o_gain, dproj4, rider=None):
    _, b, s, _ = proj4.shape
    nchunk = s // HG_CHUNK
    c = HG_CHUNK
    subs = range(HG_CHUNK // HG_SUB)
    unroll = math.gcd(nchunk, HG_UNROLL_BWD)
    ngroup = nchunk // unroll

    def body(q_ref, f_ref, i_ref, don_ref, oraw_ref, st_ref, lbl_ref, og_ref, _alias, d_ref, dog_ref, dlb_ref):
        incl, lower, before_sub, upper = _hg_masks()
        lb, oml = _hg_lower_bound(lbl_ref)
        last_row = lax.broadcasted_iota(jnp.int32, (c, HEAD_DIM), 0) == c - 1

        def group(m, carry):
            dst, dog_acc, dlb_acc = carry
            ns, rows = _hg_group_rows(ngroup - 1 - m, unroll)
            ns, rows = ns[::-1], rows[::-1]
            vs = [_hg_gates(q_ref[0, 0, r, :], f_ref[0, 0, r, :], lb, oml) for r in rows]
            inps = [i_ref[0, 0, r, :].astype(BF16) for r in rows]
            sts = [st_ref[0, 0, n] for n in ns]
            gcs = [_dot_exact(v["logf"], lower, left=True) for v in vs]
            grs = [_dot_exact(v["logf"], before_sub, left=True) for v in vs]
            e_qs = [jnp.exp(gc - gr) for gc, gr in zip(gcs, grs)]
            a_s, qdbs, ksbs, ess = _hg_intra([v["q"] * e for v, e in zip(vs, e_qs)], [v["k"] for v in vs],
                                             gcs, grs, incl)
            e_gcs = [jnp.exp(gc) for gc in gcs]
            gls = [gc[c - 1:c, :] for gc in gcs]
            e_gls = [jnp.exp(gl) for gl in gls]
            e_ks = [jnp.exp(gl - gc) for gl, gc in zip(gls, gcs)]
            normed = [_rms_bwd(oraw_ref[0, r, :], og_ref[...], don_ref[0, r, :]) for r in rows]
            dobs = [do.astype(BF16) for do, _ in normed]
            dabs = [jnp.where(incl, _dot(dob, inp, NT), 0.0).astype(BF16) for dob, inp in zip(dobs, inps)]
            adds = [_dot(dob, (v["q"] * e).astype(BF16), TN) for dob, v, e in zip(dobs, vs, e_gcs)]
            dq_inters = [_dot(dob, st.astype(BF16)) * e for dob, st, e in zip(dobs, sts, e_gcs)]
            dqds = [jnp.concatenate([_dot(dab[sub * HG_SUB:(sub + 1) * HG_SUB, :], ksb[sub]) for sub in subs], axis=0)
                    for dab, ksb in zip(dabs, ksbs)]
            dkss = [[_dot(dab[sub * HG_SUB:(sub + 1) * HG_SUB, :], qdb[sub * HG_SUB:(sub + 1) * HG_SUB, :], TN)
                     for sub in subs] for dab, qdb in zip(dabs, qdbs)]
            dsts = []
            for e_gl, add in zip(e_gls, adds):
                dsts.append(dst)
                dst = dst * e_gl + add
            dstbs = [d.astype(BF16) for d in dsts]
            dis = [_dot(a.astype(BF16), dob, TN) + _dot((v["k"] * e_k).astype(BF16), dstb, NT)
                   for a, dob, v, e_k, dstb in zip(a_s, dobs, vs, e_ks, dstbs)]
            dk_inters = [_dot(inp, dstb) * e_k for inp, dstb, e_k in zip(inps, dstbs, e_ks)]
            for u, r in enumerate(rows):
                v, q, k = vs[u], vs[u]["q"], vs[u]["k"]
                dk, dgc_k = dk_inters[u], jnp.zeros((c, HEAD_DIM), F32)
                for sub in subs:
                    dk = dk + dkss[u][sub] * ess[u][sub]
                    dgc_k = dgc_k + dkss[u][sub] * ksbs[u][sub].astype(F32)
                dq = dqds[u] * e_qs[u] + dq_inters[u]
                at_last = (jnp.sum(k * dk_inters[u], axis=0, keepdims=True)
                           + e_gls[u] * jnp.sum(sts[u] * dsts[u], axis=0, keepdims=True))
                dgc = ((qdbs[u].astype(F32) * dqds[u] - dgc_k) + (q * dq_inters[u] - k * dk_inters[u])
                       + jnp.where(last_row, at_last, 0.0))
                dlf_f = _dot_exact(dgc, upper, left=True) / v["f"]
                d_ref[0, 0, r, :] = dq * (v["sq"] * (1.0 + v["qp"] * (1.0 - v["sq"])))
                d_ref[1, 0, r, :] = (dlf_f - dk) * (oml * v["sf"] * v["sfn"])
                d_ref[2, 0, r, :] = dis[u]
                dlb_acc = dlb_acc + jnp.sum((dlf_f - dk) * v["sfn"], axis=0, keepdims=True)
                dog_acc = dog_acc + jnp.sum(normed[u][1], axis=0, keepdims=True)
            return dst, dog_acc, dlb_acc

        zero = jnp.zeros((1, HEAD_DIM), F32)
        _, dog, dlb = lax.fori_loop(0, ngroup, group, (jnp.zeros((HEAD_DIM, HEAD_DIM), F32), zero, zero))
        dog_ref[0, 0] = dog
        dlb_ref[0, 0] = dlb

    head_row = jax.ShapeDtypeStruct((b, N_HEADS, 1, HEAD_DIM), F32)
    return _pcall_riding(
        body, rider, (proj4, proj4, proj4, don3, oraw3, states, lbl4, o_gain, dproj4),
        name="hgrn2_bwd", grid=(b, N_HEADS),
        in_specs=[_head_spec(s, 0), _head_spec(s, 1), _head_spec(s, 2), _seq_spec(s), _seq_spec(s),
                  _state_spec(nchunk), pl.BlockSpec((2, 1, 1, HEAD_DIM), lambda b, h: (0, h, 0, 0)), _GAIN_SPEC,
                  pl.BlockSpec(memory_space=pl.ANY)],
        out_specs=[pl.BlockSpec((3, 1, s, HEAD_DIM), lambda b, h: (0, b, 0, h)), _HEAD_ROW_SPEC, _HEAD_ROW_SPEC],
        out_shape=[jax.ShapeDtypeStruct(dproj4.shape, F32), head_row, head_row],
        input_output_aliases={8: 0}, semantics=("parallel", "parallel"))


def local_step(x, target, sb_norm, wsi, sb_q_gain, sb_k_gain, hg_o_gain, hg_lb_logits, wso_mine, whi_mine, who_mine,
               hg_norm_mine):
    b, s, _ = x.shape
    t = b * s
    x2 = x.reshape(t, D_MODEL)
    tg2 = target.reshape(t, D_MODEL)
    lbl4 = hg_lb_logits.reshape(2, N_HEADS, 1, HEAD_DIM)
    four = (4, b, s, D_MODEL)
    three = (b, s, D_MODEL)
    rows8 = (N_DEV, W_ROWS, D_MODEL)

    (proj0, u0, qkv0), (wso,) = rms_inproj(x2, sb_norm, wsi, "sb_inproj", _Rider([wso_mine], scatter=False),
                                           qk_gains=(sb_q_gain, sb_k_gain))
    qkv0 = qkv0.reshape(3, b, s, D_MODEL)
    wso = wso.reshape(D_MODEL, D_MODEL)
    o0, (whi, who, hgn) = sb_attn_fwd(qkv0, _Rider([whi_mine, who_mine, hg_norm_mine], scatter=False))
    who = who.reshape(D_MODEL, D_MODEL)
    hg_norm_full = hgn[:, 0, :].reshape(1, D_MODEL)
    o0 = o0.reshape(t, D_MODEL)
    h1 = gate_outproj(o0, proj0, wso, x2, None, "sb_outproj")
    (proj1, u1), _ = rms_inproj(h1, hg_norm_full, whi, "hg_inproj")
    o1, o1_raw, states = hgrn2_fwd(proj1.reshape(four), lbl4, hg_o_gain)
    o1 = o1.reshape(t, D_MODEL)
    dh2, loss_parts = gate_outproj(o1, proj1, who, h1, tg2, "hg_outproj_loss")

    do1, dproj1, g_who = outproj_bwd(dh2, who, o1, proj1, "hg_outproj_bwd")
    (dproj1, g_og, g_lb), (p_who,) = hgrn2_bwd(proj1.reshape(four), do1.reshape(three), o1_raw, states, lbl4,
                                               hg_o_gain, dproj1.reshape(four),
                                               _Rider([g_who.reshape(rows8)], scatter=True))
    dproj1 = dproj1.reshape(4, t, D_MODEL)
    (dh1, g_hgn), _ = inproj_bwd_dx(dproj1, whi, h1, hg_norm_full, dh2, "hg_inproj_bwd_dx")
    g_whi = inproj_bwd_dw(u1, dproj1, "hg_inproj_bwd_dw")

    do0, dproj0, g_wso = outproj_bwd(dh1, wso, o0, proj0, "sb_outproj_bwd")
    (dproj0, g_qg, g_kg), (p_whi, p_wso) = sb_attn_bwd(qkv0, proj0.reshape(four), do0.reshape(three), o0.reshape(three),
                                                       sb_q_gain, sb_k_gain, dproj0.reshape(four),
                                                       _Rider([g_whi, g_wso.reshape(rows8)], scatter=True))
    dproj0 = dproj0.reshape(4, t, D_MODEL)
    g_wsi = inproj_bwd_dw(u0, dproj0, "sb_inproj_bwd_dw", out_dtype=BF16)
    (gx, g_sbn), (p_wsi,) = inproj_bwd_dx(dproj0, wsi, x2, sb_norm, dh1, "sb_inproj_bwd_dx",
                                          _Rider([g_wsi], scatter=True))
    return dict(loss_parts=loss_parts, gx=gx.reshape(three), p_wsi=p_wsi, p_wso=p_wso, p_whi=p_whi, p_who=p_who,
                g_sbn=g_sbn, g_hgn=g_hgn, g_qg=g_qg, g_kg=g_kg, g_og=g_og, g_lb=g_lb)


def _two_level_gather(src, out, send_sems, recv_sems, local_sem, pos):
    x, y, c = pos
    me, sibling = (x, y, c), (x, y, 1 - c)
    chips = [(1 - x, y), (x, 1 - y), (1 - x, 1 - y)]

    def copy(k, block, to, source=None):
        slot = out.at[_linear(block)]
        return pltpu.make_async_remote_copy(
            src_ref=slot if source is None else source, dst_ref=slot, send_sem=send_sems.at[k],
            recv_sem=recv_sems.at[k], device_id=to, device_id_type=MESH)

    mine = pltpu.make_async_copy(src, out.at[_linear(me)], local_sem)
    mine.start()
    first = [copy(0, me, sibling, src)] + [copy(1 + j, me, (*chip, c), src) for j, chip in enumerate(chips)]
    for cp in first:
        cp.start()
    passed = [copy(4 + j, (*chip, c), sibling) for j, chip in enumerate(chips)]
    for j, chip in enumerate(chips):
        copy(1 + j, (*chip, c), me).wait_recv()
        passed[j].start()
    copy(0, sibling, me).wait_recv()
    for j, chip in enumerate(chips):
        copy(4 + j, (*chip, 1 - c), me).wait_recv()
    for cp in first + passed:
        cp.wait_send()
    mine.wait()


def gather_first_weights(w_si, w_so, w_hi, w_ho, hg_norm):
    def body(si_ref, so_ref, hi_ref, ho_ref, hn_ref, o_si, so_b, hi_b, ho_b, hn_b, si_b, send_sems, recv_sems, local_sem):
        for src, buf in ((si_ref, si_b), (so_ref, so_b), (hi_ref, hi_b), (ho_ref, ho_b)):
            buf[...] = src[...].astype(BF16)
        hn_b[...] = jnp.broadcast_to(hn_ref[...], hn_b.shape)
        _two_level_gather(si_b, o_si, send_sems, recv_sems, local_sem, _mesh_pos())

    return _pcall(
        body, name="gather_first_weights",
        in_specs=[_VMEM] * 5, out_specs=[_ANY] + [_VMEM] * 4,
        out_shape=[jax.ShapeDtypeStruct((N_DEV,) + w_si.shape, BF16), jax.ShapeDtypeStruct(w_so.shape, BF16),
                   jax.ShapeDtypeStruct(w_hi.shape, BF16), jax.ShapeDtypeStruct(w_ho.shape, BF16),
                   jax.ShapeDtypeStruct((8, HEAD_DIM), F32)],
        scratch_shapes=[pltpu.VMEM(w_si.shape, BF16), pltpu.SemaphoreType.DMA((N_PEERS,)),
                        pltpu.SemaphoreType.DMA((N_PEERS,)), pltpu.SemaphoreType.DMA],
        compiler_params=pltpu.CompilerParams(vmem_limit_bytes=VMEM_LIMIT_BYTES),
    )(w_si, w_so, w_hi, w_ho, hg_norm)


def _adamw(w, g, m, v):
    m = ADAM_B1 * m + (1.0 - ADAM_B1) * g
    v = ADAM_B2 * v + (1.0 - ADAM_B2) * (g * g)
    m_hat = m / (1.0 - ADAM_B1 ** ADAM_STEP)
    v_hat = v / (1.0 - ADAM_B2 ** ADAM_STEP)
    delta = -ADAM_LR * (m_hat / (jnp.sqrt(v_hat) + ADAM_EPS) + ADAM_WD * w)
    return delta, m, v


def reduce_adamw(parts, w, m, v, name):
    _, r, c = parts.shape
    tr = _row_tile(r, 256)

    def body(p_ref, w_ref, m_ref, v_ref, g_ref, d_ref, m2_ref, v2_ref):
        g = p_ref[0].astype(F32)
        for dev in range(1, N_DEV):
            g = g + p_ref[dev].astype(F32)
        g_ref[...] = g
        d_ref[...], m2_ref[...], v2_ref[...] = _adamw(w_ref[...], g, m_ref[...], v_ref[...])

    tile = pl.BlockSpec((tr, c), lambda i: (i, 0))
    return _pcall(
        body, name=name, grid=(r // tr,),
        in_specs=[pl.BlockSpec((N_DEV, tr, c), lambda i: (0, i, 0)), tile, tile, tile],
        out_specs=[tile] * 4, out_shape=[jax.ShapeDtypeStruct((r, c), F32)] * 4,
        compiler_params=_params("parallel"),
    )(parts, w, m, v)


PACK_ROWS = 32
ROW_SBN, ROW_HGN, ROW_LB, ROW_QG, ROW_KG, ROW_OG, ROW_LOSS = 0, 8, 16, 24, 25, 26, 27


def small_update(g_sbn, g_hgn, g_lb, g_qg, g_kg, g_og, loss_parts, small):
    n_in = 7 + len(small)

    def body(*refs):
        sbn_ref, hgn_ref, lb_ref, qg_ref, kg_ref, og_ref, loss_ref = refs[:7]
        wmv = refs[7:n_in]
        outs = refs[n_in:n_in + 25]
        pack, gath, tot, send_sems, recv_sems, local_sems = refs[n_in + 25:]
        pos = _mesh_pos()
        me = _linear(pos)
        pack[...] = jnp.zeros_like(pack)
        pack[ROW_SBN:ROW_SBN + 8, :] = sbn_ref[...]
        pack[ROW_HGN:ROW_HGN + 8, :] = hgn_ref[...]
        pack[ROW_LB:ROW_LB + 8, :] = jnp.sum(lb_ref[...], axis=0)
        pack[ROW_QG:ROW_QG + 1, :] = jnp.sum(qg_ref[...], axis=0, keepdims=True)
        pack[ROW_KG:ROW_KG + 1, :] = jnp.sum(kg_ref[...], axis=0, keepdims=True)
        pack[ROW_OG:ROW_OG + 1, :] = jnp.sum(og_ref[...], axis=0, keepdims=True)
        pack[ROW_LOSS:ROW_LOSS + 1, :] = jnp.sum(loss_ref[...], axis=0)[0:1, :]
        _exchange_start([((lambda p: pack), gath)], send_sems, recv_sems, local_sems, pos)
        _exchange_wait([((lambda p: pack), gath)], send_sems, recv_sems, local_sems, pos)
        total = gath[0]
        for dev in range(1, N_DEV):
            total = total + gath[dev]
        tot[...] = total
        outs[0][...] = jnp.broadcast_to(tot[ROW_LOSS:ROW_LOSS + 1, :], (8, HEAD_DIM))
        l0 = wmv[15][0:8, :]
        l1 = wmv[15][8:16, :]
        p1, p0 = _sigmoid_pair(l1 - l0)
        d_l1 = p0 * p1 * tot[ROW_LB:ROW_LB + 8, :]
        grads = [tot[ROW_SBN:ROW_SBN + 8, :], tot[ROW_QG:ROW_QG + 1, :], tot[ROW_KG:ROW_KG + 1, :],
                 tot[pl.ds(ROW_HGN + me, 1), :], tot[ROW_OG:ROW_OG + 1, :],
                 jnp.concatenate([-d_l1, d_l1], axis=0)]
        for i, g in enumerate(grads):
            w_ref, m_ref, v_ref = wmv[3 * i:3 * i + 3]
            o = outs[1 + 4 * i:5 + 4 * i]
            o[0][...] = g
            o[1][...], o[2][...], o[3][...] = _adamw(w_ref[...], g, m_ref[...], v_ref[...])

    out_shape = [jax.ShapeDtypeStruct((8, HEAD_DIM), F32)]
    for i in range(6):
        out_shape += [jax.ShapeDtypeStruct(small[3 * i].shape, F32)] * 4
    return _pcall(
        body, name="small_update",
        in_specs=[_VMEM] * n_in, out_specs=[_VMEM] * 25, out_shape=out_shape,
        scratch_shapes=[pltpu.VMEM((PACK_ROWS, HEAD_DIM), F32), pltpu.VMEM((N_DEV, PACK_ROWS, HEAD_DIM), F32),
                        pltpu.VMEM((PACK_ROWS, HEAD_DIM), F32)] + _exchange_sems(1),
    )(g_sbn, g_hgn, g_lb, g_qg, g_kg, g_og, loss_parts, *small)


def kernel(x, sb_norm, sb_w_in, sb_q_gain, sb_k_gain, sb_w_out, hg_norm, hg_w_in, hg_o_gain, hg_w_out, hg_lb_logits, loss_target, m_sb_norm, m_sb_w_in, m_sb_q_gain, m_sb_k_gain, m_sb_w_out, m_hg_norm, m_hg_w_in, m_hg_o_gain, m_hg_w_out, m_hg_lb_logits, v_sb_norm, v_sb_w_in, v_sb_q_gain, v_sb_k_gain, v_sb_w_out, v_hg_norm, v_hg_w_in, v_hg_o_gain, v_hg_w_out, v_hg_lb_logits):
    b = x.shape[0]
    wsi, wso_mine, whi_mine, who_mine, hg_norm_mine = gather_first_weights(
        sb_w_in[0], sb_w_out[0], hg_w_in[0], hg_w_out[0], hg_norm)
    r = local_step(x, loss_target, sb_norm, wsi, sb_q_gain, sb_k_gain, hg_o_gain, hg_lb_logits,
                   wso_mine, whi_mine, who_mine, hg_norm_mine)
    big = {}
    for name, p, w, m, v in (("sb_w_in", r["p_wsi"], sb_w_in, m_sb_w_in, v_sb_w_in),
                             ("sb_w_out", r["p_wso"], sb_w_out, m_sb_w_out, v_sb_w_out),
                             ("hg_w_in", r["p_whi"], hg_w_in, m_hg_w_in, v_hg_w_in),
                             ("hg_w_out", r["p_who"], hg_w_out, m_hg_w_out, v_hg_w_out)):
        big[name] = [o[None] for o in reduce_adamw(p, w[0], m[0], v[0], "adamw_" + name)]

    def rows8(a):
        return a.reshape(8, HEAD_DIM)

    def rows16(a):
        return a.reshape(16, HEAD_DIM)

    small_in = [rows8(sb_norm), rows8(m_sb_norm), rows8(v_sb_norm),
                sb_q_gain, m_sb_q_gain, v_sb_q_gain,
                sb_k_gain, m_sb_k_gain, v_sb_k_gain,
                hg_norm, m_hg_norm, v_hg_norm,
                hg_o_gain, m_hg_o_gain, v_hg_o_gain,
                rows16(hg_lb_logits), rows16(m_hg_lb_logits), rows16(v_hg_lb_logits)]
    so = small_update(rows8(r["g_sbn"]), rows8(r["g_hgn"]), r["g_lb"].reshape(b, N_HEADS, HEAD_DIM),
                      r["g_qg"].reshape(b * N_HEADS, HEAD_DIM), r["g_kg"].reshape(b * N_HEADS, HEAD_DIM),
                      r["g_og"].reshape(b * N_HEADS, HEAD_DIM), r["loss_parts"], small_in)
    loss = so[0][0, 0]
    shapes = {"sb_norm": (1, D_MODEL), "sb_q_gain": (1, HEAD_DIM), "sb_k_gain": (1, HEAD_DIM),
              "hg_norm": (1, HEAD_DIM), "hg_o_gain": (1, HEAD_DIM), "hg_lb_logits": (2, D_MODEL)}
    small = {}
    for i, name in enumerate(("sb_norm", "sb_q_gain", "sb_k_gain", "hg_norm", "hg_o_gain", "hg_lb_logits")):
        small[name] = [o.reshape(shapes[name]) for o in so[1 + 4 * i:5 + 4 * i]]
    order = ("sb_norm", "sb_w_in", "sb_q_gain", "sb_k_gain", "sb_w_out",
             "hg_norm", "hg_w_in", "hg_o_gain", "hg_w_out", "hg_lb_logits")
    res = {**big, **small}
    return (loss, r["gx"]) + tuple(res[n][j] for j in range(4) for n in order)
```

```python
import functools
import math

import jax
import jax.numpy as jnp
from jax import lax
from jax.experimental import pallas as pl
from jax.experimental.pallas import tpu as pltpu

F32 = jnp.float32
BF16 = jnp.bfloat16

N_DEV = 8
D_MODEL = 1024
N_HEADS = 8
HEAD_DIM = 128
RMS_EPS = 1e-6
ATTN_BLOCK = 128
HG_CHUNK = 64
HG_SUB = 16
HG_UNROLL_FWD = 16
HG_UNROLL_BWD = 8
EXP_CLAMP = 80.0
SB_HEADS_PER_STEP = 2
SB_QBLOCKS_PER_STEP = 4
SB_GROUP_COLS = SB_HEADS_PER_STEP * 128
SB_TOP_ROWS = 32
SB_LOG_WEIGHT_FLOOR = -104.0
VMEM_LIMIT_BYTES = 48 * 1024 * 1024
W_COLS = 4 * D_MODEL // N_DEV
W_ROWS = D_MODEL // N_DEV

ADAM_LR = 0.001
ADAM_B1 = 0.9
ADAM_B2 = 0.999
ADAM_EPS = 1e-08
ADAM_WD = 0.01
ADAM_STEP = 10

NT = (((1,), (1,)), ((), ()))
TN = (((0,), (0,)), ((), ()))
NN = (((1,), (0,)), ((), ()))


def _pcall(body, *, name, **kw):
    return pl.pallas_call(body, name=name, **kw)


def _params(*sem):
    return pltpu.CompilerParams(dimension_semantics=sem, vmem_limit_bytes=VMEM_LIMIT_BYTES)


def _dot(a, b, dims=NN):
    return lax.dot_general(a, b, dims, preferred_element_type=F32)


def _dot_exact(a, m, dims=NN, left=False):
    hi = a.astype(BF16)
    lo = (a - hi.astype(F32)).astype(BF16)
    if left:
        return _dot(m, hi, dims) + _dot(m, lo, dims)
    return _dot(hi, m, dims) + _dot(lo, m, dims)


def _split(a):
    hi = a.astype(BF16)
    return hi, (a - hi.astype(F32)).astype(BF16)


def _dot3(a, b, dims=NN):
    return _dot(a[0], b[0], dims) + (_dot(a[0], b[1], dims) + _dot(a[1], b[0], dims))


def _sigmoid(x):
    return 1.0 / (1.0 + jnp.exp(-x))


def _sigmoid_pair(x):
    e = jnp.exp(-jnp.abs(x))
    big = 1.0 / (1.0 + e)
    small = e * big
    pos = x >= 0
    return jnp.where(pos, big, small), jnp.where(pos, small, big)


def _rms_scale(x):
    return lax.rsqrt(jnp.mean(x * x, axis=-1, keepdims=True) + RMS_EPS)


def _row_tile(t, want):
    return want if t % want == 0 else t


MESH = pl.DeviceIdType.MESH
N_PEERS = N_DEV - 1
_ANY = pl.BlockSpec(memory_space=pl.ANY)
_VMEM = pl.BlockSpec(memory_space=pltpu.VMEM)


def _mesh_pos():
    return lax.axis_index("x"), lax.axis_index("y"), lax.axis_index("c")


def _linear(pos):
    return 4 * pos[0] + 2 * pos[1] + pos[2]


def _peer(pos, k):
    flips = ((k + 1) >> 2 & 1, (k + 1) >> 1 & 1, (k + 1) & 1)
    return tuple(1 - p if f else p for p, f in zip(pos, flips))


def _exchange_copies(pairs, send_sems, recv_sems, local_sems, pos, landing):
    me = _linear(pos)
    local, remote = [], []
    for a, (src_of, dst) in enumerate(pairs):
        local.append(pltpu.make_async_copy(src_of(me), dst.at[me], local_sems.at[a]))
        for k in range(N_PEERS):
            peer = _peer(pos, k)
            remote.append(pltpu.make_async_remote_copy(
                src_ref=src_of(_linear(peer)), dst_ref=dst.at[_linear(peer) if landing else me],
                send_sem=send_sems.at[a, k], recv_sem=recv_sems.at[a, k], device_id=peer, device_id_type=MESH))
    return local, remote


def _exchange_start(pairs, send_sems, recv_sems, local_sems, pos):
    local, sent = _exchange_copies(pairs, send_sems, recv_sems, local_sems, pos, landing=False)
    for copy in local + sent:
        copy.start()


def _exchange_wait(pairs, send_sems, recv_sems, local_sems, pos):
    local, landed = _exchange_copies(pairs, send_sems, recv_sems, local_sems, pos, landing=True)
    for copy in landed:
        copy.wait_recv()
        copy.wait_send()
    for copy in local:
        copy.wait()


def _exchange_sems(n):
    return [pltpu.SemaphoreType.DMA((n, N_PEERS)), pltpu.SemaphoreType.DMA((n, N_PEERS)),
            pltpu.SemaphoreType.DMA((n,))]


class _Rider:
    def __init__(self, arrays, scatter):
        self.arrays = list(arrays)
        self.scatter = scatter
        self.out_shapes = [jax.ShapeDtypeStruct(a.shape if scatter else (N_DEV,) + a.shape, a.dtype)
                           for a in self.arrays]

    def pairs(self, in_refs, out_refs):
        if self.scatter:
            return [((lambda p, r=r: r.at[p]), o) for r, o in zip(in_refs, out_refs)]
        return [((lambda p, r=r: r), o) for r, o in zip(in_refs, out_refs)]


def _pcall_riding(body, rider, args, *, name, grid, in_specs, out_specs, out_shape, semantics, scratch_shapes=(),
                  input_output_aliases=None):
    aliases = input_output_aliases or {}
    if rider is None:
        outs = _pcall(body, name=name, grid=grid, in_specs=list(in_specs), out_specs=list(out_specs),
                      out_shape=list(out_shape), scratch_shapes=list(scratch_shapes), input_output_aliases=aliases,
                      compiler_params=_params(*semantics))(*args)
        return list(outs), []
    n_in, n_out, n_scr, n_r = len(in_specs), len(out_specs), len(scratch_shapes), len(rider.arrays)

    def riding(*refs):
        ins, refs = refs[:n_in], refs[n_in:]
        rider_in, refs = refs[:n_r], refs[n_r:]
        outs, refs = refs[:n_out], refs[n_out:]
        rider_out, refs = refs[:n_r], refs[n_r:]
        scratch, sems = refs[:n_scr], refs[n_scr:]
        pairs = rider.pairs(rider_in, rider_out)
        first = functools.reduce(jnp.logical_and, [pl.program_id(a) == 0 for a in range(len(grid))])
        last = functools.reduce(jnp.logical_and, [pl.program_id(a) == g - 1 for a, g in enumerate(grid)])

        @pl.when(first)
        def _():
            _exchange_start(pairs, *sems, _mesh_pos())

        body(*ins, *outs, *scratch)

        @pl.when(last)
        def _():
            _exchange_wait(pairs, *sems, _mesh_pos())

    outs = _pcall(riding, name=name, grid=grid, in_specs=list(in_specs) + [_ANY] * n_r,
                  out_specs=list(out_specs) + [_ANY] * n_r, out_shape=list(out_shape) + rider.out_shapes,
                  scratch_shapes=list(scratch_shapes) + _exchange_sems(n_r), input_output_aliases=aliases,
                  compiler_params=_params(*(("arbitrary",) * len(grid))))(*args, *rider.arrays)
    return list(outs[:n_out]), list(outs[n_out:])


def rms_inproj(x2, gain, wg, name, rider=None, qk_gains=None):
    t = x2.shape[0]
    tm = _row_tile(t, 256)
    with_qkv = qk_gains is not None

    def body(x_ref, g_ref, w_ref, *rest):
        if with_qkv:
            qg_ref, kg_ref, proj_ref, ut_ref, qkv_ref = rest
            head_gain = (qg_ref, kg_ref)
        else:
            proj_ref, ut_ref = rest
        x = x_ref[...]
        u = x * _rms_scale(x) * g_ref[...]
        ut_ref[...] = u.T.astype(BF16)
        u = u.astype(BF16)
        for p in range(N_DEV):
            part, lo = p // 2, (p % 2) * W_COLS
            res = _dot(u, w_ref[p])
            proj_ref[part, :, lo:lo + W_COLS] = res
            if with_qkv and part < 3:
                for h in range(W_COLS // HEAD_DIM):
                    y = res[:, h * HEAD_DIM:(h + 1) * HEAD_DIM]
                    if part < 2:
                        y = y * _rms_scale(y) * head_gain[part][...]
                    qkv_ref[part, :, lo + h * HEAD_DIM:lo + (h + 1) * HEAD_DIM] = y.astype(BF16)

    vec = pl.BlockSpec((1, D_MODEL), lambda i: (0, 0))
    in_specs = [pl.BlockSpec((tm, D_MODEL), lambda i: (i, 0)), vec,
                pl.BlockSpec((N_DEV, D_MODEL, W_COLS), lambda i: (0, 0, 0))]
    out_specs = [pl.BlockSpec((4, tm, D_MODEL), lambda i: (0, i, 0)), pl.BlockSpec((D_MODEL, tm), lambda i: (0, i))]
    out_shape = [jax.ShapeDtypeStruct((4, t, D_MODEL), F32), jax.ShapeDtypeStruct((D_MODEL, t), BF16)]
    args = (x2, gain, wg)
    if with_qkv:
        in_specs += [pl.BlockSpec((1, HEAD_DIM), lambda i: (0, 0))] * 2
        out_specs.append(pl.BlockSpec((3, tm, D_MODEL), lambda i: (0, i, 0)))
        out_shape.append(jax.ShapeDtypeStruct((3, t, D_MODEL), BF16))
        args += tuple(qk_gains)
    return _pcall_riding(body, rider, args, name=name, grid=(t // tm,), in_specs=in_specs, out_specs=out_specs,
                         out_shape=out_shape, semantics=("parallel",))


def gate_outproj(o2, proj, w_out, resid, target, name):
    t = o2.shape[0]
    tm = _row_tile(t, 256)
    with_loss = target is not None

    def body(o_ref, gate_ref, w_ref, r_ref, *rest):
        g = gate_ref[0]
        og = (o_ref[...] * (g * _sigmoid(g))).astype(BF16)
        h = r_ref[...] + _dot(og, w_ref[...])
        if with_loss:
            t_ref, dh_ref, loss_ref = rest
            err = h - t_ref[...]
            dh_ref[...] = err * (1.0 / D_MODEL)
            part = 0.5 * jnp.sum(jnp.mean(err * err, axis=-1, keepdims=True))
            loss_ref[...] = jnp.full(loss_ref.shape, part, F32)
        else:
            (h_ref,) = rest
            h_ref[...] = h

    row = pl.BlockSpec((tm, D_MODEL), lambda i: (i, 0))
    in_specs = [row,
                pl.BlockSpec((1, tm, D_MODEL), lambda i: (3, i, 0)),
                pl.BlockSpec((D_MODEL, D_MODEL), lambda i: (0, 0)),
                row]
    args = [o2, proj, w_out, resid]
    if with_loss:
        in_specs.append(row)
        args.append(target)
        out_specs = [row, pl.BlockSpec((1, 8, 128), lambda i: (i, 0, 0))]
        out_shape = [jax.ShapeDtypeStruct((t, D_MODEL), F32),
                     jax.ShapeDtypeStruct((t // tm, 8, 128), F32)]
    else:
        out_specs = row
        out_shape = jax.ShapeDtypeStruct((t, D_MODEL), F32)
    return _pcall(body, name=name, grid=(t // tm,), in_specs=in_specs, out_specs=out_specs,
                  out_shape=out_shape, compiler_params=_params("parallel"))(*args)


def _head_spec(s, part):
    return pl.BlockSpec((1, 1, s, HEAD_DIM), lambda b, h: (part, b, 0, h))


def _seq_spec(s):
    return pl.BlockSpec((1, s, HEAD_DIM), lambda b, h: (b, 0, h))


_GAIN_SPEC = pl.BlockSpec((1, HEAD_DIM), lambda b, h: (0, 0))
_HEAD_ROW_SPEC = pl.BlockSpec((1, 1, 1, HEAD_DIM), lambda b, h: (b, h, 0, 0))


def _sb_group_spec(s, part):
    return pl.BlockSpec((1, 1, s, SB_GROUP_COLS), lambda b, g: (part, b, 0, g))


def _sb_seq_group_spec(s):
    return pl.BlockSpec((1, s, SB_GROUP_COLS), lambda b, g: (b, 0, g))


_SB_GROUP_ROW_SPEC = pl.BlockSpec((1, SB_HEADS_PER_STEP, 1, HEAD_DIM), lambda b, g: (b, g, 0, 0))


def _sb_chains(m):
    return [(h, m * SB_QBLOCKS_PER_STEP + r) for h in range(SB_HEADS_PER_STEP) for r in range(SB_QBLOCKS_PER_STEP)]


def _sb_logits(qi, kj):
    return _dot(qi, kj, NT) * (HEAD_DIM ** -0.5)


def _sb_scores(z, diag, live, tri_lt):
    soft = jnp.log(1.0 + jnp.exp(-jnp.abs(z)))
    valid = jnp.logical_and(live, jnp.logical_or(jnp.logical_not(diag), tri_lt))
    log_skip = jnp.where(valid, -(jnp.maximum(z, 0.0) + soft), 0.0)
    log_beta = jnp.minimum(z, 0.0) - soft
    return log_skip, log_beta, valid


def _sb_keys_left(chains, watch, state):
    done, carries = state[0], state[1]
    worst = None
    for (_, i), c in zip(chains, carries):
        c = jnp.where(done <= i, c[watch], -jnp.inf)
        worst = c if worst is None else jnp.maximum(worst, c)
    return jnp.logical_and(done <= chains[-1][1],
                           jnp.logical_or(done == 0, jnp.max(worst) > SB_LOG_WEIGHT_FLOOR))


def _sb_key_rows(i, done):
    j = i - done
    return pl.ds(pl.multiple_of(jnp.maximum(j, 0) * ATTN_BLOCK, ATTN_BLOCK), ATTN_BLOCK), j >= 0


def _sb_head(ref, h, rows):
    return ref[0, 0, rows, h * HEAD_DIM:(h + 1) * HEAD_DIM]


def sb_attn_fwd(qkv4, rider=None):
    _, b, s, _ = qkv4.shape
    blk = ATTN_BLOCK
    nq = s // blk

    def body(q_ref, k_ref, v_ref, o_ref):
        row = lax.broadcasted_iota(jnp.int32, (blk, blk), 0)
        col = lax.broadcasted_iota(jnp.int32, (blk, blk), 1)
        tri_lt = col < row
        suffix = (row > col).astype(BF16)

        def q_group(m, _):
            chains = _sb_chains(m)
            qis = [_sb_head(q_ref, h, pl.ds(pl.multiple_of(i * blk, blk), blk)) for h, i in chains]

            def walk(top, watch, state):
                qs = [qi[:top, :] for qi in qis]
                mask = tri_lt[:top, :]

                def k_step(state):
                    done, cs, accs = state
                    where = [_sb_key_rows(i, done) for _, i in chains]
                    zs = [_sb_logits(q, _sb_head(k_ref, h, rows)) for (h, _), q, (rows, _) in zip(chains, qs, where)]
                    scored = [_sb_scores(z, done == 0, live, mask) for z, (_, live) in zip(zs, where)]
                    afters = [_dot_exact(log_skip, suffix) for log_skip, _, _ in scored]
                    ws = [jnp.where(valid, jnp.exp(log_beta + after + c), 0.0).astype(BF16)
                          for (_, log_beta, valid), after, c in zip(scored, afters, cs)]
                    new_accs = [acc + _dot(w, _sb_head(v_ref, h, rows))
                                for (h, _), (rows, _), w, acc in zip(chains, where, ws, accs)]
                    new_cs = [c + jnp.sum(log_skip, axis=1, keepdims=True) for (log_skip, _, _), c in zip(scored, cs)]
                    return done + 1, tuple(new_cs), tuple(new_accs)

                return lax.while_loop(functools.partial(_sb_keys_left, chains, watch), k_step, state)

            n = len(chains)
            done, cs, accs = walk(blk, slice(SB_TOP_ROWS, blk),
                                  (jnp.int32(0), (jnp.zeros((blk, 1), F32),) * n, (jnp.zeros((blk, HEAD_DIM), F32),) * n))
            _, _, tops = walk(SB_TOP_ROWS, slice(0, SB_TOP_ROWS),
                              (done, tuple(c[:SB_TOP_ROWS] for c in cs), tuple(a[:SB_TOP_ROWS] for a in accs)))
            for (h, i), acc, top in zip(chains, accs, tops):
                o_ref[0, pl.ds(pl.multiple_of(i * blk, blk), blk), h * HEAD_DIM:(h + 1) * HEAD_DIM] = (
                    jnp.concatenate([top, acc[SB_TOP_ROWS:]], axis=0))
            return 0

        lax.fori_loop(0, nq // SB_QBLOCKS_PER_STEP, q_group, 0)

    (o,), extra = _pcall_riding(
        body, rider, (qkv4, qkv4, qkv4),
        name="sb_attn_fwd", grid=(b, N_HEADS // SB_HEADS_PER_STEP),
        in_specs=[_sb_group_spec(s, 0), _sb_group_spec(s, 1), _sb_group_spec(s, 2)],
        out_specs=[_sb_seq_group_spec(s)],
        out_shape=[jax.ShapeDtypeStruct((b, s, D_MODEL), F32)],
        semantics=("parallel", "parallel"))
    return o, extra


def _hg_masks():
    c = HG_CHUNK
    row = lax.broadcasted_iota(jnp.int32, (c, c), 0)
    col = lax.broadcasted_iota(jnp.int32, (c, c), 1)
    incl = (col <= row)
    lower = incl.astype(BF16)
    before_sub = (col < (row // HG_SUB) * HG_SUB).astype(BF16)
    upper = (col >= row).astype(BF16)
    return incl, lower, before_sub, upper


def _hg_lower_bound(lbl_ref):
    l0 = lbl_ref[0, 0]
    l1 = lbl_ref[1, 0]
    d = l1 - l0
    return _sigmoid_pair(d)


def _hg_gates(qp, fp, lb, oml):
    sq = _sigmoid(qp)
    sf, sfn = _sigmoid_pair(fp)
    f = lb + oml * sf
    return dict(qp=qp, sq=sq, q=qp * sq, sf=sf, sfn=sfn, f=f, k=oml * sfn, logf=jnp.log(f))


def _hg_intra(qds, ks, gcs, grs, incl):
    subs = range(HG_CHUNK // HG_SUB)
    qdbs = [qd.astype(BF16) for qd in qds]
    ess = [[jnp.exp(jnp.minimum(gr[sub * HG_SUB:sub * HG_SUB + 1, :] - gc, EXP_CLAMP)) for sub in subs]
           for gc, gr in zip(gcs, grs)]
    ksbs = [[(k * e).astype(BF16) for e in es] for k, es in zip(ks, ess)]
    rows = [[_dot(qdb[sub * HG_SUB:(sub + 1) * HG_SUB, :], ksb[sub], NT) for sub in subs]
            for qdb, ksb in zip(qdbs, ksbs)]
    a_s = [jnp.where(incl, jnp.concatenate(r, axis=0), 0.0) for r in rows]
    return a_s, qdbs, ksbs, ess


def _hg_group_rows(outer, unroll):
    ns = [outer * unroll + u for u in range(unroll)]
    return ns, [pl.ds(pl.multiple_of(n * HG_CHUNK, HG_CHUNK), HG_CHUNK) for n in ns]


def _state_spec(nchunk):
    return pl.BlockSpec((1, 1, nchunk, HEAD_DIM, HEAD_DIM), lambda b, h: (b, h, 0, 0, 0))


def hgrn2_fwd(proj4, lbl4, o_gain):
    _, b, s, _ = proj4.shape
    nchunk = s // HG_CHUNK
    c = HG_CHUNK
    unroll = math.gcd(nchunk, HG_UNROLL_FWD)

    def body(q_ref, f_ref, i_ref, lbl_ref, og_ref, o_ref, oraw_ref, st_ref):
        incl, lower, before_sub, _ = _hg_masks()
        lb, oml = _hg_lower_bound(lbl_ref)

        def group(outer, st):
            ns, rows = _hg_group_rows(outer, unroll)
            vs = [_hg_gates(q_ref[0, 0, r, :], f_ref[0, 0, r, :], lb, oml) for r in rows]
            inps = [i_ref[0, 0, r, :].astype(BF16) for r in rows]
            gcs = [_dot_exact(v["logf"], lower, left=True) for v in vs]
            grs = [_dot_exact(v["logf"], before_sub, left=True) for v in vs]
            a_s, _, _, _ = _hg_intra([v["q"] * jnp.exp(gc - gr) for v, gc, gr in zip(vs, gcs, grs)],
                                     [v["k"] for v in vs], gcs, grs, incl)
            gls = [gc[c - 1:c, :] for gc in gcs]
            adds = [_dot(inp, (v["k"] * jnp.exp(gl - gc)).astype(BF16), TN)
                    for inp, v, gl, gc in zip(inps, vs, gls, gcs)]
            o_intra = [_dot(a.astype(BF16), inp) for a, inp in zip(a_s, inps)]
            sts = []
            for gl, add in zip(gls, adds):
                sts.append(st)
                st = st * jnp.exp(gl) + add
            outs = [oi + _dot((v["q"] * jnp.exp(gc)).astype(BF16), s0.astype(BF16), NT)
                    for oi, v, gc, s0 in zip(o_intra, vs, gcs, sts)]
            for n, r, s0, o in zip(ns, rows, sts, outs):
                st_ref[0, 0, n] = s0
                oraw_ref[0, r, :] = o
                o_ref[0, r, :] = o * _rms_scale(o) * og_ref[...]
            return st

        lax.fori_loop(0, nchunk // unroll, group, jnp.zeros((HEAD_DIM, HEAD_DIM), F32))

    seq = jax.ShapeDtypeStruct((b, s, D_MODEL), F32)
    return _pcall(
        body, name="hgrn2_fwd", grid=(b, N_HEADS),
        in_specs=[_head_spec(s, 0), _head_spec(s, 1), _head_spec(s, 2),
                  pl.BlockSpec((2, 1, 1, HEAD_DIM), lambda b, h: (0, h, 0, 0)), _GAIN_SPEC],
        out_specs=[_seq_spec(s), _seq_spec(s), _state_spec(nchunk)],
        out_shape=[seq, seq, jax.ShapeDtypeStruct((b, N_HEADS, nchunk, HEAD_DIM, HEAD_DIM), F32)],
        compiler_params=_params("parallel", "parallel"),
    )(proj4, proj4, proj4, lbl4, o_gain)


def outproj_bwd(dh, w_out, o2, proj, name):
    t = dh.shape[0]
    tm = _row_tile(t, 256)

    def body(dh_ref, w_ref, o_ref, gate_ref, do_ref, dproj_ref, dw_ref):
        dhb = dh_ref[...].astype(BF16)
        dog = _dot(dhb, w_ref[...], NT)
        g = gate_ref[0]
        sg = _sigmoid(g)
        silu = g * sg
        o = o_ref[...]
        do_ref[...] = dog * silu
        dproj_ref[0] = dog * o * (sg * (1.0 + g * (1.0 - sg)))
        part = _dot((o * silu).astype(BF16), dhb, TN)

        @pl.when(pl.program_id(0) == 0)
        def _():
            dw_ref[...] = part

        @pl.when(pl.program_id(0) > 0)
        def _():
            dw_ref[...] += part

    row = pl.BlockSpec((tm, D_MODEL), lambda i: (i, 0))
    full = pl.BlockSpec((D_MODEL, D_MODEL), lambda i: (0, 0))
    return _pcall(
        body, name=name, grid=(t // tm,),
        in_specs=[row, full, row, pl.BlockSpec((1, tm, D_MODEL), lambda i: (3, i, 0))],
        out_specs=[row, pl.BlockSpec((1, tm, D_MODEL), lambda i: (3, i, 0)), full],
        out_shape=[jax.ShapeDtypeStruct((t, D_MODEL), F32),
                   jax.ShapeDtypeStruct((4, t, D_MODEL), F32),
                   jax.ShapeDtypeStruct((D_MODEL, D_MODEL), F32)],
        compiler_params=_params("arbitrary"),
    )(dh, w_out, o2, proj)


def inproj_bwd_dx(dproj, wg, x2, gain, dres, name, rider=None):
    t = x2.shape[0]
    tm = _row_tile(t, 256)

    def body(d_ref, w_ref, x_ref, g_ref, r_ref, dx_ref, dg_ref):
        du = jnp.zeros((tm, D_MODEL), F32)
        for p in range(N_DEV):
            cols = slice((p % 2) * W_COLS, (p % 2 + 1) * W_COLS)
            du = du + _dot(d_ref[p // 2, :, cols].astype(BF16), w_ref[p], NT)
        x = x_ref[...]
        r = _rms_scale(x)
        xh = x * r
        a = du * g_ref[...]
        dx_ref[...] = r_ref[...] + r * (a - xh * jnp.mean(a * xh, axis=-1, keepdims=True))
        part = jnp.sum(du * xh, axis=0, keepdims=True)

        @pl.when(pl.program_id(0) == 0)
        def _():
            dg_ref[...] = part

        @pl.when(pl.program_id(0) > 0)
        def _():
            dg_ref[...] += part

    row = pl.BlockSpec((tm, D_MODEL), lambda i: (i, 0))
    vec = pl.BlockSpec((1, D_MODEL), lambda i: (0, 0))
    return _pcall_riding(
        body, rider, (dproj, wg, x2, gain, dres), name=name, grid=(t // tm,),
        in_specs=[pl.BlockSpec((4, tm, D_MODEL), lambda i: (0, i, 0)),
                  pl.BlockSpec((N_DEV, D_MODEL, W_COLS), lambda i: (0, 0, 0)),
                  row, vec, row],
        out_specs=[row, vec],
        out_shape=[jax.ShapeDtypeStruct((t, D_MODEL), F32), jax.ShapeDtypeStruct((1, D_MODEL), F32)],
        semantics=("arbitrary",))


def inproj_bwd_dw(ut, dproj, name, out_dtype=F32):
    t = ut.shape[1]

    def body(ut_ref, d_ref, dw_ref):
        dw_ref[0] = _dot(ut_ref[...], d_ref[0].astype(BF16)).astype(dw_ref.dtype)

    return _pcall(
        body, name=name, grid=(N_DEV,),
        in_specs=[pl.BlockSpec((D_MODEL, t), lambda j: (0, 0)),
                  pl.BlockSpec((1, t, W_COLS), lambda j: (j // 2, 0, j % 2))],
        out_specs=pl.BlockSpec((1, D_MODEL, W_COLS), lambda j: (j, 0, 0)),
        out_shape=jax.ShapeDtypeStruct((N_DEV, D_MODEL, W_COLS), out_dtype),
        compiler_params=_params("parallel"),
    )(ut, dproj)


def _rms_bwd(x, gain, dy):
    r = _rms_scale(x)
    xh = x * r
    a = dy * gain
    return r * (a - xh * jnp.mean(a * xh, axis=-1, keepdims=True)), dy * xh


def sb_attn_bwd(qkv4, proj4, do3, o3, q_gain, k_gain, dproj4, rider=None):
    _, b, s, _ = proj4.shape
    blk = ATTN_BLOCK
    nq = s // blk
    scale = HEAD_DIM ** -0.5

    def body(qn_ref, kn_ref, v_ref, q_ref, k_ref, do_ref, o_ref, qg_ref, kg_ref, _alias, d_ref, dqg_ref, dkg_ref, dob):
        for h in range(SB_HEADS_PER_STEP):
            dob[h] = do_ref[0, :, h * HEAD_DIM:(h + 1) * HEAD_DIM].astype(BF16)
        d_ref[...] = jnp.zeros_like(d_ref)
        row = lax.broadcasted_iota(jnp.int32, (blk, blk), 0)
        col = lax.broadcasted_iota(jnp.int32, (blk, blk), 1)
        tri_lt = col < row
        suffix = (row > col).astype(BF16)
        suffix_incl = (row >= col).astype(BF16)

        def q_group(m, _):
            chains = _sb_chains(m)
            qis, dois, deltas = [], [], []
            for h, i in chains:
                rows_i = pl.ds(pl.multiple_of(i * blk, blk), blk)
                qis.append(_sb_head(qn_ref, h, rows_i))
                dois.append(dob[h, rows_i, :])
                deltas.append(jnp.sum(dois[-1].astype(F32) * o_ref[0, rows_i, h * HEAD_DIM:(h + 1) * HEAD_DIM],
                                      axis=1, keepdims=True))

            def walk(top, watch, state):
                qs = [qi[:top, :] for qi in qis]
                dos = [doi[:top, :] for doi in dois]
                dels = [delta[:top, :] for delta in deltas]
                mask = tri_lt[:top, :]

                def k_step(state):
                    done, cs, cgs, dqs = state
                    where = [_sb_key_rows(i, done) for _, i in chains]
                    kjs = [_sb_head(kn_ref, h, rows) for (h, _), (rows, _) in zip(chains, where)]
                    zs = [_sb_logits(q, kj) for q, kj in zip(qs, kjs)]
                    dws = [_dot(do, _sb_head(v_ref, h, rows), NT) for (h, _), (rows, _), do in zip(chains, where, dos)]
                    scored = [_sb_scores(z, done == 0, live, mask) for z, (_, live) in zip(zs, where)]
                    afters = [_dot_exact(log_skip, suffix) for log_skip, _, _ in scored]
                    wbs = [jnp.where(valid, jnp.exp(log_beta + after + c), 0.0).astype(BF16)
                           for (_, log_beta, valid), after, c in zip(scored, afters, cs)]
                    gs = [dw * wb.astype(F32) for dw, wb in zip(dws, wbs)]
                    befores = [delta - (_dot_exact(g, suffix_incl) + cg) for g, delta, cg in zip(gs, dels, cgs)]
                    dzbs = [jnp.where(valid, g - jnp.exp(log_beta) * (g + before), 0.0).astype(BF16)
                            for (_, log_beta, valid), g, before in zip(scored, gs, befores)]
                    new_dqs = [dq + _dot(dzb, kj) for dq, dzb, kj in zip(dqs, dzbs, kjs)]
                    for (h, _), (rows, _), wb, do, dzb, q in zip(chains, where, wbs, dos, dzbs, qs):
                        cols = slice(h * HEAD_DIM, (h + 1) * HEAD_DIM)
                        d_ref[2, 0, rows, cols] += _dot(wb, do, TN)
                        d_ref[1, 0, rows, cols] += _dot(dzb, q, TN)
                    new_cs = [c + jnp.sum(log_skip, axis=1, keepdims=True) for (log_skip, _, _), c in zip(scored, cs)]
                    new_cgs = [cg + jnp.sum(g, axis=1, keepdims=True) for g, cg in zip(gs, cgs)]
                    return done + 1, tuple(new_cs), tuple(new_cgs), tuple(new_dqs)

                return lax.while_loop(functools.partial(_sb_keys_left, chains, watch), k_step, state)

            n = len(chains)
            zero = (jnp.zeros((blk, 1), F32),) * n
            done, cs, cgs, dqs = walk(blk, slice(SB_TOP_ROWS, blk),
                                      (jnp.int32(0), zero, zero, (jnp.zeros((blk, HEAD_DIM), F32),) * n))
            top = slice(0, SB_TOP_ROWS)
            _, _, _, tops = walk(SB_TOP_ROWS, top, (done, tuple(c[top] for c in cs), tuple(cg[top] for cg in cgs),
                                                    tuple(dq[top] for dq in dqs)))
            for (h, i), dq, dq_top in zip(chains, dqs, tops):
                d_ref[0, 0, pl.ds(pl.multiple_of(i * blk, blk), blk), h * HEAD_DIM:(h + 1) * HEAD_DIM] = (
                    jnp.concatenate([dq_top, dq[SB_TOP_ROWS:]], axis=0) * scale)
            return 0

        lax.fori_loop(0, nq // SB_QBLOCKS_PER_STEP, q_group, 0)

        def norm_block(i, carry):
            rows = pl.ds(pl.multiple_of(i * blk, blk), blk)
            out = []
            for h in range(SB_HEADS_PER_STEP):
                cols = slice(h * HEAD_DIM, (h + 1) * HEAD_DIM)
                for part, src_ref, gain_ref in ((0, q_ref, qg_ref), (1, k_ref, kg_ref)):
                    dy = d_ref[part, 0, rows, cols] * (scale if part == 1 else 1.0)
                    dx, pg = _rms_bwd(src_ref[0, 0, rows, cols], gain_ref[...], dy)
                    d_ref[part, 0, rows, cols] = dx
                    out.append(carry[len(out)] + jnp.sum(pg, axis=0, keepdims=True))
            return tuple(out)

        sums = lax.fori_loop(0, nq, norm_block, (jnp.zeros((1, HEAD_DIM), F32),) * (2 * SB_HEADS_PER_STEP))
        for h in range(SB_HEADS_PER_STEP):
            dqg_ref[0, h] = sums[2 * h]
            dkg_ref[0, h] = sums[2 * h + 1]

    head_row = jax.ShapeDtypeStruct((b, N_HEADS, 1, HEAD_DIM), F32)
    return _pcall_riding(
        body, rider, (qkv4, qkv4, qkv4, proj4, proj4, do3, o3, q_gain, k_gain, dproj4),
        name="sb_attn_bwd", grid=(b, N_HEADS // SB_HEADS_PER_STEP),
        in_specs=[_sb_group_spec(s, 0), _sb_group_spec(s, 1), _sb_group_spec(s, 2),
                  _sb_group_spec(s, 0), _sb_group_spec(s, 1),
                  _sb_seq_group_spec(s), _sb_seq_group_spec(s), _GAIN_SPEC, _GAIN_SPEC,
                  pl.BlockSpec(memory_space=pl.ANY)],
        out_specs=[pl.BlockSpec((3, 1, s, SB_GROUP_COLS), lambda b, g: (0, b, 0, g)),
                   _SB_GROUP_ROW_SPEC, _SB_GROUP_ROW_SPEC],
        out_shape=[jax.ShapeDtypeStruct(dproj4.shape, F32), head_row, head_row],
        scratch_shapes=[pltpu.VMEM((SB_HEADS_PER_STEP, s, HEAD_DIM), BF16)],
        input_output_aliases={9: 0}, semantics=("parallel", "parallel"))


def hgrn2_bwd(proj4, don3, oraw3, states, lbl4, o_gain, dproj4, rider=None):
    _, b, s, _ = proj4.shape
    nchunk = s // HG_CHUNK
    c = HG_CHUNK
    subs = range(HG_CHUNK // HG_SUB)
    unroll = math.gcd(nchunk, HG_UNROLL_BWD)
    ngroup = nchunk // unroll

    def body(q_ref, f_ref, i_ref, don_ref, oraw_ref, st_ref, lbl_ref, og_ref, _alias, d_ref, dog_ref, dlb_ref):
        incl, lower, before_sub, upper = _hg_masks()
        lb, oml = _hg_lower_bound(lbl_ref)
        last_row = lax.broadcasted_iota(jnp.int32, (c, HEAD_DIM), 0) == c - 1

        def group(m, carry):
            dst, dog_acc, dlb_acc = carry
            ns, rows = _hg_group_rows(ngroup - 1 - m, unroll)
            ns, rows = ns[::-1], rows[::-1]
            vs = [_hg_gates(q_ref[0, 0, r, :], f_ref[0, 0, r, :], lb, oml) for r in rows]
            inps = [i_ref[0, 0, r, :].astype(BF16) for r in rows]
            sts = [st_ref[0, 0, n] for n in ns]
            gcs = [_dot_exact(v["logf"], lower, left=True) for v in vs]
            grs = [_dot_exact(v["logf"], before_sub, left=True) for v in vs]
            e_qs = [jnp.exp(gc - gr) for gc, gr in zip(gcs, grs)]
            a_s, qdbs, ksbs, ess = _hg_intra([v["q"] * e for v, e in zip(vs, e_qs)], [v["k"] for v in vs],
                                             gcs, grs, incl)
            e_gcs = [jnp.exp(gc) for gc in gcs]
            gls = [gc[c - 1:c, :] for gc in gcs]
            e_gls = [jnp.exp(gl) for gl in gls]
            e_ks = [jnp.exp(gl - gc) for gl, gc in zip(gls, gcs)]
            normed = [_rms_bwd(oraw_ref[0, r, :], og_ref[...], don_ref[0, r, :]) for r in rows]
            dobs = [do.astype(BF16) for do, _ in normed]
            dabs = [jnp.where(incl, _dot(dob, inp, NT), 0.0).astype(BF16) for dob, inp in zip(dobs, inps)]
            adds = [_dot(dob, (v["q"] * e).astype(BF16), TN) for dob, v, e in zip(dobs, vs, e_gcs)]
            dq_inters = [_dot(dob, st.astype(BF16)) * e for dob, st, e in zip(dobs, sts, e_gcs)]
            dqds = [jnp.concatenate([_dot(dab[sub * HG_SUB:(sub + 1) * HG_SUB, :], ksb[sub]) for sub in subs], axis=0)
                    for dab, ksb in zip(dabs, ksbs)]
            dkss = [[_dot(dab[sub * HG_SUB:(sub + 1) * HG_SUB, :], qdb[sub * HG_SUB:(sub + 1) * HG_SUB, :], TN)
                     for sub in subs] for dab, qdb in zip(dabs, qdbs)]
            dsts = []
            for e_gl, add in zip(e_gls, adds):
                dsts.append(dst)
                dst = dst * e_gl + add
            dstbs = [d.astype(BF16) for d in dsts]
            dis = [_dot(a.astype(BF16), dob, TN) + _dot((v["k"] * e_k).astype(BF16), dstb, NT)
                   for a, dob, v, e_k, dstb in zip(a_s, dobs, vs, e_ks, dstbs)]
            dk_inters = [_dot(inp, dstb) * e_k for inp, dstb, e_k in zip(inps, dstbs, e_ks)]
            for u, r in enumerate(rows):
                v, q, k = vs[u], vs[u]["q"], vs[u]["k"]
                dk, dgc_k = dk_inters[u], jnp.zeros((c, HEAD_DIM), F32)
                for sub in subs:
                    dk = dk + dkss[u][sub] * ess[u][sub]
                    dgc_k = dgc_k + dkss[u][sub] * ksbs[u][sub].astype(F32)
                dq = dqds[u] * e_qs[u] + dq_inters[u]
                at_last = (jnp.sum(k * dk_inters[u], axis=0, keepdims=True)
                           + e_gls[u] * jnp.sum(sts[u] * dsts[u], axis=0, keepdims=True))
                dgc = ((qdbs[u].astype(F32) * dqds[u] - dgc_k) + (q * dq_inters[u] - k * dk_inters[u])
                       + jnp.where(last_row, at_last, 0.0))
                dlf_f = _dot_exact(dgc, upper, left=True) / v["f"]
                d_ref[0, 0, r, :] = dq * (v["sq"] * (1.0 + v["qp"] * (1.0 - v["sq"])))
                d_ref[1, 0, r, :] = (dlf_f - dk) * (oml * v["sf"] * v["sfn"])
                d_ref[2, 0, r, :] = dis[u]
                dlb_acc = dlb_acc + jnp.sum((dlf_f - dk) * v["sfn"], axis=0, keepdims=True)
                dog_acc = dog_acc + jnp.sum(normed[u][1], axis=0, keepdims=True)
            return dst, dog_acc, dlb_acc

        zero = jnp.zeros((1, HEAD_DIM), F32)
        _, dog, dlb = lax.fori_loop(0, ngroup, group, (jnp.zeros((HEAD_DIM, HEAD_DIM), F32), zero, zero))
        dog_ref[0, 0] = dog
        dlb_ref[0, 0] = dlb

    head_row = jax.ShapeDtypeStruct((b, N_HEADS, 1, HEAD_DIM), F32)
    return _pcall_riding(
        body, rider, (proj4, proj4, proj4, don3, oraw3, states, lbl4, o_gain, dproj4),
        name="hgrn2_bwd", grid=(b, N_HEADS),
        in_specs=[_head_spec(s, 0), _head_spec(s, 1), _head_spec(s, 2), _seq_spec(s), _seq_spec(s),
                  _state_spec(nchunk), pl.BlockSpec((2, 1, 1, HEAD_DIM), lambda b, h: (0, h, 0, 0)), _GAIN_SPEC,
                  pl.BlockSpec(memory_space=pl.ANY)],
        out_specs=[pl.BlockSpec((3, 1, s, HEAD_DIM), lambda b, h: (0, b, 0, h)), _HEAD_ROW_SPEC, _HEAD_ROW_SPEC],
        out_shape=[jax.ShapeDtypeStruct(dproj4.shape, F32), head_row, head_row],
        input_output_aliases={8: 0}, semantics=("parallel", "parallel"))


def local_step(x, target, sb_norm, wsi, sb_q_gain, sb_k_gain, hg_o_gain, hg_lb_logits, wso_mine, whi_mine, who_mine,
               hg_norm_mine):
    b, s, _ = x.shape
    t = b * s
    x2 = x.reshape(t, D_MODEL)
    tg2 = target.reshape(t, D_MODEL)
    lbl4 = hg_lb_logits.reshape(2, N_HEADS, 1, HEAD_DIM)
    four = (4, b, s, D_MODEL)
    three = (b, s, D_MODEL)
    rows8 = (N_DEV, W_ROWS, D_MODEL)

    (proj0, u0, qkv0), (wso, who) = rms_inproj(x2, sb_norm, wsi, "sb_inproj",
                                               _Rider([wso_mine, who_mine], scatter=False),
                                               qk_gains=(sb_q_gain, sb_k_gain))
    qkv0 = qkv0.reshape(3, b, s, D_MODEL)
    wso = wso.reshape(D_MODEL, D_MODEL)
    who = who.reshape(D_MODEL, D_MODEL)
    o0, (whi, hgn) = sb_attn_fwd(qkv0, _Rider([whi_mine, hg_norm_mine], scatter=False))
    hg_norm_full = hgn[:, 0, :].reshape(1, D_MODEL)
    o0 = o0.reshape(t, D_MODEL)
    h1 = gate_outproj(o0, proj0, wso, x2, None, "sb_outproj")
    (proj1, u1), _ = rms_inproj(h1, hg_norm_full, whi, "hg_inproj")
    o1, o1_raw, states = hgrn2_fwd(proj1.reshape(four), lbl4, hg_o_gain)
    o1 = o1.reshape(t, D_MODEL)
    dh2, loss_parts = gate_outproj(o1, proj1, who, h1, tg2, "hg_outproj_loss")

    do1, dproj1, g_who = outproj_bwd(dh2, who, o1, proj1, "hg_outproj_bwd")
    (dproj1, g_og, g_lb), (p_who,) = hgrn2_bwd(proj1.reshape(four), do1.reshape(three), o1_raw, states, lbl4,
                                               hg_o_gain, dproj1.reshape(four),
                                               _Rider([g_who.reshape(rows8)], scatter=True))
    dproj1 = dproj1.reshape(4, t, D_MODEL)
    (dh1, g_hgn), _ = inproj_bwd_dx(dproj1, whi, h1, hg_norm_full, dh2, "hg_inproj_bwd_dx")
    g_whi = inproj_bwd_dw(u1, dproj1, "hg_inproj_bwd_dw", out_dtype=BF16)

    do0, dproj0, g_wso = outproj_bwd(dh1, wso, o0, proj0, "sb_outproj_bwd")
    (dproj0, g_qg, g_kg), (p_whi, p_wso) = sb_attn_bwd(qkv0, proj0.reshape(four), do0.reshape(three), o0.reshape(three),
                                                       sb_q_gain, sb_k_gain, dproj0.reshape(four),
                                                       _Rider([g_whi, g_wso.reshape(rows8)], scatter=True))
    dproj0 = dproj0.reshape(4, t, D_MODEL)
    g_wsi = inproj_bwd_dw(u0, dproj0, "sb_inproj_bwd_dw", out_dtype=BF16)
    (gx, g_sbn), (p_wsi,) = inproj_bwd_dx(dproj0, wsi, x2, sb_norm, dh1, "sb_inproj_bwd_dx",
                                          _Rider([g_wsi], scatter=True))
    return dict(loss_parts=loss_parts, gx=gx.reshape(three), p_wsi=p_wsi, p_wso=p_wso, p_whi=p_whi, p_who=p_who,
                g_sbn=g_sbn, g_hgn=g_hgn, g_qg=g_qg, g_kg=g_kg, g_og=g_og, g_lb=g_lb)


def _two_level_gather(src, out, send_sems, recv_sems, local_sem, pos):
    x, y, c = pos
    me, sibling = (x, y, c), (x, y, 1 - c)
    chips = [(1 - x, y), (x, 1 - y), (1 - x, 1 - y)]

    def copy(k, block, to, source=None):
        slot = out.at[_linear(block)]
        return pltpu.make_async_remote_copy(
            src_ref=slot if source is None else source, dst_ref=slot, send_sem=send_sems.at[k],
            recv_sem=recv_sems.at[k], device_id=to, device_id_type=MESH)

    mine = pltpu.make_async_copy(src, out.at[_linear(me)], local_sem)
    mine.start()
    first = [copy(0, me, sibling, src)] + [copy(1 + j, me, (*chip, c), src) for j, chip in enumerate(chips)]
    for cp in first:
        cp.start()
    passed = [copy(4 + j, (*chip, c), sibling) for j, chip in enumerate(chips)]
    for j, chip in enumerate(chips):
        copy(1 + j, (*chip, c), me).wait_recv()
        passed[j].start()
    copy(0, sibling, me).wait_recv()
    for j, chip in enumerate(chips):
        copy(4 + j, (*chip, 1 - c), me).wait_recv()
    for cp in first + passed:
        cp.wait_send()
    mine.wait()


def gather_first_weights(w_si, w_so, w_hi, w_ho, hg_norm):
    def body(si_ref, so_ref, hi_ref, ho_ref, hn_ref, o_si, so_b, hi_b, ho_b, hn_b, si_b, send_sems, recv_sems, local_sem):
        for src, buf in ((si_ref, si_b), (so_ref, so_b), (hi_ref, hi_b), (ho_ref, ho_b)):
            buf[...] = src[...].astype(BF16)
        hn_b[...] = jnp.broadcast_to(hn_ref[...], hn_b.shape)
        _two_level_gather(si_b, o_si, send_sems, recv_sems, local_sem, _mesh_pos())

    return _pcall(
        body, name="gather_first_weights",
        in_specs=[_VMEM] * 5, out_specs=[_ANY] + [_VMEM] * 4,
        out_shape=[jax.ShapeDtypeStruct((N_DEV,) + w_si.shape, BF16), jax.ShapeDtypeStruct(w_so.shape, BF16),
                   jax.ShapeDtypeStruct(w_hi.shape, BF16), jax.ShapeDtypeStruct(w_ho.shape, BF16),
                   jax.ShapeDtypeStruct((8, HEAD_DIM), F32)],
        scratch_shapes=[pltpu.VMEM(w_si.shape, BF16), pltpu.SemaphoreType.DMA((N_PEERS,)),
                        pltpu.SemaphoreType.DMA((N_PEERS,)), pltpu.SemaphoreType.DMA],
        compiler_params=pltpu.CompilerParams(vmem_limit_bytes=VMEM_LIMIT_BYTES),
    )(w_si, w_so, w_hi, w_ho, hg_norm)


def _adamw(w, g, m, v):
    m = ADAM_B1 * m + (1.0 - ADAM_B1) * g
    v = ADAM_B2 * v + (1.0 - ADAM_B2) * (g * g)
    m_hat = m / (1.0 - ADAM_B1 ** ADAM_STEP)
    v_hat = v / (1.0 - ADAM_B2 ** ADAM_STEP)
    delta = -ADAM_LR * (m_hat / (jnp.sqrt(v_hat) + ADAM_EPS) + ADAM_WD * w)
    return delta, m, v


def reduce_adamw(parts, w, m, v, name):
    _, r, c = parts.shape
    tr = _row_tile(r, 256)

    def body(p_ref, w_ref, m_ref, v_ref, g_ref, d_ref, m2_ref, v2_ref):
        g = p_ref[0].astype(F32)
        for dev in range(1, N_DEV):
            g = g + p_ref[dev].astype(F32)
        g_ref[...] = g
        d_ref[...], m2_ref[...], v2_ref[...] = _adamw(w_ref[...], g, m_ref[...], v_ref[...])

    tile = pl.BlockSpec((tr, c), lambda i: (i, 0))
    return _pcall(
        body, name=name, grid=(r // tr,),
        in_specs=[pl.BlockSpec((N_DEV, tr, c), lambda i: (0, i, 0)), tile, tile, tile],
        out_specs=[tile] * 4, out_shape=[jax.ShapeDtypeStruct((r, c), F32)] * 4,
        compiler_params=_params("parallel"),
    )(parts, w, m, v)


PACK_ROWS = 32
ROW_SBN, ROW_HGN, ROW_LB, ROW_QG, ROW_KG, ROW_OG, ROW_LOSS = 0, 8, 16, 24, 25, 26, 27


def small_update(g_sbn, g_hgn, g_lb, g_qg, g_kg, g_og, loss_parts, small):
    n_in = 7 + len(small)

    def body(*refs):
        sbn_ref, hgn_ref, lb_ref, qg_ref, kg_ref, og_ref, loss_ref = refs[:7]
        wmv = refs[7:n_in]
        outs = refs[n_in:n_in + 25]
        pack, gath, tot, send_sems, recv_sems, local_sems = refs[n_in + 25:]
        pos = _mesh_pos()
        me = _linear(pos)
        pack[...] = jnp.zeros_like(pack)
        pack[ROW_SBN:ROW_SBN + 8, :] = sbn_ref[...]
        pack[ROW_HGN:ROW_HGN + 8, :] = hgn_ref[...]
        pack[ROW_LB:ROW_LB + 8, :] = jnp.sum(lb_ref[...], axis=0)
        pack[ROW_QG:ROW_QG + 1, :] = jnp.sum(qg_ref[...], axis=0, keepdims=True)
        pack[ROW_KG:ROW_KG + 1, :] = jnp.sum(kg_ref[...], axis=0, keepdims=True)
        pack[ROW_OG:ROW_OG + 1, :] = jnp.sum(og_ref[...], axis=0, keepdims=True)
        pack[ROW_LOSS:ROW_LOSS + 1, :] = jnp.sum(loss_ref[...], axis=0)[0:1, :]
        _exchange_start([((lambda p: pack), gath)], send_sems, recv_sems, local_sems, pos)
        _exchange_wait([((lambda p: pack), gath)], send_sems, recv_sems, local_sems, pos)
        total = gath[0]
        for dev in range(1, N_DEV):
            total = total + gath[dev]
        tot[...] = total
        outs[0][...] = jnp.broadcast_to(tot[ROW_LOSS:ROW_LOSS + 1, :], (8, HEAD_DIM))
        l0 = wmv[15][0:8, :]
        l1 = wmv[15][8:16, :]
        p1, p0 = _sigmoid_pair(l1 - l0)
        d_l1 = p0 * p1 * tot[ROW_LB:ROW_LB + 8, :]
        grads = [tot[ROW_SBN:ROW_SBN + 8, :], tot[ROW_QG:ROW_QG + 1, :], tot[ROW_KG:ROW_KG + 1, :],
                 tot[pl.ds(ROW_HGN + me, 1), :], tot[ROW_OG:ROW_OG + 1, :],
                 jnp.concatenate([-d_l1, d_l1], axis=0)]
        for i, g in enumerate(grads):
            w_ref, m_ref, v_ref = wmv[3 * i:3 * i + 3]
            o = outs[1 + 4 * i:5 + 4 * i]
            o[0][...] = g
            o[1][...], o[2][...], o[3][...] = _adamw(w_ref[...], g, m_ref[...], v_ref[...])

    out_shape = [jax.ShapeDtypeStruct((8, HEAD_DIM), F32)]
    for i in range(6):
        out_shape += [jax.ShapeDtypeStruct(small[3 * i].shape, F32)] * 4
    return _pcall(
        body, name="small_update",
        in_specs=[_VMEM] * n_in, out_specs=[_VMEM] * 25, out_shape=out_shape,
        scratch_shapes=[pltpu.VMEM((PACK_ROWS, HEAD_DIM), F32), pltpu.VMEM((N_DEV, PACK_ROWS, HEAD_DIM), F32),
                        pltpu.VMEM((PACK_ROWS, HEAD_DIM), F32)] + _exchange_sems(1),
    )(g_sbn, g_hgn, g_lb, g_qg, g_kg, g_og, loss_parts, *small)


def kernel(x, sb_norm, sb_w_in, sb_q_gain, sb_k_gain, sb_w_out, hg_norm, hg_w_in, hg_o_gain, hg_w_out, hg_lb_logits, loss_target, m_sb_norm, m_sb_w_in, m_sb_q_gain, m_sb_k_gain, m_sb_w_out, m_hg_norm, m_hg_w_in, m_hg_o_gain, m_hg_w_out, m_hg_lb_logits, v_sb_norm, v_sb_w_in, v_sb_q_gain, v_sb_k_gain, v_sb_w_out, v_hg_norm, v_hg_w_in, v_hg_o_gain, v_hg_w_out, v_hg_lb_logits):
    b = x.shape[0]
    wsi, wso_mine, whi_mine, who_mine, hg_norm_mine = gather_first_weights(
        sb_w_in[0], sb_w_out[0], hg_w_in[0], hg_w_out[0], hg_norm)
    r = local_step(x, loss_target, sb_norm, wsi, sb_q_gain, sb_k_gain, hg_o_gain, hg_lb_logits,
                   wso_mine, whi_mine, who_mine, hg_norm_mine)
    big = {}
    for name, p, w, m, v in (("sb_w_in", r["p_wsi"], sb_w_in, m_sb_w_in, v_sb_w_in),
                             ("sb_w_out", r["p_wso"], sb_w_out, m_sb_w_out, v_sb_w_out),
                             ("hg_w_in", r["p_whi"], hg_w_in, m_hg_w_in, v_hg_w_in),
                             ("hg_w_out", r["p_who"], hg_w_out, m_hg_w_out, v_hg_w_out)):
        big[name] = [o[None] for o in reduce_adamw(p, w[0], m[0], v[0], "adamw_" + name)]

    def rows8(a):
        return a.reshape(8, HEAD_DIM)

    def rows16(a):
        return a.reshape(16, HEAD_DIM)

    small_in = [rows8(sb_norm), rows8(m_sb_norm), rows8(v_sb_norm),
                sb_q_gain, m_sb_q_gain, v_sb_q_gain,
                sb_k_gain, m_sb_k_gain, v_sb_k_gain,
                hg_norm, m_hg_norm, v_hg_norm,
                hg_o_gain, m_hg_o_gain, v_hg_o_gain,
                rows16(hg_lb_logits), rows16(m_hg_lb_logits), rows16(v_hg_lb_logits)]
    so = small_update(rows8(r["g_sbn"]), rows8(r["g_hgn"]), r["g_lb"].reshape(b, N_HEADS, HEAD_DIM),
                      r["g_qg"].reshape(b * N_HEADS, HEAD_DIM), r["g_kg"].reshape(b * N_HEADS, HEAD_DIM),
                      r["g_og"].reshape(b * N_HEADS, HEAD_DIM), r["loss_parts"], small_in)
    loss = so[0][0, 0]
    shapes = {"sb_norm": (1, D_MODEL), "sb_q_gain": (1, HEAD_DIM), "sb_k_gain": (1, HEAD_DIM),
              "hg_norm": (1, HEAD_DIM), "hg_o_gain": (1, HEAD_DIM), "hg_lb_logits": (2, D_MODEL)}
    small = {}
    for i, name in enumerate(("sb_norm", "sb_q_gain", "sb_k_gain", "hg_norm", "hg_o_gain", "hg_lb_logits")):
        small[name] = [o.reshape(shapes[name]) for o in so[1 + 4 * i:5 + 4 * i]]
    order = ("sb_norm", "sb_w_in", "sb_q_gain", "sb_k_gain", "sb_w_out",
             "hg_norm", "hg_w_in", "hg_o_gain", "hg_w_out", "hg_lb_logits")
    res = {**big, **small}
    return (loss, r["gx"]) + tuple(res[n][j] for j in range(4) for n in order)
```

```python
import functools
import math

import jax
import jax.numpy as jnp
from jax import lax
from jax.experimental import pallas as pl
from jax.experimental.pallas import tpu as pltpu

F32 = jnp.float32
BF16 = jnp.bfloat16

N_DEV = 8
D_MODEL = 1024
N_HEADS = 8
HEAD_DIM = 128
RMS_EPS = 1e-6
ATTN_BLOCK = 128
HG_CHUNK = 64
HG_SUB = 16
HG_UNROLL_FWD = 16
HG_UNROLL_BWD = 16
EXP_CLAMP = 80.0
SB_HEADS_PER_STEP = 2
SB_QBLOCKS_PER_STEP = 4
SB_GROUP_COLS = SB_HEADS_PER_STEP * 128
SB_TOP_ROWS = 32
SB_LOG_WEIGHT_FLOOR = -104.0
VMEM_LIMIT_BYTES = 48 * 1024 * 1024
W_COLS = 4 * D_MODEL // N_DEV
W_ROWS = D_MODEL // N_DEV

ADAM_LR = 0.001
ADAM_B1 = 0.9
ADAM_B2 = 0.999
ADAM_EPS = 1e-08
ADAM_WD = 0.01
ADAM_STEP = 10

NT = (((1,), (1,)), ((), ()))
TN = (((0,), (0,)), ((), ()))
NN = (((1,), (0,)), ((), ()))


def _pcall(body, *, name, **kw):
    return pl.pallas_call(body, name=name, **kw)


def _params(*sem):
    return pltpu.CompilerParams(dimension_semantics=sem, vmem_limit_bytes=VMEM_LIMIT_BYTES)


def _dot(a, b, dims=NN):
    return lax.dot_general(a, b, dims, preferred_element_type=F32)


def _dot_exact(a, m, dims=NN, left=False):
    hi = a.astype(BF16)
    lo = (a - hi.astype(F32)).astype(BF16)
    if left:
        return _dot(m, hi, dims) + _dot(m, lo, dims)
    return _dot(hi, m, dims) + _dot(lo, m, dims)


def _split(a):
    hi = a.astype(BF16)
    return hi, (a - hi.astype(F32)).astype(BF16)


def _dot3(a, b, dims=NN):
    return _dot(a[0], b[0], dims) + (_dot(a[0], b[1], dims) + _dot(a[1], b[0], dims))


def _sigmoid(x):
    return 1.0 / (1.0 + jnp.exp(-x))


def _sigmoid_pair(x):
    e = jnp.exp(-jnp.abs(x))
    big = 1.0 / (1.0 + e)
    small = e * big
    pos = x >= 0
    return jnp.where(pos, big, small), jnp.where(pos, small, big)


def _rms_scale(x):
    return lax.rsqrt(jnp.mean(x * x, axis=-1, keepdims=True) + RMS_EPS)


def _row_tile(t, want):
    return want if t % want == 0 else t


MESH = pl.DeviceIdType.MESH
N_PEERS = N_DEV - 1
_ANY = pl.BlockSpec(memory_space=pl.ANY)
_VMEM = pl.BlockSpec(memory_space=pltpu.VMEM)


def _mesh_pos():
    return lax.axis_index("x"), lax.axis_index("y"), lax.axis_index("c")


def _linear(pos):
    return 4 * pos[0] + 2 * pos[1] + pos[2]


def _peer(pos, k):
    flips = ((k + 1) >> 2 & 1, (k + 1) >> 1 & 1, (k + 1) & 1)
    return tuple(1 - p if f else p for p, f in zip(pos, flips))


def _exchange_copies(pairs, send_sems, recv_sems, local_sems, pos, landing):
    me = _linear(pos)
    local, remote = [], []
    for a, (src_of, dst) in enumerate(pairs):
        local.append(pltpu.make_async_copy(src_of(me), dst.at[me], local_sems.at[a]))
        for k in range(N_PEERS):
            peer = _peer(pos, k)
            remote.append(pltpu.make_async_remote_copy(
                src_ref=src_of(_linear(peer)), dst_ref=dst.at[_linear(peer) if landing else me],
                send_sem=send_sems.at[a, k], recv_sem=recv_sems.at[a, k], device_id=peer, device_id_type=MESH))
    return local, remote


def _exchange_start(pairs, send_sems, recv_sems, local_sems, pos):
    local, sent = _exchange_copies(pairs, send_sems, recv_sems, local_sems, pos, landing=False)
    for copy in local + sent:
        copy.start()


def _exchange_wait(pairs, send_sems, recv_sems, local_sems, pos):
    local, landed = _exchange_copies(pairs, send_sems, recv_sems, local_sems, pos, landing=True)
    for copy in landed:
        copy.wait_recv()
        copy.wait_send()
    for copy in local:
        copy.wait()


def _exchange_sems(n):
    return [pltpu.SemaphoreType.DMA((n, N_PEERS)), pltpu.SemaphoreType.DMA((n, N_PEERS)),
            pltpu.SemaphoreType.DMA((n,))]


class _Rider:
    def __init__(self, arrays, scatter):
        self.arrays = list(arrays)
        self.scatter = scatter
        self.out_shapes = [jax.ShapeDtypeStruct(a.shape if scatter else (N_DEV,) + a.shape, a.dtype)
                           for a in self.arrays]

    def pairs(self, in_refs, out_refs):
        if self.scatter:
            return [((lambda p, r=r: r.at[p]), o) for r, o in zip(in_refs, out_refs)]
        return [((lambda p, r=r: r), o) for r, o in zip(in_refs, out_refs)]


def _pcall_riding(body, rider, args, *, name, grid, in_specs, out_specs, out_shape, semantics, scratch_shapes=(),
                  input_output_aliases=None):
    aliases = input_output_aliases or {}
    if rider is None:
        outs = _pcall(body, name=name, grid=grid, in_specs=list(in_specs), out_specs=list(out_specs),
                      out_shape=list(out_shape), scratch_shapes=list(scratch_shapes), input_output_aliases=aliases,
                      compiler_params=_params(*semantics))(*args)
        return list(outs), []
    n_in, n_out, n_scr, n_r = len(in_specs), len(out_specs), len(scratch_shapes), len(rider.arrays)

    def riding(*refs):
        ins, refs = refs[:n_in], refs[n_in:]
        rider_in, refs = refs[:n_r], refs[n_r:]
        outs, refs = refs[:n_out], refs[n_out:]
        rider_out, refs = refs[:n_r], refs[n_r:]
        scratch, sems = refs[:n_scr], refs[n_scr:]
        pairs = rider.pairs(rider_in, rider_out)
        first = functools.reduce(jnp.logical_and, [pl.program_id(a) == 0 for a in range(len(grid))])
        last = functools.reduce(jnp.logical_and, [pl.program_id(a) == g - 1 for a, g in enumerate(grid)])

        @pl.when(first)
        def _():
            _exchange_start(pairs, *sems, _mesh_pos())

        body(*ins, *outs, *scratch)

        @pl.when(last)
        def _():
            _exchange_wait(pairs, *sems, _mesh_pos())

    outs = _pcall(riding, name=name, grid=grid, in_specs=list(in_specs) + [_ANY] * n_r,
                  out_specs=list(out_specs) + [_ANY] * n_r, out_shape=list(out_shape) + rider.out_shapes,
                  scratch_shapes=list(scratch_shapes) + _exchange_sems(n_r), input_output_aliases=aliases,
                  compiler_params=_params(*(("arbitrary",) * len(grid))))(*args, *rider.arrays)
    return list(outs[:n_out]), list(outs[n_out:])


def rms_inproj(x2, gain, wg, name, rider=None, qk_gains=None):
    t = x2.shape[0]
    tm = _row_tile(t, 256)
    with_qkv = qk_gains is not None

    def body(x_ref, g_ref, w_ref, *rest):
        if with_qkv:
            qg_ref, kg_ref, proj_ref, ut_ref, qkv_ref = rest
            head_gain = (qg_ref, kg_ref)
        else:
            proj_ref, ut_ref = rest
        x = x_ref[...]
        u = x * _rms_scale(x) * g_ref[...]
        ut_ref[...] = u.T.astype(BF16)
        u = u.astype(BF16)
        for p in range(N_DEV):
            part, lo = p // 2, (p % 2) * W_COLS
            res = _dot(u, w_ref[p])
            proj_ref[part, :, lo:lo + W_COLS] = res
            if with_qkv and part < 3:
                for h in range(W_COLS // HEAD_DIM):
                    y = res[:, h * HEAD_DIM:(h + 1) * HEAD_DIM]
                    if part < 2:
                        y = y * _rms_scale(y) * head_gain[part][...]
                    qkv_ref[part, :, lo + h * HEAD_DIM:lo + (h + 1) * HEAD_DIM] = y.astype(BF16)

    vec = pl.BlockSpec((1, D_MODEL), lambda i: (0, 0))
    in_specs = [pl.BlockSpec((tm, D_MODEL), lambda i: (i, 0)), vec,
                pl.BlockSpec((N_DEV, D_MODEL, W_COLS), lambda i: (0, 0, 0))]
    out_specs = [pl.BlockSpec((4, tm, D_MODEL), lambda i: (0, i, 0)), pl.BlockSpec((D_MODEL, tm), lambda i: (0, i))]
    out_shape = [jax.ShapeDtypeStruct((4, t, D_MODEL), F32), jax.ShapeDtypeStruct((D_MODEL, t), BF16)]
    args = (x2, gain, wg)
    if with_qkv:
        in_specs += [pl.BlockSpec((1, HEAD_DIM), lambda i: (0, 0))] * 2
        out_specs.append(pl.BlockSpec((3, tm, D_MODEL), lambda i: (0, i, 0)))
        out_shape.append(jax.ShapeDtypeStruct((3, t, D_MODEL), BF16))
        args += tuple(qk_gains)
    return _pcall_riding(body, rider, args, name=name, grid=(t // tm,), in_specs=in_specs, out_specs=out_specs,
                         out_shape=out_shape, semantics=("parallel",))


def gate_outproj(o2, proj, w_out, resid, target, name):
    t = o2.shape[0]
    tm = _row_tile(t, 256)
    with_loss = target is not None

    def body(o_ref, gate_ref, w_ref, r_ref, *rest):
        g = gate_ref[0]
        og = (o_ref[...] * (g * _sigmoid(g))).astype(BF16)
        h = r_ref[...] + _dot(og, w_ref[...])
        if with_loss:
            t_ref, dh_ref, loss_ref = rest
            err = h - t_ref[...]
            dh_ref[...] = err * (1.0 / D_MODEL)
            part = 0.5 * jnp.sum(jnp.mean(err * err, axis=-1, keepdims=True))
            loss_ref[...] = jnp.full(loss_ref.shape, part, F32)
        else:
            (h_ref,) = rest
            h_ref[...] = h

    row = pl.BlockSpec((tm, D_MODEL), lambda i: (i, 0))
    in_specs = [row,
                pl.BlockSpec((1, tm, D_MODEL), lambda i: (3, i, 0)),
                pl.BlockSpec((D_MODEL, D_MODEL), lambda i: (0, 0)),
                row]
    args = [o2, proj, w_out, resid]
    if with_loss:
        in_specs.append(row)
        args.append(target)
        out_specs = [row, pl.BlockSpec((1, 8, 128), lambda i: (i, 0, 0))]
        out_shape = [jax.ShapeDtypeStruct((t, D_MODEL), F32),
                     jax.ShapeDtypeStruct((t // tm, 8, 128), F32)]
    else:
        out_specs = row
        out_shape = jax.ShapeDtypeStruct((t, D_MODEL), F32)
    return _pcall(body, name=name, grid=(t // tm,), in_specs=in_specs, out_specs=out_specs,
                  out_shape=out_shape, compiler_params=_params("parallel"))(*args)


def _head_spec(s, part):
    return pl.BlockSpec((1, 1, s, HEAD_DIM), lambda b, h: (part, b, 0, h))


def _seq_spec(s):
    return pl.BlockSpec((1, s, HEAD_DIM), lambda b, h: (b, 0, h))


_GAIN_SPEC = pl.BlockSpec((1, HEAD_DIM), lambda b, h: (0, 0))
_HEAD_ROW_SPEC = pl.BlockSpec((1, 1, 1, HEAD_DIM), lambda b, h: (b, h, 0, 0))


def _sb_group_spec(s, part):
    return pl.BlockSpec((1, 1, s, SB_GROUP_COLS), lambda b, g: (part, b, 0, g))


def _sb_seq_group_spec(s):
    return pl.BlockSpec((1, s, SB_GROUP_COLS), lambda b, g: (b, 0, g))


_SB_GROUP_ROW_SPEC = pl.BlockSpec((1, SB_HEADS_PER_STEP, 1, HEAD_DIM), lambda b, g: (b, g, 0, 0))


def _sb_chains(m):
    return [(h, m * SB_QBLOCKS_PER_STEP + r) for h in range(SB_HEADS_PER_STEP) for r in range(SB_QBLOCKS_PER_STEP)]


def _sb_logits(qi, kj):
    return _dot(qi, kj, NT) * (HEAD_DIM ** -0.5)


def _sb_scores(z, diag, live, tri_lt):
    soft = jnp.log(1.0 + jnp.exp(-jnp.abs(z)))
    valid = jnp.logical_and(live, jnp.logical_or(jnp.logical_not(diag), tri_lt))
    log_skip = jnp.where(valid, -(jnp.maximum(z, 0.0) + soft), 0.0)
    log_beta = jnp.minimum(z, 0.0) - soft
    return log_skip, log_beta, valid


def _sb_keys_left(chains, watch, state):
    done, carries = state[0], state[1]
    worst = None
    for (_, i), c in zip(chains, carries):
        c = jnp.where(done <= i, c[watch], -jnp.inf)
        worst = c if worst is None else jnp.maximum(worst, c)
    return jnp.logical_and(done <= chains[-1][1],
                           jnp.logical_or(done == 0, jnp.max(worst) > SB_LOG_WEIGHT_FLOOR))


def _sb_key_rows(i, done):
    j = i - done
    return pl.ds(pl.multiple_of(jnp.maximum(j, 0) * ATTN_BLOCK, ATTN_BLOCK), ATTN_BLOCK), j >= 0


def _sb_head(ref, h, rows):
    return ref[0, 0, rows, h * HEAD_DIM:(h + 1) * HEAD_DIM]


def sb_attn_fwd(qkv4, rider=None):
    _, b, s, _ = qkv4.shape
    blk = ATTN_BLOCK
    nq = s // blk

    def body(q_ref, k_ref, v_ref, o_ref):
        row = lax.broadcasted_iota(jnp.int32, (blk, blk), 0)
        col = lax.broadcasted_iota(jnp.int32, (blk, blk), 1)
        tri_lt = col < row
        suffix = (row > col).astype(BF16)

        def q_group(m, _):
            chains = _sb_chains(m)
            qis = [_sb_head(q_ref, h, pl.ds(pl.multiple_of(i * blk, blk), blk)) for h, i in chains]

            def walk(top, watch, state):
                qs = [qi[:top, :] for qi in qis]
                mask = tri_lt[:top, :]

                def k_step(state):
                    done, cs, accs = state
                    where = [_sb_key_rows(i, done) for _, i in chains]
                    zs = [_sb_logits(q, _sb_head(k_ref, h, rows)) for (h, _), q, (rows, _) in zip(chains, qs, where)]
                    scored = [_sb_scores(z, done == 0, live, mask) for z, (_, live) in zip(zs, where)]
                    afters = [_dot_exact(log_skip, suffix) for log_skip, _, _ in scored]
                    ws = [jnp.where(valid, jnp.exp(log_beta + after + c), 0.0).astype(BF16)
                          for (_, log_beta, valid), after, c in zip(scored, afters, cs)]
                    new_accs = [acc + _dot(w, _sb_head(v_ref, h, rows))
                                for (h, _), (rows, _), w, acc in zip(chains, where, ws, accs)]
                    new_cs = [c + jnp.sum(log_skip, axis=1, keepdims=True) for (log_skip, _, _), c in zip(scored, cs)]
                    return done + 1, tuple(new_cs), tuple(new_accs)

                return lax.while_loop(functools.partial(_sb_keys_left, chains, watch), k_step, state)

            n = len(chains)
            done, cs, accs = walk(blk, slice(SB_TOP_ROWS, blk),
                                  (jnp.int32(0), (jnp.zeros((blk, 1), F32),) * n, (jnp.zeros((blk, HEAD_DIM), F32),) * n))
            _, _, tops = walk(SB_TOP_ROWS, slice(0, SB_TOP_ROWS),
                              (done, tuple(c[:SB_TOP_ROWS] for c in cs), tuple(a[:SB_TOP_ROWS] for a in accs)))
            for (h, i), acc, top in zip(chains, accs, tops):
                o_ref[0, pl.ds(pl.multiple_of(i * blk, blk), blk), h * HEAD_DIM:(h + 1) * HEAD_DIM] = (
                    jnp.concatenate([top, acc[SB_TOP_ROWS:]], axis=0))
            return 0

        lax.fori_loop(0, nq // SB_QBLOCKS_PER_STEP, q_group, 0)

    (o,), extra = _pcall_riding(
        body, rider, (qkv4, qkv4, qkv4),
        name="sb_attn_fwd", grid=(b, N_HEADS // SB_HEADS_PER_STEP),
        in_specs=[_sb_group_spec(s, 0), _sb_group_spec(s, 1), _sb_group_spec(s, 2)],
        out_specs=[_sb_seq_group_spec(s)],
        out_shape=[jax.ShapeDtypeStruct((b, s, D_MODEL), F32)],
        semantics=("parallel", "parallel"))
    return o, extra


def _hg_masks():
    c = HG_CHUNK
    row = lax.broadcasted_iota(jnp.int32, (c, c), 0)
    col = lax.broadcasted_iota(jnp.int32, (c, c), 1)
    incl = (col <= row)
    lower = incl.astype(BF16)
    before_sub = (col < (row // HG_SUB) * HG_SUB).astype(BF16)
    upper = (col >= row).astype(BF16)
    return incl, lower, before_sub, upper


def _hg_lower_bound(lbl_ref):
    l0 = lbl_ref[0, 0]
    l1 = lbl_ref[1, 0]
    d = l1 - l0
    return _sigmoid_pair(d)


def _hg_gates(qp, fp, lb, oml):
    sq = _sigmoid(qp)
    sf, sfn = _sigmoid_pair(fp)
    f = lb + oml * sf
    return dict(qp=qp, sq=sq, q=qp * sq, sf=sf, sfn=sfn, f=f, k=oml * sfn, logf=jnp.log(f))


def _hg_intra(qds, ks, gcs, grs, incl):
    subs = range(HG_CHUNK // HG_SUB)
    qdbs = [qd.astype(BF16) for qd in qds]
    ess = [[jnp.exp(jnp.minimum(gr[sub * HG_SUB:sub * HG_SUB + 1, :] - gc, EXP_CLAMP)) for sub in subs]
           for gc, gr in zip(gcs, grs)]
    ksbs = [[(k * e).astype(BF16) for e in es] for k, es in zip(ks, ess)]
    rows = [[_dot(qdb[sub * HG_SUB:(sub + 1) * HG_SUB, :], ksb[sub], NT) for sub in subs]
            for qdb, ksb in zip(qdbs, ksbs)]
    a_s = [jnp.where(incl, jnp.concatenate(r, axis=0), 0.0) for r in rows]
    return a_s, qdbs, ksbs, ess


def _hg_group_rows(outer, unroll):
    ns = [outer * unroll + u for u in range(unroll)]
    return ns, [pl.ds(pl.multiple_of(n * HG_CHUNK, HG_CHUNK), HG_CHUNK) for n in ns]


def _state_spec(nchunk):
    return pl.BlockSpec((1, 1, nchunk, HEAD_DIM, HEAD_DIM), lambda b, h: (b, h, 0, 0, 0))


def hgrn2_fwd(proj4, lbl4, o_gain):
    _, b, s, _ = proj4.shape
    nchunk = s // HG_CHUNK
    c = HG_CHUNK
    unroll = math.gcd(nchunk, HG_UNROLL_FWD)

    def body(q_ref, f_ref, i_ref, lbl_ref, og_ref, o_ref, oraw_ref, st_ref):
        incl, lower, before_sub, _ = _hg_masks()
        lb, oml = _hg_lower_bound(lbl_ref)

        def group(outer, st):
            ns, rows = _hg_group_rows(outer, unroll)
            vs = [_hg_gates(q_ref[0, 0, r, :], f_ref[0, 0, r, :], lb, oml) for r in rows]
            inps = [i_ref[0, 0, r, :].astype(BF16) for r in rows]
            gcs = [_dot_exact(v["logf"], lower, left=True) for v in vs]
            grs = [_dot_exact(v["logf"], before_sub, left=True) for v in vs]
            a_s, _, _, _ = _hg_intra([v["q"] * jnp.exp(gc - gr) for v, gc, gr in zip(vs, gcs, grs)],
                                     [v["k"] for v in vs], gcs, grs, incl)
            gls = [gc[c - 1:c, :] for gc in gcs]
            adds = [_dot(inp, (v["k"] * jnp.exp(gl - gc)).astype(BF16), TN)
                    for inp, v, gl, gc in zip(inps, vs, gls, gcs)]
            o_intra = [_dot(a.astype(BF16), inp) for a, inp in zip(a_s, inps)]
            sts = []
            for gl, add in zip(gls, adds):
                sts.append(st)
                st = st * jnp.exp(gl) + add
            outs = [oi + _dot((v["q"] * jnp.exp(gc)).astype(BF16), s0.astype(BF16), NT)
                    for oi, v, gc, s0 in zip(o_intra, vs, gcs, sts)]
            for n, r, s0, o in zip(ns, rows, sts, outs):
                st_ref[0, 0, n] = s0
                oraw_ref[0, r, :] = o
                o_ref[0, r, :] = o * _rms_scale(o) * og_ref[...]
            return st

        lax.fori_loop(0, nchunk // unroll, group, jnp.zeros((HEAD_DIM, HEAD_DIM), F32))

    seq = jax.ShapeDtypeStruct((b, s, D_MODEL), F32)
    return _pcall(
        body, name="hgrn2_fwd", grid=(b, N_HEADS),
        in_specs=[_head_spec(s, 0), _head_spec(s, 1), _head_spec(s, 2),
                  pl.BlockSpec((2, 1, 1, HEAD_DIM), lambda b, h: (0, h, 0, 0)), _GAIN_SPEC],
        out_specs=[_seq_spec(s), _seq_spec(s), _state_spec(nchunk)],
        out_shape=[seq, seq, jax.ShapeDtypeStruct((b, N_HEADS, nchunk, HEAD_DIM, HEAD_DIM), F32)],
        compiler_params=_params("parallel", "parallel"),
    )(proj4, proj4, proj4, lbl4, o_gain)


def outproj_bwd(dh, w_out, o2, proj, name):
    t = dh.shape[0]
    tm = _row_tile(t, 256)

    def body(dh_ref, w_ref, o_ref, gate_ref, do_ref, dproj_ref, dw_ref):
        dhb = dh_ref[...].astype(BF16)
        dog = _dot(dhb, w_ref[...], NT)
        g = gate_ref[0]
        sg = _sigmoid(g)
        silu = g * sg
        o = o_ref[...]
        do_ref[...] = dog * silu
        dproj_ref[0] = dog * o * (sg * (1.0 + g * (1.0 - sg)))
        part = _dot((o * silu).astype(BF16), dhb, TN)

        @pl.when(pl.program_id(0) == 0)
        def _():
            dw_ref[...] = part

        @pl.when(pl.program_id(0) > 0)
        def _():
            dw_ref[...] += part

    row = pl.BlockSpec((tm, D_MODEL), lambda i: (i, 0))
    full = pl.BlockSpec((D_MODEL, D_MODEL), lambda i: (0, 0))
    return _pcall(
        body, name=name, grid=(t // tm,),
        in_specs=[row, full, row, pl.BlockSpec((1, tm, D_MODEL), lambda i: (3, i, 0))],
        out_specs=[row, pl.BlockSpec((1, tm, D_MODEL), lambda i: (3, i, 0)), full],
        out_shape=[jax.ShapeDtypeStruct((t, D_MODEL), F32),
                   jax.ShapeDtypeStruct((4, t, D_MODEL), F32),
                   jax.ShapeDtypeStruct((D_MODEL, D_MODEL), F32)],
        compiler_params=_params("arbitrary"),
    )(dh, w_out, o2, proj)


def inproj_bwd_dx(dproj, wg, x2, gain, dres, name, rider=None):
    t = x2.shape[0]
    tm = _row_tile(t, 256)

    def body(d_ref, w_ref, x_ref, g_ref, r_ref, dx_ref, dg_ref):
        du = jnp.zeros((tm, D_MODEL), F32)
        for p in range(N_DEV):
            cols = slice((p % 2) * W_COLS, (p % 2 + 1) * W_COLS)
            du = du + _dot(d_ref[p // 2, :, cols].astype(BF16), w_ref[p], NT)
        x = x_ref[...]
        r = _rms_scale(x)
        xh = x * r
        a = du * g_ref[...]
        dx_ref[...] = r_ref[...] + r * (a - xh * jnp.mean(a * xh, axis=-1, keepdims=True))
        part = jnp.sum(du * xh, axis=0, keepdims=True)

        @pl.when(pl.program_id(0) == 0)
        def _():
            dg_ref[...] = part

        @pl.when(pl.program_id(0) > 0)
        def _():
            dg_ref[...] += part

    row = pl.BlockSpec((tm, D_MODEL), lambda i: (i, 0))
    vec = pl.BlockSpec((1, D_MODEL), lambda i: (0, 0))
    return _pcall_riding(
        body, rider, (dproj, wg, x2, gain, dres), name=name, grid=(t // tm,),
        in_specs=[pl.BlockSpec((4, tm, D_MODEL), lambda i: (0, i, 0)),
                  pl.BlockSpec((N_DEV, D_MODEL, W_COLS), lambda i: (0, 0, 0)),
                  row, vec, row],
        out_specs=[row, vec],
        out_shape=[jax.ShapeDtypeStruct((t, D_MODEL), F32), jax.ShapeDtypeStruct((1, D_MODEL), F32)],
        semantics=("arbitrary",))


def inproj_bwd_dw(ut, dproj, name, out_dtype=F32, half=None, rider=None):
    t = ut.shape[1]
    nrows = D_MODEL if half is None else D_MODEL // 2
    first = 0 if half is None else half

    def body(ut_ref, d_ref, dw_ref):
        dw_ref[0] = _dot(ut_ref[...], d_ref[0].astype(BF16)).astype(dw_ref.dtype)

    (dw,), extra = _pcall_riding(
        body, rider, (ut, dproj), name=name, grid=(N_DEV,),
        in_specs=[pl.BlockSpec((nrows, t), lambda j: (first, 0)),
                  pl.BlockSpec((1, t, W_COLS), lambda j: (j // 2, 0, j % 2))],
        out_specs=[pl.BlockSpec((1, nrows, W_COLS), lambda j: (j, 0, 0))],
        out_shape=[jax.ShapeDtypeStruct((N_DEV, nrows, W_COLS), out_dtype)],
        semantics=("parallel",))
    return dw, extra


def _rms_bwd(x, gain, dy):
    r = _rms_scale(x)
    xh = x * r
    a = dy * gain
    return r * (a - xh * jnp.mean(a * xh, axis=-1, keepdims=True)), dy * xh


def sb_attn_bwd(qkv4, proj4, do3, o3, q_gain, k_gain, dproj4, rider=None):
    _, b, s, _ = proj4.shape
    blk = ATTN_BLOCK
    nq = s // blk
    scale = HEAD_DIM ** -0.5

    def body(qn_ref, kn_ref, v_ref, q_ref, k_ref, do_ref, o_ref, qg_ref, kg_ref, _alias, d_ref, dqg_ref, dkg_ref, dob):
        for h in range(SB_HEADS_PER_STEP):
            dob[h] = do_ref[0, :, h * HEAD_DIM:(h + 1) * HEAD_DIM].astype(BF16)
        d_ref[...] = jnp.zeros_like(d_ref)
        row = lax.broadcasted_iota(jnp.int32, (blk, blk), 0)
        col = lax.broadcasted_iota(jnp.int32, (blk, blk), 1)
        tri_lt = col < row
        suffix = (row > col).astype(BF16)
        suffix_incl = (row >= col).astype(BF16)

        def q_group(m, _):
            chains = _sb_chains(m)
            qis, dois, deltas = [], [], []
            for h, i in chains:
                rows_i = pl.ds(pl.multiple_of(i * blk, blk), blk)
                qis.append(_sb_head(qn_ref, h, rows_i))
                dois.append(dob[h, rows_i, :])
                deltas.append(jnp.sum(dois[-1].astype(F32) * o_ref[0, rows_i, h * HEAD_DIM:(h + 1) * HEAD_DIM],
                                      axis=1, keepdims=True))

            def walk(top, watch, state):
                qs = [qi[:top, :] for qi in qis]
                dos = [doi[:top, :] for doi in dois]
                dels = [delta[:top, :] for delta in deltas]
                mask = tri_lt[:top, :]

                def k_step(state):
                    done, cs, cgs, dqs = state
                    where = [_sb_key_rows(i, done) for _, i in chains]
                    kjs = [_sb_head(kn_ref, h, rows) for (h, _), (rows, _) in zip(chains, where)]
                    zs = [_sb_logits(q, kj) for q, kj in zip(qs, kjs)]
                    dws = [_dot(do, _sb_head(v_ref, h, rows), NT) for (h, _), (rows, _), do in zip(chains, where, dos)]
                    scored = [_sb_scores(z, done == 0, live, mask) for z, (_, live) in zip(zs, where)]
                    afters = [_dot_exact(log_skip, suffix) for log_skip, _, _ in scored]
                    wbs = [jnp.where(valid, jnp.exp(log_beta + after + c), 0.0).astype(BF16)
                           for (_, log_beta, valid), after, c in zip(scored, afters, cs)]
                    gs = [dw * wb.astype(F32) for dw, wb in zip(dws, wbs)]
                    befores = [delta - (_dot_exact(g, suffix_incl) + cg) for g, delta, cg in zip(gs, dels, cgs)]
                    dzbs = [jnp.where(valid, g - jnp.exp(log_beta) * (g + before), 0.0).astype(BF16)
                            for (_, log_beta, valid), g, before in zip(scored, gs, befores)]
                    new_dqs = [dq + _dot(dzb, kj) for dq, dzb, kj in zip(dqs, dzbs, kjs)]
                    for (h, _), (rows, _), wb, do, dzb, q in zip(chains, where, wbs, dos, dzbs, qs):
                        cols = slice(h * HEAD_DIM, (h + 1) * HEAD_DIM)
                        d_ref[2, 0, rows, cols] += _dot(wb, do, TN)
                        d_ref[1, 0, rows, cols] += _dot(dzb, q, TN)
                    new_cs = [c + jnp.sum(log_skip, axis=1, keepdims=True) for (log_skip, _, _), c in zip(scored, cs)]
                    new_cgs = [cg + jnp.sum(g, axis=1, keepdims=True) for g, cg in zip(gs, cgs)]
                    return done + 1, tuple(new_cs), tuple(new_cgs), tuple(new_dqs)

                return lax.while_loop(functools.partial(_sb_keys_left, chains, watch), k_step, state)

            n = len(chains)
            zero = (jnp.zeros((blk, 1), F32),) * n
            done, cs, cgs, dqs = walk(blk, slice(SB_TOP_ROWS, blk),
                                      (jnp.int32(0), zero, zero, (jnp.zeros((blk, HEAD_DIM), F32),) * n))
            top = slice(0, SB_TOP_ROWS)
            _, _, _, tops = walk(SB_TOP_ROWS, top, (done, tuple(c[top] for c in cs), tuple(cg[top] for cg in cgs),
                                                    tuple(dq[top] for dq in dqs)))
            for (h, i), dq, dq_top in zip(chains, dqs, tops):
                d_ref[0, 0, pl.ds(pl.multiple_of(i * blk, blk), blk), h * HEAD_DIM:(h + 1) * HEAD_DIM] = (
                    jnp.concatenate([dq_top, dq[SB_TOP_ROWS:]], axis=0) * scale)
            return 0

        lax.fori_loop(0, nq // SB_QBLOCKS_PER_STEP, q_group, 0)

        def norm_block(i, carry):
            rows = pl.ds(pl.multiple_of(i * blk, blk), blk)
            out = []
            for h in range(SB_HEADS_PER_STEP):
                cols = slice(h * HEAD_DIM, (h + 1) * HEAD_DIM)
                for part, src_ref, gain_ref in ((0, q_ref, qg_ref), (1, k_ref, kg_ref)):
                    dy = d_ref[part, 0, rows, cols] * (scale if part == 1 else 1.0)
                    dx, pg = _rms_bwd(src_ref[0, 0, rows, cols], gain_ref[...], dy)
                    d_ref[part, 0, rows, cols] = dx
                    out.append(carry[len(out)] + jnp.sum(pg, axis=0, keepdims=True))
            return tuple(out)

        sums = lax.fori_loop(0, nq, norm_block, (jnp.zeros((1, HEAD_DIM), F32),) * (2 * SB_HEADS_PER_STEP))
        for h in range(SB_HEADS_PER_STEP):
            dqg_ref[0, h] = sums[2 * h]
            dkg_ref[0, h] = sums[2 * h + 1]

    head_row = jax.ShapeDtypeStruct((b, N_HEADS, 1, HEAD_DIM), F32)
    return _pcall_riding(
        body, rider, (qkv4, qkv4, qkv4, proj4, proj4, do3, o3, q_gain, k_gain, dproj4),
        name="sb_attn_bwd", grid=(b, N_HEADS // SB_HEADS_PER_STEP),
        in_specs=[_sb_group_spec(s, 0), _sb_group_spec(s, 1), _sb_group_spec(s, 2),
                  _sb_group_spec(s, 0), _sb_group_spec(s, 1),
                  _sb_seq_group_spec(s), _sb_seq_group_spec(s), _GAIN_SPEC, _GAIN_SPEC,
                  pl.BlockSpec(memory_space=pl.ANY)],
        out_specs=[pl.BlockSpec((3, 1, s, SB_GROUP_COLS), lambda b, g: (0, b, 0, g)),
                   _SB_GROUP_ROW_SPEC, _SB_GROUP_ROW_SPEC],
        out_shape=[jax.ShapeDtypeStruct(dproj4.shape, F32), head_row, head_row],
        scratch_shapes=[pltpu.VMEM((SB_HEADS_PER_STEP, s, HEAD_DIM), BF16)],
        input_output_aliases={9: 0}, semantics=("parallel", "parallel"))


def hgrn2_bwd(proj4, don3, oraw3, states, lbl4, o_gain, dproj4, rider=None):
    _, b, s, _ = proj4.shape
    nchunk = s // HG_CHUNK
    c = HG_CHUNK
    subs = range(HG_CHUNK // HG_SUB)
    unroll = math.gcd(nchunk, HG_UNROLL_BWD)
    ngroup = nchunk // unroll

    def body(q_ref, f_ref, i_ref, don_ref, oraw_ref, st_ref, lbl_ref, og_ref, _alias, d_ref, dog_ref, dlb_ref):
        incl, lower, before_sub, upper = _hg_masks()
        lb, oml = _hg_lower_bound(lbl_ref)
        last_row = lax.broadcasted_iota(jnp.int32, (c, HEAD_DIM), 0) == c - 1

        def group(m, carry):
            dst, dog_acc, dlb_acc = carry
            ns, rows = _hg_group_rows(ngroup - 1 - m, unroll)
            ns, rows = ns[::-1], rows[::-1]
            vs = [_hg_gates(q_ref[0, 0, r, :], f_ref[0, 0, r, :], lb, oml) for r in rows]
            inps = [i_ref[0, 0, r, :].astype(BF16) for r in rows]
            sts = [st_ref[0, 0, n] for n in ns]
            gcs = [_dot_exact(v["logf"], lower, left=True) for v in vs]
            grs = [_dot_exact(v["logf"], before_sub, left=True) for v in vs]
            e_qs = [jnp.exp(gc - gr) for gc, gr in zip(gcs, grs)]
            a_s, qdbs, ksbs, ess = _hg_intra([v["q"] * e for v, e in zip(vs, e_qs)], [v["k"] for v in vs],
                                             gcs, grs, incl)
            e_gcs = [jnp.exp(gc) for gc in gcs]
            gls = [gc[c - 1:c, :] for gc in gcs]
            e_gls = [jnp.exp(gl) for gl in gls]
            e_ks = [jnp.exp(gl - gc) for gl, gc in zip(gls, gcs)]
            normed = [_rms_bwd(oraw_ref[0, r, :], og_ref[...], don_ref[0, r, :]) for r in rows]
            dobs = [do.astype(BF16) for do, _ in normed]
            dabs = [jnp.where(incl, _dot(dob, inp, NT), 0.0).astype(BF16) for dob, inp in zip(dobs, inps)]
            adds = [_dot(dob, (v["q"] * e).astype(BF16), TN) for dob, v, e in zip(dobs, vs, e_gcs)]
            dq_inters = [_dot(dob, st.astype(BF16)) * e for dob, st, e in zip(dobs, sts, e_gcs)]
            dqds = [jnp.concatenate([_dot(dab[sub * HG_SUB:(sub + 1) * HG_SUB, :], ksb[sub]) for sub in subs], axis=0)
                    for dab, ksb in zip(dabs, ksbs)]
            dkss = [[_dot(dab[sub * HG_SUB:(sub + 1) * HG_SUB, :], qdb[sub * HG_SUB:(sub + 1) * HG_SUB, :], TN)
                     for sub in subs] for dab, qdb in zip(dabs, qdbs)]
            dsts = []
            for e_gl, add in zip(e_gls, adds):
                dsts.append(dst)
                dst = dst * e_gl + add
            dstbs = [d.astype(BF16) for d in dsts]
            dis = [_dot(a.astype(BF16), dob, TN) + _dot((v["k"] * e_k).astype(BF16), dstb, NT)
                   for a, dob, v, e_k, dstb in zip(a_s, dobs, vs, e_ks, dstbs)]
            dk_inters = [_dot(inp, dstb) * e_k for inp, dstb, e_k in zip(inps, dstbs, e_ks)]
            for u, r in enumerate(rows):
                v, q, k = vs[u], vs[u]["q"], vs[u]["k"]
                dk, dgc_k = dk_inters[u], jnp.zeros((c, HEAD_DIM), F32)
                for sub in subs:
                    dk = dk + dkss[u][sub] * ess[u][sub]
                    dgc_k = dgc_k + dkss[u][sub] * ksbs[u][sub].astype(F32)
                dq = dqds[u] * e_qs[u] + dq_inters[u]
                at_last = (jnp.sum(k * dk_inters[u], axis=0, keepdims=True)
                           + e_gls[u] * jnp.sum(sts[u] * dsts[u], axis=0, keepdims=True))
                dgc = ((qdbs[u].astype(F32) * dqds[u] - dgc_k) + (q * dq_inters[u] - k * dk_inters[u])
                       + jnp.where(last_row, at_last, 0.0))
                dlf_f = _dot_exact(dgc, upper, left=True) / v["f"]
                d_ref[0, 0, r, :] = dq * (v["sq"] * (1.0 + v["qp"] * (1.0 - v["sq"])))
                d_ref[1, 0, r, :] = (dlf_f - dk) * (oml * v["sf"] * v["sfn"])
                d_ref[2, 0, r, :] = dis[u]
                dlb_acc = dlb_acc + jnp.sum((dlf_f - dk) * v["sfn"], axis=0, keepdims=True)
                dog_acc = dog_acc + jnp.sum(normed[u][1], axis=0, keepdims=True)
            return dst, dog_acc, dlb_acc

        zero = jnp.zeros((1, HEAD_DIM), F32)
        _, dog, dlb = lax.fori_loop(0, ngroup, group, (jnp.zeros((HEAD_DIM, HEAD_DIM), F32), zero, zero))
        dog_ref[0, 0] = dog
        dlb_ref[0, 0] = dlb

    head_row = jax.ShapeDtypeStruct((b, N_HEADS, 1, HEAD_DIM), F32)
    return _pcall_riding(
        body, rider, (proj4, proj4, proj4, don3, oraw3, states, lbl4, o_gain, dproj4),
        name="hgrn2_bwd", grid=(b, N_HEADS),
        in_specs=[_head_spec(s, 0), _head_spec(s, 1), _head_spec(s, 2), _seq_spec(s), _seq_spec(s),
                  _state_spec(nchunk), pl.BlockSpec((2, 1, 1, HEAD_DIM), lambda b, h: (0, h, 0, 0)), _GAIN_SPEC,
                  pl.BlockSpec(memory_space=pl.ANY)],
        out_specs=[pl.BlockSpec((3, 1, s, HEAD_DIM), lambda b, h: (0, b, 0, h)), _HEAD_ROW_SPEC, _HEAD_ROW_SPEC],
        out_shape=[jax.ShapeDtypeStruct(dproj4.shape, F32), head_row, head_row],
        input_output_aliases={8: 0}, semantics=("parallel", "parallel"))


def local_step(x, target, sb_norm, wsi, sb_q_gain, sb_k_gain, hg_o_gain, hg_lb_logits, wso_mine, whi_mine, who_mine,
               hg_norm_mine):
    b, s, _ = x.shape
    t = b * s
    x2 = x.reshape(t, D_MODEL)
    tg2 = target.reshape(t, D_MODEL)
    lbl4 = hg_lb_logits.reshape(2, N_HEADS, 1, HEAD_DIM)
    four = (4, b, s, D_MODEL)
    three = (b, s, D_MODEL)
    rows8 = (N_DEV, W_ROWS, D_MODEL)

    (proj0, u0, qkv0), (wso, who) = rms_inproj(x2, sb_norm, wsi, "sb_inproj",
                                               _Rider([wso_mine, who_mine], scatter=False),
                                               qk_gains=(sb_q_gain, sb_k_gain))
    qkv0 = qkv0.reshape(3, b, s, D_MODEL)
    wso = wso.reshape(D_MODEL, D_MODEL)
    who = who.reshape(D_MODEL, D_MODEL)
    o0, (whi, hgn) = sb_attn_fwd(qkv0, _Rider([whi_mine, hg_norm_mine], scatter=False))
    hg_norm_full = hgn[:, 0, :].reshape(1, D_MODEL)
    o0 = o0.reshape(t, D_MODEL)
    h1 = gate_outproj(o0, proj0, wso, x2, None, "sb_outproj")
    (proj1, u1), _ = rms_inproj(h1, hg_norm_full, whi, "hg_inproj")
    o1, o1_raw, states = hgrn2_fwd(proj1.reshape(four), lbl4, hg_o_gain)
    o1 = o1.reshape(t, D_MODEL)
    dh2, loss_parts = gate_outproj(o1, proj1, who, h1, tg2, "hg_outproj_loss")

    do1, dproj1, g_who = outproj_bwd(dh2, who, o1, proj1, "hg_outproj_bwd")
    (dproj1, g_og, g_lb), (p_who,) = hgrn2_bwd(proj1.reshape(four), do1.reshape(three), o1_raw, states, lbl4,
                                               hg_o_gain, dproj1.reshape(four),
                                               _Rider([g_who.reshape(rows8)], scatter=True))
    dproj1 = dproj1.reshape(4, t, D_MODEL)
    (dh1, g_hgn), _ = inproj_bwd_dx(dproj1, whi, h1, hg_norm_full, dh2, "hg_inproj_bwd_dx")
    g_whi, _ = inproj_bwd_dw(u1, dproj1, "hg_inproj_bwd_dw", out_dtype=BF16)

    do0, dproj0, g_wso = outproj_bwd(dh1, wso, o0, proj0, "sb_outproj_bwd")
    (dproj0, g_qg, g_kg), (p_whi, p_wso) = sb_attn_bwd(qkv0, proj0.reshape(four), do0.reshape(three), o0.reshape(three),
                                                       sb_q_gain, sb_k_gain, dproj0.reshape(four),
                                                       _Rider([g_whi, g_wso.reshape(rows8)], scatter=True))
    dproj0 = dproj0.reshape(4, t, D_MODEL)
    g_up, _ = inproj_bwd_dw(u0, dproj0, "sb_inproj_bwd_dw_upper", out_dtype=BF16, half=0)
    g_low, (p_up,) = inproj_bwd_dw(u0, dproj0, "sb_inproj_bwd_dw_lower", out_dtype=BF16, half=1,
                                   rider=_Rider([g_up], scatter=True))
    (gx, g_sbn), (p_low,) = inproj_bwd_dx(dproj0, wsi, x2, sb_norm, dh1, "sb_inproj_bwd_dx",
                                          _Rider([g_low], scatter=True))
    p_wsi = jnp.concatenate([p_up, p_low], axis=1)
    return dict(loss_parts=loss_parts, gx=gx.reshape(three), p_wsi=p_wsi, p_wso=p_wso, p_whi=p_whi, p_who=p_who,
                g_sbn=g_sbn, g_hgn=g_hgn, g_qg=g_qg, g_kg=g_kg, g_og=g_og, g_lb=g_lb)


def _two_level_gather(src, out, send_sems, recv_sems, local_sem, pos):
    x, y, c = pos
    me, sibling = (x, y, c), (x, y, 1 - c)
    chips = [(1 - x, y), (x, 1 - y), (1 - x, 1 - y)]

    def copy(k, block, to, source=None):
        slot = out.at[_linear(block)]
        return pltpu.make_async_remote_copy(
            src_ref=slot if source is None else source, dst_ref=slot, send_sem=send_sems.at[k],
            recv_sem=recv_sems.at[k], device_id=to, device_id_type=MESH)

    mine = pltpu.make_async_copy(src, out.at[_linear(me)], local_sem)
    mine.start()
    first = [copy(0, me, sibling, src)] + [copy(1 + j, me, (*chip, c), src) for j, chip in enumerate(chips)]
    for cp in first:
        cp.start()
    passed = [copy(4 + j, (*chip, c), sibling) for j, chip in enumerate(chips)]
    for j, chip in enumerate(chips):
        copy(1 + j, (*chip, c), me).wait_recv()
        passed[j].start()
    copy(0, sibling, me).wait_recv()
    for j, chip in enumerate(chips):
        copy(4 + j, (*chip, 1 - c), me).wait_recv()
    for cp in first + passed:
        cp.wait_send()
    mine.wait()


def gather_first_weights(w_si, w_so, w_hi, w_ho, hg_norm):
    def body(si_ref, so_ref, hi_ref, ho_ref, hn_ref, o_si, so_b, hi_b, ho_b, hn_b, si_b, send_sems, recv_sems, local_sem):
        for src, buf in ((si_ref, si_b), (so_ref, so_b), (hi_ref, hi_b), (ho_ref, ho_b)):
            buf[...] = src[...].astype(BF16)
        hn_b[...] = jnp.broadcast_to(hn_ref[...], hn_b.shape)
        _two_level_gather(si_b, o_si, send_sems, recv_sems, local_sem, _mesh_pos())

    return _pcall(
        body, name="gather_first_weights",
        in_specs=[_VMEM] * 5, out_specs=[_ANY] + [_VMEM] * 4,
        out_shape=[jax.ShapeDtypeStruct((N_DEV,) + w_si.shape, BF16), jax.ShapeDtypeStruct(w_so.shape, BF16),
                   jax.ShapeDtypeStruct(w_hi.shape, BF16), jax.ShapeDtypeStruct(w_ho.shape, BF16),
                   jax.ShapeDtypeStruct((8, HEAD_DIM), F32)],
        scratch_shapes=[pltpu.VMEM(w_si.shape, BF16), pltpu.SemaphoreType.DMA((N_PEERS,)),
                        pltpu.SemaphoreType.DMA((N_PEERS,)), pltpu.SemaphoreType.DMA],
        compiler_params=pltpu.CompilerParams(vmem_limit_bytes=VMEM_LIMIT_BYTES),
    )(w_si, w_so, w_hi, w_ho, hg_norm)


def _adamw(w, g, m, v):
    m = ADAM_B1 * m + (1.0 - ADAM_B1) * g
    v = ADAM_B2 * v + (1.0 - ADAM_B2) * (g * g)
    m_hat = m / (1.0 - ADAM_B1 ** ADAM_STEP)
    v_hat = v / (1.0 - ADAM_B2 ** ADAM_STEP)
    delta = -ADAM_LR * (m_hat / (jnp.sqrt(v_hat) + ADAM_EPS) + ADAM_WD * w)
    return delta, m, v


def reduce_adamw(parts, w, m, v, name):
    _, r, c = parts.shape
    tr = _row_tile(r, 256)

    def body(p_ref, w_ref, m_ref, v_ref, g_ref, d_ref, m2_ref, v2_ref):
        g = p_ref[0].astype(F32)
        for dev in range(1, N_DEV):
            g = g + p_ref[dev].astype(F32)
        g_ref[...] = g
        d_ref[...], m2_ref[...], v2_ref[...] = _adamw(w_ref[...], g, m_ref[...], v_ref[...])

    tile = pl.BlockSpec((tr, c), lambda i: (i, 0))
    return _pcall(
        body, name=name, grid=(r // tr,),
        in_specs=[pl.BlockSpec((N_DEV, tr, c), lambda i: (0, i, 0)), tile, tile, tile],
        out_specs=[tile] * 4, out_shape=[jax.ShapeDtypeStruct((r, c), F32)] * 4,
        compiler_params=_params("parallel"),
    )(parts, w, m, v)


PACK_ROWS = 32
ROW_SBN, ROW_HGN, ROW_LB, ROW_QG, ROW_KG, ROW_OG, ROW_LOSS = 0, 8, 16, 24, 25, 26, 27


def small_update(g_sbn, g_hgn, g_lb, g_qg, g_kg, g_og, loss_parts, small):
    n_in = 7 + len(small)

    def body(*refs):
        sbn_ref, hgn_ref, lb_ref, qg_ref, kg_ref, og_ref, loss_ref = refs[:7]
        wmv = refs[7:n_in]
        outs = refs[n_in:n_in + 25]
        pack, gath, tot, send_sems, recv_sems, local_sems = refs[n_in + 25:]
        pos = _mesh_pos()
        me = _linear(pos)
        pack[...] = jnp.zeros_like(pack)
        pack[ROW_SBN:ROW_SBN + 8, :] = sbn_ref[...]
        pack[ROW_HGN:ROW_HGN + 8, :] = hgn_ref[...]
        pack[ROW_LB:ROW_LB + 8, :] = jnp.sum(lb_ref[...], axis=0)
        pack[ROW_QG:ROW_QG + 1, :] = jnp.sum(qg_ref[...], axis=0, keepdims=True)
        pack[ROW_KG:ROW_KG + 1, :] = jnp.sum(kg_ref[...], axis=0, keepdims=True)
        pack[ROW_OG:ROW_OG + 1, :] = jnp.sum(og_ref[...], axis=0, keepdims=True)
        pack[ROW_LOSS:ROW_LOSS + 1, :] = jnp.sum(loss_ref[...], axis=0)[0:1, :]
        _exchange_start([((lambda p: pack), gath)], send_sems, recv_sems, local_sems, pos)
        _exchange_wait([((lambda p: pack), gath)], send_sems, recv_sems, local_sems, pos)
        total = gath[0]
        for dev in range(1, N_DEV):
            total = total + gath[dev]
        tot[...] = total
        outs[0][...] = jnp.broadcast_to(tot[ROW_LOSS:ROW_LOSS + 1, :], (8, HEAD_DIM))
        l0 = wmv[15][0:8, :]
        l1 = wmv[15][8:16, :]
        p1, p0 = _sigmoid_pair(l1 - l0)
        d_l1 = p0 * p1 * tot[ROW_LB:ROW_LB + 8, :]
        grads = [tot[ROW_SBN:ROW_SBN + 8, :], tot[ROW_QG:ROW_QG + 1, :], tot[ROW_KG:ROW_KG + 1, :],
                 tot[pl.ds(ROW_HGN + me, 1), :], tot[ROW_OG:ROW_OG + 1, :],
                 jnp.concatenate([-d_l1, d_l1], axis=0)]
        for i, g in enumerate(grads):
            w_ref, m_ref, v_ref = wmv[3 * i:3 * i + 3]
            o = outs[1 + 4 * i:5 + 4 * i]
            o[0][...] = g
            o[1][...], o[2][...], o[3][...] = _adamw(w_ref[...], g, m_ref[...], v_ref[...])

    out_shape = [jax.ShapeDtypeStruct((8, HEAD_DIM), F32)]
    for i in range(6):
        out_shape += [jax.ShapeDtypeStruct(small[3 * i].shape, F32)] * 4
    return _pcall(
        body, name="small_update",
        in_specs=[_VMEM] * n_in, out_specs=[_VMEM] * 25, out_shape=out_shape,
        scratch_shapes=[pltpu.VMEM((PACK_ROWS, HEAD_DIM), F32), pltpu.VMEM((N_DEV, PACK_ROWS, HEAD_DIM), F32),
                        pltpu.VMEM((PACK_ROWS, HEAD_DIM), F32)] + _exchange_sems(1),
    )(g_sbn, g_hgn, g_lb, g_qg, g_kg, g_og, loss_parts, *small)


def kernel(x, sb_norm, sb_w_in, sb_q_gain, sb_k_gain, sb_w_out, hg_norm, hg_w_in, hg_o_gain, hg_w_out, hg_lb_logits, loss_target, m_sb_norm, m_sb_w_in, m_sb_q_gain, m_sb_k_gain, m_sb_w_out, m_hg_norm, m_hg_w_in, m_hg_o_gain, m_hg_w_out, m_hg_lb_logits, v_sb_norm, v_sb_w_in, v_sb_q_gain, v_sb_k_gain, v_sb_w_out, v_hg_norm, v_hg_w_in, v_hg_o_gain, v_hg_w_out, v_hg_lb_logits):
    b = x.shape[0]
    wsi, wso_mine, whi_mine, who_mine, hg_norm_mine = gather_first_weights(
        sb_w_in[0], sb_w_out[0], hg_w_in[0], hg_w_out[0], hg_norm)
    r = local_step(x, loss_target, sb_norm, wsi, sb_q_gain, sb_k_gain, hg_o_gain, hg_lb_logits,
                   wso_mine, whi_mine, who_mine, hg_norm_mine)
    big = {}
    for name, p, w, m, v in (("sb_w_in", r["p_wsi"], sb_w_in, m_sb_w_in, v_sb_w_in),
                             ("sb_w_out", r["p_wso"], sb_w_out, m_sb_w_out, v_sb_w_out),
                             ("hg_w_in", r["p_whi"], hg_w_in, m_hg_w_in, v_hg_w_in),
                             ("hg_w_out", r["p_who"], hg_w_out, m_hg_w_out, v_hg_w_out)):
        big[name] = [o[None] for o in reduce_adamw(p, w[0], m[0], v[0], "adamw_" + name)]

    def rows8(a):
        return a.reshape(8, HEAD_DIM)

    def rows16(a):
        return a.reshape(16, HEAD_DIM)

    small_in = [rows8(sb_norm), rows8(m_sb_norm), rows8(v_sb_norm),
                sb_q_gain, m_sb_q_gain, v_sb_q_gain,
                sb_k_gain, m_sb_k_gain, v_sb_k_gain,
                hg_norm, m_hg_norm, v_hg_norm,
                hg_o_gain, m_hg_o_gain, v_hg_o_gain,
                rows16(hg_lb_logits), rows16(m_hg_lb_logits), rows16(v_hg_lb_logits)]
    so = small_update(rows8(r["g_sbn"]), rows8(r["g_hgn"]), r["g_lb"].reshape(b, N_HEADS, HEAD_DIM),
                      r["g_qg"].reshape(b * N_HEADS, HEAD_DIM), r["g_kg"].reshape(b * N_HEADS, HEAD_DIM),
                      r["g_og"].reshape(b * N_HEADS, HEAD_DIM), r["loss_parts"], small_in)
    loss = so[0][0, 0]
    shapes = {"sb_norm": (1, D_MODEL), "sb_q_gain": (1, HEAD_DIM), "sb_k_gain": (1, HEAD_DIM),
              "hg_norm": (1, HEAD_DIM), "hg_o_gain": (1, HEAD_DIM), "hg_lb_logits": (2, D_MODEL)}
    small = {}
    for i, name in enumerate(("sb_norm", "sb_q_gain", "sb_k_gain", "hg_norm", "hg_o_gain", "hg_lb_logits")):
        small[name] = [o.reshape(shapes[name]) for o in so[1 + 4 * i:5 + 4 * i]]
    order = ("sb_norm", "sb_w_in", "sb_q_gain", "sb_k_gain", "sb_w_out",
             "hg_norm", "hg_w_in", "hg_o_gain", "hg_w_out", "hg_lb_logits")
    res = {**big, **small}
    return (loss, r["gx"]) + tuple(res[n][j] for j in range(4) for n in order)
```

```python
import functools
import math

import jax
import jax.numpy as jnp
from jax import lax
from jax.experimental import pallas as pl
from jax.experimental.pallas import tpu as pltpu

F32 = jnp.float32
BF16 = jnp.bfloat16

N_DEV = 8
D_MODEL = 1024
N_HEADS = 8
HEAD_DIM = 128
RMS_EPS = 1e-6
ATTN_BLOCK = 128
HG_CHUNK = 64
HG_SUB = 16
HG_UNROLL_FWD = 16
HG_UNROLL_BWD = 16
EXP_CLAMP = 80.0
SB_HEADS_PER_STEP = 2
SB_QBLOCKS_PER_STEP = 4
SB_GROUP_COLS = SB_HEADS_PER_STEP * 128
SB_TOP_ROWS = 32
SB_LOG_WEIGHT_FLOOR = -104.0
VMEM_LIMIT_BYTES = 48 * 1024 * 1024
W_COLS = 4 * D_MODEL // N_DEV
W_ROWS = D_MODEL // N_DEV

ADAM_LR = 0.001
ADAM_B1 = 0.9
ADAM_B2 = 0.999
ADAM_EPS = 1e-08
ADAM_WD = 0.01
ADAM_STEP = 10

NT = (((1,), (1,)), ((), ()))
TN = (((0,), (0,)), ((), ()))
NN = (((1,), (0,)), ((), ()))


def _pcall(body, *, name, **kw):
    return pl.pallas_call(body, name=name, **kw)


def _params(*sem):
    return pltpu.CompilerParams(dimension_semantics=sem, vmem_limit_bytes=VMEM_LIMIT_BYTES)


def _dot(a, b, dims=NN):
    return lax.dot_general(a, b, dims, preferred_element_type=F32)


def _dot_exact(a, m, dims=NN, left=False):
    hi = a.astype(BF16)
    lo = (a - hi.astype(F32)).astype(BF16)
    if left:
        return _dot(m, hi, dims) + _dot(m, lo, dims)
    return _dot(hi, m, dims) + _dot(lo, m, dims)


def _split(a):
    hi = a.astype(BF16)
    return hi, (a - hi.astype(F32)).astype(BF16)


def _dot3(a, b, dims=NN):
    return _dot(a[0], b[0], dims) + (_dot(a[0], b[1], dims) + _dot(a[1], b[0], dims))


def _sigmoid(x):
    return 1.0 / (1.0 + jnp.exp(-x))


def _sigmoid_pair(x):
    e = jnp.exp(-jnp.abs(x))
    big = 1.0 / (1.0 + e)
    small = e * big
    pos = x >= 0
    return jnp.where(pos, big, small), jnp.where(pos, small, big)


def _rms_scale(x):
    return lax.rsqrt(jnp.mean(x * x, axis=-1, keepdims=True) + RMS_EPS)


def _row_tile(t, want):
    return want if t % want == 0 else t


MESH = pl.DeviceIdType.MESH
N_PEERS = N_DEV - 1
_ANY = pl.BlockSpec(memory_space=pl.ANY)
_VMEM = pl.BlockSpec(memory_space=pltpu.VMEM)


def _mesh_pos():
    return lax.axis_index("x"), lax.axis_index("y"), lax.axis_index("c")


def _linear(pos):
    return 4 * pos[0] + 2 * pos[1] + pos[2]


def _peer(pos, k):
    flips = ((k + 1) >> 2 & 1, (k + 1) >> 1 & 1, (k + 1) & 1)
    return tuple(1 - p if f else p for p, f in zip(pos, flips))


def _exchange_copies(pairs, send_sems, recv_sems, local_sems, pos, landing):
    me = _linear(pos)
    local, remote = [], []
    for a, (src_of, dst) in enumerate(pairs):
        local.append(pltpu.make_async_copy(src_of(me), dst.at[me], local_sems.at[a]))
        for k in range(N_PEERS):
            peer = _peer(pos, k)
            remote.append(pltpu.make_async_remote_copy(
                src_ref=src_of(_linear(peer)), dst_ref=dst.at[_linear(peer) if landing else me],
                send_sem=send_sems.at[a, k], recv_sem=recv_sems.at[a, k], device_id=peer, device_id_type=MESH))
    return local, remote


def _exchange_start(pairs, send_sems, recv_sems, local_sems, pos):
    local, sent = _exchange_copies(pairs, send_sems, recv_sems, local_sems, pos, landing=False)
    for copy in local + sent:
        copy.start()


def _exchange_wait(pairs, send_sems, recv_sems, local_sems, pos):
    local, landed = _exchange_copies(pairs, send_sems, recv_sems, local_sems, pos, landing=True)
    for copy in landed:
        copy.wait_recv()
        copy.wait_send()
    for copy in local:
        copy.wait()


def _exchange_sems(n):
    return [pltpu.SemaphoreType.DMA((n, N_PEERS)), pltpu.SemaphoreType.DMA((n, N_PEERS)),
            pltpu.SemaphoreType.DMA((n,))]


class _Rider:
    def __init__(self, arrays, scatter):
        self.arrays = list(arrays)
        self.scatter = scatter
        self.out_shapes = [jax.ShapeDtypeStruct(a.shape if scatter else (N_DEV,) + a.shape, a.dtype)
                           for a in self.arrays]

    def pairs(self, in_refs, out_refs):
        if self.scatter:
            return [((lambda p, r=r: r.at[p]), o) for r, o in zip(in_refs, out_refs)]
        return [((lambda p, r=r: r), o) for r, o in zip(in_refs, out_refs)]


def _pcall_riding(body, rider, args, *, name, grid, in_specs, out_specs, out_shape, semantics, scratch_shapes=(),
                  input_output_aliases=None):
    aliases = input_output_aliases or {}
    if rider is None:
        outs = _pcall(body, name=name, grid=grid, in_specs=list(in_specs), out_specs=list(out_specs),
                      out_shape=list(out_shape), scratch_shapes=list(scratch_shapes), input_output_aliases=aliases,
                      compiler_params=_params(*semantics))(*args)
        return list(outs), []
    n_in, n_out, n_scr, n_r = len(in_specs), len(out_specs), len(scratch_shapes), len(rider.arrays)

    def riding(*refs):
        ins, refs = refs[:n_in], refs[n_in:]
        rider_in, refs = refs[:n_r], refs[n_r:]
        outs, refs = refs[:n_out], refs[n_out:]
        rider_out, refs = refs[:n_r], refs[n_r:]
        scratch, sems = refs[:n_scr], refs[n_scr:]
        pairs = rider.pairs(rider_in, rider_out)
        first = functools.reduce(jnp.logical_and, [pl.program_id(a) == 0 for a in range(len(grid))])
        last = functools.reduce(jnp.logical_and, [pl.program_id(a) == g - 1 for a, g in enumerate(grid)])

        @pl.when(first)
        def _():
            _exchange_start(pairs, *sems, _mesh_pos())

        body(*ins, *outs, *scratch)

        @pl.when(last)
        def _():
            _exchange_wait(pairs, *sems, _mesh_pos())

    outs = _pcall(riding, name=name, grid=grid, in_specs=list(in_specs) + [_ANY] * n_r,
                  out_specs=list(out_specs) + [_ANY] * n_r, out_shape=list(out_shape) + rider.out_shapes,
                  scratch_shapes=list(scratch_shapes) + _exchange_sems(n_r), input_output_aliases=aliases,
                  compiler_params=_params(*(("arbitrary",) * len(grid))))(*args, *rider.arrays)
    return list(outs[:n_out]), list(outs[n_out:])


def rms_inproj(x2, gain, wg, name, rider=None, qk_gains=None):
    t = x2.shape[0]
    tm = _row_tile(t, 256)
    with_qkv = qk_gains is not None

    def body(x_ref, g_ref, w_ref, *rest):
        if with_qkv:
            qg_ref, kg_ref, proj_ref, ut_ref, qkv_ref = rest
            head_gain = (qg_ref, kg_ref)
        else:
            proj_ref, ut_ref = rest
        x = x_ref[...]
        u = x * _rms_scale(x) * g_ref[...]
        ut_ref[...] = u.T.astype(BF16)
        u = u.astype(BF16)
        for p in range(N_DEV):
            part, lo = p // 2, (p % 2) * W_COLS
            res = _dot(u, w_ref[p])
            proj_ref[part, :, lo:lo + W_COLS] = res
            if with_qkv and part < 3:
                for h in range(W_COLS // HEAD_DIM):
                    y = res[:, h * HEAD_DIM:(h + 1) * HEAD_DIM]
                    if part < 2:
                        y = y * _rms_scale(y) * head_gain[part][...]
                    qkv_ref[part, :, lo + h * HEAD_DIM:lo + (h + 1) * HEAD_DIM] = y.astype(BF16)

    vec = pl.BlockSpec((1, D_MODEL), lambda i: (0, 0))
    in_specs = [pl.BlockSpec((tm, D_MODEL), lambda i: (i, 0)), vec,
                pl.BlockSpec((N_DEV, D_MODEL, W_COLS), lambda i: (0, 0, 0))]
    out_specs = [pl.BlockSpec((4, tm, D_MODEL), lambda i: (0, i, 0)), pl.BlockSpec((D_MODEL, tm), lambda i: (0, i))]
    out_shape = [jax.ShapeDtypeStruct((4, t, D_MODEL), F32), jax.ShapeDtypeStruct((D_MODEL, t), BF16)]
    args = (x2, gain, wg)
    if with_qkv:
        in_specs += [pl.BlockSpec((1, HEAD_DIM), lambda i: (0, 0))] * 2
        out_specs.append(pl.BlockSpec((3, tm, D_MODEL), lambda i: (0, i, 0)))
        out_shape.append(jax.ShapeDtypeStruct((3, t, D_MODEL), BF16))
        args += tuple(qk_gains)
    return _pcall_riding(body, rider, args, name=name, grid=(t // tm,), in_specs=in_specs, out_specs=out_specs,
                         out_shape=out_shape, semantics=("parallel",))


def gate_outproj(o2, proj, w_out, resid, target, name):
    t = o2.shape[0]
    tm = _row_tile(t, 256)
    with_loss = target is not None

    def body(o_ref, gate_ref, w_ref, r_ref, *rest):
        g = gate_ref[0]
        og = (o_ref[...] * (g * _sigmoid(g))).astype(BF16)
        h = r_ref[...] + _dot(og, w_ref[...])
        if with_loss:
            t_ref, dh_ref, loss_ref = rest
            err = h - t_ref[...]
            dh_ref[...] = err * (1.0 / D_MODEL)
            part = 0.5 * jnp.sum(jnp.mean(err * err, axis=-1, keepdims=True))
            loss_ref[...] = jnp.full(loss_ref.shape, part, F32)
        else:
            (h_ref,) = rest
            h_ref[...] = h

    row = pl.BlockSpec((tm, D_MODEL), lambda i: (i, 0))
    in_specs = [row,
                pl.BlockSpec((1, tm, D_MODEL), lambda i: (3, i, 0)),
                pl.BlockSpec((D_MODEL, D_MODEL), lambda i: (0, 0)),
                row]
    args = [o2, proj, w_out, resid]
    if with_loss:
        in_specs.append(row)
        args.append(target)
        out_specs = [row, pl.BlockSpec((1, 8, 128), lambda i: (i, 0, 0))]
        out_shape = [jax.ShapeDtypeStruct((t, D_MODEL), F32),
                     jax.ShapeDtypeStruct((t // tm, 8, 128), F32)]
    else:
        out_specs = row
        out_shape = jax.ShapeDtypeStruct((t, D_MODEL), F32)
    return _pcall(body, name=name, grid=(t // tm,), in_specs=in_specs, out_specs=out_specs,
                  out_shape=out_shape, compiler_params=_params("parallel"))(*args)


def _head_spec(s, part):
    return pl.BlockSpec((1, 1, s, HEAD_DIM), lambda b, h: (part, b, 0, h))


def _seq_spec(s):
    return pl.BlockSpec((1, s, HEAD_DIM), lambda b, h: (b, 0, h))


_GAIN_SPEC = pl.BlockSpec((1, HEAD_DIM), lambda b, h: (0, 0))
_HEAD_ROW_SPEC = pl.BlockSpec((1, 1, 1, HEAD_DIM), lambda b, h: (b, h, 0, 0))


def _sb_group_spec(s, part):
    return pl.BlockSpec((1, 1, s, SB_GROUP_COLS), lambda b, g: (part, b, 0, g))


def _sb_seq_group_spec(s):
    return pl.BlockSpec((1, s, SB_GROUP_COLS), lambda b, g: (b, 0, g))


_SB_GROUP_ROW_SPEC = pl.BlockSpec((1, SB_HEADS_PER_STEP, 1, HEAD_DIM), lambda b, g: (b, g, 0, 0))


def _sb_chains(m):
    return [(h, m * SB_QBLOCKS_PER_STEP + r) for h in range(SB_HEADS_PER_STEP) for r in range(SB_QBLOCKS_PER_STEP)]


def _sb_logits(qi, kj):
    return _dot(qi, kj, NT) * (HEAD_DIM ** -0.5)


def _sb_scores(z, diag, live, tri_lt):
    soft = jnp.log(1.0 + jnp.exp(-jnp.abs(z)))
    valid = jnp.logical_and(live, jnp.logical_or(jnp.logical_not(diag), tri_lt))
    log_skip = jnp.where(valid, -(jnp.maximum(z, 0.0) + soft), 0.0)
    log_beta = jnp.minimum(z, 0.0) - soft
    return log_skip, log_beta, valid


def _sb_keys_left(chains, watch, state):
    done, carries = state[0], state[1]
    worst = None
    for (_, i), c in zip(chains, carries):
        c = jnp.where(done <= i, c[watch], -jnp.inf)
        worst = c if worst is None else jnp.maximum(worst, c)
    return jnp.logical_and(done <= chains[-1][1],
                           jnp.logical_or(done == 0, jnp.max(worst) > SB_LOG_WEIGHT_FLOOR))


def _sb_key_rows(i, done):
    j = i - done
    return pl.ds(pl.multiple_of(jnp.maximum(j, 0) * ATTN_BLOCK, ATTN_BLOCK), ATTN_BLOCK), j >= 0


def _sb_head(ref, h, rows):
    return ref[0, 0, rows, h * HEAD_DIM:(h + 1) * HEAD_DIM]


def _sb_rows(i, offset, count):
    return pl.ds(pl.multiple_of(jnp.maximum(i, 0) * ATTN_BLOCK + offset, 8), count)


def sb_attn_fwd(qkv4, rider=None):
    _, b, s, _ = qkv4.shape
    blk, top = ATTN_BLOCK, SB_TOP_ROWS
    ngroups = s // blk // SB_QBLOCKS_PER_STEP
    assert ngroups * SB_QBLOCKS_PER_STEP * blk == s

    def body(q_ref, k_ref, v_ref, o_ref):
        row = lax.broadcasted_iota(jnp.int32, (blk, blk), 0)
        col = lax.broadcasted_iota(jnp.int32, (blk, blk), 1)
        tri_lt = col < row
        suffix = (row > col).astype(BF16)

        def step(items):
            where = [_sb_key_rows(i, done) for _, i, done, _, _, _, _ in items]
            zs = [_sb_logits(q, _sb_head(k_ref, h, rows)) for (h, _, _, q, _, _, _), (rows, _) in zip(items, where)]
            scored = [_sb_scores(z, done == 0, live, mask)
                      for z, (_, _, done, _, mask, _, _), (_, live) in zip(zs, items, where)]
            afters = [_dot_exact(log_skip, suffix) for log_skip, _, _ in scored]
            ws = [jnp.where(valid, jnp.exp(log_beta + after + c), 0.0).astype(BF16)
                  for (_, log_beta, valid), after, (_, _, _, _, _, c, _) in zip(scored, afters, items)]
            accs = [acc + _dot(w, _sb_head(v_ref, h, rows))
                    for (h, _, _, _, _, _, acc), (rows, _), w in zip(items, where, ws)]
            cs = [c + jnp.sum(log_skip, axis=1, keepdims=True)
                  for (log_skip, _, _), (_, _, _, _, _, c, _) in zip(scored, items)]
            return cs, accs

        def top_items(chains, done, cs, accs):
            return [(h, i, done, _sb_head(q_ref, h, _sb_rows(i, 0, top)), tri_lt[:top, :], c, acc)
                    for (h, i), c, acc in zip(chains, cs, accs)]

        def finish_tops(chains, state):
            def k_step(state):
                done, cs, accs = state
                cs, accs = step(top_items(chains, done, cs, accs))
                return done + 1, tuple(cs), tuple(accs)

            _, _, accs = lax.while_loop(functools.partial(_sb_keys_left, chains, slice(0, top)), k_step, state)
            for (h, i), acc in zip(chains, accs):
                o_ref[0, _sb_rows(i, 0, top), h * HEAD_DIM:(h + 1) * HEAD_DIM] = acc

        def q_group(m, before):
            chains, chains_before = _sb_chains(m), _sb_chains(m - 1)
            qis = [_sb_head(q_ref, h, _sb_rows(i, 0, blk)) for h, i in chains]
            n = len(chains)

            def items_of(done, cs, accs):
                return [(h, i, done, q, tri_lt, c, acc) for (h, i), q, c, acc in zip(chains, qis, cs, accs)]

            def k_step(state):
                done, cs, accs = state
                cs, accs = step(items_of(done, cs, accs))
                return done + 1, tuple(cs), tuple(accs)

            done_b, cs_b, accs_b = before
            cs0, accs0 = step(items_of(jnp.int32(0), (jnp.zeros((blk, 1), F32),) * n,
                                       (jnp.zeros((blk, HEAD_DIM), F32),) * n)
                              + top_items(chains_before, done_b, cs_b, accs_b))
            done, cs, accs = lax.while_loop(functools.partial(_sb_keys_left, chains, slice(top, blk)), k_step,
                                            (jnp.int32(1), tuple(cs0[:n]), tuple(accs0[:n])))

            @pl.when(m > 0)
            def _():
                finish_tops(chains_before, (done_b + 1, tuple(cs0[n:]), tuple(accs0[n:])))

            for (h, i), acc in zip(chains, accs):
                o_ref[0, _sb_rows(i, top, blk - top), h * HEAD_DIM:(h + 1) * HEAD_DIM] = acc[top:, :]
            return done, tuple(c[:top, :] for c in cs), tuple(acc[:top, :] for acc in accs)

        n = SB_HEADS_PER_STEP * SB_QBLOCKS_PER_STEP
        nothing = (jnp.int32(0), (jnp.zeros((top, 1), F32),) * n, (jnp.zeros((top, HEAD_DIM), F32),) * n)
        last = lax.fori_loop(0, ngroups, q_group, nothing)
        finish_tops(_sb_chains(ngroups - 1), last)

    (o,), extra = _pcall_riding(
        body, rider, (qkv4, qkv4, qkv4),
        name="sb_attn_fwd", grid=(b, N_HEADS // SB_HEADS_PER_STEP),
        in_specs=[_sb_group_spec(s, 0), _sb_group_spec(s, 1), _sb_group_spec(s, 2)],
        out_specs=[_sb_seq_group_spec(s)],
        out_shape=[jax.ShapeDtypeStruct((b, s, D_MODEL), F32)],
        semantics=("parallel", "parallel"))
    return o, extra


def _hg_masks():
    c = HG_CHUNK
    row = lax.broadcasted_iota(jnp.int32, (c, c), 0)
    col = lax.broadcasted_iota(jnp.int32, (c, c), 1)
    incl = (col <= row)
    lower = incl.astype(BF16)
    before_sub = (col < (row // HG_SUB) * HG_SUB).astype(BF16)
    upper = (col >= row).astype(BF16)
    return incl, lower, before_sub, upper


def _hg_lower_bound(lbl_ref):
    l0 = lbl_ref[0, 0]
    l1 = lbl_ref[1, 0]
    d = l1 - l0
    return _sigmoid_pair(d)


def _hg_gates(qp, fp, lb, oml):
    sq = _sigmoid(qp)
    sf, sfn = _sigmoid_pair(fp)
    f = lb + oml * sf
    return dict(qp=qp, sq=sq, q=qp * sq, sf=sf, sfn=sfn, f=f, k=oml * sfn, logf=jnp.log(f))


def _hg_intra(qds, ks, gcs, grs, incl):
    subs = range(HG_CHUNK // HG_SUB)
    qdbs = [qd.astype(BF16) for qd in qds]
    ess = [[jnp.exp(jnp.minimum(gr[sub * HG_SUB:sub * HG_SUB + 1, :] - gc, EXP_CLAMP)) for sub in subs]
           for gc, gr in zip(gcs, grs)]
    ksbs = [[(k * e).astype(BF16) for e in es] for k, es in zip(ks, ess)]
    rows = [[_dot(qdb[sub * HG_SUB:(sub + 1) * HG_SUB, :], ksb[sub], NT) for sub in subs]
            for qdb, ksb in zip(qdbs, ksbs)]
    a_s = [jnp.where(incl, jnp.concatenate(r, axis=0), 0.0) for r in rows]
    return a_s, qdbs, ksbs, ess


def _hg_group_rows(outer, unroll):
    ns = [outer * unroll + u for u in range(unroll)]
    return ns, [pl.ds(pl.multiple_of(n * HG_CHUNK, HG_CHUNK), HG_CHUNK) for n in ns]


def _state_spec(nchunk):
    return pl.BlockSpec((1, 1, nchunk, HEAD_DIM, HEAD_DIM), lambda b, h: (b, h, 0, 0, 0))


def hgrn2_fwd(proj4, lbl4, o_gain):
    _, b, s, _ = proj4.shape
    nchunk = s // HG_CHUNK
    c = HG_CHUNK
    unroll = math.gcd(nchunk, HG_UNROLL_FWD)

    def body(q_ref, f_ref, i_ref, lbl_ref, og_ref, o_ref, oraw_ref, st_ref):
        incl, lower, before_sub, _ = _hg_masks()
        lb, oml = _hg_lower_bound(lbl_ref)

        def group(outer, st):
            ns, rows = _hg_group_rows(outer, unroll)
            vs = [_hg_gates(q_ref[0, 0, r, :], f_ref[0, 0, r, :], lb, oml) for r in rows]
            inps = [i_ref[0, 0, r, :].astype(BF16) for r in rows]
            gcs = [_dot_exact(v["logf"], lower, left=True) for v in vs]
            grs = [_dot_exact(v["logf"], before_sub, left=True) for v in vs]
            a_s, _, _, _ = _hg_intra([v["q"] * jnp.exp(gc - gr) for v, gc, gr in zip(vs, gcs, grs)],
                                     [v["k"] for v in vs], gcs, grs, incl)
            gls = [gc[c - 1:c, :] for gc in gcs]
            adds = [_dot(inp, (v["k"] * jnp.exp(gl - gc)).astype(BF16), TN)
                    for inp, v, gl, gc in zip(inps, vs, gls, gcs)]
            o_intra = [_dot(a.astype(BF16), inp) for a, inp in zip(a_s, inps)]
            sts = []
            for gl, add in zip(gls, adds):
                sts.append(st)
                st = st * jnp.exp(gl) + add
            outs = [oi + _dot((v["q"] * jnp.exp(gc)).astype(BF16), s0.astype(BF16), NT)
                    for oi, v, gc, s0 in zip(o_intra, vs, gcs, sts)]
            for n, r, s0, o in zip(ns, rows, sts, outs):
                st_ref[0, 0, n] = s0
                oraw_ref[0, r, :] = o
                o_ref[0, r, :] = o * _rms_scale(o) * og_ref[...]
            return st

        lax.fori_loop(0, nchunk // unroll, group, jnp.zeros((HEAD_DIM, HEAD_DIM), F32))

    seq = jax.ShapeDtypeStruct((b, s, D_MODEL), F32)
    return _pcall(
        body, name="hgrn2_fwd", grid=(b, N_HEADS),
        in_specs=[_head_spec(s, 0), _head_spec(s, 1), _head_spec(s, 2),
                  pl.BlockSpec((2, 1, 1, HEAD_DIM), lambda b, h: (0, h, 0, 0)), _GAIN_SPEC],
        out_specs=[_seq_spec(s), _seq_spec(s), _state_spec(nchunk)],
        out_shape=[seq, seq, jax.ShapeDtypeStruct((b, N_HEADS, nchunk, HEAD_DIM, HEAD_DIM), F32)],
        compiler_params=_params("parallel", "parallel"),
    )(proj4, proj4, proj4, lbl4, o_gain)


def outproj_bwd(dh, w_out, o2, proj, name):
    t = dh.shape[0]
    tm = _row_tile(t, 256)

    def body(dh_ref, w_ref, o_ref, gate_ref, do_ref, dproj_ref, dw_ref):
        dhb = dh_ref[...].astype(BF16)
        dog = _dot(dhb, w_ref[...], NT)
        g = gate_ref[0]
        sg = _sigmoid(g)
        silu = g * sg
        o = o_ref[...]
        do_ref[...] = dog * silu
        dproj_ref[0] = dog * o * (sg * (1.0 + g * (1.0 - sg)))
        part = _dot((o * silu).astype(BF16), dhb, TN)

        @pl.when(pl.program_id(0) == 0)
        def _():
            dw_ref[...] = part

        @pl.when(pl.program_id(0) > 0)
        def _():
            dw_ref[...] += part

    row = pl.BlockSpec((tm, D_MODEL), lambda i: (i, 0))
    full = pl.BlockSpec((D_MODEL, D_MODEL), lambda i: (0, 0))
    return _pcall(
        body, name=name, grid=(t // tm,),
        in_specs=[row, full, row, pl.BlockSpec((1, tm, D_MODEL), lambda i: (3, i, 0))],
        out_specs=[row, pl.BlockSpec((1, tm, D_MODEL), lambda i: (3, i, 0)), full],
        out_shape=[jax.ShapeDtypeStruct((t, D_MODEL), F32),
                   jax.ShapeDtypeStruct((4, t, D_MODEL), F32),
                   jax.ShapeDtypeStruct((D_MODEL, D_MODEL), F32)],
        compiler_params=_params("arbitrary"),
    )(dh, w_out, o2, proj)


def inproj_bwd_dx(dproj, wg, x2, gain, dres, name, rider=None):
    t = x2.shape[0]
    tm = _row_tile(t, 256)

    def body(d_ref, w_ref, x_ref, g_ref, r_ref, dx_ref, dg_ref):
        du = jnp.zeros((tm, D_MODEL), F32)
        for p in range(N_DEV):
            cols = slice((p % 2) * W_COLS, (p % 2 + 1) * W_COLS)
            du = du + _dot(d_ref[p // 2, :, cols].astype(BF16), w_ref[p], NT)
        x = x_ref[...]
        r = _rms_scale(x)
        xh = x * r
        a = du * g_ref[...]
        dx_ref[...] = r_ref[...] + r * (a - xh * jnp.mean(a * xh, axis=-1, keepdims=True))
        part = jnp.sum(du * xh, axis=0, keepdims=True)

        @pl.when(pl.program_id(0) == 0)
        def _():
            dg_ref[...] = part

        @pl.when(pl.program_id(0) > 0)
        def _():
            dg_ref[...] += part

    row = pl.BlockSpec((tm, D_MODEL), lambda i: (i, 0))
    vec = pl.BlockSpec((1, D_MODEL), lambda i: (0, 0))
    return _pcall_riding(
        body, rider, (dproj, wg, x2, gain, dres), name=name, grid=(t // tm,),
        in_specs=[pl.BlockSpec((4, tm, D_MODEL), lambda i: (0, i, 0)),
                  pl.BlockSpec((N_DEV, D_MODEL, W_COLS), lambda i: (0, 0, 0)),
                  row, vec, row],
        out_specs=[row, vec],
        out_shape=[jax.ShapeDtypeStruct((t, D_MODEL), F32), jax.ShapeDtypeStruct((1, D_MODEL), F32)],
        semantics=("arbitrary",))


def inproj_bwd_dw(ut, dproj, name, out_dtype):
    t = ut.shape[1]

    def body(ut_ref, d_ref, dw_ref):
        dw_ref[0] = _dot(ut_ref[...], d_ref[0].astype(BF16)).astype(dw_ref.dtype)

    return _pcall(
        body, name=name, grid=(N_DEV,),
        in_specs=[pl.BlockSpec((D_MODEL, t), lambda j: (0, 0)),
                  pl.BlockSpec((1, t, W_COLS), lambda j: (j // 2, 0, j % 2))],
        out_specs=pl.BlockSpec((1, D_MODEL, W_COLS), lambda j: (j, 0, 0)),
        out_shape=jax.ShapeDtypeStruct((N_DEV, D_MODEL, W_COLS), out_dtype),
        compiler_params=_params("parallel"),
    )(ut, dproj)


def _rms_bwd(x, gain, dy):
    r = _rms_scale(x)
    xh = x * r
    a = dy * gain
    return r * (a - xh * jnp.mean(a * xh, axis=-1, keepdims=True)), dy * xh


def sb_attn_bwd(qkv4, proj4, do3, o3, q_gain, k_gain, dproj4, rider=None):
    _, b, s, _ = proj4.shape
    blk, top = ATTN_BLOCK, SB_TOP_ROWS
    nq = s // blk
    ngroups = nq // SB_QBLOCKS_PER_STEP
    assert ngroups * SB_QBLOCKS_PER_STEP * blk == s
    scale = HEAD_DIM ** -0.5

    def body(qn_ref, kn_ref, v_ref, q_ref, k_ref, do_ref, o_ref, qg_ref, kg_ref, _alias, d_ref, dqg_ref, dkg_ref, dob):
        for h in range(SB_HEADS_PER_STEP):
            dob[h] = do_ref[0, :, h * HEAD_DIM:(h + 1) * HEAD_DIM].astype(BF16)
        d_ref[...] = jnp.zeros_like(d_ref)
        row = lax.broadcasted_iota(jnp.int32, (blk, blk), 0)
        col = lax.broadcasted_iota(jnp.int32, (blk, blk), 1)
        tri_lt = col < row
        suffix = (row > col).astype(BF16)
        suffix_incl = (row >= col).astype(BF16)

        def step(items):
            heads = [it[0] for it in items]
            where = [_sb_key_rows(it[1], it[2]) for it in items]
            kjs = [_sb_head(kn_ref, h, rows) for h, (rows, _) in zip(heads, where)]
            zs = [_sb_logits(it[3], kj) for it, kj in zip(items, kjs)]
            dws = [_dot(it[4], _sb_head(v_ref, h, rows), NT) for it, h, (rows, _) in zip(items, heads, where)]
            scored = [_sb_scores(z, it[2] == 0, live, it[6]) for z, it, (_, live) in zip(zs, items, where)]
            afters = [_dot_exact(log_skip, suffix) for log_skip, _, _ in scored]
            wbs = [jnp.where(valid, jnp.exp(log_beta + after + it[7]), 0.0).astype(BF16)
                   for (_, log_beta, valid), after, it in zip(scored, afters, items)]
            gs = [dw * wb.astype(F32) for dw, wb in zip(dws, wbs)]
            befores = [it[5] - (_dot_exact(g, suffix_incl) + it[8]) for g, it in zip(gs, items)]
            dzbs = [jnp.where(valid, g - jnp.exp(log_beta) * (g + before), 0.0).astype(BF16)
                    for (_, log_beta, valid), g, before in zip(scored, gs, befores)]
            dqs = [it[9] + _dot(dzb, kj) for it, dzb, kj in zip(items, dzbs, kjs)]
            for it, h, (rows, _), wb, dzb in zip(items, heads, where, wbs, dzbs):
                cols = slice(h * HEAD_DIM, (h + 1) * HEAD_DIM)
                d_ref[2, 0, rows, cols] += _dot(wb, it[4], TN)
                d_ref[1, 0, rows, cols] += _dot(dzb, it[3], TN)
            cs = [it[7] + jnp.sum(log_skip, axis=1, keepdims=True) for (log_skip, _, _), it in zip(scored, items)]
            cgs = [it[8] + jnp.sum(g, axis=1, keepdims=True) for g, it in zip(gs, items)]
            return cs, cgs, dqs

        def top_items(chains, deltas, done, cs, cgs, dqs):
            return [(h, i, done, _sb_head(qn_ref, h, _sb_rows(i, 0, top)), dob[h, _sb_rows(i, 0, top), :], delta,
                     tri_lt[:top, :], c, cg, dq)
                    for (h, i), delta, c, cg, dq in zip(chains, deltas, cs, cgs, dqs)]

        def finish_tops(chains, deltas, state):
            def k_step(state):
                done, cs, cgs, dqs = state
                cs, cgs, dqs = step(top_items(chains, deltas, done, cs, cgs, dqs))
                return done + 1, tuple(cs), tuple(cgs), tuple(dqs)

            _, _, _, dqs = lax.while_loop(functools.partial(_sb_keys_left, chains, slice(0, top)), k_step, state)
            for (h, i), dq in zip(chains, dqs):
                d_ref[0, 0, _sb_rows(i, 0, top), h * HEAD_DIM:(h + 1) * HEAD_DIM] = dq * scale

        def q_group(m, before):
            chains, chains_before = _sb_chains(m), _sb_chains(m - 1)
            deltas_b, before = before[0], before[1:]
            qis, dois, deltas = [], [], []
            for h, i in chains:
                rows_i = _sb_rows(i, 0, blk)
                qis.append(_sb_head(qn_ref, h, rows_i))
                dois.append(dob[h, rows_i, :])
                deltas.append(jnp.sum(dois[-1].astype(F32) * o_ref[0, rows_i, h * HEAD_DIM:(h + 1) * HEAD_DIM],
                                      axis=1, keepdims=True))
            n = len(chains)

            def items_of(done, cs, cgs, dqs):
                return [(h, i, done, q, do, delta, tri_lt, c, cg, dq)
                        for (h, i), q, do, delta, c, cg, dq in zip(chains, qis, dois, deltas, cs, cgs, dqs)]

            def k_step(state):
                done, cs, cgs, dqs = state
                cs, cgs, dqs = step(items_of(done, cs, cgs, dqs))
                return done + 1, tuple(cs), tuple(cgs), tuple(dqs)

            done_b, cs_b, cgs_b, dqs_b = before
            zero = (jnp.zeros((blk, 1), F32),) * n
            new = step(items_of(jnp.int32(0), zero, zero, (jnp.zeros((blk, HEAD_DIM), F32),) * n)
                       + top_items(chains_before, deltas_b, done_b, cs_b, cgs_b, dqs_b))
            done, cs, cgs, dqs = lax.while_loop(functools.partial(_sb_keys_left, chains, slice(top, blk)), k_step,
                                                (jnp.int32(1),) + tuple(tuple(x[:n]) for x in new))

            @pl.when(m > 0)
            def _():
                finish_tops(chains_before, deltas_b, (done_b + 1,) + tuple(tuple(x[n:]) for x in new))

            for (h, i), dq in zip(chains, dqs):
                d_ref[0, 0, _sb_rows(i, top, blk - top), h * HEAD_DIM:(h + 1) * HEAD_DIM] = dq[top:, :] * scale
            first = lambda xs: tuple(x[:top, :] for x in xs)
            return first(deltas), done, first(cs), first(cgs), first(dqs)

        n = SB_HEADS_PER_STEP * SB_QBLOCKS_PER_STEP
        zero = (jnp.zeros((top, 1), F32),) * n
        nothing = (zero, jnp.int32(0), zero, zero, (jnp.zeros((top, HEAD_DIM), F32),) * n)
        last = lax.fori_loop(0, ngroups, q_group, nothing)
        finish_tops(_sb_chains(ngroups - 1), last[0], last[1:])

        def norm_block(i, carry):
            rows = pl.ds(pl.multiple_of(i * blk, blk), blk)
            out = []
            for h in range(SB_HEADS_PER_STEP):
                cols = slice(h * HEAD_DIM, (h + 1) * HEAD_DIM)
                for part, src_ref, gain_ref in ((0, q_ref, qg_ref), (1, k_ref, kg_ref)):
                    dy = d_ref[part, 0, rows, cols] * (scale if part == 1 else 1.0)
                    dx, pg = _rms_bwd(src_ref[0, 0, rows, cols], gain_ref[...], dy)
                    d_ref[part, 0, rows, cols] = dx
                    out.append(carry[len(out)] + jnp.sum(pg, axis=0, keepdims=True))
            return tuple(out)

        sums = lax.fori_loop(0, nq, norm_block, (jnp.zeros((1, HEAD_DIM), F32),) * (2 * SB_HEADS_PER_STEP))
        for h in range(SB_HEADS_PER_STEP):
            dqg_ref[0, h] = sums[2 * h]
            dkg_ref[0, h] = sums[2 * h + 1]

    head_row = jax.ShapeDtypeStruct((b, N_HEADS, 1, HEAD_DIM), F32)
    return _pcall_riding(
        body, rider, (qkv4, qkv4, qkv4, proj4, proj4, do3, o3, q_gain, k_gain, dproj4),
        name="sb_attn_bwd", grid=(b, N_HEADS // SB_HEADS_PER_STEP),
        in_specs=[_sb_group_spec(s, 0), _sb_group_spec(s, 1), _sb_group_spec(s, 2),
                  _sb_group_spec(s, 0), _sb_group_spec(s, 1),
                  _sb_seq_group_spec(s), _sb_seq_group_spec(s), _GAIN_SPEC, _GAIN_SPEC,
                  pl.BlockSpec(memory_space=pl.ANY)],
        out_specs=[pl.BlockSpec((3, 1, s, SB_GROUP_COLS), lambda b, g: (0, b, 0, g)),
                   _SB_GROUP_ROW_SPEC, _SB_GROUP_ROW_SPEC],
        out_shape=[jax.ShapeDtypeStruct(dproj4.shape, F32), head_row, head_row],
        scratch_shapes=[pltpu.VMEM((SB_HEADS_PER_STEP, s, HEAD_DIM), BF16)],
        input_output_aliases={9: 0}, semantics=("parallel", "parallel"))


def hgrn2_bwd(proj4, don3, oraw3, states, lbl4, o_gain, dproj4, rider=None):
    _, b, s, _ = proj4.shape
    nchunk = s // HG_CHUNK
    c = HG_CHUNK
    subs = range(HG_CHUNK // HG_SUB)
    unroll = math.gcd(nchunk, HG_UNROLL_BWD)
    ngroup = nchunk // unroll

    def body(q_ref, f_ref, i_ref, don_ref, oraw_ref, st_ref, lbl_ref, og_ref, _alias, d_ref, dog_ref, dlb_ref):
        incl, lower, before_sub, upper = _hg_masks()
        lb, oml = _hg_lower_bound(lbl_ref)
        last_row = lax.broadcasted_iota(jnp.int32, (c, HEAD_DIM), 0) == c - 1

        def group(m, carry):
            dst, dog_acc, dlb_acc = carry
            ns, rows = _hg_group_rows(ngroup - 1 - m, unroll)
            ns, rows = ns[::-1], rows[::-1]
            vs = [_hg_gates(q_ref[0, 0, r, :], f_ref[0, 0, r, :], lb, oml) for r in rows]
            inps = [i_ref[0, 0, r, :].astype(BF16) for r in rows]
            sts = [st_ref[0, 0, n] for n in ns]
            gcs = [_dot_exact(v["logf"], lower, left=True) for v in vs]
            grs = [_dot_exact(v["logf"], before_sub, left=True) for v in vs]
            e_qs = [jnp.exp(gc - gr) for gc, gr in zip(gcs, grs)]
            a_s, qdbs, ksbs, ess = _hg_intra([v["q"] * e for v, e in zip(vs, e_qs)], [v["k"] for v in vs],
                                             gcs, grs, incl)
            e_gcs = [jnp.exp(gc) for gc in gcs]
            gls = [gc[c - 1:c, :] for gc in gcs]
            e_gls = [jnp.exp(gl) for gl in gls]
            e_ks = [jnp.exp(gl - gc) for gl, gc in zip(gls, gcs)]
            normed = [_rms_bwd(oraw_ref[0, r, :], og_ref[...], don_ref[0, r, :]) for r in rows]
            dobs = [do.astype(BF16) for do, _ in normed]
            dabs = [jnp.where(incl, _dot(dob, inp, NT), 0.0).astype(BF16) for dob, inp in zip(dobs, inps)]
            adds = [_dot(dob, (v["q"] * e).astype(BF16), TN) for dob, v, e in zip(dobs, vs, e_gcs)]
            dq_inters = [_dot(dob, st.astype(BF16)) * e for dob, st, e in zip(dobs, sts, e_gcs)]
            dqds = [jnp.concatenate([_dot(dab[sub * HG_SUB:(sub + 1) * HG_SUB, :], ksb[sub]) for sub in subs], axis=0)
                    for dab, ksb in zip(dabs, ksbs)]
            dkss = [[_dot(dab[sub * HG_SUB:(sub + 1) * HG_SUB, :], qdb[sub * HG_SUB:(sub + 1) * HG_SUB, :], TN)
                     for sub in subs] for dab, qdb in zip(dabs, qdbs)]
            dsts = []
            for e_gl, add in zip(e_gls, adds):
                dsts.append(dst)
                dst = dst * e_gl + add
            dstbs = [d.astype(BF16) for d in dsts]
            dis = [_dot(a.astype(BF16), dob, TN) + _dot((v["k"] * e_k).astype(BF16), dstb, NT)
                   for a, dob, v, e_k, dstb in zip(a_s, dobs, vs, e_ks, dstbs)]
            dk_inters = [_dot(inp, dstb) * e_k for inp, dstb, e_k in zip(inps, dstbs, e_ks)]
            for u, r in enumerate(rows):
                v, q, k = vs[u], vs[u]["q"], vs[u]["k"]
                dk, dgc_k = dk_inters[u], jnp.zeros((c, HEAD_DIM), F32)
                for sub in subs:
                    dk = dk + dkss[u][sub] * ess[u][sub]
                    dgc_k = dgc_k + dkss[u][sub] * ksbs[u][sub].astype(F32)
                dq = dqds[u] * e_qs[u] + dq_inters[u]
                at_last = (jnp.sum(k * dk_inters[u], axis=0, keepdims=True)
                           + e_gls[u] * jnp.sum(sts[u] * dsts[u], axis=0, keepdims=True))
                dgc = ((qdbs[u].astype(F32) * dqds[u] - dgc_k) + (q * dq_inters[u] - k * dk_inters[u])
                       + jnp.where(last_row, at_last, 0.0))
                dlf_f = _dot_exact(dgc, upper, left=True) / v["f"]
                d_ref[0, 0, r, :] = dq * (v["sq"] * (1.0 + v["qp"] * (1.0 - v["sq"])))
                d_ref[1, 0, r, :] = (dlf_f - dk) * (oml * v["sf"] * v["sfn"])
                d_ref[2, 0, r, :] = dis[u]
                dlb_acc = dlb_acc + jnp.sum((dlf_f - dk) * v["sfn"], axis=0, keepdims=True)
                dog_acc = dog_acc + jnp.sum(normed[u][1], axis=0, keepdims=True)
            return dst, dog_acc, dlb_acc

        zero = jnp.zeros((1, HEAD_DIM), F32)
        _, dog, dlb = lax.fori_loop(0, ngroup, group, (jnp.zeros((HEAD_DIM, HEAD_DIM), F32), zero, zero))
        dog_ref[0, 0] = dog
        dlb_ref[0, 0] = dlb

    head_row = jax.ShapeDtypeStruct((b, N_HEADS, 1, HEAD_DIM), F32)
    return _pcall_riding(
        body, rider, (proj4, proj4, proj4, don3, oraw3, states, lbl4, o_gain, dproj4),
        name="hgrn2_bwd", grid=(b, N_HEADS),
        in_specs=[_head_spec(s, 0), _head_spec(s, 1), _head_spec(s, 2), _seq_spec(s), _seq_spec(s),
                  _state_spec(nchunk), pl.BlockSpec((2, 1, 1, HEAD_DIM), lambda b, h: (0, h, 0, 0)), _GAIN_SPEC,
                  pl.BlockSpec(memory_space=pl.ANY)],
        out_specs=[pl.BlockSpec((3, 1, s, HEAD_DIM), lambda b, h: (0, b, 0, h)), _HEAD_ROW_SPEC, _HEAD_ROW_SPEC],
        out_shape=[jax.ShapeDtypeStruct(dproj4.shape, F32), head_row, head_row],
        input_output_aliases={8: 0}, semantics=("parallel", "parallel"))


def local_step(x, target, sb_norm, wsi, sb_q_gain, sb_k_gain, hg_o_gain, hg_lb_logits, wso_mine, whi_mine, who_mine,
               hg_norm_mine):
    b, s, _ = x.shape
    t = b * s
    x2 = x.reshape(t, D_MODEL)
    tg2 = target.reshape(t, D_MODEL)
    lbl4 = hg_lb_logits.reshape(2, N_HEADS, 1, HEAD_DIM)
    four = (4, b, s, D_MODEL)
    three = (b, s, D_MODEL)
    rows8 = (N_DEV, W_ROWS, D_MODEL)

    (proj0, u0, qkv0), (wso, who) = rms_inproj(x2, sb_norm, wsi, "sb_inproj",
                                               _Rider([wso_mine, who_mine], scatter=False),
                                               qk_gains=(sb_q_gain, sb_k_gain))
    qkv0 = qkv0.reshape(3, b, s, D_MODEL)
    wso = wso.reshape(D_MODEL, D_MODEL)
    who = who.reshape(D_MODEL, D_MODEL)
    o0, (whi, hgn) = sb_attn_fwd(qkv0, _Rider([whi_mine, hg_norm_mine], scatter=False))
    hg_norm_full = hgn[:, 0, :].reshape(1, D_MODEL)
    o0 = o0.reshape(t, D_MODEL)
    h1 = gate_outproj(o0, proj0, wso, x2, None, "sb_outproj")
    (proj1, u1), _ = rms_inproj(h1, hg_norm_full, whi, "hg_inproj")
    o1, o1_raw, states = hgrn2_fwd(proj1.reshape(four), lbl4, hg_o_gain)
    o1 = o1.reshape(t, D_MODEL)
    dh2, loss_parts = gate_outproj(o1, proj1, who, h1, tg2, "hg_outproj_loss")

    do1, dproj1, g_who = outproj_bwd(dh2, who, o1, proj1, "hg_outproj_bwd")
    (dproj1, g_og, g_lb), (p_who,) = hgrn2_bwd(proj1.reshape(four), do1.reshape(three), o1_raw, states, lbl4,
                                               hg_o_gain, dproj1.reshape(four),
                                               _Rider([g_who.reshape(rows8)], scatter=True))
    dproj1 = dproj1.reshape(4, t, D_MODEL)
    (dh1, g_hgn), _ = inproj_bwd_dx(dproj1, whi, h1, hg_norm_full, dh2, "hg_inproj_bwd_dx")
    g_whi = inproj_bwd_dw(u1, dproj1, "hg_inproj_bwd_dw", out_dtype=BF16)

    do0, dproj0, g_wso = outproj_bwd(dh1, wso, o0, proj0, "sb_outproj_bwd")
    (dproj0, g_qg, g_kg), (p_whi, p_wso) = sb_attn_bwd(qkv0, proj0.reshape(four), do0.reshape(three), o0.reshape(three),
                                                       sb_q_gain, sb_k_gain, dproj0.reshape(four),
                                                       _Rider([g_whi, g_wso.reshape(rows8)], scatter=True))
    dproj0 = dproj0.reshape(4, t, D_MODEL)
    g_wsi = inproj_bwd_dw(u0, dproj0, "sb_inproj_bwd_dw", out_dtype=BF16)
    (gx, g_sbn), (p_wsi,) = inproj_bwd_dx(dproj0, wsi, x2, sb_norm, dh1, "sb_inproj_bwd_dx",
                                          _Rider([g_wsi], scatter=True))
    return dict(loss_parts=loss_parts, gx=gx.reshape(three), p_wsi=p_wsi, p_wso=p_wso, p_whi=p_whi, p_who=p_who,
                g_sbn=g_sbn, g_hgn=g_hgn, g_qg=g_qg, g_kg=g_kg, g_og=g_og, g_lb=g_lb)


def _two_level_gather(src, out, send_sems, recv_sems, local_sem, pos):
    x, y, c = pos
    me, sibling = (x, y, c), (x, y, 1 - c)
    chips = [(1 - x, y), (x, 1 - y), (1 - x, 1 - y)]

    def copy(k, block, to, source=None):
        slot = out.at[_linear(block)]
        return pltpu.make_async_remote_copy(
            src_ref=slot if source is None else source, dst_ref=slot, send_sem=send_sems.at[k],
            recv_sem=recv_sems.at[k], device_id=to, device_id_type=MESH)

    mine = pltpu.make_async_copy(src, out.at[_linear(me)], local_sem)
    mine.start()
    first = [copy(0, me, sibling, src)] + [copy(1 + j, me, (*chip, c), src) for j, chip in enumerate(chips)]
    for cp in first:
        cp.start()
    passed = [copy(4 + j, (*chip, c), sibling) for j, chip in enumerate(chips)]
    for j, chip in enumerate(chips):
        copy(1 + j, (*chip, c), me).wait_recv()
        passed[j].start()
    copy(0, sibling, me).wait_recv()
    for j, chip in enumerate(chips):
        copy(4 + j, (*chip, 1 - c), me).wait_recv()
    for cp in first + passed:
        cp.wait_send()
    mine.wait()


def gather_first_weights(w_si, w_so, w_hi, w_ho, hg_norm):
    def body(si_ref, so_ref, hi_ref, ho_ref, hn_ref, o_si, so_b, hi_b, ho_b, hn_b, si_b, send_sems, recv_sems, local_sem):
        for src, buf in ((si_ref, si_b), (so_ref, so_b), (hi_ref, hi_b), (ho_ref, ho_b)):
            buf[...] = src[...].astype(BF16)
        hn_b[...] = jnp.broadcast_to(hn_ref[...], hn_b.shape)
        _two_level_gather(si_b, o_si, send_sems, recv_sems, local_sem, _mesh_pos())

    return _pcall(
        body, name="gather_first_weights",
        in_specs=[_VMEM] * 5, out_specs=[_ANY] + [_VMEM] * 4,
        out_shape=[jax.ShapeDtypeStruct((N_DEV,) + w_si.shape, BF16), jax.ShapeDtypeStruct(w_so.shape, BF16),
                   jax.ShapeDtypeStruct(w_hi.shape, BF16), jax.ShapeDtypeStruct(w_ho.shape, BF16),
                   jax.ShapeDtypeStruct((8, HEAD_DIM), F32)],
        scratch_shapes=[pltpu.VMEM(w_si.shape, BF16), pltpu.SemaphoreType.DMA((N_PEERS,)),
                        pltpu.SemaphoreType.DMA((N_PEERS,)), pltpu.SemaphoreType.DMA],
        compiler_params=pltpu.CompilerParams(vmem_limit_bytes=VMEM_LIMIT_BYTES),
    )(w_si, w_so, w_hi, w_ho, hg_norm)


def _adamw(w, g, m, v):
    m = ADAM_B1 * m + (1.0 - ADAM_B1) * g
    v = ADAM_B2 * v + (1.0 - ADAM_B2) * (g * g)
    m_hat = m / (1.0 - ADAM_B1 ** ADAM_STEP)
    v_hat = v / (1.0 - ADAM_B2 ** ADAM_STEP)
    delta = -ADAM_LR * (m_hat / (jnp.sqrt(v_hat) + ADAM_EPS) + ADAM_WD * w)
    return delta, m, v


def reduce_adamw(parts, w, m, v, name):
    _, r, c = parts.shape
    tr = _row_tile(r, 256)

    def body(p_ref, w_ref, m_ref, v_ref, g_ref, d_ref, m2_ref, v2_ref):
        g = p_ref[0].astype(F32)
        for dev in range(1, N_DEV):
            g = g + p_ref[dev].astype(F32)
        g_ref[...] = g
        d_ref[...], m2_ref[...], v2_ref[...] = _adamw(w_ref[...], g, m_ref[...], v_ref[...])

    tile = pl.BlockSpec((tr, c), lambda i: (i, 0))
    return _pcall(
        body, name=name, grid=(r // tr,),
        in_specs=[pl.BlockSpec((N_DEV, tr, c), lambda i: (0, i, 0)), tile, tile, tile],
        out_specs=[tile] * 4, out_shape=[jax.ShapeDtypeStruct((r, c), F32)] * 4,
        compiler_params=_params("parallel"),
    )(parts, w, m, v)


PACK_ROWS = 32
ROW_SBN, ROW_HGN, ROW_LB, ROW_QG, ROW_KG, ROW_OG, ROW_LOSS = 0, 8, 16, 24, 25, 26, 27


def small_update(g_sbn, g_hgn, g_lb, g_qg, g_kg, g_og, loss_parts, small):
    n_in = 7 + len(small)

    def body(*refs):
        sbn_ref, hgn_ref, lb_ref, qg_ref, kg_ref, og_ref, loss_ref = refs[:7]
        wmv = refs[7:n_in]
        outs = refs[n_in:n_in + 25]
        pack, gath, tot, send_sems, recv_sems, local_sems = refs[n_in + 25:]
        pos = _mesh_pos()
        me = _linear(pos)
        pack[...] = jnp.zeros_like(pack)
        pack[ROW_SBN:ROW_SBN + 8, :] = sbn_ref[...]
        pack[ROW_HGN:ROW_HGN + 8, :] = hgn_ref[...]
        pack[ROW_LB:ROW_LB + 8, :] = jnp.sum(lb_ref[...], axis=0)
        pack[ROW_QG:ROW_QG + 1, :] = jnp.sum(qg_ref[...], axis=0, keepdims=True)
        pack[ROW_KG:ROW_KG + 1, :] = jnp.sum(kg_ref[...], axis=0, keepdims=True)
        pack[ROW_OG:ROW_OG + 1, :] = jnp.sum(og_ref[...], axis=0, keepdims=True)
        pack[ROW_LOSS:ROW_LOSS + 1, :] = jnp.sum(loss_ref[...], axis=0)[0:1, :]
        _exchange_start([((lambda p: pack), gath)], send_sems, recv_sems, local_sems, pos)
        _exchange_wait([((lambda p: pack), gath)], send_sems, recv_sems, local_sems, pos)
        total = gath[0]
        for dev in range(1, N_DEV):
            total = total + gath[dev]
        tot[...] = total
        outs[0][...] = jnp.broadcast_to(tot[ROW_LOSS:ROW_LOSS + 1, :], (8, HEAD_DIM))
        l0 = wmv[15][0:8, :]
        l1 = wmv[15][8:16, :]
        p1, p0 = _sigmoid_pair(l1 - l0)
        d_l1 = p0 * p1 * tot[ROW_LB:ROW_LB + 8, :]
        grads = [tot[ROW_SBN:ROW_SBN + 8, :], tot[ROW_QG:ROW_QG + 1, :], tot[ROW_KG:ROW_KG + 1, :],
                 tot[pl.ds(ROW_HGN + me, 1), :], tot[ROW_OG:ROW_OG + 1, :],
                 jnp.concatenate([-d_l1, d_l1], axis=0)]
        for i, g in enumerate(grads):
            w_ref, m_ref, v_ref = wmv[3 * i:3 * i + 3]
            o = outs[1 + 4 * i:5 + 4 * i]
            o[0][...] = g
            o[1][...], o[2][...], o[3][...] = _adamw(w_ref[...], g, m_ref[...], v_ref[...])

    out_shape = [jax.ShapeDtypeStruct((8, HEAD_DIM), F32)]
    for i in range(6):
        out_shape += [jax.ShapeDtypeStruct(small[3 * i].shape, F32)] * 4
    return _pcall(
        body, name="small_update",
        in_specs=[_VMEM] * n_in, out_specs=[_VMEM] * 25, out_shape=out_shape,
        scratch_shapes=[pltpu.VMEM((PACK_ROWS, HEAD_DIM), F32), pltpu.VMEM((N_DEV, PACK_ROWS, HEAD_DIM), F32),
                        pltpu.VMEM((PACK_ROWS, HEAD_DIM), F32)] + _exchange_sems(1),
    )(g_sbn, g_hgn, g_lb, g_qg, g_kg, g_og, loss_parts, *small)


def kernel(x, sb_norm, sb_w_in, sb_q_gain, sb_k_gain, sb_w_out, hg_norm, hg_w_in, hg_o_gain, hg_w_out, hg_lb_logits, loss_target, m_sb_norm, m_sb_w_in, m_sb_q_gain, m_sb_k_gain, m_sb_w_out, m_hg_norm, m_hg_w_in, m_hg_o_gain, m_hg_w_out, m_hg_lb_logits, v_sb_norm, v_sb_w_in, v_sb_q_gain, v_sb_k_gain, v_sb_w_out, v_hg_norm, v_hg_w_in, v_hg_o_gain, v_hg_w_out, v_hg_lb_logits):
    b = x.shape[0]
    wsi, wso_mine, whi_mine, who_mine, hg_norm_mine = gather_first_weights(
        sb_w_in[0], sb_w_out[0], hg_w_in[0], hg_w_out[0], hg_norm)
    r = local_step(x, loss_target, sb_norm, wsi, sb_q_gain, sb_k_gain, hg_o_gain, hg_lb_logits,
                   wso_mine, whi_mine, who_mine, hg_norm_mine)
    big = {}
    for name, p, w, m, v in (("sb_w_in", r["p_wsi"], sb_w_in, m_sb_w_in, v_sb_w_in),
                             ("sb_w_out", r["p_wso"], sb_w_out, m_sb_w_out, v_sb_w_out),
                             ("hg_w_in", r["p_whi"], hg_w_in, m_hg_w_in, v_hg_w_in),
                             ("hg_w_out", r["p_who"], hg_w_out, m_hg_w_out, v_hg_w_out)):
        big[name] = [o[None] for o in reduce_adamw(p, w[0], m[0], v[0], "adamw_" + name)]

    def rows8(a):
        return a.reshape(8, HEAD_DIM)

    def rows16(a):
        return a.reshape(16, HEAD_DIM)

    small_in = [rows8(sb_norm), rows8(m_sb_norm), rows8(v_sb_norm),
                sb_q_gain, m_sb_q_gain, v_sb_q_gain,
                sb_k_gain, m_sb_k_gain, v_sb_k_gain,
                hg_norm, m_hg_norm, v_hg_norm,
                hg_o_gain, m_hg_o_gain, v_hg_o_gain,
                rows16(hg_lb_logits), rows16(m_hg_lb_logits), rows16(v_hg_lb_logits)]
    so = small_update(rows8(r["g_sbn"]), rows8(r["g_hgn"]), r["g_lb"].reshape(b, N_HEADS, HEAD_DIM),
                      r["g_qg"].reshape(b * N_HEADS, HEAD_DIM), r["g_kg"].reshape(b * N_HEADS, HEAD_DIM),
                      r["g_og"].reshape(b * N_HEADS, HEAD_DIM), r["loss_parts"], small_in)
    loss = so[0][0, 0]
    shapes = {"sb_norm": (1, D_MODEL), "sb_q_gain": (1, HEAD_DIM), "sb_k_gain": (1, HEAD_DIM),
              "hg_norm": (1, HEAD_DIM), "hg_o_gain": (1, HEAD_DIM), "hg_lb_logits": (2, D_MODEL)}
    small = {}
    for i, name in enumerate(("sb_norm", "sb_q_gain", "sb_k_gain", "hg_norm", "hg_o_gain", "hg_lb_logits")):
        small[name] = [o.reshape(shapes[name]) for o in so[1 + 4 * i:5 + 4 * i]]
    order = ("sb_norm", "sb_w_in", "sb_q_gain", "sb_k_gain", "sb_w_out",
             "hg_norm", "hg_w_in", "hg_o_gain", "hg_w_out", "hg_lb_logits")
    res = {**big, **small}
    return (loss, r["gx"]) + tuple(res[n][j] for j in range(4) for n in order)
```

```python
import functools
import math

import jax
import jax.numpy as jnp
from jax import lax
from jax.experimental import pallas as pl
from jax.experimental.pallas import tpu as pltpu

F32 = jnp.float32
BF16 = jnp.bfloat16

N_DEV = 8
D_MODEL = 1024
N_HEADS = 8
HEAD_DIM = 128
RMS_EPS = 1e-6
ATTN_BLOCK = 128
HG_CHUNK = 64
HG_SUB = 16
HG_UNROLL_FWD = 16
HG_UNROLL_BWD = 16
EXP_CLAMP = 80.0
SB_HEADS_PER_STEP = 2
SB_QBLOCKS_PER_STEP = 4
SB_GROUP_COLS = SB_HEADS_PER_STEP * 128
SB_TOP_ROWS = 32
SB_LOG_WEIGHT_FLOOR = -104.0
VMEM_LIMIT_BYTES = 48 * 1024 * 1024
W_COLS = 4 * D_MODEL // N_DEV
W_ROWS = D_MODEL // N_DEV

ADAM_LR = 0.001
ADAM_B1 = 0.9
ADAM_B2 = 0.999
ADAM_EPS = 1e-08
ADAM_WD = 0.01
ADAM_STEP = 10

NT = (((1,), (1,)), ((), ()))
TN = (((0,), (0,)), ((), ()))
NN = (((1,), (0,)), ((), ()))


def _pcall(body, *, name, **kw):
    return pl.pallas_call(body, name=name, **kw)


def _params(*sem):
    return pltpu.CompilerParams(dimension_semantics=sem, vmem_limit_bytes=VMEM_LIMIT_BYTES)


def _dot(a, b, dims=NN):
    return lax.dot_general(a, b, dims, preferred_element_type=F32)


def _dot_exact(a, m, dims=NN, left=False):
    hi = a.astype(BF16)
    lo = (a - hi.astype(F32)).astype(BF16)
    if left:
        return _dot(m, hi, dims) + _dot(m, lo, dims)
    return _dot(hi, m, dims) + _dot(lo, m, dims)


def _split(a):
    hi = a.astype(BF16)
    return hi, (a - hi.astype(F32)).astype(BF16)


def _dot3(a, b, dims=NN):
    return _dot(a[0], b[0], dims) + (_dot(a[0], b[1], dims) + _dot(a[1], b[0], dims))


def _sigmoid(x):
    return 1.0 / (1.0 + jnp.exp(-x))


def _sigmoid_pair(x):
    e = jnp.exp(-jnp.abs(x))
    big = 1.0 / (1.0 + e)
    small = e * big
    pos = x >= 0
    return jnp.where(pos, big, small), jnp.where(pos, small, big)


def _rms_scale(x):
    return lax.rsqrt(jnp.mean(x * x, axis=-1, keepdims=True) + RMS_EPS)


def _row_tile(t, want):
    return want if t % want == 0 else t


MESH = pl.DeviceIdType.MESH
N_PEERS = N_DEV - 1
_ANY = pl.BlockSpec(memory_space=pl.ANY)
_VMEM = pl.BlockSpec(memory_space=pltpu.VMEM)


def _mesh_pos():
    return lax.axis_index("x"), lax.axis_index("y"), lax.axis_index("c")


def _linear(pos):
    return 4 * pos[0] + 2 * pos[1] + pos[2]


def _peer(pos, k):
    flips = ((k + 1) >> 2 & 1, (k + 1) >> 1 & 1, (k + 1) & 1)
    return tuple(1 - p if f else p for p, f in zip(pos, flips))


def _chip(pos):
    return 2 * pos[0] + pos[1]


ALL_PEERS = tuple(range(N_PEERS))
SAME_CORE_PEERS = (1, 3, 5)


def _exchange_copies(pairs, send_sems, recv_sems, local_sems, pos, landing, peers, slot):
    me = slot(pos)
    local, remote = [], []
    for a, (src_of, dst) in enumerate(pairs):
        local.append(pltpu.make_async_copy(src_of(pos), dst.at[me], local_sems.at[a]))
        for k in peers:
            peer = _peer(pos, k)
            remote.append(pltpu.make_async_remote_copy(
                src_ref=src_of(peer), dst_ref=dst.at[slot(peer) if landing else me],
                send_sem=send_sems.at[a, k], recv_sem=recv_sems.at[a, k], device_id=peer, device_id_type=MESH))
    return local, remote


def _exchange_start(pairs, send_sems, recv_sems, local_sems, pos, peers=ALL_PEERS, slot=_linear):
    local, sent = _exchange_copies(pairs, send_sems, recv_sems, local_sems, pos, False, peers, slot)
    for copy in local + sent:
        copy.start()


def _exchange_wait(pairs, send_sems, recv_sems, local_sems, pos, peers=ALL_PEERS, slot=_linear):
    local, landed = _exchange_copies(pairs, send_sems, recv_sems, local_sems, pos, True, peers, slot)
    for copy in landed:
        copy.wait_recv()
        copy.wait_send()
    for copy in local:
        copy.wait()


def _exchange_sems(n):
    return [pltpu.SemaphoreType.DMA((n, N_PEERS)), pltpu.SemaphoreType.DMA((n, N_PEERS)),
            pltpu.SemaphoreType.DMA((n,))]


class _Rider:
    def __init__(self, arrays, scatter, chips=False):
        self.arrays = list(arrays)
        self.scatter = scatter
        self.peers = SAME_CORE_PEERS if chips else ALL_PEERS
        self.slot = _chip if chips else _linear
        self.out_shapes = [jax.ShapeDtypeStruct(a.shape if scatter else (N_DEV,) + a.shape, a.dtype)
                           for a in self.arrays]

    def pairs(self, in_refs, out_refs):
        if self.scatter:
            return [((lambda pos, r=r: r.at[self.slot(pos)]), o) for r, o in zip(in_refs, out_refs)]
        return [((lambda pos, r=r: r), o) for r, o in zip(in_refs, out_refs)]


def _pcall_riding(body, rider, args, *, name, grid, in_specs, out_specs, out_shape, semantics, scratch_shapes=(),
                  input_output_aliases=None):
    aliases = input_output_aliases or {}
    if rider is None:
        outs = _pcall(body, name=name, grid=grid, in_specs=list(in_specs), out_specs=list(out_specs),
                      out_shape=list(out_shape), scratch_shapes=list(scratch_shapes), input_output_aliases=aliases,
                      compiler_params=_params(*semantics))(*args)
        return list(outs), []
    n_in, n_out, n_scr, n_r = len(in_specs), len(out_specs), len(scratch_shapes), len(rider.arrays)

    def riding(*refs):
        ins, refs = refs[:n_in], refs[n_in:]
        rider_in, refs = refs[:n_r], refs[n_r:]
        outs, refs = refs[:n_out], refs[n_out:]
        rider_out, refs = refs[:n_r], refs[n_r:]
        scratch, sems = refs[:n_scr], refs[n_scr:]
        pairs = rider.pairs(rider_in, rider_out)
        first = functools.reduce(jnp.logical_and, [pl.program_id(a) == 0 for a in range(len(grid))])
        last = functools.reduce(jnp.logical_and, [pl.program_id(a) == g - 1 for a, g in enumerate(grid)])

        @pl.when(first)
        def _():
            _exchange_start(pairs, *sems, _mesh_pos(), rider.peers, rider.slot)

        body(*ins, *outs, *scratch)

        @pl.when(last)
        def _():
            _exchange_wait(pairs, *sems, _mesh_pos(), rider.peers, rider.slot)

    outs = _pcall(riding, name=name, grid=grid, in_specs=list(in_specs) + [_ANY] * n_r,
                  out_specs=list(out_specs) + [_ANY] * n_r, out_shape=list(out_shape) + rider.out_shapes,
                  scratch_shapes=list(scratch_shapes) + _exchange_sems(n_r), input_output_aliases=aliases,
                  compiler_params=_params(*(("arbitrary",) * len(grid))))(*args, *rider.arrays)
    return list(outs[:n_out]), list(outs[n_out:])


def rms_inproj(x2, gain, wg, name, rider=None, qk_gains=None):
    t = x2.shape[0]
    tm = _row_tile(t, 256)
    with_qkv = qk_gains is not None

    def body(x_ref, g_ref, w_ref, *rest):
        if with_qkv:
            qg_ref, kg_ref, proj_ref, ut_ref, qkv_ref = rest
            head_gain = (qg_ref, kg_ref)
        else:
            proj_ref, ut_ref = rest
        x = x_ref[...]
        u = x * _rms_scale(x) * g_ref[...]
        ut_ref[...] = u.T.astype(BF16)
        u = u.astype(BF16)
        for p in range(N_DEV):
            part, lo = p // 2, (p % 2) * W_COLS
            res = _dot(u, w_ref[p])
            proj_ref[part, :, lo:lo + W_COLS] = res
            if with_qkv and part < 3:
                for h in range(W_COLS // HEAD_DIM):
                    y = res[:, h * HEAD_DIM:(h + 1) * HEAD_DIM]
                    if part < 2:
                        y = y * _rms_scale(y) * head_gain[part][...]
                    qkv_ref[part, :, lo + h * HEAD_DIM:lo + (h + 1) * HEAD_DIM] = y.astype(BF16)

    vec = pl.BlockSpec((1, D_MODEL), lambda i: (0, 0))
    in_specs = [pl.BlockSpec((tm, D_MODEL), lambda i: (i, 0)), vec,
                pl.BlockSpec((N_DEV, D_MODEL, W_COLS), lambda i: (0, 0, 0))]
    out_specs = [pl.BlockSpec((4, tm, D_MODEL), lambda i: (0, i, 0)), pl.BlockSpec((D_MODEL, tm), lambda i: (0, i))]
    out_shape = [jax.ShapeDtypeStruct((4, t, D_MODEL), F32), jax.ShapeDtypeStruct((D_MODEL, t), BF16)]
    args = (x2, gain, wg)
    if with_qkv:
        in_specs += [pl.BlockSpec((1, HEAD_DIM), lambda i: (0, 0))] * 2
        out_specs.append(pl.BlockSpec((3, tm, D_MODEL), lambda i: (0, i, 0)))
        out_shape.append(jax.ShapeDtypeStruct((3, t, D_MODEL), BF16))
        args += tuple(qk_gains)
    return _pcall_riding(body, rider, args, name=name, grid=(t // tm,), in_specs=in_specs, out_specs=out_specs,
                         out_shape=out_shape, semantics=("parallel",))


def gate_outproj(o2, proj, w_out, resid, target, name):
    t = o2.shape[0]
    tm = _row_tile(t, 256)
    with_loss = target is not None

    def body(o_ref, gate_ref, w_ref, r_ref, *rest):
        g = gate_ref[0]
        og = (o_ref[...] * (g * _sigmoid(g))).astype(BF16)
        h = r_ref[...] + _dot(og, w_ref[...])
        if with_loss:
            t_ref, dh_ref, loss_ref = rest
            err = h - t_ref[...]
            dh_ref[...] = err * (1.0 / D_MODEL)
            part = 0.5 * jnp.sum(jnp.mean(err * err, axis=-1, keepdims=True))
            loss_ref[...] = jnp.full(loss_ref.shape, part, F32)
        else:
            (h_ref,) = rest
            h_ref[...] = h

    row = pl.BlockSpec((tm, D_MODEL), lambda i: (i, 0))
    in_specs = [row,
                pl.BlockSpec((1, tm, D_MODEL), lambda i: (3, i, 0)),
                pl.BlockSpec((D_MODEL, D_MODEL), lambda i: (0, 0)),
                row]
    args = [o2, proj, w_out, resid]
    if with_loss:
        in_specs.append(row)
        args.append(target)
        out_specs = [row, pl.BlockSpec((1, 8, 128), lambda i: (i, 0, 0))]
        out_shape = [jax.ShapeDtypeStruct((t, D_MODEL), F32),
                     jax.ShapeDtypeStruct((t // tm, 8, 128), F32)]
    else:
        out_specs = row
        out_shape = jax.ShapeDtypeStruct((t, D_MODEL), F32)
    return _pcall(body, name=name, grid=(t // tm,), in_specs=in_specs, out_specs=out_specs,
                  out_shape=out_shape, compiler_params=_params("parallel"))(*args)


def _head_spec(s, part):
    return pl.BlockSpec((1, 1, s, HEAD_DIM), lambda b, h: (part, b, 0, h))


def _seq_spec(s):
    return pl.BlockSpec((1, s, HEAD_DIM), lambda b, h: (b, 0, h))


_GAIN_SPEC = pl.BlockSpec((1, HEAD_DIM), lambda b, h: (0, 0))
_HEAD_ROW_SPEC = pl.BlockSpec((1, 1, 1, HEAD_DIM), lambda b, h: (b, h, 0, 0))


def _sb_group_spec(s, part):
    return pl.BlockSpec((1, 1, s, SB_GROUP_COLS), lambda b, g: (part, b, 0, g))


def _sb_seq_group_spec(s):
    return pl.BlockSpec((1, s, SB_GROUP_COLS), lambda b, g: (b, 0, g))


_SB_GROUP_ROW_SPEC = pl.BlockSpec((1, SB_HEADS_PER_STEP, 1, HEAD_DIM), lambda b, g: (b, g, 0, 0))


def _sb_chains(m):
    return [(h, m * SB_QBLOCKS_PER_STEP + r) for h in range(SB_HEADS_PER_STEP) for r in range(SB_QBLOCKS_PER_STEP)]


def _sb_logits(qi, kj):
    return _dot(qi, kj, NT) * (HEAD_DIM ** -0.5)


def _sb_scores(z, diag, live, tri_lt):
    soft = jnp.log(1.0 + jnp.exp(-jnp.abs(z)))
    valid = jnp.logical_and(live, jnp.logical_or(jnp.logical_not(diag), tri_lt))
    log_skip = jnp.where(valid, -(jnp.maximum(z, 0.0) + soft), 0.0)
    log_beta = jnp.minimum(z, 0.0) - soft
    return log_skip, log_beta, valid


def _sb_keys_left(chains, watch, state):
    done, carries = state[0], state[1]
    worst = None
    for (_, i), c in zip(chains, carries):
        c = jnp.where(done <= i, c[watch], -jnp.inf)
        worst = c if worst is None else jnp.maximum(worst, c)
    return jnp.logical_and(done <= chains[-1][1],
                           jnp.logical_or(done == 0, jnp.max(worst) > SB_LOG_WEIGHT_FLOOR))


def _sb_key_rows(i, done):
    j = i - done
    return pl.ds(pl.multiple_of(jnp.maximum(j, 0) * ATTN_BLOCK, ATTN_BLOCK), ATTN_BLOCK), j >= 0


def _sb_head(ref, h, rows):
    return ref[0, 0, rows, h * HEAD_DIM:(h + 1) * HEAD_DIM]


def _sb_rows(i, offset, count):
    return pl.ds(pl.multiple_of(jnp.maximum(i, 0) * ATTN_BLOCK + offset, 8), count)


def sb_attn_fwd(qkv4, rider=None):
    _, b, s, _ = qkv4.shape
    blk, top = ATTN_BLOCK, SB_TOP_ROWS
    ngroups = s // blk // SB_QBLOCKS_PER_STEP
    assert ngroups * SB_QBLOCKS_PER_STEP * blk == s

    def body(q_ref, k_ref, v_ref, o_ref):
        row = lax.broadcasted_iota(jnp.int32, (blk, blk), 0)
        col = lax.broadcasted_iota(jnp.int32, (blk, blk), 1)
        tri_lt = col < row
        suffix = (row > col).astype(BF16)

        def step(items):
            where = [_sb_key_rows(i, done) for _, i, done, _, _, _, _ in items]
            zs = [_sb_logits(q, _sb_head(k_ref, h, rows)) for (h, _, _, q, _, _, _), (rows, _) in zip(items, where)]
            scored = [_sb_scores(z, done == 0, live, mask)
                      for z, (_, _, done, _, mask, _, _), (_, live) in zip(zs, items, where)]
            afters = [_dot_exact(log_skip, suffix) for log_skip, _, _ in scored]
            ws = [jnp.where(valid, jnp.exp(log_beta + after + c), 0.0).astype(BF16)
                  for (_, log_beta, valid), after, (_, _, _, _, _, c, _) in zip(scored, afters, items)]
            accs = [acc + _dot(w, _sb_head(v_ref, h, rows))
                    for (h, _, _, _, _, _, acc), (rows, _), w in zip(items, where, ws)]
            cs = [c + jnp.sum(log_skip, axis=1, keepdims=True)
                  for (log_skip, _, _), (_, _, _, _, _, c, _) in zip(scored, items)]
            return cs, accs

        def top_items(chains, done, cs, accs):
            return [(h, i, done, _sb_head(q_ref, h, _sb_rows(i, 0, top)), tri_lt[:top, :], c, acc)
                    for (h, i), c, acc in zip(chains, cs, accs)]

        def finish_tops(chains, state):
            def k_step(state):
                done, cs, accs = state
                cs, accs = step(top_items(chains, done, cs, accs))
                return done + 1, tuple(cs), tuple(accs)

            _, _, accs = lax.while_loop(functools.partial(_sb_keys_left, chains, slice(0, top)), k_step, state)
            for (h, i), acc in zip(chains, accs):
                o_ref[0, _sb_rows(i, 0, top), h * HEAD_DIM:(h + 1) * HEAD_DIM] = acc

        def q_group(m, before):
            chains, chains_before = _sb_chains(m), _sb_chains(m - 1)
            qis = [_sb_head(q_ref, h, _sb_rows(i, 0, blk)) for h, i in chains]
            n = len(chains)

            def items_of(done, cs, accs):
                return [(h, i, done, q, tri_lt, c, acc) for (h, i), q, c, acc in zip(chains, qis, cs, accs)]

            def k_step(state):
                done, cs, accs = state
                cs, accs = step(items_of(done, cs, accs))
                return done + 1, tuple(cs), tuple(accs)

            done_b, cs_b, accs_b = before
            cs0, accs0 = step(items_of(jnp.int32(0), (jnp.zeros((blk, 1), F32),) * n,
                                       (jnp.zeros((blk, HEAD_DIM), F32),) * n)
                              + top_items(chains_before, done_b, cs_b, accs_b))
            done, cs, accs = lax.while_loop(functools.partial(_sb_keys_left, chains, slice(top, blk)), k_step,
                                            (jnp.int32(1), tuple(cs0[:n]), tuple(accs0[:n])))

            @pl.when(m > 0)
            def _():
                finish_tops(chains_before, (done_b + 1, tuple(cs0[n:]), tuple(accs0[n:])))

            for (h, i), acc in zip(chains, accs):
                o_ref[0, _sb_rows(i, top, blk - top), h * HEAD_DIM:(h + 1) * HEAD_DIM] = acc[top:, :]
            return done, tuple(c[:top, :] for c in cs), tuple(acc[:top, :] for acc in accs)

        n = SB_HEADS_PER_STEP * SB_QBLOCKS_PER_STEP
        nothing = (jnp.int32(0), (jnp.zeros((top, 1), F32),) * n, (jnp.zeros((top, HEAD_DIM), F32),) * n)
        last = lax.fori_loop(0, ngroups, q_group, nothing)
        finish_tops(_sb_chains(ngroups - 1), last)

    (o,), extra = _pcall_riding(
        body, rider, (qkv4, qkv4, qkv4),
        name="sb_attn_fwd", grid=(b, N_HEADS // SB_HEADS_PER_STEP),
        in_specs=[_sb_group_spec(s, 0), _sb_group_spec(s, 1), _sb_group_spec(s, 2)],
        out_specs=[_sb_seq_group_spec(s)],
        out_shape=[jax.ShapeDtypeStruct((b, s, D_MODEL), F32)],
        semantics=("parallel", "parallel"))
    return o, extra


def _hg_masks():
    c = HG_CHUNK
    row = lax.broadcasted_iota(jnp.int32, (c, c), 0)
    col = lax.broadcasted_iota(jnp.int32, (c, c), 1)
    incl = (col <= row)
    lower = incl.astype(BF16)
    before_sub = (col < (row // HG_SUB) * HG_SUB).astype(BF16)
    upper = (col >= row).astype(BF16)
    return incl, lower, before_sub, upper


def _hg_lower_bound(lbl_ref):
    l0 = lbl_ref[0, 0]
    l1 = lbl_ref[1, 0]
    d = l1 - l0
    return _sigmoid_pair(d)


def _hg_gates(qp, fp, lb, oml):
    sq = _sigmoid(qp)
    sf, sfn = _sigmoid_pair(fp)
    f = lb + oml * sf
    return dict(qp=qp, sq=sq, q=qp * sq, sf=sf, sfn=sfn, f=f, k=oml * sfn, logf=jnp.log(f))


def _hg_intra(qds, ks, gcs, grs, incl):
    subs = range(HG_CHUNK // HG_SUB)
    qdbs = [qd.astype(BF16) for qd in qds]
    ess = [[jnp.exp(jnp.minimum(gr[sub * HG_SUB:sub * HG_SUB + 1, :] - gc, EXP_CLAMP)) for sub in subs]
           for gc, gr in zip(gcs, grs)]
    ksbs = [[(k * e).astype(BF16) for e in es] for k, es in zip(ks, ess)]
    rows = [[_dot(qdb[sub * HG_SUB:(sub + 1) * HG_SUB, :], ksb[sub], NT) for sub in subs]
            for qdb, ksb in zip(qdbs, ksbs)]
    a_s = [jnp.where(incl, jnp.concatenate(r, axis=0), 0.0) for r in rows]
    return a_s, qdbs, ksbs, ess


def _hg_group_rows(outer, unroll):
    ns = [outer * unroll + u for u in range(unroll)]
    return ns, [pl.ds(pl.multiple_of(n * HG_CHUNK, HG_CHUNK), HG_CHUNK) for n in ns]


def _state_spec(nchunk):
    return pl.BlockSpec((1, 1, nchunk, HEAD_DIM, HEAD_DIM), lambda b, h: (b, h, 0, 0, 0))


def hgrn2_fwd(proj4, lbl4, o_gain):
    _, b, s, _ = proj4.shape
    nchunk = s // HG_CHUNK
    c = HG_CHUNK
    unroll = math.gcd(nchunk, HG_UNROLL_FWD)

    def body(q_ref, f_ref, i_ref, lbl_ref, og_ref, o_ref, oraw_ref, st_ref):
        incl, lower, before_sub, _ = _hg_masks()
        lb, oml = _hg_lower_bound(lbl_ref)

        def group(outer, st):
            ns, rows = _hg_group_rows(outer, unroll)
            vs = [_hg_gates(q_ref[0, 0, r, :], f_ref[0, 0, r, :], lb, oml) for r in rows]
            inps = [i_ref[0, 0, r, :].astype(BF16) for r in rows]
            gcs = [_dot_exact(v["logf"], lower, left=True) for v in vs]
            grs = [_dot_exact(v["logf"], before_sub, left=True) for v in vs]
            a_s, _, _, _ = _hg_intra([v["q"] * jnp.exp(gc - gr) for v, gc, gr in zip(vs, gcs, grs)],
                                     [v["k"] for v in vs], gcs, grs, incl)
            gls = [gc[c - 1:c, :] for gc in gcs]
            adds = [_dot(inp, (v["k"] * jnp.exp(gl - gc)).astype(BF16), TN)
                    for inp, v, gl, gc in zip(inps, vs, gls, gcs)]
            o_intra = [_dot(a.astype(BF16), inp) for a, inp in zip(a_s, inps)]
            sts = []
            for gl, add in zip(gls, adds):
                sts.append(st)
                st = st * jnp.exp(gl) + add
            outs = [oi + _dot((v["q"] * jnp.exp(gc)).astype(BF16), s0.astype(BF16), NT)
                    for oi, v, gc, s0 in zip(o_intra, vs, gcs, sts)]
            for n, r, s0, o in zip(ns, rows, sts, outs):
                st_ref[0, 0, n] = s0
                oraw_ref[0, r, :] = o
                o_ref[0, r, :] = o * _rms_scale(o) * og_ref[...]
            return st

        lax.fori_loop(0, nchunk // unroll, group, jnp.zeros((HEAD_DIM, HEAD_DIM), F32))

    seq = jax.ShapeDtypeStruct((b, s, D_MODEL), F32)
    return _pcall(
        body, name="hgrn2_fwd", grid=(b, N_HEADS),
        in_specs=[_head_spec(s, 0), _head_spec(s, 1), _head_spec(s, 2),
                  pl.BlockSpec((2, 1, 1, HEAD_DIM), lambda b, h: (0, h, 0, 0)), _GAIN_SPEC],
        out_specs=[_seq_spec(s), _seq_spec(s), _state_spec(nchunk)],
        out_shape=[seq, seq, jax.ShapeDtypeStruct((b, N_HEADS, nchunk, HEAD_DIM, HEAD_DIM), F32)],
        compiler_params=_params("parallel", "parallel"),
    )(proj4, proj4, proj4, lbl4, o_gain)


def outproj_bwd(dh, w_out, o2, proj, name):
    t = dh.shape[0]
    tm = _row_tile(t, 256)

    def body(dh_ref, w_ref, o_ref, gate_ref, do_ref, dproj_ref, dw_ref):
        dhb = dh_ref[...].astype(BF16)
        dog = _dot(dhb, w_ref[...], NT)
        g = gate_ref[0]
        sg = _sigmoid(g)
        silu = g * sg
        o = o_ref[...]
        do_ref[...] = dog * silu
        dproj_ref[0] = dog * o * (sg * (1.0 + g * (1.0 - sg)))
        part = _dot((o * silu).astype(BF16), dhb, TN)

        @pl.when(pl.program_id(0) == 0)
        def _():
            dw_ref[...] = part

        @pl.when(pl.program_id(0) > 0)
        def _():
            dw_ref[...] += part

    row = pl.BlockSpec((tm, D_MODEL), lambda i: (i, 0))
    full = pl.BlockSpec((D_MODEL, D_MODEL), lambda i: (0, 0))
    return _pcall(
        body, name=name, grid=(t // tm,),
        in_specs=[row, full, row, pl.BlockSpec((1, tm, D_MODEL), lambda i: (3, i, 0))],
        out_specs=[row, pl.BlockSpec((1, tm, D_MODEL), lambda i: (3, i, 0)), full],
        out_shape=[jax.ShapeDtypeStruct((t, D_MODEL), F32),
                   jax.ShapeDtypeStruct((4, t, D_MODEL), F32),
                   jax.ShapeDtypeStruct((D_MODEL, D_MODEL), F32)],
        compiler_params=_params("arbitrary"),
    )(dh, w_out, o2, proj)


def inproj_bwd_dx(dproj, wg, x2, gain, dres, name, rider=None):
    t = x2.shape[0]
    tm = _row_tile(t, 256)

    def body(d_ref, w_ref, x_ref, g_ref, r_ref, dx_ref, dg_ref):
        du = jnp.zeros((tm, D_MODEL), F32)
        for p in range(N_DEV):
            cols = slice((p % 2) * W_COLS, (p % 2 + 1) * W_COLS)
            du = du + _dot(d_ref[p // 2, :, cols].astype(BF16), w_ref[p], NT)
        x = x_ref[...]
        r = _rms_scale(x)
        xh = x * r
        a = du * g_ref[...]
        dx_ref[...] = r_ref[...] + r * (a - xh * jnp.mean(a * xh, axis=-1, keepdims=True))
        part = jnp.sum(du * xh, axis=0, keepdims=True)

        @pl.when(pl.program_id(0) == 0)
        def _():
            dg_ref[...] = part

        @pl.when(pl.program_id(0) > 0)
        def _():
            dg_ref[...] += part

    row = pl.BlockSpec((tm, D_MODEL), lambda i: (i, 0))
    vec = pl.BlockSpec((1, D_MODEL), lambda i: (0, 0))
    return _pcall_riding(
        body, rider, (dproj, wg, x2, gain, dres), name=name, grid=(t // tm,),
        in_specs=[pl.BlockSpec((4, tm, D_MODEL), lambda i: (0, i, 0)),
                  pl.BlockSpec((N_DEV, D_MODEL, W_COLS), lambda i: (0, 0, 0)),
                  row, vec, row],
        out_specs=[row, vec],
        out_shape=[jax.ShapeDtypeStruct((t, D_MODEL), F32), jax.ShapeDtypeStruct((1, D_MODEL), F32)],
        semantics=("arbitrary",))


def inproj_bwd_dw(ut, dproj, name, out_dtype):
    t = ut.shape[1]

    def body(ut_ref, d_ref, dw_ref):
        dw_ref[0] = _dot(ut_ref[...], d_ref[0].astype(BF16)).astype(dw_ref.dtype)

    return _pcall(
        body, name=name, grid=(N_DEV,),
        in_specs=[pl.BlockSpec((D_MODEL, t), lambda j: (0, 0)),
                  pl.BlockSpec((1, t, W_COLS), lambda j: (j // 2, 0, j % 2))],
        out_specs=pl.BlockSpec((1, D_MODEL, W_COLS), lambda j: (j, 0, 0)),
        out_shape=jax.ShapeDtypeStruct((N_DEV, D_MODEL, W_COLS), out_dtype),
        compiler_params=_params("parallel"),
    )(ut, dproj)


def _rms_bwd(x, gain, dy):
    r = _rms_scale(x)
    xh = x * r
    a = dy * gain
    return r * (a - xh * jnp.mean(a * xh, axis=-1, keepdims=True)), dy * xh


def sb_attn_bwd(qkv4, proj4, do3, o3, q_gain, k_gain, dproj4, rider=None):
    _, b, s, _ = proj4.shape
    blk, top = ATTN_BLOCK, SB_TOP_ROWS
    nq = s // blk
    ngroups = nq // SB_QBLOCKS_PER_STEP
    assert ngroups * SB_QBLOCKS_PER_STEP * blk == s
    scale = HEAD_DIM ** -0.5

    def body(qn_ref, kn_ref, v_ref, q_ref, k_ref, do_ref, o_ref, qg_ref, kg_ref, _alias, d_ref, dqg_ref, dkg_ref, dob):
        for h in range(SB_HEADS_PER_STEP):
            dob[h] = do_ref[0, :, h * HEAD_DIM:(h + 1) * HEAD_DIM].astype(BF16)
        d_ref[...] = jnp.zeros_like(d_ref)
        row = lax.broadcasted_iota(jnp.int32, (blk, blk), 0)
        col = lax.broadcasted_iota(jnp.int32, (blk, blk), 1)
        tri_lt = col < row
        suffix = (row > col).astype(BF16)
        suffix_incl = (row >= col).astype(BF16)

        def step(items):
            heads = [it[0] for it in items]
            where = [_sb_key_rows(it[1], it[2]) for it in items]
            kjs = [_sb_head(kn_ref, h, rows) for h, (rows, _) in zip(heads, where)]
            zs = [_sb_logits(it[3], kj) for it, kj in zip(items, kjs)]
            dws = [_dot(it[4], _sb_head(v_ref, h, rows), NT) for it, h, (rows, _) in zip(items, heads, where)]
            scored = [_sb_scores(z, it[2] == 0, live, it[6]) for z, it, (_, live) in zip(zs, items, where)]
            afters = [_dot_exact(log_skip, suffix) for log_skip, _, _ in scored]
            wbs = [jnp.where(valid, jnp.exp(log_beta + after + it[7]), 0.0).astype(BF16)
                   for (_, log_beta, valid), after, it in zip(scored, afters, items)]
            gs = [dw * wb.astype(F32) for dw, wb in zip(dws, wbs)]
            befores = [it[5] - (_dot_exact(g, suffix_incl) + it[8]) for g, it in zip(gs, items)]
            dzbs = [jnp.where(valid, g - jnp.exp(log_beta) * (g + before), 0.0).astype(BF16)
                    for (_, log_beta, valid), g, before in zip(scored, gs, befores)]
            dqs = [it[9] + _dot(dzb, kj) for it, dzb, kj in zip(items, dzbs, kjs)]
            for it, h, (rows, _), wb, dzb in zip(items, heads, where, wbs, dzbs):
                cols = slice(h * HEAD_DIM, (h + 1) * HEAD_DIM)
                d_ref[2, 0, rows, cols] += _dot(wb, it[4], TN)
                d_ref[1, 0, rows, cols] += _dot(dzb, it[3], TN)
            cs = [it[7] + jnp.sum(log_skip, axis=1, keepdims=True) for (log_skip, _, _), it in zip(scored, items)]
            cgs = [it[8] + jnp.sum(g, axis=1, keepdims=True) for g, it in zip(gs, items)]
            return cs, cgs, dqs

        def top_items(chains, deltas, done, cs, cgs, dqs):
            return [(h, i, done, _sb_head(qn_ref, h, _sb_rows(i, 0, top)), dob[h, _sb_rows(i, 0, top), :], delta,
                     tri_lt[:top, :], c, cg, dq)
                    for (h, i), delta, c, cg, dq in zip(chains, deltas, cs, cgs, dqs)]

        def finish_tops(chains, deltas, state):
            def k_step(state):
                done, cs, cgs, dqs = state
                cs, cgs, dqs = step(top_items(chains, deltas, done, cs, cgs, dqs))
                return done + 1, tuple(cs), tuple(cgs), tuple(dqs)

            _, _, _, dqs = lax.while_loop(functools.partial(_sb_keys_left, chains, slice(0, top)), k_step, state)
            for (h, i), dq in zip(chains, dqs):
                d_ref[0, 0, _sb_rows(i, 0, top), h * HEAD_DIM:(h + 1) * HEAD_DIM] = dq * scale

        def q_group(m, before):
            chains, chains_before = _sb_chains(m), _sb_chains(m - 1)
            deltas_b, before = before[0], before[1:]
            qis, dois, deltas = [], [], []
            for h, i in chains:
                rows_i = _sb_rows(i, 0, blk)
                qis.append(_sb_head(qn_ref, h, rows_i))
                dois.append(dob[h, rows_i, :])
                deltas.append(jnp.sum(dois[-1].astype(F32) * o_ref[0, rows_i, h * HEAD_DIM:(h + 1) * HEAD_DIM],
                                      axis=1, keepdims=True))
            n = len(chains)

            def items_of(done, cs, cgs, dqs):
                return [(h, i, done, q, do, delta, tri_lt, c, cg, dq)
                        for (h, i), q, do, delta, c, cg, dq in zip(chains, qis, dois, deltas, cs, cgs, dqs)]

            def k_step(state):
                done, cs, cgs, dqs = state
                cs, cgs, dqs = step(items_of(done, cs, cgs, dqs))
                return done + 1, tuple(cs), tuple(cgs), tuple(dqs)

            done_b, cs_b, cgs_b, dqs_b = before
            zero = (jnp.zeros((blk, 1), F32),) * n
            new = step(items_of(jnp.int32(0), zero, zero, (jnp.zeros((blk, HEAD_DIM), F32),) * n)
                       + top_items(chains_before, deltas_b, done_b, cs_b, cgs_b, dqs_b))
            done, cs, cgs, dqs = lax.while_loop(functools.partial(_sb_keys_left, chains, slice(top, blk)), k_step,
                                                (jnp.int32(1),) + tuple(tuple(x[:n]) for x in new))

            @pl.when(m > 0)
            def _():
                finish_tops(chains_before, deltas_b, (done_b + 1,) + tuple(tuple(x[n:]) for x in new))

            for (h, i), dq in zip(chains, dqs):
                d_ref[0, 0, _sb_rows(i, top, blk - top), h * HEAD_DIM:(h + 1) * HEAD_DIM] = dq[top:, :] * scale
            first = lambda xs: tuple(x[:top, :] for x in xs)
            return first(deltas), done, first(cs), first(cgs), first(dqs)

        n = SB_HEADS_PER_STEP * SB_QBLOCKS_PER_STEP
        zero = (jnp.zeros((top, 1), F32),) * n
        nothing = (zero, jnp.int32(0), zero, zero, (jnp.zeros((top, HEAD_DIM), F32),) * n)
        last = lax.fori_loop(0, ngroups, q_group, nothing)
        finish_tops(_sb_chains(ngroups - 1), last[0], last[1:])

        def norm_block(i, carry):
            rows = pl.ds(pl.multiple_of(i * blk, blk), blk)
            out = []
            for h in range(SB_HEADS_PER_STEP):
                cols = slice(h * HEAD_DIM, (h + 1) * HEAD_DIM)
                for part, src_ref, gain_ref in ((0, q_ref, qg_ref), (1, k_ref, kg_ref)):
                    dy = d_ref[part, 0, rows, cols] * (scale if part == 1 else 1.0)
                    dx, pg = _rms_bwd(src_ref[0, 0, rows, cols], gain_ref[...], dy)
                    d_ref[part, 0, rows, cols] = dx
                    out.append(carry[len(out)] + jnp.sum(pg, axis=0, keepdims=True))
            return tuple(out)

        sums = lax.fori_loop(0, nq, norm_block, (jnp.zeros((1, HEAD_DIM), F32),) * (2 * SB_HEADS_PER_STEP))
        for h in range(SB_HEADS_PER_STEP):
            dqg_ref[0, h] = sums[2 * h]
            dkg_ref[0, h] = sums[2 * h + 1]

    head_row = jax.ShapeDtypeStruct((b, N_HEADS, 1, HEAD_DIM), F32)
    return _pcall_riding(
        body, rider, (qkv4, qkv4, qkv4, proj4, proj4, do3, o3, q_gain, k_gain, dproj4),
        name="sb_attn_bwd", grid=(b, N_HEADS // SB_HEADS_PER_STEP),
        in_specs=[_sb_group_spec(s, 0), _sb_group_spec(s, 1), _sb_group_spec(s, 2),
                  _sb_group_spec(s, 0), _sb_group_spec(s, 1),
                  _sb_seq_group_spec(s), _sb_seq_group_spec(s), _GAIN_SPEC, _GAIN_SPEC,
                  pl.BlockSpec(memory_space=pl.ANY)],
        out_specs=[pl.BlockSpec((3, 1, s, SB_GROUP_COLS), lambda b, g: (0, b, 0, g)),
                   _SB_GROUP_ROW_SPEC, _SB_GROUP_ROW_SPEC],
        out_shape=[jax.ShapeDtypeStruct(dproj4.shape, F32), head_row, head_row],
        scratch_shapes=[pltpu.VMEM((SB_HEADS_PER_STEP, s, HEAD_DIM), BF16)],
        input_output_aliases={9: 0}, semantics=("parallel", "parallel"))


def hgrn2_bwd(proj4, don3, oraw3, states, lbl4, o_gain, dproj4, rider=None):
    _, b, s, _ = proj4.shape
    nchunk = s // HG_CHUNK
    c = HG_CHUNK
    subs = range(HG_CHUNK // HG_SUB)
    unroll = math.gcd(nchunk, HG_UNROLL_BWD)
    ngroup = nchunk // unroll

    def body(q_ref, f_ref, i_ref, don_ref, oraw_ref, st_ref, lbl_ref, og_ref, _alias, d_ref, dog_ref, dlb_ref):
        incl, lower, before_sub, upper = _hg_masks()
        lb, oml = _hg_lower_bound(lbl_ref)
        last_row = lax.broadcasted_iota(jnp.int32, (c, HEAD_DIM), 0) == c - 1

        def group(m, carry):
            dst, dog_acc, dlb_acc = carry
            ns, rows = _hg_group_rows(ngroup - 1 - m, unroll)
            ns, rows = ns[::-1], rows[::-1]
            vs = [_hg_gates(q_ref[0, 0, r, :], f_ref[0, 0, r, :], lb, oml) for r in rows]
            inps = [i_ref[0, 0, r, :].astype(BF16) for r in rows]
            sts = [st_ref[0, 0, n] for n in ns]
            gcs = [_dot_exact(v["logf"], lower, left=True) for v in vs]
            grs = [_dot_exact(v["logf"], before_sub, left=True) for v in vs]
            e_qs = [jnp.exp(gc - gr) for gc, gr in zip(gcs, grs)]
            a_s, qdbs, ksbs, ess = _hg_intra([v["q"] * e for v, e in zip(vs, e_qs)], [v["k"] for v in vs],
                                             gcs, grs, incl)
            e_gcs = [jnp.exp(gc) for gc in gcs]
            gls = [gc[c - 1:c, :] for gc in gcs]
            e_gls = [jnp.exp(gl) for gl in gls]
            e_ks = [jnp.exp(gl - gc) for gl, gc in zip(gls, gcs)]
            normed = [_rms_bwd(oraw_ref[0, r, :], og_ref[...], don_ref[0, r, :]) for r in rows]
            dobs = [do.astype(BF16) for do, _ in normed]
            dabs = [jnp.where(incl, _dot(dob, inp, NT), 0.0).astype(BF16) for dob, inp in zip(dobs, inps)]
            adds = [_dot(dob, (v["q"] * e).astype(BF16), TN) for dob, v, e in zip(dobs, vs, e_gcs)]
            dq_inters = [_dot(dob, st.astype(BF16)) * e for dob, st, e in zip(dobs, sts, e_gcs)]
            dqds = [jnp.concatenate([_dot(dab[sub * HG_SUB:(sub + 1) * HG_SUB, :], ksb[sub]) for sub in subs], axis=0)
                    for dab, ksb in zip(dabs, ksbs)]
            dkss = [[_dot(dab[sub * HG_SUB:(sub + 1) * HG_SUB, :], qdb[sub * HG_SUB:(sub + 1) * HG_SUB, :], TN)
                     for sub in subs] for dab, qdb in zip(dabs, qdbs)]
            dsts = []
            for e_gl, add in zip(e_gls, adds):
                dsts.append(dst)
                dst = dst * e_gl + add
            dstbs = [d.astype(BF16) for d in dsts]
            dis = [_dot(a.astype(BF16), dob, TN) + _dot((v["k"] * e_k).astype(BF16), dstb, NT)
                   for a, dob, v, e_k, dstb in zip(a_s, dobs, vs, e_ks, dstbs)]
            dk_inters = [_dot(inp, dstb) * e_k for inp, dstb, e_k in zip(inps, dstbs, e_ks)]
            for u, r in enumerate(rows):
                v, q, k = vs[u], vs[u]["q"], vs[u]["k"]
                dk, dgc_k = dk_inters[u], jnp.zeros((c, HEAD_DIM), F32)
                for sub in subs:
                    dk = dk + dkss[u][sub] * ess[u][sub]
                    dgc_k = dgc_k + dkss[u][sub] * ksbs[u][sub].astype(F32)
                dq = dqds[u] * e_qs[u] + dq_inters[u]
                at_last = (jnp.sum(k * dk_inters[u], axis=0, keepdims=True)
                           + e_gls[u] * jnp.sum(sts[u] * dsts[u], axis=0, keepdims=True))
                dgc = ((qdbs[u].astype(F32) * dqds[u] - dgc_k) + (q * dq_inters[u] - k * dk_inters[u])
                       + jnp.where(last_row, at_last, 0.0))
                dlf_f = _dot_exact(dgc, upper, left=True) / v["f"]
                d_ref[0, 0, r, :] = dq * (v["sq"] * (1.0 + v["qp"] * (1.0 - v["sq"])))
                d_ref[1, 0, r, :] = (dlf_f - dk) * (oml * v["sf"] * v["sfn"])
                d_ref[2, 0, r, :] = dis[u]
                dlb_acc = dlb_acc + jnp.sum((dlf_f - dk) * v["sfn"], axis=0, keepdims=True)
                dog_acc = dog_acc + jnp.sum(normed[u][1], axis=0, keepdims=True)
            return dst, dog_acc, dlb_acc

        zero = jnp.zeros((1, HEAD_DIM), F32)
        _, dog, dlb = lax.fori_loop(0, ngroup, group, (jnp.zeros((HEAD_DIM, HEAD_DIM), F32), zero, zero))
        dog_ref[0, 0] = dog
        dlb_ref[0, 0] = dlb

    head_row = jax.ShapeDtypeStruct((b, N_HEADS, 1, HEAD_DIM), F32)
    return _pcall_riding(
        body, rider, (proj4, proj4, proj4, don3, oraw3, states, lbl4, o_gain, dproj4),
        name="hgrn2_bwd", grid=(b, N_HEADS),
        in_specs=[_head_spec(s, 0), _head_spec(s, 1), _head_spec(s, 2), _seq_spec(s), _seq_spec(s),
                  _state_spec(nchunk), pl.BlockSpec((2, 1, 1, HEAD_DIM), lambda b, h: (0, h, 0, 0)), _GAIN_SPEC,
                  pl.BlockSpec(memory_space=pl.ANY)],
        out_specs=[pl.BlockSpec((3, 1, s, HEAD_DIM), lambda b, h: (0, b, 0, h)), _HEAD_ROW_SPEC, _HEAD_ROW_SPEC],
        out_shape=[jax.ShapeDtypeStruct(dproj4.shape, F32), head_row, head_row],
        input_output_aliases={8: 0}, semantics=("parallel", "parallel"))


def local_step(x, target, sb_norm, wsi, sb_q_gain, sb_k_gain, hg_o_gain, hg_lb_logits, wso_mine, whi_mine, who_mine,
               hg_norm_mine):
    b, s, _ = x.shape
    t = b * s
    x2 = x.reshape(t, D_MODEL)
    tg2 = target.reshape(t, D_MODEL)
    lbl4 = hg_lb_logits.reshape(2, N_HEADS, 1, HEAD_DIM)
    four = (4, b, s, D_MODEL)
    three = (b, s, D_MODEL)
    rows8 = (N_DEV, W_ROWS, D_MODEL)

    (proj0, u0, qkv0), (wso, who) = rms_inproj(x2, sb_norm, wsi, "sb_inproj",
                                               _Rider([wso_mine, who_mine], scatter=False),
                                               qk_gains=(sb_q_gain, sb_k_gain))
    qkv0 = qkv0.reshape(3, b, s, D_MODEL)
    wso = wso.reshape(D_MODEL, D_MODEL)
    who = who.reshape(D_MODEL, D_MODEL)
    o0, (whi, hgn) = sb_attn_fwd(qkv0, _Rider([whi_mine, hg_norm_mine], scatter=False))
    hg_norm_full = hgn[:, 0, :].reshape(1, D_MODEL)
    o0 = o0.reshape(t, D_MODEL)
    h1 = gate_outproj(o0, proj0, wso, x2, None, "sb_outproj")
    (proj1, u1), _ = rms_inproj(h1, hg_norm_full, whi, "hg_inproj")
    o1, o1_raw, states = hgrn2_fwd(proj1.reshape(four), lbl4, hg_o_gain)
    o1 = o1.reshape(t, D_MODEL)
    dh2, loss_parts = gate_outproj(o1, proj1, who, h1, tg2, "hg_outproj_loss")

    do1, dproj1, g_who = outproj_bwd(dh2, who, o1, proj1, "hg_outproj_bwd")
    (dproj1, g_og, g_lb), (p_who,) = hgrn2_bwd(proj1.reshape(four), do1.reshape(three), o1_raw, states, lbl4,
                                               hg_o_gain, dproj1.reshape(four),
                                               _Rider([g_who.reshape(rows8)], scatter=True))
    dproj1 = dproj1.reshape(4, t, D_MODEL)
    (dh1, g_hgn), _ = inproj_bwd_dx(dproj1, whi, h1, hg_norm_full, dh2, "hg_inproj_bwd_dx")
    g_whi = inproj_bwd_dw(u1, dproj1, "hg_inproj_bwd_dw", out_dtype=BF16)

    do0, dproj0, g_wso = outproj_bwd(dh1, wso, o0, proj0, "sb_outproj_bwd")
    (dproj0, g_qg, g_kg), (p_whi, p_wso) = sb_attn_bwd(qkv0, proj0.reshape(four), do0.reshape(three), o0.reshape(three),
                                                       sb_q_gain, sb_k_gain, dproj0.reshape(four),
                                                       _Rider([g_whi, g_wso.reshape(rows8)], scatter=True))
    dproj0 = dproj0.reshape(4, t, D_MODEL)
    g_wsi = inproj_bwd_dw(u0, dproj0, "sb_inproj_bwd_dw", out_dtype=BF16)
    (gx, g_sbn), (p_wsi,) = inproj_bwd_dx(dproj0, wsi, x2, sb_norm, dh1, "sb_inproj_bwd_dx",
                                          _Rider([sum_within_chip(g_wsi)], scatter=True, chips=True))
    return dict(loss_parts=loss_parts, gx=gx.reshape(three), p_wsi=p_wsi, p_wso=p_wso, p_whi=p_whi, p_who=p_who,
                g_sbn=g_sbn, g_hgn=g_hgn, g_qg=g_qg, g_kg=g_kg, g_og=g_og, g_lb=g_lb)


def sum_within_chip(g):
    _, r, c_ = g.shape
    chips = N_DEV // 2

    def swap(g_ref, got_ref, send_sems, recv_sems):
        x, y, c = _mesh_pos()
        copies = [pltpu.make_async_remote_copy(
            src_ref=g_ref.at[2 * q + 1 - c], dst_ref=got_ref.at[q], send_sem=send_sems.at[q], recv_sem=recv_sems.at[q],
            device_id=(x, y, 1 - c), device_id_type=MESH) for q in range(chips)]
        for cp in copies:
            cp.start()
        for cp in copies:
            cp.wait_recv()
            cp.wait_send()

    got = _pcall(swap, name="swap_with_sibling", in_specs=[_ANY], out_specs=_ANY,
                 out_shape=jax.ShapeDtypeStruct((chips, r, c_), g.dtype),
                 scratch_shapes=[pltpu.SemaphoreType.DMA((chips,)), pltpu.SemaphoreType.DMA((chips,))])(g)

    def add(g_ref, got_ref, out_ref):
        mine = g_ref[0, lax.axis_index("c")]
        out_ref[0] = (mine.astype(F32) + got_ref[0].astype(F32)).astype(out_ref.dtype)

    return _pcall(
        add, name="add_sibling_partials", grid=(chips,),
        in_specs=[pl.BlockSpec((1, 2, r, c_), lambda q: (q, 0, 0, 0)), pl.BlockSpec((1, r, c_), lambda q: (q, 0, 0))],
        out_specs=pl.BlockSpec((1, r, c_), lambda q: (q, 0, 0)),
        out_shape=jax.ShapeDtypeStruct((chips, r, c_), g.dtype),
        compiler_params=_params("parallel"),
    )(g.reshape(chips, 2, r, c_), got)


def _two_level_gather(src, out, send_sems, recv_sems, local_sem, pos):
    x, y, c = pos
    me, sibling = (x, y, c), (x, y, 1 - c)
    chips = [(1 - x, y), (x, 1 - y), (1 - x, 1 - y)]

    def copy(k, block, to, source=None):
        slot = out.at[_linear(block)]
        return pltpu.make_async_remote_copy(
            src_ref=slot if source is None else source, dst_ref=slot, send_sem=send_sems.at[k],
            recv_sem=recv_sems.at[k], device_id=to, device_id_type=MESH)

    mine = pltpu.make_async_copy(src, out.at[_linear(me)], local_sem)
    mine.start()
    first = [copy(0, me, sibling, src)] + [copy(1 + j, me, (*chip, c), src) for j, chip in enumerate(chips)]
    for cp in first:
        cp.start()
    passed = [copy(4 + j, (*chip, c), sibling) for j, chip in enumerate(chips)]
    for j, chip in enumerate(chips):
        copy(1 + j, (*chip, c), me).wait_recv()
        passed[j].start()
    copy(0, sibling, me).wait_recv()
    for j, chip in enumerate(chips):
        copy(4 + j, (*chip, 1 - c), me).wait_recv()
    for cp in first + passed:
        cp.wait_send()
    mine.wait()


def gather_first_weights(w_si, w_so, w_hi, w_ho, hg_norm):
    def body(si_ref, so_ref, hi_ref, ho_ref, hn_ref, o_si, so_b, hi_b, ho_b, hn_b, si_b, send_sems, recv_sems, local_sem):
        for src, buf in ((si_ref, si_b), (so_ref, so_b), (hi_ref, hi_b), (ho_ref, ho_b)):
            buf[...] = src[...].astype(BF16)
        hn_b[...] = jnp.broadcast_to(hn_ref[...], hn_b.shape)
        _two_level_gather(si_b, o_si, send_sems, recv_sems, local_sem, _mesh_pos())

    return _pcall(
        body, name="gather_first_weights",
        in_specs=[_VMEM] * 5, out_specs=[_ANY] + [_VMEM] * 4,
        out_shape=[jax.ShapeDtypeStruct((N_DEV,) + w_si.shape, BF16), jax.ShapeDtypeStruct(w_so.shape, BF16),
                   jax.ShapeDtypeStruct(w_hi.shape, BF16), jax.ShapeDtypeStruct(w_ho.shape, BF16),
                   jax.ShapeDtypeStruct((8, HEAD_DIM), F32)],
        scratch_shapes=[pltpu.VMEM(w_si.shape, BF16), pltpu.SemaphoreType.DMA((N_PEERS,)),
                        pltpu.SemaphoreType.DMA((N_PEERS,)), pltpu.SemaphoreType.DMA],
        compiler_params=pltpu.CompilerParams(vmem_limit_bytes=VMEM_LIMIT_BYTES),
    )(w_si, w_so, w_hi, w_ho, hg_norm)


def _adamw(w, g, m, v):
    m = ADAM_B1 * m + (1.0 - ADAM_B1) * g
    v = ADAM_B2 * v + (1.0 - ADAM_B2) * (g * g)
    m_hat = m / (1.0 - ADAM_B1 ** ADAM_STEP)
    v_hat = v / (1.0 - ADAM_B2 ** ADAM_STEP)
    delta = -ADAM_LR * (m_hat / (jnp.sqrt(v_hat) + ADAM_EPS) + ADAM_WD * w)
    return delta, m, v


def reduce_adamw(parts, w, m, v, name):
    n, r, c = parts.shape
    tr = _row_tile(r, 256)

    def body(p_ref, w_ref, m_ref, v_ref, g_ref, d_ref, m2_ref, v2_ref):
        g = p_ref[0].astype(F32)
        for sender in range(1, n):
            g = g + p_ref[sender].astype(F32)
        g_ref[...] = g
        d_ref[...], m2_ref[...], v2_ref[...] = _adamw(w_ref[...], g, m_ref[...], v_ref[...])

    tile = pl.BlockSpec((tr, c), lambda i: (i, 0))
    return _pcall(
        body, name=name, grid=(r // tr,),
        in_specs=[pl.BlockSpec((n, tr, c), lambda i: (0, i, 0)), tile, tile, tile],
        out_specs=[tile] * 4, out_shape=[jax.ShapeDtypeStruct((r, c), F32)] * 4,
        compiler_params=_params("parallel"),
    )(parts, w, m, v)


PACK_ROWS = 32
ROW_SBN, ROW_HGN, ROW_LB, ROW_QG, ROW_KG, ROW_OG, ROW_LOSS = 0, 8, 16, 24, 25, 26, 27


def small_update(g_sbn, g_hgn, g_lb, g_qg, g_kg, g_og, loss_parts, small):
    n_in = 7 + len(small)

    def body(*refs):
        sbn_ref, hgn_ref, lb_ref, qg_ref, kg_ref, og_ref, loss_ref = refs[:7]
        wmv = refs[7:n_in]
        outs = refs[n_in:n_in + 25]
        pack, gath, tot, send_sems, recv_sems, local_sems = refs[n_in + 25:]
        pos = _mesh_pos()
        me = _linear(pos)
        pack[...] = jnp.zeros_like(pack)
        pack[ROW_SBN:ROW_SBN + 8, :] = sbn_ref[...]
        pack[ROW_HGN:ROW_HGN + 8, :] = hgn_ref[...]
        pack[ROW_LB:ROW_LB + 8, :] = jnp.sum(lb_ref[...], axis=0)
        pack[ROW_QG:ROW_QG + 1, :] = jnp.sum(qg_ref[...], axis=0, keepdims=True)
        pack[ROW_KG:ROW_KG + 1, :] = jnp.sum(kg_ref[...], axis=0, keepdims=True)
        pack[ROW_OG:ROW_OG + 1, :] = jnp.sum(og_ref[...], axis=0, keepdims=True)
        pack[ROW_LOSS:ROW_LOSS + 1, :] = jnp.sum(loss_ref[...], axis=0)[0:1, :]
        _exchange_start([((lambda p: pack), gath)], send_sems, recv_sems, local_sems, pos)
        _exchange_wait([((lambda p: pack), gath)], send_sems, recv_sems, local_sems, pos)
        total = gath[0]
        for dev in range(1, N_DEV):
            total = total + gath[dev]
        tot[...] = total
        outs[0][...] = jnp.broadcast_to(tot[ROW_LOSS:ROW_LOSS + 1, :], (8, HEAD_DIM))
        l0 = wmv[15][0:8, :]
        l1 = wmv[15][8:16, :]
        p1, p0 = _sigmoid_pair(l1 - l0)
        d_l1 = p0 * p1 * tot[ROW_LB:ROW_LB + 8, :]
        grads = [tot[ROW_SBN:ROW_SBN + 8, :], tot[ROW_QG:ROW_QG + 1, :], tot[ROW_KG:ROW_KG + 1, :],
                 tot[pl.ds(ROW_HGN + me, 1), :], tot[ROW_OG:ROW_OG + 1, :],
                 jnp.concatenate([-d_l1, d_l1], axis=0)]
        for i, g in enumerate(grads):
            w_ref, m_ref, v_ref = wmv[3 * i:3 * i + 3]
            o = outs[1 + 4 * i:5 + 4 * i]
            o[0][...] = g
            o[1][...], o[2][...], o[3][...] = _adamw(w_ref[...], g, m_ref[...], v_ref[...])

    out_shape = [jax.ShapeDtypeStruct((8, HEAD_DIM), F32)]
    for i in range(6):
        out_shape += [jax.ShapeDtypeStruct(small[3 * i].shape, F32)] * 4
    return _pcall(
        body, name="small_update",
        in_specs=[_VMEM] * n_in, out_specs=[_VMEM] * 25, out_shape=out_shape,
        scratch_shapes=[pltpu.VMEM((PACK_ROWS, HEAD_DIM), F32), pltpu.VMEM((N_DEV, PACK_ROWS, HEAD_DIM), F32),
                        pltpu.VMEM((PACK_ROWS, HEAD_DIM), F32)] + _exchange_sems(1),
    )(g_sbn, g_hgn, g_lb, g_qg, g_kg, g_og, loss_parts, *small)


def kernel(x, sb_norm, sb_w_in, sb_q_gain, sb_k_gain, sb_w_out, hg_norm, hg_w_in, hg_o_gain, hg_w_out, hg_lb_logits, loss_target, m_sb_norm, m_sb_w_in, m_sb_q_gain, m_sb_k_gain, m_sb_w_out, m_hg_norm, m_hg_w_in, m_hg_o_gain, m_hg_w_out, m_hg_lb_logits, v_sb_norm, v_sb_w_in, v_sb_q_gain, v_sb_k_gain, v_sb_w_out, v_hg_norm, v_hg_w_in, v_hg_o_gain, v_hg_w_out, v_hg_lb_logits):
    b = x.shape[0]
    wsi, wso_mine, whi_mine, who_mine, hg_norm_mine = gather_first_weights(
        sb_w_in[0], sb_w_out[0], hg_w_in[0], hg_w_out[0], hg_norm)
    r = local_step(x, loss_target, sb_norm, wsi, sb_q_gain, sb_k_gain, hg_o_gain, hg_lb_logits,
                   wso_mine, whi_mine, who_mine, hg_norm_mine)
    big = {}
    for name, p, w, m, v in (("sb_w_in", r["p_wsi"], sb_w_in, m_sb_w_in, v_sb_w_in),
                             ("sb_w_out", r["p_wso"], sb_w_out, m_sb_w_out, v_sb_w_out),
                             ("hg_w_in", r["p_whi"], hg_w_in, m_hg_w_in, v_hg_w_in),
                             ("hg_w_out", r["p_who"], hg_w_out, m_hg_w_out, v_hg_w_out)):
        big[name] = [o[None] for o in reduce_adamw(p, w[0], m[0], v[0], "adamw_" + name)]

    def rows8(a):
        return a.reshape(8, HEAD_DIM)

    def rows16(a):
        return a.reshape(16, HEAD_DIM)

    small_in = [rows8(sb_norm), rows8(m_sb_norm), rows8(v_sb_norm),
                sb_q_gain, m_sb_q_gain, v_sb_q_gain,
                sb_k_gain, m_sb_k_gain, v_sb_k_gain,
                hg_norm, m_hg_norm, v_hg_norm,
                hg_o_gain, m_hg_o_gain, v_hg_o_gain,
                rows16(hg_lb_logits), rows16(m_hg_lb_logits), rows16(v_hg_lb_logits)]
    so = small_update(rows8(r["g_sbn"]), rows8(r["g_hgn"]), r["g_lb"].reshape(b, N_HEADS, HEAD_DIM),
                      r["g_qg"].reshape(b * N_HEADS, HEAD_DIM), r["g_kg"].reshape(b * N_HEADS, HEAD_DIM),
                      r["g_og"].reshape(b * N_HEADS, HEAD_DIM), r["loss_parts"], small_in)
    loss = so[0][0, 0]
    shapes = {"sb_norm": (1, D_MODEL), "sb_q_gain": (1, HEAD_DIM), "sb_k_gain": (1, HEAD_DIM),
              "hg_norm": (1, HEAD_DIM), "hg_o_gain": (1, HEAD_DIM), "hg_lb_logits": (2, D_MODEL)}
    small = {}
    for i, name in enumerate(("sb_norm", "sb_q_gain", "sb_k_gain", "hg_norm", "hg_o_gain", "hg_lb_logits")):
        small[name] = [o.reshape(shapes[name]) for o in so[1 + 4 * i:5 + 4 * i]]
    order = ("sb_norm", "sb_w_in", "sb_q_gain", "sb_k_gain", "sb_w_out",
             "hg_norm", "hg_w_in", "hg_o_gain", "hg_w_out", "hg_lb_logits")
    res = {**big, **small}
    return (loss, r["gx"]) + tuple(res[n][j] for j in range(4) for n in order)
```

```python
import functools
import math

import jax
import jax.numpy as jnp
from jax import lax
from jax.experimental import pallas as pl
from jax.experimental.pallas import tpu as pltpu

F32 = jnp.float32
BF16 = jnp.bfloat16

N_DEV = 8
D_MODEL = 1024
N_HEADS = 8
HEAD_DIM = 128
RMS_EPS = 1e-6
ATTN_BLOCK = 128
HG_CHUNK = 64
HG_SUB = 16
HG_UNROLL_FWD = 16
HG_UNROLL_BWD = 16
EXP_CLAMP = 80.0
SB_HEADS_PER_STEP = 2
SB_QBLOCKS_PER_STEP = 4
SB_GROUP_COLS = SB_HEADS_PER_STEP * 128
SB_TOP_ROWS = 32
SB_LOG_WEIGHT_FLOOR = -104.0
VMEM_LIMIT_BYTES = 48 * 1024 * 1024
W_COLS = 4 * D_MODEL // N_DEV
W_ROWS = D_MODEL // N_DEV

ADAM_LR = 0.001
ADAM_B1 = 0.9
ADAM_B2 = 0.999
ADAM_EPS = 1e-08
ADAM_WD = 0.01
ADAM_STEP = 10

NT = (((1,), (1,)), ((), ()))
TN = (((0,), (0,)), ((), ()))
NN = (((1,), (0,)), ((), ()))


def _pcall(body, *, name, **kw):
    return pl.pallas_call(body, name=name, **kw)


def _params(*sem):
    return pltpu.CompilerParams(dimension_semantics=sem, vmem_limit_bytes=VMEM_LIMIT_BYTES)


def _dot(a, b, dims=NN):
    return lax.dot_general(a, b, dims, preferred_element_type=F32)


def _dot_exact(a, m, dims=NN, left=False):
    hi = a.astype(BF16)
    lo = (a - hi.astype(F32)).astype(BF16)
    if left:
        return _dot(m, hi, dims) + _dot(m, lo, dims)
    return _dot(hi, m, dims) + _dot(lo, m, dims)


def _split(a):
    hi = a.astype(BF16)
    return hi, (a - hi.astype(F32)).astype(BF16)


def _dot3(a, b, dims=NN):
    return _dot(a[0], b[0], dims) + (_dot(a[0], b[1], dims) + _dot(a[1], b[0], dims))


def _sigmoid(x):
    return 1.0 / (1.0 + jnp.exp(-x))


def _sigmoid_pair(x):
    e = jnp.exp(-jnp.abs(x))
    big = 1.0 / (1.0 + e)
    small = e * big
    pos = x >= 0
    return jnp.where(pos, big, small), jnp.where(pos, small, big)


def _rms_scale(x):
    return lax.rsqrt(jnp.mean(x * x, axis=-1, keepdims=True) + RMS_EPS)


def _row_tile(t, want):
    return want if t % want == 0 else t


MESH = pl.DeviceIdType.MESH
N_PEERS = N_DEV - 1
_ANY = pl.BlockSpec(memory_space=pl.ANY)
_VMEM = pl.BlockSpec(memory_space=pltpu.VMEM)


def _mesh_pos():
    return lax.axis_index("x"), lax.axis_index("y"), lax.axis_index("c")


def _linear(pos):
    return 4 * pos[0] + 2 * pos[1] + pos[2]


def _peer(pos, k):
    flips = ((k + 1) >> 2 & 1, (k + 1) >> 1 & 1, (k + 1) & 1)
    return tuple(1 - p if f else p for p, f in zip(pos, flips))


def _chip(pos):
    return 2 * pos[0] + pos[1]


ALL_PEERS = tuple(range(N_PEERS))
SAME_CORE_PEERS = (1, 3, 5)


def _exchange_copies(pairs, send_sems, recv_sems, local_sems, pos, landing, peers, slot):
    me = slot(pos)
    local, remote = [], []
    for a, (src_of, dst) in enumerate(pairs):
        local.append(pltpu.make_async_copy(src_of(pos), dst.at[me], local_sems.at[a]))
        for k in peers:
            peer = _peer(pos, k)
            remote.append(pltpu.make_async_remote_copy(
                src_ref=src_of(peer), dst_ref=dst.at[slot(peer) if landing else me],
                send_sem=send_sems.at[a, k], recv_sem=recv_sems.at[a, k], device_id=peer, device_id_type=MESH))
    return local, remote


def _exchange_start(pairs, send_sems, recv_sems, local_sems, pos, peers=ALL_PEERS, slot=_linear):
    local, sent = _exchange_copies(pairs, send_sems, recv_sems, local_sems, pos, False, peers, slot)
    for copy in local + sent:
        copy.start()


def _exchange_wait(pairs, send_sems, recv_sems, local_sems, pos, peers=ALL_PEERS, slot=_linear):
    local, landed = _exchange_copies(pairs, send_sems, recv_sems, local_sems, pos, True, peers, slot)
    for copy in landed:
        copy.wait_recv()
        copy.wait_send()
    for copy in local:
        copy.wait()


def _exchange_sems(n):
    return [pltpu.SemaphoreType.DMA((n, N_PEERS)), pltpu.SemaphoreType.DMA((n, N_PEERS)),
            pltpu.SemaphoreType.DMA((n,))]


class _Rider:
    def __init__(self, arrays, scatter, chips=False):
        self.arrays = list(arrays)
        self.scatter = scatter
        self.peers = SAME_CORE_PEERS if chips else ALL_PEERS
        self.slot = _chip if chips else _linear
        self.out_shapes = [jax.ShapeDtypeStruct(a.shape if scatter else (N_DEV,) + a.shape, a.dtype)
                           for a in self.arrays]

    def pairs(self, in_refs, out_refs):
        if self.scatter:
            return [((lambda pos, r=r: r.at[self.slot(pos)]), o) for r, o in zip(in_refs, out_refs)]
        return [((lambda pos, r=r: r), o) for r, o in zip(in_refs, out_refs)]


def _pcall_riding(body, rider, args, *, name, grid, in_specs, out_specs, out_shape, semantics, scratch_shapes=(),
                  input_output_aliases=None):
    aliases = input_output_aliases or {}
    if rider is None:
        outs = _pcall(body, name=name, grid=grid, in_specs=list(in_specs), out_specs=list(out_specs),
                      out_shape=list(out_shape), scratch_shapes=list(scratch_shapes), input_output_aliases=aliases,
                      compiler_params=_params(*semantics))(*args)
        return list(outs), []
    n_in, n_out, n_scr, n_r = len(in_specs), len(out_specs), len(scratch_shapes), len(rider.arrays)

    def riding(*refs):
        ins, refs = refs[:n_in], refs[n_in:]
        rider_in, refs = refs[:n_r], refs[n_r:]
        outs, refs = refs[:n_out], refs[n_out:]
        rider_out, refs = refs[:n_r], refs[n_r:]
        scratch, sems = refs[:n_scr], refs[n_scr:]
        pairs = rider.pairs(rider_in, rider_out)
        first = functools.reduce(jnp.logical_and, [pl.program_id(a) == 0 for a in range(len(grid))])
        last = functools.reduce(jnp.logical_and, [pl.program_id(a) == g - 1 for a, g in enumerate(grid)])

        @pl.when(first)
        def _():
            _exchange_start(pairs, *sems, _mesh_pos(), rider.peers, rider.slot)

        body(*ins, *outs, *scratch)

        @pl.when(last)
        def _():
            _exchange_wait(pairs, *sems, _mesh_pos(), rider.peers, rider.slot)

    outs = _pcall(riding, name=name, grid=grid, in_specs=list(in_specs) + [_ANY] * n_r,
                  out_specs=list(out_specs) + [_ANY] * n_r, out_shape=list(out_shape) + rider.out_shapes,
                  scratch_shapes=list(scratch_shapes) + _exchange_sems(n_r), input_output_aliases=aliases,
                  compiler_params=_params(*(("arbitrary",) * len(grid))))(*args, *rider.arrays)
    return list(outs[:n_out]), list(outs[n_out:])


def rms_inproj(x2, gain, wg, name, rider=None, qk_gains=None):
    t = x2.shape[0]
    tm = _row_tile(t, 256)
    with_qkv = qk_gains is not None

    def body(x_ref, g_ref, w_ref, *rest):
        if with_qkv:
            qg_ref, kg_ref, proj_ref, ut_ref, qkv_ref = rest
            head_gain = (qg_ref, kg_ref)
        else:
            proj_ref, ut_ref = rest
        x = x_ref[...]
        u = x * _rms_scale(x) * g_ref[...]
        ut_ref[...] = u.T.astype(BF16)
        u = u.astype(BF16)
        for p in range(N_DEV):
            part, lo = p // 2, (p % 2) * W_COLS
            res = _dot(u, w_ref[p])
            proj_ref[part, :, lo:lo + W_COLS] = res
            if with_qkv and part < 3:
                for h in range(W_COLS // HEAD_DIM):
                    y = res[:, h * HEAD_DIM:(h + 1) * HEAD_DIM]
                    if part < 2:
                        y = y * _rms_scale(y) * head_gain[part][...]
                    qkv_ref[part, :, lo + h * HEAD_DIM:lo + (h + 1) * HEAD_DIM] = y.astype(BF16)

    vec = pl.BlockSpec((1, D_MODEL), lambda i: (0, 0))
    in_specs = [pl.BlockSpec((tm, D_MODEL), lambda i: (i, 0)), vec,
                pl.BlockSpec((N_DEV, D_MODEL, W_COLS), lambda i: (0, 0, 0))]
    out_specs = [pl.BlockSpec((4, tm, D_MODEL), lambda i: (0, i, 0)), pl.BlockSpec((D_MODEL, tm), lambda i: (0, i))]
    out_shape = [jax.ShapeDtypeStruct((4, t, D_MODEL), F32), jax.ShapeDtypeStruct((D_MODEL, t), BF16)]
    args = (x2, gain, wg)
    if with_qkv:
        in_specs += [pl.BlockSpec((1, HEAD_DIM), lambda i: (0, 0))] * 2
        out_specs.append(pl.BlockSpec((3, tm, D_MODEL), lambda i: (0, i, 0)))
        out_shape.append(jax.ShapeDtypeStruct((3, t, D_MODEL), BF16))
        args += tuple(qk_gains)
    return _pcall_riding(body, rider, args, name=name, grid=(t // tm,), in_specs=in_specs, out_specs=out_specs,
                         out_shape=out_shape, semantics=("parallel",))


def gate_outproj(o2, proj, w_out, resid, target, name):
    t = o2.shape[0]
    tm = _row_tile(t, 512)
    with_loss = target is not None

    def body(o_ref, gate_ref, w_ref, r_ref, *rest):
        g = gate_ref[0]
        og = (o_ref[...] * (g * _sigmoid(g))).astype(BF16)
        h = r_ref[...] + _dot(og, w_ref[...])
        if with_loss:
            t_ref, dh_ref, loss_ref = rest
            err = h - t_ref[...]
            dh_ref[...] = err * (1.0 / D_MODEL)
            part = 0.5 * jnp.sum(jnp.mean(err * err, axis=-1, keepdims=True))
            loss_ref[...] = jnp.full(loss_ref.shape, part, F32)
        else:
            (h_ref,) = rest
            h_ref[...] = h

    row = pl.BlockSpec((tm, D_MODEL), lambda i: (i, 0))
    in_specs = [row,
                pl.BlockSpec((1, tm, D_MODEL), lambda i: (3, i, 0)),
                pl.BlockSpec((D_MODEL, D_MODEL), lambda i: (0, 0)),
                row]
    args = [o2, proj, w_out, resid]
    if with_loss:
        in_specs.append(row)
        args.append(target)
        out_specs = [row, pl.BlockSpec((1, 8, 128), lambda i: (i, 0, 0))]
        out_shape = [jax.ShapeDtypeStruct((t, D_MODEL), F32),
                     jax.ShapeDtypeStruct((t // tm, 8, 128), F32)]
    else:
        out_specs = row
        out_shape = jax.ShapeDtypeStruct((t, D_MODEL), F32)
    return _pcall(body, name=name, grid=(t // tm,), in_specs=in_specs, out_specs=out_specs,
                  out_shape=out_shape, compiler_params=_params("parallel"))(*args)


def _head_spec(s, part):
    return pl.BlockSpec((1, 1, s, HEAD_DIM), lambda b, h: (part, b, 0, h))


def _seq_spec(s):
    return pl.BlockSpec((1, s, HEAD_DIM), lambda b, h: (b, 0, h))


_GAIN_SPEC = pl.BlockSpec((1, HEAD_DIM), lambda b, h: (0, 0))
_HEAD_ROW_SPEC = pl.BlockSpec((1, 1, 1, HEAD_DIM), lambda b, h: (b, h, 0, 0))


def _sb_group_spec(s, part):
    return pl.BlockSpec((1, 1, s, SB_GROUP_COLS), lambda b, g: (part, b, 0, g))


def _sb_seq_group_spec(s):
    return pl.BlockSpec((1, s, SB_GROUP_COLS), lambda b, g: (b, 0, g))


_SB_GROUP_ROW_SPEC = pl.BlockSpec((1, SB_HEADS_PER_STEP, 1, HEAD_DIM), lambda b, g: (b, g, 0, 0))


def _sb_chains(m):
    return [(h, m * SB_QBLOCKS_PER_STEP + r) for h in range(SB_HEADS_PER_STEP) for r in range(SB_QBLOCKS_PER_STEP)]


def _sb_logits(qi, kj):
    return _dot(qi, kj, NT) * (HEAD_DIM ** -0.5)


def _sb_scores(z, diag, live, tri_lt):
    soft = jnp.log(1.0 + jnp.exp(-jnp.abs(z)))
    valid = jnp.logical_and(live, jnp.logical_or(jnp.logical_not(diag), tri_lt))
    log_skip = jnp.where(valid, -(jnp.maximum(z, 0.0) + soft), 0.0)
    log_beta = jnp.minimum(z, 0.0) - soft
    return log_skip, log_beta, valid


def _sb_keys_left(chains, watch, state):
    done, carries = state[0], state[1]
    worst = None
    for (_, i), c in zip(chains, carries):
        c = jnp.where(done <= i, c[watch], -jnp.inf)
        worst = c if worst is None else jnp.maximum(worst, c)
    return jnp.logical_and(done <= chains[-1][1],
                           jnp.logical_or(done == 0, jnp.max(worst) > SB_LOG_WEIGHT_FLOOR))


def _sb_key_rows(i, done):
    j = i - done
    return pl.ds(pl.multiple_of(jnp.maximum(j, 0) * ATTN_BLOCK, ATTN_BLOCK), ATTN_BLOCK), j >= 0


def _sb_head(ref, h, rows):
    return ref[0, 0, rows, h * HEAD_DIM:(h + 1) * HEAD_DIM]


def _sb_rows(i, offset, count):
    return pl.ds(pl.multiple_of(jnp.maximum(i, 0) * ATTN_BLOCK + offset, 8), count)


def sb_attn_fwd(qkv4, rider=None):
    _, b, s, _ = qkv4.shape
    blk, top = ATTN_BLOCK, SB_TOP_ROWS
    ngroups = s // blk // SB_QBLOCKS_PER_STEP
    assert ngroups * SB_QBLOCKS_PER_STEP * blk == s

    def body(q_ref, k_ref, v_ref, o_ref):
        row = lax.broadcasted_iota(jnp.int32, (blk, blk), 0)
        col = lax.broadcasted_iota(jnp.int32, (blk, blk), 1)
        tri_lt = col < row
        suffix = (row > col).astype(BF16)

        def step(items):
            where = [_sb_key_rows(i, done) for _, i, done, _, _, _, _ in items]
            zs = [_sb_logits(q, _sb_head(k_ref, h, rows)) for (h, _, _, q, _, _, _), (rows, _) in zip(items, where)]
            scored = [_sb_scores(z, done == 0, live, mask)
                      for z, (_, _, done, _, mask, _, _), (_, live) in zip(zs, items, where)]
            afters = [_dot_exact(log_skip, suffix) for log_skip, _, _ in scored]
            ws = [jnp.where(valid, jnp.exp(log_beta + after + c), 0.0).astype(BF16)
                  for (_, log_beta, valid), after, (_, _, _, _, _, c, _) in zip(scored, afters, items)]
            accs = [acc + _dot(w, _sb_head(v_ref, h, rows))
                    for (h, _, _, _, _, _, acc), (rows, _), w in zip(items, where, ws)]
            cs = [c + jnp.sum(log_skip, axis=1, keepdims=True)
                  for (log_skip, _, _), (_, _, _, _, _, c, _) in zip(scored, items)]
            return cs, accs

        def top_items(chains, done, cs, accs):
            return [(h, i, done, _sb_head(q_ref, h, _sb_rows(i, 0, top)), tri_lt[:top, :], c, acc)
                    for (h, i), c, acc in zip(chains, cs, accs)]

        def finish_tops(chains, state):
            def k_step(state):
                done, cs, accs = state
                cs, accs = step(top_items(chains, done, cs, accs))
                return done + 1, tuple(cs), tuple(accs)

            _, _, accs = lax.while_loop(functools.partial(_sb_keys_left, chains, slice(0, top)), k_step, state)
            for (h, i), acc in zip(chains, accs):
                o_ref[0, _sb_rows(i, 0, top), h * HEAD_DIM:(h + 1) * HEAD_DIM] = acc

        def q_group(m, before):
            chains, chains_before = _sb_chains(m), _sb_chains(m - 1)
            qis = [_sb_head(q_ref, h, _sb_rows(i, 0, blk)) for h, i in chains]
            n = len(chains)

            def items_of(done, cs, accs):
                return [(h, i, done, q, tri_lt, c, acc) for (h, i), q, c, acc in zip(chains, qis, cs, accs)]

            def k_step(state):
                done, cs, accs = state
                cs, accs = step(items_of(done, cs, accs))
                return done + 1, tuple(cs), tuple(accs)

            done_b, cs_b, accs_b = before
            cs0, accs0 = step(items_of(jnp.int32(0), (jnp.zeros((blk, 1), F32),) * n,
                                       (jnp.zeros((blk, HEAD_DIM), F32),) * n)
                              + top_items(chains_before, done_b, cs_b, accs_b))
            done, cs, accs = lax.while_loop(functools.partial(_sb_keys_left, chains, slice(top, blk)), k_step,
                                            (jnp.int32(1), tuple(cs0[:n]), tuple(accs0[:n])))

            @pl.when(m > 0)
            def _():
                finish_tops(chains_before, (done_b + 1, tuple(cs0[n:]), tuple(accs0[n:])))

            for (h, i), acc in zip(chains, accs):
                o_ref[0, _sb_rows(i, top, blk - top), h * HEAD_DIM:(h + 1) * HEAD_DIM] = acc[top:, :]
            return done, tuple(c[:top, :] for c in cs), tuple(acc[:top, :] for acc in accs)

        n = SB_HEADS_PER_STEP * SB_QBLOCKS_PER_STEP
        nothing = (jnp.int32(0), (jnp.zeros((top, 1), F32),) * n, (jnp.zeros((top, HEAD_DIM), F32),) * n)
        last = lax.fori_loop(0, ngroups, q_group, nothing)
        finish_tops(_sb_chains(ngroups - 1), last)

    (o,), extra = _pcall_riding(
        body, rider, (qkv4, qkv4, qkv4),
        name="sb_attn_fwd", grid=(b, N_HEADS // SB_HEADS_PER_STEP),
        in_specs=[_sb_group_spec(s, 0), _sb_group_spec(s, 1), _sb_group_spec(s, 2)],
        out_specs=[_sb_seq_group_spec(s)],
        out_shape=[jax.ShapeDtypeStruct((b, s, D_MODEL), F32)],
        semantics=("parallel", "parallel"))
    return o, extra


def _hg_masks():
    c = HG_CHUNK
    row = lax.broadcasted_iota(jnp.int32, (c, c), 0)
    col = lax.broadcasted_iota(jnp.int32, (c, c), 1)
    incl = (col <= row)
    lower = incl.astype(BF16)
    before_sub = (col < (row // HG_SUB) * HG_SUB).astype(BF16)
    upper = (col >= row).astype(BF16)
    return incl, lower, before_sub, upper


def _hg_lower_bound(lbl_ref):
    l0 = lbl_ref[0, 0]
    l1 = lbl_ref[1, 0]
    d = l1 - l0
    return _sigmoid_pair(d)


def _hg_gates(qp, fp, lb, oml):
    sq = _sigmoid(qp)
    sf, sfn = _sigmoid_pair(fp)
    f = lb + oml * sf
    return dict(qp=qp, sq=sq, q=qp * sq, sf=sf, sfn=sfn, f=f, k=oml * sfn, logf=jnp.log(f))


def _hg_intra(qds, ks, gcs, grs, incl):
    subs = range(HG_CHUNK // HG_SUB)
    qdbs = [qd.astype(BF16) for qd in qds]
    ess = [[jnp.exp(jnp.minimum(gr[sub * HG_SUB:sub * HG_SUB + 1, :] - gc, EXP_CLAMP)) for sub in subs]
           for gc, gr in zip(gcs, grs)]
    ksbs = [[(k * e).astype(BF16) for e in es] for k, es in zip(ks, ess)]
    rows = [[_dot(qdb[sub * HG_SUB:(sub + 1) * HG_SUB, :], ksb[sub], NT) for sub in subs]
            for qdb, ksb in zip(qdbs, ksbs)]
    a_s = [jnp.where(incl, jnp.concatenate(r, axis=0), 0.0) for r in rows]
    return a_s, qdbs, ksbs, ess


def _hg_group_rows(outer, unroll):
    ns = [outer * unroll + u for u in range(unroll)]
    return ns, [pl.ds(pl.multiple_of(n * HG_CHUNK, HG_CHUNK), HG_CHUNK) for n in ns]


def _state_spec(nchunk):
    return pl.BlockSpec((1, 1, nchunk, HEAD_DIM, HEAD_DIM), lambda b, h: (b, h, 0, 0, 0))


def hgrn2_fwd(proj4, lbl4, o_gain):
    _, b, s, _ = proj4.shape
    nchunk = s // HG_CHUNK
    c = HG_CHUNK
    unroll = math.gcd(nchunk, HG_UNROLL_FWD)

    def body(q_ref, f_ref, i_ref, lbl_ref, og_ref, o_ref, oraw_ref, st_ref):
        incl, lower, before_sub, _ = _hg_masks()
        lb, oml = _hg_lower_bound(lbl_ref)

        def group(outer, st):
            ns, rows = _hg_group_rows(outer, unroll)
            vs = [_hg_gates(q_ref[0, 0, r, :], f_ref[0, 0, r, :], lb, oml) for r in rows]
            inps = [i_ref[0, 0, r, :].astype(BF16) for r in rows]
            gcs = [_dot_exact(v["logf"], lower, left=True) for v in vs]
            grs = [_dot_exact(v["logf"], before_sub, left=True) for v in vs]
            a_s, _, _, _ = _hg_intra([v["q"] * jnp.exp(gc - gr) for v, gc, gr in zip(vs, gcs, grs)],
                                     [v["k"] for v in vs], gcs, grs, incl)
            gls = [gc[c - 1:c, :] for gc in gcs]
            adds = [_dot(inp, (v["k"] * jnp.exp(gl - gc)).astype(BF16), TN)
                    for inp, v, gl, gc in zip(inps, vs, gls, gcs)]
            o_intra = [_dot(a.astype(BF16), inp) for a, inp in zip(a_s, inps)]
            sts = []
            for gl, add in zip(gls, adds):
                sts.append(st)
                st = st * jnp.exp(gl) + add
            outs = [oi + _dot((v["q"] * jnp.exp(gc)).astype(BF16), s0.astype(BF16), NT)
                    for oi, v, gc, s0 in zip(o_intra, vs, gcs, sts)]
            for n, r, s0, o in zip(ns, rows, sts, outs):
                st_ref[0, 0, n] = s0
                oraw_ref[0, r, :] = o
                o_ref[0, r, :] = o * _rms_scale(o) * og_ref[...]
            return st

        lax.fori_loop(0, nchunk // unroll, group, jnp.zeros((HEAD_DIM, HEAD_DIM), F32))

    seq = jax.ShapeDtypeStruct((b, s, D_MODEL), F32)
    return _pcall(
        body, name="hgrn2_fwd", grid=(b, N_HEADS),
        in_specs=[_head_spec(s, 0), _head_spec(s, 1), _head_spec(s, 2),
                  pl.BlockSpec((2, 1, 1, HEAD_DIM), lambda b, h: (0, h, 0, 0)), _GAIN_SPEC],
        out_specs=[_seq_spec(s), _seq_spec(s), _state_spec(nchunk)],
        out_shape=[seq, seq, jax.ShapeDtypeStruct((b, N_HEADS, nchunk, HEAD_DIM, HEAD_DIM), F32)],
        compiler_params=_params("parallel", "parallel"),
    )(proj4, proj4, proj4, lbl4, o_gain)


def outproj_bwd(dh, w_out, o2, proj, name):
    t = dh.shape[0]
    tm = _row_tile(t, 512)

    def body(dh_ref, w_ref, o_ref, gate_ref, do_ref, dproj_ref, dw_ref):
        dhb = dh_ref[...].astype(BF16)
        dog = _dot(dhb, w_ref[...], NT)
        g = gate_ref[0]
        sg = _sigmoid(g)
        silu = g * sg
        o = o_ref[...]
        do_ref[...] = dog * silu
        dproj_ref[0] = dog * o * (sg * (1.0 + g * (1.0 - sg)))
        part = _dot((o * silu).astype(BF16), dhb, TN)

        @pl.when(pl.program_id(0) == 0)
        def _():
            dw_ref[...] = part

        @pl.when(pl.program_id(0) > 0)
        def _():
            dw_ref[...] += part

    row = pl.BlockSpec((tm, D_MODEL), lambda i: (i, 0))
    full = pl.BlockSpec((D_MODEL, D_MODEL), lambda i: (0, 0))
    return _pcall(
        body, name=name, grid=(t // tm,),
        in_specs=[row, full, row, pl.BlockSpec((1, tm, D_MODEL), lambda i: (3, i, 0))],
        out_specs=[row, pl.BlockSpec((1, tm, D_MODEL), lambda i: (3, i, 0)), full],
        out_shape=[jax.ShapeDtypeStruct((t, D_MODEL), F32),
                   jax.ShapeDtypeStruct((4, t, D_MODEL), F32),
                   jax.ShapeDtypeStruct((D_MODEL, D_MODEL), F32)],
        compiler_params=_params("arbitrary"),
    )(dh, w_out, o2, proj)


def inproj_bwd_dx(dproj, wg, x2, gain, dres, name, rider=None):
    t = x2.shape[0]
    tm = _row_tile(t, 512)

    def body(d_ref, w_ref, x_ref, g_ref, r_ref, dx_ref, dg_ref):
        du = jnp.zeros((tm, D_MODEL), F32)
        for p in range(N_DEV):
            cols = slice((p % 2) * W_COLS, (p % 2 + 1) * W_COLS)
            du = du + _dot(d_ref[p // 2, :, cols].astype(BF16), w_ref[p], NT)
        x = x_ref[...]
        r = _rms_scale(x)
        xh = x * r
        a = du * g_ref[...]
        dx_ref[...] = r_ref[...] + r * (a - xh * jnp.mean(a * xh, axis=-1, keepdims=True))
        part = jnp.sum(du * xh, axis=0, keepdims=True)

        @pl.when(pl.program_id(0) == 0)
        def _():
            dg_ref[...] = part

        @pl.when(pl.program_id(0) > 0)
        def _():
            dg_ref[...] += part

    row = pl.BlockSpec((tm, D_MODEL), lambda i: (i, 0))
    vec = pl.BlockSpec((1, D_MODEL), lambda i: (0, 0))
    return _pcall_riding(
        body, rider, (dproj, wg, x2, gain, dres), name=name, grid=(t // tm,),
        in_specs=[pl.BlockSpec((4, tm, D_MODEL), lambda i: (0, i, 0)),
                  pl.BlockSpec((N_DEV, D_MODEL, W_COLS), lambda i: (0, 0, 0)),
                  row, vec, row],
        out_specs=[row, vec],
        out_shape=[jax.ShapeDtypeStruct((t, D_MODEL), F32), jax.ShapeDtypeStruct((1, D_MODEL), F32)],
        semantics=("arbitrary",))


def inproj_bwd_dw(ut, dproj, name, out_dtype):
    t = ut.shape[1]

    def body(ut_ref, d_ref, dw_ref):
        dw_ref[0] = _dot(ut_ref[...], d_ref[0].astype(BF16)).astype(dw_ref.dtype)

    return _pcall(
        body, name=name, grid=(N_DEV,),
        in_specs=[pl.BlockSpec((D_MODEL, t), lambda j: (0, 0)),
                  pl.BlockSpec((1, t, W_COLS), lambda j: (j // 2, 0, j % 2))],
        out_specs=pl.BlockSpec((1, D_MODEL, W_COLS), lambda j: (j, 0, 0)),
        out_shape=jax.ShapeDtypeStruct((N_DEV, D_MODEL, W_COLS), out_dtype),
        compiler_params=_params("parallel"),
    )(ut, dproj)


def _rms_bwd(x, gain, dy):
    r = _rms_scale(x)
    xh = x * r
    a = dy * gain
    return r * (a - xh * jnp.mean(a * xh, axis=-1, keepdims=True)), dy * xh


def sb_attn_bwd(qkv4, proj4, do3, o3, q_gain, k_gain, dproj4, rider=None):
    _, b, s, _ = proj4.shape
    blk, top = ATTN_BLOCK, SB_TOP_ROWS
    nq = s // blk
    ngroups = nq // SB_QBLOCKS_PER_STEP
    assert ngroups * SB_QBLOCKS_PER_STEP * blk == s
    scale = HEAD_DIM ** -0.5

    def body(qn_ref, kn_ref, v_ref, q_ref, k_ref, do_ref, o_ref, qg_ref, kg_ref, _alias, d_ref, dqg_ref, dkg_ref, dob):
        for h in range(SB_HEADS_PER_STEP):
            dob[h] = do_ref[0, :, h * HEAD_DIM:(h + 1) * HEAD_DIM].astype(BF16)
        d_ref[...] = jnp.zeros_like(d_ref)
        row = lax.broadcasted_iota(jnp.int32, (blk, blk), 0)
        col = lax.broadcasted_iota(jnp.int32, (blk, blk), 1)
        tri_lt = col < row
        suffix = (row > col).astype(BF16)
        suffix_incl = (row >= col).astype(BF16)

        def step(items):
            heads = [it[0] for it in items]
            where = [_sb_key_rows(it[1], it[2]) for it in items]
            kjs = [_sb_head(kn_ref, h, rows) for h, (rows, _) in zip(heads, where)]
            zs = [_sb_logits(it[3], kj) for it, kj in zip(items, kjs)]
            dws = [_dot(it[4], _sb_head(v_ref, h, rows), NT) for it, h, (rows, _) in zip(items, heads, where)]
            scored = [_sb_scores(z, it[2] == 0, live, it[6]) for z, it, (_, live) in zip(zs, items, where)]
            afters = [_dot_exact(log_skip, suffix) for log_skip, _, _ in scored]
            wbs = [jnp.where(valid, jnp.exp(log_beta + after + it[7]), 0.0).astype(BF16)
                   for (_, log_beta, valid), after, it in zip(scored, afters, items)]
            gs = [dw * wb.astype(F32) for dw, wb in zip(dws, wbs)]
            befores = [it[5] - (_dot_exact(g, suffix_incl) + it[8]) for g, it in zip(gs, items)]
            dzbs = [jnp.where(valid, g - jnp.exp(log_beta) * (g + before), 0.0).astype(BF16)
                    for (_, log_beta, valid), g, before in zip(scored, gs, befores)]
            dqs = [it[9] + _dot(dzb, kj) for it, dzb, kj in zip(items, dzbs, kjs)]
            for it, h, (rows, _), wb, dzb in zip(items, heads, where, wbs, dzbs):
                cols = slice(h * HEAD_DIM, (h + 1) * HEAD_DIM)
                d_ref[2, 0, rows, cols] += _dot(wb, it[4], TN)
                d_ref[1, 0, rows, cols] += _dot(dzb, it[3], TN)
            cs = [it[7] + jnp.sum(log_skip, axis=1, keepdims=True) for (log_skip, _, _), it in zip(scored, items)]
            cgs = [it[8] + jnp.sum(g, axis=1, keepdims=True) for g, it in zip(gs, items)]
            return cs, cgs, dqs

        def top_items(chains, deltas, done, cs, cgs, dqs):
            return [(h, i, done, _sb_head(qn_ref, h, _sb_rows(i, 0, top)), dob[h, _sb_rows(i, 0, top), :], delta,
                     tri_lt[:top, :], c, cg, dq)
                    for (h, i), delta, c, cg, dq in zip(chains, deltas, cs, cgs, dqs)]

        def finish_tops(chains, deltas, state):
            def k_step(state):
                done, cs, cgs, dqs = state
                cs, cgs, dqs = step(top_items(chains, deltas, done, cs, cgs, dqs))
                return done + 1, tuple(cs), tuple(cgs), tuple(dqs)

            _, _, _, dqs = lax.while_loop(functools.partial(_sb_keys_left, chains, slice(0, top)), k_step, state)
            for (h, i), dq in zip(chains, dqs):
                d_ref[0, 0, _sb_rows(i, 0, top), h * HEAD_DIM:(h + 1) * HEAD_DIM] = dq * scale

        def q_group(m, before):
            chains, chains_before = _sb_chains(m), _sb_chains(m - 1)
            deltas_b, before = before[0], before[1:]
            qis, dois, deltas = [], [], []
            for h, i in chains:
                rows_i = _sb_rows(i, 0, blk)
                qis.append(_sb_head(qn_ref, h, rows_i))
                dois.append(dob[h, rows_i, :])
                deltas.append(jnp.sum(dois[-1].astype(F32) * o_ref[0, rows_i, h * HEAD_DIM:(h + 1) * HEAD_DIM],
                                      axis=1, keepdims=True))
            n = len(chains)

            def items_of(done, cs, cgs, dqs):
                return [(h, i, done, q, do, delta, tri_lt, c, cg, dq)
                        for (h, i), q, do, delta, c, cg, dq in zip(chains, qis, dois, deltas, cs, cgs, dqs)]

            def k_step(state):
                done, cs, cgs, dqs = state
                cs, cgs, dqs = step(items_of(done, cs, cgs, dqs))
                return done + 1, tuple(cs), tuple(cgs), tuple(dqs)

            done_b, cs_b, cgs_b, dqs_b = before
            zero = (jnp.zeros((blk, 1), F32),) * n
            new = step(items_of(jnp.int32(0), zero, zero, (jnp.zeros((blk, HEAD_DIM), F32),) * n)
                       + top_items(chains_before, deltas_b, done_b, cs_b, cgs_b, dqs_b))
            done, cs, cgs, dqs = lax.while_loop(functools.partial(_sb_keys_left, chains, slice(top, blk)), k_step,
                                                (jnp.int32(1),) + tuple(tuple(x[:n]) for x in new))

            @pl.when(m > 0)
            def _():
                finish_tops(chains_before, deltas_b, (done_b + 1,) + tuple(tuple(x[n:]) for x in new))

            for (h, i), dq in zip(chains, dqs):
                d_ref[0, 0, _sb_rows(i, top, blk - top), h * HEAD_DIM:(h + 1) * HEAD_DIM] = dq[top:, :] * scale
            first = lambda xs: tuple(x[:top, :] for x in xs)
            return first(deltas), done, first(cs), first(cgs), first(dqs)

        n = SB_HEADS_PER_STEP * SB_QBLOCKS_PER_STEP
        zero = (jnp.zeros((top, 1), F32),) * n
        nothing = (zero, jnp.int32(0), zero, zero, (jnp.zeros((top, HEAD_DIM), F32),) * n)
        last = lax.fori_loop(0, ngroups, q_group, nothing)
        finish_tops(_sb_chains(ngroups - 1), last[0], last[1:])

        def norm_block(i, carry):
            rows = pl.ds(pl.multiple_of(i * blk, blk), blk)
            out = []
            for h in range(SB_HEADS_PER_STEP):
                cols = slice(h * HEAD_DIM, (h + 1) * HEAD_DIM)
                for part, src_ref, gain_ref in ((0, q_ref, qg_ref), (1, k_ref, kg_ref)):
                    dy = d_ref[part, 0, rows, cols] * (scale if part == 1 else 1.0)
                    dx, pg = _rms_bwd(src_ref[0, 0, rows, cols], gain_ref[...], dy)
                    d_ref[part, 0, rows, cols] = dx
                    out.append(carry[len(out)] + jnp.sum(pg, axis=0, keepdims=True))
            return tuple(out)

        sums = lax.fori_loop(0, nq, norm_block, (jnp.zeros((1, HEAD_DIM), F32),) * (2 * SB_HEADS_PER_STEP))
        for h in range(SB_HEADS_PER_STEP):
            dqg_ref[0, h] = sums[2 * h]
            dkg_ref[0, h] = sums[2 * h + 1]

    head_row = jax.ShapeDtypeStruct((b, N_HEADS, 1, HEAD_DIM), F32)
    return _pcall_riding(
        body, rider, (qkv4, qkv4, qkv4, proj4, proj4, do3, o3, q_gain, k_gain, dproj4),
        name="sb_attn_bwd", grid=(b, N_HEADS // SB_HEADS_PER_STEP),
        in_specs=[_sb_group_spec(s, 0), _sb_group_spec(s, 1), _sb_group_spec(s, 2),
                  _sb_group_spec(s, 0), _sb_group_spec(s, 1),
                  _sb_seq_group_spec(s), _sb_seq_group_spec(s), _GAIN_SPEC, _GAIN_SPEC,
                  pl.BlockSpec(memory_space=pl.ANY)],
        out_specs=[pl.BlockSpec((3, 1, s, SB_GROUP_COLS), lambda b, g: (0, b, 0, g)),
                   _SB_GROUP_ROW_SPEC, _SB_GROUP_ROW_SPEC],
        out_shape=[jax.ShapeDtypeStruct(dproj4.shape, F32), head_row, head_row],
        scratch_shapes=[pltpu.VMEM((SB_HEADS_PER_STEP, s, HEAD_DIM), BF16)],
        input_output_aliases={9: 0}, semantics=("parallel", "parallel"))


def hgrn2_bwd(proj4, don3, oraw3, states, lbl4, o_gain, dproj4, rider=None):
    _, b, s, _ = proj4.shape
    nchunk = s // HG_CHUNK
    c = HG_CHUNK
    subs = range(HG_CHUNK // HG_SUB)
    unroll = math.gcd(nchunk, HG_UNROLL_BWD)
    ngroup = nchunk // unroll

    def body(q_ref, f_ref, i_ref, don_ref, oraw_ref, st_ref, lbl_ref, og_ref, _alias, d_ref, dog_ref, dlb_ref):
        incl, lower, before_sub, upper = _hg_masks()
        lb, oml = _hg_lower_bound(lbl_ref)
        last_row = lax.broadcasted_iota(jnp.int32, (c, HEAD_DIM), 0) == c - 1

        def group(m, carry):
            dst, dog_acc, dlb_acc = carry
            ns, rows = _hg_group_rows(ngroup - 1 - m, unroll)
            ns, rows = ns[::-1], rows[::-1]
            vs = [_hg_gates(q_ref[0, 0, r, :], f_ref[0, 0, r, :], lb, oml) for r in rows]
            inps = [i_ref[0, 0, r, :].astype(BF16) for r in rows]
            sts = [st_ref[0, 0, n] for n in ns]
            gcs = [_dot_exact(v["logf"], lower, left=True) for v in vs]
            grs = [_dot_exact(v["logf"], before_sub, left=True) for v in vs]
            e_qs = [jnp.exp(gc - gr) for gc, gr in zip(gcs, grs)]
            a_s, qdbs, ksbs, ess = _hg_intra([v["q"] * e for v, e in zip(vs, e_qs)], [v["k"] for v in vs],
                                             gcs, grs, incl)
            e_gcs = [jnp.exp(gc) for gc in gcs]
            gls = [gc[c - 1:c, :] for gc in gcs]
            e_gls = [jnp.exp(gl) for gl in gls]
            e_ks = [jnp.exp(gl - gc) for gl, gc in zip(gls, gcs)]
            normed = [_rms_bwd(oraw_ref[0, r, :], og_ref[...], don_ref[0, r, :]) for r in rows]
            dobs = [do.astype(BF16) for do, _ in normed]
            dabs = [jnp.where(incl, _dot(dob, inp, NT), 0.0).astype(BF16) for dob, inp in zip(dobs, inps)]
            adds = [_dot(dob, (v["q"] * e).astype(BF16), TN) for dob, v, e in zip(dobs, vs, e_gcs)]
            dq_inters = [_dot(dob, st.astype(BF16)) * e for dob, st, e in zip(dobs, sts, e_gcs)]
            dqds = [jnp.concatenate([_dot(dab[sub * HG_SUB:(sub + 1) * HG_SUB, :], ksb[sub]) for sub in subs], axis=0)
                    for dab, ksb in zip(dabs, ksbs)]
            dkss = [[_dot(dab[sub * HG_SUB:(sub + 1) * HG_SUB, :], qdb[sub * HG_SUB:(sub + 1) * HG_SUB, :], TN)
                     for sub in subs] for dab, qdb in zip(dabs, qdbs)]
            dsts = []
            for e_gl, add in zip(e_gls, adds):
                dsts.append(dst)
                dst = dst * e_gl + add
            dstbs = [d.astype(BF16) for d in dsts]
            dis = [_dot(a.astype(BF16), dob, TN) + _dot((v["k"] * e_k).astype(BF16), dstb, NT)
                   for a, dob, v, e_k, dstb in zip(a_s, dobs, vs, e_ks, dstbs)]
            dk_inters = [_dot(inp, dstb) * e_k for inp, dstb, e_k in zip(inps, dstbs, e_ks)]
            for u, r in enumerate(rows):
                v, q, k = vs[u], vs[u]["q"], vs[u]["k"]
                dk, dgc_k = dk_inters[u], jnp.zeros((c, HEAD_DIM), F32)
                for sub in subs:
                    dk = dk + dkss[u][sub] * ess[u][sub]
                    dgc_k = dgc_k + dkss[u][sub] * ksbs[u][sub].astype(F32)
                dq = dqds[u] * e_qs[u] + dq_inters[u]
                at_last = (jnp.sum(k * dk_inters[u], axis=0, keepdims=True)
                           + e_gls[u] * jnp.sum(sts[u] * dsts[u], axis=0, keepdims=True))
                dgc = ((qdbs[u].astype(F32) * dqds[u] - dgc_k) + (q * dq_inters[u] - k * dk_inters[u])
                       + jnp.where(last_row, at_last, 0.0))
                dlf_f = _dot_exact(dgc, upper, left=True) / v["f"]
                d_ref[0, 0, r, :] = dq * (v["sq"] * (1.0 + v["qp"] * (1.0 - v["sq"])))
                d_ref[1, 0, r, :] = (dlf_f - dk) * (oml * v["sf"] * v["sfn"])
                d_ref[2, 0, r, :] = dis[u]
                dlb_acc = dlb_acc + jnp.sum((dlf_f - dk) * v["sfn"], axis=0, keepdims=True)
                dog_acc = dog_acc + jnp.sum(normed[u][1], axis=0, keepdims=True)
            return dst, dog_acc, dlb_acc

        zero = jnp.zeros((1, HEAD_DIM), F32)
        _, dog, dlb = lax.fori_loop(0, ngroup, group, (jnp.zeros((HEAD_DIM, HEAD_DIM), F32), zero, zero))
        dog_ref[0, 0] = dog
        dlb_ref[0, 0] = dlb

    head_row = jax.ShapeDtypeStruct((b, N_HEADS, 1, HEAD_DIM), F32)
    return _pcall_riding(
        body, rider, (proj4, proj4, proj4, don3, oraw3, states, lbl4, o_gain, dproj4),
        name="hgrn2_bwd", grid=(b, N_HEADS),
        in_specs=[_head_spec(s, 0), _head_spec(s, 1), _head_spec(s, 2), _seq_spec(s), _seq_spec(s),
                  _state_spec(nchunk), pl.BlockSpec((2, 1, 1, HEAD_DIM), lambda b, h: (0, h, 0, 0)), _GAIN_SPEC,
                  pl.BlockSpec(memory_space=pl.ANY)],
        out_specs=[pl.BlockSpec((3, 1, s, HEAD_DIM), lambda b, h: (0, b, 0, h)), _HEAD_ROW_SPEC, _HEAD_ROW_SPEC],
        out_shape=[jax.ShapeDtypeStruct(dproj4.shape, F32), head_row, head_row],
        input_output_aliases={8: 0}, semantics=("parallel", "parallel"))


def local_step(x, target, sb_norm, wsi, sb_q_gain, sb_k_gain, hg_o_gain, hg_lb_logits, wso_mine, whi_mine, who_mine,
               hg_norm_mine):
    b, s, _ = x.shape
    t = b * s
    x2 = x.reshape(t, D_MODEL)
    tg2 = target.reshape(t, D_MODEL)
    lbl4 = hg_lb_logits.reshape(2, N_HEADS, 1, HEAD_DIM)
    four = (4, b, s, D_MODEL)
    three = (b, s, D_MODEL)
    rows8 = (N_DEV, W_ROWS, D_MODEL)

    (proj0, u0, qkv0), (wso, who) = rms_inproj(x2, sb_norm, wsi, "sb_inproj",
                                               _Rider([wso_mine, who_mine], scatter=False),
                                               qk_gains=(sb_q_gain, sb_k_gain))
    qkv0 = qkv0.reshape(3, b, s, D_MODEL)
    wso = wso.reshape(D_MODEL, D_MODEL)
    who = who.reshape(D_MODEL, D_MODEL)
    o0, (whi, hgn) = sb_attn_fwd(qkv0, _Rider([whi_mine, hg_norm_mine], scatter=False))
    hg_norm_full = hgn[:, 0, :].reshape(1, D_MODEL)
    o0 = o0.reshape(t, D_MODEL)
    h1 = gate_outproj(o0, proj0, wso, x2, None, "sb_outproj")
    (proj1, u1), _ = rms_inproj(h1, hg_norm_full, whi, "hg_inproj")
    o1, o1_raw, states = hgrn2_fwd(proj1.reshape(four), lbl4, hg_o_gain)
    o1 = o1.reshape(t, D_MODEL)
    dh2, loss_parts = gate_outproj(o1, proj1, who, h1, tg2, "hg_outproj_loss")

    do1, dproj1, g_who = outproj_bwd(dh2, who, o1, proj1, "hg_outproj_bwd")
    (dproj1, g_og, g_lb), (p_who,) = hgrn2_bwd(proj1.reshape(four), do1.reshape(three), o1_raw, states, lbl4,
                                               hg_o_gain, dproj1.reshape(four),
                                               _Rider([g_who.reshape(rows8)], scatter=True))
    dproj1 = dproj1.reshape(4, t, D_MODEL)
    (dh1, g_hgn), _ = inproj_bwd_dx(dproj1, whi, h1, hg_norm_full, dh2, "hg_inproj_bwd_dx")
    g_whi = inproj_bwd_dw(u1, dproj1, "hg_inproj_bwd_dw", out_dtype=BF16)

    do0, dproj0, g_wso = outproj_bwd(dh1, wso, o0, proj0, "sb_outproj_bwd")
    (dproj0, g_qg, g_kg), (p_whi, p_wso) = sb_attn_bwd(qkv0, proj0.reshape(four), do0.reshape(three), o0.reshape(three),
                                                       sb_q_gain, sb_k_gain, dproj0.reshape(four),
                                                       _Rider([g_whi, g_wso.reshape(rows8)], scatter=True))
    dproj0 = dproj0.reshape(4, t, D_MODEL)
    g_wsi = inproj_bwd_dw(u0, dproj0, "sb_inproj_bwd_dw", out_dtype=BF16)
    (gx, g_sbn), (p_wsi,) = inproj_bwd_dx(dproj0, wsi, x2, sb_norm, dh1, "sb_inproj_bwd_dx",
                                          _Rider([sum_within_chip(g_wsi)], scatter=True, chips=True))
    return dict(loss_parts=loss_parts, gx=gx.reshape(three), p_wsi=p_wsi, p_wso=p_wso, p_whi=p_whi, p_who=p_who,
                g_sbn=g_sbn, g_hgn=g_hgn, g_qg=g_qg, g_kg=g_kg, g_og=g_og, g_lb=g_lb)


def sum_within_chip(g):
    _, r, c_ = g.shape
    chips = N_DEV // 2

    def swap(g_ref, got_ref, send_sems, recv_sems):
        x, y, c = _mesh_pos()
        copies = [pltpu.make_async_remote_copy(
            src_ref=g_ref.at[2 * q + 1 - c], dst_ref=got_ref.at[q], send_sem=send_sems.at[q], recv_sem=recv_sems.at[q],
            device_id=(x, y, 1 - c), device_id_type=MESH) for q in range(chips)]
        for cp in copies:
            cp.start()
        for cp in copies:
            cp.wait_recv()
            cp.wait_send()

    got = _pcall(swap, name="swap_with_sibling", in_specs=[_ANY], out_specs=_ANY,
                 out_shape=jax.ShapeDtypeStruct((chips, r, c_), g.dtype),
                 scratch_shapes=[pltpu.SemaphoreType.DMA((chips,)), pltpu.SemaphoreType.DMA((chips,))])(g)

    def add(g_ref, got_ref, out_ref):
        mine = g_ref[0, lax.axis_index("c")]
        out_ref[0] = (mine.astype(F32) + got_ref[0].astype(F32)).astype(out_ref.dtype)

    return _pcall(
        add, name="add_sibling_partials", grid=(chips,),
        in_specs=[pl.BlockSpec((1, 2, r, c_), lambda q: (q, 0, 0, 0)), pl.BlockSpec((1, r, c_), lambda q: (q, 0, 0))],
        out_specs=pl.BlockSpec((1, r, c_), lambda q: (q, 0, 0)),
        out_shape=jax.ShapeDtypeStruct((chips, r, c_), g.dtype),
        compiler_params=_params("parallel"),
    )(g.reshape(chips, 2, r, c_), got)


def _two_level_gather(src, out, send_sems, recv_sems, local_sem, pos):
    x, y, c = pos
    me, sibling = (x, y, c), (x, y, 1 - c)
    chips = [(1 - x, y), (x, 1 - y), (1 - x, 1 - y)]

    def copy(k, block, to, source=None):
        slot = out.at[_linear(block)]
        return pltpu.make_async_remote_copy(
            src_ref=slot if source is None else source, dst_ref=slot, send_sem=send_sems.at[k],
            recv_sem=recv_sems.at[k], device_id=to, device_id_type=MESH)

    mine = pltpu.make_async_copy(src, out.at[_linear(me)], local_sem)
    mine.start()
    first = [copy(0, me, sibling, src)] + [copy(1 + j, me, (*chip, c), src) for j, chip in enumerate(chips)]
    for cp in first:
        cp.start()
    passed = [copy(4 + j, (*chip, c), sibling) for j, chip in enumerate(chips)]
    for j, chip in enumerate(chips):
        copy(1 + j, (*chip, c), me).wait_recv()
        passed[j].start()
    copy(0, sibling, me).wait_recv()
    for j, chip in enumerate(chips):
        copy(4 + j, (*chip, 1 - c), me).wait_recv()
    for cp in first + passed:
        cp.wait_send()
    mine.wait()


def gather_first_weights(w_si, w_so, w_hi, w_ho, hg_norm):
    def body(si_ref, so_ref, hi_ref, ho_ref, hn_ref, o_si, so_b, hi_b, ho_b, hn_b, si_b, send_sems, recv_sems, local_sem):
        for src, buf in ((si_ref, si_b), (so_ref, so_b), (hi_ref, hi_b), (ho_ref, ho_b)):
            buf[...] = src[...].astype(BF16)
        hn_b[...] = jnp.broadcast_to(hn_ref[...], hn_b.shape)
        _two_level_gather(si_b, o_si, send_sems, recv_sems, local_sem, _mesh_pos())

    return _pcall(
        body, name="gather_first_weights",
        in_specs=[_VMEM] * 5, out_specs=[_ANY] + [_VMEM] * 4,
        out_shape=[jax.ShapeDtypeStruct((N_DEV,) + w_si.shape, BF16), jax.ShapeDtypeStruct(w_so.shape, BF16),
                   jax.ShapeDtypeStruct(w_hi.shape, BF16), jax.ShapeDtypeStruct(w_ho.shape, BF16),
                   jax.ShapeDtypeStruct((8, HEAD_DIM), F32)],
        scratch_shapes=[pltpu.VMEM(w_si.shape, BF16), pltpu.SemaphoreType.DMA((N_PEERS,)),
                        pltpu.SemaphoreType.DMA((N_PEERS,)), pltpu.SemaphoreType.DMA],
        compiler_params=pltpu.CompilerParams(vmem_limit_bytes=VMEM_LIMIT_BYTES),
    )(w_si, w_so, w_hi, w_ho, hg_norm)


def _adamw(w, g, m, v):
    m = ADAM_B1 * m + (1.0 - ADAM_B1) * g
    v = ADAM_B2 * v + (1.0 - ADAM_B2) * (g * g)
    m_hat = m / (1.0 - ADAM_B1 ** ADAM_STEP)
    v_hat = v / (1.0 - ADAM_B2 ** ADAM_STEP)
    delta = -ADAM_LR * (m_hat / (jnp.sqrt(v_hat) + ADAM_EPS) + ADAM_WD * w)
    return delta, m, v


def reduce_adamw(parts, w, m, v, name):
    n, r, c = parts.shape
    tr = _row_tile(r, 256)

    def body(p_ref, w_ref, m_ref, v_ref, g_ref, d_ref, m2_ref, v2_ref):
        g = p_ref[0].astype(F32)
        for sender in range(1, n):
            g = g + p_ref[sender].astype(F32)
        g_ref[...] = g
        d_ref[...], m2_ref[...], v2_ref[...] = _adamw(w_ref[...], g, m_ref[...], v_ref[...])

    tile = pl.BlockSpec((tr, c), lambda i: (i, 0))
    return _pcall(
        body, name=name, grid=(r // tr,),
        in_specs=[pl.BlockSpec((n, tr, c), lambda i: (0, i, 0)), tile, tile, tile],
        out_specs=[tile] * 4, out_shape=[jax.ShapeDtypeStruct((r, c), F32)] * 4,
        compiler_params=_params("parallel"),
    )(parts, w, m, v)


PACK_ROWS = 32
ROW_SBN, ROW_HGN, ROW_LB, ROW_QG, ROW_KG, ROW_OG, ROW_LOSS = 0, 8, 16, 24, 25, 26, 27


def small_update(g_sbn, g_hgn, g_lb, g_qg, g_kg, g_og, loss_parts, small):
    n_in = 7 + len(small)

    def body(*refs):
        sbn_ref, hgn_ref, lb_ref, qg_ref, kg_ref, og_ref, loss_ref = refs[:7]
        wmv = refs[7:n_in]
        outs = refs[n_in:n_in + 25]
        pack, gath, tot, send_sems, recv_sems, local_sems = refs[n_in + 25:]
        pos = _mesh_pos()
        me = _linear(pos)
        pack[...] = jnp.zeros_like(pack)
        pack[ROW_SBN:ROW_SBN + 8, :] = sbn_ref[...]
        pack[ROW_HGN:ROW_HGN + 8, :] = hgn_ref[...]
        pack[ROW_LB:ROW_LB + 8, :] = jnp.sum(lb_ref[...], axis=0)
        pack[ROW_QG:ROW_QG + 1, :] = jnp.sum(qg_ref[...], axis=0, keepdims=True)
        pack[ROW_KG:ROW_KG + 1, :] = jnp.sum(kg_ref[...], axis=0, keepdims=True)
        pack[ROW_OG:ROW_OG + 1, :] = jnp.sum(og_ref[...], axis=0, keepdims=True)
        pack[ROW_LOSS:ROW_LOSS + 1, :] = jnp.sum(loss_ref[...], axis=0)[0:1, :]
        _exchange_start([((lambda p: pack), gath)], send_sems, recv_sems, local_sems, pos)
        _exchange_wait([((lambda p: pack), gath)], send_sems, recv_sems, local_sems, pos)
        total = gath[0]
        for dev in range(1, N_DEV):
            total = total + gath[dev]
        tot[...] = total
        outs[0][...] = jnp.broadcast_to(tot[ROW_LOSS:ROW_LOSS + 1, :], (8, HEAD_DIM))
        l0 = wmv[15][0:8, :]
        l1 = wmv[15][8:16, :]
        p1, p0 = _sigmoid_pair(l1 - l0)
        d_l1 = p0 * p1 * tot[ROW_LB:ROW_LB + 8, :]
        grads = [tot[ROW_SBN:ROW_SBN + 8, :], tot[ROW_QG:ROW_QG + 1, :], tot[ROW_KG:ROW_KG + 1, :],
                 tot[pl.ds(ROW_HGN + me, 1), :], tot[ROW_OG:ROW_OG + 1, :],
                 jnp.concatenate([-d_l1, d_l1], axis=0)]
        for i, g in enumerate(grads):
            w_ref, m_ref, v_ref = wmv[3 * i:3 * i + 3]
            o = outs[1 + 4 * i:5 + 4 * i]
            o[0][...] = g
            o[1][...], o[2][...], o[3][...] = _adamw(w_ref[...], g, m_ref[...], v_ref[...])

    out_shape = [jax.ShapeDtypeStruct((8, HEAD_DIM), F32)]
    for i in range(6):
        out_shape += [jax.ShapeDtypeStruct(small[3 * i].shape, F32)] * 4
    return _pcall(
        body, name="small_update",
        in_specs=[_VMEM] * n_in, out_specs=[_VMEM] * 25, out_shape=out_shape,
        scratch_shapes=[pltpu.VMEM((PACK_ROWS, HEAD_DIM), F32), pltpu.VMEM((N_DEV, PACK_ROWS, HEAD_DIM), F32),
                        pltpu.VMEM((PACK_ROWS, HEAD_DIM), F32)] + _exchange_sems(1),
    )(g_sbn, g_hgn, g_lb, g_qg, g_kg, g_og, loss_parts, *small)


def kernel(x, sb_norm, sb_w_in, sb_q_gain, sb_k_gain, sb_w_out, hg_norm, hg_w_in, hg_o_gain, hg_w_out, hg_lb_logits, loss_target, m_sb_norm, m_sb_w_in, m_sb_q_gain, m_sb_k_gain, m_sb_w_out, m_hg_norm, m_hg_w_in, m_hg_o_gain, m_hg_w_out, m_hg_lb_logits, v_sb_norm, v_sb_w_in, v_sb_q_gain, v_sb_k_gain, v_sb_w_out, v_hg_norm, v_hg_w_in, v_hg_o_gain, v_hg_w_out, v_hg_lb_logits):
    b = x.shape[0]
    wsi, wso_mine, whi_mine, who_mine, hg_norm_mine = gather_first_weights(
        sb_w_in[0], sb_w_out[0], hg_w_in[0], hg_w_out[0], hg_norm)
    r = local_step(x, loss_target, sb_norm, wsi, sb_q_gain, sb_k_gain, hg_o_gain, hg_lb_logits,
                   wso_mine, whi_mine, who_mine, hg_norm_mine)
    big = {}
    for name, p, w, m, v in (("sb_w_in", r["p_wsi"], sb_w_in, m_sb_w_in, v_sb_w_in),
                             ("sb_w_out", r["p_wso"], sb_w_out, m_sb_w_out, v_sb_w_out),
                             ("hg_w_in", r["p_whi"], hg_w_in, m_hg_w_in, v_hg_w_in),
                             ("hg_w_out", r["p_who"], hg_w_out, m_hg_w_out, v_hg_w_out)):
        big[name] = [o[None] for o in reduce_adamw(p, w[0], m[0], v[0], "adamw_" + name)]

    def rows8(a):
        return a.reshape(8, HEAD_DIM)

    def rows16(a):
        return a.reshape(16, HEAD_DIM)

    small_in = [rows8(sb_norm), rows8(m_sb_norm), rows8(v_sb_norm),
                sb_q_gain, m_sb_q_gain, v_sb_q_gain,
                sb_k_gain, m_sb_k_gain, v_sb_k_gain,
                hg_norm, m_hg_norm, v_hg_norm,
                hg_o_gain, m_hg_o_gain, v_hg_o_gain,
                rows16(hg_lb_logits), rows16(m_hg_lb_logits), rows16(v_hg_lb_logits)]
    so = small_update(rows8(r["g_sbn"]), rows8(r["g_hgn"]), r["g_lb"].reshape(b, N_HEADS, HEAD_DIM),
                      r["g_qg"].reshape(b * N_HEADS, HEAD_DIM), r["g_kg"].reshape(b * N_HEADS, HEAD_DIM),
                      r["g_og"].reshape(b * N_HEADS, HEAD_DIM), r["loss_parts"], small_in)
    loss = so[0][0, 0]
    shapes = {"sb_norm": (1, D_MODEL), "sb_q_gain": (1, HEAD_DIM), "sb_k_gain": (1, HEAD_DIM),
              "hg_norm": (1, HEAD_DIM), "hg_o_gain": (1, HEAD_DIM), "hg_lb_logits": (2, D_MODEL)}
    small = {}
    for i, name in enumerate(("sb_norm", "sb_q_gain", "sb_k_gain", "hg_norm", "hg_o_gain", "hg_lb_logits")):
        small[name] = [o.reshape(shapes[name]) for o in so[1 + 4 * i:5 + 4 * i]]
    order = ("sb_norm", "sb_w_in", "sb_q_gain", "sb_k_gain", "sb_w_out",
             "hg_norm", "hg_w_in", "hg_o_gain", "hg_w_out", "hg_lb_logits")
    res = {**big, **small}
    return (loss, r["gx"]) + tuple(res[n][j] for j in range(4) for n in order)
```

```python
import functools
import math

import jax
import jax.numpy as jnp
from jax import lax
from jax.experimental import pallas as pl
from jax.experimental.pallas import tpu as pltpu

F32 = jnp.float32
BF16 = jnp.bfloat16

N_DEV = 8
D_MODEL = 1024
N_HEADS = 8
HEAD_DIM = 128
RMS_EPS = 1e-6
ATTN_BLOCK = 128
HG_CHUNK = 64
HG_SUB = 16
HG_UNROLL_FWD = 16
HG_UNROLL_BWD = 16
EXP_CLAMP = 80.0
SB_HEADS_PER_STEP = 2
SB_QBLOCKS_PER_STEP = 4
SB_GROUP_COLS = SB_HEADS_PER_STEP * 128
SB_TOP_ROWS = 32
SB_LOG_WEIGHT_FLOOR = -104.0
VMEM_LIMIT_BYTES = 48 * 1024 * 1024
W_COLS = 4 * D_MODEL // N_DEV
W_ROWS = D_MODEL // N_DEV

ADAM_LR = 0.001
ADAM_B1 = 0.9
ADAM_B2 = 0.999
ADAM_EPS = 1e-08
ADAM_WD = 0.01
ADAM_STEP = 10

NT = (((1,), (1,)), ((), ()))
TN = (((0,), (0,)), ((), ()))
NN = (((1,), (0,)), ((), ()))


def _pcall(body, *, name, **kw):
    return pl.pallas_call(body, name=name, **kw)


def _params(*sem):
    return pltpu.CompilerParams(dimension_semantics=sem, vmem_limit_bytes=VMEM_LIMIT_BYTES)


def _dot(a, b, dims=NN):
    return lax.dot_general(a, b, dims, preferred_element_type=F32)


def _dot_exact(a, m, dims=NN, left=False):
    hi = a.astype(BF16)
    lo = (a - hi.astype(F32)).astype(BF16)
    if left:
        return _dot(m, hi, dims) + _dot(m, lo, dims)
    return _dot(hi, m, dims) + _dot(lo, m, dims)


def _split(a):
    hi = a.astype(BF16)
    return hi, (a - hi.astype(F32)).astype(BF16)


def _dot3(a, b, dims=NN):
    return _dot(a[0], b[0], dims) + (_dot(a[0], b[1], dims) + _dot(a[1], b[0], dims))


def _sigmoid(x):
    return 1.0 / (1.0 + jnp.exp(-x))


def _sigmoid_pair(x):
    e = jnp.exp(-jnp.abs(x))
    big = 1.0 / (1.0 + e)
    small = e * big
    pos = x >= 0
    return jnp.where(pos, big, small), jnp.where(pos, small, big)


def _rms_scale(x):
    return lax.rsqrt(jnp.mean(x * x, axis=-1, keepdims=True) + RMS_EPS)


def _row_tile(t, want):
    return want if t % want == 0 else t


MESH = pl.DeviceIdType.MESH
N_PEERS = N_DEV - 1
_ANY = pl.BlockSpec(memory_space=pl.ANY)
_VMEM = pl.BlockSpec(memory_space=pltpu.VMEM)


def _mesh_pos():
    return lax.axis_index("x"), lax.axis_index("y"), lax.axis_index("c")


def _linear(pos):
    return 4 * pos[0] + 2 * pos[1] + pos[2]


def _peer(pos, k):
    flips = ((k + 1) >> 2 & 1, (k + 1) >> 1 & 1, (k + 1) & 1)
    return tuple(1 - p if f else p for p, f in zip(pos, flips))


def _chip(pos):
    return 2 * pos[0] + pos[1]


ALL_PEERS = tuple(range(N_PEERS))
SAME_CORE_PEERS = (1, 3, 5)


def _exchange_copies(pairs, send_sems, recv_sems, local_sems, pos, landing, peers, slot):
    me = slot(pos)
    local, remote = [], []
    for a, (src_of, dst) in enumerate(pairs):
        local.append(pltpu.make_async_copy(src_of(pos), dst.at[me], local_sems.at[a]))
        for k in peers:
            peer = _peer(pos, k)
            remote.append(pltpu.make_async_remote_copy(
                src_ref=src_of(peer), dst_ref=dst.at[slot(peer) if landing else me],
                send_sem=send_sems.at[a, k], recv_sem=recv_sems.at[a, k], device_id=peer, device_id_type=MESH))
    return local, remote


def _exchange_start(pairs, send_sems, recv_sems, local_sems, pos, peers=ALL_PEERS, slot=_linear):
    local, sent = _exchange_copies(pairs, send_sems, recv_sems, local_sems, pos, False, peers, slot)
    for copy in local + sent:
        copy.start()


def _exchange_wait(pairs, send_sems, recv_sems, local_sems, pos, peers=ALL_PEERS, slot=_linear):
    local, landed = _exchange_copies(pairs, send_sems, recv_sems, local_sems, pos, True, peers, slot)
    for copy in landed:
        copy.wait_recv()
        copy.wait_send()
    for copy in local:
        copy.wait()


def _exchange_sems(n):
    return [pltpu.SemaphoreType.DMA((n, N_PEERS)), pltpu.SemaphoreType.DMA((n, N_PEERS)),
            pltpu.SemaphoreType.DMA((n,))]


class _Rider:
    def __init__(self, arrays, scatter, chips=False):
        self.arrays = list(arrays)
        self.scatter = scatter
        self.peers = SAME_CORE_PEERS if chips else ALL_PEERS
        self.slot = _chip if chips else _linear
        self.out_shapes = [jax.ShapeDtypeStruct(a.shape if scatter else (N_DEV,) + a.shape, a.dtype)
                           for a in self.arrays]

    def pairs(self, in_refs, out_refs):
        if self.scatter:
            return [((lambda pos, r=r: r.at[self.slot(pos)]), o) for r, o in zip(in_refs, out_refs)]
        return [((lambda pos, r=r: r), o) for r, o in zip(in_refs, out_refs)]


def _pcall_riding(body, rider, args, *, name, grid, in_specs, out_specs, out_shape, semantics, scratch_shapes=(),
                  input_output_aliases=None):
    aliases = input_output_aliases or {}
    if rider is None:
        outs = _pcall(body, name=name, grid=grid, in_specs=list(in_specs), out_specs=list(out_specs),
                      out_shape=list(out_shape), scratch_shapes=list(scratch_shapes), input_output_aliases=aliases,
                      compiler_params=_params(*semantics))(*args)
        return list(outs), []
    n_in, n_out, n_scr, n_r = len(in_specs), len(out_specs), len(scratch_shapes), len(rider.arrays)

    def riding(*refs):
        ins, refs = refs[:n_in], refs[n_in:]
        rider_in, refs = refs[:n_r], refs[n_r:]
        outs, refs = refs[:n_out], refs[n_out:]
        rider_out, refs = refs[:n_r], refs[n_r:]
        scratch, sems = refs[:n_scr], refs[n_scr:]
        pairs = rider.pairs(rider_in, rider_out)
        first = functools.reduce(jnp.logical_and, [pl.program_id(a) == 0 for a in range(len(grid))])
        last = functools.reduce(jnp.logical_and, [pl.program_id(a) == g - 1 for a, g in enumerate(grid)])

        @pl.when(first)
        def _():
            _exchange_start(pairs, *sems, _mesh_pos(), rider.peers, rider.slot)

        body(*ins, *outs, *scratch)

        @pl.when(last)
        def _():
            _exchange_wait(pairs, *sems, _mesh_pos(), rider.peers, rider.slot)

    outs = _pcall(riding, name=name, grid=grid, in_specs=list(in_specs) + [_ANY] * n_r,
                  out_specs=list(out_specs) + [_ANY] * n_r, out_shape=list(out_shape) + rider.out_shapes,
                  scratch_shapes=list(scratch_shapes) + _exchange_sems(n_r), input_output_aliases=aliases,
                  compiler_params=_params(*(("arbitrary",) * len(grid))))(*args, *rider.arrays)
    return list(outs[:n_out]), list(outs[n_out:])


def rms_inproj(x2, gain, wg, name, rider=None, qk_gains=None):
    t = x2.shape[0]
    tm = _row_tile(t, 256)
    with_qkv = qk_gains is not None

    def body(x_ref, g_ref, w_ref, *rest):
        if with_qkv:
            qg_ref, kg_ref, proj_ref, ut_ref, qkv_ref = rest
            head_gain = (qg_ref, kg_ref)
        else:
            proj_ref, ut_ref = rest
        x = x_ref[...]
        u = x * _rms_scale(x) * g_ref[...]
        ut_ref[...] = u.T.astype(BF16)
        u = u.astype(BF16)
        for p in range(N_DEV):
            part, lo = p // 2, (p % 2) * W_COLS
            res = _dot(u, w_ref[p])
            proj_ref[part, :, lo:lo + W_COLS] = res
            if with_qkv and part < 3:
                for h in range(W_COLS // HEAD_DIM):
                    y = res[:, h * HEAD_DIM:(h + 1) * HEAD_DIM]
                    if part < 2:
                        y = y * _rms_scale(y) * head_gain[part][...]
                    qkv_ref[part, :, lo + h * HEAD_DIM:lo + (h + 1) * HEAD_DIM] = y.astype(BF16)

    vec = pl.BlockSpec((1, D_MODEL), lambda i: (0, 0))
    in_specs = [pl.BlockSpec((tm, D_MODEL), lambda i: (i, 0)), vec,
                pl.BlockSpec((N_DEV, D_MODEL, W_COLS), lambda i: (0, 0, 0))]
    out_specs = [pl.BlockSpec((4, tm, D_MODEL), lambda i: (0, i, 0)), pl.BlockSpec((D_MODEL, tm), lambda i: (0, i))]
    out_shape = [jax.ShapeDtypeStruct((4, t, D_MODEL), F32), jax.ShapeDtypeStruct((D_MODEL, t), BF16)]
    args = (x2, gain, wg)
    if with_qkv:
        in_specs += [pl.BlockSpec((1, HEAD_DIM), lambda i: (0, 0))] * 2
        out_specs.append(pl.BlockSpec((3, tm, D_MODEL), lambda i: (0, i, 0)))
        out_shape.append(jax.ShapeDtypeStruct((3, t, D_MODEL), BF16))
        args += tuple(qk_gains)
    return _pcall_riding(body, rider, args, name=name, grid=(t // tm,), in_specs=in_specs, out_specs=out_specs,
                         out_shape=out_shape, semantics=("parallel",))


def gate_outproj(o2, proj, w_out, resid, target, name):
    t = o2.shape[0]
    tm = _row_tile(t, 512)
    with_loss = target is not None

    def body(o_ref, gate_ref, w_ref, r_ref, *rest):
        g = gate_ref[0]
        og = (o_ref[...] * (g * _sigmoid(g))).astype(BF16)
        h = r_ref[...] + _dot(og, w_ref[...])
        if with_loss:
            t_ref, dh_ref, loss_ref = rest
            err = h - t_ref[...]
            dh_ref[...] = err * (1.0 / D_MODEL)
            part = 0.5 * jnp.sum(jnp.mean(err * err, axis=-1, keepdims=True))
            loss_ref[...] = jnp.full(loss_ref.shape, part, F32)
        else:
            (h_ref,) = rest
            h_ref[...] = h

    row = pl.BlockSpec((tm, D_MODEL), lambda i: (i, 0))
    in_specs = [row,
                pl.BlockSpec((1, tm, D_MODEL), lambda i: (3, i, 0)),
                pl.BlockSpec((D_MODEL, D_MODEL), lambda i: (0, 0)),
                row]
    args = [o2, proj, w_out, resid]
    if with_loss:
        in_specs.append(row)
        args.append(target)
        out_specs = [row, pl.BlockSpec((1, 8, 128), lambda i: (i, 0, 0))]
        out_shape = [jax.ShapeDtypeStruct((t, D_MODEL), F32),
                     jax.ShapeDtypeStruct((t // tm, 8, 128), F32)]
    else:
        out_specs = row
        out_shape = jax.ShapeDtypeStruct((t, D_MODEL), F32)
    return _pcall(body, name=name, grid=(t // tm,), in_specs=in_specs, out_specs=out_specs,
                  out_shape=out_shape, compiler_params=_params("parallel"))(*args)


def _head_spec(s, part):
    return pl.BlockSpec((1, 1, s, HEAD_DIM), lambda b, h: (part, b, 0, h))


def _seq_spec(s):
    return pl.BlockSpec((1, s, HEAD_DIM), lambda b, h: (b, 0, h))


_GAIN_SPEC = pl.BlockSpec((1, HEAD_DIM), lambda b, h: (0, 0))
_HEAD_ROW_SPEC = pl.BlockSpec((1, 1, 1, HEAD_DIM), lambda b, h: (b, h, 0, 0))


def _sb_group_spec(s, part):
    return pl.BlockSpec((1, 1, s, SB_GROUP_COLS), lambda b, g: (part, b, 0, g))


def _sb_seq_group_spec(s):
    return pl.BlockSpec((1, s, SB_GROUP_COLS), lambda b, g: (b, 0, g))


_SB_GROUP_ROW_SPEC = pl.BlockSpec((1, SB_HEADS_PER_STEP, 1, HEAD_DIM), lambda b, g: (b, g, 0, 0))


def _sb_chains(m):
    return [(h, m * SB_QBLOCKS_PER_STEP + r) for h in range(SB_HEADS_PER_STEP) for r in range(SB_QBLOCKS_PER_STEP)]


def _sb_logits(qi, kj):
    return _dot(qi, kj, NT) * (HEAD_DIM ** -0.5)


def _sb_scores(z, diag, live, tri_lt):
    soft = jnp.log(1.0 + jnp.exp(-jnp.abs(z)))
    valid = jnp.logical_and(live, jnp.logical_or(jnp.logical_not(diag), tri_lt))
    log_skip = jnp.where(valid, -(jnp.maximum(z, 0.0) + soft), 0.0)
    log_beta = jnp.minimum(z, 0.0) - soft
    return log_skip, log_beta, valid


def _sb_keys_left(chains, watch, state):
    done, carries = state[0], state[1]
    worst = None
    for (_, i), c in zip(chains, carries):
        c = jnp.where(done <= i, c[watch], -jnp.inf)
        worst = c if worst is None else jnp.maximum(worst, c)
    return jnp.logical_and(done <= chains[-1][1],
                           jnp.logical_or(done == 0, jnp.max(worst) > SB_LOG_WEIGHT_FLOOR))


def _sb_key_rows(i, done):
    j = i - done
    return pl.ds(pl.multiple_of(jnp.maximum(j, 0) * ATTN_BLOCK, ATTN_BLOCK), ATTN_BLOCK), j >= 0


def _sb_head(ref, h, rows):
    return ref[0, 0, rows, h * HEAD_DIM:(h + 1) * HEAD_DIM]


def _sb_rows(i, offset, count):
    return pl.ds(pl.multiple_of(jnp.maximum(i, 0) * ATTN_BLOCK + offset, 8), count)


def sb_attn_fwd(qkv4, rider=None):
    _, b, s, _ = qkv4.shape
    blk, top = ATTN_BLOCK, SB_TOP_ROWS
    ngroups = s // blk // SB_QBLOCKS_PER_STEP
    assert ngroups * SB_QBLOCKS_PER_STEP * blk == s

    def body(q_ref, k_ref, v_ref, o_ref):
        row = lax.broadcasted_iota(jnp.int32, (blk, blk), 0)
        col = lax.broadcasted_iota(jnp.int32, (blk, blk), 1)
        tri_lt = col < row
        suffix = (row > col).astype(BF16)

        def step(items):
            where = [_sb_key_rows(i, done) for _, i, done, _, _, _, _ in items]
            zs = [_sb_logits(q, _sb_head(k_ref, h, rows)) for (h, _, _, q, _, _, _), (rows, _) in zip(items, where)]
            scored = [_sb_scores(z, done == 0, live, mask)
                      for z, (_, _, done, _, mask, _, _), (_, live) in zip(zs, items, where)]
            afters = [_dot_exact(log_skip, suffix) for log_skip, _, _ in scored]
            ws = [jnp.where(valid, jnp.exp(log_beta + after + c), 0.0).astype(BF16)
                  for (_, log_beta, valid), after, (_, _, _, _, _, c, _) in zip(scored, afters, items)]
            accs = [acc + _dot(w, _sb_head(v_ref, h, rows))
                    for (h, _, _, _, _, _, acc), (rows, _), w in zip(items, where, ws)]
            cs = [c + jnp.sum(log_skip, axis=1, keepdims=True)
                  for (log_skip, _, _), (_, _, _, _, _, c, _) in zip(scored, items)]
            return cs, accs

        def top_items(chains, done, cs, accs):
            return [(h, i, done, _sb_head(q_ref, h, _sb_rows(i, 0, top)), tri_lt[:top, :], c, acc)
                    for (h, i), c, acc in zip(chains, cs, accs)]

        def finish_tops(chains, state):
            def k_step(state):
                done, cs, accs = state
                cs, accs = step(top_items(chains, done, cs, accs))
                return done + 1, tuple(cs), tuple(accs)

            _, _, accs = lax.while_loop(functools.partial(_sb_keys_left, chains, slice(0, top)), k_step, state)
            for (h, i), acc in zip(chains, accs):
                o_ref[0, _sb_rows(i, 0, top), h * HEAD_DIM:(h + 1) * HEAD_DIM] = acc

        def q_group(m, before):
            chains, chains_before = _sb_chains(m), _sb_chains(m - 1)
            qis = [_sb_head(q_ref, h, _sb_rows(i, 0, blk)) for h, i in chains]
            n = len(chains)

            def items_of(done, cs, accs):
                return [(h, i, done, q, tri_lt, c, acc) for (h, i), q, c, acc in zip(chains, qis, cs, accs)]

            def k_step(state):
                done, cs, accs = state
                cs, accs = step(items_of(done, cs, accs))
                return done + 1, tuple(cs), tuple(accs)

            done_b, cs_b, accs_b = before
            cs0, accs0 = step(items_of(jnp.int32(0), (jnp.zeros((blk, 1), F32),) * n,
                                       (jnp.zeros((blk, HEAD_DIM), F32),) * n)
                              + top_items(chains_before, done_b, cs_b, accs_b))
            done, cs, accs = lax.while_loop(functools.partial(_sb_keys_left, chains, slice(top, blk)), k_step,
                                            (jnp.int32(1), tuple(cs0[:n]), tuple(accs0[:n])))

            @pl.when(m > 0)
            def _():
                finish_tops(chains_before, (done_b + 1, tuple(cs0[n:]), tuple(accs0[n:])))

            for (h, i), acc in zip(chains, accs):
                o_ref[0, _sb_rows(i, top, blk - top), h * HEAD_DIM:(h + 1) * HEAD_DIM] = acc[top:, :]
            return done, tuple(c[:top, :] for c in cs), tuple(acc[:top, :] for acc in accs)

        n = SB_HEADS_PER_STEP * SB_QBLOCKS_PER_STEP
        nothing = (jnp.int32(0), (jnp.zeros((top, 1), F32),) * n, (jnp.zeros((top, HEAD_DIM), F32),) * n)
        last = lax.fori_loop(0, ngroups, q_group, nothing)
        finish_tops(_sb_chains(ngroups - 1), last)

    (o,), extra = _pcall_riding(
        body, rider, (qkv4, qkv4, qkv4),
        name="sb_attn_fwd", grid=(b, N_HEADS // SB_HEADS_PER_STEP),
        in_specs=[_sb_group_spec(s, 0), _sb_group_spec(s, 1), _sb_group_spec(s, 2)],
        out_specs=[_sb_seq_group_spec(s)],
        out_shape=[jax.ShapeDtypeStruct((b, s, D_MODEL), F32)],
        semantics=("parallel", "parallel"))
    return o, extra


def _hg_masks():
    c = HG_CHUNK
    row = lax.broadcasted_iota(jnp.int32, (c, c), 0)
    col = lax.broadcasted_iota(jnp.int32, (c, c), 1)
    incl = (col <= row)
    lower = incl.astype(BF16)
    before_sub = (col < (row // HG_SUB) * HG_SUB).astype(BF16)
    upper = (col >= row).astype(BF16)
    return incl, lower, before_sub, upper


def _hg_lower_bound(lbl_ref):
    l0 = lbl_ref[0, 0]
    l1 = lbl_ref[1, 0]
    d = l1 - l0
    return _sigmoid_pair(d)


def _hg_gates(qp, fp, lb, oml):
    sq = _sigmoid(qp)
    sf, sfn = _sigmoid_pair(fp)
    f = lb + oml * sf
    return dict(qp=qp, sq=sq, q=qp * sq, sf=sf, sfn=sfn, f=f, k=oml * sfn, logf=jnp.log(f))


def _hg_intra(qds, ks, gcs, grs, incl):
    subs = range(HG_CHUNK // HG_SUB)
    qdbs = [qd.astype(BF16) for qd in qds]
    ess = [[jnp.exp(jnp.minimum(gr[sub * HG_SUB:sub * HG_SUB + 1, :] - gc, EXP_CLAMP)) for sub in subs]
           for gc, gr in zip(gcs, grs)]
    ksbs = [[(k * e).astype(BF16) for e in es] for k, es in zip(ks, ess)]
    rows = [[_dot(qdb[sub * HG_SUB:(sub + 1) * HG_SUB, :], ksb[sub], NT) for sub in subs]
            for qdb, ksb in zip(qdbs, ksbs)]
    a_s = [jnp.where(incl, jnp.concatenate(r, axis=0), 0.0) for r in rows]
    return a_s, qdbs, ksbs, ess


def _hg_group_rows(outer, unroll):
    ns = [outer * unroll + u for u in range(unroll)]
    return ns, [pl.ds(pl.multiple_of(n * HG_CHUNK, HG_CHUNK), HG_CHUNK) for n in ns]


def _state_spec(nchunk):
    return pl.BlockSpec((1, 1, nchunk, HEAD_DIM, HEAD_DIM), lambda b, h: (b, h, 0, 0, 0))


def hgrn2_fwd(proj4, lbl4, o_gain, rider=None):
    _, b, s, _ = proj4.shape
    nchunk = s // HG_CHUNK
    c = HG_CHUNK
    unroll = math.gcd(nchunk, HG_UNROLL_FWD)

    def body(q_ref, f_ref, i_ref, lbl_ref, og_ref, o_ref, oraw_ref, st_ref):
        incl, lower, before_sub, _ = _hg_masks()
        lb, oml = _hg_lower_bound(lbl_ref)

        def group(outer, st):
            ns, rows = _hg_group_rows(outer, unroll)
            vs = [_hg_gates(q_ref[0, 0, r, :], f_ref[0, 0, r, :], lb, oml) for r in rows]
            inps = [i_ref[0, 0, r, :].astype(BF16) for r in rows]
            gcs = [_dot_exact(v["logf"], lower, left=True) for v in vs]
            grs = [_dot_exact(v["logf"], before_sub, left=True) for v in vs]
            a_s, _, _, _ = _hg_intra([v["q"] * jnp.exp(gc - gr) for v, gc, gr in zip(vs, gcs, grs)],
                                     [v["k"] for v in vs], gcs, grs, incl)
            gls = [gc[c - 1:c, :] for gc in gcs]
            adds = [_dot(inp, (v["k"] * jnp.exp(gl - gc)).astype(BF16), TN)
                    for inp, v, gl, gc in zip(inps, vs, gls, gcs)]
            o_intra = [_dot(a.astype(BF16), inp) for a, inp in zip(a_s, inps)]
            sts = []
            for gl, add in zip(gls, adds):
                sts.append(st)
                st = st * jnp.exp(gl) + add
            outs = [oi + _dot((v["q"] * jnp.exp(gc)).astype(BF16), s0.astype(BF16), NT)
                    for oi, v, gc, s0 in zip(o_intra, vs, gcs, sts)]
            for n, r, s0, o in zip(ns, rows, sts, outs):
                st_ref[0, 0, n] = s0
                oraw_ref[0, r, :] = o
                o_ref[0, r, :] = o * _rms_scale(o) * og_ref[...]
            return st

        lax.fori_loop(0, nchunk // unroll, group, jnp.zeros((HEAD_DIM, HEAD_DIM), F32))

    seq = jax.ShapeDtypeStruct((b, s, D_MODEL), F32)
    return _pcall_riding(
        body, rider, (proj4, proj4, proj4, lbl4, o_gain), name="hgrn2_fwd", grid=(b, N_HEADS),
        in_specs=[_head_spec(s, 0), _head_spec(s, 1), _head_spec(s, 2),
                  pl.BlockSpec((2, 1, 1, HEAD_DIM), lambda b, h: (0, h, 0, 0)), _GAIN_SPEC],
        out_specs=[_seq_spec(s), _seq_spec(s), _state_spec(nchunk)],
        out_shape=[seq, seq, jax.ShapeDtypeStruct((b, N_HEADS, nchunk, HEAD_DIM, HEAD_DIM), F32)],
        semantics=("parallel", "parallel"))


def outproj_bwd(dh, w_out, o2, proj, name):
    t = dh.shape[0]
    tm = _row_tile(t, 512)

    def body(dh_ref, w_ref, o_ref, gate_ref, do_ref, dproj_ref, dw_ref):
        dhb = dh_ref[...].astype(BF16)
        dog = _dot(dhb, w_ref[...], NT)
        g = gate_ref[0]
        sg = _sigmoid(g)
        silu = g * sg
        o = o_ref[...]
        do_ref[...] = dog * silu
        dproj_ref[0] = dog * o * (sg * (1.0 + g * (1.0 - sg)))
        part = _dot((o * silu).astype(BF16), dhb, TN)

        @pl.when(pl.program_id(0) == 0)
        def _():
            dw_ref[...] = part

        @pl.when(pl.program_id(0) > 0)
        def _():
            dw_ref[...] += part

    row = pl.BlockSpec((tm, D_MODEL), lambda i: (i, 0))
    full = pl.BlockSpec((D_MODEL, D_MODEL), lambda i: (0, 0))
    return _pcall(
        body, name=name, grid=(t // tm,),
        in_specs=[row, full, row, pl.BlockSpec((1, tm, D_MODEL), lambda i: (3, i, 0))],
        out_specs=[row, pl.BlockSpec((1, tm, D_MODEL), lambda i: (3, i, 0)), full],
        out_shape=[jax.ShapeDtypeStruct((t, D_MODEL), F32),
                   jax.ShapeDtypeStruct((4, t, D_MODEL), F32),
                   jax.ShapeDtypeStruct((D_MODEL, D_MODEL), F32)],
        compiler_params=_params("arbitrary"),
    )(dh, w_out, o2, proj)


def inproj_bwd_dx(dproj, wg, x2, gain, dres, name, rider=None):
    t = x2.shape[0]
    tm = _row_tile(t, 512)

    def body(d_ref, w_ref, x_ref, g_ref, r_ref, dx_ref, dg_ref):
        du = jnp.zeros((tm, D_MODEL), F32)
        for p in range(N_DEV):
            cols = slice((p % 2) * W_COLS, (p % 2 + 1) * W_COLS)
            du = du + _dot(d_ref[p // 2, :, cols].astype(BF16), w_ref[p], NT)
        x = x_ref[...]
        r = _rms_scale(x)
        xh = x * r
        a = du * g_ref[...]
        dx_ref[...] = r_ref[...] + r * (a - xh * jnp.mean(a * xh, axis=-1, keepdims=True))
        part = jnp.sum(du * xh, axis=0, keepdims=True)

        @pl.when(pl.program_id(0) == 0)
        def _():
            dg_ref[...] = part

        @pl.when(pl.program_id(0) > 0)
        def _():
            dg_ref[...] += part

    row = pl.BlockSpec((tm, D_MODEL), lambda i: (i, 0))
    vec = pl.BlockSpec((1, D_MODEL), lambda i: (0, 0))
    return _pcall_riding(
        body, rider, (dproj, wg, x2, gain, dres), name=name, grid=(t // tm,),
        in_specs=[pl.BlockSpec((4, tm, D_MODEL), lambda i: (0, i, 0)),
                  pl.BlockSpec((N_DEV, D_MODEL, W_COLS), lambda i: (0, 0, 0)),
                  row, vec, row],
        out_specs=[row, vec],
        out_shape=[jax.ShapeDtypeStruct((t, D_MODEL), F32), jax.ShapeDtypeStruct((1, D_MODEL), F32)],
        semantics=("arbitrary",))


def inproj_bwd_dw(ut, dproj, name, out_dtype):
    t = ut.shape[1]

    def body(ut_ref, d_ref, dw_ref):
        dw_ref[0] = _dot(ut_ref[...], d_ref[0].astype(BF16)).astype(dw_ref.dtype)

    return _pcall(
        body, name=name, grid=(N_DEV,),
        in_specs=[pl.BlockSpec((D_MODEL, t), lambda j: (0, 0)),
                  pl.BlockSpec((1, t, W_COLS), lambda j: (j // 2, 0, j % 2))],
        out_specs=pl.BlockSpec((1, D_MODEL, W_COLS), lambda j: (j, 0, 0)),
        out_shape=jax.ShapeDtypeStruct((N_DEV, D_MODEL, W_COLS), out_dtype),
        compiler_params=_params("parallel"),
    )(ut, dproj)


def _rms_bwd(x, gain, dy):
    r = _rms_scale(x)
    xh = x * r
    a = dy * gain
    return r * (a - xh * jnp.mean(a * xh, axis=-1, keepdims=True)), dy * xh


def sb_attn_bwd(qkv4, proj4, do3, o3, q_gain, k_gain, dproj4, rider=None):
    _, b, s, _ = proj4.shape
    blk, top = ATTN_BLOCK, SB_TOP_ROWS
    nq = s // blk
    ngroups = nq // SB_QBLOCKS_PER_STEP
    assert ngroups * SB_QBLOCKS_PER_STEP * blk == s
    scale = HEAD_DIM ** -0.5

    def body(qn_ref, kn_ref, v_ref, q_ref, k_ref, do_ref, o_ref, qg_ref, kg_ref, _alias, d_ref, dqg_ref, dkg_ref, dob):
        for h in range(SB_HEADS_PER_STEP):
            dob[h] = do_ref[0, :, h * HEAD_DIM:(h + 1) * HEAD_DIM].astype(BF16)
        d_ref[...] = jnp.zeros_like(d_ref)
        row = lax.broadcasted_iota(jnp.int32, (blk, blk), 0)
        col = lax.broadcasted_iota(jnp.int32, (blk, blk), 1)
        tri_lt = col < row
        suffix = (row > col).astype(BF16)
        suffix_incl = (row >= col).astype(BF16)

        def step(items):
            heads = [it[0] for it in items]
            where = [_sb_key_rows(it[1], it[2]) for it in items]
            kjs = [_sb_head(kn_ref, h, rows) for h, (rows, _) in zip(heads, where)]
            zs = [_sb_logits(it[3], kj) for it, kj in zip(items, kjs)]
            dws = [_dot(it[4], _sb_head(v_ref, h, rows), NT) for it, h, (rows, _) in zip(items, heads, where)]
            scored = [_sb_scores(z, it[2] == 0, live, it[6]) for z, it, (_, live) in zip(zs, items, where)]
            afters = [_dot_exact(log_skip, suffix) for log_skip, _, _ in scored]
            wbs = [jnp.where(valid, jnp.exp(log_beta + after + it[7]), 0.0).astype(BF16)
                   for (_, log_beta, valid), after, it in zip(scored, afters, items)]
            gs = [dw * wb.astype(F32) for dw, wb in zip(dws, wbs)]
            befores = [it[5] - (_dot_exact(g, suffix_incl) + it[8]) for g, it in zip(gs, items)]
            dzbs = [jnp.where(valid, g - jnp.exp(log_beta) * (g + before), 0.0).astype(BF16)
                    for (_, log_beta, valid), g, before in zip(scored, gs, befores)]
            dqs = [it[9] + _dot(dzb, kj) for it, dzb, kj in zip(items, dzbs, kjs)]
            for it, h, (rows, _), wb, dzb in zip(items, heads, where, wbs, dzbs):
                cols = slice(h * HEAD_DIM, (h + 1) * HEAD_DIM)
                d_ref[2, 0, rows, cols] += _dot(wb, it[4], TN)
                d_ref[1, 0, rows, cols] += _dot(dzb, it[3], TN)
            cs = [it[7] + jnp.sum(log_skip, axis=1, keepdims=True) for (log_skip, _, _), it in zip(scored, items)]
            cgs = [it[8] + jnp.sum(g, axis=1, keepdims=True) for g, it in zip(gs, items)]
            return cs, cgs, dqs

        def top_items(chains, deltas, done, cs, cgs, dqs):
            return [(h, i, done, _sb_head(qn_ref, h, _sb_rows(i, 0, top)), dob[h, _sb_rows(i, 0, top), :], delta,
                     tri_lt[:top, :], c, cg, dq)
                    for (h, i), delta, c, cg, dq in zip(chains, deltas, cs, cgs, dqs)]

        def finish_tops(chains, deltas, state):
            def k_step(state):
                done, cs, cgs, dqs = state
                cs, cgs, dqs = step(top_items(chains, deltas, done, cs, cgs, dqs))
                return done + 1, tuple(cs), tuple(cgs), tuple(dqs)

            _, _, _, dqs = lax.while_loop(functools.partial(_sb_keys_left, chains, slice(0, top)), k_step, state)
            for (h, i), dq in zip(chains, dqs):
                d_ref[0, 0, _sb_rows(i, 0, top), h * HEAD_DIM:(h + 1) * HEAD_DIM] = dq * scale

        def q_group(m, before):
            chains, chains_before = _sb_chains(m), _sb_chains(m - 1)
            deltas_b, before = before[0], before[1:]
            qis, dois, deltas = [], [], []
            for h, i in chains:
                rows_i = _sb_rows(i, 0, blk)
                qis.append(_sb_head(qn_ref, h, rows_i))
                dois.append(dob[h, rows_i, :])
                deltas.append(jnp.sum(dois[-1].astype(F32) * o_ref[0, rows_i, h * HEAD_DIM:(h + 1) * HEAD_DIM],
                                      axis=1, keepdims=True))
            n = len(chains)

            def items_of(done, cs, cgs, dqs):
                return [(h, i, done, q, do, delta, tri_lt, c, cg, dq)
                        for (h, i), q, do, delta, c, cg, dq in zip(chains, qis, dois, deltas, cs, cgs, dqs)]

            def k_step(state):
                done, cs, cgs, dqs = state
                cs, cgs, dqs = step(items_of(done, cs, cgs, dqs))
                return done + 1, tuple(cs), tuple(cgs), tuple(dqs)

            done_b, cs_b, cgs_b, dqs_b = before
            zero = (jnp.zeros((blk, 1), F32),) * n
            new = step(items_of(jnp.int32(0), zero, zero, (jnp.zeros((blk, HEAD_DIM), F32),) * n)
                       + top_items(chains_before, deltas_b, done_b, cs_b, cgs_b, dqs_b))
            done, cs, cgs, dqs = lax.while_loop(functools.partial(_sb_keys_left, chains, slice(top, blk)), k_step,
                                                (jnp.int32(1),) + tuple(tuple(x[:n]) for x in new))

            @pl.when(m > 0)
            def _():
                finish_tops(chains_before, deltas_b, (done_b + 1,) + tuple(tuple(x[n:]) for x in new))

            for (h, i), dq in zip(chains, dqs):
                d_ref[0, 0, _sb_rows(i, top, blk - top), h * HEAD_DIM:(h + 1) * HEAD_DIM] = dq[top:, :] * scale
            first = lambda xs: tuple(x[:top, :] for x in xs)
            return first(deltas), done, first(cs), first(cgs), first(dqs)

        n = SB_HEADS_PER_STEP * SB_QBLOCKS_PER_STEP
        zero = (jnp.zeros((top, 1), F32),) * n
        nothing = (zero, jnp.int32(0), zero, zero, (jnp.zeros((top, HEAD_DIM), F32),) * n)
        last = lax.fori_loop(0, ngroups, q_group, nothing)
        finish_tops(_sb_chains(ngroups - 1), last[0], last[1:])

        def norm_block(i, carry):
            rows = pl.ds(pl.multiple_of(i * blk, blk), blk)
            out = []
            for h in range(SB_HEADS_PER_STEP):
                cols = slice(h * HEAD_DIM, (h + 1) * HEAD_DIM)
                for part, src_ref, gain_ref in ((0, q_ref, qg_ref), (1, k_ref, kg_ref)):
                    dy = d_ref[part, 0, rows, cols] * (scale if part == 1 else 1.0)
                    dx, pg = _rms_bwd(src_ref[0, 0, rows, cols], gain_ref[...], dy)
                    d_ref[part, 0, rows, cols] = dx
                    out.append(carry[len(out)] + jnp.sum(pg, axis=0, keepdims=True))
            return tuple(out)

        sums = lax.fori_loop(0, nq, norm_block, (jnp.zeros((1, HEAD_DIM), F32),) * (2 * SB_HEADS_PER_STEP))
        for h in range(SB_HEADS_PER_STEP):
            dqg_ref[0, h] = sums[2 * h]
            dkg_ref[0, h] = sums[2 * h + 1]

    head_row = jax.ShapeDtypeStruct((b, N_HEADS, 1, HEAD_DIM), F32)
    return _pcall_riding(
        body, rider, (qkv4, qkv4, qkv4, proj4, proj4, do3, o3, q_gain, k_gain, dproj4),
        name="sb_attn_bwd", grid=(b, N_HEADS // SB_HEADS_PER_STEP),
        in_specs=[_sb_group_spec(s, 0), _sb_group_spec(s, 1), _sb_group_spec(s, 2),
                  _sb_group_spec(s, 0), _sb_group_spec(s, 1),
                  _sb_seq_group_spec(s), _sb_seq_group_spec(s), _GAIN_SPEC, _GAIN_SPEC,
                  pl.BlockSpec(memory_space=pl.ANY)],
        out_specs=[pl.BlockSpec((3, 1, s, SB_GROUP_COLS), lambda b, g: (0, b, 0, g)),
                   _SB_GROUP_ROW_SPEC, _SB_GROUP_ROW_SPEC],
        out_shape=[jax.ShapeDtypeStruct(dproj4.shape, F32), head_row, head_row],
        scratch_shapes=[pltpu.VMEM((SB_HEADS_PER_STEP, s, HEAD_DIM), BF16)],
        input_output_aliases={9: 0}, semantics=("parallel", "parallel"))


def hgrn2_bwd(proj4, don3, oraw3, states, lbl4, o_gain, dproj4, rider=None):
    _, b, s, _ = proj4.shape
    nchunk = s // HG_CHUNK
    c = HG_CHUNK
    subs = range(HG_CHUNK // HG_SUB)
    unroll = math.gcd(nchunk, HG_UNROLL_BWD)
    ngroup = nchunk // unroll

    def body(q_ref, f_ref, i_ref, don_ref, oraw_ref, st_ref, lbl_ref, og_ref, _alias, d_ref, dog_ref, dlb_ref):
        incl, lower, before_sub, upper = _hg_masks()
        lb, oml = _hg_lower_bound(lbl_ref)
        last_row = lax.broadcasted_iota(jnp.int32, (c, HEAD_DIM), 0) == c - 1

        def group(m, carry):
            dst, dog_acc, dlb_acc = carry
            ns, rows = _hg_group_rows(ngroup - 1 - m, unroll)
            ns, rows = ns[::-1], rows[::-1]
            vs = [_hg_gates(q_ref[0, 0, r, :], f_ref[0, 0, r, :], lb, oml) for r in rows]
            inps = [i_ref[0, 0, r, :].astype(BF16) for r in rows]
            sts = [st_ref[0, 0, n] for n in ns]
            gcs = [_dot_exact(v["logf"], lower, left=True) for v in vs]
            grs = [_dot_exact(v["logf"], before_sub, left=True) for v in vs]
            e_qs = [jnp.exp(gc - gr) for gc, gr in zip(gcs, grs)]
            a_s, qdbs, ksbs, ess = _hg_intra([v["q"] * e for v, e in zip(vs, e_qs)], [v["k"] for v in vs],
                                             gcs, grs, incl)
            e_gcs = [jnp.exp(gc) for gc in gcs]
            gls = [gc[c - 1:c, :] for gc in gcs]
            e_gls = [jnp.exp(gl) for gl in gls]
            e_ks = [jnp.exp(gl - gc) for gl, gc in zip(gls, gcs)]
            normed = [_rms_bwd(oraw_ref[0, r, :], og_ref[...], don_ref[0, r, :]) for r in rows]
            dobs = [do.astype(BF16) for do, _ in normed]
            dabs = [jnp.where(incl, _dot(dob, inp, NT), 0.0).astype(BF16) for dob, inp in zip(dobs, inps)]
            adds = [_dot(dob, (v["q"] * e).astype(BF16), TN) for dob, v, e in zip(dobs, vs, e_gcs)]
            dq_inters = [_dot(dob, st.astype(BF16)) * e for dob, st, e in zip(dobs, sts, e_gcs)]
            dqds = [jnp.concatenate([_dot(dab[sub * HG_SUB:(sub + 1) * HG_SUB, :], ksb[sub]) for sub in subs], axis=0)
                    for dab, ksb in zip(dabs, ksbs)]
            dkss = [[_dot(dab[sub * HG_SUB:(sub + 1) * HG_SUB, :], qdb[sub * HG_SUB:(sub + 1) * HG_SUB, :], TN)
                     for sub in subs] for dab, qdb in zip(dabs, qdbs)]
            dsts = []
            for e_gl, add in zip(e_gls, adds):
                dsts.append(dst)
                dst = dst * e_gl + add
            dstbs = [d.astype(BF16) for d in dsts]
            dis = [_dot(a.astype(BF16), dob, TN) + _dot((v["k"] * e_k).astype(BF16), dstb, NT)
                   for a, dob, v, e_k, dstb in zip(a_s, dobs, vs, e_ks, dstbs)]
            dk_inters = [_dot(inp, dstb) * e_k for inp, dstb, e_k in zip(inps, dstbs, e_ks)]
            dks, dgcs = [], []
            for u in range(len(rows)):
                q, k = vs[u]["q"], vs[u]["k"]
                dk, dgc_k = dk_inters[u], jnp.zeros((c, HEAD_DIM), F32)
                for sub in subs:
                    dk = dk + dkss[u][sub] * ess[u][sub]
                    dgc_k = dgc_k + dkss[u][sub] * ksbs[u][sub].astype(F32)
                at_last = (jnp.sum(k * dk_inters[u], axis=0, keepdims=True)
                           + e_gls[u] * jnp.sum(sts[u] * dsts[u], axis=0, keepdims=True))
                dks.append(dk)
                dgcs.append((qdbs[u].astype(F32) * dqds[u] - dgc_k) + (q * dq_inters[u] - k * dk_inters[u])
                            + jnp.where(last_row, at_last, 0.0))
            dlf_fs = [_dot_exact(dgc, upper, left=True) / v["f"] for dgc, v in zip(dgcs, vs)]
            for u, r in enumerate(rows):
                v = vs[u]
                dq = dqds[u] * e_qs[u] + dq_inters[u]
                d_ref[0, 0, r, :] = dq * (v["sq"] * (1.0 + v["qp"] * (1.0 - v["sq"])))
                d_ref[1, 0, r, :] = (dlf_fs[u] - dks[u]) * (oml * v["sf"] * v["sfn"])
                d_ref[2, 0, r, :] = dis[u]
                dlb_acc = dlb_acc + jnp.sum((dlf_fs[u] - dks[u]) * v["sfn"], axis=0, keepdims=True)
                dog_acc = dog_acc + jnp.sum(normed[u][1], axis=0, keepdims=True)
            return dst, dog_acc, dlb_acc

        zero = jnp.zeros((1, HEAD_DIM), F32)
        _, dog, dlb = lax.fori_loop(0, ngroup, group, (jnp.zeros((HEAD_DIM, HEAD_DIM), F32), zero, zero))
        dog_ref[0, 0] = dog
        dlb_ref[0, 0] = dlb

    head_row = jax.ShapeDtypeStruct((b, N_HEADS, 1, HEAD_DIM), F32)
    return _pcall_riding(
        body, rider, (proj4, proj4, proj4, don3, oraw3, states, lbl4, o_gain, dproj4),
        name="hgrn2_bwd", grid=(b, N_HEADS),
        in_specs=[_head_spec(s, 0), _head_spec(s, 1), _head_spec(s, 2), _seq_spec(s), _seq_spec(s),
                  _state_spec(nchunk), pl.BlockSpec((2, 1, 1, HEAD_DIM), lambda b, h: (0, h, 0, 0)), _GAIN_SPEC,
                  pl.BlockSpec(memory_space=pl.ANY)],
        out_specs=[pl.BlockSpec((3, 1, s, HEAD_DIM), lambda b, h: (0, b, 0, h)), _HEAD_ROW_SPEC, _HEAD_ROW_SPEC],
        out_shape=[jax.ShapeDtypeStruct(dproj4.shape, F32), head_row, head_row],
        input_output_aliases={8: 0}, semantics=("parallel", "parallel"))


def local_step(x, target, sb_norm, wsi, sb_q_gain, sb_k_gain, hg_o_gain, hg_lb_logits, wso_mine, whi_mine, who_mine,
               hg_norm_mine):
    b, s, _ = x.shape
    t = b * s
    x2 = x.reshape(t, D_MODEL)
    tg2 = target.reshape(t, D_MODEL)
    lbl4 = hg_lb_logits.reshape(2, N_HEADS, 1, HEAD_DIM)
    four = (4, b, s, D_MODEL)
    three = (b, s, D_MODEL)
    rows8 = (N_DEV, W_ROWS, D_MODEL)

    (proj0, u0, qkv0), (wso, hgn) = rms_inproj(x2, sb_norm, wsi, "sb_inproj",
                                               _Rider([wso_mine, hg_norm_mine], scatter=False),
                                               qk_gains=(sb_q_gain, sb_k_gain))
    qkv0 = qkv0.reshape(3, b, s, D_MODEL)
    wso = wso.reshape(D_MODEL, D_MODEL)
    hg_norm_full = hgn[:, 0, :].reshape(1, D_MODEL)
    o0, (whi,) = sb_attn_fwd(qkv0, _Rider([whi_mine], scatter=False))
    o0 = o0.reshape(t, D_MODEL)
    h1 = gate_outproj(o0, proj0, wso, x2, None, "sb_outproj")
    (proj1, u1), _ = rms_inproj(h1, hg_norm_full, whi, "hg_inproj")
    (o1, o1_raw, states), (who,) = hgrn2_fwd(proj1.reshape(four), lbl4, hg_o_gain,
                                             _Rider([who_mine], scatter=False))
    who = who.reshape(D_MODEL, D_MODEL)
    o1 = o1.reshape(t, D_MODEL)
    dh2, loss_parts = gate_outproj(o1, proj1, who, h1, tg2, "hg_outproj_loss")

    do1, dproj1, g_who = outproj_bwd(dh2, who, o1, proj1, "hg_outproj_bwd")
    (dproj1, g_og, g_lb), (p_who,) = hgrn2_bwd(proj1.reshape(four), do1.reshape(three), o1_raw, states, lbl4,
                                               hg_o_gain, dproj1.reshape(four),
                                               _Rider([g_who.reshape(rows8)], scatter=True))
    dproj1 = dproj1.reshape(4, t, D_MODEL)
    (dh1, g_hgn), _ = inproj_bwd_dx(dproj1, whi, h1, hg_norm_full, dh2, "hg_inproj_bwd_dx")
    g_whi = inproj_bwd_dw(u1, dproj1, "hg_inproj_bwd_dw", out_dtype=BF16)

    do0, dproj0, g_wso = outproj_bwd(dh1, wso, o0, proj0, "sb_outproj_bwd")
    (dproj0, g_qg, g_kg), (p_whi, p_wso) = sb_attn_bwd(qkv0, proj0.reshape(four), do0.reshape(three), o0.reshape(three),
                                                       sb_q_gain, sb_k_gain, dproj0.reshape(four),
                                                       _Rider([g_whi, g_wso.reshape(rows8)], scatter=True))
    dproj0 = dproj0.reshape(4, t, D_MODEL)
    g_wsi = inproj_bwd_dw(u0, dproj0, "sb_inproj_bwd_dw", out_dtype=BF16)
    (gx, g_sbn), (p_wsi,) = inproj_bwd_dx(dproj0, wsi, x2, sb_norm, dh1, "sb_inproj_bwd_dx",
                                          _Rider([sum_within_chip(g_wsi)], scatter=True, chips=True))
    return dict(loss_parts=loss_parts, gx=gx.reshape(three), p_wsi=p_wsi, p_wso=p_wso, p_whi=p_whi, p_who=p_who,
                g_sbn=g_sbn, g_hgn=g_hgn, g_qg=g_qg, g_kg=g_kg, g_og=g_og, g_lb=g_lb)


def sum_within_chip(g):
    _, r, c_ = g.shape
    chips = N_DEV // 2

    def swap(g_ref, got_ref, send_sems, recv_sems):
        x, y, c = _mesh_pos()
        copies = [pltpu.make_async_remote_copy(
            src_ref=g_ref.at[2 * q + 1 - c], dst_ref=got_ref.at[q], send_sem=send_sems.at[q], recv_sem=recv_sems.at[q],
            device_id=(x, y, 1 - c), device_id_type=MESH) for q in range(chips)]
        for cp in copies:
            cp.start()
        for cp in copies:
            cp.wait_recv()
            cp.wait_send()

    got = _pcall(swap, name="swap_with_sibling", in_specs=[_ANY], out_specs=_ANY,
                 out_shape=jax.ShapeDtypeStruct((chips, r, c_), g.dtype),
                 scratch_shapes=[pltpu.SemaphoreType.DMA((chips,)), pltpu.SemaphoreType.DMA((chips,))])(g)

    def add(g_ref, got_ref, out_ref):
        mine = g_ref[0, lax.axis_index("c")]
        out_ref[0] = (mine.astype(F32) + got_ref[0].astype(F32)).astype(out_ref.dtype)

    return _pcall(
        add, name="add_sibling_partials", grid=(chips,),
        in_specs=[pl.BlockSpec((1, 2, r, c_), lambda q: (q, 0, 0, 0)), pl.BlockSpec((1, r, c_), lambda q: (q, 0, 0))],
        out_specs=pl.BlockSpec((1, r, c_), lambda q: (q, 0, 0)),
        out_shape=jax.ShapeDtypeStruct((chips, r, c_), g.dtype),
        compiler_params=_params("parallel"),
    )(g.reshape(chips, 2, r, c_), got)


def _two_level_gather(src, out, send_sems, recv_sems, local_sem, pos):
    x, y, c = pos
    me, sibling = (x, y, c), (x, y, 1 - c)
    chips = [(1 - x, y), (x, 1 - y), (1 - x, 1 - y)]

    def copy(k, block, to, source=None):
        slot = out.at[_linear(block)]
        return pltpu.make_async_remote_copy(
            src_ref=slot if source is None else source, dst_ref=slot, send_sem=send_sems.at[k],
            recv_sem=recv_sems.at[k], device_id=to, device_id_type=MESH)

    mine = pltpu.make_async_copy(src, out.at[_linear(me)], local_sem)
    mine.start()
    first = [copy(0, me, sibling, src)] + [copy(1 + j, me, (*chip, c), src) for j, chip in enumerate(chips)]
    for cp in first:
        cp.start()
    passed = [copy(4 + j, (*chip, c), sibling) for j, chip in enumerate(chips)]
    for j, chip in enumerate(chips):
        copy(1 + j, (*chip, c), me).wait_recv()
        passed[j].start()
    copy(0, sibling, me).wait_recv()
    for j, chip in enumerate(chips):
        copy(4 + j, (*chip, 1 - c), me).wait_recv()
    for cp in first + passed:
        cp.wait_send()
    mine.wait()


def gather_first_weights(w_si, w_so, w_hi, w_ho, hg_norm):
    def body(si_ref, so_ref, hi_ref, ho_ref, hn_ref, o_si, so_b, hi_b, ho_b, hn_b, si_b, send_sems, recv_sems, local_sem):
        for src, buf in ((si_ref, si_b), (so_ref, so_b), (hi_ref, hi_b), (ho_ref, ho_b)):
            buf[...] = src[...].astype(BF16)
        hn_b[...] = jnp.broadcast_to(hn_ref[...], hn_b.shape)
        _two_level_gather(si_b, o_si, send_sems, recv_sems, local_sem, _mesh_pos())

    return _pcall(
        body, name="gather_first_weights",
        in_specs=[_VMEM] * 5, out_specs=[_ANY] + [_VMEM] * 4,
        out_shape=[jax.ShapeDtypeStruct((N_DEV,) + w_si.shape, BF16), jax.ShapeDtypeStruct(w_so.shape, BF16),
                   jax.ShapeDtypeStruct(w_hi.shape, BF16), jax.ShapeDtypeStruct(w_ho.shape, BF16),
                   jax.ShapeDtypeStruct((8, HEAD_DIM), F32)],
        scratch_shapes=[pltpu.VMEM(w_si.shape, BF16), pltpu.SemaphoreType.DMA((N_PEERS,)),
                        pltpu.SemaphoreType.DMA((N_PEERS,)), pltpu.SemaphoreType.DMA],
        compiler_params=pltpu.CompilerParams(vmem_limit_bytes=VMEM_LIMIT_BYTES),
    )(w_si, w_so, w_hi, w_ho, hg_norm)


def _adamw(w, g, m, v):
    m = ADAM_B1 * m + (1.0 - ADAM_B1) * g
    v = ADAM_B2 * v + (1.0 - ADAM_B2) * (g * g)
    m_hat = m / (1.0 - ADAM_B1 ** ADAM_STEP)
    v_hat = v / (1.0 - ADAM_B2 ** ADAM_STEP)
    delta = -ADAM_LR * (m_hat / (jnp.sqrt(v_hat) + ADAM_EPS) + ADAM_WD * w)
    return delta, m, v


def reduce_adamw(parts, w, m, v, name):
    n, r, c = parts.shape
    tr = _row_tile(r, 256)

    def body(p_ref, w_ref, m_ref, v_ref, g_ref, d_ref, m2_ref, v2_ref):
        g = p_ref[0].astype(F32)
        for sender in range(1, n):
            g = g + p_ref[sender].astype(F32)
        g_ref[...] = g
        d_ref[...], m2_ref[...], v2_ref[...] = _adamw(w_ref[...], g, m_ref[...], v_ref[...])

    tile = pl.BlockSpec((tr, c), lambda i: (i, 0))
    return _pcall(
        body, name=name, grid=(r // tr,),
        in_specs=[pl.BlockSpec((n, tr, c), lambda i: (0, i, 0)), tile, tile, tile],
        out_specs=[tile] * 4, out_shape=[jax.ShapeDtypeStruct((r, c), F32)] * 4,
        compiler_params=_params("parallel"),
    )(parts, w, m, v)


PACK_ROWS = 32
ROW_SBN, ROW_HGN, ROW_LB, ROW_QG, ROW_KG, ROW_OG, ROW_LOSS = 0, 8, 16, 24, 25, 26, 27


def small_update(g_sbn, g_hgn, g_lb, g_qg, g_kg, g_og, loss_parts, small):
    n_in = 7 + len(small)

    def body(*refs):
        sbn_ref, hgn_ref, lb_ref, qg_ref, kg_ref, og_ref, loss_ref = refs[:7]
        wmv = refs[7:n_in]
        outs = refs[n_in:n_in + 25]
        pack, gath, tot, send_sems, recv_sems, local_sems = refs[n_in + 25:]
        pos = _mesh_pos()
        me = _linear(pos)
        pack[...] = jnp.zeros_like(pack)
        pack[ROW_SBN:ROW_SBN + 8, :] = sbn_ref[...]
        pack[ROW_HGN:ROW_HGN + 8, :] = hgn_ref[...]
        pack[ROW_LB:ROW_LB + 8, :] = jnp.sum(lb_ref[...], axis=0)
        pack[ROW_QG:ROW_QG + 1, :] = jnp.sum(qg_ref[...], axis=0, keepdims=True)
        pack[ROW_KG:ROW_KG + 1, :] = jnp.sum(kg_ref[...], axis=0, keepdims=True)
        pack[ROW_OG:ROW_OG + 1, :] = jnp.sum(og_ref[...], axis=0, keepdims=True)
        pack[ROW_LOSS:ROW_LOSS + 1, :] = jnp.sum(loss_ref[...], axis=0)[0:1, :]
        _exchange_start([((lambda p: pack), gath)], send_sems, recv_sems, local_sems, pos)
        _exchange_wait([((lambda p: pack), gath)], send_sems, recv_sems, local_sems, pos)
        total = gath[0]
        for dev in range(1, N_DEV):
            total = total + gath[dev]
        tot[...] = total
        outs[0][...] = jnp.broadcast_to(tot[ROW_LOSS:ROW_LOSS + 1, :], (8, HEAD_DIM))
        l0 = wmv[15][0:8, :]
        l1 = wmv[15][8:16, :]
        p1, p0 = _sigmoid_pair(l1 - l0)
        d_l1 = p0 * p1 * tot[ROW_LB:ROW_LB + 8, :]
        grads = [tot[ROW_SBN:ROW_SBN + 8, :], tot[ROW_QG:ROW_QG + 1, :], tot[ROW_KG:ROW_KG + 1, :],
                 tot[pl.ds(ROW_HGN + me, 1), :], tot[ROW_OG:ROW_OG + 1, :],
                 jnp.concatenate([-d_l1, d_l1], axis=0)]
        for i, g in enumerate(grads):
            w_ref, m_ref, v_ref = wmv[3 * i:3 * i + 3]
            o = outs[1 + 4 * i:5 + 4 * i]
            o[0][...] = g
            o[1][...], o[2][...], o[3][...] = _adamw(w_ref[...], g, m_ref[...], v_ref[...])

    out_shape = [jax.ShapeDtypeStruct((8, HEAD_DIM), F32)]
    for i in range(6):
        out_shape += [jax.ShapeDtypeStruct(small[3 * i].shape, F32)] * 4
    return _pcall(
        body, name="small_update",
        in_specs=[_VMEM] * n_in, out_specs=[_VMEM] * 25, out_shape=out_shape,
        scratch_shapes=[pltpu.VMEM((PACK_ROWS, HEAD_DIM), F32), pltpu.VMEM((N_DEV, PACK_ROWS, HEAD_DIM), F32),
                        pltpu.VMEM((PACK_ROWS, HEAD_DIM), F32)] + _exchange_sems(1),
    )(g_sbn, g_hgn, g_lb, g_qg, g_kg, g_og, loss_parts, *small)


def kernel(x, sb_norm, sb_w_in, sb_q_gain, sb_k_gain, sb_w_out, hg_norm, hg_w_in, hg_o_gain, hg_w_out, hg_lb_logits, loss_target, m_sb_norm, m_sb_w_in, m_sb_q_gain, m_sb_k_gain, m_sb_w_out, m_hg_norm, m_hg_w_in, m_hg_o_gain, m_hg_w_out, m_hg_lb_logits, v_sb_norm, v_sb_w_in, v_sb_q_gain, v_sb_k_gain, v_sb_w_out, v_hg_norm, v_hg_w_in, v_hg_o_gain, v_hg_w_out, v_hg_lb_logits):
    b = x.shape[0]
    wsi, wso_mine, whi_mine, who_mine, hg_norm_mine = gather_first_weights(
        sb_w_in[0], sb_w_out[0], hg_w_in[0], hg_w_out[0], hg_norm)
    r = local_step(x, loss_target, sb_norm, wsi, sb_q_gain, sb_k_gain, hg_o_gain, hg_lb_logits,
                   wso_mine, whi_mine, who_mine, hg_norm_mine)
    big = {}
    for name, p, w, m, v in (("sb_w_in", r["p_wsi"], sb_w_in, m_sb_w_in, v_sb_w_in),
                             ("sb_w_out", r["p_wso"], sb_w_out, m_sb_w_out, v_sb_w_out),
                             ("hg_w_in", r["p_whi"], hg_w_in, m_hg_w_in, v_hg_w_in),
                             ("hg_w_out", r["p_who"], hg_w_out, m_hg_w_out, v_hg_w_out)):
        big[name] = [o[None] for o in reduce_adamw(p, w[0], m[0], v[0], "adamw_" + name)]

    def rows8(a):
        return a.reshape(8, HEAD_DIM)

    def rows16(a):
        return a.reshape(16, HEAD_DIM)

    small_in = [rows8(sb_norm), rows8(m_sb_norm), rows8(v_sb_norm),
                sb_q_gain, m_sb_q_gain, v_sb_q_gain,
                sb_k_gain, m_sb_k_gain, v_sb_k_gain,
                hg_norm, m_hg_norm, v_hg_norm,
                hg_o_gain, m_hg_o_gain, v_hg_o_gain,
                rows16(hg_lb_logits), rows16(m_hg_lb_logits), rows16(v_hg_lb_logits)]
    so = small_update(rows8(r["g_sbn"]), rows8(r["g_hgn"]), r["g_lb"].reshape(b, N_HEADS, HEAD_DIM),
                      r["g_qg"].reshape(b * N_HEADS, HEAD_DIM), r["g_kg"].reshape(b * N_HEADS, HEAD_DIM),
                      r["g_og"].reshape(b * N_HEADS, HEAD_DIM), r["loss_parts"], small_in)
    loss = so[0][0, 0]
    shapes = {"sb_norm": (1, D_MODEL), "sb_q_gain": (1, HEAD_DIM), "sb_k_gain": (1, HEAD_DIM),
              "hg_norm": (1, HEAD_DIM), "hg_o_gain": (1, HEAD_DIM), "hg_lb_logits": (2, D_MODEL)}
    small = {}
    for i, name in enumerate(("sb_norm", "sb_q_gain", "sb_k_gain", "hg_norm", "hg_o_gain", "hg_lb_logits")):
        small[name] = [o.reshape(shapes[name]) for o in so[1 + 4 * i:5 + 4 * i]]
    order = ("sb_norm", "sb_w_in", "sb_q_gain", "sb_k_gain", "sb_w_out",
             "hg_norm", "hg_w_in", "hg_o_gain", "hg_w_out", "hg_lb_logits")
    res = {**big, **small}
    return (loss, r["gx"]) + tuple(res[n][j] for j in range(4) for n in order)
```

```python
import functools
import math

import jax
import jax.numpy as jnp
from jax import lax
from jax.experimental import pallas as pl
from jax.experimental.pallas import tpu as pltpu

F32 = jnp.float32
BF16 = jnp.bfloat16

N_DEV = 8
D_MODEL = 1024
N_HEADS = 8
HEAD_DIM = 128
RMS_EPS = 1e-6
ATTN_BLOCK = 128
HG_CHUNK = 64
HG_SUB = 16
HG_UNROLL_FWD = 16
HG_UNROLL_BWD = 16
EXP_CLAMP = 80.0
SB_HEADS_PER_STEP = 2
SB_QBLOCKS_PER_STEP = 4
SB_GROUP_COLS = SB_HEADS_PER_STEP * 128
SB_TOP_ROWS = 32
SB_LOG_WEIGHT_FLOOR = -104.0
VMEM_LIMIT_BYTES = 48 * 1024 * 1024
W_COLS = 4 * D_MODEL // N_DEV
W_ROWS = D_MODEL // N_DEV

ADAM_LR = 0.001
ADAM_B1 = 0.9
ADAM_B2 = 0.999
ADAM_EPS = 1e-08
ADAM_WD = 0.01
ADAM_STEP = 10

NT = (((1,), (1,)), ((), ()))
TN = (((0,), (0,)), ((), ()))
NN = (((1,), (0,)), ((), ()))


def _pcall(body, *, name, **kw):
    return pl.pallas_call(body, name=name, **kw)


def _params(*sem):
    return pltpu.CompilerParams(dimension_semantics=sem, vmem_limit_bytes=VMEM_LIMIT_BYTES)


def _dot(a, b, dims=NN):
    return lax.dot_general(a, b, dims, preferred_element_type=F32)


def _dot_exact(a, m, dims=NN, left=False):
    hi = a.astype(BF16)
    lo = (a - hi.astype(F32)).astype(BF16)
    if left:
        return _dot(m, hi, dims) + _dot(m, lo, dims)
    return _dot(hi, m, dims) + _dot(lo, m, dims)


def _sigmoid(x):
    return 1.0 / (1.0 + jnp.exp(-x))


def _sigmoid_pair(x):
    e = jnp.exp(-jnp.abs(x))
    big = 1.0 / (1.0 + e)
    small = e * big
    pos = x >= 0
    return jnp.where(pos, big, small), jnp.where(pos, small, big)


def _rms_scale(x):
    return lax.rsqrt(jnp.mean(x * x, axis=-1, keepdims=True) + RMS_EPS)


def _row_tile(t, want):
    return want if t % want == 0 else t


MESH = pl.DeviceIdType.MESH
N_PEERS = N_DEV - 1
_ANY = pl.BlockSpec(memory_space=pl.ANY)
_VMEM = pl.BlockSpec(memory_space=pltpu.VMEM)


def _mesh_pos():
    return lax.axis_index("x"), lax.axis_index("y"), lax.axis_index("c")


def _linear(pos):
    return 4 * pos[0] + 2 * pos[1] + pos[2]


def _peer(pos, k):
    flips = ((k + 1) >> 2 & 1, (k + 1) >> 1 & 1, (k + 1) & 1)
    return tuple(1 - p if f else p for p, f in zip(pos, flips))


def _chip(pos):
    return 2 * pos[0] + pos[1]


ALL_PEERS = tuple(range(N_PEERS))
SAME_CORE_PEERS = (1, 3, 5)


def _exchange_copies(pairs, send_sems, recv_sems, local_sems, pos, landing, peers, slot):
    me = slot(pos)
    local, remote = [], []
    for a, (src_of, dst) in enumerate(pairs):
        local.append(pltpu.make_async_copy(src_of(pos), dst.at[me], local_sems.at[a]))
        for k in peers:
            peer = _peer(pos, k)
            remote.append(pltpu.make_async_remote_copy(
                src_ref=src_of(peer), dst_ref=dst.at[slot(peer) if landing else me],
                send_sem=send_sems.at[a, k], recv_sem=recv_sems.at[a, k], device_id=peer, device_id_type=MESH))
    return local, remote


def _exchange_start(pairs, send_sems, recv_sems, local_sems, pos, peers=ALL_PEERS, slot=_linear):
    local, sent = _exchange_copies(pairs, send_sems, recv_sems, local_sems, pos, False, peers, slot)
    for copy in local + sent:
        copy.start()


def _exchange_wait(pairs, send_sems, recv_sems, local_sems, pos, peers=ALL_PEERS, slot=_linear):
    local, landed = _exchange_copies(pairs, send_sems, recv_sems, local_sems, pos, True, peers, slot)
    for copy in landed:
        copy.wait_recv()
        copy.wait_send()
    for copy in local:
        copy.wait()


def _exchange_sems(n):
    return [pltpu.SemaphoreType.DMA((n, N_PEERS)), pltpu.SemaphoreType.DMA((n, N_PEERS)),
            pltpu.SemaphoreType.DMA((n,))]


class _Rider:
    def __init__(self, arrays, scatter, chips=False):
        self.arrays = list(arrays)
        self.scatter = scatter
        self.peers = SAME_CORE_PEERS if chips else ALL_PEERS
        self.slot = _chip if chips else _linear
        self.out_shapes = [jax.ShapeDtypeStruct(a.shape if scatter else (N_DEV,) + a.shape, a.dtype)
                           for a in self.arrays]

    def pairs(self, in_refs, out_refs):
        if self.scatter:
            return [((lambda pos, r=r: r.at[self.slot(pos)]), o) for r, o in zip(in_refs, out_refs)]
        return [((lambda pos, r=r: r), o) for r, o in zip(in_refs, out_refs)]


def _pcall_riding(body, rider, args, *, name, grid, in_specs, out_specs, out_shape, semantics, scratch_shapes=(),
                  input_output_aliases=None):
    aliases = input_output_aliases or {}
    if rider is None:
        outs = _pcall(body, name=name, grid=grid, in_specs=list(in_specs), out_specs=list(out_specs),
                      out_shape=list(out_shape), scratch_shapes=list(scratch_shapes), input_output_aliases=aliases,
                      compiler_params=_params(*semantics))(*args)
        return list(outs), []
    n_in, n_out, n_scr, n_r = len(in_specs), len(out_specs), len(scratch_shapes), len(rider.arrays)

    def riding(*refs):
        ins, refs = refs[:n_in], refs[n_in:]
        rider_in, refs = refs[:n_r], refs[n_r:]
        outs, refs = refs[:n_out], refs[n_out:]
        rider_out, refs = refs[:n_r], refs[n_r:]
        scratch, sems = refs[:n_scr], refs[n_scr:]
        pairs = rider.pairs(rider_in, rider_out)
        first = functools.reduce(jnp.logical_and, [pl.program_id(a) == 0 for a in range(len(grid))])
        last = functools.reduce(jnp.logical_and, [pl.program_id(a) == g - 1 for a, g in enumerate(grid)])

        @pl.when(first)
        def _():
            _exchange_start(pairs, *sems, _mesh_pos(), rider.peers, rider.slot)

        body(*ins, *outs, *scratch)

        @pl.when(last)
        def _():
            _exchange_wait(pairs, *sems, _mesh_pos(), rider.peers, rider.slot)

    outs = _pcall(riding, name=name, grid=grid, in_specs=list(in_specs) + [_ANY] * n_r,
                  out_specs=list(out_specs) + [_ANY] * n_r, out_shape=list(out_shape) + rider.out_shapes,
                  scratch_shapes=list(scratch_shapes) + _exchange_sems(n_r), input_output_aliases=aliases,
                  compiler_params=_params(*(("arbitrary",) * len(grid))))(*args, *rider.arrays)
    return list(outs[:n_out]), list(outs[n_out:])


def rms_inproj(x2, gain, wg, name, rider=None, qk_gains=None):
    t = x2.shape[0]
    with_qkv = qk_gains is not None
    tm = _row_tile(t, 256 if with_qkv else 512)

    def body(x_ref, g_ref, w_ref, *rest):
        if with_qkv:
            qg_ref, kg_ref, proj_ref, ut_ref, qkv_ref = rest
            head_gain = (qg_ref, kg_ref)
        else:
            proj_ref, ut_ref = rest
        x = x_ref[...]
        u = x * _rms_scale(x) * g_ref[...]
        ut_ref[...] = u.T.astype(BF16)
        u = u.astype(BF16)
        for p in range(N_DEV):
            part, lo = p // 2, (p % 2) * W_COLS
            res = _dot(u, w_ref[p])
            proj_ref[part, :, lo:lo + W_COLS] = res
            if with_qkv and part < 3:
                for h in range(W_COLS // HEAD_DIM):
                    y = res[:, h * HEAD_DIM:(h + 1) * HEAD_DIM]
                    if part < 2:
                        y = y * _rms_scale(y) * head_gain[part][...]
                    qkv_ref[part, :, lo + h * HEAD_DIM:lo + (h + 1) * HEAD_DIM] = y.astype(BF16)

    vec = pl.BlockSpec((1, D_MODEL), lambda i: (0, 0))
    in_specs = [pl.BlockSpec((tm, D_MODEL), lambda i: (i, 0)), vec,
                pl.BlockSpec((N_DEV, D_MODEL, W_COLS), lambda i: (0, 0, 0))]
    out_specs = [pl.BlockSpec((4, tm, D_MODEL), lambda i: (0, i, 0)), pl.BlockSpec((D_MODEL, tm), lambda i: (0, i))]
    out_shape = [jax.ShapeDtypeStruct((4, t, D_MODEL), F32), jax.ShapeDtypeStruct((D_MODEL, t), BF16)]
    args = (x2, gain, wg)
    if with_qkv:
        in_specs += [pl.BlockSpec((1, HEAD_DIM), lambda i: (0, 0))] * 2
        out_specs.append(pl.BlockSpec((3, tm, D_MODEL), lambda i: (0, i, 0)))
        out_shape.append(jax.ShapeDtypeStruct((3, t, D_MODEL), BF16))
        args += tuple(qk_gains)
    return _pcall_riding(body, rider, args, name=name, grid=(t // tm,), in_specs=in_specs, out_specs=out_specs,
                         out_shape=out_shape, semantics=("parallel",))


def gate_outproj(o2, proj, w_out, resid, target, name):
    t = o2.shape[0]
    tm = _row_tile(t, 512)
    with_loss = target is not None

    def body(o_ref, gate_ref, w_ref, r_ref, *rest):
        g = gate_ref[0]
        og = (o_ref[...] * (g * _sigmoid(g))).astype(BF16)
        h = r_ref[...] + _dot(og, w_ref[...])
        if with_loss:
            t_ref, dh_ref, loss_ref = rest
            err = h - t_ref[...]
            dh_ref[...] = err * (1.0 / D_MODEL)
            part = 0.5 * jnp.sum(jnp.mean(err * err, axis=-1, keepdims=True))
            loss_ref[...] = jnp.full(loss_ref.shape, part, F32)
        else:
            (h_ref,) = rest
            h_ref[...] = h

    row = pl.BlockSpec((tm, D_MODEL), lambda i: (i, 0))
    in_specs = [row,
                pl.BlockSpec((1, tm, D_MODEL), lambda i: (3, i, 0)),
                pl.BlockSpec((D_MODEL, D_MODEL), lambda i: (0, 0)),
                row]
    args = [o2, proj, w_out, resid]
    if with_loss:
        in_specs.append(row)
        args.append(target)
        out_specs = [row, pl.BlockSpec((1, 8, 128), lambda i: (i, 0, 0))]
        out_shape = [jax.ShapeDtypeStruct((t, D_MODEL), F32),
                     jax.ShapeDtypeStruct((t // tm, 8, 128), F32)]
    else:
        out_specs = row
        out_shape = jax.ShapeDtypeStruct((t, D_MODEL), F32)
    return _pcall(body, name=name, grid=(t // tm,), in_specs=in_specs, out_specs=out_specs,
                  out_shape=out_shape, compiler_params=_params("parallel"))(*args)


def _head_spec(s, part):
    return pl.BlockSpec((1, 1, s, HEAD_DIM), lambda b, h: (part, b, 0, h))


def _seq_spec(s):
    return pl.BlockSpec((1, s, HEAD_DIM), lambda b, h: (b, 0, h))


_GAIN_SPEC = pl.BlockSpec((1, HEAD_DIM), lambda b, h: (0, 0))
_HEAD_ROW_SPEC = pl.BlockSpec((1, 1, 1, HEAD_DIM), lambda b, h: (b, h, 0, 0))


def _sb_group_spec(s, part):
    return pl.BlockSpec((1, 1, s, SB_GROUP_COLS), lambda b, g: (part, b, 0, g))


def _sb_seq_group_spec(s):
    return pl.BlockSpec((1, s, SB_GROUP_COLS), lambda b, g: (b, 0, g))


_SB_GROUP_ROW_SPEC = pl.BlockSpec((1, SB_HEADS_PER_STEP, 1, HEAD_DIM), lambda b, g: (b, g, 0, 0))


def _sb_chains(m):
    return [(h, m * SB_QBLOCKS_PER_STEP + r) for h in range(SB_HEADS_PER_STEP) for r in range(SB_QBLOCKS_PER_STEP)]


def _sb_logits(qi, kj):
    return _dot(qi, kj, NT) * (HEAD_DIM ** -0.5)


def _sb_scores(z, diag, live, tri_lt):
    soft = jnp.log(1.0 + jnp.exp(-jnp.abs(z)))
    valid = jnp.logical_and(live, jnp.logical_or(jnp.logical_not(diag), tri_lt))
    log_skip = jnp.where(valid, -(jnp.maximum(z, 0.0) + soft), 0.0)
    log_beta = jnp.minimum(z, 0.0) - soft
    return log_skip, log_beta, valid


def _sb_keys_left(chains, watch, state):
    done, carries = state[0], state[1]
    worst = None
    for (_, i), c in zip(chains, carries):
        c = jnp.where(done <= i, c[watch], -jnp.inf)
        worst = c if worst is None else jnp.maximum(worst, c)
    return jnp.logical_and(done <= chains[-1][1],
                           jnp.logical_or(done == 0, jnp.max(worst) > SB_LOG_WEIGHT_FLOOR))


def _sb_key_rows(i, done):
    j = i - done
    return pl.ds(pl.multiple_of(jnp.maximum(j, 0) * ATTN_BLOCK, ATTN_BLOCK), ATTN_BLOCK), j >= 0


def _sb_head(ref, h, rows):
    return ref[0, 0, rows, h * HEAD_DIM:(h + 1) * HEAD_DIM]


def _sb_rows(i, offset, count):
    return pl.ds(pl.multiple_of(jnp.maximum(i, 0) * ATTN_BLOCK + offset, 8), count)


def sb_attn_fwd(qkv4, rider=None):
    _, b, s, _ = qkv4.shape
    blk, top = ATTN_BLOCK, SB_TOP_ROWS
    ngroups = s // blk // SB_QBLOCKS_PER_STEP
    assert ngroups * SB_QBLOCKS_PER_STEP * blk == s

    def body(q_ref, k_ref, v_ref, o_ref):
        row = lax.broadcasted_iota(jnp.int32, (blk, blk), 0)
        col = lax.broadcasted_iota(jnp.int32, (blk, blk), 1)
        tri_lt = col < row
        suffix = (row > col).astype(BF16)

        def step(items):
            where = [_sb_key_rows(i, done) for _, i, done, _, _, _, _ in items]
            zs = [_sb_logits(q, _sb_head(k_ref, h, rows)) for (h, _, _, q, _, _, _), (rows, _) in zip(items, where)]
            scored = [_sb_scores(z, done == 0, live, mask)
                      for z, (_, _, done, _, mask, _, _), (_, live) in zip(zs, items, where)]
            afters = [_dot_exact(log_skip, suffix) for log_skip, _, _ in scored]
            ws = [jnp.where(valid, jnp.exp(log_beta + after + c), 0.0).astype(BF16)
                  for (_, log_beta, valid), after, (_, _, _, _, _, c, _) in zip(scored, afters, items)]
            accs = [acc + _dot(w, _sb_head(v_ref, h, rows))
                    for (h, _, _, _, _, _, acc), (rows, _), w in zip(items, where, ws)]
            cs = [c + jnp.sum(log_skip, axis=1, keepdims=True)
                  for (log_skip, _, _), (_, _, _, _, _, c, _) in zip(scored, items)]
            return cs, accs

        def top_items(chains, done, cs, accs):
            return [(h, i, done, _sb_head(q_ref, h, _sb_rows(i, 0, top)), tri_lt[:top, :], c, acc)
                    for (h, i), c, acc in zip(chains, cs, accs)]

        def finish_tops(chains, state):
            def k_step(state):
                done, cs, accs = state
                cs, accs = step(top_items(chains, done, cs, accs))
                return done + 1, tuple(cs), tuple(accs)

            _, _, accs = lax.while_loop(functools.partial(_sb_keys_left, chains, slice(0, top)), k_step, state)
            for (h, i), acc in zip(chains, accs):
                o_ref[0, _sb_rows(i, 0, top), h * HEAD_DIM:(h + 1) * HEAD_DIM] = acc

        def q_group(m, before):
            chains, chains_before = _sb_chains(m), _sb_chains(m - 1)
            qis = [_sb_head(q_ref, h, _sb_rows(i, 0, blk)) for h, i in chains]
            n = len(chains)

            def items_of(done, cs, accs):
                return [(h, i, done, q, tri_lt, c, acc) for (h, i), q, c, acc in zip(chains, qis, cs, accs)]

            def k_step(state):
                done, cs, accs = state
                cs, accs = step(items_of(done, cs, accs))
                return done + 1, tuple(cs), tuple(accs)

            done_b, cs_b, accs_b = before
            cs0, accs0 = step(items_of(jnp.int32(0), (jnp.zeros((blk, 1), F32),) * n,
                                       (jnp.zeros((blk, HEAD_DIM), F32),) * n)
                              + top_items(chains_before, done_b, cs_b, accs_b))
            done, cs, accs = lax.while_loop(functools.partial(_sb_keys_left, chains, slice(top, blk)), k_step,
                                            (jnp.int32(1), tuple(cs0[:n]), tuple(accs0[:n])))

            @pl.when(m > 0)
            def _():
                finish_tops(chains_before, (done_b + 1, tuple(cs0[n:]), tuple(accs0[n:])))

            for (h, i), acc in zip(chains, accs):
                o_ref[0, _sb_rows(i, top, blk - top), h * HEAD_DIM:(h + 1) * HEAD_DIM] = acc[top:, :]
            return done, tuple(c[:top, :] for c in cs), tuple(acc[:top, :] for acc in accs)

        n = SB_HEADS_PER_STEP * SB_QBLOCKS_PER_STEP
        nothing = (jnp.int32(0), (jnp.zeros((top, 1), F32),) * n, (jnp.zeros((top, HEAD_DIM), F32),) * n)
        last = lax.fori_loop(0, ngroups, q_group, nothing)
        finish_tops(_sb_chains(ngroups - 1), last)

    (o,), extra = _pcall_riding(
        body, rider, (qkv4, qkv4, qkv4),
        name="sb_attn_fwd", grid=(b, N_HEADS // SB_HEADS_PER_STEP),
        in_specs=[_sb_group_spec(s, 0), _sb_group_spec(s, 1), _sb_group_spec(s, 2)],
        out_specs=[_sb_seq_group_spec(s)],
        out_shape=[jax.ShapeDtypeStruct((b, s, D_MODEL), F32)],
        semantics=("parallel", "parallel"))
    return o, extra


def _hg_masks():
    c = HG_CHUNK
    row = lax.broadcasted_iota(jnp.int32, (c, c), 0)
    col = lax.broadcasted_iota(jnp.int32, (c, c), 1)
    incl = (col <= row)
    lower = incl.astype(BF16)
    before_sub = (col < (row // HG_SUB) * HG_SUB).astype(BF16)
    upper = (col >= row).astype(BF16)
    return incl, lower, before_sub, upper


def _hg_lower_bound(lbl_ref):
    l0 = lbl_ref[0, 0]
    l1 = lbl_ref[1, 0]
    d = l1 - l0
    return _sigmoid_pair(d)


def _hg_gates(qp, fp, lb, oml):
    sq = _sigmoid(qp)
    sf, sfn = _sigmoid_pair(fp)
    f = lb + oml * sf
    return dict(qp=qp, sq=sq, q=qp * sq, sf=sf, sfn=sfn, f=f, k=oml * sfn, logf=jnp.log(f))


def _hg_intra(qds, ks, gcs, grs, incl):
    subs = range(HG_CHUNK // HG_SUB)
    qdbs = [qd.astype(BF16) for qd in qds]
    ess = [[jnp.exp(jnp.minimum(gr[sub * HG_SUB:sub * HG_SUB + 1, :] - gc, EXP_CLAMP)) for sub in subs]
           for gc, gr in zip(gcs, grs)]
    ksbs = [[(k * e).astype(BF16) for e in es] for k, es in zip(ks, ess)]
    rows = [[_dot(qdb[sub * HG_SUB:(sub + 1) * HG_SUB, :], ksb[sub], NT) for sub in subs]
            for qdb, ksb in zip(qdbs, ksbs)]
    a_s = [jnp.where(incl, jnp.concatenate(r, axis=0), 0.0) for r in rows]
    return a_s, qdbs, ksbs, ess


def _hg_group_rows(outer, unroll):
    ns = [outer * unroll + u for u in range(unroll)]
    return ns, [pl.ds(pl.multiple_of(n * HG_CHUNK, HG_CHUNK), HG_CHUNK) for n in ns]


def _state_spec(nchunk):
    return pl.BlockSpec((1, 1, nchunk, HEAD_DIM, HEAD_DIM), lambda b, h: (b, h, 0, 0, 0))


def hgrn2_fwd(proj4, lbl4, o_gain, rider=None):
    _, b, s, _ = proj4.shape
    nchunk = s // HG_CHUNK
    c = HG_CHUNK
    unroll = math.gcd(nchunk, HG_UNROLL_FWD)

    def body(q_ref, f_ref, i_ref, lbl_ref, og_ref, o_ref, oraw_ref, st_ref):
        incl, lower, before_sub, _ = _hg_masks()
        lb, oml = _hg_lower_bound(lbl_ref)

        def group(outer, st):
            ns, rows = _hg_group_rows(outer, unroll)
            vs = [_hg_gates(q_ref[0, 0, r, :], f_ref[0, 0, r, :], lb, oml) for r in rows]
            inps = [i_ref[0, 0, r, :].astype(BF16) for r in rows]
            gcs = [_dot_exact(v["logf"], lower, left=True) for v in vs]
            grs = [_dot_exact(v["logf"], before_sub, left=True) for v in vs]
            a_s, _, _, _ = _hg_intra([v["q"] * jnp.exp(gc - gr) for v, gc, gr in zip(vs, gcs, grs)],
                                     [v["k"] for v in vs], gcs, grs, incl)
            gls = [gc[c - 1:c, :] for gc in gcs]
            adds = [_dot(inp, (v["k"] * jnp.exp(gl - gc)).astype(BF16), TN)
                    for inp, v, gl, gc in zip(inps, vs, gls, gcs)]
            o_intra = [_dot(a.astype(BF16), inp) for a, inp in zip(a_s, inps)]
            sts = []
            for gl, add in zip(gls, adds):
                sts.append(st)
                st = st * jnp.exp(gl) + add
            outs = [oi + _dot((v["q"] * jnp.exp(gc)).astype(BF16), s0.astype(BF16), NT)
                    for oi, v, gc, s0 in zip(o_intra, vs, gcs, sts)]
            for n, r, s0, o in zip(ns, rows, sts, outs):
                st_ref[0, 0, n] = s0
                oraw_ref[0, r, :] = o
                o_ref[0, r, :] = o * _rms_scale(o) * og_ref[...]
            return st

        lax.fori_loop(0, nchunk // unroll, group, jnp.zeros((HEAD_DIM, HEAD_DIM), F32))

    seq = jax.ShapeDtypeStruct((b, s, D_MODEL), F32)
    return _pcall_riding(
        body, rider, (proj4, proj4, proj4, lbl4, o_gain), name="hgrn2_fwd", grid=(b, N_HEADS),
        in_specs=[_head_spec(s, 0), _head_spec(s, 1), _head_spec(s, 2),
                  pl.BlockSpec((2, 1, 1, HEAD_DIM), lambda b, h: (0, h, 0, 0)), _GAIN_SPEC],
        out_specs=[_seq_spec(s), _seq_spec(s), _state_spec(nchunk)],
        out_shape=[seq, seq, jax.ShapeDtypeStruct((b, N_HEADS, nchunk, HEAD_DIM, HEAD_DIM), F32)],
        semantics=("parallel", "parallel"))


def outproj_bwd(dh, w_out, o2, proj, name):
    t = dh.shape[0]
    tm = _row_tile(t, 512)

    def body(dh_ref, w_ref, o_ref, gate_ref, do_ref, dproj_ref, dw_ref):
        dhb = dh_ref[...].astype(BF16)
        dog = _dot(dhb, w_ref[...], NT)
        g = gate_ref[0]
        sg = _sigmoid(g)
        silu = g * sg
        o = o_ref[...]
        do_ref[...] = dog * silu
        dproj_ref[0] = dog * o * (sg * (1.0 + g * (1.0 - sg)))
        part = _dot((o * silu).astype(BF16), dhb, TN)

        @pl.when(pl.program_id(0) == 0)
        def _():
            dw_ref[...] = part

        @pl.when(pl.program_id(0) > 0)
        def _():
            dw_ref[...] += part

    row = pl.BlockSpec((tm, D_MODEL), lambda i: (i, 0))
    full = pl.BlockSpec((D_MODEL, D_MODEL), lambda i: (0, 0))
    return _pcall(
        body, name=name, grid=(t // tm,),
        in_specs=[row, full, row, pl.BlockSpec((1, tm, D_MODEL), lambda i: (3, i, 0))],
        out_specs=[row, pl.BlockSpec((1, tm, D_MODEL), lambda i: (3, i, 0)), full],
        out_shape=[jax.ShapeDtypeStruct((t, D_MODEL), F32),
                   jax.ShapeDtypeStruct((4, t, D_MODEL), F32),
                   jax.ShapeDtypeStruct((D_MODEL, D_MODEL), F32)],
        compiler_params=_params("arbitrary"),
    )(dh, w_out, o2, proj)


def inproj_bwd_dx(dproj, wg, x2, gain, dres, name, rider=None):
    t = x2.shape[0]
    tm = _row_tile(t, 512)

    def body(d_ref, w_ref, x_ref, g_ref, r_ref, dx_ref, dg_ref):
        du = jnp.zeros((tm, D_MODEL), F32)
        for p in range(N_DEV):
            cols = slice((p % 2) * W_COLS, (p % 2 + 1) * W_COLS)
            du = du + _dot(d_ref[p // 2, :, cols].astype(BF16), w_ref[p], NT)
        x = x_ref[...]
        r = _rms_scale(x)
        xh = x * r
        a = du * g_ref[...]
        dx_ref[...] = r_ref[...] + r * (a - xh * jnp.mean(a * xh, axis=-1, keepdims=True))
        part = jnp.sum(du * xh, axis=0, keepdims=True)

        @pl.when(pl.program_id(0) == 0)
        def _():
            dg_ref[...] = part

        @pl.when(pl.program_id(0) > 0)
        def _():
            dg_ref[...] += part

    row = pl.BlockSpec((tm, D_MODEL), lambda i: (i, 0))
    vec = pl.BlockSpec((1, D_MODEL), lambda i: (0, 0))
    return _pcall_riding(
        body, rider, (dproj, wg, x2, gain, dres), name=name, grid=(t // tm,),
        in_specs=[pl.BlockSpec((4, tm, D_MODEL), lambda i: (0, i, 0)),
                  pl.BlockSpec((N_DEV, D_MODEL, W_COLS), lambda i: (0, 0, 0)),
                  row, vec, row],
        out_specs=[row, vec],
        out_shape=[jax.ShapeDtypeStruct((t, D_MODEL), F32), jax.ShapeDtypeStruct((1, D_MODEL), F32)],
        semantics=("arbitrary",))


def inproj_bwd_dw(ut, dproj, name, out_dtype):
    t = ut.shape[1]

    def body(ut_ref, d_ref, dw_ref):
        dw_ref[0] = _dot(ut_ref[...], d_ref[0].astype(BF16)).astype(dw_ref.dtype)

    return _pcall(
        body, name=name, grid=(N_DEV,),
        in_specs=[pl.BlockSpec((D_MODEL, t), lambda j: (0, 0)),
                  pl.BlockSpec((1, t, W_COLS), lambda j: (j // 2, 0, j % 2))],
        out_specs=pl.BlockSpec((1, D_MODEL, W_COLS), lambda j: (j, 0, 0)),
        out_shape=jax.ShapeDtypeStruct((N_DEV, D_MODEL, W_COLS), out_dtype),
        compiler_params=_params("parallel"),
    )(ut, dproj)


def _rms_bwd(x, gain, dy):
    r = _rms_scale(x)
    xh = x * r
    a = dy * gain
    return r * (a - xh * jnp.mean(a * xh, axis=-1, keepdims=True)), dy * xh


def sb_attn_bwd(qkv4, proj4, do3, o3, q_gain, k_gain, dproj4, rider=None):
    _, b, s, _ = proj4.shape
    blk, top = ATTN_BLOCK, SB_TOP_ROWS
    nq = s // blk
    ngroups = nq // SB_QBLOCKS_PER_STEP
    assert ngroups * SB_QBLOCKS_PER_STEP * blk == s
    scale = HEAD_DIM ** -0.5

    def body(qn_ref, kn_ref, v_ref, q_ref, k_ref, do_ref, o_ref, qg_ref, kg_ref, _alias, d_ref, dqg_ref, dkg_ref, dob):
        for h in range(SB_HEADS_PER_STEP):
            dob[h] = do_ref[0, :, h * HEAD_DIM:(h + 1) * HEAD_DIM].astype(BF16)
        d_ref[...] = jnp.zeros_like(d_ref)
        row = lax.broadcasted_iota(jnp.int32, (blk, blk), 0)
        col = lax.broadcasted_iota(jnp.int32, (blk, blk), 1)
        tri_lt = col < row
        suffix = (row > col).astype(BF16)
        suffix_incl = (row >= col).astype(BF16)

        def step(items):
            heads = [it[0] for it in items]
            where = [_sb_key_rows(it[1], it[2]) for it in items]
            kjs = [_sb_head(kn_ref, h, rows) for h, (rows, _) in zip(heads, where)]
            zs = [_sb_logits(it[3], kj) for it, kj in zip(items, kjs)]
            dws = [_dot(it[4], _sb_head(v_ref, h, rows), NT) for it, h, (rows, _) in zip(items, heads, where)]
            scored = [_sb_scores(z, it[2] == 0, live, it[6]) for z, it, (_, live) in zip(zs, items, where)]
            afters = [_dot_exact(log_skip, suffix) for log_skip, _, _ in scored]
            wbs = [jnp.where(valid, jnp.exp(log_beta + after + it[7]), 0.0).astype(BF16)
                   for (_, log_beta, valid), after, it in zip(scored, afters, items)]
            gs = [dw * wb.astype(F32) for dw, wb in zip(dws, wbs)]
            befores = [it[5] - (_dot_exact(g, suffix_incl) + it[8]) for g, it in zip(gs, items)]
            dzbs = [jnp.where(valid, g - jnp.exp(log_beta) * (g + before), 0.0).astype(BF16)
                    for (_, log_beta, valid), g, before in zip(scored, gs, befores)]
            dqs = [it[9] + _dot(dzb, kj) for it, dzb, kj in zip(items, dzbs, kjs)]
            for it, h, (rows, _), wb, dzb in zip(items, heads, where, wbs, dzbs):
                cols = slice(h * HEAD_DIM, (h + 1) * HEAD_DIM)
                d_ref[2, 0, rows, cols] += _dot(wb, it[4], TN)
                d_ref[1, 0, rows, cols] += _dot(dzb, it[3], TN)
            cs = [it[7] + jnp.sum(log_skip, axis=1, keepdims=True) for (log_skip, _, _), it in zip(scored, items)]
            cgs = [it[8] + jnp.sum(g, axis=1, keepdims=True) for g, it in zip(gs, items)]
            return cs, cgs, dqs

        def top_items(chains, deltas, done, cs, cgs, dqs):
            return [(h, i, done, _sb_head(qn_ref, h, _sb_rows(i, 0, top)), dob[h, _sb_rows(i, 0, top), :], delta,
                     tri_lt[:top, :], c, cg, dq)
                    for (h, i), delta, c, cg, dq in zip(chains, deltas, cs, cgs, dqs)]

        def finish_tops(chains, deltas, state):
            def k_step(state):
                done, cs, cgs, dqs = state
                cs, cgs, dqs = step(top_items(chains, deltas, done, cs, cgs, dqs))
                return done + 1, tuple(cs), tuple(cgs), tuple(dqs)

            _, _, _, dqs = lax.while_loop(functools.partial(_sb_keys_left, chains, slice(0, top)), k_step, state)
            for (h, i), dq in zip(chains, dqs):
                d_ref[0, 0, _sb_rows(i, 0, top), h * HEAD_DIM:(h + 1) * HEAD_DIM] = dq * scale

        def q_group(m, before):
            chains, chains_before = _sb_chains(m), _sb_chains(m - 1)
            deltas_b, before = before[0], before[1:]
            qis, dois, deltas = [], [], []
            for h, i in chains:
                rows_i = _sb_rows(i, 0, blk)
                qis.append(_sb_head(qn_ref, h, rows_i))
                dois.append(dob[h, rows_i, :])
                deltas.append(jnp.sum(dois[-1].astype(F32) * o_ref[0, rows_i, h * HEAD_DIM:(h + 1) * HEAD_DIM],
                                      axis=1, keepdims=True))
            n = len(chains)

            def items_of(done, cs, cgs, dqs):
                return [(h, i, done, q, do, delta, tri_lt, c, cg, dq)
                        for (h, i), q, do, delta, c, cg, dq in zip(chains, qis, dois, deltas, cs, cgs, dqs)]

            def k_step(state):
                done, cs, cgs, dqs = state
                cs, cgs, dqs = step(items_of(done, cs, cgs, dqs))
                return done + 1, tuple(cs), tuple(cgs), tuple(dqs)

            done_b, cs_b, cgs_b, dqs_b = before
            zero = (jnp.zeros((blk, 1), F32),) * n
            new = step(items_of(jnp.int32(0), zero, zero, (jnp.zeros((blk, HEAD_DIM), F32),) * n)
                       + top_items(chains_before, deltas_b, done_b, cs_b, cgs_b, dqs_b))
            done, cs, cgs, dqs = lax.while_loop(functools.partial(_sb_keys_left, chains, slice(top, blk)), k_step,
                                                (jnp.int32(1),) + tuple(tuple(x[:n]) for x in new))

            @pl.when(m > 0)
            def _():
                finish_tops(chains_before, deltas_b, (done_b + 1,) + tuple(tuple(x[n:]) for x in new))

            for (h, i), dq in zip(chains, dqs):
                d_ref[0, 0, _sb_rows(i, top, blk - top), h * HEAD_DIM:(h + 1) * HEAD_DIM] = dq[top:, :] * scale
            first = lambda xs: tuple(x[:top, :] for x in xs)
            return first(deltas), done, first(cs), first(cgs), first(dqs)

        n = SB_HEADS_PER_STEP * SB_QBLOCKS_PER_STEP
        zero = (jnp.zeros((top, 1), F32),) * n
        nothing = (zero, jnp.int32(0), zero, zero, (jnp.zeros((top, HEAD_DIM), F32),) * n)
        last = lax.fori_loop(0, ngroups, q_group, nothing)
        finish_tops(_sb_chains(ngroups - 1), last[0], last[1:])

        def norm_block(i, carry):
            rows = pl.ds(pl.multiple_of(i * blk, blk), blk)
            out = []
            for h in range(SB_HEADS_PER_STEP):
                cols = slice(h * HEAD_DIM, (h + 1) * HEAD_DIM)
                for part, src_ref, gain_ref in ((0, q_ref, qg_ref), (1, k_ref, kg_ref)):
                    dy = d_ref[part, 0, rows, cols] * (scale if part == 1 else 1.0)
                    dx, pg = _rms_bwd(src_ref[0, 0, rows, cols], gain_ref[...], dy)
                    d_ref[part, 0, rows, cols] = dx
                    out.append(carry[len(out)] + jnp.sum(pg, axis=0, keepdims=True))
            return tuple(out)

        sums = lax.fori_loop(0, nq, norm_block, (jnp.zeros((1, HEAD_DIM), F32),) * (2 * SB_HEADS_PER_STEP))
        for h in range(SB_HEADS_PER_STEP):
            dqg_ref[0, h] = sums[2 * h]
            dkg_ref[0, h] = sums[2 * h + 1]

    head_row = jax.ShapeDtypeStruct((b, N_HEADS, 1, HEAD_DIM), F32)
    return _pcall_riding(
        body, rider, (qkv4, qkv4, qkv4, proj4, proj4, do3, o3, q_gain, k_gain, dproj4),
        name="sb_attn_bwd", grid=(b, N_HEADS // SB_HEADS_PER_STEP),
        in_specs=[_sb_group_spec(s, 0), _sb_group_spec(s, 1), _sb_group_spec(s, 2),
                  _sb_group_spec(s, 0), _sb_group_spec(s, 1),
                  _sb_seq_group_spec(s), _sb_seq_group_spec(s), _GAIN_SPEC, _GAIN_SPEC,
                  pl.BlockSpec(memory_space=pl.ANY)],
        out_specs=[pl.BlockSpec((3, 1, s, SB_GROUP_COLS), lambda b, g: (0, b, 0, g)),
                   _SB_GROUP_ROW_SPEC, _SB_GROUP_ROW_SPEC],
        out_shape=[jax.ShapeDtypeStruct(dproj4.shape, F32), head_row, head_row],
        scratch_shapes=[pltpu.VMEM((SB_HEADS_PER_STEP, s, HEAD_DIM), BF16)],
        input_output_aliases={9: 0}, semantics=("parallel", "parallel"))


def hgrn2_bwd(proj4, don3, oraw3, states, lbl4, o_gain, dproj4, rider=None):
    _, b, s, _ = proj4.shape
    nchunk = s // HG_CHUNK
    c = HG_CHUNK
    subs = range(HG_CHUNK // HG_SUB)
    unroll = math.gcd(nchunk, HG_UNROLL_BWD)
    ngroup = nchunk // unroll

    def body(q_ref, f_ref, i_ref, don_ref, oraw_ref, st_ref, lbl_ref, og_ref, _alias, d_ref, dog_ref, dlb_ref):
        incl, lower, before_sub, upper = _hg_masks()
        lb, oml = _hg_lower_bound(lbl_ref)
        last_row = lax.broadcasted_iota(jnp.int32, (c, HEAD_DIM), 0) == c - 1

        def group(m, carry):
            dst, dog_acc, dlb_acc = carry
            ns, rows = _hg_group_rows(ngroup - 1 - m, unroll)
            ns, rows = ns[::-1], rows[::-1]
            vs = [_hg_gates(q_ref[0, 0, r, :], f_ref[0, 0, r, :], lb, oml) for r in rows]
            inps = [i_ref[0, 0, r, :].astype(BF16) for r in rows]
            sts = [st_ref[0, 0, n] for n in ns]
            gcs = [_dot_exact(v["logf"], lower, left=True) for v in vs]
            grs = [_dot_exact(v["logf"], before_sub, left=True) for v in vs]
            e_qs = [jnp.exp(gc - gr) for gc, gr in zip(gcs, grs)]
            a_s, qdbs, ksbs, ess = _hg_intra([v["q"] * e for v, e in zip(vs, e_qs)], [v["k"] for v in vs],
                                             gcs, grs, incl)
            e_gcs = [jnp.exp(gc) for gc in gcs]
            gls = [gc[c - 1:c, :] for gc in gcs]
            e_gls = [jnp.exp(gl) for gl in gls]
            e_ks = [jnp.exp(gl - gc) for gl, gc in zip(gls, gcs)]
            normed = [_rms_bwd(oraw_ref[0, r, :], og_ref[...], don_ref[0, r, :]) for r in rows]
            dobs = [do.astype(BF16) for do, _ in normed]
            dabs = [jnp.where(incl, _dot(dob, inp, NT), 0.0).astype(BF16) for dob, inp in zip(dobs, inps)]
            adds = [_dot(dob, (v["q"] * e).astype(BF16), TN) for dob, v, e in zip(dobs, vs, e_gcs)]
            dq_inters = [_dot(dob, st.astype(BF16)) * e for dob, st, e in zip(dobs, sts, e_gcs)]
            dqds = [jnp.concatenate([_dot(dab[sub * HG_SUB:(sub + 1) * HG_SUB, :], ksb[sub]) for sub in subs], axis=0)
                    for dab, ksb in zip(dabs, ksbs)]
            dkss = [[_dot(dab[sub * HG_SUB:(sub + 1) * HG_SUB, :], qdb[sub * HG_SUB:(sub + 1) * HG_SUB, :], TN)
                     for sub in subs] for dab, qdb in zip(dabs, qdbs)]
            dsts = []
            for e_gl, add in zip(e_gls, adds):
                dsts.append(dst)
                dst = dst * e_gl + add
            dstbs = [d.astype(BF16) for d in dsts]
            dis = [_dot(a.astype(BF16), dob, TN) + _dot((v["k"] * e_k).astype(BF16), dstb, NT)
                   for a, dob, v, e_k, dstb in zip(a_s, dobs, vs, e_ks, dstbs)]
            dk_inters = [_dot(inp, dstb) * e_k for inp, dstb, e_k in zip(inps, dstbs, e_ks)]
            dks, dgcs = [], []
            for u in range(len(rows)):
                q, k = vs[u]["q"], vs[u]["k"]
                dk, dgc_k = dk_inters[u], jnp.zeros((c, HEAD_DIM), F32)
                for sub in subs:
                    dk = dk + dkss[u][sub] * ess[u][sub]
                    dgc_k = dgc_k + dkss[u][sub] * ksbs[u][sub].astype(F32)
                at_last = (jnp.sum(k * dk_inters[u], axis=0, keepdims=True)
                           + e_gls[u] * jnp.sum(sts[u] * dsts[u], axis=0, keepdims=True))
                dks.append(dk)
                dgcs.append((qdbs[u].astype(F32) * dqds[u] - dgc_k) + (q * dq_inters[u] - k * dk_inters[u])
                            + jnp.where(last_row, at_last, 0.0))
            dlf_fs = [_dot_exact(dgc, upper, left=True) / v["f"] for dgc, v in zip(dgcs, vs)]
            for u, r in enumerate(rows):
                v = vs[u]
                dq = dqds[u] * e_qs[u] + dq_inters[u]
                d_ref[0, 0, r, :] = dq * (v["sq"] * (1.0 + v["qp"] * (1.0 - v["sq"])))
                d_ref[1, 0, r, :] = (dlf_fs[u] - dks[u]) * (oml * v["sf"] * v["sfn"])
                d_ref[2, 0, r, :] = dis[u]
                dlb_acc = dlb_acc + jnp.sum((dlf_fs[u] - dks[u]) * v["sfn"], axis=0, keepdims=True)
                dog_acc = dog_acc + jnp.sum(normed[u][1], axis=0, keepdims=True)
            return dst, dog_acc, dlb_acc

        zero = jnp.zeros((1, HEAD_DIM), F32)
        _, dog, dlb = lax.fori_loop(0, ngroup, group, (jnp.zeros((HEAD_DIM, HEAD_DIM), F32), zero, zero))
        dog_ref[0, 0] = dog
        dlb_ref[0, 0] = dlb

    head_row = jax.ShapeDtypeStruct((b, N_HEADS, 1, HEAD_DIM), F32)
    return _pcall_riding(
        body, rider, (proj4, proj4, proj4, don3, oraw3, states, lbl4, o_gain, dproj4),
        name="hgrn2_bwd", grid=(b, N_HEADS),
        in_specs=[_head_spec(s, 0), _head_spec(s, 1), _head_spec(s, 2), _seq_spec(s), _seq_spec(s),
                  _state_spec(nchunk), pl.BlockSpec((2, 1, 1, HEAD_DIM), lambda b, h: (0, h, 0, 0)), _GAIN_SPEC,
                  pl.BlockSpec(memory_space=pl.ANY)],
        out_specs=[pl.BlockSpec((3, 1, s, HEAD_DIM), lambda b, h: (0, b, 0, h)), _HEAD_ROW_SPEC, _HEAD_ROW_SPEC],
        out_shape=[jax.ShapeDtypeStruct(dproj4.shape, F32), head_row, head_row],
        input_output_aliases={8: 0}, semantics=("parallel", "parallel"))


def local_step(x, target, sb_norm, wsi, sb_q_gain, sb_k_gain, hg_o_gain, hg_lb_logits, wso_mine, whi_mine, who_mine,
               hg_norm_mine):
    b, s, _ = x.shape
    t = b * s
    x2 = x.reshape(t, D_MODEL)
    tg2 = target.reshape(t, D_MODEL)
    lbl4 = hg_lb_logits.reshape(2, N_HEADS, 1, HEAD_DIM)
    four = (4, b, s, D_MODEL)
    three = (b, s, D_MODEL)
    rows8 = (N_DEV, W_ROWS, D_MODEL)

    (proj0, u0, qkv0), (wso, hgn) = rms_inproj(x2, sb_norm, wsi, "sb_inproj",
                                               _Rider([wso_mine, hg_norm_mine], scatter=False),
                                               qk_gains=(sb_q_gain, sb_k_gain))
    qkv0 = qkv0.reshape(3, b, s, D_MODEL)
    wso = wso.reshape(D_MODEL, D_MODEL)
    hg_norm_full = hgn[:, 0, :].reshape(1, D_MODEL)
    o0, (whi,) = sb_attn_fwd(qkv0, _Rider([whi_mine], scatter=False))
    o0 = o0.reshape(t, D_MODEL)
    h1 = gate_outproj(o0, proj0, wso, x2, None, "sb_outproj")
    (proj1, u1), _ = rms_inproj(h1, hg_norm_full, whi, "hg_inproj")
    (o1, o1_raw, states), (who,) = hgrn2_fwd(proj1.reshape(four), lbl4, hg_o_gain,
                                             _Rider([who_mine], scatter=False))
    who = who.reshape(D_MODEL, D_MODEL)
    o1 = o1.reshape(t, D_MODEL)
    dh2, loss_parts = gate_outproj(o1, proj1, who, h1, tg2, "hg_outproj_loss")

    do1, dproj1, g_who = outproj_bwd(dh2, who, o1, proj1, "hg_outproj_bwd")
    (dproj1, g_og, g_lb), (p_who,) = hgrn2_bwd(proj1.reshape(four), do1.reshape(three), o1_raw, states, lbl4,
                                               hg_o_gain, dproj1.reshape(four),
                                               _Rider([g_who.reshape(rows8)], scatter=True))
    dproj1 = dproj1.reshape(4, t, D_MODEL)
    (dh1, g_hgn), _ = inproj_bwd_dx(dproj1, whi, h1, hg_norm_full, dh2, "hg_inproj_bwd_dx")
    g_whi = inproj_bwd_dw(u1, dproj1, "hg_inproj_bwd_dw", out_dtype=BF16)

    do0, dproj0, g_wso = outproj_bwd(dh1, wso, o0, proj0, "sb_outproj_bwd")
    (dproj0, g_qg, g_kg), (p_whi, p_wso) = sb_attn_bwd(qkv0, proj0.reshape(four), do0.reshape(three), o0.reshape(three),
                                                       sb_q_gain, sb_k_gain, dproj0.reshape(four),
                                                       _Rider([g_whi, g_wso.reshape(rows8)], scatter=True))
    dproj0 = dproj0.reshape(4, t, D_MODEL)
    g_wsi = inproj_bwd_dw(u0, dproj0, "sb_inproj_bwd_dw", out_dtype=BF16)
    (gx, g_sbn), (p_wsi,) = inproj_bwd_dx(dproj0, wsi, x2, sb_norm, dh1, "sb_inproj_bwd_dx",
                                          _Rider([sum_within_chip(g_wsi)], scatter=True, chips=True))
    return dict(loss_parts=loss_parts, gx=gx.reshape(three), p_wsi=p_wsi, p_wso=p_wso, p_whi=p_whi, p_who=p_who,
                g_sbn=g_sbn, g_hgn=g_hgn, g_qg=g_qg, g_kg=g_kg, g_og=g_og, g_lb=g_lb)


def sum_within_chip(g):
    _, r, c_ = g.shape
    chips = N_DEV // 2

    def swap(g_ref, got_ref, send_sems, recv_sems):
        x, y, c = _mesh_pos()
        copies = [pltpu.make_async_remote_copy(
            src_ref=g_ref.at[2 * q + 1 - c], dst_ref=got_ref.at[q], send_sem=send_sems.at[q], recv_sem=recv_sems.at[q],
            device_id=(x, y, 1 - c), device_id_type=MESH) for q in range(chips)]
        for cp in copies:
            cp.start()
        for cp in copies:
            cp.wait_recv()
            cp.wait_send()

    got = _pcall(swap, name="swap_with_sibling", in_specs=[_ANY], out_specs=_ANY,
                 out_shape=jax.ShapeDtypeStruct((chips, r, c_), g.dtype),
                 scratch_shapes=[pltpu.SemaphoreType.DMA((chips,)), pltpu.SemaphoreType.DMA((chips,))])(g)

    def add(g_ref, got_ref, out_ref):
        mine = g_ref[0, lax.axis_index("c")]
        out_ref[0] = (mine.astype(F32) + got_ref[0].astype(F32)).astype(out_ref.dtype)

    return _pcall(
        add, name="add_sibling_partials", grid=(chips,),
        in_specs=[pl.BlockSpec((1, 2, r, c_), lambda q: (q, 0, 0, 0)), pl.BlockSpec((1, r, c_), lambda q: (q, 0, 0))],
        out_specs=pl.BlockSpec((1, r, c_), lambda q: (q, 0, 0)),
        out_shape=jax.ShapeDtypeStruct((chips, r, c_), g.dtype),
        compiler_params=_params("parallel"),
    )(g.reshape(chips, 2, r, c_), got)


def _two_level_gather(src, out, send_sems, recv_sems, local_sem, pos, meanwhile):
    x, y, c = pos
    me, sibling = (x, y, c), (x, y, 1 - c)
    chips = [(1 - x, y), (x, 1 - y), (1 - x, 1 - y)]

    def copy(k, block, to, source=None):
        slot = out.at[_linear(block)]
        return pltpu.make_async_remote_copy(
            src_ref=slot if source is None else source, dst_ref=slot, send_sem=send_sems.at[k],
            recv_sem=recv_sems.at[k], device_id=to, device_id_type=MESH)

    mine = pltpu.make_async_copy(src, out.at[_linear(me)], local_sem)
    mine.start()
    first = [copy(0, me, sibling, src)] + [copy(1 + j, me, (*chip, c), src) for j, chip in enumerate(chips)]
    for cp in first:
        cp.start()
    meanwhile()
    passed = [copy(4 + j, (*chip, c), sibling) for j, chip in enumerate(chips)]
    for j, chip in enumerate(chips):
        copy(1 + j, (*chip, c), me).wait_recv()
        passed[j].start()
    copy(0, sibling, me).wait_recv()
    for j, chip in enumerate(chips):
        copy(4 + j, (*chip, 1 - c), me).wait_recv()
    for cp in first + passed:
        cp.wait_send()
    mine.wait()


def gather_first_weights(w_si, w_so, w_hi, w_ho, hg_norm):
    def body(si_ref, so_ref, hi_ref, ho_ref, hn_ref, o_si, so_b, hi_b, ho_b, hn_b, si_b, send_sems, recv_sems, local_sem):
        si_b[...] = si_ref[...].astype(BF16)

        def cast_the_rest():
            for src, buf in ((so_ref, so_b), (hi_ref, hi_b), (ho_ref, ho_b)):
                buf[...] = src[...].astype(BF16)
            hn_b[...] = jnp.broadcast_to(hn_ref[...], hn_b.shape)

        _two_level_gather(si_b, o_si, send_sems, recv_sems, local_sem, _mesh_pos(), cast_the_rest)

    return _pcall(
        body, name="gather_first_weights",
        in_specs=[_VMEM] * 5, out_specs=[_ANY] + [_VMEM] * 4,
        out_shape=[jax.ShapeDtypeStruct((N_DEV,) + w_si.shape, BF16), jax.ShapeDtypeStruct(w_so.shape, BF16),
                   jax.ShapeDtypeStruct(w_hi.shape, BF16), jax.ShapeDtypeStruct(w_ho.shape, BF16),
                   jax.ShapeDtypeStruct((8, HEAD_DIM), F32)],
        scratch_shapes=[pltpu.VMEM(w_si.shape, BF16), pltpu.SemaphoreType.DMA((N_PEERS,)),
                        pltpu.SemaphoreType.DMA((N_PEERS,)), pltpu.SemaphoreType.DMA],
        compiler_params=pltpu.CompilerParams(vmem_limit_bytes=VMEM_LIMIT_BYTES),
    )(w_si, w_so, w_hi, w_ho, hg_norm)


def _adamw(w, g, m, v):
    m = ADAM_B1 * m + (1.0 - ADAM_B1) * g
    v = ADAM_B2 * v + (1.0 - ADAM_B2) * (g * g)
    m_hat = m / (1.0 - ADAM_B1 ** ADAM_STEP)
    v_hat = v / (1.0 - ADAM_B2 ** ADAM_STEP)
    delta = -ADAM_LR * (m_hat / (jnp.sqrt(v_hat) + ADAM_EPS) + ADAM_WD * w)
    return delta, m, v


def reduce_adamw(parts, w, m, v, name):
    n, r, c = parts.shape
    tr = _row_tile(r, 256)

    def body(p_ref, w_ref, m_ref, v_ref, g_ref, d_ref, m2_ref, v2_ref):
        g = p_ref[0].astype(F32)
        for sender in range(1, n):
            g = g + p_ref[sender].astype(F32)
        g_ref[...] = g
        d_ref[...], m2_ref[...], v2_ref[...] = _adamw(w_ref[...], g, m_ref[...], v_ref[...])

    tile = pl.BlockSpec((tr, c), lambda i: (i, 0))
    return _pcall(
        body, name=name, grid=(r // tr,),
        in_specs=[pl.BlockSpec((n, tr, c), lambda i: (0, i, 0)), tile, tile, tile],
        out_specs=[tile] * 4, out_shape=[jax.ShapeDtypeStruct((r, c), F32)] * 4,
        compiler_params=_params("parallel"),
    )(parts, w, m, v)


PACK_ROWS = 32
ROW_SBN, ROW_HGN, ROW_LB, ROW_QG, ROW_KG, ROW_OG, ROW_LOSS = 0, 8, 16, 24, 25, 26, 27


def small_update(g_sbn, g_hgn, g_lb, g_qg, g_kg, g_og, loss_parts, small):
    n_in = 7 + len(small)

    def body(*refs):
        sbn_ref, hgn_ref, lb_ref, qg_ref, kg_ref, og_ref, loss_ref = refs[:7]
        wmv = refs[7:n_in]
        outs = refs[n_in:n_in + 25]
        pack, gath, tot, send_sems, recv_sems, local_sems = refs[n_in + 25:]
        pos = _mesh_pos()
        me = _linear(pos)
        pack[...] = jnp.zeros_like(pack)
        pack[ROW_SBN:ROW_SBN + 8, :] = sbn_ref[...]
        pack[ROW_HGN:ROW_HGN + 8, :] = hgn_ref[...]
        pack[ROW_LB:ROW_LB + 8, :] = jnp.sum(lb_ref[...], axis=0)
        pack[ROW_QG:ROW_QG + 1, :] = jnp.sum(qg_ref[...], axis=0, keepdims=True)
        pack[ROW_KG:ROW_KG + 1, :] = jnp.sum(kg_ref[...], axis=0, keepdims=True)
        pack[ROW_OG:ROW_OG + 1, :] = jnp.sum(og_ref[...], axis=0, keepdims=True)
        pack[ROW_LOSS:ROW_LOSS + 1, :] = jnp.sum(loss_ref[...], axis=0)[0:1, :]
        _exchange_start([((lambda p: pack), gath)], send_sems, recv_sems, local_sems, pos)
        _exchange_wait([((lambda p: pack), gath)], send_sems, recv_sems, local_sems, pos)
        total = gath[0]
        for dev in range(1, N_DEV):
            total = total + gath[dev]
        tot[...] = total
        outs[0][...] = jnp.broadcast_to(tot[ROW_LOSS:ROW_LOSS + 1, :], (8, HEAD_DIM))
        l0 = wmv[15][0:8, :]
        l1 = wmv[15][8:16, :]
        p1, p0 = _sigmoid_pair(l1 - l0)
        d_l1 = p0 * p1 * tot[ROW_LB:ROW_LB + 8, :]
        grads = [tot[ROW_SBN:ROW_SBN + 8, :], tot[ROW_QG:ROW_QG + 1, :], tot[ROW_KG:ROW_KG + 1, :],
                 tot[pl.ds(ROW_HGN + me, 1), :], tot[ROW_OG:ROW_OG + 1, :],
                 jnp.concatenate([-d_l1, d_l1], axis=0)]
        for i, g in enumerate(grads):
            w_ref, m_ref, v_ref = wmv[3 * i:3 * i + 3]
            o = outs[1 + 4 * i:5 + 4 * i]
            o[0][...] = g
            o[1][...], o[2][...], o[3][...] = _adamw(w_ref[...], g, m_ref[...], v_ref[...])

    out_shape = [jax.ShapeDtypeStruct((8, HEAD_DIM), F32)]
    for i in range(6):
        out_shape += [jax.ShapeDtypeStruct(small[3 * i].shape, F32)] * 4
    return _pcall(
        body, name="small_update",
        in_specs=[_VMEM] * n_in, out_specs=[_VMEM] * 25, out_shape=out_shape,
        scratch_shapes=[pltpu.VMEM((PACK_ROWS, HEAD_DIM), F32), pltpu.VMEM((N_DEV, PACK_ROWS, HEAD_DIM), F32),
                        pltpu.VMEM((PACK_ROWS, HEAD_DIM), F32)] + _exchange_sems(1),
    )(g_sbn, g_hgn, g_lb, g_qg, g_kg, g_og, loss_parts, *small)


def kernel(x, sb_norm, sb_w_in, sb_q_gain, sb_k_gain, sb_w_out, hg_norm, hg_w_in, hg_o_gain, hg_w_out, hg_lb_logits, loss_target, m_sb_norm, m_sb_w_in, m_sb_q_gain, m_sb_k_gain, m_sb_w_out, m_hg_norm, m_hg_w_in, m_hg_o_gain, m_hg_w_out, m_hg_lb_logits, v_sb_norm, v_sb_w_in, v_sb_q_gain, v_sb_k_gain, v_sb_w_out, v_hg_norm, v_hg_w_in, v_hg_o_gain, v_hg_w_out, v_hg_lb_logits):
    b = x.shape[0]
    wsi, wso_mine, whi_mine, who_mine, hg_norm_mine = gather_first_weights(
        sb_w_in[0], sb_w_out[0], hg_w_in[0], hg_w_out[0], hg_norm)
    r = local_step(x, loss_target, sb_norm, wsi, sb_q_gain, sb_k_gain, hg_o_gain, hg_lb_logits,
                   wso_mine, whi_mine, who_mine, hg_norm_mine)
    big = {}
    for name, p, w, m, v in (("sb_w_in", r["p_wsi"], sb_w_in, m_sb_w_in, v_sb_w_in),
                             ("sb_w_out", r["p_wso"], sb_w_out, m_sb_w_out, v_sb_w_out),
                             ("hg_w_in", r["p_whi"], hg_w_in, m_hg_w_in, v_hg_w_in),
                             ("hg_w_out", r["p_who"], hg_w_out, m_hg_w_out, v_hg_w_out)):
        big[name] = [o[None] for o in reduce_adamw(p, w[0], m[0], v[0], "adamw_" + name)]

    def rows8(a):
        return a.reshape(8, HEAD_DIM)

    def rows16(a):
        return a.reshape(16, HEAD_DIM)

    small_in = [rows8(sb_norm), rows8(m_sb_norm), rows8(v_sb_norm),
                sb_q_gain, m_sb_q_gain, v_sb_q_gain,
                sb_k_gain, m_sb_k_gain, v_sb_k_gain,
                hg_norm, m_hg_norm, v_hg_norm,
                hg_o_gain, m_hg_o_gain, v_hg_o_gain,
                rows16(hg_lb_logits), rows16(m_hg_lb_logits), rows16(v_hg_lb_logits)]
    so = small_update(rows8(r["g_sbn"]), rows8(r["g_hgn"]), r["g_lb"].reshape(b, N_HEADS, HEAD_DIM),
                      r["g_qg"].reshape(b * N_HEADS, HEAD_DIM), r["g_kg"].reshape(b * N_HEADS, HEAD_DIM),
                      r["g_og"].reshape(b * N_HEADS, HEAD_DIM), r["loss_parts"], small_in)
    loss = so[0][0, 0]
    shapes = {"sb_norm": (1, D_MODEL), "sb_q_gain": (1, HEAD_DIM), "sb_k_gain": (1, HEAD_DIM),
              "hg_norm": (1, HEAD_DIM), "hg_o_gain": (1, HEAD_DIM), "hg_lb_logits": (2, D_MODEL)}
    small = {}
    for i, name in enumerate(("sb_norm", "sb_q_gain", "sb_k_gain", "hg_norm", "hg_o_gain", "hg_lb_logits")):
        small[name] = [o.reshape(shapes[name]) for o in so[1 + 4 * i:5 + 4 * i]]
    order = ("sb_norm", "sb_w_in", "sb_q_gain", "sb_k_gain", "sb_w_out",
             "hg_norm", "hg_w_in", "hg_o_gain", "hg_w_out", "hg_lb_logits")
    res = {**big, **small}
    return (loss, r["gx"]) + tuple(res[n][j] for j in range(4) for n in order)
```

```python
import functools
import math

import jax
import jax.numpy as jnp
from jax import lax
from jax.experimental import pallas as pl
from jax.experimental.pallas import tpu as pltpu

F32 = jnp.float32
BF16 = jnp.bfloat16

N_DEV = 8
D_MODEL = 1024
N_HEADS = 8
HEAD_DIM = 128
RMS_EPS = 1e-6
ATTN_BLOCK = 128
HG_CHUNK = 64
HG_SUB = 16
HG_UNROLL_FWD = 16
HG_UNROLL_BWD = 16
EXP_CLAMP = 80.0
SB_HEADS_PER_STEP = 2
SB_QBLOCKS_PER_STEP = 4
SB_GROUP_COLS = SB_HEADS_PER_STEP * 128
SB_TOP_ROWS = 32
SB_LOG_WEIGHT_FLOOR = -104.0
VMEM_LIMIT_BYTES = 48 * 1024 * 1024
W_COLS = 4 * D_MODEL // N_DEV
W_ROWS = D_MODEL // N_DEV

ADAM_LR = 0.001
ADAM_B1 = 0.9
ADAM_B2 = 0.999
ADAM_EPS = 1e-08
ADAM_WD = 0.01
ADAM_STEP = 10

NT = (((1,), (1,)), ((), ()))
TN = (((0,), (0,)), ((), ()))
NN = (((1,), (0,)), ((), ()))


def _pcall(body, *, name, **kw):
    return pl.pallas_call(body, name=name, **kw)


def _params(*sem):
    return pltpu.CompilerParams(dimension_semantics=sem, vmem_limit_bytes=VMEM_LIMIT_BYTES)


def _dot(a, b, dims=NN):
    return lax.dot_general(a, b, dims, preferred_element_type=F32)


def _dot_exact(a, m, dims=NN, left=False):
    hi = a.astype(BF16)
    lo = (a - hi.astype(F32)).astype(BF16)
    if left:
        return _dot(m, hi, dims) + _dot(m, lo, dims)
    return _dot(hi, m, dims) + _dot(lo, m, dims)


def _sigmoid(x):
    return 1.0 / (1.0 + jnp.exp(-x))


def _sigmoid_pair(x):
    e = jnp.exp(-jnp.abs(x))
    big = 1.0 / (1.0 + e)
    small = e * big
    pos = x >= 0
    return jnp.where(pos, big, small), jnp.where(pos, small, big)


def _rms_scale(x):
    return lax.rsqrt(jnp.mean(x * x, axis=-1, keepdims=True) + RMS_EPS)


def _row_tile(t, want):
    return want if t % want == 0 else t


MESH = pl.DeviceIdType.MESH
N_PEERS = N_DEV - 1
_ANY = pl.BlockSpec(memory_space=pl.ANY)
_VMEM = pl.BlockSpec(memory_space=pltpu.VMEM)


def _mesh_pos():
    return lax.axis_index("x"), lax.axis_index("y"), lax.axis_index("c")


def _linear(pos):
    return 4 * pos[0] + 2 * pos[1] + pos[2]


def _peer(pos, k):
    flips = ((k + 1) >> 2 & 1, (k + 1) >> 1 & 1, (k + 1) & 1)
    return tuple(1 - p if f else p for p, f in zip(pos, flips))


def _chip(pos):
    return 2 * pos[0] + pos[1]


ALL_PEERS = tuple(range(N_PEERS))
SAME_CORE_PEERS = (1, 3, 5)


def _exchange_copies(pairs, send_sems, recv_sems, local_sems, pos, landing, peers, slot):
    me = slot(pos)
    local, remote = [], []
    for a, (src_of, dst) in enumerate(pairs):
        local.append(pltpu.make_async_copy(src_of(pos), dst.at[me], local_sems.at[a]))
        for k in peers:
            peer = _peer(pos, k)
            remote.append(pltpu.make_async_remote_copy(
                src_ref=src_of(peer), dst_ref=dst.at[slot(peer) if landing else me],
                send_sem=send_sems.at[a, k], recv_sem=recv_sems.at[a, k], device_id=peer, device_id_type=MESH))
    return local, remote


def _exchange_start(pairs, send_sems, recv_sems, local_sems, pos, peers=ALL_PEERS, slot=_linear):
    local, sent = _exchange_copies(pairs, send_sems, recv_sems, local_sems, pos, False, peers, slot)
    for copy in local + sent:
        copy.start()


def _exchange_wait(pairs, send_sems, recv_sems, local_sems, pos, peers=ALL_PEERS, slot=_linear):
    local, landed = _exchange_copies(pairs, send_sems, recv_sems, local_sems, pos, True, peers, slot)
    for copy in landed:
        copy.wait_recv()
        copy.wait_send()
    for copy in local:
        copy.wait()


def _exchange_sems(n):
    return [pltpu.SemaphoreType.DMA((n, N_PEERS)), pltpu.SemaphoreType.DMA((n, N_PEERS)),
            pltpu.SemaphoreType.DMA((n,))]


class _Rider:
    def __init__(self, arrays, scatter, chips=False):
        self.arrays = list(arrays)
        self.scatter = scatter
        self.peers = SAME_CORE_PEERS if chips else ALL_PEERS
        self.slot = _chip if chips else _linear
        self.out_shapes = [jax.ShapeDtypeStruct(a.shape if scatter else (N_DEV,) + a.shape, a.dtype)
                           for a in self.arrays]

    def pairs(self, in_refs, out_refs):
        if self.scatter:
            return [((lambda pos, r=r: r.at[self.slot(pos)]), o) for r, o in zip(in_refs, out_refs)]
        return [((lambda pos, r=r: r), o) for r, o in zip(in_refs, out_refs)]


def _pcall_riding(body, rider, args, *, name, grid, in_specs, out_specs, out_shape, semantics, scratch_shapes=(),
                  input_output_aliases=None):
    aliases = input_output_aliases or {}
    if rider is None:
        outs = _pcall(body, name=name, grid=grid, in_specs=list(in_specs), out_specs=list(out_specs),
                      out_shape=list(out_shape), scratch_shapes=list(scratch_shapes), input_output_aliases=aliases,
                      compiler_params=_params(*semantics))(*args)
        return list(outs), []
    n_in, n_out, n_scr, n_r = len(in_specs), len(out_specs), len(scratch_shapes), len(rider.arrays)

    def riding(*refs):
        ins, refs = refs[:n_in], refs[n_in:]
        rider_in, refs = refs[:n_r], refs[n_r:]
        outs, refs = refs[:n_out], refs[n_out:]
        rider_out, refs = refs[:n_r], refs[n_r:]
        scratch, sems = refs[:n_scr], refs[n_scr:]
        pairs = rider.pairs(rider_in, rider_out)
        first = functools.reduce(jnp.logical_and, [pl.program_id(a) == 0 for a in range(len(grid))])
        last = functools.reduce(jnp.logical_and, [pl.program_id(a) == g - 1 for a, g in enumerate(grid)])

        @pl.when(first)
        def _():
            _exchange_start(pairs, *sems, _mesh_pos(), rider.peers, rider.slot)

        body(*ins, *outs, *scratch)

        @pl.when(last)
        def _():
            _exchange_wait(pairs, *sems, _mesh_pos(), rider.peers, rider.slot)

    outs = _pcall(riding, name=name, grid=grid, in_specs=list(in_specs) + [_ANY] * n_r,
                  out_specs=list(out_specs) + [_ANY] * n_r, out_shape=list(out_shape) + rider.out_shapes,
                  scratch_shapes=list(scratch_shapes) + _exchange_sems(n_r), input_output_aliases=aliases,
                  compiler_params=_params(*(("arbitrary",) * len(grid))))(*args, *rider.arrays)
    return list(outs[:n_out]), list(outs[n_out:])


def rms_inproj(x2, gain, wg, name, rider=None, qk_gains=None):
    t = x2.shape[0]
    with_qkv = qk_gains is not None
    tm = _row_tile(t, 256 if with_qkv else 512)

    def body(x_ref, g_ref, w_ref, *rest):
        if with_qkv:
            qg_ref, kg_ref, proj_ref, ut_ref, qkv_ref = rest
            head_gain = (qg_ref, kg_ref)
        else:
            proj_ref, ut_ref = rest
        x = x_ref[...]
        u = x * _rms_scale(x) * g_ref[...]
        ut_ref[...] = u.T.astype(BF16)
        u = u.astype(BF16)
        for p in range(N_DEV):
            part, lo = p // 2, (p % 2) * W_COLS
            res = _dot(u, w_ref[p])
            proj_ref[part, :, lo:lo + W_COLS] = res
            if with_qkv and part < 3:
                for h in range(W_COLS // HEAD_DIM):
                    y = res[:, h * HEAD_DIM:(h + 1) * HEAD_DIM]
                    if part < 2:
                        y = y * _rms_scale(y) * head_gain[part][...]
                    qkv_ref[part, :, lo + h * HEAD_DIM:lo + (h + 1) * HEAD_DIM] = y.astype(BF16)

    vec = pl.BlockSpec((1, D_MODEL), lambda i: (0, 0))
    in_specs = [pl.BlockSpec((tm, D_MODEL), lambda i: (i, 0)), vec,
                pl.BlockSpec((N_DEV, D_MODEL, W_COLS), lambda i: (0, 0, 0))]
    out_specs = [pl.BlockSpec((4, tm, D_MODEL), lambda i: (0, i, 0)), pl.BlockSpec((D_MODEL, tm), lambda i: (0, i))]
    out_shape = [jax.ShapeDtypeStruct((4, t, D_MODEL), F32), jax.ShapeDtypeStruct((D_MODEL, t), BF16)]
    args = (x2, gain, wg)
    if with_qkv:
        in_specs += [pl.BlockSpec((1, HEAD_DIM), lambda i: (0, 0))] * 2
        out_specs.append(pl.BlockSpec((3, tm, D_MODEL), lambda i: (0, i, 0)))
        out_shape.append(jax.ShapeDtypeStruct((3, t, D_MODEL), BF16))
        args += tuple(qk_gains)
    return _pcall_riding(body, rider, args, name=name, grid=(t // tm,), in_specs=in_specs, out_specs=out_specs,
                         out_shape=out_shape, semantics=("parallel",))


def gate_outproj(o2, proj, w_out, resid, target, name):
    t = o2.shape[0]
    tm = _row_tile(t, 512)
    with_loss = target is not None

    def body(o_ref, gate_ref, w_ref, r_ref, *rest):
        g = gate_ref[0]
        og = (o_ref[...] * (g * _sigmoid(g))).astype(BF16)
        h = r_ref[...] + _dot(og, w_ref[...])
        if with_loss:
            t_ref, dh_ref, loss_ref = rest
            err = h - t_ref[...]
            dh_ref[...] = err * (1.0 / D_MODEL)
            part = 0.5 * jnp.sum(jnp.mean(err * err, axis=-1, keepdims=True))
            loss_ref[...] = jnp.full(loss_ref.shape, part, F32)
        else:
            (h_ref,) = rest
            h_ref[...] = h

    row = pl.BlockSpec((tm, D_MODEL), lambda i: (i, 0))
    in_specs = [row,
                pl.BlockSpec((1, tm, D_MODEL), lambda i: (3, i, 0)),
                pl.BlockSpec((D_MODEL, D_MODEL), lambda i: (0, 0)),
                row]
    args = [o2, proj, w_out, resid]
    if with_loss:
        in_specs.append(row)
        args.append(target)
        out_specs = [row, pl.BlockSpec((1, 8, 128), lambda i: (i, 0, 0))]
        out_shape = [jax.ShapeDtypeStruct((t, D_MODEL), F32),
                     jax.ShapeDtypeStruct((t // tm, 8, 128), F32)]
    else:
        out_specs = row
        out_shape = jax.ShapeDtypeStruct((t, D_MODEL), F32)
    return _pcall(body, name=name, grid=(t // tm,), in_specs=in_specs, out_specs=out_specs,
                  out_shape=out_shape, compiler_params=_params("parallel"))(*args)


def _head_spec(s, part):
    return pl.BlockSpec((1, 1, s, HEAD_DIM), lambda b, h: (part, b, 0, h))


def _seq_spec(s):
    return pl.BlockSpec((1, s, HEAD_DIM), lambda b, h: (b, 0, h))


_GAIN_SPEC = pl.BlockSpec((1, HEAD_DIM), lambda b, h: (0, 0))
_HEAD_ROW_SPEC = pl.BlockSpec((1, 1, 1, HEAD_DIM), lambda b, h: (b, h, 0, 0))


def _sb_group_spec(s, part):
    return pl.BlockSpec((1, 1, s, SB_GROUP_COLS), lambda b, g: (part, b, 0, g))


def _sb_seq_group_spec(s):
    return pl.BlockSpec((1, s, SB_GROUP_COLS), lambda b, g: (b, 0, g))


_SB_GROUP_ROW_SPEC = pl.BlockSpec((1, SB_HEADS_PER_STEP, 1, HEAD_DIM), lambda b, g: (b, g, 0, 0))


def _sb_chains(m):
    return [(h, m * SB_QBLOCKS_PER_STEP + r) for h in range(SB_HEADS_PER_STEP) for r in range(SB_QBLOCKS_PER_STEP)]


def _sb_logits(qi, kj):
    return _dot(qi, kj, NT) * (HEAD_DIM ** -0.5)


def _sb_scores(z, diag, live, tri_lt):
    soft = jnp.log(1.0 + jnp.exp(-jnp.abs(z)))
    valid = jnp.logical_and(live, jnp.logical_or(jnp.logical_not(diag), tri_lt))
    log_skip = jnp.where(valid, -(jnp.maximum(z, 0.0) + soft), 0.0)
    log_beta = jnp.minimum(z, 0.0) - soft
    return log_skip, log_beta, valid


def _sb_keys_left(chains, watch, state):
    done, carries = state[0], state[1]
    worst = None
    for (_, i), c in zip(chains, carries):
        c = jnp.where(done <= i, c[watch], -jnp.inf)
        worst = c if worst is None else jnp.maximum(worst, c)
    return jnp.logical_and(done <= chains[-1][1],
                           jnp.logical_or(done == 0, jnp.max(worst) > SB_LOG_WEIGHT_FLOOR))


def _sb_key_rows(i, done):
    j = i - done
    return pl.ds(pl.multiple_of(jnp.maximum(j, 0) * ATTN_BLOCK, ATTN_BLOCK), ATTN_BLOCK), j >= 0


def _sb_head(ref, h, rows):
    return ref[0, 0, rows, h * HEAD_DIM:(h + 1) * HEAD_DIM]


def _sb_rows(i, offset, count):
    return pl.ds(pl.multiple_of(jnp.maximum(i, 0) * ATTN_BLOCK + offset, 8), count)


def sb_attn_fwd(qkv4, rider=None):
    _, b, s, _ = qkv4.shape
    blk, top = ATTN_BLOCK, SB_TOP_ROWS
    ngroups = s // blk // SB_QBLOCKS_PER_STEP
    assert ngroups * SB_QBLOCKS_PER_STEP * blk == s

    def body(q_ref, k_ref, v_ref, o_ref):
        row = lax.broadcasted_iota(jnp.int32, (blk, blk), 0)
        col = lax.broadcasted_iota(jnp.int32, (blk, blk), 1)
        tri_lt = col < row
        suffix = (row > col).astype(BF16)

        def step(items):
            where = [_sb_key_rows(i, done) for _, i, done, _, _, _, _ in items]
            zs = [_sb_logits(q, _sb_head(k_ref, h, rows)) for (h, _, _, q, _, _, _), (rows, _) in zip(items, where)]
            scored = [_sb_scores(z, done == 0, live, mask)
                      for z, (_, _, done, _, mask, _, _), (_, live) in zip(zs, items, where)]
            afters = [_dot_exact(log_skip, suffix) for log_skip, _, _ in scored]
            ws = [jnp.where(valid, jnp.exp(log_beta + after + c), 0.0).astype(BF16)
                  for (_, log_beta, valid), after, (_, _, _, _, _, c, _) in zip(scored, afters, items)]
            accs = [acc + _dot(w, _sb_head(v_ref, h, rows))
                    for (h, _, _, _, _, _, acc), (rows, _), w in zip(items, where, ws)]
            cs = [c + jnp.sum(log_skip, axis=1, keepdims=True)
                  for (log_skip, _, _), (_, _, _, _, _, c, _) in zip(scored, items)]
            return cs, accs

        def top_items(chains, done, cs, accs):
            return [(h, i, done, _sb_head(q_ref, h, _sb_rows(i, 0, top)), tri_lt[:top, :], c, acc)
                    for (h, i), c, acc in zip(chains, cs, accs)]

        def finish_tops(chains, state):
            def k_step(state):
                done, cs, accs = state
                cs, accs = step(top_items(chains, done, cs, accs))
                return done + 1, tuple(cs), tuple(accs)

            _, _, accs = lax.while_loop(functools.partial(_sb_keys_left, chains, slice(0, top)), k_step, state)
            for (h, i), acc in zip(chains, accs):
                o_ref[0, _sb_rows(i, 0, top), h * HEAD_DIM:(h + 1) * HEAD_DIM] = acc

        def q_group(m, before):
            chains, chains_before = _sb_chains(m), _sb_chains(m - 1)
            qis = [_sb_head(q_ref, h, _sb_rows(i, 0, blk)) for h, i in chains]
            n = len(chains)

            def items_of(done, cs, accs):
                return [(h, i, done, q, tri_lt, c, acc) for (h, i), q, c, acc in zip(chains, qis, cs, accs)]

            def k_step(state):
                done, cs, accs = state
                cs, accs = step(items_of(done, cs, accs))
                return done + 1, tuple(cs), tuple(accs)

            done_b, cs_b, accs_b = before
            cs0, accs0 = step(items_of(jnp.int32(0), (jnp.zeros((blk, 1), F32),) * n,
                                       (jnp.zeros((blk, HEAD_DIM), F32),) * n)
                              + top_items(chains_before, done_b, cs_b, accs_b))
            done, cs, accs = lax.while_loop(functools.partial(_sb_keys_left, chains, slice(top, blk)), k_step,
                                            (jnp.int32(1), tuple(cs0[:n]), tuple(accs0[:n])))

            @pl.when(m > 0)
            def _():
                finish_tops(chains_before, (done_b + 1, tuple(cs0[n:]), tuple(accs0[n:])))

            for (h, i), acc in zip(chains, accs):
                o_ref[0, _sb_rows(i, top, blk - top), h * HEAD_DIM:(h + 1) * HEAD_DIM] = acc[top:, :]
            return done, tuple(c[:top, :] for c in cs), tuple(acc[:top, :] for acc in accs)

        n = SB_HEADS_PER_STEP * SB_QBLOCKS_PER_STEP
        nothing = (jnp.int32(0), (jnp.zeros((top, 1), F32),) * n, (jnp.zeros((top, HEAD_DIM), F32),) * n)
        last = lax.fori_loop(0, ngroups, q_group, nothing)
        finish_tops(_sb_chains(ngroups - 1), last)

    (o,), extra = _pcall_riding(
        body, rider, (qkv4, qkv4, qkv4),
        name="sb_attn_fwd", grid=(b, N_HEADS // SB_HEADS_PER_STEP),
        in_specs=[_sb_group_spec(s, 0), _sb_group_spec(s, 1), _sb_group_spec(s, 2)],
        out_specs=[_sb_seq_group_spec(s)],
        out_shape=[jax.ShapeDtypeStruct((b, s, D_MODEL), F32)],
        semantics=("parallel", "parallel"))
    return o, extra


def _hg_masks():
    c = HG_CHUNK
    row = lax.broadcasted_iota(jnp.int32, (c, c), 0)
    col = lax.broadcasted_iota(jnp.int32, (c, c), 1)
    incl = (col <= row)
    lower = incl.astype(BF16)
    before_sub = (col < (row // HG_SUB) * HG_SUB).astype(BF16)
    upper = (col >= row).astype(BF16)
    return incl, lower, before_sub, upper


def _hg_lower_bound(lbl_ref):
    l0 = lbl_ref[0, 0]
    l1 = lbl_ref[1, 0]
    d = l1 - l0
    return _sigmoid_pair(d)


def _hg_gates(qp, fp, lb, oml):
    sq = _sigmoid(qp)
    sf, sfn = _sigmoid_pair(fp)
    f = lb + oml * sf
    return dict(qp=qp, sq=sq, q=qp * sq, sf=sf, sfn=sfn, f=f, k=oml * sfn, logf=jnp.log(f))


def _hg_intra(qds, ks, gcs, grs, incl):
    subs = range(HG_CHUNK // HG_SUB)
    qdbs = [qd.astype(BF16) for qd in qds]
    ess = [[jnp.exp(jnp.minimum(gr[sub * HG_SUB:sub * HG_SUB + 1, :] - gc, EXP_CLAMP)) for sub in subs]
           for gc, gr in zip(gcs, grs)]
    ksbs = [[(k * e).astype(BF16) for e in es] for k, es in zip(ks, ess)]
    rows = [[_dot(qdb[sub * HG_SUB:(sub + 1) * HG_SUB, :], ksb[sub], NT) for sub in subs]
            for qdb, ksb in zip(qdbs, ksbs)]
    a_s = [jnp.where(incl, jnp.concatenate(r, axis=0), 0.0) for r in rows]
    return a_s, qdbs, ksbs, ess


def _hg_group_rows(outer, unroll):
    ns = [outer * unroll + u for u in range(unroll)]
    return ns, [pl.ds(pl.multiple_of(n * HG_CHUNK, HG_CHUNK), HG_CHUNK) for n in ns]


def _state_spec(nchunk):
    return pl.BlockSpec((1, 1, nchunk, HEAD_DIM, HEAD_DIM), lambda b, h: (b, h, 0, 0, 0))


def hgrn2_fwd(proj4, lbl4, o_gain, rider=None):
    _, b, s, _ = proj4.shape
    nchunk = s // HG_CHUNK
    c = HG_CHUNK
    unroll = math.gcd(nchunk, HG_UNROLL_FWD)

    def body(q_ref, f_ref, i_ref, lbl_ref, og_ref, o_ref, oraw_ref, st_ref):
        incl, lower, before_sub, _ = _hg_masks()
        lb, oml = _hg_lower_bound(lbl_ref)

        def group(outer, st):
            ns, rows = _hg_group_rows(outer, unroll)
            vs = [_hg_gates(q_ref[0, 0, r, :], f_ref[0, 0, r, :], lb, oml) for r in rows]
            inps = [i_ref[0, 0, r, :].astype(BF16) for r in rows]
            gcs = [_dot_exact(v["logf"], lower, left=True) for v in vs]
            grs = [_dot_exact(v["logf"], before_sub, left=True) for v in vs]
            a_s, _, _, _ = _hg_intra([v["q"] * jnp.exp(gc - gr) for v, gc, gr in zip(vs, gcs, grs)],
                                     [v["k"] for v in vs], gcs, grs, incl)
            gls = [gc[c - 1:c, :] for gc in gcs]
            adds = [_dot(inp, (v["k"] * jnp.exp(gl - gc)).astype(BF16), TN)
                    for inp, v, gl, gc in zip(inps, vs, gls, gcs)]
            o_intra = [_dot(a.astype(BF16), inp) for a, inp in zip(a_s, inps)]
            sts = []
            for gl, add in zip(gls, adds):
                sts.append(st)
                st = st * jnp.exp(gl) + add
            outs = [oi + _dot((v["q"] * jnp.exp(gc)).astype(BF16), s0.astype(BF16), NT)
                    for oi, v, gc, s0 in zip(o_intra, vs, gcs, sts)]
            for n, r, s0, o in zip(ns, rows, sts, outs):
                st_ref[0, 0, n] = s0
                oraw_ref[0, r, :] = o
                o_ref[0, r, :] = o * _rms_scale(o) * og_ref[...]
            return st

        lax.fori_loop(0, nchunk // unroll, group, jnp.zeros((HEAD_DIM, HEAD_DIM), F32))

    seq = jax.ShapeDtypeStruct((b, s, D_MODEL), F32)
    return _pcall_riding(
        body, rider, (proj4, proj4, proj4, lbl4, o_gain), name="hgrn2_fwd", grid=(b, N_HEADS),
        in_specs=[_head_spec(s, 0), _head_spec(s, 1), _head_spec(s, 2),
                  pl.BlockSpec((2, 1, 1, HEAD_DIM), lambda b, h: (0, h, 0, 0)), _GAIN_SPEC],
        out_specs=[_seq_spec(s), _seq_spec(s), _state_spec(nchunk)],
        out_shape=[seq, seq, jax.ShapeDtypeStruct((b, N_HEADS, nchunk, HEAD_DIM, HEAD_DIM), F32)],
        semantics=("parallel", "parallel"))


def outproj_bwd(dh, w_out, o2, proj, name):
    t = dh.shape[0]
    tm = _row_tile(t, 512)

    def body(dh_ref, w_ref, o_ref, gate_ref, do_ref, dproj_ref, dw_ref):
        dhb = dh_ref[...].astype(BF16)
        dog = _dot(dhb, w_ref[...], NT)
        g = gate_ref[0]
        sg = _sigmoid(g)
        silu = g * sg
        o = o_ref[...]
        do_ref[...] = dog * silu
        dproj_ref[0] = dog * o * (sg * (1.0 + g * (1.0 - sg)))
        part = _dot((o * silu).astype(BF16), dhb, TN)

        @pl.when(pl.program_id(0) == 0)
        def _():
            dw_ref[...] = part

        @pl.when(pl.program_id(0) > 0)
        def _():
            dw_ref[...] += part

    row = pl.BlockSpec((tm, D_MODEL), lambda i: (i, 0))
    full = pl.BlockSpec((D_MODEL, D_MODEL), lambda i: (0, 0))
    return _pcall(
        body, name=name, grid=(t // tm,),
        in_specs=[row, full, row, pl.BlockSpec((1, tm, D_MODEL), lambda i: (3, i, 0))],
        out_specs=[row, pl.BlockSpec((1, tm, D_MODEL), lambda i: (3, i, 0)), full],
        out_shape=[jax.ShapeDtypeStruct((t, D_MODEL), F32),
                   jax.ShapeDtypeStruct((4, t, D_MODEL), F32),
                   jax.ShapeDtypeStruct((D_MODEL, D_MODEL), F32)],
        compiler_params=_params("arbitrary"),
    )(dh, w_out, o2, proj)


def inproj_bwd_dx(dproj, wg, x2, gain, dres, name, rider=None):
    t = x2.shape[0]
    tm = _row_tile(t, 512)

    def body(d_ref, w_ref, x_ref, g_ref, r_ref, dx_ref, dg_ref):
        du = jnp.zeros((tm, D_MODEL), F32)
        for p in range(N_DEV):
            cols = slice((p % 2) * W_COLS, (p % 2 + 1) * W_COLS)
            du = du + _dot(d_ref[p // 2, :, cols].astype(BF16), w_ref[p], NT)
        x = x_ref[...]
        r = _rms_scale(x)
        xh = x * r
        a = du * g_ref[...]
        dx_ref[...] = r_ref[...] + r * (a - xh * jnp.mean(a * xh, axis=-1, keepdims=True))
        part = jnp.sum(du * xh, axis=0, keepdims=True)

        @pl.when(pl.program_id(0) == 0)
        def _():
            dg_ref[...] = part

        @pl.when(pl.program_id(0) > 0)
        def _():
            dg_ref[...] += part

    row = pl.BlockSpec((tm, D_MODEL), lambda i: (i, 0))
    vec = pl.BlockSpec((1, D_MODEL), lambda i: (0, 0))
    return _pcall_riding(
        body, rider, (dproj, wg, x2, gain, dres), name=name, grid=(t // tm,),
        in_specs=[pl.BlockSpec((4, tm, D_MODEL), lambda i: (0, i, 0)),
                  pl.BlockSpec((N_DEV, D_MODEL, W_COLS), lambda i: (0, 0, 0)),
                  row, vec, row],
        out_specs=[row, vec],
        out_shape=[jax.ShapeDtypeStruct((t, D_MODEL), F32), jax.ShapeDtypeStruct((1, D_MODEL), F32)],
        semantics=("arbitrary",))


def inproj_bwd_dw(ut, dproj, name, out_dtype):
    t = ut.shape[1]

    def body(ut_ref, d_ref, dw_ref):
        dw_ref[0] = _dot(ut_ref[...], d_ref[0].astype(BF16)).astype(dw_ref.dtype)

    return _pcall(
        body, name=name, grid=(N_DEV,),
        in_specs=[pl.BlockSpec((D_MODEL, t), lambda j: (0, 0)),
                  pl.BlockSpec((1, t, W_COLS), lambda j: (j // 2, 0, j % 2))],
        out_specs=pl.BlockSpec((1, D_MODEL, W_COLS), lambda j: (j, 0, 0)),
        out_shape=jax.ShapeDtypeStruct((N_DEV, D_MODEL, W_COLS), out_dtype),
        compiler_params=_params("parallel"),
    )(ut, dproj)


def _rms_bwd(x, gain, dy):
    r = _rms_scale(x)
    xh = x * r
    a = dy * gain
    return r * (a - xh * jnp.mean(a * xh, axis=-1, keepdims=True)), dy * xh


def sb_attn_bwd(qkv4, proj4, do3, o3, q_gain, k_gain, dproj4, rider=None):
    _, b, s, _ = proj4.shape
    blk, top = ATTN_BLOCK, SB_TOP_ROWS
    nq = s // blk
    ngroups = nq // SB_QBLOCKS_PER_STEP
    assert ngroups * SB_QBLOCKS_PER_STEP * blk == s
    scale = HEAD_DIM ** -0.5

    def body(qn_ref, kn_ref, v_ref, q_ref, k_ref, do_ref, o_ref, qg_ref, kg_ref, _alias, d_ref, dqg_ref, dkg_ref, dob):
        for h in range(SB_HEADS_PER_STEP):
            dob[h] = do_ref[0, :, h * HEAD_DIM:(h + 1) * HEAD_DIM].astype(BF16)
        d_ref[...] = jnp.zeros_like(d_ref)
        row = lax.broadcasted_iota(jnp.int32, (blk, blk), 0)
        col = lax.broadcasted_iota(jnp.int32, (blk, blk), 1)
        tri_lt = col < row
        suffix = (row > col).astype(BF16)
        suffix_incl = (row >= col).astype(BF16)

        def step(items):
            heads = [it[0] for it in items]
            where = [_sb_key_rows(it[1], it[2]) for it in items]
            kjs = [_sb_head(kn_ref, h, rows) for h, (rows, _) in zip(heads, where)]
            zs = [_sb_logits(it[3], kj) for it, kj in zip(items, kjs)]
            dws = [_dot(it[4], _sb_head(v_ref, h, rows), NT) for it, h, (rows, _) in zip(items, heads, where)]
            scored = [_sb_scores(z, it[2] == 0, live, it[6]) for z, it, (_, live) in zip(zs, items, where)]
            afters = [_dot_exact(log_skip, suffix) for log_skip, _, _ in scored]
            wbs = [jnp.where(valid, jnp.exp(log_beta + after + it[7]), 0.0).astype(BF16)
                   for (_, log_beta, valid), after, it in zip(scored, afters, items)]
            gs = [dw * wb.astype(F32) for dw, wb in zip(dws, wbs)]
            befores = [it[5] - (_dot_exact(g, suffix_incl) + it[8]) for g, it in zip(gs, items)]
            dzbs = [jnp.where(valid, g - jnp.exp(log_beta) * (g + before), 0.0).astype(BF16)
                    for (_, log_beta, valid), g, before in zip(scored, gs, befores)]
            dqs = [it[9] + _dot(dzb, kj) for it, dzb, kj in zip(items, dzbs, kjs)]
            for it, h, (rows, _), wb, dzb in zip(items, heads, where, wbs, dzbs):
                cols = slice(h * HEAD_DIM, (h + 1) * HEAD_DIM)
                d_ref[2, 0, rows, cols] += _dot(wb, it[4], TN)
                d_ref[1, 0, rows, cols] += _dot(dzb, it[3], TN)
            cs = [it[7] + jnp.sum(log_skip, axis=1, keepdims=True) for (log_skip, _, _), it in zip(scored, items)]
            cgs = [it[8] + jnp.sum(g, axis=1, keepdims=True) for g, it in zip(gs, items)]
            return cs, cgs, dqs

        def top_items(chains, deltas, done, cs, cgs, dqs):
            return [(h, i, done, _sb_head(qn_ref, h, _sb_rows(i, 0, top)), dob[h, _sb_rows(i, 0, top), :], delta,
                     tri_lt[:top, :], c, cg, dq)
                    for (h, i), delta, c, cg, dq in zip(chains, deltas, cs, cgs, dqs)]

        def finish_tops(chains, deltas, state):
            def k_step(state):
                done, cs, cgs, dqs = state
                cs, cgs, dqs = step(top_items(chains, deltas, done, cs, cgs, dqs))
                return done + 1, tuple(cs), tuple(cgs), tuple(dqs)

            _, _, _, dqs = lax.while_loop(functools.partial(_sb_keys_left, chains, slice(0, top)), k_step, state)
            for (h, i), dq in zip(chains, dqs):
                d_ref[0, 0, _sb_rows(i, 0, top), h * HEAD_DIM:(h + 1) * HEAD_DIM] = dq * scale

        def q_group(m, before):
            chains, chains_before = _sb_chains(m), _sb_chains(m - 1)
            deltas_b, before = before[0], before[1:]
            qis, dois, deltas = [], [], []
            for h, i in chains:
                rows_i = _sb_rows(i, 0, blk)
                qis.append(_sb_head(qn_ref, h, rows_i))
                dois.append(dob[h, rows_i, :])
                deltas.append(jnp.sum(dois[-1].astype(F32) * o_ref[0, rows_i, h * HEAD_DIM:(h + 1) * HEAD_DIM],
                                      axis=1, keepdims=True))
            n = len(chains)

            def items_of(done, cs, cgs, dqs):
                return [(h, i, done, q, do, delta, tri_lt, c, cg, dq)
                        for (h, i), q, do, delta, c, cg, dq in zip(chains, qis, dois, deltas, cs, cgs, dqs)]

            def k_step(state):
                done, cs, cgs, dqs = state
                cs, cgs, dqs = step(items_of(done, cs, cgs, dqs))
                return done + 1, tuple(cs), tuple(cgs), tuple(dqs)

            done_b, cs_b, cgs_b, dqs_b = before
            zero = (jnp.zeros((blk, 1), F32),) * n
            new = step(items_of(jnp.int32(0), zero, zero, (jnp.zeros((blk, HEAD_DIM), F32),) * n)
                       + top_items(chains_before, deltas_b, done_b, cs_b, cgs_b, dqs_b))
            done, cs, cgs, dqs = lax.while_loop(functools.partial(_sb_keys_left, chains, slice(top, blk)), k_step,
                                                (jnp.int32(1),) + tuple(tuple(x[:n]) for x in new))

            @pl.when(m > 0)
            def _():
                finish_tops(chains_before, deltas_b, (done_b + 1,) + tuple(tuple(x[n:]) for x in new))

            for (h, i), dq in zip(chains, dqs):
                d_ref[0, 0, _sb_rows(i, top, blk - top), h * HEAD_DIM:(h + 1) * HEAD_DIM] = dq[top:, :] * scale
            first = lambda xs: tuple(x[:top, :] for x in xs)
            return first(deltas), done, first(cs), first(cgs), first(dqs)

        n = SB_HEADS_PER_STEP * SB_QBLOCKS_PER_STEP
        zero = (jnp.zeros((top, 1), F32),) * n
        nothing = (zero, jnp.int32(0), zero, zero, (jnp.zeros((top, HEAD_DIM), F32),) * n)
        last = lax.fori_loop(0, ngroups, q_group, nothing)
        finish_tops(_sb_chains(ngroups - 1), last[0], last[1:])

        def norm_block(i, carry):
            rows = pl.ds(pl.multiple_of(i * blk, blk), blk)
            out = []
            for h in range(SB_HEADS_PER_STEP):
                cols = slice(h * HEAD_DIM, (h + 1) * HEAD_DIM)
                for part, src_ref, gain_ref in ((0, q_ref, qg_ref), (1, k_ref, kg_ref)):
                    dy = d_ref[part, 0, rows, cols] * (scale if part == 1 else 1.0)
                    dx, pg = _rms_bwd(src_ref[0, 0, rows, cols], gain_ref[...], dy)
                    d_ref[part, 0, rows, cols] = dx
                    out.append(carry[len(out)] + jnp.sum(pg, axis=0, keepdims=True))
            return tuple(out)

        sums = lax.fori_loop(0, nq, norm_block, (jnp.zeros((1, HEAD_DIM), F32),) * (2 * SB_HEADS_PER_STEP))
        for h in range(SB_HEADS_PER_STEP):
            dqg_ref[0, h] = sums[2 * h]
            dkg_ref[0, h] = sums[2 * h + 1]

    head_row = jax.ShapeDtypeStruct((b, N_HEADS, 1, HEAD_DIM), F32)
    return _pcall_riding(
        body, rider, (qkv4, qkv4, qkv4, proj4, proj4, do3, o3, q_gain, k_gain, dproj4),
        name="sb_attn_bwd", grid=(b, N_HEADS // SB_HEADS_PER_STEP),
        in_specs=[_sb_group_spec(s, 0), _sb_group_spec(s, 1), _sb_group_spec(s, 2),
                  _sb_group_spec(s, 0), _sb_group_spec(s, 1),
                  _sb_seq_group_spec(s), _sb_seq_group_spec(s), _GAIN_SPEC, _GAIN_SPEC,
                  pl.BlockSpec(memory_space=pl.ANY)],
        out_specs=[pl.BlockSpec((3, 1, s, SB_GROUP_COLS), lambda b, g: (0, b, 0, g)),
                   _SB_GROUP_ROW_SPEC, _SB_GROUP_ROW_SPEC],
        out_shape=[jax.ShapeDtypeStruct(dproj4.shape, F32), head_row, head_row],
        scratch_shapes=[pltpu.VMEM((SB_HEADS_PER_STEP, s, HEAD_DIM), BF16)],
        input_output_aliases={9: 0}, semantics=("parallel", "parallel"))


def hgrn2_bwd(proj4, don3, oraw3, states, lbl4, o_gain, dproj4, rider=None):
    _, b, s, _ = proj4.shape
    nchunk = s // HG_CHUNK
    c = HG_CHUNK
    subs = range(HG_CHUNK // HG_SUB)
    unroll = math.gcd(nchunk, HG_UNROLL_BWD)
    ngroup = nchunk // unroll

    def body(q_ref, f_ref, i_ref, don_ref, oraw_ref, st_ref, lbl_ref, og_ref, _alias, d_ref, dog_ref, dlb_ref):
        incl, lower, before_sub, upper = _hg_masks()
        lb, oml = _hg_lower_bound(lbl_ref)
        last_row = lax.broadcasted_iota(jnp.int32, (c, HEAD_DIM), 0) == c - 1

        def group(m, carry):
            dst, dog_acc, dlb_acc = carry
            ns, rows = _hg_group_rows(ngroup - 1 - m, unroll)
            ns, rows = ns[::-1], rows[::-1]
            vs = [_hg_gates(q_ref[0, 0, r, :], f_ref[0, 0, r, :], lb, oml) for r in rows]
            inps = [i_ref[0, 0, r, :].astype(BF16) for r in rows]
            sts = [st_ref[0, 0, n] for n in ns]
            gcs = [_dot_exact(v["logf"], lower, left=True) for v in vs]
            grs = [_dot_exact(v["logf"], before_sub, left=True) for v in vs]
            e_qs = [jnp.exp(gc - gr) for gc, gr in zip(gcs, grs)]
            a_s, qdbs, ksbs, ess = _hg_intra([v["q"] * e for v, e in zip(vs, e_qs)], [v["k"] for v in vs],
                                             gcs, grs, incl)
            e_gcs = [jnp.exp(gc) for gc in gcs]
            gls = [gc[c - 1:c, :] for gc in gcs]
            e_gls = [jnp.exp(gl) for gl in gls]
            e_ks = [jnp.exp(gl - gc) for gl, gc in zip(gls, gcs)]
            normed = [_rms_bwd(oraw_ref[0, r, :], og_ref[...], don_ref[0, r, :]) for r in rows]
            dobs = [do.astype(BF16) for do, _ in normed]
            dabs = [jnp.where(incl, _dot(dob, inp, NT), 0.0).astype(BF16) for dob, inp in zip(dobs, inps)]
            adds = [_dot(dob, (v["q"] * e).astype(BF16), TN) for dob, v, e in zip(dobs, vs, e_gcs)]
            dq_inters = [_dot(dob, st.astype(BF16)) * e for dob, st, e in zip(dobs, sts, e_gcs)]
            dqds = [jnp.concatenate([_dot(dab[sub * HG_SUB:(sub + 1) * HG_SUB, :], ksb[sub]) for sub in subs], axis=0)
                    for dab, ksb in zip(dabs, ksbs)]
            dkss = [[_dot(dab[sub * HG_SUB:(sub + 1) * HG_SUB, :], qdb[sub * HG_SUB:(sub + 1) * HG_SUB, :], TN)
                     for sub in subs] for dab, qdb in zip(dabs, qdbs)]
            dsts = []
            for e_gl, add in zip(e_gls, adds):
                dsts.append(dst)
                dst = dst * e_gl + add
            dstbs = [d.astype(BF16) for d in dsts]
            dis = [_dot(a.astype(BF16), dob, TN) + _dot((v["k"] * e_k).astype(BF16), dstb, NT)
                   for a, dob, v, e_k, dstb in zip(a_s, dobs, vs, e_ks, dstbs)]
            dk_inters = [_dot(inp, dstb) * e_k for inp, dstb, e_k in zip(inps, dstbs, e_ks)]
            dks, dgcs = [], []
            for u in range(len(rows)):
                q, k = vs[u]["q"], vs[u]["k"]
                dk, dgc_k = dk_inters[u], jnp.zeros((c, HEAD_DIM), F32)
                for sub in subs:
                    dk = dk + dkss[u][sub] * ess[u][sub]
                    dgc_k = dgc_k + dkss[u][sub] * ksbs[u][sub].astype(F32)
                at_last = (jnp.sum(k * dk_inters[u], axis=0, keepdims=True)
                           + e_gls[u] * jnp.sum(sts[u] * dsts[u], axis=0, keepdims=True))
                dks.append(dk)
                dgcs.append((qdbs[u].astype(F32) * dqds[u] - dgc_k) + (q * dq_inters[u] - k * dk_inters[u])
                            + jnp.where(last_row, at_last, 0.0))
            dlf_fs = [_dot_exact(dgc, upper, left=True) / v["f"] for dgc, v in zip(dgcs, vs)]
            for u, r in enumerate(rows):
                v = vs[u]
                dq = dqds[u] * e_qs[u] + dq_inters[u]
                d_ref[0, 0, r, :] = dq * (v["sq"] * (1.0 + v["qp"] * (1.0 - v["sq"])))
                d_ref[1, 0, r, :] = (dlf_fs[u] - dks[u]) * (oml * v["sf"] * v["sfn"])
                d_ref[2, 0, r, :] = dis[u]
                dlb_acc = dlb_acc + jnp.sum((dlf_fs[u] - dks[u]) * v["sfn"], axis=0, keepdims=True)
                dog_acc = dog_acc + jnp.sum(normed[u][1], axis=0, keepdims=True)
            return dst, dog_acc, dlb_acc

        zero = jnp.zeros((1, HEAD_DIM), F32)
        _, dog, dlb = lax.fori_loop(0, ngroup, group, (jnp.zeros((HEAD_DIM, HEAD_DIM), F32), zero, zero))
        dog_ref[0, 0] = dog
        dlb_ref[0, 0] = dlb

    head_row = jax.ShapeDtypeStruct((b, N_HEADS, 1, HEAD_DIM), F32)
    return _pcall_riding(
        body, rider, (proj4, proj4, proj4, don3, oraw3, states, lbl4, o_gain, dproj4),
        name="hgrn2_bwd", grid=(b, N_HEADS),
        in_specs=[_head_spec(s, 0), _head_spec(s, 1), _head_spec(s, 2), _seq_spec(s), _seq_spec(s),
                  _state_spec(nchunk), pl.BlockSpec((2, 1, 1, HEAD_DIM), lambda b, h: (0, h, 0, 0)), _GAIN_SPEC,
                  pl.BlockSpec(memory_space=pl.ANY)],
        out_specs=[pl.BlockSpec((3, 1, s, HEAD_DIM), lambda b, h: (0, b, 0, h)), _HEAD_ROW_SPEC, _HEAD_ROW_SPEC],
        out_shape=[jax.ShapeDtypeStruct(dproj4.shape, F32), head_row, head_row],
        input_output_aliases={8: 0}, semantics=("parallel", "parallel"))


def local_step(x, target, sb_norm, wsi, sb_q_gain, sb_k_gain, hg_o_gain, hg_lb_logits, wso_mine, whi_mine, who_mine,
               hg_norm_mine):
    b, s, _ = x.shape
    t = b * s
    x2 = x.reshape(t, D_MODEL)
    tg2 = target.reshape(t, D_MODEL)
    lbl4 = hg_lb_logits.reshape(2, N_HEADS, 1, HEAD_DIM)
    four = (4, b, s, D_MODEL)
    three = (b, s, D_MODEL)
    rows8 = (N_DEV, W_ROWS, D_MODEL)

    (proj0, u0, qkv0), (wso, hgn) = rms_inproj(x2, sb_norm, wsi, "sb_inproj",
                                               _Rider([wso_mine, hg_norm_mine], scatter=False),
                                               qk_gains=(sb_q_gain, sb_k_gain))
    qkv0 = qkv0.reshape(3, b, s, D_MODEL)
    wso = wso.reshape(D_MODEL, D_MODEL)
    hg_norm_full = hgn[:, 0, :].reshape(1, D_MODEL)
    o0, (whi,) = sb_attn_fwd(qkv0, _Rider([whi_mine], scatter=False))
    o0 = o0.reshape(t, D_MODEL)
    h1 = gate_outproj(o0, proj0, wso, x2, None, "sb_outproj")
    (proj1, u1), _ = rms_inproj(h1, hg_norm_full, whi, "hg_inproj")
    (o1, o1_raw, states), (who,) = hgrn2_fwd(proj1.reshape(four), lbl4, hg_o_gain,
                                             _Rider([who_mine], scatter=False))
    who = who.reshape(D_MODEL, D_MODEL)
    o1 = o1.reshape(t, D_MODEL)
    dh2, loss_parts = gate_outproj(o1, proj1, who, h1, tg2, "hg_outproj_loss")

    do1, dproj1, g_who = outproj_bwd(dh2, who, o1, proj1, "hg_outproj_bwd")
    (dproj1, g_og, g_lb), (p_who,) = hgrn2_bwd(proj1.reshape(four), do1.reshape(three), o1_raw, states, lbl4,
                                               hg_o_gain, dproj1.reshape(four),
                                               _Rider([g_who.reshape(rows8)], scatter=True))
    dproj1 = dproj1.reshape(4, t, D_MODEL)
    (dh1, g_hgn), _ = inproj_bwd_dx(dproj1, whi, h1, hg_norm_full, dh2, "hg_inproj_bwd_dx")
    g_whi = inproj_bwd_dw(u1, dproj1, "hg_inproj_bwd_dw", out_dtype=BF16)

    do0, dproj0, g_wso = outproj_bwd(dh1, wso, o0, proj0, "sb_outproj_bwd")
    (dproj0, g_qg, g_kg), (p_whi, p_wso) = sb_attn_bwd(qkv0, proj0.reshape(four), do0.reshape(three), o0.reshape(three),
                                                       sb_q_gain, sb_k_gain, dproj0.reshape(four),
                                                       _Rider([g_whi, g_wso.reshape(rows8)], scatter=True))
    dproj0 = dproj0.reshape(4, t, D_MODEL)
    g_wsi = inproj_bwd_dw(u0, dproj0, "sb_inproj_bwd_dw", out_dtype=BF16)
    (gx, g_sbn), (p_wsi,) = inproj_bwd_dx(dproj0, wsi, x2, sb_norm, dh1, "sb_inproj_bwd_dx",
                                          _Rider([sum_within_chip(g_wsi)], scatter=True, chips=True))
    return dict(loss_parts=loss_parts, gx=gx.reshape(three), p_wsi=p_wsi, p_wso=p_wso, p_whi=p_whi, p_who=p_who,
                g_sbn=g_sbn, g_hgn=g_hgn, g_qg=g_qg, g_kg=g_kg, g_og=g_og, g_lb=g_lb)


def sum_within_chip(g):
    _, r, c_ = g.shape
    chips = N_DEV // 2

    def swap(g_ref, got_ref, send_sems, recv_sems):
        x, y, c = _mesh_pos()
        copies = [pltpu.make_async_remote_copy(
            src_ref=g_ref.at[2 * q + 1 - c], dst_ref=got_ref.at[q], send_sem=send_sems.at[q], recv_sem=recv_sems.at[q],
            device_id=(x, y, 1 - c), device_id_type=MESH) for q in range(chips)]
        for cp in copies:
            cp.start()
        for cp in copies:
            cp.wait_recv()
            cp.wait_send()

    got = _pcall(swap, name="swap_with_sibling", in_specs=[_ANY], out_specs=_ANY,
                 out_shape=jax.ShapeDtypeStruct((chips, r, c_), g.dtype),
                 scratch_shapes=[pltpu.SemaphoreType.DMA((chips,)), pltpu.SemaphoreType.DMA((chips,))])(g)

    def add(g_ref, got_ref, out_ref):
        mine = g_ref[0, lax.axis_index("c")]
        out_ref[0] = (mine.astype(F32) + got_ref[0].astype(F32)).astype(out_ref.dtype)

    return _pcall(
        add, name="add_sibling_partials", grid=(chips,),
        in_specs=[pl.BlockSpec((1, 2, r, c_), lambda q: (q, 0, 0, 0)), pl.BlockSpec((1, r, c_), lambda q: (q, 0, 0))],
        out_specs=pl.BlockSpec((1, r, c_), lambda q: (q, 0, 0)),
        out_shape=jax.ShapeDtypeStruct((chips, r, c_), g.dtype),
        compiler_params=_params("parallel"),
    )(g.reshape(chips, 2, r, c_), got)


def _two_level_gather(src, out, send_sems, recv_sems, local_sem, pos, meanwhile):
    x, y, c = pos
    me, sibling = (x, y, c), (x, y, 1 - c)
    chips = [(1 - x, y), (x, 1 - y), (1 - x, 1 - y)]

    def copy(k, block, to, source=None):
        slot = out.at[_linear(block)]
        return pltpu.make_async_remote_copy(
            src_ref=slot if source is None else source, dst_ref=slot, send_sem=send_sems.at[k],
            recv_sem=recv_sems.at[k], device_id=to, device_id_type=MESH)

    mine = pltpu.make_async_copy(src, out.at[_linear(me)], local_sem)
    mine.start()
    first = [copy(0, me, sibling, src)] + [copy(1 + j, me, (*chip, c), src) for j, chip in enumerate(chips)]
    for cp in first:
        cp.start()
    meanwhile()
    passed = [copy(4 + j, (*chip, c), sibling) for j, chip in enumerate(chips)]
    for j, chip in enumerate(chips):
        copy(1 + j, (*chip, c), me).wait_recv()
        passed[j].start()
    copy(0, sibling, me).wait_recv()
    for j, chip in enumerate(chips):
        copy(4 + j, (*chip, 1 - c), me).wait_recv()
    for cp in first + passed:
        cp.wait_send()
    mine.wait()


def gather_first_weights(w_si, w_so, w_hi, w_ho, hg_norm):
    def body(si_ref, so_ref, hi_ref, ho_ref, hn_ref, o_si, so_b, hi_b, ho_b, hn_b, si_b, send_sems, recv_sems, local_sem):
        si_b[...] = si_ref[...].astype(BF16)

        def cast_the_rest():
            for src, buf in ((so_ref, so_b), (hi_ref, hi_b), (ho_ref, ho_b)):
                buf[...] = src[...].astype(BF16)
            hn_b[...] = jnp.broadcast_to(hn_ref[...], hn_b.shape)

        _two_level_gather(si_b, o_si, send_sems, recv_sems, local_sem, _mesh_pos(), cast_the_rest)

    return _pcall(
        body, name="gather_first_weights",
        in_specs=[_VMEM] * 5, out_specs=[_ANY] + [_VMEM] * 4,
        out_shape=[jax.ShapeDtypeStruct((N_DEV,) + w_si.shape, BF16), jax.ShapeDtypeStruct(w_so.shape, BF16),
                   jax.ShapeDtypeStruct(w_hi.shape, BF16), jax.ShapeDtypeStruct(w_ho.shape, BF16),
                   jax.ShapeDtypeStruct((8, HEAD_DIM), F32)],
        scratch_shapes=[pltpu.VMEM(w_si.shape, BF16), pltpu.SemaphoreType.DMA((N_PEERS,)),
                        pltpu.SemaphoreType.DMA((N_PEERS,)), pltpu.SemaphoreType.DMA],
        compiler_params=pltpu.CompilerParams(vmem_limit_bytes=VMEM_LIMIT_BYTES),
    )(w_si, w_so, w_hi, w_ho, hg_norm)


def _adamw(w, g, m, v):
    m = ADAM_B1 * m + (1.0 - ADAM_B1) * g
    v = ADAM_B2 * v + (1.0 - ADAM_B2) * (g * g)
    m_hat = m / (1.0 - ADAM_B1 ** ADAM_STEP)
    v_hat = v / (1.0 - ADAM_B2 ** ADAM_STEP)
    delta = -ADAM_LR * (m_hat / (jnp.sqrt(v_hat) + ADAM_EPS) + ADAM_WD * w)
    return delta, m, v


def reduce_adamw(parts, w, m, v, name):
    n, r, c = parts.shape
    tr = _row_tile(r, 256)

    def body(p_ref, w_ref, m_ref, v_ref, g_ref, d_ref, m2_ref, v2_ref):
        g = p_ref[0].astype(F32)
        for sender in range(1, n):
            g = g + p_ref[sender].astype(F32)
        g_ref[0] = g
        d_ref[0], m2_ref[0], v2_ref[0] = _adamw(w_ref[0], g, m_ref[0], v_ref[0])

    tile = pl.BlockSpec((1, tr, c), lambda i: (0, i, 0))
    return _pcall(
        body, name=name, grid=(r // tr,),
        in_specs=[pl.BlockSpec((n, tr, c), lambda i: (0, i, 0)), tile, tile, tile],
        out_specs=[tile] * 4, out_shape=[jax.ShapeDtypeStruct((1, r, c), F32)] * 4,
        compiler_params=_params("parallel"),
    )(parts, w, m, v)


PACK_ROWS = 32
ROW_SBN, ROW_HGN, ROW_LB, ROW_QG, ROW_KG, ROW_OG, ROW_LOSS = 0, 8, 16, 24, 25, 26, 27


def small_update(g_sbn, g_hgn, g_lb, g_qg, g_kg, g_og, loss_parts, small):
    def gather(sbn_ref, hgn_ref, lb_ref, qg_ref, kg_ref, og_ref, loss_ref, gath, pack, send_sems, recv_sems, local_sems):
        pos = _mesh_pos()
        pack[...] = jnp.zeros_like(pack)
        pack[ROW_SBN:ROW_SBN + 8, :] = sbn_ref[...]
        pack[ROW_HGN:ROW_HGN + 8, :] = hgn_ref[...]
        pack[ROW_LB:ROW_LB + 8, :] = jnp.sum(lb_ref[...], axis=0)
        pack[ROW_QG:ROW_QG + 1, :] = jnp.sum(qg_ref[...], axis=0, keepdims=True)
        pack[ROW_KG:ROW_KG + 1, :] = jnp.sum(kg_ref[...], axis=0, keepdims=True)
        pack[ROW_OG:ROW_OG + 1, :] = jnp.sum(og_ref[...], axis=0, keepdims=True)
        pack[ROW_LOSS:ROW_LOSS + 1, :] = jnp.sum(loss_ref[...], axis=0)[0:1, :]
        _exchange_start([((lambda p: pack), gath)], send_sems, recv_sems, local_sems, pos)
        _exchange_wait([((lambda p: pack), gath)], send_sems, recv_sems, local_sems, pos)

    packs = _pcall(
        gather, name="small_gather",
        in_specs=[_VMEM] * 7, out_specs=_VMEM, out_shape=jax.ShapeDtypeStruct((N_DEV, PACK_ROWS, HEAD_DIM), F32),
        scratch_shapes=[pltpu.VMEM((PACK_ROWS, HEAD_DIM), F32)] + _exchange_sems(1),
    )(g_sbn, g_hgn, g_lb, g_qg, g_kg, g_og, loss_parts)

    def apply(gath, *refs):
        wmv, outs, tot = refs[:len(small)], refs[len(small):-1], refs[-1]
        me = _linear(_mesh_pos())
        total = gath[0]
        for dev in range(1, N_DEV):
            total = total + gath[dev]
        tot[...] = total
        outs[0][...] = jnp.broadcast_to(tot[ROW_LOSS:ROW_LOSS + 1, :], (8, HEAD_DIM))
        l0 = wmv[15][0:8, :]
        l1 = wmv[15][8:16, :]
        p1, p0 = _sigmoid_pair(l1 - l0)
        d_l1 = p0 * p1 * tot[ROW_LB:ROW_LB + 8, :]
        grads = [tot[ROW_SBN:ROW_SBN + 8, :], tot[ROW_QG:ROW_QG + 1, :], tot[ROW_KG:ROW_KG + 1, :],
                 tot[pl.ds(ROW_HGN + me, 1), :], tot[ROW_OG:ROW_OG + 1, :],
                 jnp.concatenate([-d_l1, d_l1], axis=0)]
        for i, g in enumerate(grads):
            w_ref, m_ref, v_ref = wmv[3 * i:3 * i + 3]
            o = outs[1 + 4 * i:5 + 4 * i]
            o[0][...] = g
            o[1][...], o[2][...], o[3][...] = _adamw(w_ref[...], g, m_ref[...], v_ref[...])

    out_shape = [jax.ShapeDtypeStruct((8, HEAD_DIM), F32)]
    for i in range(6):
        out_shape += [jax.ShapeDtypeStruct(small[3 * i].shape, F32)] * 4
    return _pcall(
        apply, name="small_update",
        in_specs=[_VMEM] * (1 + len(small)), out_specs=[_VMEM] * 25, out_shape=out_shape,
        scratch_shapes=[pltpu.VMEM((PACK_ROWS, HEAD_DIM), F32)],
    )(packs, *small)


def kernel(x, sb_norm, sb_w_in, sb_q_gain, sb_k_gain, sb_w_out, hg_norm, hg_w_in, hg_o_gain, hg_w_out, hg_lb_logits, loss_target, m_sb_norm, m_sb_w_in, m_sb_q_gain, m_sb_k_gain, m_sb_w_out, m_hg_norm, m_hg_w_in, m_hg_o_gain, m_hg_w_out, m_hg_lb_logits, v_sb_norm, v_sb_w_in, v_sb_q_gain, v_sb_k_gain, v_sb_w_out, v_hg_norm, v_hg_w_in, v_hg_o_gain, v_hg_w_out, v_hg_lb_logits):
    b = x.shape[0]
    wsi, wso_mine, whi_mine, who_mine, hg_norm_mine = gather_first_weights(
        sb_w_in[0], sb_w_out[0], hg_w_in[0], hg_w_out[0], hg_norm)
    r = local_step(x, loss_target, sb_norm, wsi, sb_q_gain, sb_k_gain, hg_o_gain, hg_lb_logits,
                   wso_mine, whi_mine, who_mine, hg_norm_mine)
    big = {}
    for name, p, w, m, v in (("sb_w_in", r["p_wsi"], sb_w_in, m_sb_w_in, v_sb_w_in),
                             ("sb_w_out", r["p_wso"], sb_w_out, m_sb_w_out, v_sb_w_out),
                             ("hg_w_in", r["p_whi"], hg_w_in, m_hg_w_in, v_hg_w_in),
                             ("hg_w_out", r["p_who"], hg_w_out, m_hg_w_out, v_hg_w_out)):
        big[name] = reduce_adamw(p, w, m, v, "adamw_" + name)

    def rows8(a):
        return a.reshape(8, HEAD_DIM)

    def rows16(a):
        return a.reshape(16, HEAD_DIM)

    small_in = [rows8(sb_norm), rows8(m_sb_norm), rows8(v_sb_norm),
                sb_q_gain, m_sb_q_gain, v_sb_q_gain,
                sb_k_gain, m_sb_k_gain, v_sb_k_gain,
                hg_norm, m_hg_norm, v_hg_norm,
                hg_o_gain, m_hg_o_gain, v_hg_o_gain,
                rows16(hg_lb_logits), rows16(m_hg_lb_logits), rows16(v_hg_lb_logits)]
    so = small_update(rows8(r["g_sbn"]), rows8(r["g_hgn"]), r["g_lb"].reshape(b, N_HEADS, HEAD_DIM),
                      r["g_qg"].reshape(b * N_HEADS, HEAD_DIM), r["g_kg"].reshape(b * N_HEADS, HEAD_DIM),
                      r["g_og"].reshape(b * N_HEADS, HEAD_DIM), r["loss_parts"], small_in)
    loss = so[0][0, 0]
    shapes = {"sb_norm": (1, D_MODEL), "sb_q_gain": (1, HEAD_DIM), "sb_k_gain": (1, HEAD_DIM),
              "hg_norm": (1, HEAD_DIM), "hg_o_gain": (1, HEAD_DIM), "hg_lb_logits": (2, D_MODEL)}
    small = {}
    for i, name in enumerate(("sb_norm", "sb_q_gain", "sb_k_gain", "hg_norm", "hg_o_gain", "hg_lb_logits")):
        small[name] = [o.reshape(shapes[name]) for o in so[1 + 4 * i:5 + 4 * i]]
    order = ("sb_norm", "sb_w_in", "sb_q_gain", "sb_k_gain", "sb_w_out",
             "hg_norm", "hg_w_in", "hg_o_gain", "hg_w_out", "hg_lb_logits")
    res = {**big, **small}
    return (loss, r["gx"]) + tuple(res[n][j] for j in range(4) for n in order)
```

```python
import functools
import math

import jax
import jax.numpy as jnp
from jax import lax
from jax.experimental import pallas as pl
from jax.experimental.pallas import tpu as pltpu

F32 = jnp.float32
BF16 = jnp.bfloat16

N_DEV = 8
D_MODEL = 1024
N_HEADS = 8
HEAD_DIM = 128
RMS_EPS = 1e-6
ATTN_BLOCK = 128
HG_CHUNK = 64
HG_SUB = 16
HG_UNROLL_FWD = 16
HG_UNROLL_BWD = 16
EXP_CLAMP = 80.0
SB_HEADS_PER_STEP = 2
SB_QBLOCKS_PER_STEP = 4
SB_GROUP_COLS = SB_HEADS_PER_STEP * 128
SB_TOP_ROWS = 32
SB_LOG_WEIGHT_FLOOR = -104.0
VMEM_LIMIT_BYTES = 48 * 1024 * 1024
W_COLS = 4 * D_MODEL // N_DEV
W_ROWS = D_MODEL // N_DEV

ADAM_LR = 0.001
ADAM_B1 = 0.9
ADAM_B2 = 0.999
ADAM_EPS = 1e-08
ADAM_WD = 0.01
ADAM_STEP = 10

NT = (((1,), (1,)), ((), ()))
TN = (((0,), (0,)), ((), ()))
NN = (((1,), (0,)), ((), ()))


def _pcall(body, *, name, **kw):
    return pl.pallas_call(body, name=name, **kw)


def _params(*sem):
    return pltpu.CompilerParams(dimension_semantics=sem, vmem_limit_bytes=VMEM_LIMIT_BYTES)


def _dot(a, b, dims=NN):
    return lax.dot_general(a, b, dims, preferred_element_type=F32)


def _dot_exact(a, m, dims=NN, left=False):
    hi = a.astype(BF16)
    lo = (a - hi.astype(F32)).astype(BF16)
    if left:
        return _dot(m, hi, dims) + _dot(m, lo, dims)
    return _dot(hi, m, dims) + _dot(lo, m, dims)


def _sigmoid(x):
    return 1.0 / (1.0 + jnp.exp(-x))


def _sigmoid_pair(x):
    e = jnp.exp(-jnp.abs(x))
    big = 1.0 / (1.0 + e)
    small = e * big
    pos = x >= 0
    return jnp.where(pos, big, small), jnp.where(pos, small, big)


def _rms_scale(x):
    return lax.rsqrt(jnp.mean(x * x, axis=-1, keepdims=True) + RMS_EPS)


def _row_tile(t, want):
    return want if t % want == 0 else t


MESH = pl.DeviceIdType.MESH
N_PEERS = N_DEV - 1
_ANY = pl.BlockSpec(memory_space=pl.ANY)
_VMEM = pl.BlockSpec(memory_space=pltpu.VMEM)


def _mesh_pos():
    return lax.axis_index("x"), lax.axis_index("y"), lax.axis_index("c")


def _linear(pos):
    return 4 * pos[0] + 2 * pos[1] + pos[2]


def _peer(pos, k):
    flips = ((k + 1) >> 2 & 1, (k + 1) >> 1 & 1, (k + 1) & 1)
    return tuple(1 - p if f else p for p, f in zip(pos, flips))


def _chip(pos):
    return 2 * pos[0] + pos[1]


ALL_PEERS = tuple(range(N_PEERS))
SAME_CORE_PEERS = (1, 3, 5)


def _exchange_copies(pairs, send_sems, recv_sems, local_sems, pos, landing, peers, slot):
    me = slot(pos)
    local, remote = [], []
    for a, (src_of, dst) in enumerate(pairs):
        local.append(pltpu.make_async_copy(src_of(pos), dst.at[me], local_sems.at[a]))
        for k in peers:
            peer = _peer(pos, k)
            remote.append(pltpu.make_async_remote_copy(
                src_ref=src_of(peer), dst_ref=dst.at[slot(peer) if landing else me],
                send_sem=send_sems.at[a, k], recv_sem=recv_sems.at[a, k], device_id=peer, device_id_type=MESH))
    return local, remote


def _exchange_start(pairs, send_sems, recv_sems, local_sems, pos, peers=ALL_PEERS, slot=_linear):
    local, sent = _exchange_copies(pairs, send_sems, recv_sems, local_sems, pos, False, peers, slot)
    for copy in local + sent:
        copy.start()


def _exchange_wait(pairs, send_sems, recv_sems, local_sems, pos, peers=ALL_PEERS, slot=_linear):
    local, landed = _exchange_copies(pairs, send_sems, recv_sems, local_sems, pos, True, peers, slot)
    for copy in landed:
        copy.wait_recv()
        copy.wait_send()
    for copy in local:
        copy.wait()


def _exchange_sems(n):
    return [pltpu.SemaphoreType.DMA((n, N_PEERS)), pltpu.SemaphoreType.DMA((n, N_PEERS)),
            pltpu.SemaphoreType.DMA((n,))]


class _Rider:
    def __init__(self, arrays, scatter, chips=False):
        self.arrays = list(arrays)
        self.scatter = scatter
        self.peers = SAME_CORE_PEERS if chips else ALL_PEERS
        self.slot = _chip if chips else _linear
        self.out_shapes = [jax.ShapeDtypeStruct(a.shape if scatter else (N_DEV,) + a.shape, a.dtype)
                           for a in self.arrays]

    def pairs(self, in_refs, out_refs):
        if self.scatter:
            return [((lambda pos, r=r: r.at[self.slot(pos)]), o) for r, o in zip(in_refs, out_refs)]
        return [((lambda pos, r=r: r), o) for r, o in zip(in_refs, out_refs)]


def _pcall_riding(body, rider, args, *, name, grid, in_specs, out_specs, out_shape, semantics, scratch_shapes=(),
                  input_output_aliases=None):
    aliases = input_output_aliases or {}
    if rider is None:
        outs = _pcall(body, name=name, grid=grid, in_specs=list(in_specs), out_specs=list(out_specs),
                      out_shape=list(out_shape), scratch_shapes=list(scratch_shapes), input_output_aliases=aliases,
                      compiler_params=_params(*semantics))(*args)
        return list(outs), []
    n_in, n_out, n_scr, n_r = len(in_specs), len(out_specs), len(scratch_shapes), len(rider.arrays)

    def riding(*refs):
        ins, refs = refs[:n_in], refs[n_in:]
        rider_in, refs = refs[:n_r], refs[n_r:]
        outs, refs = refs[:n_out], refs[n_out:]
        rider_out, refs = refs[:n_r], refs[n_r:]
        scratch, sems = refs[:n_scr], refs[n_scr:]
        pairs = rider.pairs(rider_in, rider_out)
        first = functools.reduce(jnp.logical_and, [pl.program_id(a) == 0 for a in range(len(grid))])
        last = functools.reduce(jnp.logical_and, [pl.program_id(a) == g - 1 for a, g in enumerate(grid)])

        @pl.when(first)
        def _():
            _exchange_start(pairs, *sems, _mesh_pos(), rider.peers, rider.slot)

        body(*ins, *outs, *scratch)

        @pl.when(last)
        def _():
            _exchange_wait(pairs, *sems, _mesh_pos(), rider.peers, rider.slot)

    outs = _pcall(riding, name=name, grid=grid, in_specs=list(in_specs) + [_ANY] * n_r,
                  out_specs=list(out_specs) + [_ANY] * n_r, out_shape=list(out_shape) + rider.out_shapes,
                  scratch_shapes=list(scratch_shapes) + _exchange_sems(n_r), input_output_aliases=aliases,
                  compiler_params=_params(*(("arbitrary",) * len(grid))))(*args, *rider.arrays)
    return list(outs[:n_out]), list(outs[n_out:])


def rms_inproj(x2, gain, wg, name, rider=None, qk_gains=None):
    t = x2.shape[0]
    with_qkv = qk_gains is not None
    tm = _row_tile(t, 256 if with_qkv else 512)

    def body(x_ref, g_ref, w_ref, *rest):
        if with_qkv:
            qg_ref, kg_ref, proj_ref, ut_ref, qkv_ref = rest
            head_gain = (qg_ref, kg_ref)
        else:
            proj_ref, ut_ref = rest
        x = x_ref[...]
        u = x * _rms_scale(x) * g_ref[...]
        ut_ref[...] = u.T.astype(BF16)
        u = u.astype(BF16)
        for p in range(N_DEV):
            part, lo = p // 2, (p % 2) * W_COLS
            res = _dot(u, w_ref[p])
            proj_ref[part, :, lo:lo + W_COLS] = res
            if with_qkv and part < 3:
                for h in range(W_COLS // HEAD_DIM):
                    y = res[:, h * HEAD_DIM:(h + 1) * HEAD_DIM]
                    if part < 2:
                        y = y * _rms_scale(y) * head_gain[part][...]
                    qkv_ref[part, :, lo + h * HEAD_DIM:lo + (h + 1) * HEAD_DIM] = y.astype(BF16)

    vec = pl.BlockSpec((1, D_MODEL), lambda i: (0, 0))
    in_specs = [pl.BlockSpec((tm, D_MODEL), lambda i: (i, 0)), vec,
                pl.BlockSpec((N_DEV, D_MODEL, W_COLS), lambda i: (0, 0, 0))]
    out_specs = [pl.BlockSpec((4, tm, D_MODEL), lambda i: (0, i, 0)), pl.BlockSpec((D_MODEL, tm), lambda i: (0, i))]
    out_shape = [jax.ShapeDtypeStruct((4, t, D_MODEL), F32), jax.ShapeDtypeStruct((D_MODEL, t), BF16)]
    args = (x2, gain, wg)
    if with_qkv:
        in_specs += [pl.BlockSpec((1, HEAD_DIM), lambda i: (0, 0))] * 2
        out_specs.append(pl.BlockSpec((3, tm, D_MODEL), lambda i: (0, i, 0)))
        out_shape.append(jax.ShapeDtypeStruct((3, t, D_MODEL), BF16))
        args += tuple(qk_gains)
    return _pcall_riding(body, rider, args, name=name, grid=(t // tm,), in_specs=in_specs, out_specs=out_specs,
                         out_shape=out_shape, semantics=("parallel",))


def gate_outproj(o2, proj, w_out, resid, target, name):
    t = o2.shape[0]
    tm = _row_tile(t, 512)
    with_loss = target is not None

    def body(o_ref, gate_ref, w_ref, r_ref, *rest):
        g = gate_ref[0]
        og = (o_ref[...] * (g * _sigmoid(g))).astype(BF16)
        h = r_ref[...] + _dot(og, w_ref[...])
        if with_loss:
            t_ref, dh_ref, loss_ref = rest
            err = h - t_ref[...]
            dh_ref[...] = err * (1.0 / D_MODEL)
            part = 0.5 * jnp.sum(jnp.mean(err * err, axis=-1, keepdims=True))
            loss_ref[...] = jnp.full(loss_ref.shape, part, F32)
        else:
            (h_ref,) = rest
            h_ref[...] = h

    row = pl.BlockSpec((tm, D_MODEL), lambda i: (i, 0))
    in_specs = [row,
                pl.BlockSpec((1, tm, D_MODEL), lambda i: (3, i, 0)),
                pl.BlockSpec((D_MODEL, D_MODEL), lambda i: (0, 0)),
                row]
    args = [o2, proj, w_out, resid]
    if with_loss:
        in_specs.append(row)
        args.append(target)
        out_specs = [row, pl.BlockSpec((1, 8, 128), lambda i: (i, 0, 0))]
        out_shape = [jax.ShapeDtypeStruct((t, D_MODEL), F32),
                     jax.ShapeDtypeStruct((t // tm, 8, 128), F32)]
    else:
        out_specs = row
        out_shape = jax.ShapeDtypeStruct((t, D_MODEL), F32)
    return _pcall(body, name=name, grid=(t // tm,), in_specs=in_specs, out_specs=out_specs,
                  out_shape=out_shape, compiler_params=_params("parallel"))(*args)


def _head_spec(s, part):
    return pl.BlockSpec((1, 1, s, HEAD_DIM), lambda b, h: (part, b, 0, h))


def _seq_spec(s):
    return pl.BlockSpec((1, s, HEAD_DIM), lambda b, h: (b, 0, h))


_GAIN_SPEC = pl.BlockSpec((1, HEAD_DIM), lambda b, h: (0, 0))
_HEAD_ROW_SPEC = pl.BlockSpec((1, 1, 1, HEAD_DIM), lambda b, h: (b, h, 0, 0))


def _sb_group_spec(s, part):
    return pl.BlockSpec((1, 1, s, SB_GROUP_COLS), lambda b, g: (part, b, 0, g))


def _sb_seq_group_spec(s):
    return pl.BlockSpec((1, s, SB_GROUP_COLS), lambda b, g: (b, 0, g))


_SB_GROUP_ROW_SPEC = pl.BlockSpec((1, SB_HEADS_PER_STEP, 1, HEAD_DIM), lambda b, g: (b, g, 0, 0))


def _sb_chains(m):
    return [(h, m * SB_QBLOCKS_PER_STEP + r) for h in range(SB_HEADS_PER_STEP) for r in range(SB_QBLOCKS_PER_STEP)]


def _sb_logits(qi, kj):
    return _dot(qi, kj, NT) * (HEAD_DIM ** -0.5)


def _sb_scores(z, diag, live, tri_lt):
    soft = jnp.log(1.0 + jnp.exp(-jnp.abs(z)))
    valid = jnp.logical_and(live, jnp.logical_or(jnp.logical_not(diag), tri_lt))
    log_skip = jnp.where(valid, -(jnp.maximum(z, 0.0) + soft), 0.0)
    log_beta = jnp.minimum(z, 0.0) - soft
    return log_skip, log_beta, valid


def _sb_keys_left(chains, watch, state):
    done, carries = state[0], state[1]
    worst = None
    for (_, i), c in zip(chains, carries):
        c = jnp.where(done <= i, c[watch], -jnp.inf)
        worst = c if worst is None else jnp.maximum(worst, c)
    return jnp.logical_and(done <= chains[-1][1],
                           jnp.logical_or(done == 0, jnp.max(worst) > SB_LOG_WEIGHT_FLOOR))


def _sb_key_rows(i, done):
    j = i - done
    return pl.ds(pl.multiple_of(jnp.maximum(j, 0) * ATTN_BLOCK, ATTN_BLOCK), ATTN_BLOCK), j >= 0


def _sb_head(ref, h, rows):
    return ref[0, 0, rows, h * HEAD_DIM:(h + 1) * HEAD_DIM]


def _sb_rows(i, offset, count):
    return pl.ds(pl.multiple_of(jnp.maximum(i, 0) * ATTN_BLOCK + offset, 8), count)


def sb_attn_fwd(qkv4, rider=None):
    _, b, s, _ = qkv4.shape
    blk, top = ATTN_BLOCK, SB_TOP_ROWS
    ngroups = s // blk // SB_QBLOCKS_PER_STEP
    assert ngroups * SB_QBLOCKS_PER_STEP * blk == s

    def body(q_ref, k_ref, v_ref, o_ref):
        row = lax.broadcasted_iota(jnp.int32, (blk, blk), 0)
        col = lax.broadcasted_iota(jnp.int32, (blk, blk), 1)
        tri_lt = col < row
        suffix = (row > col).astype(BF16)

        def step(items):
            where = [_sb_key_rows(i, done) for _, i, done, _, _, _, _ in items]
            zs = [_sb_logits(q, _sb_head(k_ref, h, rows)) for (h, _, _, q, _, _, _), (rows, _) in zip(items, where)]
            scored = [_sb_scores(z, done == 0, live, mask)
                      for z, (_, _, done, _, mask, _, _), (_, live) in zip(zs, items, where)]
            afters = [_dot_exact(log_skip, suffix) for log_skip, _, _ in scored]
            ws = [jnp.where(valid, jnp.exp(log_beta + after + c), 0.0).astype(BF16)
                  for (_, log_beta, valid), after, (_, _, _, _, _, c, _) in zip(scored, afters, items)]
            accs = [acc + _dot(w, _sb_head(v_ref, h, rows))
                    for (h, _, _, _, _, _, acc), (rows, _), w in zip(items, where, ws)]
            cs = [c + jnp.sum(log_skip, axis=1, keepdims=True)
                  for (log_skip, _, _), (_, _, _, _, _, c, _) in zip(scored, items)]
            return cs, accs

        def top_items(chains, done, cs, accs):
            return [(h, i, done, _sb_head(q_ref, h, _sb_rows(i, 0, top)), tri_lt[:top, :], c, acc)
                    for (h, i), c, acc in zip(chains, cs, accs)]

        def finish_tops(chains, state):
            def k_step(state):
                done, cs, accs = state
                cs, accs = step(top_items(chains, done, cs, accs))
                return done + 1, tuple(cs), tuple(accs)

            _, _, accs = lax.while_loop(functools.partial(_sb_keys_left, chains, slice(0, top)), k_step, state)
            for (h, i), acc in zip(chains, accs):
                o_ref[0, _sb_rows(i, 0, top), h * HEAD_DIM:(h + 1) * HEAD_DIM] = acc

        def q_group(m, before):
            chains, chains_before = _sb_chains(m), _sb_chains(m - 1)
            qis = [_sb_head(q_ref, h, _sb_rows(i, 0, blk)) for h, i in chains]
            n = len(chains)

            def items_of(done, cs, accs):
                return [(h, i, done, q, tri_lt, c, acc) for (h, i), q, c, acc in zip(chains, qis, cs, accs)]

            def k_step(state):
                done, cs, accs = state
                cs, accs = step(items_of(done, cs, accs))
                return done + 1, tuple(cs), tuple(accs)

            done_b, cs_b, accs_b = before
            cs0, accs0 = step(items_of(jnp.int32(0), (jnp.zeros((blk, 1), F32),) * n,
                                       (jnp.zeros((blk, HEAD_DIM), F32),) * n)
                              + top_items(chains_before, done_b, cs_b, accs_b))
            done, cs, accs = lax.while_loop(functools.partial(_sb_keys_left, chains, slice(top, blk)), k_step,
                                            (jnp.int32(1), tuple(cs0[:n]), tuple(accs0[:n])))

            @pl.when(m > 0)
            def _():
                finish_tops(chains_before, (done_b + 1, tuple(cs0[n:]), tuple(accs0[n:])))

            for (h, i), acc in zip(chains, accs):
                o_ref[0, _sb_rows(i, top, blk - top), h * HEAD_DIM:(h + 1) * HEAD_DIM] = acc[top:, :]
            return done, tuple(c[:top, :] for c in cs), tuple(acc[:top, :] for acc in accs)

        n = SB_HEADS_PER_STEP * SB_QBLOCKS_PER_STEP
        nothing = (jnp.int32(0), (jnp.zeros((top, 1), F32),) * n, (jnp.zeros((top, HEAD_DIM), F32),) * n)
        last = lax.fori_loop(0, ngroups, q_group, nothing)
        finish_tops(_sb_chains(ngroups - 1), last)

    (o,), extra = _pcall_riding(
        body, rider, (qkv4, qkv4, qkv4),
        name="sb_attn_fwd", grid=(b, N_HEADS // SB_HEADS_PER_STEP),
        in_specs=[_sb_group_spec(s, 0), _sb_group_spec(s, 1), _sb_group_spec(s, 2)],
        out_specs=[_sb_seq_group_spec(s)],
        out_shape=[jax.ShapeDtypeStruct((b, s, D_MODEL), F32)],
        semantics=("parallel", "parallel"))
    return o, extra


def _hg_masks():
    c = HG_CHUNK
    row = lax.broadcasted_iota(jnp.int32, (c, c), 0)
    col = lax.broadcasted_iota(jnp.int32, (c, c), 1)
    incl = (col <= row)
    lower = incl.astype(BF16)
    before_sub = (col < (row // HG_SUB) * HG_SUB).astype(BF16)
    upper = (col >= row).astype(BF16)
    return incl, lower, before_sub, upper


def _hg_lower_bound(lbl_ref):
    l0 = lbl_ref[0, 0]
    l1 = lbl_ref[1, 0]
    d = l1 - l0
    return _sigmoid_pair(d)


def _hg_gates(qp, fp, lb, oml):
    sq = _sigmoid(qp)
    sf, sfn = _sigmoid_pair(fp)
    f = lb + oml * sf
    return dict(qp=qp, sq=sq, q=qp * sq, sf=sf, sfn=sfn, f=f, k=oml * sfn, logf=jnp.log(f))


def _hg_intra(qds, ks, gcs, grs, incl):
    subs = range(HG_CHUNK // HG_SUB)
    qdbs = [qd.astype(BF16) for qd in qds]
    ess = [[jnp.exp(jnp.minimum(gr[sub * HG_SUB:sub * HG_SUB + 1, :] - gc, EXP_CLAMP)) for sub in subs]
           for gc, gr in zip(gcs, grs)]
    ksbs = [[(k * e).astype(BF16) for e in es] for k, es in zip(ks, ess)]
    rows = [[_dot(qdb[sub * HG_SUB:(sub + 1) * HG_SUB, :], ksb[sub], NT) for sub in subs]
            for qdb, ksb in zip(qdbs, ksbs)]
    a_s = [jnp.where(incl, jnp.concatenate(r, axis=0), 0.0) for r in rows]
    return a_s, qdbs, ksbs, ess


def _hg_group_rows(outer, unroll):
    ns = [outer * unroll + u for u in range(unroll)]
    return ns, [pl.ds(pl.multiple_of(n * HG_CHUNK, HG_CHUNK), HG_CHUNK) for n in ns]


def _state_spec(nchunk):
    return pl.BlockSpec((1, 1, nchunk, HEAD_DIM, HEAD_DIM), lambda b, h: (b, h, 0, 0, 0))


def hgrn2_fwd(proj4, lbl4, o_gain, rider=None):
    _, b, s, _ = proj4.shape
    nchunk = s // HG_CHUNK
    c = HG_CHUNK
    unroll = math.gcd(nchunk, HG_UNROLL_FWD)

    def body(q_ref, f_ref, i_ref, lbl_ref, og_ref, o_ref, oraw_ref, st_ref):
        incl, lower, before_sub, _ = _hg_masks()
        lb, oml = _hg_lower_bound(lbl_ref)

        def group(outer, st):
            ns, rows = _hg_group_rows(outer, unroll)
            vs = [_hg_gates(q_ref[0, 0, r, :], f_ref[0, 0, r, :], lb, oml) for r in rows]
            inps = [i_ref[0, 0, r, :].astype(BF16) for r in rows]
            gcs = [_dot_exact(v["logf"], lower, left=True) for v in vs]
            grs = [_dot_exact(v["logf"], before_sub, left=True) for v in vs]
            a_s, _, _, _ = _hg_intra([v["q"] * jnp.exp(gc - gr) for v, gc, gr in zip(vs, gcs, grs)],
                                     [v["k"] for v in vs], gcs, grs, incl)
            gls = [gc[c - 1:c, :] for gc in gcs]
            adds = [_dot(inp, (v["k"] * jnp.exp(gl - gc)).astype(BF16), TN)
                    for inp, v, gl, gc in zip(inps, vs, gls, gcs)]
            o_intra = [_dot(a.astype(BF16), inp) for a, inp in zip(a_s, inps)]
            sts = []
            for gl, add in zip(gls, adds):
                sts.append(st)
                st = st * jnp.exp(gl) + add
            outs = [oi + _dot((v["q"] * jnp.exp(gc)).astype(BF16), s0.astype(BF16), NT)
                    for oi, v, gc, s0 in zip(o_intra, vs, gcs, sts)]
            for n, r, s0, o in zip(ns, rows, sts, outs):
                st_ref[0, 0, n] = s0
                oraw_ref[0, r, :] = o
                o_ref[0, r, :] = o * _rms_scale(o) * og_ref[...]
            return st

        lax.fori_loop(0, nchunk // unroll, group, jnp.zeros((HEAD_DIM, HEAD_DIM), F32))

    seq = jax.ShapeDtypeStruct((b, s, D_MODEL), F32)
    return _pcall_riding(
        body, rider, (proj4, proj4, proj4, lbl4, o_gain), name="hgrn2_fwd", grid=(b, N_HEADS),
        in_specs=[_head_spec(s, 0), _head_spec(s, 1), _head_spec(s, 2),
                  pl.BlockSpec((2, 1, 1, HEAD_DIM), lambda b, h: (0, h, 0, 0)), _GAIN_SPEC],
        out_specs=[_seq_spec(s), _seq_spec(s), _state_spec(nchunk)],
        out_shape=[seq, seq, jax.ShapeDtypeStruct((b, N_HEADS, nchunk, HEAD_DIM, HEAD_DIM), F32)],
        semantics=("parallel", "parallel"))


def outproj_bwd(dh, w_out, o2, proj, name):
    t = dh.shape[0]
    tm = _row_tile(t, 512)

    def body(dh_ref, w_ref, o_ref, gate_ref, do_ref, dproj_ref, dw_ref):
        dhb = dh_ref[...].astype(BF16)
        dog = _dot(dhb, w_ref[...], NT)
        g = gate_ref[0]
        sg = _sigmoid(g)
        silu = g * sg
        o = o_ref[...]
        do_ref[...] = dog * silu
        dproj_ref[0] = dog * o * (sg * (1.0 + g * (1.0 - sg)))
        part = _dot((o * silu).astype(BF16), dhb, TN)

        @pl.when(pl.program_id(0) == 0)
        def _():
            dw_ref[...] = part

        @pl.when(pl.program_id(0) > 0)
        def _():
            dw_ref[...] += part

    row = pl.BlockSpec((tm, D_MODEL), lambda i: (i, 0))
    full = pl.BlockSpec((D_MODEL, D_MODEL), lambda i: (0, 0))
    return _pcall(
        body, name=name, grid=(t // tm,),
        in_specs=[row, full, row, pl.BlockSpec((1, tm, D_MODEL), lambda i: (3, i, 0))],
        out_specs=[row, pl.BlockSpec((1, tm, D_MODEL), lambda i: (3, i, 0)), full],
        out_shape=[jax.ShapeDtypeStruct((t, D_MODEL), F32),
                   jax.ShapeDtypeStruct((4, t, D_MODEL), F32),
                   jax.ShapeDtypeStruct((D_MODEL, D_MODEL), F32)],
        compiler_params=_params("arbitrary"),
    )(dh, w_out, o2, proj)


def inproj_bwd_dx(dproj, wg, x2, gain, dres, name, rider=None):
    t = x2.shape[0]
    tm = _row_tile(t, 512)

    def body(d_ref, w_ref, x_ref, g_ref, r_ref, dx_ref, dg_ref):
        du = jnp.zeros((tm, D_MODEL), F32)
        for p in range(N_DEV):
            cols = slice((p % 2) * W_COLS, (p % 2 + 1) * W_COLS)
            du = du + _dot(d_ref[p // 2, :, cols].astype(BF16), w_ref[p], NT)
        x = x_ref[...]
        r = _rms_scale(x)
        xh = x * r
        a = du * g_ref[...]
        dx_ref[...] = r_ref[...] + r * (a - xh * jnp.mean(a * xh, axis=-1, keepdims=True))
        part = jnp.sum(du * xh, axis=0, keepdims=True)

        @pl.when(pl.program_id(0) == 0)
        def _():
            dg_ref[...] = part

        @pl.when(pl.program_id(0) > 0)
        def _():
            dg_ref[...] += part

    row = pl.BlockSpec((tm, D_MODEL), lambda i: (i, 0))
    vec = pl.BlockSpec((1, D_MODEL), lambda i: (0, 0))
    return _pcall_riding(
        body, rider, (dproj, wg, x2, gain, dres), name=name, grid=(t // tm,),
        in_specs=[pl.BlockSpec((4, tm, D_MODEL), lambda i: (0, i, 0)),
                  pl.BlockSpec((N_DEV, D_MODEL, W_COLS), lambda i: (0, 0, 0)),
                  row, vec, row],
        out_specs=[row, vec],
        out_shape=[jax.ShapeDtypeStruct((t, D_MODEL), F32), jax.ShapeDtypeStruct((1, D_MODEL), F32)],
        semantics=("arbitrary",))


def inproj_bwd_dw(ut, dproj, name, out_dtype):
    t = ut.shape[1]

    def body(ut_ref, d_ref, dw_ref):
        dw_ref[0] = _dot(ut_ref[...], d_ref[0].astype(BF16)).astype(dw_ref.dtype)

    return _pcall(
        body, name=name, grid=(N_DEV,),
        in_specs=[pl.BlockSpec((D_MODEL, t), lambda j: (0, 0)),
                  pl.BlockSpec((1, t, W_COLS), lambda j: (j // 2, 0, j % 2))],
        out_specs=pl.BlockSpec((1, D_MODEL, W_COLS), lambda j: (j, 0, 0)),
        out_shape=jax.ShapeDtypeStruct((N_DEV, D_MODEL, W_COLS), out_dtype),
        compiler_params=_params("parallel"),
    )(ut, dproj)


def _rms_bwd(x, gain, dy):
    r = _rms_scale(x)
    xh = x * r
    a = dy * gain
    return r * (a - xh * jnp.mean(a * xh, axis=-1, keepdims=True)), dy * xh


def sb_attn_bwd(qkv4, proj4, do3, o3, q_gain, k_gain, dproj4, rider=None):
    _, b, s, _ = proj4.shape
    blk, top = ATTN_BLOCK, SB_TOP_ROWS
    nq = s // blk
    ngroups = nq // SB_QBLOCKS_PER_STEP
    assert ngroups * SB_QBLOCKS_PER_STEP * blk == s
    scale = HEAD_DIM ** -0.5

    def body(qn_ref, kn_ref, v_ref, q_ref, k_ref, do_ref, o_ref, qg_ref, kg_ref, _alias, d_ref, dqg_ref, dkg_ref, dob):
        for h in range(SB_HEADS_PER_STEP):
            dob[h] = do_ref[0, :, h * HEAD_DIM:(h + 1) * HEAD_DIM].astype(BF16)
        d_ref[...] = jnp.zeros_like(d_ref)
        row = lax.broadcasted_iota(jnp.int32, (blk, blk), 0)
        col = lax.broadcasted_iota(jnp.int32, (blk, blk), 1)
        tri_lt = col < row
        suffix = (row > col).astype(BF16)
        suffix_incl = (row >= col).astype(BF16)

        def step(items):
            heads = [it[0] for it in items]
            where = [_sb_key_rows(it[1], it[2]) for it in items]
            kjs = [_sb_head(kn_ref, h, rows) for h, (rows, _) in zip(heads, where)]
            zs = [_sb_logits(it[3], kj) for it, kj in zip(items, kjs)]
            dws = [_dot(it[4], _sb_head(v_ref, h, rows), NT) for it, h, (rows, _) in zip(items, heads, where)]
            scored = [_sb_scores(z, it[2] == 0, live, it[6]) for z, it, (_, live) in zip(zs, items, where)]
            afters = [_dot_exact(log_skip, suffix) for log_skip, _, _ in scored]
            wbs = [jnp.where(valid, jnp.exp(log_beta + after + it[7]), 0.0).astype(BF16)
                   for (_, log_beta, valid), after, it in zip(scored, afters, items)]
            gs = [dw * wb.astype(F32) for dw, wb in zip(dws, wbs)]
            befores = [it[5] - (_dot_exact(g, suffix_incl) + it[8]) for g, it in zip(gs, items)]
            dzbs = [jnp.where(valid, g - jnp.exp(log_beta) * (g + before), 0.0).astype(BF16)
                    for (_, log_beta, valid), g, before in zip(scored, gs, befores)]
            dqs = [it[9] + _dot(dzb, kj) for it, dzb, kj in zip(items, dzbs, kjs)]
            for it, h, (rows, _), wb, dzb in zip(items, heads, where, wbs, dzbs):
                cols = slice(h * HEAD_DIM, (h + 1) * HEAD_DIM)
                d_ref[2, 0, rows, cols] += _dot(wb, it[4], TN)
                d_ref[1, 0, rows, cols] += _dot(dzb, it[3], TN)
            cs = [it[7] + jnp.sum(log_skip, axis=1, keepdims=True) for (log_skip, _, _), it in zip(scored, items)]
            cgs = [it[8] + jnp.sum(g, axis=1, keepdims=True) for g, it in zip(gs, items)]
            return cs, cgs, dqs

        def top_items(chains, deltas, done, cs, cgs, dqs):
            return [(h, i, done, _sb_head(qn_ref, h, _sb_rows(i, 0, top)), dob[h, _sb_rows(i, 0, top), :], delta,
                     tri_lt[:top, :], c, cg, dq)
                    for (h, i), delta, c, cg, dq in zip(chains, deltas, cs, cgs, dqs)]

        def finish_tops(chains, deltas, state):
            def k_step(state):
                done, cs, cgs, dqs = state
                cs, cgs, dqs = step(top_items(chains, deltas, done, cs, cgs, dqs))
                return done + 1, tuple(cs), tuple(cgs), tuple(dqs)

            _, _, _, dqs = lax.while_loop(functools.partial(_sb_keys_left, chains, slice(0, top)), k_step, state)
            for (h, i), dq in zip(chains, dqs):
                d_ref[0, 0, _sb_rows(i, 0, top), h * HEAD_DIM:(h + 1) * HEAD_DIM] = dq * scale

        def q_group(m, before):
            chains, chains_before = _sb_chains(m), _sb_chains(m - 1)
            deltas_b, before = before[0], before[1:]
            qis, dois, deltas = [], [], []
            for h, i in chains:
                rows_i = _sb_rows(i, 0, blk)
                qis.append(_sb_head(qn_ref, h, rows_i))
                dois.append(dob[h, rows_i, :])
                deltas.append(jnp.sum(dois[-1].astype(F32) * o_ref[0, rows_i, h * HEAD_DIM:(h + 1) * HEAD_DIM],
                                      axis=1, keepdims=True))
            n = len(chains)

            def items_of(done, cs, cgs, dqs):
                return [(h, i, done, q, do, delta, tri_lt, c, cg, dq)
                        for (h, i), q, do, delta, c, cg, dq in zip(chains, qis, dois, deltas, cs, cgs, dqs)]

            def k_step(state):
                done, cs, cgs, dqs = state
                cs, cgs, dqs = step(items_of(done, cs, cgs, dqs))
                return done + 1, tuple(cs), tuple(cgs), tuple(dqs)

            done_b, cs_b, cgs_b, dqs_b = before
            zero = (jnp.zeros((blk, 1), F32),) * n
            new = step(items_of(jnp.int32(0), zero, zero, (jnp.zeros((blk, HEAD_DIM), F32),) * n)
                       + top_items(chains_before, deltas_b, done_b, cs_b, cgs_b, dqs_b))
            done, cs, cgs, dqs = lax.while_loop(functools.partial(_sb_keys_left, chains, slice(top, blk)), k_step,
                                                (jnp.int32(1),) + tuple(tuple(x[:n]) for x in new))

            @pl.when(m > 0)
            def _():
                finish_tops(chains_before, deltas_b, (done_b + 1,) + tuple(tuple(x[n:]) for x in new))

            for (h, i), dq in zip(chains, dqs):
                d_ref[0, 0, _sb_rows(i, top, blk - top), h * HEAD_DIM:(h + 1) * HEAD_DIM] = dq[top:, :] * scale
            first = lambda xs: tuple(x[:top, :] for x in xs)
            return first(deltas), done, first(cs), first(cgs), first(dqs)

        n = SB_HEADS_PER_STEP * SB_QBLOCKS_PER_STEP
        zero = (jnp.zeros((top, 1), F32),) * n
        nothing = (zero, jnp.int32(0), zero, zero, (jnp.zeros((top, HEAD_DIM), F32),) * n)
        last = lax.fori_loop(0, ngroups, q_group, nothing)
        finish_tops(_sb_chains(ngroups - 1), last[0], last[1:])

        def norm_block(i, carry):
            rows = pl.ds(pl.multiple_of(i * blk, blk), blk)
            out = []
            for h in range(SB_HEADS_PER_STEP):
                cols = slice(h * HEAD_DIM, (h + 1) * HEAD_DIM)
                for part, src_ref, gain_ref in ((0, q_ref, qg_ref), (1, k_ref, kg_ref)):
                    dy = d_ref[part, 0, rows, cols] * (scale if part == 1 else 1.0)
                    dx, pg = _rms_bwd(src_ref[0, 0, rows, cols], gain_ref[...], dy)
                    d_ref[part, 0, rows, cols] = dx
                    out.append(carry[len(out)] + jnp.sum(pg, axis=0, keepdims=True))
            return tuple(out)

        sums = lax.fori_loop(0, nq, norm_block, (jnp.zeros((1, HEAD_DIM), F32),) * (2 * SB_HEADS_PER_STEP))
        for h in range(SB_HEADS_PER_STEP):
            dqg_ref[0, h] = sums[2 * h]
            dkg_ref[0, h] = sums[2 * h + 1]

    head_row = jax.ShapeDtypeStruct((b, N_HEADS, 1, HEAD_DIM), F32)
    return _pcall_riding(
        body, rider, (qkv4, qkv4, qkv4, proj4, proj4, do3, o3, q_gain, k_gain, dproj4),
        name="sb_attn_bwd", grid=(b, N_HEADS // SB_HEADS_PER_STEP),
        in_specs=[_sb_group_spec(s, 0), _sb_group_spec(s, 1), _sb_group_spec(s, 2),
                  _sb_group_spec(s, 0), _sb_group_spec(s, 1),
                  _sb_seq_group_spec(s), _sb_seq_group_spec(s), _GAIN_SPEC, _GAIN_SPEC,
                  pl.BlockSpec(memory_space=pl.ANY)],
        out_specs=[pl.BlockSpec((3, 1, s, SB_GROUP_COLS), lambda b, g: (0, b, 0, g)),
                   _SB_GROUP_ROW_SPEC, _SB_GROUP_ROW_SPEC],
        out_shape=[jax.ShapeDtypeStruct(dproj4.shape, F32), head_row, head_row],
        scratch_shapes=[pltpu.VMEM((SB_HEADS_PER_STEP, s, HEAD_DIM), BF16)],
        input_output_aliases={9: 0}, semantics=("parallel", "parallel"))


def hgrn2_bwd(proj4, don3, oraw3, states, lbl4, o_gain, dproj4, rider=None):
    _, b, s, _ = proj4.shape
    nchunk = s // HG_CHUNK
    c = HG_CHUNK
    subs = range(HG_CHUNK // HG_SUB)
    unroll = math.gcd(nchunk, HG_UNROLL_BWD)
    ngroup = nchunk // unroll

    def body(q_ref, f_ref, i_ref, don_ref, oraw_ref, st_ref, lbl_ref, og_ref, _alias, d_ref, dog_ref, dlb_ref):
        incl, lower, before_sub, upper = _hg_masks()
        lb, oml = _hg_lower_bound(lbl_ref)
        last_row = lax.broadcasted_iota(jnp.int32, (c, HEAD_DIM), 0) == c - 1

        def group(m, carry):
            dst, dog_acc, dlb_acc = carry
            ns, rows = _hg_group_rows(ngroup - 1 - m, unroll)
            ns, rows = ns[::-1], rows[::-1]
            vs = [_hg_gates(q_ref[0, 0, r, :], f_ref[0, 0, r, :], lb, oml) for r in rows]
            inps = [i_ref[0, 0, r, :].astype(BF16) for r in rows]
            sts = [st_ref[0, 0, n] for n in ns]
            gcs = [_dot_exact(v["logf"], lower, left=True) for v in vs]
            grs = [_dot_exact(v["logf"], before_sub, left=True) for v in vs]
            e_qs = [jnp.exp(gc - gr) for gc, gr in zip(gcs, grs)]
            a_s, qdbs, ksbs, ess = _hg_intra([v["q"] * e for v, e in zip(vs, e_qs)], [v["k"] for v in vs],
                                             gcs, grs, incl)
            e_gcs = [jnp.exp(gc) for gc in gcs]
            gls = [gc[c - 1:c, :] for gc in gcs]
            e_gls = [jnp.exp(gl) for gl in gls]
            e_ks = [jnp.exp(gl - gc) for gl, gc in zip(gls, gcs)]
            normed = [_rms_bwd(oraw_ref[0, r, :], og_ref[...], don_ref[0, r, :]) for r in rows]
            dobs = [do.astype(BF16) for do, _ in normed]
            dabs = [jnp.where(incl, _dot(dob, inp, NT), 0.0).astype(BF16) for dob, inp in zip(dobs, inps)]
            adds = [_dot(dob, (v["q"] * e).astype(BF16), TN) for dob, v, e in zip(dobs, vs, e_gcs)]
            dq_inters = [_dot(dob, st.astype(BF16)) * e for dob, st, e in zip(dobs, sts, e_gcs)]
            dqds = [jnp.concatenate([_dot(dab[sub * HG_SUB:(sub + 1) * HG_SUB, :], ksb[sub]) for sub in subs], axis=0)
                    for dab, ksb in zip(dabs, ksbs)]
            dkss = [[_dot(dab[sub * HG_SUB:(sub + 1) * HG_SUB, :], qdb[sub * HG_SUB:(sub + 1) * HG_SUB, :], TN)
                     for sub in subs] for dab, qdb in zip(dabs, qdbs)]
            dsts = []
            for e_gl, add in zip(e_gls, adds):
                dsts.append(dst)
                dst = dst * e_gl + add
            dstbs = [d.astype(BF16) for d in dsts]
            dis = [_dot(a.astype(BF16), dob, TN) + _dot((v["k"] * e_k).astype(BF16), dstb, NT)
                   for a, dob, v, e_k, dstb in zip(a_s, dobs, vs, e_ks, dstbs)]
            dk_inters = [_dot(inp, dstb) * e_k for inp, dstb, e_k in zip(inps, dstbs, e_ks)]
            dks, dgcs = [], []
            for u in range(len(rows)):
                q, k = vs[u]["q"], vs[u]["k"]
                dk, dgc_k = dk_inters[u], jnp.zeros((c, HEAD_DIM), F32)
                for sub in subs:
                    dk = dk + dkss[u][sub] * ess[u][sub]
                    dgc_k = dgc_k + dkss[u][sub] * ksbs[u][sub].astype(F32)
                at_last = (jnp.sum(k * dk_inters[u], axis=0, keepdims=True)
                           + e_gls[u] * jnp.sum(sts[u] * dsts[u], axis=0, keepdims=True))
                dks.append(dk)
                dgcs.append((qdbs[u].astype(F32) * dqds[u] - dgc_k) + (q * dq_inters[u] - k * dk_inters[u])
                            + jnp.where(last_row, at_last, 0.0))
            dlf_fs = [_dot_exact(dgc, upper, left=True) / v["f"] for dgc, v in zip(dgcs, vs)]
            for u, r in enumerate(rows):
                v = vs[u]
                dq = dqds[u] * e_qs[u] + dq_inters[u]
                d_ref[0, 0, r, :] = dq * (v["sq"] * (1.0 + v["qp"] * (1.0 - v["sq"])))
                d_ref[1, 0, r, :] = (dlf_fs[u] - dks[u]) * (oml * v["sf"] * v["sfn"])
                d_ref[2, 0, r, :] = dis[u]
                dlb_acc = dlb_acc + jnp.sum((dlf_fs[u] - dks[u]) * v["sfn"], axis=0, keepdims=True)
                dog_acc = dog_acc + jnp.sum(normed[u][1], axis=0, keepdims=True)
            return dst, dog_acc, dlb_acc

        zero = jnp.zeros((1, HEAD_DIM), F32)
        _, dog, dlb = lax.fori_loop(0, ngroup, group, (jnp.zeros((HEAD_DIM, HEAD_DIM), F32), zero, zero))
        dog_ref[0, 0] = dog
        dlb_ref[0, 0] = dlb

    head_row = jax.ShapeDtypeStruct((b, N_HEADS, 1, HEAD_DIM), F32)
    return _pcall_riding(
        body, rider, (proj4, proj4, proj4, don3, oraw3, states, lbl4, o_gain, dproj4),
        name="hgrn2_bwd", grid=(b, N_HEADS),
        in_specs=[_head_spec(s, 0), _head_spec(s, 1), _head_spec(s, 2), _seq_spec(s), _seq_spec(s),
                  _state_spec(nchunk), pl.BlockSpec((2, 1, 1, HEAD_DIM), lambda b, h: (0, h, 0, 0)), _GAIN_SPEC,
                  pl.BlockSpec(memory_space=pl.ANY)],
        out_specs=[pl.BlockSpec((3, 1, s, HEAD_DIM), lambda b, h: (0, b, 0, h)), _HEAD_ROW_SPEC, _HEAD_ROW_SPEC],
        out_shape=[jax.ShapeDtypeStruct(dproj4.shape, F32), head_row, head_row],
        input_output_aliases={8: 0}, semantics=("parallel", "parallel"))


def local_step(x, target, sb_norm, wsi, sb_q_gain, sb_k_gain, hg_o_gain, hg_lb_logits, wso_mine, whi_mine, who_mine,
               hg_norm_mine):
    b, s, _ = x.shape
    t = b * s
    x2 = x.reshape(t, D_MODEL)
    tg2 = target.reshape(t, D_MODEL)
    lbl4 = hg_lb_logits.reshape(2, N_HEADS, 1, HEAD_DIM)
    four = (4, b, s, D_MODEL)
    three = (b, s, D_MODEL)
    rows8 = (N_DEV, W_ROWS, D_MODEL)

    (proj0, u0, qkv0), (wso, hgn) = rms_inproj(x2, sb_norm, wsi, "sb_inproj",
                                               _Rider([wso_mine, hg_norm_mine], scatter=False),
                                               qk_gains=(sb_q_gain, sb_k_gain))
    qkv0 = qkv0.reshape(3, b, s, D_MODEL)
    wso = wso.reshape(D_MODEL, D_MODEL)
    hg_norm_full = hgn[:, 0, :].reshape(1, D_MODEL)
    o0, (whi,) = sb_attn_fwd(qkv0, _Rider([whi_mine], scatter=False))
    o0 = o0.reshape(t, D_MODEL)
    h1 = gate_outproj(o0, proj0, wso, x2, None, "sb_outproj")
    (proj1, u1), _ = rms_inproj(h1, hg_norm_full, whi, "hg_inproj")
    (o1, o1_raw, states), (who,) = hgrn2_fwd(proj1.reshape(four), lbl4, hg_o_gain,
                                             _Rider([who_mine], scatter=False))
    who = who.reshape(D_MODEL, D_MODEL)
    o1 = o1.reshape(t, D_MODEL)
    dh2, loss_parts = gate_outproj(o1, proj1, who, h1, tg2, "hg_outproj_loss")

    do1, dproj1, g_who = outproj_bwd(dh2, who, o1, proj1, "hg_outproj_bwd")
    (dproj1, g_og, g_lb), (p_who,) = hgrn2_bwd(proj1.reshape(four), do1.reshape(three), o1_raw, states, lbl4,
                                               hg_o_gain, dproj1.reshape(four),
                                               _Rider([g_who.reshape(rows8)], scatter=True))
    dproj1 = dproj1.reshape(4, t, D_MODEL)
    (dh1, g_hgn), _ = inproj_bwd_dx(dproj1, whi, h1, hg_norm_full, dh2, "hg_inproj_bwd_dx")
    g_whi = inproj_bwd_dw(u1, dproj1, "hg_inproj_bwd_dw", out_dtype=BF16)

    do0, dproj0, g_wso = outproj_bwd(dh1, wso, o0, proj0, "sb_outproj_bwd")
    (dproj0, g_qg, g_kg), (p_whi, p_wso) = sb_attn_bwd(qkv0, proj0.reshape(four), do0.reshape(three), o0.reshape(three),
                                                       sb_q_gain, sb_k_gain, dproj0.reshape(four),
                                                       _Rider([g_whi, g_wso.reshape(rows8)], scatter=True))
    dproj0 = dproj0.reshape(4, t, D_MODEL)
    g_wsi = inproj_bwd_dw(u0, dproj0, "sb_inproj_bwd_dw", out_dtype=BF16)
    (gx, g_sbn), (p_wsi,) = inproj_bwd_dx(dproj0, wsi, x2, sb_norm, dh1, "sb_inproj_bwd_dx",
                                          _Rider([sum_within_chip(g_wsi)], scatter=True, chips=True))
    return dict(loss_parts=loss_parts, gx=gx.reshape(three), p_wsi=p_wsi, p_wso=p_wso, p_whi=p_whi, p_who=p_who,
                g_sbn=g_sbn, g_hgn=g_hgn, g_qg=g_qg, g_kg=g_kg, g_og=g_og, g_lb=g_lb)


def sum_within_chip(g):
    _, r, c_ = g.shape
    chips = N_DEV // 2

    def swap(g_ref, got_ref, send_sems, recv_sems):
        x, y, c = _mesh_pos()
        copies = [pltpu.make_async_remote_copy(
            src_ref=g_ref.at[2 * q + 1 - c], dst_ref=got_ref.at[q], send_sem=send_sems.at[q], recv_sem=recv_sems.at[q],
            device_id=(x, y, 1 - c), device_id_type=MESH) for q in range(chips)]
        for cp in copies:
            cp.start()
        for cp in copies:
            cp.wait_recv()
            cp.wait_send()

    got = _pcall(swap, name="swap_with_sibling", in_specs=[_ANY], out_specs=_ANY,
                 out_shape=jax.ShapeDtypeStruct((chips, r, c_), g.dtype),
                 scratch_shapes=[pltpu.SemaphoreType.DMA((chips,)), pltpu.SemaphoreType.DMA((chips,))])(g)

    def add(g_ref, got_ref, out_ref):
        mine = g_ref[0, lax.axis_index("c")]
        out_ref[0] = (mine.astype(F32) + got_ref[0].astype(F32)).astype(out_ref.dtype)

    return _pcall(
        add, name="add_sibling_partials", grid=(chips,),
        in_specs=[pl.BlockSpec((1, 2, r, c_), lambda q: (q, 0, 0, 0)), pl.BlockSpec((1, r, c_), lambda q: (q, 0, 0))],
        out_specs=pl.BlockSpec((1, r, c_), lambda q: (q, 0, 0)),
        out_shape=jax.ShapeDtypeStruct((chips, r, c_), g.dtype),
        compiler_params=_params("parallel"),
    )(g.reshape(chips, 2, r, c_), got)


def _two_level_gather(src, out, send_sems, recv_sems, local_sem, pos, meanwhile):
    x, y, c = pos
    me, sibling = (x, y, c), (x, y, 1 - c)
    chips = [(1 - x, y), (x, 1 - y), (1 - x, 1 - y)]

    def copy(k, block, to, source=None):
        slot = out.at[_linear(block)]
        return pltpu.make_async_remote_copy(
            src_ref=slot if source is None else source, dst_ref=slot, send_sem=send_sems.at[k],
            recv_sem=recv_sems.at[k], device_id=to, device_id_type=MESH)

    mine = pltpu.make_async_copy(src, out.at[_linear(me)], local_sem)
    mine.start()
    first = [copy(0, me, sibling, src)] + [copy(1 + j, me, (*chip, c), src) for j, chip in enumerate(chips)]
    for cp in first:
        cp.start()
    meanwhile()
    passed = [copy(4 + j, (*chip, c), sibling) for j, chip in enumerate(chips)]
    for j, chip in enumerate(chips):
        copy(1 + j, (*chip, c), me).wait_recv()
        passed[j].start()
    copy(0, sibling, me).wait_recv()
    for j, chip in enumerate(chips):
        copy(4 + j, (*chip, 1 - c), me).wait_recv()
    for cp in first + passed:
        cp.wait_send()
    mine.wait()


def gather_first_weights(w_si, w_so, w_hi, w_ho, hg_norm):
    def body(si_ref, so_ref, hi_ref, ho_ref, hn_ref, o_si, so_b, hi_b, ho_b, hn_b, si_b, send_sems, recv_sems, local_sem):
        si_b[...] = si_ref[...].astype(BF16)

        def cast_the_rest():
            for src, buf in ((so_ref, so_b), (hi_ref, hi_b), (ho_ref, ho_b)):
                buf[...] = src[...].astype(BF16)
            hn_b[...] = jnp.broadcast_to(hn_ref[...], hn_b.shape)

        _two_level_gather(si_b, o_si, send_sems, recv_sems, local_sem, _mesh_pos(), cast_the_rest)

    return _pcall(
        body, name="gather_first_weights",
        in_specs=[_VMEM] * 5, out_specs=[_ANY] + [_VMEM] * 4,
        out_shape=[jax.ShapeDtypeStruct((N_DEV,) + w_si.shape, BF16), jax.ShapeDtypeStruct(w_so.shape, BF16),
                   jax.ShapeDtypeStruct(w_hi.shape, BF16), jax.ShapeDtypeStruct(w_ho.shape, BF16),
                   jax.ShapeDtypeStruct((8, HEAD_DIM), F32)],
        scratch_shapes=[pltpu.VMEM(w_si.shape, BF16), pltpu.SemaphoreType.DMA((N_PEERS,)),
                        pltpu.SemaphoreType.DMA((N_PEERS,)), pltpu.SemaphoreType.DMA],
        compiler_params=pltpu.CompilerParams(vmem_limit_bytes=VMEM_LIMIT_BYTES),
    )(w_si, w_so, w_hi, w_ho, hg_norm)


def _adamw(w, g, m, v):
    m = ADAM_B1 * m + (1.0 - ADAM_B1) * g
    v = ADAM_B2 * v + (1.0 - ADAM_B2) * (g * g)
    m_hat = m / (1.0 - ADAM_B1 ** ADAM_STEP)
    v_hat = v / (1.0 - ADAM_B2 ** ADAM_STEP)
    delta = -ADAM_LR * (m_hat / (jnp.sqrt(v_hat) + ADAM_EPS) + ADAM_WD * w)
    return delta, m, v


def reduce_adamw(parts, w, m, v, name):
    n, r, c = parts.shape
    tr = _row_tile(r, 256)

    def body(p_ref, w_ref, m_ref, v_ref, g_ref, d_ref, m2_ref, v2_ref):
        g = p_ref[0].astype(F32)
        for sender in range(1, n):
            g = g + p_ref[sender].astype(F32)
        g_ref[0] = g
        d_ref[0], m2_ref[0], v2_ref[0] = _adamw(w_ref[0], g, m_ref[0], v_ref[0])

    tile = pl.BlockSpec((1, tr, c), lambda i: (0, i, 0))
    return _pcall(
        body, name=name, grid=(r // tr,),
        in_specs=[pl.BlockSpec((n, tr, c), lambda i: (0, i, 0)), tile, tile, tile],
        out_specs=[tile] * 4, out_shape=[jax.ShapeDtypeStruct((1, r, c), F32)] * 4,
        compiler_params=_params("parallel"),
    )(parts, w, m, v)


PACK_ROWS = 32
ROW_SBN, ROW_HGN, ROW_LB, ROW_QG, ROW_KG, ROW_OG, ROW_LOSS = 0, 8, 16, 24, 25, 26, 27


def small_update(g_sbn, g_hgn, g_lb, g_qg, g_kg, g_og, loss_parts, small):
    def gather(sbn_ref, hgn_ref, lb_ref, qg_ref, kg_ref, og_ref, loss_ref, gath, pack, send_sems, recv_sems, local_sems):
        pos = _mesh_pos()
        pack[...] = jnp.zeros_like(pack)
        pack[ROW_SBN:ROW_SBN + 8, :] = sbn_ref[...]
        pack[ROW_HGN:ROW_HGN + 8, :] = hgn_ref[...]
        pack[ROW_LB:ROW_LB + 8, :] = jnp.sum(lb_ref[...], axis=0)
        pack[ROW_QG:ROW_QG + 1, :] = jnp.sum(qg_ref[...], axis=0, keepdims=True)
        pack[ROW_KG:ROW_KG + 1, :] = jnp.sum(kg_ref[...], axis=0, keepdims=True)
        pack[ROW_OG:ROW_OG + 1, :] = jnp.sum(og_ref[...], axis=0, keepdims=True)
        pack[ROW_LOSS:ROW_LOSS + 1, :] = jnp.sum(loss_ref[...], axis=0)[0:1, :]
        _exchange_start([((lambda p: pack), gath)], send_sems, recv_sems, local_sems, pos)
        _exchange_wait([((lambda p: pack), gath)], send_sems, recv_sems, local_sems, pos)

    packs = _pcall(
        gather, name="small_gather",
        in_specs=[_VMEM] * 7, out_specs=_VMEM, out_shape=jax.ShapeDtypeStruct((N_DEV, PACK_ROWS, HEAD_DIM), F32),
        scratch_shapes=[pltpu.VMEM((PACK_ROWS, HEAD_DIM), F32)] + _exchange_sems(1),
    )(g_sbn, g_hgn, g_lb, g_qg, g_kg, g_og, loss_parts)

    def apply(gath, *refs):
        wmv, outs, tot = refs[:len(small)], refs[len(small):-1], refs[-1]
        me = _linear(_mesh_pos())
        total = gath[0]
        for dev in range(1, N_DEV):
            total = total + gath[dev]
        tot[...] = total
        outs[0][...] = jnp.broadcast_to(tot[ROW_LOSS:ROW_LOSS + 1, :], (8, HEAD_DIM))
        logits_ref, g_logits_ref = wmv[15], outs[1 + 4 * 5]
        for h in range(N_HEADS):
            lanes = slice(h * HEAD_DIM, (h + 1) * HEAD_DIM)
            p1, p0 = _sigmoid_pair(logits_ref[1:2, lanes] - logits_ref[0:1, lanes])
            d_l1 = p0 * p1 * tot[ROW_LB + h:ROW_LB + h + 1, :]
            g_logits_ref[0:1, lanes] = -d_l1
            g_logits_ref[1:2, lanes] = d_l1
        grads = [tot[ROW_SBN:ROW_SBN + 8, :], tot[ROW_QG:ROW_QG + 1, :], tot[ROW_KG:ROW_KG + 1, :],
                 tot[pl.ds(ROW_HGN + me, 1), :], tot[ROW_OG:ROW_OG + 1, :], g_logits_ref[...]]
        for i, g in enumerate(grads):
            w_ref, m_ref, v_ref = wmv[3 * i:3 * i + 3]
            o = outs[1 + 4 * i:5 + 4 * i]
            if i < 5:
                o[0][...] = g
            o[1][...], o[2][...], o[3][...] = _adamw(w_ref[...], g, m_ref[...], v_ref[...])

    out_shape = [jax.ShapeDtypeStruct((8, HEAD_DIM), F32)]
    for i in range(6):
        out_shape += [jax.ShapeDtypeStruct(small[3 * i].shape, F32)] * 4
    return _pcall(
        apply, name="small_update",
        in_specs=[_VMEM] * (1 + len(small)), out_specs=[_VMEM] * 25, out_shape=out_shape,
        scratch_shapes=[pltpu.VMEM((PACK_ROWS, HEAD_DIM), F32)],
    )(packs, *small)


def kernel(x, sb_norm, sb_w_in, sb_q_gain, sb_k_gain, sb_w_out, hg_norm, hg_w_in, hg_o_gain, hg_w_out, hg_lb_logits, loss_target, m_sb_norm, m_sb_w_in, m_sb_q_gain, m_sb_k_gain, m_sb_w_out, m_hg_norm, m_hg_w_in, m_hg_o_gain, m_hg_w_out, m_hg_lb_logits, v_sb_norm, v_sb_w_in, v_sb_q_gain, v_sb_k_gain, v_sb_w_out, v_hg_norm, v_hg_w_in, v_hg_o_gain, v_hg_w_out, v_hg_lb_logits):
    b = x.shape[0]
    wsi, wso_mine, whi_mine, who_mine, hg_norm_mine = gather_first_weights(
        sb_w_in[0], sb_w_out[0], hg_w_in[0], hg_w_out[0], hg_norm)
    r = local_step(x, loss_target, sb_norm, wsi, sb_q_gain, sb_k_gain, hg_o_gain, hg_lb_logits,
                   wso_mine, whi_mine, who_mine, hg_norm_mine)
    big = {}
    for name, p, w, m, v in (("sb_w_in", r["p_wsi"], sb_w_in, m_sb_w_in, v_sb_w_in),
                             ("sb_w_out", r["p_wso"], sb_w_out, m_sb_w_out, v_sb_w_out),
                             ("hg_w_in", r["p_whi"], hg_w_in, m_hg_w_in, v_hg_w_in),
                             ("hg_w_out", r["p_who"], hg_w_out, m_hg_w_out, v_hg_w_out)):
        big[name] = reduce_adamw(p, w, m, v, "adamw_" + name)

    def rows8(a):
        return a.reshape(8, HEAD_DIM)

    small_in = [rows8(sb_norm), rows8(m_sb_norm), rows8(v_sb_norm),
                sb_q_gain, m_sb_q_gain, v_sb_q_gain,
                sb_k_gain, m_sb_k_gain, v_sb_k_gain,
                hg_norm, m_hg_norm, v_hg_norm,
                hg_o_gain, m_hg_o_gain, v_hg_o_gain,
                hg_lb_logits, m_hg_lb_logits, v_hg_lb_logits]
    so = small_update(rows8(r["g_sbn"]), rows8(r["g_hgn"]), r["g_lb"].reshape(b, N_HEADS, HEAD_DIM),
                      r["g_qg"].reshape(b * N_HEADS, HEAD_DIM), r["g_kg"].reshape(b * N_HEADS, HEAD_DIM),
                      r["g_og"].reshape(b * N_HEADS, HEAD_DIM), r["loss_parts"], small_in)
    loss = so[0][0, 0]
    shapes = {"sb_norm": (1, D_MODEL), "sb_q_gain": (1, HEAD_DIM), "sb_k_gain": (1, HEAD_DIM),
              "hg_norm": (1, HEAD_DIM), "hg_o_gain": (1, HEAD_DIM), "hg_lb_logits": (2, D_MODEL)}
    small = {}
    for i, name in enumerate(("sb_norm", "sb_q_gain", "sb_k_gain", "hg_norm", "hg_o_gain", "hg_lb_logits")):
        small[name] = [o.reshape(shapes[name]) for o in so[1 + 4 * i:5 + 4 * i]]
    order = ("sb_norm", "sb_w_in", "sb_q_gain", "sb_k_gain", "sb_w_out",
             "hg_norm", "hg_w_in", "hg_o_gain", "hg_w_out", "hg_lb_logits")
    res = {**big, **small}
    return (loss, r["gx"]) + tuple(res[n][j] for j in range(4) for n in order)
```

```python
import functools
import math

import jax
import jax.numpy as jnp
from jax import lax
from jax.experimental import pallas as pl
from jax.experimental.pallas import tpu as pltpu

F32 = jnp.float32
BF16 = jnp.bfloat16

N_DEV = 8
D_MODEL = 1024
N_HEADS = 8
HEAD_DIM = 128
RMS_EPS = 1e-6
ATTN_BLOCK = 128
HG_CHUNK = 64
HG_SUB = 16
HG_UNROLL_FWD = 16
HG_UNROLL_BWD = 16
EXP_CLAMP = 80.0
SB_HEADS_PER_STEP = 2
SB_QBLOCKS_PER_STEP = 4
SB_GROUP_COLS = SB_HEADS_PER_STEP * 128
SB_TOP_ROWS = 32
SB_LOG_WEIGHT_FLOOR = -104.0
VMEM_LIMIT_BYTES = 48 * 1024 * 1024
W_COLS = 4 * D_MODEL // N_DEV
W_ROWS = D_MODEL // N_DEV

ADAM_LR = 0.001
ADAM_B1 = 0.9
ADAM_B2 = 0.999
ADAM_EPS = 1e-08
ADAM_WD = 0.01
ADAM_STEP = 10

NT = (((1,), (1,)), ((), ()))
TN = (((0,), (0,)), ((), ()))
NN = (((1,), (0,)), ((), ()))


def _pcall(body, *, name, **kw):
    return pl.pallas_call(body, name=name, **kw)


def _params(*sem):
    return pltpu.CompilerParams(dimension_semantics=sem, vmem_limit_bytes=VMEM_LIMIT_BYTES)


def _dot(a, b, dims=NN):
    return lax.dot_general(a, b, dims, preferred_element_type=F32)


def _dot_exact(a, m, dims=NN, left=False):
    hi = a.astype(BF16)
    lo = (a - hi.astype(F32)).astype(BF16)
    if left:
        return _dot(m, hi, dims) + _dot(m, lo, dims)
    return _dot(hi, m, dims) + _dot(lo, m, dims)


def _sigmoid(x):
    return 1.0 / (1.0 + jnp.exp(-x))


def _sigmoid_pair(x):
    e = jnp.exp(-jnp.abs(x))
    big = 1.0 / (1.0 + e)
    small = e * big
    pos = x >= 0
    return jnp.where(pos, big, small), jnp.where(pos, small, big)


def _rms_scale(x):
    return lax.rsqrt(jnp.mean(x * x, axis=-1, keepdims=True) + RMS_EPS)


def _row_tile(t, want):
    return want if t % want == 0 else t


MESH = pl.DeviceIdType.MESH
N_PEERS = N_DEV - 1
_ANY = pl.BlockSpec(memory_space=pl.ANY)
_VMEM = pl.BlockSpec(memory_space=pltpu.VMEM)


def _mesh_pos():
    return lax.axis_index("x"), lax.axis_index("y"), lax.axis_index("c")


def _linear(pos):
    return 4 * pos[0] + 2 * pos[1] + pos[2]


def _peer(pos, k):
    flips = ((k + 1) >> 2 & 1, (k + 1) >> 1 & 1, (k + 1) & 1)
    return tuple(1 - p if f else p for p, f in zip(pos, flips))


def _chip(pos):
    return 2 * pos[0] + pos[1]


ALL_PEERS = tuple(range(N_PEERS))
SAME_CORE_PEERS = (1, 3, 5)


def _exchange_copies(pairs, send_sems, recv_sems, local_sems, pos, landing, peers, slot):
    me = slot(pos)
    local, remote = [], []
    for a, (src_of, dst) in enumerate(pairs):
        local.append(pltpu.make_async_copy(src_of(pos), dst.at[me], local_sems.at[a]))
        for k in peers:
            peer = _peer(pos, k)
            remote.append(pltpu.make_async_remote_copy(
                src_ref=src_of(peer), dst_ref=dst.at[slot(peer) if landing else me],
                send_sem=send_sems.at[a, k], recv_sem=recv_sems.at[a, k], device_id=peer, device_id_type=MESH))
    return local, remote


def _exchange_start(pairs, send_sems, recv_sems, local_sems, pos, peers=ALL_PEERS, slot=_linear):
    local, sent = _exchange_copies(pairs, send_sems, recv_sems, local_sems, pos, False, peers, slot)
    for copy in local + sent:
        copy.start()


def _exchange_wait(pairs, send_sems, recv_sems, local_sems, pos, peers=ALL_PEERS, slot=_linear):
    local, landed = _exchange_copies(pairs, send_sems, recv_sems, local_sems, pos, True, peers, slot)
    for copy in landed:
        copy.wait_recv()
        copy.wait_send()
    for copy in local:
        copy.wait()


def _exchange_sems(n):
    return [pltpu.SemaphoreType.DMA((n, N_PEERS)), pltpu.SemaphoreType.DMA((n, N_PEERS)),
            pltpu.SemaphoreType.DMA((n,))]


class _Rider:
    def __init__(self, arrays, scatter, chips=False):
        self.arrays = list(arrays)
        self.scatter = scatter
        self.peers = SAME_CORE_PEERS if chips else ALL_PEERS
        self.slot = _chip if chips else _linear
        self.out_shapes = [jax.ShapeDtypeStruct(a.shape if scatter else (N_DEV,) + a.shape, a.dtype)
                           for a in self.arrays]

    def pairs(self, in_refs, out_refs):
        if self.scatter:
            return [((lambda pos, r=r: r.at[self.slot(pos)]), o) for r, o in zip(in_refs, out_refs)]
        return [((lambda pos, r=r: r), o) for r, o in zip(in_refs, out_refs)]


def _pcall_riding(body, rider, args, *, name, grid, in_specs, out_specs, out_shape, semantics, scratch_shapes=(),
                  input_output_aliases=None):
    aliases = input_output_aliases or {}
    if rider is None:
        outs = _pcall(body, name=name, grid=grid, in_specs=list(in_specs), out_specs=list(out_specs),
                      out_shape=list(out_shape), scratch_shapes=list(scratch_shapes), input_output_aliases=aliases,
                      compiler_params=_params(*semantics))(*args)
        return list(outs), []
    n_in, n_out, n_scr, n_r = len(in_specs), len(out_specs), len(scratch_shapes), len(rider.arrays)

    def riding(*refs):
        ins, refs = refs[:n_in], refs[n_in:]
        rider_in, refs = refs[:n_r], refs[n_r:]
        outs, refs = refs[:n_out], refs[n_out:]
        rider_out, refs = refs[:n_r], refs[n_r:]
        scratch, sems = refs[:n_scr], refs[n_scr:]
        pairs = rider.pairs(rider_in, rider_out)
        first = functools.reduce(jnp.logical_and, [pl.program_id(a) == 0 for a in range(len(grid))])
        last = functools.reduce(jnp.logical_and, [pl.program_id(a) == g - 1 for a, g in enumerate(grid)])

        @pl.when(first)
        def _():
            _exchange_start(pairs, *sems, _mesh_pos(), rider.peers, rider.slot)

        body(*ins, *outs, *scratch)

        @pl.when(last)
        def _():
            _exchange_wait(pairs, *sems, _mesh_pos(), rider.peers, rider.slot)

    outs = _pcall(riding, name=name, grid=grid, in_specs=list(in_specs) + [_ANY] * n_r,
                  out_specs=list(out_specs) + [_ANY] * n_r, out_shape=list(out_shape) + rider.out_shapes,
                  scratch_shapes=list(scratch_shapes) + _exchange_sems(n_r), input_output_aliases=aliases,
                  compiler_params=_params(*(("arbitrary",) * len(grid))))(*args, *rider.arrays)
    return list(outs[:n_out]), list(outs[n_out:])


def rms_inproj(x2, gain, wg, name, rider=None, qk_gains=None):
    t = x2.shape[0]
    with_qkv = qk_gains is not None
    tm = _row_tile(t, 256 if with_qkv else 512)

    def body(x_ref, g_ref, w_ref, *rest):
        if with_qkv:
            qg_ref, kg_ref, proj_ref, ut_ref, qkv_ref = rest
            head_gain = (qg_ref, kg_ref)
        else:
            proj_ref, ut_ref = rest
        x = x_ref[...]
        u = x * _rms_scale(x) * g_ref[...]
        ut_ref[...] = u.T.astype(BF16)
        u = u.astype(BF16)
        for p in range(N_DEV):
            part, lo = p // 2, (p % 2) * W_COLS
            res = _dot(u, w_ref[p])
            proj_ref[part, :, lo:lo + W_COLS] = res
            if with_qkv and part < 3:
                for h in range(W_COLS // HEAD_DIM):
                    y = res[:, h * HEAD_DIM:(h + 1) * HEAD_DIM]
                    if part < 2:
                        y = y * _rms_scale(y) * head_gain[part][...]
                    qkv_ref[part, :, lo + h * HEAD_DIM:lo + (h + 1) * HEAD_DIM] = y.astype(BF16)

    vec = pl.BlockSpec((1, D_MODEL), lambda i: (0, 0))
    in_specs = [pl.BlockSpec((tm, D_MODEL), lambda i: (i, 0)), vec,
                pl.BlockSpec((N_DEV, D_MODEL, W_COLS), lambda i: (0, 0, 0))]
    out_specs = [pl.BlockSpec((4, tm, D_MODEL), lambda i: (0, i, 0)), pl.BlockSpec((D_MODEL, tm), lambda i: (0, i))]
    out_shape = [jax.ShapeDtypeStruct((4, t, D_MODEL), F32), jax.ShapeDtypeStruct((D_MODEL, t), BF16)]
    args = (x2, gain, wg)
    if with_qkv:
        in_specs += [pl.BlockSpec((1, HEAD_DIM), lambda i: (0, 0))] * 2
        out_specs.append(pl.BlockSpec((3, tm, D_MODEL), lambda i: (0, i, 0)))
        out_shape.append(jax.ShapeDtypeStruct((3, t, D_MODEL), BF16))
        args += tuple(qk_gains)
    return _pcall_riding(body, rider, args, name=name, grid=(t // tm,), in_specs=in_specs, out_specs=out_specs,
                         out_shape=out_shape, semantics=("parallel",))


def gate_outproj(o2, proj, w_out, resid, target, name):
    t = o2.shape[0]
    tm = _row_tile(t, 512)
    with_loss = target is not None

    def body(o_ref, gate_ref, w_ref, r_ref, *rest):
        g = gate_ref[0]
        og = (o_ref[...] * (g * _sigmoid(g))).astype(BF16)
        h = r_ref[...] + _dot(og, w_ref[...])
        if with_loss:
            t_ref, dh_ref, loss_ref = rest
            err = h - t_ref[...]
            dh_ref[...] = err * (1.0 / D_MODEL)
            part = 0.5 * jnp.sum(jnp.mean(err * err, axis=-1, keepdims=True))
            loss_ref[...] = jnp.full(loss_ref.shape, part, F32)
        else:
            (h_ref,) = rest
            h_ref[...] = h

    row = pl.BlockSpec((tm, D_MODEL), lambda i: (i, 0))
    in_specs = [row,
                pl.BlockSpec((1, tm, D_MODEL), lambda i: (3, i, 0)),
                pl.BlockSpec((D_MODEL, D_MODEL), lambda i: (0, 0)),
                row]
    args = [o2, proj, w_out, resid]
    if with_loss:
        in_specs.append(row)
        args.append(target)
        out_specs = [row, pl.BlockSpec((1, 8, 128), lambda i: (i, 0, 0))]
        out_shape = [jax.ShapeDtypeStruct((t, D_MODEL), F32),
                     jax.ShapeDtypeStruct((t // tm, 8, 128), F32)]
    else:
        out_specs = row
        out_shape = jax.ShapeDtypeStruct((t, D_MODEL), F32)
    return _pcall(body, name=name, grid=(t // tm,), in_specs=in_specs, out_specs=out_specs,
                  out_shape=out_shape, compiler_params=_params("parallel"))(*args)


def _head_spec(s, part):
    return pl.BlockSpec((1, 1, s, HEAD_DIM), lambda b, h: (part, b, 0, h))


def _seq_spec(s):
    return pl.BlockSpec((1, s, HEAD_DIM), lambda b, h: (b, 0, h))


_GAIN_SPEC = pl.BlockSpec((1, HEAD_DIM), lambda b, h: (0, 0))
_HEAD_ROW_SPEC = pl.BlockSpec((1, 1, 1, HEAD_DIM), lambda b, h: (b, h, 0, 0))


def _sb_group_spec(s, part):
    return pl.BlockSpec((1, 1, s, SB_GROUP_COLS), lambda b, g: (part, b, 0, g))


def _sb_seq_group_spec(s):
    return pl.BlockSpec((1, s, SB_GROUP_COLS), lambda b, g: (b, 0, g))


_SB_GROUP_ROW_SPEC = pl.BlockSpec((1, SB_HEADS_PER_STEP, 1, HEAD_DIM), lambda b, g: (b, g, 0, 0))


def _sb_chains(m):
    return [(h, m * SB_QBLOCKS_PER_STEP + r) for h in range(SB_HEADS_PER_STEP) for r in range(SB_QBLOCKS_PER_STEP)]


def _sb_logits(qi, kj):
    return _dot(qi, kj, NT) * (HEAD_DIM ** -0.5)


def _sb_scores(z, diag, live, tri_lt):
    soft = jnp.log(1.0 + jnp.exp(-jnp.abs(z)))
    valid = jnp.logical_and(live, jnp.logical_or(jnp.logical_not(diag), tri_lt))
    log_skip = jnp.where(valid, -(jnp.maximum(z, 0.0) + soft), 0.0)
    log_beta = jnp.minimum(z, 0.0) - soft
    return log_skip, log_beta, valid


def _sb_keys_left(chains, watch, state):
    done, carries = state[0], state[1]
    worst = None
    for (_, i), c in zip(chains, carries):
        c = jnp.where(done <= i, c[watch], -jnp.inf)
        worst = c if worst is None else jnp.maximum(worst, c)
    return jnp.logical_and(done <= chains[-1][1],
                           jnp.logical_or(done == 0, jnp.max(worst) > SB_LOG_WEIGHT_FLOOR))


def _sb_key_rows(i, done):
    j = i - done
    return pl.ds(pl.multiple_of(jnp.maximum(j, 0) * ATTN_BLOCK, ATTN_BLOCK), ATTN_BLOCK), j >= 0


def _sb_head(ref, h, rows):
    return ref[0, 0, rows, h * HEAD_DIM:(h + 1) * HEAD_DIM]


def _sb_rows(i, offset, count):
    return pl.ds(pl.multiple_of(jnp.maximum(i, 0) * ATTN_BLOCK + offset, 8), count)


def sb_attn_fwd(qkv4, rider=None):
    _, b, s, _ = qkv4.shape
    blk, top = ATTN_BLOCK, SB_TOP_ROWS
    ngroups = s // blk // SB_QBLOCKS_PER_STEP
    assert ngroups * SB_QBLOCKS_PER_STEP * blk == s

    def body(q_ref, k_ref, v_ref, o_ref):
        row = lax.broadcasted_iota(jnp.int32, (blk, blk), 0)
        col = lax.broadcasted_iota(jnp.int32, (blk, blk), 1)
        tri_lt = col < row
        suffix = (row > col).astype(BF16)

        def step(items):
            where = [_sb_key_rows(i, done) for _, i, done, _, _, _, _ in items]
            zs = [_sb_logits(q, _sb_head(k_ref, h, rows)) for (h, _, _, q, _, _, _), (rows, _) in zip(items, where)]
            scored = [_sb_scores(z, done == 0, live, mask)
                      for z, (_, _, done, _, mask, _, _), (_, live) in zip(zs, items, where)]
            afters = [_dot_exact(log_skip, suffix) for log_skip, _, _ in scored]
            ws = [jnp.where(valid, jnp.exp(log_beta + after + c), 0.0).astype(BF16)
                  for (_, log_beta, valid), after, (_, _, _, _, _, c, _) in zip(scored, afters, items)]
            accs = [acc + _dot(w, _sb_head(v_ref, h, rows))
                    for (h, _, _, _, _, _, acc), (rows, _), w in zip(items, where, ws)]
            cs = [c + jnp.sum(log_skip, axis=1, keepdims=True)
                  for (log_skip, _, _), (_, _, _, _, _, c, _) in zip(scored, items)]
            return cs, accs

        def top_items(chains, done, cs, accs):
            return [(h, i, done, _sb_head(q_ref, h, _sb_rows(i, 0, top)), tri_lt[:top, :], c, acc)
                    for (h, i), c, acc in zip(chains, cs, accs)]

        def finish_tops(chains, state):
            def k_step(state):
                done, cs, accs = state
                cs, accs = step(top_items(chains, done, cs, accs))
                return done + 1, tuple(cs), tuple(accs)

            _, _, accs = lax.while_loop(functools.partial(_sb_keys_left, chains, slice(0, top)), k_step, state)
            for (h, i), acc in zip(chains, accs):
                o_ref[0, _sb_rows(i, 0, top), h * HEAD_DIM:(h + 1) * HEAD_DIM] = acc

        def q_group(m, before):
            chains, chains_before = _sb_chains(m), _sb_chains(m - 1)
            qis = [_sb_head(q_ref, h, _sb_rows(i, 0, blk)) for h, i in chains]
            n = len(chains)

            def items_of(done, cs, accs):
                return [(h, i, done, q, tri_lt, c, acc) for (h, i), q, c, acc in zip(chains, qis, cs, accs)]

            def k_step(state):
                done, cs, accs = state
                cs, accs = step(items_of(done, cs, accs))
                return done + 1, tuple(cs), tuple(accs)

            done_b, cs_b, accs_b = before
            cs0, accs0 = step(items_of(jnp.int32(0), (jnp.zeros((blk, 1), F32),) * n,
                                       (jnp.zeros((blk, HEAD_DIM), F32),) * n)
                              + top_items(chains_before, done_b, cs_b, accs_b))
            done, cs, accs = lax.while_loop(functools.partial(_sb_keys_left, chains, slice(top, blk)), k_step,
                                            (jnp.int32(1), tuple(cs0[:n]), tuple(accs0[:n])))

            @pl.when(m > 0)
            def _():
                finish_tops(chains_before, (done_b + 1, tuple(cs0[n:]), tuple(accs0[n:])))

            for (h, i), acc in zip(chains, accs):
                o_ref[0, _sb_rows(i, top, blk - top), h * HEAD_DIM:(h + 1) * HEAD_DIM] = acc[top:, :]
            return done, tuple(c[:top, :] for c in cs), tuple(acc[:top, :] for acc in accs)

        n = SB_HEADS_PER_STEP * SB_QBLOCKS_PER_STEP
        nothing = (jnp.int32(0), (jnp.zeros((top, 1), F32),) * n, (jnp.zeros((top, HEAD_DIM), F32),) * n)
        last = lax.fori_loop(0, ngroups, q_group, nothing)
        finish_tops(_sb_chains(ngroups - 1), last)

    (o,), extra = _pcall_riding(
        body, rider, (qkv4, qkv4, qkv4),
        name="sb_attn_fwd", grid=(b, N_HEADS // SB_HEADS_PER_STEP),
        in_specs=[_sb_group_spec(s, 0), _sb_group_spec(s, 1), _sb_group_spec(s, 2)],
        out_specs=[_sb_seq_group_spec(s)],
        out_shape=[jax.ShapeDtypeStruct((b, s, D_MODEL), F32)],
        semantics=("parallel", "parallel"))
    return o, extra


def _hg_masks():
    c = HG_CHUNK
    row = lax.broadcasted_iota(jnp.int32, (c, c), 0)
    col = lax.broadcasted_iota(jnp.int32, (c, c), 1)
    incl = (col <= row)
    lower = incl.astype(BF16)
    before_sub = (col < (row // HG_SUB) * HG_SUB).astype(BF16)
    upper = (col >= row).astype(BF16)
    return incl, lower, before_sub, upper


def _hg_lower_bound(lbl_ref):
    l0 = lbl_ref[0, 0]
    l1 = lbl_ref[1, 0]
    d = l1 - l0
    return _sigmoid_pair(d)


def _hg_gates(qp, fp, lb, oml):
    sq = _sigmoid(qp)
    sf, sfn = _sigmoid_pair(fp)
    f = lb + oml * sf
    return dict(qp=qp, sq=sq, q=qp * sq, sf=sf, sfn=sfn, f=f, k=oml * sfn, logf=jnp.log(f))


def _hg_intra(qds, ks, gcs, grs, incl):
    subs = range(HG_CHUNK // HG_SUB)
    qdbs = [qd.astype(BF16) for qd in qds]
    ess = [[jnp.exp(jnp.minimum(gr[sub * HG_SUB:sub * HG_SUB + 1, :] - gc, EXP_CLAMP)) for sub in subs]
           for gc, gr in zip(gcs, grs)]
    ksbs = [[(k * e).astype(BF16) for e in es] for k, es in zip(ks, ess)]
    rows = [[_dot(qdb[sub * HG_SUB:(sub + 1) * HG_SUB, :], ksb[sub], NT) for sub in subs]
            for qdb, ksb in zip(qdbs, ksbs)]
    a_s = [jnp.where(incl, jnp.concatenate(r, axis=0), 0.0) for r in rows]
    return a_s, qdbs, ksbs, ess


def _hg_group_rows(outer, unroll):
    ns = [outer * unroll + u for u in range(unroll)]
    return ns, [pl.ds(pl.multiple_of(n * HG_CHUNK, HG_CHUNK), HG_CHUNK) for n in ns]


def _state_spec(nchunk):
    return pl.BlockSpec((1, 1, nchunk, HEAD_DIM, HEAD_DIM), lambda b, h: (b, h, 0, 0, 0))


def hgrn2_fwd(proj4, lbl4, o_gain, rider=None):
    _, b, s, _ = proj4.shape
    nchunk = s // HG_CHUNK
    c = HG_CHUNK
    unroll = math.gcd(nchunk, HG_UNROLL_FWD)

    def body(q_ref, f_ref, i_ref, lbl_ref, og_ref, o_ref, oraw_ref, st_ref):
        incl, lower, before_sub, _ = _hg_masks()
        lb, oml = _hg_lower_bound(lbl_ref)

        def group(outer, st):
            ns, rows = _hg_group_rows(outer, unroll)
            vs = [_hg_gates(q_ref[0, 0, r, :], f_ref[0, 0, r, :], lb, oml) for r in rows]
            inps = [i_ref[0, 0, r, :].astype(BF16) for r in rows]
            gcs = [_dot_exact(v["logf"], lower, left=True) for v in vs]
            grs = [_dot_exact(v["logf"], before_sub, left=True) for v in vs]
            a_s, _, _, _ = _hg_intra([v["q"] * jnp.exp(gc - gr) for v, gc, gr in zip(vs, gcs, grs)],
                                     [v["k"] for v in vs], gcs, grs, incl)
            gls = [gc[c - 1:c, :] for gc in gcs]
            adds = [_dot(inp, (v["k"] * jnp.exp(gl - gc)).astype(BF16), TN)
                    for inp, v, gl, gc in zip(inps, vs, gls, gcs)]
            o_intra = [_dot(a.astype(BF16), inp) for a, inp in zip(a_s, inps)]
            sts = []
            for gl, add in zip(gls, adds):
                sts.append(st)
                st = st * jnp.exp(gl) + add
            outs = [oi + _dot((v["q"] * jnp.exp(gc)).astype(BF16), s0.astype(BF16), NT)
                    for oi, v, gc, s0 in zip(o_intra, vs, gcs, sts)]
            for n, r, s0, o in zip(ns, rows, sts, outs):
                st_ref[0, 0, n] = s0
                oraw_ref[0, r, :] = o
                o_ref[0, r, :] = o * _rms_scale(o) * og_ref[...]
            return st

        lax.fori_loop(0, nchunk // unroll, group, jnp.zeros((HEAD_DIM, HEAD_DIM), F32))

    seq = jax.ShapeDtypeStruct((b, s, D_MODEL), F32)
    return _pcall_riding(
        body, rider, (proj4, proj4, proj4, lbl4, o_gain), name="hgrn2_fwd", grid=(b, N_HEADS),
        in_specs=[_head_spec(s, 0), _head_spec(s, 1), _head_spec(s, 2),
                  pl.BlockSpec((2, 1, 1, HEAD_DIM), lambda b, h: (0, h, 0, 0)), _GAIN_SPEC],
        out_specs=[_seq_spec(s), _seq_spec(s), _state_spec(nchunk)],
        out_shape=[seq, seq, jax.ShapeDtypeStruct((b, N_HEADS, nchunk, HEAD_DIM, HEAD_DIM), F32)],
        semantics=("parallel", "parallel"))


def outproj_bwd(dh, w_out, o2, proj, name):
    t = dh.shape[0]
    tm = _row_tile(t, 512)

    def body(dh_ref, w_ref, o_ref, gate_ref, do_ref, dproj_ref, dw_ref):
        dhb = dh_ref[...].astype(BF16)
        dog = _dot(dhb, w_ref[...], NT)
        g = gate_ref[0]
        sg = _sigmoid(g)
        silu = g * sg
        o = o_ref[...]
        do_ref[...] = dog * silu
        dproj_ref[0] = dog * o * (sg * (1.0 + g * (1.0 - sg)))
        part = _dot((o * silu).astype(BF16), dhb, TN)

        @pl.when(pl.program_id(0) == 0)
        def _():
            dw_ref[...] = part

        @pl.when(pl.program_id(0) > 0)
        def _():
            dw_ref[...] += part

    row = pl.BlockSpec((tm, D_MODEL), lambda i: (i, 0))
    full = pl.BlockSpec((D_MODEL, D_MODEL), lambda i: (0, 0))
    return _pcall(
        body, name=name, grid=(t // tm,),
        in_specs=[row, full, row, pl.BlockSpec((1, tm, D_MODEL), lambda i: (3, i, 0))],
        out_specs=[row, pl.BlockSpec((1, tm, D_MODEL), lambda i: (3, i, 0)), full],
        out_shape=[jax.ShapeDtypeStruct((t, D_MODEL), F32),
                   jax.ShapeDtypeStruct((4, t, D_MODEL), F32),
                   jax.ShapeDtypeStruct((D_MODEL, D_MODEL), F32)],
        compiler_params=_params("arbitrary"),
    )(dh, w_out, o2, proj)


def inproj_bwd_dx(dproj, wg, x2, gain, dres, name, rider=None):
    t = x2.shape[0]
    tm = _row_tile(t, 512)

    def body(d_ref, w_ref, x_ref, g_ref, r_ref, dx_ref, dg_ref):
        du = jnp.zeros((tm, D_MODEL), F32)
        for p in range(N_DEV):
            cols = slice((p % 2) * W_COLS, (p % 2 + 1) * W_COLS)
            du = du + _dot(d_ref[p // 2, :, cols].astype(BF16), w_ref[p], NT)
        x = x_ref[...]
        r = _rms_scale(x)
        xh = x * r
        a = du * g_ref[...]
        dx_ref[...] = r_ref[...] + r * (a - xh * jnp.mean(a * xh, axis=-1, keepdims=True))
        part = jnp.sum(du * xh, axis=0, keepdims=True)

        @pl.when(pl.program_id(0) == 0)
        def _():
            dg_ref[...] = part

        @pl.when(pl.program_id(0) > 0)
        def _():
            dg_ref[...] += part

    row = pl.BlockSpec((tm, D_MODEL), lambda i: (i, 0))
    vec = pl.BlockSpec((1, D_MODEL), lambda i: (0, 0))
    return _pcall_riding(
        body, rider, (dproj, wg, x2, gain, dres), name=name, grid=(t // tm,),
        in_specs=[pl.BlockSpec((4, tm, D_MODEL), lambda i: (0, i, 0)),
                  pl.BlockSpec((N_DEV, D_MODEL, W_COLS), lambda i: (0, 0, 0)),
                  row, vec, row],
        out_specs=[row, vec],
        out_shape=[jax.ShapeDtypeStruct((t, D_MODEL), F32), jax.ShapeDtypeStruct((1, D_MODEL), F32)],
        semantics=("arbitrary",))


def inproj_bwd_dw(ut, dproj, name, out_dtype):
    t = ut.shape[1]

    def body(ut_ref, d_ref, dw_ref):
        dw_ref[0] = _dot(ut_ref[...], d_ref[0].astype(BF16)).astype(dw_ref.dtype)

    return _pcall(
        body, name=name, grid=(N_DEV,),
        in_specs=[pl.BlockSpec((D_MODEL, t), lambda j: (0, 0)),
                  pl.BlockSpec((1, t, W_COLS), lambda j: (j // 2, 0, j % 2))],
        out_specs=pl.BlockSpec((1, D_MODEL, W_COLS), lambda j: (j, 0, 0)),
        out_shape=jax.ShapeDtypeStruct((N_DEV, D_MODEL, W_COLS), out_dtype),
        compiler_params=_params("parallel"),
    )(ut, dproj)


def _rms_bwd(x, gain, dy):
    r = _rms_scale(x)
    xh = x * r
    a = dy * gain
    return r * (a - xh * jnp.mean(a * xh, axis=-1, keepdims=True)), dy * xh


def sb_attn_bwd(qkv4, proj4, do3, o3, q_gain, k_gain, dproj4, rider=None):
    _, b, s, _ = proj4.shape
    blk, top = ATTN_BLOCK, SB_TOP_ROWS
    nq = s // blk
    ngroups = nq // SB_QBLOCKS_PER_STEP
    assert ngroups * SB_QBLOCKS_PER_STEP * blk == s
    scale = HEAD_DIM ** -0.5

    def body(qn_ref, kn_ref, v_ref, q_ref, k_ref, do_ref, o_ref, qg_ref, kg_ref, _alias, d_ref, dqg_ref, dkg_ref, dob):
        for h in range(SB_HEADS_PER_STEP):
            dob[h] = do_ref[0, :, h * HEAD_DIM:(h + 1) * HEAD_DIM].astype(BF16)
        d_ref[...] = jnp.zeros_like(d_ref)
        row = lax.broadcasted_iota(jnp.int32, (blk, blk), 0)
        col = lax.broadcasted_iota(jnp.int32, (blk, blk), 1)
        tri_lt = col < row
        suffix = (row > col).astype(BF16)
        suffix_incl = (row >= col).astype(BF16)

        def step(items):
            heads = [it[0] for it in items]
            where = [_sb_key_rows(it[1], it[2]) for it in items]
            kjs = [_sb_head(kn_ref, h, rows) for h, (rows, _) in zip(heads, where)]
            zs = [_sb_logits(it[3], kj) for it, kj in zip(items, kjs)]
            dws = [_dot(it[4], _sb_head(v_ref, h, rows), NT) for it, h, (rows, _) in zip(items, heads, where)]
            scored = [_sb_scores(z, it[2] == 0, live, it[6]) for z, it, (_, live) in zip(zs, items, where)]
            afters = [_dot_exact(log_skip, suffix) for log_skip, _, _ in scored]
            wbs = [jnp.where(valid, jnp.exp(log_beta + after + it[7]), 0.0).astype(BF16)
                   for (_, log_beta, valid), after, it in zip(scored, afters, items)]
            gs = [dw * wb.astype(F32) for dw, wb in zip(dws, wbs)]
            befores = [it[5] - (_dot_exact(g, suffix_incl) + it[8]) for g, it in zip(gs, items)]
            dzbs = [jnp.where(valid, g - jnp.exp(log_beta) * (g + before), 0.0).astype(BF16)
                    for (_, log_beta, valid), g, before in zip(scored, gs, befores)]
            dqs = [it[9] + _dot(dzb, kj) for it, dzb, kj in zip(items, dzbs, kjs)]
            for it, h, (rows, _), wb, dzb in zip(items, heads, where, wbs, dzbs):
                cols = slice(h * HEAD_DIM, (h + 1) * HEAD_DIM)
                d_ref[2, 0, rows, cols] += _dot(wb, it[4], TN)
                d_ref[1, 0, rows, cols] += _dot(dzb, it[3], TN)
            cs = [it[7] + jnp.sum(log_skip, axis=1, keepdims=True) for (log_skip, _, _), it in zip(scored, items)]
            cgs = [it[8] + jnp.sum(g, axis=1, keepdims=True) for g, it in zip(gs, items)]
            return cs, cgs, dqs

        def top_items(chains, deltas, done, cs, cgs, dqs):
            return [(h, i, done, _sb_head(qn_ref, h, _sb_rows(i, 0, top)), dob[h, _sb_rows(i, 0, top), :], delta,
                     tri_lt[:top, :], c, cg, dq)
                    for (h, i), delta, c, cg, dq in zip(chains, deltas, cs, cgs, dqs)]

        def finish_tops(chains, deltas, state):
            def k_step(state):
                done, cs, cgs, dqs = state
                cs, cgs, dqs = step(top_items(chains, deltas, done, cs, cgs, dqs))
                return done + 1, tuple(cs), tuple(cgs), tuple(dqs)

            _, _, _, dqs = lax.while_loop(functools.partial(_sb_keys_left, chains, slice(0, top)), k_step, state)
            for (h, i), dq in zip(chains, dqs):
                d_ref[0, 0, _sb_rows(i, 0, top), h * HEAD_DIM:(h + 1) * HEAD_DIM] = dq * scale

        def q_group(m, before):
            chains, chains_before = _sb_chains(m), _sb_chains(m - 1)
            deltas_b, before = before[0], before[1:]
            qis, dois, deltas = [], [], []
            for h, i in chains:
                rows_i = _sb_rows(i, 0, blk)
                qis.append(_sb_head(qn_ref, h, rows_i))
                dois.append(dob[h, rows_i, :])
                deltas.append(jnp.sum(dois[-1].astype(F32) * o_ref[0, rows_i, h * HEAD_DIM:(h + 1) * HEAD_DIM],
                                      axis=1, keepdims=True))
            n = len(chains)

            def items_of(done, cs, cgs, dqs):
                return [(h, i, done, q, do, delta, tri_lt, c, cg, dq)
                        for (h, i), q, do, delta, c, cg, dq in zip(chains, qis, dois, deltas, cs, cgs, dqs)]

            def k_step(state):
                done, cs, cgs, dqs = state
                cs, cgs, dqs = step(items_of(done, cs, cgs, dqs))
                return done + 1, tuple(cs), tuple(cgs), tuple(dqs)

            done_b, cs_b, cgs_b, dqs_b = before
            zero = (jnp.zeros((blk, 1), F32),) * n
            new = step(items_of(jnp.int32(0), zero, zero, (jnp.zeros((blk, HEAD_DIM), F32),) * n)
                       + top_items(chains_before, deltas_b, done_b, cs_b, cgs_b, dqs_b))
            done, cs, cgs, dqs = lax.while_loop(functools.partial(_sb_keys_left, chains, slice(top, blk)), k_step,
                                                (jnp.int32(1),) + tuple(tuple(x[:n]) for x in new))

            @pl.when(m > 0)
            def _():
                finish_tops(chains_before, deltas_b, (done_b + 1,) + tuple(tuple(x[n:]) for x in new))

            for (h, i), dq in zip(chains, dqs):
                d_ref[0, 0, _sb_rows(i, top, blk - top), h * HEAD_DIM:(h + 1) * HEAD_DIM] = dq[top:, :] * scale
            first = lambda xs: tuple(x[:top, :] for x in xs)
            return first(deltas), done, first(cs), first(cgs), first(dqs)

        n = SB_HEADS_PER_STEP * SB_QBLOCKS_PER_STEP
        zero = (jnp.zeros((top, 1), F32),) * n
        nothing = (zero, jnp.int32(0), zero, zero, (jnp.zeros((top, HEAD_DIM), F32),) * n)
        last = lax.fori_loop(0, ngroups, q_group, nothing)
        finish_tops(_sb_chains(ngroups - 1), last[0], last[1:])

        def norm_block(i, carry):
            rows = pl.ds(pl.multiple_of(i * blk, blk), blk)
            out = []
            for h in range(SB_HEADS_PER_STEP):
                cols = slice(h * HEAD_DIM, (h + 1) * HEAD_DIM)
                for part, src_ref, gain_ref in ((0, q_ref, qg_ref), (1, k_ref, kg_ref)):
                    dy = d_ref[part, 0, rows, cols] * (scale if part == 1 else 1.0)
                    dx, pg = _rms_bwd(src_ref[0, 0, rows, cols], gain_ref[...], dy)
                    d_ref[part, 0, rows, cols] = dx
                    out.append(carry[len(out)] + jnp.sum(pg, axis=0, keepdims=True))
            return tuple(out)

        sums = lax.fori_loop(0, nq, norm_block, (jnp.zeros((1, HEAD_DIM), F32),) * (2 * SB_HEADS_PER_STEP))
        for h in range(SB_HEADS_PER_STEP):
            dqg_ref[0, h] = sums[2 * h]
            dkg_ref[0, h] = sums[2 * h + 1]

    head_row = jax.ShapeDtypeStruct((b, N_HEADS, 1, HEAD_DIM), F32)
    return _pcall_riding(
        body, rider, (qkv4, qkv4, qkv4, proj4, proj4, do3, o3, q_gain, k_gain, dproj4),
        name="sb_attn_bwd", grid=(b, N_HEADS // SB_HEADS_PER_STEP),
        in_specs=[_sb_group_spec(s, 0), _sb_group_spec(s, 1), _sb_group_spec(s, 2),
                  _sb_group_spec(s, 0), _sb_group_spec(s, 1),
                  _sb_seq_group_spec(s), _sb_seq_group_spec(s), _GAIN_SPEC, _GAIN_SPEC,
                  pl.BlockSpec(memory_space=pl.ANY)],
        out_specs=[pl.BlockSpec((3, 1, s, SB_GROUP_COLS), lambda b, g: (0, b, 0, g)),
                   _SB_GROUP_ROW_SPEC, _SB_GROUP_ROW_SPEC],
        out_shape=[jax.ShapeDtypeStruct(dproj4.shape, F32), head_row, head_row],
        scratch_shapes=[pltpu.VMEM((SB_HEADS_PER_STEP, s, HEAD_DIM), BF16)],
        input_output_aliases={9: 0}, semantics=("parallel", "parallel"))


def hgrn2_bwd(proj4, don3, oraw3, states, lbl4, o_gain, dproj4, rider=None):
    _, b, s, _ = proj4.shape
    nchunk = s // HG_CHUNK
    c = HG_CHUNK
    subs = range(HG_CHUNK // HG_SUB)
    unroll = math.gcd(nchunk, HG_UNROLL_BWD)
    ngroup = nchunk // unroll

    def body(q_ref, f_ref, i_ref, don_ref, oraw_ref, st_ref, lbl_ref, og_ref, _alias, d_ref, dog_ref, dlb_ref):
        incl, lower, before_sub, upper = _hg_masks()
        lb, oml = _hg_lower_bound(lbl_ref)
        last_row = lax.broadcasted_iota(jnp.int32, (c, HEAD_DIM), 0) == c - 1

        def group(m, carry):
            dst, dog_acc, dlb_acc = carry
            ns, rows = _hg_group_rows(ngroup - 1 - m, unroll)
            ns, rows = ns[::-1], rows[::-1]
            vs = [_hg_gates(q_ref[0, 0, r, :], f_ref[0, 0, r, :], lb, oml) for r in rows]
            inps = [i_ref[0, 0, r, :].astype(BF16) for r in rows]
            sts = [st_ref[0, 0, n] for n in ns]
            gcs = [_dot_exact(v["logf"], lower, left=True) for v in vs]
            grs = [_dot_exact(v["logf"], before_sub, left=True) for v in vs]
            e_qs = [jnp.exp(gc - gr) for gc, gr in zip(gcs, grs)]
            a_s, qdbs, ksbs, ess = _hg_intra([v["q"] * e for v, e in zip(vs, e_qs)], [v["k"] for v in vs],
                                             gcs, grs, incl)
            e_gcs = [jnp.exp(gc) for gc in gcs]
            gls = [gc[c - 1:c, :] for gc in gcs]
            e_gls = [jnp.exp(gl) for gl in gls]
            e_ks = [jnp.exp(gl - gc) for gl, gc in zip(gls, gcs)]
            normed = [_rms_bwd(oraw_ref[0, r, :], og_ref[...], don_ref[0, r, :]) for r in rows]
            dobs = [do.astype(BF16) for do, _ in normed]
            dabs = [jnp.where(incl, _dot(dob, inp, NT), 0.0).astype(BF16) for dob, inp in zip(dobs, inps)]
            adds = [_dot(dob, (v["q"] * e).astype(BF16), TN) for dob, v, e in zip(dobs, vs, e_gcs)]
            dq_inters = [_dot(dob, st.astype(BF16)) * e for dob, st, e in zip(dobs, sts, e_gcs)]
            dqds = [jnp.concatenate([_dot(dab[sub * HG_SUB:(sub + 1) * HG_SUB, :], ksb[sub]) for sub in subs], axis=0)
                    for dab, ksb in zip(dabs, ksbs)]
            dkss = [[_dot(dab[sub * HG_SUB:(sub + 1) * HG_SUB, :], qdb[sub * HG_SUB:(sub + 1) * HG_SUB, :], TN)
                     for sub in subs] for dab, qdb in zip(dabs, qdbs)]
            dsts = []
            for e_gl, add in zip(e_gls, adds):
                dsts.append(dst)
                dst = dst * e_gl + add
            dstbs = [d.astype(BF16) for d in dsts]
            dis = [_dot(a.astype(BF16), dob, TN) + _dot((v["k"] * e_k).astype(BF16), dstb, NT)
                   for a, dob, v, e_k, dstb in zip(a_s, dobs, vs, e_ks, dstbs)]
            dk_inters = [_dot(inp, dstb) * e_k for inp, dstb, e_k in zip(inps, dstbs, e_ks)]
            dks, dgcs = [], []
            for u in range(len(rows)):
                q, k = vs[u]["q"], vs[u]["k"]
                dk, dgc_k = dk_inters[u], jnp.zeros((c, HEAD_DIM), F32)
                for sub in subs:
                    dk = dk + dkss[u][sub] * ess[u][sub]
                    dgc_k = dgc_k + dkss[u][sub] * ksbs[u][sub].astype(F32)
                at_last = (jnp.sum(k * dk_inters[u], axis=0, keepdims=True)
                           + e_gls[u] * jnp.sum(sts[u] * dsts[u], axis=0, keepdims=True))
                dks.append(dk)
                dgcs.append((qdbs[u].astype(F32) * dqds[u] - dgc_k) + (q * dq_inters[u] - k * dk_inters[u])
                            + jnp.where(last_row, at_last, 0.0))
            dlf_fs = [_dot_exact(dgc, upper, left=True) / v["f"] for dgc, v in zip(dgcs, vs)]
            for u, r in enumerate(rows):
                v = vs[u]
                dq = dqds[u] * e_qs[u] + dq_inters[u]
                d_ref[0, 0, r, :] = dq * (v["sq"] * (1.0 + v["qp"] * (1.0 - v["sq"])))
                d_ref[1, 0, r, :] = (dlf_fs[u] - dks[u]) * (oml * v["sf"] * v["sfn"])
                d_ref[2, 0, r, :] = dis[u]
                dlb_acc = dlb_acc + jnp.sum((dlf_fs[u] - dks[u]) * v["sfn"], axis=0, keepdims=True)
                dog_acc = dog_acc + jnp.sum(normed[u][1], axis=0, keepdims=True)
            return dst, dog_acc, dlb_acc

        zero = jnp.zeros((1, HEAD_DIM), F32)
        _, dog, dlb = lax.fori_loop(0, ngroup, group, (jnp.zeros((HEAD_DIM, HEAD_DIM), F32), zero, zero))
        dog_ref[0, 0] = dog
        dlb_ref[0, 0] = dlb

    head_row = jax.ShapeDtypeStruct((b, N_HEADS, 1, HEAD_DIM), F32)
    return _pcall_riding(
        body, rider, (proj4, proj4, proj4, don3, oraw3, states, lbl4, o_gain, dproj4),
        name="hgrn2_bwd", grid=(b, N_HEADS),
        in_specs=[_head_spec(s, 0), _head_spec(s, 1), _head_spec(s, 2), _seq_spec(s), _seq_spec(s),
                  _state_spec(nchunk), pl.BlockSpec((2, 1, 1, HEAD_DIM), lambda b, h: (0, h, 0, 0)), _GAIN_SPEC,
                  pl.BlockSpec(memory_space=pl.ANY)],
        out_specs=[pl.BlockSpec((3, 1, s, HEAD_DIM), lambda b, h: (0, b, 0, h)), _HEAD_ROW_SPEC, _HEAD_ROW_SPEC],
        out_shape=[jax.ShapeDtypeStruct(dproj4.shape, F32), head_row, head_row],
        input_output_aliases={8: 0}, semantics=("parallel", "parallel"))


def local_step(x, target, sb_norm, wsi, sb_q_gain, sb_k_gain, hg_o_gain, hg_lb_logits, wso_mine, whi_mine, who_mine,
               hg_norm_mine):
    b, s, _ = x.shape
    t = b * s
    x2 = x.reshape(t, D_MODEL)
    tg2 = target.reshape(t, D_MODEL)
    lbl4 = hg_lb_logits.reshape(2, N_HEADS, 1, HEAD_DIM)
    four = (4, b, s, D_MODEL)
    three = (b, s, D_MODEL)
    rows8 = (N_DEV, W_ROWS, D_MODEL)

    (proj0, u0, qkv0), (wso, hgn) = rms_inproj(x2, sb_norm, wsi, "sb_inproj",
                                               _Rider([wso_mine, hg_norm_mine], scatter=False),
                                               qk_gains=(sb_q_gain, sb_k_gain))
    qkv0 = qkv0.reshape(3, b, s, D_MODEL)
    wso = wso.reshape(D_MODEL, D_MODEL)
    hg_norm_full = hgn[:, 0, :].reshape(1, D_MODEL)
    o0, (whi,) = sb_attn_fwd(qkv0, _Rider([whi_mine], scatter=False))
    o0 = o0.reshape(t, D_MODEL)
    h1 = gate_outproj(o0, proj0, wso, x2, None, "sb_outproj")
    (proj1, u1), _ = rms_inproj(h1, hg_norm_full, whi, "hg_inproj")
    (o1, o1_raw, states), (who,) = hgrn2_fwd(proj1.reshape(four), lbl4, hg_o_gain,
                                             _Rider([who_mine], scatter=False))
    who = who.reshape(D_MODEL, D_MODEL)
    o1 = o1.reshape(t, D_MODEL)
    dh2, loss_parts = gate_outproj(o1, proj1, who, h1, tg2, "hg_outproj_loss")

    do1, dproj1, g_who = outproj_bwd(dh2, who, o1, proj1, "hg_outproj_bwd")
    (dproj1, g_og, g_lb), (p_who,) = hgrn2_bwd(proj1.reshape(four), do1.reshape(three), o1_raw, states, lbl4,
                                               hg_o_gain, dproj1.reshape(four),
                                               _Rider([g_who.reshape(rows8)], scatter=True))
    dproj1 = dproj1.reshape(4, t, D_MODEL)
    (dh1, g_hgn), _ = inproj_bwd_dx(dproj1, whi, h1, hg_norm_full, dh2, "hg_inproj_bwd_dx")
    g_whi = inproj_bwd_dw(u1, dproj1, "hg_inproj_bwd_dw", out_dtype=BF16)

    do0, dproj0, g_wso = outproj_bwd(dh1, wso, o0, proj0, "sb_outproj_bwd")
    (dproj0, g_qg, g_kg), (p_whi, p_wso) = sb_attn_bwd(qkv0, proj0.reshape(four), do0.reshape(three), o0.reshape(three),
                                                       sb_q_gain, sb_k_gain, dproj0.reshape(four),
                                                       _Rider([g_whi, g_wso.reshape(rows8)], scatter=True))
    dproj0 = dproj0.reshape(4, t, D_MODEL)
    g_wsi = inproj_bwd_dw(u0, dproj0, "sb_inproj_bwd_dw", out_dtype=BF16)
    (gx, g_sbn), (p_wsi,) = inproj_bwd_dx(dproj0, wsi, x2, sb_norm, dh1, "sb_inproj_bwd_dx",
                                          _Rider([sum_within_chip(g_wsi)], scatter=True, chips=True))
    return dict(loss_parts=loss_parts, gx=gx.reshape(three), p_wsi=p_wsi, p_wso=p_wso, p_whi=p_whi, p_who=p_who,
                g_sbn=g_sbn, g_hgn=g_hgn, g_qg=g_qg, g_kg=g_kg, g_og=g_og, g_lb=g_lb)


def sum_within_chip(g):
    _, r, c_ = g.shape
    chips = N_DEV // 2

    def swap(g_ref, got_ref, send_sems, recv_sems):
        x, y, c = _mesh_pos()
        copies = [pltpu.make_async_remote_copy(
            src_ref=g_ref.at[2 * q + 1 - c], dst_ref=got_ref.at[q], send_sem=send_sems.at[q], recv_sem=recv_sems.at[q],
            device_id=(x, y, 1 - c), device_id_type=MESH) for q in range(chips)]
        for cp in copies:
            cp.start()
        for cp in copies:
            cp.wait_recv()
            cp.wait_send()

    got = _pcall(swap, name="swap_with_sibling", in_specs=[_ANY], out_specs=_ANY,
                 out_shape=jax.ShapeDtypeStruct((chips, r, c_), g.dtype),
                 scratch_shapes=[pltpu.SemaphoreType.DMA((chips,)), pltpu.SemaphoreType.DMA((chips,))])(g)

    def add(g_ref, got_ref, out_ref):
        mine = g_ref[0, lax.axis_index("c")]
        out_ref[0] = (mine.astype(F32) + got_ref[0].astype(F32)).astype(out_ref.dtype)

    return _pcall(
        add, name="add_sibling_partials", grid=(chips,),
        in_specs=[pl.BlockSpec((1, 2, r, c_), lambda q: (q, 0, 0, 0)), pl.BlockSpec((1, r, c_), lambda q: (q, 0, 0))],
        out_specs=pl.BlockSpec((1, r, c_), lambda q: (q, 0, 0)),
        out_shape=jax.ShapeDtypeStruct((chips, r, c_), g.dtype),
        compiler_params=_params("parallel"),
    )(g.reshape(chips, 2, r, c_), got)


def _two_level_gather(src, out, send_sems, recv_sems, local_sem, pos, meanwhile):
    x, y, c = pos
    me, sibling = (x, y, c), (x, y, 1 - c)
    chips = [(1 - x, y), (x, 1 - y), (1 - x, 1 - y)]

    def copy(k, block, to, source=None):
        slot = out.at[_linear(block)]
        return pltpu.make_async_remote_copy(
            src_ref=slot if source is None else source, dst_ref=slot, send_sem=send_sems.at[k],
            recv_sem=recv_sems.at[k], device_id=to, device_id_type=MESH)

    mine = pltpu.make_async_copy(src, out.at[_linear(me)], local_sem)
    mine.start()
    first = [copy(0, me, sibling, src)] + [copy(1 + j, me, (*chip, c), src) for j, chip in enumerate(chips)]
    for cp in first:
        cp.start()
    meanwhile()
    passed = [copy(4 + j, (*chip, c), sibling) for j, chip in enumerate(chips)]
    for j, chip in enumerate(chips):
        copy(1 + j, (*chip, c), me).wait_recv()
        passed[j].start()
    copy(0, sibling, me).wait_recv()
    for j, chip in enumerate(chips):
        copy(4 + j, (*chip, 1 - c), me).wait_recv()
    for cp in first + passed:
        cp.wait_send()
    mine.wait()


def gather_first_weights(w_si, w_so, w_hi, w_ho, hg_norm):
    def body(si_ref, so_ref, hi_ref, ho_ref, hn_ref, o_si, so_b, hi_b, ho_b, hn_b, si_b, send_sems, recv_sems, local_sem):
        si_b[...] = si_ref[...].astype(BF16)

        def cast_the_rest():
            for src, buf in ((so_ref, so_b), (hi_ref, hi_b), (ho_ref, ho_b)):
                buf[...] = src[...].astype(BF16)
            hn_b[...] = jnp.broadcast_to(hn_ref[...], hn_b.shape)

        _two_level_gather(si_b, o_si, send_sems, recv_sems, local_sem, _mesh_pos(), cast_the_rest)

    return _pcall(
        body, name="gather_first_weights",
        in_specs=[_VMEM] * 5, out_specs=[_ANY] + [_VMEM] * 4,
        out_shape=[jax.ShapeDtypeStruct((N_DEV,) + w_si.shape, BF16), jax.ShapeDtypeStruct(w_so.shape, BF16),
                   jax.ShapeDtypeStruct(w_hi.shape, BF16), jax.ShapeDtypeStruct(w_ho.shape, BF16),
                   jax.ShapeDtypeStruct((8, HEAD_DIM), F32)],
        scratch_shapes=[pltpu.VMEM(w_si.shape, BF16), pltpu.SemaphoreType.DMA((N_PEERS,)),
                        pltpu.SemaphoreType.DMA((N_PEERS,)), pltpu.SemaphoreType.DMA],
        compiler_params=pltpu.CompilerParams(vmem_limit_bytes=VMEM_LIMIT_BYTES),
    )(w_si, w_so, w_hi, w_ho, hg_norm)


def _adamw(w, g, m, v):
    m = ADAM_B1 * m + (1.0 - ADAM_B1) * g
    v = ADAM_B2 * v + (1.0 - ADAM_B2) * (g * g)
    m_hat = m / (1.0 - ADAM_B1 ** ADAM_STEP)
    v_hat = v / (1.0 - ADAM_B2 ** ADAM_STEP)
    delta = -ADAM_LR * (m_hat / (jnp.sqrt(v_hat) + ADAM_EPS) + ADAM_WD * w)
    return delta, m, v


def reduce_adamw(parts, w, m, v, name):
    n, r, c = parts.shape
    tr = _row_tile(r, 256)

    def body(p_ref, w_ref, m_ref, v_ref, g_ref, d_ref, m2_ref, v2_ref):
        g = p_ref[0].astype(F32)
        for sender in range(1, n):
            g = g + p_ref[sender].astype(F32)
        g_ref[0] = g
        d_ref[0], m2_ref[0], v2_ref[0] = _adamw(w_ref[0], g, m_ref[0], v_ref[0])

    tile = pl.BlockSpec((1, tr, c), lambda i: (0, i, 0))
    return _pcall(
        body, name=name, grid=(r // tr,),
        in_specs=[pl.BlockSpec((n, tr, c), lambda i: (0, i, 0)), tile, tile, tile],
        out_specs=[tile] * 4, out_shape=[jax.ShapeDtypeStruct((1, r, c), F32)] * 4,
        compiler_params=_params("parallel"),
    )(parts, w, m, v)


PACK_ROWS = 32
ROW_SBN, ROW_HGN, ROW_LB, ROW_QG, ROW_KG, ROW_OG, ROW_LOSS = 0, 8, 16, 24, 25, 26, 27


def small_update(g_sbn, g_hgn, g_lb, g_qg, g_kg, g_og, loss_parts, small):
    def gather(sbn_ref, hgn_ref, lb_ref, qg_ref, kg_ref, og_ref, loss_ref, gath, pack, send_sems, recv_sems, local_sems):
        pos = _mesh_pos()
        pack[...] = jnp.zeros_like(pack)
        for h in range(D_MODEL // HEAD_DIM):
            lanes = slice(h * HEAD_DIM, (h + 1) * HEAD_DIM)
            pack[ROW_SBN + h:ROW_SBN + h + 1, :] = sbn_ref[0:1, lanes]
            pack[ROW_HGN + h:ROW_HGN + h + 1, :] = hgn_ref[0:1, lanes]
        pack[ROW_LB:ROW_LB + 8, :] = jnp.sum(lb_ref[...], axis=0)
        pack[ROW_QG:ROW_QG + 1, :] = jnp.sum(qg_ref[...], axis=0, keepdims=True)
        pack[ROW_KG:ROW_KG + 1, :] = jnp.sum(kg_ref[...], axis=0, keepdims=True)
        pack[ROW_OG:ROW_OG + 1, :] = jnp.sum(og_ref[...], axis=0, keepdims=True)
        pack[ROW_LOSS:ROW_LOSS + 1, :] = jnp.sum(loss_ref[...], axis=0)[0:1, :]
        _exchange_start([((lambda p: pack), gath)], send_sems, recv_sems, local_sems, pos)
        _exchange_wait([((lambda p: pack), gath)], send_sems, recv_sems, local_sems, pos)

    packs = _pcall(
        gather, name="small_gather",
        in_specs=[_VMEM] * 7, out_specs=_VMEM, out_shape=jax.ShapeDtypeStruct((N_DEV, PACK_ROWS, HEAD_DIM), F32),
        scratch_shapes=[pltpu.VMEM((PACK_ROWS, HEAD_DIM), F32)] + _exchange_sems(1),
    )(g_sbn, g_hgn, g_lb, g_qg, g_kg, g_og, loss_parts)

    def apply(gath, *refs):
        wmv, outs, tot = refs[:len(small)], refs[len(small):-1], refs[-1]
        me = _linear(_mesh_pos())
        total = gath[0]
        for dev in range(1, N_DEV):
            total = total + gath[dev]
        tot[...] = total
        outs[0][...] = jnp.broadcast_to(tot[ROW_LOSS:ROW_LOSS + 1, :], (8, HEAD_DIM))
        g_norm_ref, logits_ref, g_logits_ref = outs[1], wmv[15], outs[1 + 4 * 5]
        for h in range(N_HEADS):
            lanes = slice(h * HEAD_DIM, (h + 1) * HEAD_DIM)
            g_norm_ref[0:1, lanes] = tot[ROW_SBN + h:ROW_SBN + h + 1, :]
            p1, p0 = _sigmoid_pair(logits_ref[1:2, lanes] - logits_ref[0:1, lanes])
            d_l1 = p0 * p1 * tot[ROW_LB + h:ROW_LB + h + 1, :]
            g_logits_ref[0:1, lanes] = -d_l1
            g_logits_ref[1:2, lanes] = d_l1
        grads = [g_norm_ref[...], tot[ROW_QG:ROW_QG + 1, :], tot[ROW_KG:ROW_KG + 1, :],
                 tot[pl.ds(ROW_HGN + me, 1), :], tot[ROW_OG:ROW_OG + 1, :], g_logits_ref[...]]
        for i, g in enumerate(grads):
            w_ref, m_ref, v_ref = wmv[3 * i:3 * i + 3]
            o = outs[1 + 4 * i:5 + 4 * i]
            if 0 < i < 5:
                o[0][...] = g
            o[1][...], o[2][...], o[3][...] = _adamw(w_ref[...], g, m_ref[...], v_ref[...])

    out_shape = [jax.ShapeDtypeStruct((8, HEAD_DIM), F32)]
    for i in range(6):
        out_shape += [jax.ShapeDtypeStruct(small[3 * i].shape, F32)] * 4
    return _pcall(
        apply, name="small_update",
        in_specs=[_VMEM] * (1 + len(small)), out_specs=[_VMEM] * 25, out_shape=out_shape,
        scratch_shapes=[pltpu.VMEM((PACK_ROWS, HEAD_DIM), F32)],
    )(packs, *small)


def kernel(x, sb_norm, sb_w_in, sb_q_gain, sb_k_gain, sb_w_out, hg_norm, hg_w_in, hg_o_gain, hg_w_out, hg_lb_logits, loss_target, m_sb_norm, m_sb_w_in, m_sb_q_gain, m_sb_k_gain, m_sb_w_out, m_hg_norm, m_hg_w_in, m_hg_o_gain, m_hg_w_out, m_hg_lb_logits, v_sb_norm, v_sb_w_in, v_sb_q_gain, v_sb_k_gain, v_sb_w_out, v_hg_norm, v_hg_w_in, v_hg_o_gain, v_hg_w_out, v_hg_lb_logits):
    b = x.shape[0]
    wsi, wso_mine, whi_mine, who_mine, hg_norm_mine = gather_first_weights(
        sb_w_in[0], sb_w_out[0], hg_w_in[0], hg_w_out[0], hg_norm)
    r = local_step(x, loss_target, sb_norm, wsi, sb_q_gain, sb_k_gain, hg_o_gain, hg_lb_logits,
                   wso_mine, whi_mine, who_mine, hg_norm_mine)
    big = {}
    for name, p, w, m, v in (("sb_w_in", r["p_wsi"], sb_w_in, m_sb_w_in, v_sb_w_in),
                             ("sb_w_out", r["p_wso"], sb_w_out, m_sb_w_out, v_sb_w_out),
                             ("hg_w_in", r["p_whi"], hg_w_in, m_hg_w_in, v_hg_w_in),
                             ("hg_w_out", r["p_who"], hg_w_out, m_hg_w_out, v_hg_w_out)):
        big[name] = reduce_adamw(p, w, m, v, "adamw_" + name)

    small_in = [sb_norm, m_sb_norm, v_sb_norm,
                sb_q_gain, m_sb_q_gain, v_sb_q_gain,
                sb_k_gain, m_sb_k_gain, v_sb_k_gain,
                hg_norm, m_hg_norm, v_hg_norm,
                hg_o_gain, m_hg_o_gain, v_hg_o_gain,
                hg_lb_logits, m_hg_lb_logits, v_hg_lb_logits]
    so = small_update(r["g_sbn"], r["g_hgn"], r["g_lb"].reshape(b, N_HEADS, HEAD_DIM),
                      r["g_qg"].reshape(b * N_HEADS, HEAD_DIM), r["g_kg"].reshape(b * N_HEADS, HEAD_DIM),
                      r["g_og"].reshape(b * N_HEADS, HEAD_DIM), r["loss_parts"], small_in)
    loss = so[0][0, 0]
    small = {}
    for i, name in enumerate(("sb_norm", "sb_q_gain", "sb_k_gain", "hg_norm", "hg_o_gain", "hg_lb_logits")):
        small[name] = so[1 + 4 * i:5 + 4 * i]
    order = ("sb_norm", "sb_w_in", "sb_q_gain", "sb_k_gain", "sb_w_out",
             "hg_norm", "hg_w_in", "hg_o_gain", "hg_w_out", "hg_lb_logits")
    res = {**big, **small}
    return (loss, r["gx"]) + tuple(res[n][j] for j in range(4) for n in order)
```

```python
import functools
import math

import jax
import jax.numpy as jnp
from jax import lax
from jax.experimental import pallas as pl
from jax.experimental.pallas import tpu as pltpu

F32 = jnp.float32
BF16 = jnp.bfloat16

N_DEV = 8
D_MODEL = 1024
N_HEADS = 8
HEAD_DIM = 128
RMS_EPS = 1e-6
ATTN_BLOCK = 128
HG_CHUNK = 64
HG_SUB = 16
HG_UNROLL_FWD = 16
HG_UNROLL_BWD = 16
EXP_CLAMP = 80.0
SB_HEADS_PER_STEP = 2
SB_QBLOCKS_PER_STEP = 4
SB_GROUP_COLS = SB_HEADS_PER_STEP * 128
SB_TOP_ROWS = 32
SB_LOG_WEIGHT_FLOOR = -104.0
VMEM_LIMIT_BYTES = 48 * 1024 * 1024
W_COLS = 4 * D_MODEL // N_DEV
W_ROWS = D_MODEL // N_DEV

ADAM_LR = 0.001
ADAM_B1 = 0.9
ADAM_B2 = 0.999
ADAM_EPS = 1e-08
ADAM_WD = 0.01
ADAM_STEP = 10

NT = (((1,), (1,)), ((), ()))
TN = (((0,), (0,)), ((), ()))
NN = (((1,), (0,)), ((), ()))


def _pcall(body, *, name, **kw):
    return pl.pallas_call(body, name=name, **kw)


def _params(*sem):
    return pltpu.CompilerParams(dimension_semantics=sem, vmem_limit_bytes=VMEM_LIMIT_BYTES)


def _dot(a, b, dims=NN):
    return lax.dot_general(a, b, dims, preferred_element_type=F32)


def _dot_exact(a, m, dims=NN, left=False):
    hi = a.astype(BF16)
    lo = (a - hi.astype(F32)).astype(BF16)
    if left:
        return _dot(m, hi, dims) + _dot(m, lo, dims)
    return _dot(hi, m, dims) + _dot(lo, m, dims)


def _sigmoid(x):
    return 1.0 / (1.0 + jnp.exp(-x))


def _sigmoid_pair(x):
    e = jnp.exp(-jnp.abs(x))
    big = 1.0 / (1.0 + e)
    small = e * big
    pos = x >= 0
    return jnp.where(pos, big, small), jnp.where(pos, small, big)


def _rms_scale(x):
    return lax.rsqrt(jnp.mean(x * x, axis=-1, keepdims=True) + RMS_EPS)


def _row_tile(t, want):
    return want if t % want == 0 else t


MESH = pl.DeviceIdType.MESH
N_PEERS = N_DEV - 1
_ANY = pl.BlockSpec(memory_space=pl.ANY)
_VMEM = pl.BlockSpec(memory_space=pltpu.VMEM)


def _mesh_pos():
    return lax.axis_index("x"), lax.axis_index("y"), lax.axis_index("c")


def _linear(pos):
    return 4 * pos[0] + 2 * pos[1] + pos[2]


def _peer(pos, k):
    flips = ((k + 1) >> 2 & 1, (k + 1) >> 1 & 1, (k + 1) & 1)
    return tuple(1 - p if f else p for p, f in zip(pos, flips))


def _chip(pos):
    return 2 * pos[0] + pos[1]


ALL_PEERS = tuple(range(N_PEERS))
SAME_CORE_PEERS = (1, 3, 5)


def _exchange_copies(pairs, send_sems, recv_sems, local_sems, pos, landing, peers, slot):
    me = slot(pos)
    local, remote = [], []
    for a, (src_of, dst) in enumerate(pairs):
        local.append(pltpu.make_async_copy(src_of(pos), dst.at[me], local_sems.at[a]))
        for k in peers:
            peer = _peer(pos, k)
            remote.append(pltpu.make_async_remote_copy(
                src_ref=src_of(peer), dst_ref=dst.at[slot(peer) if landing else me],
                send_sem=send_sems.at[a, k], recv_sem=recv_sems.at[a, k], device_id=peer, device_id_type=MESH))
    return local, remote


def _exchange_start(pairs, send_sems, recv_sems, local_sems, pos, peers=ALL_PEERS, slot=_linear):
    local, sent = _exchange_copies(pairs, send_sems, recv_sems, local_sems, pos, False, peers, slot)
    for copy in local + sent:
        copy.start()


def _exchange_wait(pairs, send_sems, recv_sems, local_sems, pos, peers=ALL_PEERS, slot=_linear):
    local, landed = _exchange_copies(pairs, send_sems, recv_sems, local_sems, pos, True, peers, slot)
    for copy in landed:
        copy.wait_recv()
        copy.wait_send()
    for copy in local:
        copy.wait()


def _exchange_sems(n):
    return [pltpu.SemaphoreType.DMA((n, N_PEERS)), pltpu.SemaphoreType.DMA((n, N_PEERS)),
            pltpu.SemaphoreType.DMA((n,))]


class _Rider:
    def __init__(self, arrays, scatter, chips=False):
        self.arrays = list(arrays)
        self.scatter = scatter
        self.peers = SAME_CORE_PEERS if chips else ALL_PEERS
        self.slot = _chip if chips else _linear
        self.out_shapes = [jax.ShapeDtypeStruct(a.shape if scatter else (N_DEV,) + a.shape, a.dtype)
                           for a in self.arrays]

    def pairs(self, in_refs, out_refs):
        if self.scatter:
            return [((lambda pos, r=r: r.at[self.slot(pos)]), o) for r, o in zip(in_refs, out_refs)]
        return [((lambda pos, r=r: r), o) for r, o in zip(in_refs, out_refs)]


def _pcall_riding(body, rider, args, *, name, grid, in_specs, out_specs, out_shape, semantics, scratch_shapes=(),
                  input_output_aliases=None):
    aliases = input_output_aliases or {}
    if rider is None:
        outs = _pcall(body, name=name, grid=grid, in_specs=list(in_specs), out_specs=list(out_specs),
                      out_shape=list(out_shape), scratch_shapes=list(scratch_shapes), input_output_aliases=aliases,
                      compiler_params=_params(*semantics))(*args)
        return list(outs), []
    n_in, n_out, n_scr, n_r = len(in_specs), len(out_specs), len(scratch_shapes), len(rider.arrays)

    def riding(*refs):
        ins, refs = refs[:n_in], refs[n_in:]
        rider_in, refs = refs[:n_r], refs[n_r:]
        outs, refs = refs[:n_out], refs[n_out:]
        rider_out, refs = refs[:n_r], refs[n_r:]
        scratch, sems = refs[:n_scr], refs[n_scr:]
        pairs = rider.pairs(rider_in, rider_out)
        first = functools.reduce(jnp.logical_and, [pl.program_id(a) == 0 for a in range(len(grid))])
        last = functools.reduce(jnp.logical_and, [pl.program_id(a) == g - 1 for a, g in enumerate(grid)])

        @pl.when(first)
        def _():
            _exchange_start(pairs, *sems, _mesh_pos(), rider.peers, rider.slot)

        body(*ins, *outs, *scratch)

        @pl.when(last)
        def _():
            _exchange_wait(pairs, *sems, _mesh_pos(), rider.peers, rider.slot)

    outs = _pcall(riding, name=name, grid=grid, in_specs=list(in_specs) + [_ANY] * n_r,
                  out_specs=list(out_specs) + [_ANY] * n_r, out_shape=list(out_shape) + rider.out_shapes,
                  scratch_shapes=list(scratch_shapes) + _exchange_sems(n_r), input_output_aliases=aliases,
                  compiler_params=_params(*(("arbitrary",) * len(grid))))(*args, *rider.arrays)
    return list(outs[:n_out]), list(outs[n_out:])


def rms_inproj(x2, gain, wg, name, rider=None, qk_gains=None):
    t = x2.shape[0]
    with_qkv = qk_gains is not None
    tm = _row_tile(t, 256 if with_qkv else 512)

    def body(x_ref, g_ref, w_ref, *rest):
        if with_qkv:
            qg_ref, kg_ref, proj_ref, ut_ref, qkv_ref = rest
            head_gain = (qg_ref, kg_ref)
        else:
            proj_ref, ut_ref = rest
        x = x_ref[...]
        u = x * _rms_scale(x) * g_ref[...]
        ut_ref[...] = u.T.astype(BF16)
        u = u.astype(BF16)
        for p in range(N_DEV):
            part, lo = p // 2, (p % 2) * W_COLS
            res = _dot(u, w_ref[p])
            proj_ref[part, :, lo:lo + W_COLS] = res
            if with_qkv and part < 3:
                for h in range(W_COLS // HEAD_DIM):
                    y = res[:, h * HEAD_DIM:(h + 1) * HEAD_DIM]
                    if part < 2:
                        y = y * _rms_scale(y) * head_gain[part][...]
                    qkv_ref[part, :, lo + h * HEAD_DIM:lo + (h + 1) * HEAD_DIM] = y.astype(BF16)

    vec = pl.BlockSpec((1, D_MODEL), lambda i: (0, 0))
    in_specs = [pl.BlockSpec((tm, D_MODEL), lambda i: (i, 0)), vec,
                pl.BlockSpec((N_DEV, D_MODEL, W_COLS), lambda i: (0, 0, 0))]
    out_specs = [pl.BlockSpec((4, tm, D_MODEL), lambda i: (0, i, 0)), pl.BlockSpec((D_MODEL, tm), lambda i: (0, i))]
    out_shape = [jax.ShapeDtypeStruct((4, t, D_MODEL), F32), jax.ShapeDtypeStruct((D_MODEL, t), BF16)]
    args = (x2, gain, wg)
    if with_qkv:
        in_specs += [pl.BlockSpec((1, HEAD_DIM), lambda i: (0, 0))] * 2
        out_specs.append(pl.BlockSpec((3, tm, D_MODEL), lambda i: (0, i, 0)))
        out_shape.append(jax.ShapeDtypeStruct((3, t, D_MODEL), BF16))
        args += tuple(qk_gains)
    return _pcall_riding(body, rider, args, name=name, grid=(t // tm,), in_specs=in_specs, out_specs=out_specs,
                         out_shape=out_shape, semantics=("parallel",))


def gate_outproj(o2, proj, w_out, resid, target, name):
    t = o2.shape[0]
    tm = _row_tile(t, 512)
    with_loss = target is not None

    def body(o_ref, gate_ref, w_ref, r_ref, *rest):
        g = gate_ref[0]
        og = (o_ref[...] * (g * _sigmoid(g))).astype(BF16)
        h = r_ref[...] + _dot(og, w_ref[...])
        if with_loss:
            t_ref, dh_ref, loss_ref = rest
            err = h - t_ref[...]
            dh_ref[...] = err * (1.0 / D_MODEL)
            part = 0.5 * jnp.sum(jnp.mean(err * err, axis=-1, keepdims=True))
            loss_ref[...] = jnp.full(loss_ref.shape, part, F32)
        else:
            (h_ref,) = rest
            h_ref[...] = h

    row = pl.BlockSpec((tm, D_MODEL), lambda i: (i, 0))
    in_specs = [row,
                pl.BlockSpec((1, tm, D_MODEL), lambda i: (3, i, 0)),
                pl.BlockSpec((D_MODEL, D_MODEL), lambda i: (0, 0)),
                row]
    args = [o2, proj, w_out, resid]
    if with_loss:
        in_specs.append(row)
        args.append(target)
        out_specs = [row, pl.BlockSpec((1, 8, 128), lambda i: (i, 0, 0))]
        out_shape = [jax.ShapeDtypeStruct((t, D_MODEL), F32),
                     jax.ShapeDtypeStruct((t // tm, 8, 128), F32)]
    else:
        out_specs = row
        out_shape = jax.ShapeDtypeStruct((t, D_MODEL), F32)
    return _pcall(body, name=name, grid=(t // tm,), in_specs=in_specs, out_specs=out_specs,
                  out_shape=out_shape, compiler_params=_params("parallel"))(*args)


def outproj_residual(og2, w_out, resid, name):
    t = og2.shape[0]
    tm = _row_tile(t, 512)

    def body(og_ref, w_ref, r_ref, h_ref):
        h_ref[...] = r_ref[...] + _dot(og_ref[...], w_ref[...])

    row = pl.BlockSpec((tm, D_MODEL), lambda i: (i, 0))
    return _pcall(body, name=name, grid=(t // tm,),
                  in_specs=[row, pl.BlockSpec((D_MODEL, D_MODEL), lambda i: (0, 0)), row], out_specs=row,
                  out_shape=jax.ShapeDtypeStruct((t, D_MODEL), F32), compiler_params=_params("parallel"))(og2, w_out, resid)


def _head_spec(s, part):
    return pl.BlockSpec((1, 1, s, HEAD_DIM), lambda b, h: (part, b, 0, h))


def _seq_spec(s):
    return pl.BlockSpec((1, s, HEAD_DIM), lambda b, h: (b, 0, h))


_GAIN_SPEC = pl.BlockSpec((1, HEAD_DIM), lambda b, h: (0, 0))
_HEAD_ROW_SPEC = pl.BlockSpec((1, 1, 1, HEAD_DIM), lambda b, h: (b, h, 0, 0))


def _sb_group_spec(s, part):
    return pl.BlockSpec((1, 1, s, SB_GROUP_COLS), lambda b, g: (part, b, 0, g))


def _sb_seq_group_spec(s):
    return pl.BlockSpec((1, s, SB_GROUP_COLS), lambda b, g: (b, 0, g))


_SB_GROUP_ROW_SPEC = pl.BlockSpec((1, SB_HEADS_PER_STEP, 1, HEAD_DIM), lambda b, g: (b, g, 0, 0))


def _sb_chains(m):
    return [(h, m * SB_QBLOCKS_PER_STEP + r) for h in range(SB_HEADS_PER_STEP) for r in range(SB_QBLOCKS_PER_STEP)]


def _sb_logits(qi, kj):
    return _dot(qi, kj, NT) * (HEAD_DIM ** -0.5)


def _sb_scores(z, diag, live, tri_lt):
    soft = jnp.log(1.0 + jnp.exp(-jnp.abs(z)))
    valid = jnp.logical_and(live, jnp.logical_or(jnp.logical_not(diag), tri_lt))
    log_skip = jnp.where(valid, -(jnp.maximum(z, 0.0) + soft), 0.0)
    log_beta = jnp.minimum(z, 0.0) - soft
    return log_skip, log_beta, valid


def _sb_keys_left(chains, watch, state):
    done, carries = state[0], state[1]
    worst = None
    for (_, i), c in zip(chains, carries):
        c = jnp.where(done <= i, c[watch], -jnp.inf)
        worst = c if worst is None else jnp.maximum(worst, c)
    return jnp.logical_and(done <= chains[-1][1],
                           jnp.logical_or(done == 0, jnp.max(worst) > SB_LOG_WEIGHT_FLOOR))


def _sb_key_rows(i, done):
    j = i - done
    return pl.ds(pl.multiple_of(jnp.maximum(j, 0) * ATTN_BLOCK, ATTN_BLOCK), ATTN_BLOCK), j >= 0


def _sb_head(ref, h, rows):
    return ref[0, 0, rows, h * HEAD_DIM:(h + 1) * HEAD_DIM]


def _sb_rows(i, offset, count):
    return pl.ds(pl.multiple_of(jnp.maximum(i, 0) * ATTN_BLOCK + offset, math.gcd(ATTN_BLOCK, SB_TOP_ROWS)), count)


def sb_attn_fwd(qkv4, proj4, rider=None):
    _, b, s, _ = qkv4.shape
    blk, top = ATTN_BLOCK, SB_TOP_ROWS
    ngroups = s // blk // SB_QBLOCKS_PER_STEP
    assert ngroups * SB_QBLOCKS_PER_STEP * blk == s

    def body(q_ref, k_ref, v_ref, gate_ref, o_ref, og_ref):
        def write(h, rows, o):
            cols = slice(h * HEAD_DIM, (h + 1) * HEAD_DIM)
            g = _sb_head(gate_ref, h, rows)
            o_ref[0, rows, cols] = o
            og_ref[0, rows, cols] = (o * (g * _sigmoid(g))).astype(BF16)

        row = lax.broadcasted_iota(jnp.int32, (blk, blk), 0)
        col = lax.broadcasted_iota(jnp.int32, (blk, blk), 1)
        tri_lt = col < row
        suffix = (row > col).astype(BF16)

        def step(items):
            where = [_sb_key_rows(i, done) for _, i, done, _, _, _, _ in items]
            zs = [_sb_logits(q, _sb_head(k_ref, h, rows)) for (h, _, _, q, _, _, _), (rows, _) in zip(items, where)]
            scored = [_sb_scores(z, done == 0, live, mask)
                      for z, (_, _, done, _, mask, _, _), (_, live) in zip(zs, items, where)]
            afters = [_dot_exact(log_skip, suffix) for log_skip, _, _ in scored]
            ws = [jnp.where(valid, jnp.exp(log_beta + after + c), 0.0).astype(BF16)
                  for (_, log_beta, valid), after, (_, _, _, _, _, c, _) in zip(scored, afters, items)]
            accs = [acc + _dot(w, _sb_head(v_ref, h, rows))
                    for (h, _, _, _, _, _, acc), (rows, _), w in zip(items, where, ws)]
            cs = [c + jnp.sum(log_skip, axis=1, keepdims=True)
                  for (log_skip, _, _), (_, _, _, _, _, c, _) in zip(scored, items)]
            return cs, accs

        def top_items(chains, done, cs, accs):
            return [(h, i, done, _sb_head(q_ref, h, _sb_rows(i, 0, top)), tri_lt[:top, :], c, acc)
                    for (h, i), c, acc in zip(chains, cs, accs)]

        def finish_tops(chains, state):
            def k_step(state):
                done, cs, accs = state
                cs, accs = step(top_items(chains, done, cs, accs))
                return done + 1, tuple(cs), tuple(accs)

            _, _, accs = lax.while_loop(functools.partial(_sb_keys_left, chains, slice(0, top)), k_step, state)
            for (h, i), acc in zip(chains, accs):
                write(h, _sb_rows(i, 0, top), acc)

        def q_group(m, before):
            chains, chains_before = _sb_chains(m), _sb_chains(m - 1)
            qis = [_sb_head(q_ref, h, _sb_rows(i, 0, blk)) for h, i in chains]
            n = len(chains)

            def items_of(done, cs, accs):
                return [(h, i, done, q, tri_lt, c, acc) for (h, i), q, c, acc in zip(chains, qis, cs, accs)]

            def k_step(state):
                done, cs, accs = state
                cs, accs = step(items_of(done, cs, accs))
                return done + 1, tuple(cs), tuple(accs)

            done_b, cs_b, accs_b = before
            cs0, accs0 = step(items_of(jnp.int32(0), (jnp.zeros((blk, 1), F32),) * n,
                                       (jnp.zeros((blk, HEAD_DIM), F32),) * n)
                              + top_items(chains_before, done_b, cs_b, accs_b))
            done, cs, accs = lax.while_loop(functools.partial(_sb_keys_left, chains, slice(top, blk)), k_step,
                                            (jnp.int32(1), tuple(cs0[:n]), tuple(accs0[:n])))

            @pl.when(m > 0)
            def _():
                finish_tops(chains_before, (done_b + 1, tuple(cs0[n:]), tuple(accs0[n:])))

            for (h, i), acc in zip(chains, accs):
                write(h, _sb_rows(i, top, blk - top), acc[top:, :])
            return done, tuple(c[:top, :] for c in cs), tuple(acc[:top, :] for acc in accs)

        n = SB_HEADS_PER_STEP * SB_QBLOCKS_PER_STEP
        nothing = (jnp.int32(0), (jnp.zeros((top, 1), F32),) * n, (jnp.zeros((top, HEAD_DIM), F32),) * n)
        last = lax.fori_loop(0, ngroups, q_group, nothing)
        finish_tops(_sb_chains(ngroups - 1), last)

    return _pcall_riding(
        body, rider, (qkv4, qkv4, qkv4, proj4),
        name="sb_attn_fwd", grid=(b, N_HEADS // SB_HEADS_PER_STEP),
        in_specs=[_sb_group_spec(s, 0), _sb_group_spec(s, 1), _sb_group_spec(s, 2), _sb_group_spec(s, 3)],
        out_specs=[_sb_seq_group_spec(s)] * 2,
        out_shape=[jax.ShapeDtypeStruct((b, s, D_MODEL), F32), jax.ShapeDtypeStruct((b, s, D_MODEL), BF16)],
        semantics=("parallel", "parallel"))


def _hg_masks():
    c = HG_CHUNK
    row = lax.broadcasted_iota(jnp.int32, (c, c), 0)
    col = lax.broadcasted_iota(jnp.int32, (c, c), 1)
    incl = (col <= row)
    lower = incl.astype(BF16)
    before_sub = (col < (row // HG_SUB) * HG_SUB).astype(BF16)
    upper = (col >= row).astype(BF16)
    return incl, lower, before_sub, upper


def _hg_lower_bound(lbl_ref):
    l0 = lbl_ref[0, 0]
    l1 = lbl_ref[1, 0]
    d = l1 - l0
    return _sigmoid_pair(d)


def _hg_gates(qp, fp, lb, oml):
    sq = _sigmoid(qp)
    sf, sfn = _sigmoid_pair(fp)
    f = lb + oml * sf
    return dict(qp=qp, sq=sq, q=qp * sq, sf=sf, sfn=sfn, f=f, k=oml * sfn, logf=jnp.log(f))


def _hg_intra(qds, ks, gcs, grs, incl):
    subs = range(HG_CHUNK // HG_SUB)
    qdbs = [qd.astype(BF16) for qd in qds]
    ess = [[jnp.exp(jnp.minimum(gr[sub * HG_SUB:sub * HG_SUB + 1, :] - gc, EXP_CLAMP)) for sub in subs]
           for gc, gr in zip(gcs, grs)]
    ksbs = [[(k * e).astype(BF16) for e in es] for k, es in zip(ks, ess)]
    rows = [[_dot(qdb[sub * HG_SUB:(sub + 1) * HG_SUB, :], ksb[sub], NT) for sub in subs]
            for qdb, ksb in zip(qdbs, ksbs)]
    a_s = [jnp.where(incl, jnp.concatenate(r, axis=0), 0.0) for r in rows]
    return a_s, qdbs, ksbs, ess


def _hg_group_rows(outer, unroll):
    ns = [outer * unroll + u for u in range(unroll)]
    return ns, [pl.ds(pl.multiple_of(n * HG_CHUNK, HG_CHUNK), HG_CHUNK) for n in ns]


def _state_spec(nchunk):
    return pl.BlockSpec((1, 1, nchunk, HEAD_DIM, HEAD_DIM), lambda b, h: (b, h, 0, 0, 0))


def hgrn2_fwd(proj4, lbl4, o_gain, rider=None):
    _, b, s, _ = proj4.shape
    nchunk = s // HG_CHUNK
    c = HG_CHUNK
    unroll = math.gcd(nchunk, HG_UNROLL_FWD)

    def body(q_ref, f_ref, i_ref, lbl_ref, og_ref, o_ref, oraw_ref, st_ref):
        incl, lower, before_sub, _ = _hg_masks()
        lb, oml = _hg_lower_bound(lbl_ref)

        def group(outer, st):
            ns, rows = _hg_group_rows(outer, unroll)
            vs = [_hg_gates(q_ref[0, 0, r, :], f_ref[0, 0, r, :], lb, oml) for r in rows]
            inps = [i_ref[0, 0, r, :].astype(BF16) for r in rows]
            gcs = [_dot_exact(v["logf"], lower, left=True) for v in vs]
            grs = [_dot_exact(v["logf"], before_sub, left=True) for v in vs]
            a_s, _, _, _ = _hg_intra([v["q"] * jnp.exp(gc - gr) for v, gc, gr in zip(vs, gcs, grs)],
                                     [v["k"] for v in vs], gcs, grs, incl)
            gls = [gc[c - 1:c, :] for gc in gcs]
            adds = [_dot(inp, (v["k"] * jnp.exp(gl - gc)).astype(BF16), TN)
                    for inp, v, gl, gc in zip(inps, vs, gls, gcs)]
            o_intra = [_dot(a.astype(BF16), inp) for a, inp in zip(a_s, inps)]
            sts = []
            for gl, add in zip(gls, adds):
                sts.append(st)
                st = st * jnp.exp(gl) + add
            outs = [oi + _dot((v["q"] * jnp.exp(gc)).astype(BF16), s0.astype(BF16), NT)
                    for oi, v, gc, s0 in zip(o_intra, vs, gcs, sts)]
            for n, r, s0, o in zip(ns, rows, sts, outs):
                st_ref[0, 0, n] = s0
                oraw_ref[0, r, :] = o
                o_ref[0, r, :] = o * _rms_scale(o) * og_ref[...]
            return st

        lax.fori_loop(0, nchunk // unroll, group, jnp.zeros((HEAD_DIM, HEAD_DIM), F32))

    seq = jax.ShapeDtypeStruct((b, s, D_MODEL), F32)
    return _pcall_riding(
        body, rider, (proj4, proj4, proj4, lbl4, o_gain), name="hgrn2_fwd", grid=(b, N_HEADS),
        in_specs=[_head_spec(s, 0), _head_spec(s, 1), _head_spec(s, 2),
                  pl.BlockSpec((2, 1, 1, HEAD_DIM), lambda b, h: (0, h, 0, 0)), _GAIN_SPEC],
        out_specs=[_seq_spec(s), _seq_spec(s), _state_spec(nchunk)],
        out_shape=[seq, seq, jax.ShapeDtypeStruct((b, N_HEADS, nchunk, HEAD_DIM, HEAD_DIM), F32)],
        semantics=("parallel", "parallel"))


def outproj_bwd(dh, w_out, o2, proj, name):
    t = dh.shape[0]
    tm = _row_tile(t, 512)

    def body(dh_ref, w_ref, o_ref, gate_ref, do_ref, dproj_ref, dw_ref):
        dhb = dh_ref[...].astype(BF16)
        dog = _dot(dhb, w_ref[...], NT)
        g = gate_ref[0]
        sg = _sigmoid(g)
        silu = g * sg
        o = o_ref[...]
        do_ref[...] = dog * silu
        dproj_ref[0] = dog * o * (sg * (1.0 + g * (1.0 - sg)))
        part = _dot((o * silu).astype(BF16), dhb, TN)

        @pl.when(pl.program_id(0) == 0)
        def _():
            dw_ref[...] = part

        @pl.when(pl.program_id(0) > 0)
        def _():
            dw_ref[...] += part

    row = pl.BlockSpec((tm, D_MODEL), lambda i: (i, 0))
    full = pl.BlockSpec((D_MODEL, D_MODEL), lambda i: (0, 0))
    return _pcall(
        body, name=name, grid=(t // tm,),
        in_specs=[row, full, row, pl.BlockSpec((1, tm, D_MODEL), lambda i: (3, i, 0))],
        out_specs=[row, pl.BlockSpec((1, tm, D_MODEL), lambda i: (3, i, 0)), full],
        out_shape=[jax.ShapeDtypeStruct((t, D_MODEL), F32),
                   jax.ShapeDtypeStruct((4, t, D_MODEL), F32),
                   jax.ShapeDtypeStruct((D_MODEL, D_MODEL), F32)],
        compiler_params=_params("arbitrary"),
    )(dh, w_out, o2, proj)


def inproj_bwd_dx(dproj, wg, x2, gain, dres, name, rider=None):
    t = x2.shape[0]
    tm = _row_tile(t, 512)

    def body(d_ref, w_ref, x_ref, g_ref, r_ref, dx_ref, dg_ref):
        du = jnp.zeros((tm, D_MODEL), F32)
        for p in range(N_DEV):
            cols = slice((p % 2) * W_COLS, (p % 2 + 1) * W_COLS)
            du = du + _dot(d_ref[p // 2, :, cols].astype(BF16), w_ref[p], NT)
        x = x_ref[...]
        r = _rms_scale(x)
        xh = x * r
        a = du * g_ref[...]
        dx_ref[...] = r_ref[...] + r * (a - xh * jnp.mean(a * xh, axis=-1, keepdims=True))
        part = jnp.sum(du * xh, axis=0, keepdims=True)

        @pl.when(pl.program_id(0) == 0)
        def _():
            dg_ref[...] = part

        @pl.when(pl.program_id(0) > 0)
        def _():
            dg_ref[...] += part

    row = pl.BlockSpec((tm, D_MODEL), lambda i: (i, 0))
    vec = pl.BlockSpec((1, D_MODEL), lambda i: (0, 0))
    return _pcall_riding(
        body, rider, (dproj, wg, x2, gain, dres), name=name, grid=(t // tm,),
        in_specs=[pl.BlockSpec((4, tm, D_MODEL), lambda i: (0, i, 0)),
                  pl.BlockSpec((N_DEV, D_MODEL, W_COLS), lambda i: (0, 0, 0)),
                  row, vec, row],
        out_specs=[row, vec],
        out_shape=[jax.ShapeDtypeStruct((t, D_MODEL), F32), jax.ShapeDtypeStruct((1, D_MODEL), F32)],
        semantics=("arbitrary",))


def inproj_bwd_dw(ut, dproj, name, out_dtype):
    t = ut.shape[1]

    def body(ut_ref, d_ref, dw_ref):
        dw_ref[0] = _dot(ut_ref[...], d_ref[0].astype(BF16)).astype(dw_ref.dtype)

    return _pcall(
        body, name=name, grid=(N_DEV,),
        in_specs=[pl.BlockSpec((D_MODEL, t), lambda j: (0, 0)),
                  pl.BlockSpec((1, t, W_COLS), lambda j: (j // 2, 0, j % 2))],
        out_specs=pl.BlockSpec((1, D_MODEL, W_COLS), lambda j: (j, 0, 0)),
        out_shape=jax.ShapeDtypeStruct((N_DEV, D_MODEL, W_COLS), out_dtype),
        compiler_params=_params("parallel"),
    )(ut, dproj)


def _rms_bwd(x, gain, dy):
    r = _rms_scale(x)
    xh = x * r
    a = dy * gain
    return r * (a - xh * jnp.mean(a * xh, axis=-1, keepdims=True)), dy * xh


def sb_attn_bwd(qkv4, proj4, do3, o3, q_gain, k_gain, dproj4, rider=None):
    _, b, s, _ = proj4.shape
    blk, top = ATTN_BLOCK, SB_TOP_ROWS
    nq = s // blk
    ngroups = nq // SB_QBLOCKS_PER_STEP
    assert ngroups * SB_QBLOCKS_PER_STEP * blk == s
    scale = HEAD_DIM ** -0.5

    def body(qn_ref, kn_ref, v_ref, q_ref, k_ref, do_ref, o_ref, qg_ref, kg_ref, _alias, d_ref, dqg_ref, dkg_ref, dob):
        for h in range(SB_HEADS_PER_STEP):
            dob[h] = do_ref[0, :, h * HEAD_DIM:(h + 1) * HEAD_DIM].astype(BF16)
        d_ref[...] = jnp.zeros_like(d_ref)
        row = lax.broadcasted_iota(jnp.int32, (blk, blk), 0)
        col = lax.broadcasted_iota(jnp.int32, (blk, blk), 1)
        tri_lt = col < row
        suffix = (row > col).astype(BF16)
        suffix_incl = (row >= col).astype(BF16)

        def step(items):
            heads = [it[0] for it in items]
            where = [_sb_key_rows(it[1], it[2]) for it in items]
            kjs = [_sb_head(kn_ref, h, rows) for h, (rows, _) in zip(heads, where)]
            zs = [_sb_logits(it[3], kj) for it, kj in zip(items, kjs)]
            dws = [_dot(it[4], _sb_head(v_ref, h, rows), NT) for it, h, (rows, _) in zip(items, heads, where)]
            scored = [_sb_scores(z, it[2] == 0, live, it[6]) for z, it, (_, live) in zip(zs, items, where)]
            afters = [_dot_exact(log_skip, suffix) for log_skip, _, _ in scored]
            wbs = [jnp.where(valid, jnp.exp(log_beta + after + it[7]), 0.0).astype(BF16)
                   for (_, log_beta, valid), after, it in zip(scored, afters, items)]
            gs = [dw * wb.astype(F32) for dw, wb in zip(dws, wbs)]
            befores = [it[5] - (_dot_exact(g, suffix_incl) + it[8]) for g, it in zip(gs, items)]
            dzbs = [jnp.where(valid, g - jnp.exp(log_beta) * (g + before), 0.0).astype(BF16)
                    for (_, log_beta, valid), g, before in zip(scored, gs, befores)]
            dqs = [it[9] + _dot(dzb, kj) for it, dzb, kj in zip(items, dzbs, kjs)]
            for it, h, (rows, _), wb, dzb in zip(items, heads, where, wbs, dzbs):
                cols = slice(h * HEAD_DIM, (h + 1) * HEAD_DIM)
                d_ref[2, 0, rows, cols] += _dot(wb, it[4], TN)
                d_ref[1, 0, rows, cols] += _dot(dzb, it[3], TN)
            cs = [it[7] + jnp.sum(log_skip, axis=1, keepdims=True) for (log_skip, _, _), it in zip(scored, items)]
            cgs = [it[8] + jnp.sum(g, axis=1, keepdims=True) for g, it in zip(gs, items)]
            return cs, cgs, dqs

        def top_items(chains, deltas, done, cs, cgs, dqs):
            return [(h, i, done, _sb_head(qn_ref, h, _sb_rows(i, 0, top)), dob[h, _sb_rows(i, 0, top), :], delta,
                     tri_lt[:top, :], c, cg, dq)
                    for (h, i), delta, c, cg, dq in zip(chains, deltas, cs, cgs, dqs)]

        def finish_tops(chains, deltas, state):
            def k_step(state):
                done, cs, cgs, dqs = state
                cs, cgs, dqs = step(top_items(chains, deltas, done, cs, cgs, dqs))
                return done + 1, tuple(cs), tuple(cgs), tuple(dqs)

            _, _, _, dqs = lax.while_loop(functools.partial(_sb_keys_left, chains, slice(0, top)), k_step, state)
            for (h, i), dq in zip(chains, dqs):
                d_ref[0, 0, _sb_rows(i, 0, top), h * HEAD_DIM:(h + 1) * HEAD_DIM] = dq * scale

        def q_group(m, before):
            chains, chains_before = _sb_chains(m), _sb_chains(m - 1)
            deltas_b, before = before[0], before[1:]
            qis, dois, deltas = [], [], []
            for h, i in chains:
                rows_i = _sb_rows(i, 0, blk)
                qis.append(_sb_head(qn_ref, h, rows_i))
                dois.append(dob[h, rows_i, :])
                deltas.append(jnp.sum(dois[-1].astype(F32) * o_ref[0, rows_i, h * HEAD_DIM:(h + 1) * HEAD_DIM],
                                      axis=1, keepdims=True))
            n = len(chains)

            def items_of(done, cs, cgs, dqs):
                return [(h, i, done, q, do, delta, tri_lt, c, cg, dq)
                        for (h, i), q, do, delta, c, cg, dq in zip(chains, qis, dois, deltas, cs, cgs, dqs)]

            def k_step(state):
                done, cs, cgs, dqs = state
                cs, cgs, dqs = step(items_of(done, cs, cgs, dqs))
                return done + 1, tuple(cs), tuple(cgs), tuple(dqs)

            done_b, cs_b, cgs_b, dqs_b = before
            zero = (jnp.zeros((blk, 1), F32),) * n
            new = step(items_of(jnp.int32(0), zero, zero, (jnp.zeros((blk, HEAD_DIM), F32),) * n)
                       + top_items(chains_before, deltas_b, done_b, cs_b, cgs_b, dqs_b))
            done, cs, cgs, dqs = lax.while_loop(functools.partial(_sb_keys_left, chains, slice(top, blk)), k_step,
                                                (jnp.int32(1),) + tuple(tuple(x[:n]) for x in new))

            @pl.when(m > 0)
            def _():
                finish_tops(chains_before, deltas_b, (done_b + 1,) + tuple(tuple(x[n:]) for x in new))

            for (h, i), dq in zip(chains, dqs):
                d_ref[0, 0, _sb_rows(i, top, blk - top), h * HEAD_DIM:(h + 1) * HEAD_DIM] = dq[top:, :] * scale
            first = lambda xs: tuple(x[:top, :] for x in xs)
            return first(deltas), done, first(cs), first(cgs), first(dqs)

        n = SB_HEADS_PER_STEP * SB_QBLOCKS_PER_STEP
        zero = (jnp.zeros((top, 1), F32),) * n
        nothing = (zero, jnp.int32(0), zero, zero, (jnp.zeros((top, HEAD_DIM), F32),) * n)
        last = lax.fori_loop(0, ngroups, q_group, nothing)
        finish_tops(_sb_chains(ngroups - 1), last[0], last[1:])

        def norm_block(i, carry):
            rows = pl.ds(pl.multiple_of(i * blk, blk), blk)
            out = []
            for h in range(SB_HEADS_PER_STEP):
                cols = slice(h * HEAD_DIM, (h + 1) * HEAD_DIM)
                for part, src_ref, gain_ref in ((0, q_ref, qg_ref), (1, k_ref, kg_ref)):
                    dy = d_ref[part, 0, rows, cols] * (scale if part == 1 else 1.0)
                    dx, pg = _rms_bwd(src_ref[0, 0, rows, cols], gain_ref[...], dy)
                    d_ref[part, 0, rows, cols] = dx
                    out.append(carry[len(out)] + jnp.sum(pg, axis=0, keepdims=True))
            return tuple(out)

        sums = lax.fori_loop(0, nq, norm_block, (jnp.zeros((1, HEAD_DIM), F32),) * (2 * SB_HEADS_PER_STEP))
        for h in range(SB_HEADS_PER_STEP):
            dqg_ref[0, h] = sums[2 * h]
            dkg_ref[0, h] = sums[2 * h + 1]

    head_row = jax.ShapeDtypeStruct((b, N_HEADS, 1, HEAD_DIM), F32)
    return _pcall_riding(
        body, rider, (qkv4, qkv4, qkv4, proj4, proj4, do3, o3, q_gain, k_gain, dproj4),
        name="sb_attn_bwd", grid=(b, N_HEADS // SB_HEADS_PER_STEP),
        in_specs=[_sb_group_spec(s, 0), _sb_group_spec(s, 1), _sb_group_spec(s, 2),
                  _sb_group_spec(s, 0), _sb_group_spec(s, 1),
                  _sb_seq_group_spec(s), _sb_seq_group_spec(s), _GAIN_SPEC, _GAIN_SPEC,
                  pl.BlockSpec(memory_space=pl.ANY)],
        out_specs=[pl.BlockSpec((3, 1, s, SB_GROUP_COLS), lambda b, g: (0, b, 0, g)),
                   _SB_GROUP_ROW_SPEC, _SB_GROUP_ROW_SPEC],
        out_shape=[jax.ShapeDtypeStruct(dproj4.shape, F32), head_row, head_row],
        scratch_shapes=[pltpu.VMEM((SB_HEADS_PER_STEP, s, HEAD_DIM), BF16)],
        input_output_aliases={9: 0}, semantics=("parallel", "parallel"))


def hgrn2_bwd(proj4, don3, oraw3, states, lbl4, o_gain, dproj4, rider=None):
    _, b, s, _ = proj4.shape
    nchunk = s // HG_CHUNK
    c = HG_CHUNK
    subs = range(HG_CHUNK // HG_SUB)
    unroll = math.gcd(nchunk, HG_UNROLL_BWD)
    ngroup = nchunk // unroll

    def body(q_ref, f_ref, i_ref, don_ref, oraw_ref, st_ref, lbl_ref, og_ref, _alias, d_ref, dog_ref, dlb_ref):
        incl, lower, before_sub, upper = _hg_masks()
        lb, oml = _hg_lower_bound(lbl_ref)
        last_row = lax.broadcasted_iota(jnp.int32, (c, HEAD_DIM), 0) == c - 1

        def group(m, carry):
            dst, dog_acc, dlb_acc = carry
            ns, rows = _hg_group_rows(ngroup - 1 - m, unroll)
            ns, rows = ns[::-1], rows[::-1]
            vs = [_hg_gates(q_ref[0, 0, r, :], f_ref[0, 0, r, :], lb, oml) for r in rows]
            inps = [i_ref[0, 0, r, :].astype(BF16) for r in rows]
            sts = [st_ref[0, 0, n] for n in ns]
            gcs = [_dot_exact(v["logf"], lower, left=True) for v in vs]
            grs = [_dot_exact(v["logf"], before_sub, left=True) for v in vs]
            e_qs = [jnp.exp(gc - gr) for gc, gr in zip(gcs, grs)]
            a_s, qdbs, ksbs, ess = _hg_intra([v["q"] * e for v, e in zip(vs, e_qs)], [v["k"] for v in vs],
                                             gcs, grs, incl)
            e_gcs = [jnp.exp(gc) for gc in gcs]
            gls = [gc[c - 1:c, :] for gc in gcs]
            e_gls = [jnp.exp(gl) for gl in gls]
            e_ks = [jnp.exp(gl - gc) for gl, gc in zip(gls, gcs)]
            normed = [_rms_bwd(oraw_ref[0, r, :], og_ref[...], don_ref[0, r, :]) for r in rows]
            dobs = [do.astype(BF16) for do, _ in normed]
            dabs = [jnp.where(incl, _dot(dob, inp, NT), 0.0).astype(BF16) for dob, inp in zip(dobs, inps)]
            adds = [_dot(dob, (v["q"] * e).astype(BF16), TN) for dob, v, e in zip(dobs, vs, e_gcs)]
            dq_inters = [_dot(dob, st.astype(BF16)) * e for dob, st, e in zip(dobs, sts, e_gcs)]
            dqds = [jnp.concatenate([_dot(dab[sub * HG_SUB:(sub + 1) * HG_SUB, :], ksb[sub]) for sub in subs], axis=0)
                    for dab, ksb in zip(dabs, ksbs)]
            dkss = [[_dot(dab[sub * HG_SUB:(sub + 1) * HG_SUB, :], qdb[sub * HG_SUB:(sub + 1) * HG_SUB, :], TN)
                     for sub in subs] for dab, qdb in zip(dabs, qdbs)]
            dsts = []
            for e_gl, add in zip(e_gls, adds):
                dsts.append(dst)
                dst = dst * e_gl + add
            dstbs = [d.astype(BF16) for d in dsts]
            dis = [_dot(a.astype(BF16), dob, TN) + _dot((v["k"] * e_k).astype(BF16), dstb, NT)
                   for a, dob, v, e_k, dstb in zip(a_s, dobs, vs, e_ks, dstbs)]
            dk_inters = [_dot(inp, dstb) * e_k for inp, dstb, e_k in zip(inps, dstbs, e_ks)]
            dks, dgcs = [], []
            for u in range(len(rows)):
                q, k = vs[u]["q"], vs[u]["k"]
                dk, dgc_k = dk_inters[u], jnp.zeros((c, HEAD_DIM), F32)
                for sub in subs:
                    dk = dk + dkss[u][sub] * ess[u][sub]
                    dgc_k = dgc_k + dkss[u][sub] * ksbs[u][sub].astype(F32)
                at_last = (jnp.sum(k * dk_inters[u], axis=0, keepdims=True)
                           + e_gls[u] * jnp.sum(sts[u] * dsts[u], axis=0, keepdims=True))
                dks.append(dk)
                dgcs.append((qdbs[u].astype(F32) * dqds[u] - dgc_k) + (q * dq_inters[u] - k * dk_inters[u])
                            + jnp.where(last_row, at_last, 0.0))
            dlf_fs = [_dot_exact(dgc, upper, left=True) / v["f"] for dgc, v in zip(dgcs, vs)]
            for u, r in enumerate(rows):
                v = vs[u]
                dq = dqds[u] * e_qs[u] + dq_inters[u]
                d_ref[0, 0, r, :] = dq * (v["sq"] * (1.0 + v["qp"] * (1.0 - v["sq"])))
                d_ref[1, 0, r, :] = (dlf_fs[u] - dks[u]) * (oml * v["sf"] * v["sfn"])
                d_ref[2, 0, r, :] = dis[u]
                dlb_acc = dlb_acc + jnp.sum((dlf_fs[u] - dks[u]) * v["sfn"], axis=0, keepdims=True)
                dog_acc = dog_acc + jnp.sum(normed[u][1], axis=0, keepdims=True)
            return dst, dog_acc, dlb_acc

        zero = jnp.zeros((1, HEAD_DIM), F32)
        _, dog, dlb = lax.fori_loop(0, ngroup, group, (jnp.zeros((HEAD_DIM, HEAD_DIM), F32), zero, zero))
        dog_ref[0, 0] = dog
        dlb_ref[0, 0] = dlb

    head_row = jax.ShapeDtypeStruct((b, N_HEADS, 1, HEAD_DIM), F32)
    return _pcall_riding(
        body, rider, (proj4, proj4, proj4, don3, oraw3, states, lbl4, o_gain, dproj4),
        name="hgrn2_bwd", grid=(b, N_HEADS),
        in_specs=[_head_spec(s, 0), _head_spec(s, 1), _head_spec(s, 2), _seq_spec(s), _seq_spec(s),
                  _state_spec(nchunk), pl.BlockSpec((2, 1, 1, HEAD_DIM), lambda b, h: (0, h, 0, 0)), _GAIN_SPEC,
                  pl.BlockSpec(memory_space=pl.ANY)],
        out_specs=[pl.BlockSpec((3, 1, s, HEAD_DIM), lambda b, h: (0, b, 0, h)), _HEAD_ROW_SPEC, _HEAD_ROW_SPEC],
        out_shape=[jax.ShapeDtypeStruct(dproj4.shape, F32), head_row, head_row],
        input_output_aliases={8: 0}, semantics=("parallel", "parallel"))


def local_step(x, target, sb_norm, wsi, sb_q_gain, sb_k_gain, hg_o_gain, hg_lb_logits, wso_mine, whi_mine, who_mine,
               hg_norm_mine):
    b, s, _ = x.shape
    t = b * s
    x2 = x.reshape(t, D_MODEL)
    tg2 = target.reshape(t, D_MODEL)
    lbl4 = hg_lb_logits.reshape(2, N_HEADS, 1, HEAD_DIM)
    four = (4, b, s, D_MODEL)
    three = (b, s, D_MODEL)
    rows8 = (N_DEV, W_ROWS, D_MODEL)

    (proj0, u0, qkv0), (wso, hgn) = rms_inproj(x2, sb_norm, wsi, "sb_inproj",
                                               _Rider([wso_mine, hg_norm_mine], scatter=False),
                                               qk_gains=(sb_q_gain, sb_k_gain))
    qkv0 = qkv0.reshape(3, b, s, D_MODEL)
    wso = wso.reshape(D_MODEL, D_MODEL)
    hg_norm_full = hgn[:, 0, :].reshape(1, D_MODEL)
    (o0, og0), (whi,) = sb_attn_fwd(qkv0, proj0.reshape(four), _Rider([whi_mine], scatter=False))
    o0 = o0.reshape(t, D_MODEL)
    h1 = outproj_residual(og0.reshape(t, D_MODEL), wso, x2, "sb_outproj")
    (proj1, u1), _ = rms_inproj(h1, hg_norm_full, whi, "hg_inproj")
    (o1, o1_raw, states), (who,) = hgrn2_fwd(proj1.reshape(four), lbl4, hg_o_gain,
                                             _Rider([who_mine], scatter=False))
    who = who.reshape(D_MODEL, D_MODEL)
    o1 = o1.reshape(t, D_MODEL)
    dh2, loss_parts = gate_outproj(o1, proj1, who, h1, tg2, "hg_outproj_loss")

    do1, dproj1, g_who = outproj_bwd(dh2, who, o1, proj1, "hg_outproj_bwd")
    (dproj1, g_og, g_lb), (p_who,) = hgrn2_bwd(proj1.reshape(four), do1.reshape(three), o1_raw, states, lbl4,
                                               hg_o_gain, dproj1.reshape(four),
                                               _Rider([g_who.reshape(rows8)], scatter=True))
    dproj1 = dproj1.reshape(4, t, D_MODEL)
    (dh1, g_hgn), _ = inproj_bwd_dx(dproj1, whi, h1, hg_norm_full, dh2, "hg_inproj_bwd_dx")
    g_whi = inproj_bwd_dw(u1, dproj1, "hg_inproj_bwd_dw", out_dtype=BF16)

    do0, dproj0, g_wso = outproj_bwd(dh1, wso, o0, proj0, "sb_outproj_bwd")
    (dproj0, g_qg, g_kg), (p_whi, p_wso) = sb_attn_bwd(qkv0, proj0.reshape(four), do0.reshape(three), o0.reshape(three),
                                                       sb_q_gain, sb_k_gain, dproj0.reshape(four),
                                                       _Rider([g_whi, g_wso.reshape(rows8)], scatter=True))
    dproj0 = dproj0.reshape(4, t, D_MODEL)
    g_wsi = inproj_bwd_dw(u0, dproj0, "sb_inproj_bwd_dw", out_dtype=BF16)
    (gx, g_sbn), (p_wsi,) = inproj_bwd_dx(dproj0, wsi, x2, sb_norm, dh1, "sb_inproj_bwd_dx",
                                          _Rider([sum_within_chip(g_wsi)], scatter=True, chips=True))
    return dict(loss_parts=loss_parts, gx=gx.reshape(three), p_wsi=p_wsi, p_wso=p_wso, p_whi=p_whi, p_who=p_who,
                g_sbn=g_sbn, g_hgn=g_hgn, g_qg=g_qg, g_kg=g_kg, g_og=g_og, g_lb=g_lb)


def sum_within_chip(g):
    _, r, c_ = g.shape
    chips = N_DEV // 2

    def swap(g_ref, got_ref, send_sems, recv_sems):
        x, y, c = _mesh_pos()
        copies = [pltpu.make_async_remote_copy(
            src_ref=g_ref.at[2 * q + 1 - c], dst_ref=got_ref.at[q], send_sem=send_sems.at[q], recv_sem=recv_sems.at[q],
            device_id=(x, y, 1 - c), device_id_type=MESH) for q in range(chips)]
        for cp in copies:
            cp.start()
        for cp in copies:
            cp.wait_recv()
            cp.wait_send()

    got = _pcall(swap, name="swap_with_sibling", in_specs=[_ANY], out_specs=_ANY,
                 out_shape=jax.ShapeDtypeStruct((chips, r, c_), g.dtype),
                 scratch_shapes=[pltpu.SemaphoreType.DMA((chips,)), pltpu.SemaphoreType.DMA((chips,))])(g)

    def add(g_ref, got_ref, out_ref):
        mine = g_ref[0, lax.axis_index("c")]
        out_ref[0] = (mine.astype(F32) + got_ref[0].astype(F32)).astype(out_ref.dtype)

    return _pcall(
        add, name="add_sibling_partials", grid=(chips,),
        in_specs=[pl.BlockSpec((1, 2, r, c_), lambda q: (q, 0, 0, 0)), pl.BlockSpec((1, r, c_), lambda q: (q, 0, 0))],
        out_specs=pl.BlockSpec((1, r, c_), lambda q: (q, 0, 0)),
        out_shape=jax.ShapeDtypeStruct((chips, r, c_), g.dtype),
        compiler_params=_params("parallel"),
    )(g.reshape(chips, 2, r, c_), got)


def _two_level_gather(src, out, send_sems, recv_sems, local_sem, pos, meanwhile):
    x, y, c = pos
    me, sibling = (x, y, c), (x, y, 1 - c)
    chips = [(1 - x, y), (x, 1 - y), (1 - x, 1 - y)]

    def copy(k, block, to, source=None):
        slot = out.at[_linear(block)]
        return pltpu.make_async_remote_copy(
            src_ref=slot if source is None else source, dst_ref=slot, send_sem=send_sems.at[k],
            recv_sem=recv_sems.at[k], device_id=to, device_id_type=MESH)

    mine = pltpu.make_async_copy(src, out.at[_linear(me)], local_sem)
    mine.start()
    first = [copy(0, me, sibling, src)] + [copy(1 + j, me, (*chip, c), src) for j, chip in enumerate(chips)]
    for cp in first:
        cp.start()
    meanwhile()
    passed = [copy(4 + j, (*chip, c), sibling) for j, chip in enumerate(chips)]
    for j, chip in enumerate(chips):
        copy(1 + j, (*chip, c), me).wait_recv()
        passed[j].start()
    copy(0, sibling, me).wait_recv()
    for j, chip in enumerate(chips):
        copy(4 + j, (*chip, 1 - c), me).wait_recv()
    for cp in first + passed:
        cp.wait_send()
    mine.wait()


def gather_first_weights(w_si, w_so, w_hi, w_ho, hg_norm):
    def body(si_ref, so_ref, hi_ref, ho_ref, hn_ref, o_si, so_b, hi_b, ho_b, hn_b, si_b, send_sems, recv_sems, local_sem):
        si_b[...] = si_ref[...].astype(BF16)

        def cast_the_rest():
            for src, buf in ((so_ref, so_b), (hi_ref, hi_b), (ho_ref, ho_b)):
                buf[...] = src[...].astype(BF16)
            hn_b[...] = jnp.broadcast_to(hn_ref[...], hn_b.shape)

        _two_level_gather(si_b, o_si, send_sems, recv_sems, local_sem, _mesh_pos(), cast_the_rest)

    return _pcall(
        body, name="gather_first_weights",
        in_specs=[_VMEM] * 5, out_specs=[_ANY] + [_VMEM] * 4,
        out_shape=[jax.ShapeDtypeStruct((N_DEV,) + w_si.shape, BF16), jax.ShapeDtypeStruct(w_so.shape, BF16),
                   jax.ShapeDtypeStruct(w_hi.shape, BF16), jax.ShapeDtypeStruct(w_ho.shape, BF16),
                   jax.ShapeDtypeStruct((8, HEAD_DIM), F32)],
        scratch_shapes=[pltpu.VMEM(w_si.shape, BF16), pltpu.SemaphoreType.DMA((N_PEERS,)),
                        pltpu.SemaphoreType.DMA((N_PEERS,)), pltpu.SemaphoreType.DMA],
        compiler_params=pltpu.CompilerParams(vmem_limit_bytes=VMEM_LIMIT_BYTES),
    )(w_si, w_so, w_hi, w_ho, hg_norm)


def _adamw(w, g, m, v):
    m = ADAM_B1 * m + (1.0 - ADAM_B1) * g
    v = ADAM_B2 * v + (1.0 - ADAM_B2) * (g * g)
    m_hat = m / (1.0 - ADAM_B1 ** ADAM_STEP)
    v_hat = v / (1.0 - ADAM_B2 ** ADAM_STEP)
    delta = -ADAM_LR * (m_hat / (jnp.sqrt(v_hat) + ADAM_EPS) + ADAM_WD * w)
    return delta, m, v


def reduce_adamw(parts, w, m, v, name):
    n, r, c = parts.shape
    tr = _row_tile(r, 256)

    def body(p_ref, w_ref, m_ref, v_ref, g_ref, d_ref, m2_ref, v2_ref):
        g = p_ref[0].astype(F32)
        for sender in range(1, n):
            g = g + p_ref[sender].astype(F32)
        g_ref[0] = g
        d_ref[0], m2_ref[0], v2_ref[0] = _adamw(w_ref[0], g, m_ref[0], v_ref[0])

    tile = pl.BlockSpec((1, tr, c), lambda i: (0, i, 0))
    return _pcall(
        body, name=name, grid=(r // tr,),
        in_specs=[pl.BlockSpec((n, tr, c), lambda i: (0, i, 0)), tile, tile, tile],
        out_specs=[tile] * 4, out_shape=[jax.ShapeDtypeStruct((1, r, c), F32)] * 4,
        compiler_params=_params("parallel"),
    )(parts, w, m, v)


PACK_ROWS = 32
ROW_SBN, ROW_HGN, ROW_LB, ROW_QG, ROW_KG, ROW_OG, ROW_LOSS = 0, 8, 16, 24, 25, 26, 27


def small_update(g_sbn, g_hgn, g_lb, g_qg, g_kg, g_og, loss_parts, small):
    def gather(sbn_ref, hgn_ref, lb_ref, qg_ref, kg_ref, og_ref, loss_ref, gath, pack, send_sems, recv_sems, local_sems):
        pos = _mesh_pos()
        pack[...] = jnp.zeros_like(pack)
        pack[ROW_SBN:ROW_SBN + 8, :] = sbn_ref[...]
        pack[ROW_HGN:ROW_HGN + 8, :] = hgn_ref[...]
        pack[ROW_LB:ROW_LB + 8, :] = jnp.sum(lb_ref[...], axis=0)
        pack[ROW_QG:ROW_QG + 1, :] = jnp.sum(qg_ref[...], axis=0, keepdims=True)
        pack[ROW_KG:ROW_KG + 1, :] = jnp.sum(kg_ref[...], axis=0, keepdims=True)
        pack[ROW_OG:ROW_OG + 1, :] = jnp.sum(og_ref[...], axis=0, keepdims=True)
        pack[ROW_LOSS:ROW_LOSS + 1, :] = jnp.sum(loss_ref[...], axis=0)[0:1, :]
        _exchange_start([((lambda p: pack), gath)], send_sems, recv_sems, local_sems, pos)
        _exchange_wait([((lambda p: pack), gath)], send_sems, recv_sems, local_sems, pos)

    packs = _pcall(
        gather, name="small_gather",
        in_specs=[_VMEM] * 7, out_specs=_VMEM, out_shape=jax.ShapeDtypeStruct((N_DEV, PACK_ROWS, HEAD_DIM), F32),
        scratch_shapes=[pltpu.VMEM((PACK_ROWS, HEAD_DIM), F32)] + _exchange_sems(1),
    )(g_sbn, g_hgn, g_lb, g_qg, g_kg, g_og, loss_parts)

    def apply(gath, *refs):
        wmv, outs, tot = refs[:len(small)], refs[len(small):-1], refs[-1]
        me = _linear(_mesh_pos())
        total = gath[0]
        for dev in range(1, N_DEV):
            total = total + gath[dev]
        tot[...] = total
        outs[0][...] = jnp.broadcast_to(tot[ROW_LOSS:ROW_LOSS + 1, :], (8, HEAD_DIM))
        logits_ref, g_logits_ref = wmv[15], outs[1 + 4 * 5]
        for h in range(N_HEADS):
            lanes = slice(h * HEAD_DIM, (h + 1) * HEAD_DIM)
            p1, p0 = _sigmoid_pair(logits_ref[1:2, lanes] - logits_ref[0:1, lanes])
            d_l1 = p0 * p1 * tot[ROW_LB + h:ROW_LB + h + 1, :]
            g_logits_ref[0:1, lanes] = -d_l1
            g_logits_ref[1:2, lanes] = d_l1
        grads = [tot[ROW_SBN:ROW_SBN + 8, :], tot[ROW_QG:ROW_QG + 1, :], tot[ROW_KG:ROW_KG + 1, :],
                 tot[pl.ds(ROW_HGN + me, 1), :], tot[ROW_OG:ROW_OG + 1, :], g_logits_ref[...]]
        for i, g in enumerate(grads):
            w_ref, m_ref, v_ref = wmv[3 * i:3 * i + 3]
            o = outs[1 + 4 * i:5 + 4 * i]
            if i < 5:
                o[0][...] = g
            o[1][...], o[2][...], o[3][...] = _adamw(w_ref[...], g, m_ref[...], v_ref[...])

    out_shape = [jax.ShapeDtypeStruct((8, HEAD_DIM), F32)]
    for i in range(6):
        out_shape += [jax.ShapeDtypeStruct(small[3 * i].shape, F32)] * 4
    return _pcall(
        apply, name="small_update",
        in_specs=[_VMEM] * (1 + len(small)), out_specs=[_VMEM] * 25, out_shape=out_shape,
        scratch_shapes=[pltpu.VMEM((PACK_ROWS, HEAD_DIM), F32)],
    )(packs, *small)


def kernel(x, sb_norm, sb_w_in, sb_q_gain, sb_k_gain, sb_w_out, hg_norm, hg_w_in, hg_o_gain, hg_w_out, hg_lb_logits, loss_target, m_sb_norm, m_sb_w_in, m_sb_q_gain, m_sb_k_gain, m_sb_w_out, m_hg_norm, m_hg_w_in, m_hg_o_gain, m_hg_w_out, m_hg_lb_logits, v_sb_norm, v_sb_w_in, v_sb_q_gain, v_sb_k_gain, v_sb_w_out, v_hg_norm, v_hg_w_in, v_hg_o_gain, v_hg_w_out, v_hg_lb_logits):
    b = x.shape[0]
    wsi, wso_mine, whi_mine, who_mine, hg_norm_mine = gather_first_weights(
        sb_w_in[0], sb_w_out[0], hg_w_in[0], hg_w_out[0], hg_norm)
    r = local_step(x, loss_target, sb_norm, wsi, sb_q_gain, sb_k_gain, hg_o_gain, hg_lb_logits,
                   wso_mine, whi_mine, who_mine, hg_norm_mine)
    big = {}
    for name, p, w, m, v in (("sb_w_in", r["p_wsi"], sb_w_in, m_sb_w_in, v_sb_w_in),
                             ("sb_w_out", r["p_wso"], sb_w_out, m_sb_w_out, v_sb_w_out),
                             ("hg_w_in", r["p_whi"], hg_w_in, m_hg_w_in, v_hg_w_in),
                             ("hg_w_out", r["p_who"], hg_w_out, m_hg_w_out, v_hg_w_out)):
        big[name] = reduce_adamw(p, w, m, v, "adamw_" + name)

    def rows8(a):
        return a.reshape(8, HEAD_DIM)

    small_in = [rows8(sb_norm), rows8(m_sb_norm), rows8(v_sb_norm),
                sb_q_gain, m_sb_q_gain, v_sb_q_gain,
                sb_k_gain, m_sb_k_gain, v_sb_k_gain,
                hg_norm, m_hg_norm, v_hg_norm,
                hg_o_gain, m_hg_o_gain, v_hg_o_gain,
                hg_lb_logits, m_hg_lb_logits, v_hg_lb_logits]
    so = small_update(rows8(r["g_sbn"]), rows8(r["g_hgn"]), r["g_lb"].reshape(b, N_HEADS, HEAD_DIM),
                      r["g_qg"].reshape(b * N_HEADS, HEAD_DIM), r["g_kg"].reshape(b * N_HEADS, HEAD_DIM),
                      r["g_og"].reshape(b * N_HEADS, HEAD_DIM), r["loss_parts"], small_in)
    loss = so[0][0, 0]
    shapes = {"sb_norm": (1, D_MODEL), "sb_q_gain": (1, HEAD_DIM), "sb_k_gain": (1, HEAD_DIM),
              "hg_norm": (1, HEAD_DIM), "hg_o_gain": (1, HEAD_DIM), "hg_lb_logits": (2, D_MODEL)}
    small = {}
    for i, name in enumerate(("sb_norm", "sb_q_gain", "sb_k_gain", "hg_norm", "hg_o_gain", "hg_lb_logits")):
        small[name] = [o.reshape(shapes[name]) for o in so[1 + 4 * i:5 + 4 * i]]
    order = ("sb_norm", "sb_w_in", "sb_q_gain", "sb_k_gain", "sb_w_out",
             "hg_norm", "hg_w_in", "hg_o_gain", "hg_w_out", "hg_lb_logits")
    res = {**big, **small}
    return (loss, r["gx"]) + tuple(res[n][j] for j in range(4) for n in order)
```

```python
import functools
import math

import jax
import jax.numpy as jnp
from jax import lax
from jax.experimental import pallas as pl
from jax.experimental.pallas import tpu as pltpu

F32 = jnp.float32
BF16 = jnp.bfloat16

N_DEV = 8
D_MODEL = 1024
N_HEADS = 8
HEAD_DIM = 128
RMS_EPS = 1e-6
ATTN_BLOCK = 128
HG_CHUNK = 64
HG_SUB = 16
HG_UNROLL_FWD = 16
HG_UNROLL_BWD = 16
EXP_CLAMP = 80.0
SB_HEADS_PER_STEP = 2
SB_QBLOCKS_PER_STEP = 4
SB_GROUP_COLS = SB_HEADS_PER_STEP * 128
SB_TOP_ROWS = 32
SB_LOG_WEIGHT_FLOOR = -104.0
VMEM_LIMIT_BYTES = 48 * 1024 * 1024
W_COLS = 4 * D_MODEL // N_DEV
W_ROWS = D_MODEL // N_DEV

ADAM_LR = 0.001
ADAM_B1 = 0.9
ADAM_B2 = 0.999
ADAM_EPS = 1e-08
ADAM_WD = 0.01
ADAM_STEP = 10

NT = (((1,), (1,)), ((), ()))
TN = (((0,), (0,)), ((), ()))
NN = (((1,), (0,)), ((), ()))


def _pcall(body, *, name, **kw):
    return pl.pallas_call(body, name=name, **kw)


def _params(*sem):
    return pltpu.CompilerParams(dimension_semantics=sem, vmem_limit_bytes=VMEM_LIMIT_BYTES)


def _dot(a, b, dims=NN):
    return lax.dot_general(a, b, dims, preferred_element_type=F32)


def _dot_exact(a, m, dims=NN, left=False):
    hi = a.astype(BF16)
    lo = (a - hi.astype(F32)).astype(BF16)
    if left:
        return _dot(m, hi, dims) + _dot(m, lo, dims)
    return _dot(hi, m, dims) + _dot(lo, m, dims)


def _sigmoid(x):
    return 1.0 / (1.0 + jnp.exp(-x))


def _sigmoid_pair(x):
    e = jnp.exp(-jnp.abs(x))
    big = 1.0 / (1.0 + e)
    small = e * big
    pos = x >= 0
    return jnp.where(pos, big, small), jnp.where(pos, small, big)


def _rms_scale(x):
    return lax.rsqrt(jnp.mean(x * x, axis=-1, keepdims=True) + RMS_EPS)


def _row_tile(t, want):
    return want if t % want == 0 else t


MESH = pl.DeviceIdType.MESH
N_PEERS = N_DEV - 1
_ANY = pl.BlockSpec(memory_space=pl.ANY)
_VMEM = pl.BlockSpec(memory_space=pltpu.VMEM)


def _mesh_pos():
    return lax.axis_index("x"), lax.axis_index("y"), lax.axis_index("c")


def _linear(pos):
    return 4 * pos[0] + 2 * pos[1] + pos[2]


def _peer(pos, k):
    flips = ((k + 1) >> 2 & 1, (k + 1) >> 1 & 1, (k + 1) & 1)
    return tuple(1 - p if f else p for p, f in zip(pos, flips))


def _chip(pos):
    return 2 * pos[0] + pos[1]


ALL_PEERS = tuple(range(N_PEERS))
SAME_CORE_PEERS = (1, 3, 5)


def _exchange_copies(pairs, send_sems, recv_sems, local_sems, pos, landing, peers, slot):
    me = slot(pos)
    local, remote = [], []
    for a, (src_of, dst) in enumerate(pairs):
        local.append(pltpu.make_async_copy(src_of(pos), dst.at[me], local_sems.at[a]))
        for k in peers:
            peer = _peer(pos, k)
            remote.append(pltpu.make_async_remote_copy(
                src_ref=src_of(peer), dst_ref=dst.at[slot(peer) if landing else me],
                send_sem=send_sems.at[a, k], recv_sem=recv_sems.at[a, k], device_id=peer, device_id_type=MESH))
    return local, remote


def _exchange_start(pairs, send_sems, recv_sems, local_sems, pos, peers=ALL_PEERS, slot=_linear):
    local, sent = _exchange_copies(pairs, send_sems, recv_sems, local_sems, pos, False, peers, slot)
    for copy in local + sent:
        copy.start()


def _exchange_wait(pairs, send_sems, recv_sems, local_sems, pos, peers=ALL_PEERS, slot=_linear):
    local, landed = _exchange_copies(pairs, send_sems, recv_sems, local_sems, pos, True, peers, slot)
    for copy in landed:
        copy.wait_recv()
        copy.wait_send()
    for copy in local:
        copy.wait()


def _exchange_sems(n):
    return [pltpu.SemaphoreType.DMA((n, N_PEERS)), pltpu.SemaphoreType.DMA((n, N_PEERS)),
            pltpu.SemaphoreType.DMA((n,))]


class _Rider:
    def __init__(self, arrays, scatter, chips=False):
        self.arrays = list(arrays)
        self.scatter = scatter
        self.peers = SAME_CORE_PEERS if chips else ALL_PEERS
        self.slot = _chip if chips else _linear
        self.out_shapes = [jax.ShapeDtypeStruct(a.shape if scatter else (N_DEV,) + a.shape, a.dtype)
                           for a in self.arrays]

    def pairs(self, in_refs, out_refs):
        if self.scatter:
            return [((lambda pos, r=r: r.at[self.slot(pos)]), o) for r, o in zip(in_refs, out_refs)]
        return [((lambda pos, r=r: r), o) for r, o in zip(in_refs, out_refs)]


def _pcall_riding(body, rider, args, *, name, grid, in_specs, out_specs, out_shape, semantics, scratch_shapes=(),
                  input_output_aliases=None):
    aliases = input_output_aliases or {}
    if rider is None:
        outs = _pcall(body, name=name, grid=grid, in_specs=list(in_specs), out_specs=list(out_specs),
                      out_shape=list(out_shape), scratch_shapes=list(scratch_shapes), input_output_aliases=aliases,
                      compiler_params=_params(*semantics))(*args)
        return list(outs), []
    n_in, n_out, n_scr, n_r = len(in_specs), len(out_specs), len(scratch_shapes), len(rider.arrays)

    def riding(*refs):
        ins, refs = refs[:n_in], refs[n_in:]
        rider_in, refs = refs[:n_r], refs[n_r:]
        outs, refs = refs[:n_out], refs[n_out:]
        rider_out, refs = refs[:n_r], refs[n_r:]
        scratch, sems = refs[:n_scr], refs[n_scr:]
        pairs = rider.pairs(rider_in, rider_out)
        first = functools.reduce(jnp.logical_and, [pl.program_id(a) == 0 for a in range(len(grid))])
        last = functools.reduce(jnp.logical_and, [pl.program_id(a) == g - 1 for a, g in enumerate(grid)])

        @pl.when(first)
        def _():
            _exchange_start(pairs, *sems, _mesh_pos(), rider.peers, rider.slot)

        body(*ins, *outs, *scratch)

        @pl.when(last)
        def _():
            _exchange_wait(pairs, *sems, _mesh_pos(), rider.peers, rider.slot)

    outs = _pcall(riding, name=name, grid=grid, in_specs=list(in_specs) + [_ANY] * n_r,
                  out_specs=list(out_specs) + [_ANY] * n_r, out_shape=list(out_shape) + rider.out_shapes,
                  scratch_shapes=list(scratch_shapes) + _exchange_sems(n_r), input_output_aliases=aliases,
                  compiler_params=_params(*(("arbitrary",) * len(grid))))(*args, *rider.arrays)
    return list(outs[:n_out]), list(outs[n_out:])


def rms_inproj(x2, gain, wg, name, rider=None, qk_gains=None):
    t = x2.shape[0]
    with_qkv = qk_gains is not None
    tm = _row_tile(t, 256 if with_qkv else 512)

    def body(x_ref, g_ref, w_ref, *rest):
        if with_qkv:
            qg_ref, kg_ref, proj_ref, ut_ref, qkv_ref = rest
            head_gain = (qg_ref, kg_ref)
        else:
            proj_ref, ut_ref = rest
        x = x_ref[...]
        u = x * _rms_scale(x) * g_ref[...]
        ut_ref[...] = u.T.astype(BF16)
        u = u.astype(BF16)
        for p in range(N_DEV):
            part, lo = p // 2, (p % 2) * W_COLS
            res = _dot(u, w_ref[p])
            proj_ref[part, :, lo:lo + W_COLS] = res
            if with_qkv and part < 3:
                for h in range(W_COLS // HEAD_DIM):
                    y = res[:, h * HEAD_DIM:(h + 1) * HEAD_DIM]
                    if part < 2:
                        y = y * _rms_scale(y) * head_gain[part][...]
                    qkv_ref[part, :, lo + h * HEAD_DIM:lo + (h + 1) * HEAD_DIM] = y.astype(BF16)

    vec = pl.BlockSpec((1, D_MODEL), lambda i: (0, 0))
    in_specs = [pl.BlockSpec((tm, D_MODEL), lambda i: (i, 0)), vec,
                pl.BlockSpec((N_DEV, D_MODEL, W_COLS), lambda i: (0, 0, 0))]
    out_specs = [pl.BlockSpec((4, tm, D_MODEL), lambda i: (0, i, 0)), pl.BlockSpec((D_MODEL, tm), lambda i: (0, i))]
    out_shape = [jax.ShapeDtypeStruct((4, t, D_MODEL), F32), jax.ShapeDtypeStruct((D_MODEL, t), BF16)]
    args = (x2, gain, wg)
    if with_qkv:
        in_specs += [pl.BlockSpec((1, HEAD_DIM), lambda i: (0, 0))] * 2
        out_specs.append(pl.BlockSpec((3, tm, D_MODEL), lambda i: (0, i, 0)))
        out_shape.append(jax.ShapeDtypeStruct((3, t, D_MODEL), BF16))
        args += tuple(qk_gains)
    return _pcall_riding(body, rider, args, name=name, grid=(t // tm,), in_specs=in_specs, out_specs=out_specs,
                         out_shape=out_shape, semantics=("parallel",))


def outproj_residual(og2, w_out, resid, target, name):
    t = og2.shape[0]
    tm = _row_tile(t, 512)
    with_loss = target is not None

    def body(og_ref, w_ref, r_ref, *rest):
        h = r_ref[...] + _dot(og_ref[...], w_ref[...])
        if with_loss:
            t_ref, dh_ref, loss_ref = rest
            err = h - t_ref[...]
            dh_ref[...] = err * (1.0 / D_MODEL)
            part = 0.5 * jnp.sum(jnp.mean(err * err, axis=-1, keepdims=True))
            loss_ref[...] = jnp.full(loss_ref.shape, part, F32)
        else:
            (h_ref,) = rest
            h_ref[...] = h

    row = pl.BlockSpec((tm, D_MODEL), lambda i: (i, 0))
    in_specs = [row, pl.BlockSpec((D_MODEL, D_MODEL), lambda i: (0, 0)), row]
    args = [og2, w_out, resid]
    if with_loss:
        in_specs.append(row)
        args.append(target)
        out_specs = [row, pl.BlockSpec((1, 8, 128), lambda i: (i, 0, 0))]
        out_shape = [jax.ShapeDtypeStruct((t, D_MODEL), F32),
                     jax.ShapeDtypeStruct((t // tm, 8, 128), F32)]
    else:
        out_specs = row
        out_shape = jax.ShapeDtypeStruct((t, D_MODEL), F32)
    return _pcall(body, name=name, grid=(t // tm,), in_specs=in_specs, out_specs=out_specs,
                  out_shape=out_shape, compiler_params=_params("parallel"))(*args)


def _head_spec(s, part):
    return pl.BlockSpec((1, 1, s, HEAD_DIM), lambda b, h: (part, b, 0, h))


def _seq_spec(s):
    return pl.BlockSpec((1, s, HEAD_DIM), lambda b, h: (b, 0, h))


_GAIN_SPEC = pl.BlockSpec((1, HEAD_DIM), lambda b, h: (0, 0))
_HEAD_ROW_SPEC = pl.BlockSpec((1, 1, 1, HEAD_DIM), lambda b, h: (b, h, 0, 0))


def _sb_group_spec(s, part):
    return pl.BlockSpec((1, 1, s, SB_GROUP_COLS), lambda b, g: (part, b, 0, g))


def _sb_seq_group_spec(s):
    return pl.BlockSpec((1, s, SB_GROUP_COLS), lambda b, g: (b, 0, g))


_SB_GROUP_ROW_SPEC = pl.BlockSpec((1, SB_HEADS_PER_STEP, 1, HEAD_DIM), lambda b, g: (b, g, 0, 0))


def _sb_chains(m):
    return [(h, m * SB_QBLOCKS_PER_STEP + r) for h in range(SB_HEADS_PER_STEP) for r in range(SB_QBLOCKS_PER_STEP)]


def _sb_logits(qi, kj):
    return _dot(qi, kj, NT) * (HEAD_DIM ** -0.5)


def _sb_scores(z, diag, live, tri_lt):
    soft = jnp.log(1.0 + jnp.exp(-jnp.abs(z)))
    valid = jnp.logical_and(live, jnp.logical_or(jnp.logical_not(diag), tri_lt))
    log_skip = jnp.where(valid, -(jnp.maximum(z, 0.0) + soft), 0.0)
    log_beta = jnp.minimum(z, 0.0) - soft
    return log_skip, log_beta, valid


def _sb_keys_left(chains, watch, state):
    done, carries = state[0], state[1]
    worst = None
    for (_, i), c in zip(chains, carries):
        c = jnp.where(done <= i, c[watch], -jnp.inf)
        worst = c if worst is None else jnp.maximum(worst, c)
    return jnp.logical_and(done <= chains[-1][1],
                           jnp.logical_or(done == 0, jnp.max(worst) > SB_LOG_WEIGHT_FLOOR))


def _sb_key_rows(i, done):
    j = i - done
    return pl.ds(pl.multiple_of(jnp.maximum(j, 0) * ATTN_BLOCK, ATTN_BLOCK), ATTN_BLOCK), j >= 0


def _sb_head(ref, h, rows):
    return ref[0, 0, rows, h * HEAD_DIM:(h + 1) * HEAD_DIM]


def _sb_rows(i, offset, count):
    return pl.ds(pl.multiple_of(jnp.maximum(i, 0) * ATTN_BLOCK + offset, math.gcd(ATTN_BLOCK, SB_TOP_ROWS)), count)


def sb_attn_fwd(qkv4, proj4, rider=None):
    _, b, s, _ = qkv4.shape
    blk, top = ATTN_BLOCK, SB_TOP_ROWS
    ngroups = s // blk // SB_QBLOCKS_PER_STEP
    assert ngroups * SB_QBLOCKS_PER_STEP * blk == s

    def body(q_ref, k_ref, v_ref, gate_ref, o_ref, og_ref):
        def write(h, rows, o):
            cols = slice(h * HEAD_DIM, (h + 1) * HEAD_DIM)
            g = _sb_head(gate_ref, h, rows)
            o_ref[0, rows, cols] = o
            og_ref[0, rows, cols] = (o * (g * _sigmoid(g))).astype(BF16)

        row = lax.broadcasted_iota(jnp.int32, (blk, blk), 0)
        col = lax.broadcasted_iota(jnp.int32, (blk, blk), 1)
        tri_lt = col < row
        suffix = (row > col).astype(BF16)

        def step(items):
            where = [_sb_key_rows(i, done) for _, i, done, _, _, _, _ in items]
            zs = [_sb_logits(q, _sb_head(k_ref, h, rows)) for (h, _, _, q, _, _, _), (rows, _) in zip(items, where)]
            scored = [_sb_scores(z, done == 0, live, mask)
                      for z, (_, _, done, _, mask, _, _), (_, live) in zip(zs, items, where)]
            afters = [_dot_exact(log_skip, suffix) for log_skip, _, _ in scored]
            ws = [jnp.where(valid, jnp.exp(log_beta + after + c), 0.0).astype(BF16)
                  for (_, log_beta, valid), after, (_, _, _, _, _, c, _) in zip(scored, afters, items)]
            accs = [acc + _dot(w, _sb_head(v_ref, h, rows))
                    for (h, _, _, _, _, _, acc), (rows, _), w in zip(items, where, ws)]
            cs = [c + jnp.sum(log_skip, axis=1, keepdims=True)
                  for (log_skip, _, _), (_, _, _, _, _, c, _) in zip(scored, items)]
            return cs, accs

        def top_items(chains, done, cs, accs):
            return [(h, i, done, _sb_head(q_ref, h, _sb_rows(i, 0, top)), tri_lt[:top, :], c, acc)
                    for (h, i), c, acc in zip(chains, cs, accs)]

        def finish_tops(chains, state):
            def k_step(state):
                done, cs, accs = state
                cs, accs = step(top_items(chains, done, cs, accs))
                return done + 1, tuple(cs), tuple(accs)

            _, _, accs = lax.while_loop(functools.partial(_sb_keys_left, chains, slice(0, top)), k_step, state)
            for (h, i), acc in zip(chains, accs):
                write(h, _sb_rows(i, 0, top), acc)

        def q_group(m, before):
            chains, chains_before = _sb_chains(m), _sb_chains(m - 1)
            qis = [_sb_head(q_ref, h, _sb_rows(i, 0, blk)) for h, i in chains]
            n = len(chains)

            def items_of(done, cs, accs):
                return [(h, i, done, q, tri_lt, c, acc) for (h, i), q, c, acc in zip(chains, qis, cs, accs)]

            def k_step(state):
                done, cs, accs = state
                cs, accs = step(items_of(done, cs, accs))
                return done + 1, tuple(cs), tuple(accs)

            done_b, cs_b, accs_b = before
            cs0, accs0 = step(items_of(jnp.int32(0), (jnp.zeros((blk, 1), F32),) * n,
                                       (jnp.zeros((blk, HEAD_DIM), F32),) * n)
                              + top_items(chains_before, done_b, cs_b, accs_b))
            done, cs, accs = lax.while_loop(functools.partial(_sb_keys_left, chains, slice(top, blk)), k_step,
                                            (jnp.int32(1), tuple(cs0[:n]), tuple(accs0[:n])))

            @pl.when(m > 0)
            def _():
                finish_tops(chains_before, (done_b + 1, tuple(cs0[n:]), tuple(accs0[n:])))

            for (h, i), acc in zip(chains, accs):
                write(h, _sb_rows(i, top, blk - top), acc[top:, :])
            return done, tuple(c[:top, :] for c in cs), tuple(acc[:top, :] for acc in accs)

        n = SB_HEADS_PER_STEP * SB_QBLOCKS_PER_STEP
        nothing = (jnp.int32(0), (jnp.zeros((top, 1), F32),) * n, (jnp.zeros((top, HEAD_DIM), F32),) * n)
        last = lax.fori_loop(0, ngroups, q_group, nothing)
        finish_tops(_sb_chains(ngroups - 1), last)

    return _pcall_riding(
        body, rider, (qkv4, qkv4, qkv4, proj4),
        name="sb_attn_fwd", grid=(b, N_HEADS // SB_HEADS_PER_STEP),
        in_specs=[_sb_group_spec(s, 0), _sb_group_spec(s, 1), _sb_group_spec(s, 2), _sb_group_spec(s, 3)],
        out_specs=[_sb_seq_group_spec(s)] * 2,
        out_shape=[jax.ShapeDtypeStruct((b, s, D_MODEL), F32), jax.ShapeDtypeStruct((b, s, D_MODEL), BF16)],
        semantics=("parallel", "parallel"))


def _hg_masks():
    c = HG_CHUNK
    row = lax.broadcasted_iota(jnp.int32, (c, c), 0)
    col = lax.broadcasted_iota(jnp.int32, (c, c), 1)
    incl = (col <= row)
    lower = incl.astype(BF16)
    before_sub = (col < (row // HG_SUB) * HG_SUB).astype(BF16)
    upper = (col >= row).astype(BF16)
    return incl, lower, before_sub, upper


def _hg_lower_bound(lbl_ref):
    l0 = lbl_ref[0, 0]
    l1 = lbl_ref[1, 0]
    d = l1 - l0
    return _sigmoid_pair(d)


def _hg_gates(qp, fp, lb, oml):
    sq = _sigmoid(qp)
    sf, sfn = _sigmoid_pair(fp)
    f = lb + oml * sf
    return dict(qp=qp, sq=sq, q=qp * sq, sf=sf, sfn=sfn, f=f, k=oml * sfn, logf=jnp.log(f))


def _hg_intra(qds, ks, gcs, grs, incl):
    subs = range(HG_CHUNK // HG_SUB)
    qdbs = [qd.astype(BF16) for qd in qds]
    ess = [[jnp.exp(jnp.minimum(gr[sub * HG_SUB:sub * HG_SUB + 1, :] - gc, EXP_CLAMP)) for sub in subs]
           for gc, gr in zip(gcs, grs)]
    ksbs = [[(k * e).astype(BF16) for e in es] for k, es in zip(ks, ess)]
    rows = [[_dot(qdb[sub * HG_SUB:(sub + 1) * HG_SUB, :], ksb[sub], NT) for sub in subs]
            for qdb, ksb in zip(qdbs, ksbs)]
    a_s = [jnp.where(incl, jnp.concatenate(r, axis=0), 0.0) for r in rows]
    return a_s, qdbs, ksbs, ess


def _hg_group_rows(outer, unroll):
    ns = [outer * unroll + u for u in range(unroll)]
    return ns, [pl.ds(pl.multiple_of(n * HG_CHUNK, HG_CHUNK), HG_CHUNK) for n in ns]


def _state_spec(nchunk):
    return pl.BlockSpec((1, 1, nchunk, HEAD_DIM, HEAD_DIM), lambda b, h: (b, h, 0, 0, 0))


def hgrn2_fwd(proj4, lbl4, o_gain, rider=None):
    _, b, s, _ = proj4.shape
    nchunk = s // HG_CHUNK
    c = HG_CHUNK
    unroll = math.gcd(nchunk, HG_UNROLL_FWD)

    def body(q_ref, f_ref, i_ref, gate_ref, lbl_ref, og_ref, o_ref, oraw_ref, st_ref, ogate_ref):
        incl, lower, before_sub, _ = _hg_masks()
        lb, oml = _hg_lower_bound(lbl_ref)

        def group(outer, st):
            ns, rows = _hg_group_rows(outer, unroll)
            vs = [_hg_gates(q_ref[0, 0, r, :], f_ref[0, 0, r, :], lb, oml) for r in rows]
            inps = [i_ref[0, 0, r, :].astype(BF16) for r in rows]
            gcs = [_dot_exact(v["logf"], lower, left=True) for v in vs]
            grs = [_dot_exact(v["logf"], before_sub, left=True) for v in vs]
            a_s, _, _, _ = _hg_intra([v["q"] * jnp.exp(gc - gr) for v, gc, gr in zip(vs, gcs, grs)],
                                     [v["k"] for v in vs], gcs, grs, incl)
            gls = [gc[c - 1:c, :] for gc in gcs]
            adds = [_dot(inp, (v["k"] * jnp.exp(gl - gc)).astype(BF16), TN)
                    for inp, v, gl, gc in zip(inps, vs, gls, gcs)]
            o_intra = [_dot(a.astype(BF16), inp) for a, inp in zip(a_s, inps)]
            sts = []
            for gl, add in zip(gls, adds):
                sts.append(st)
                st = st * jnp.exp(gl) + add
            outs = [oi + _dot((v["q"] * jnp.exp(gc)).astype(BF16), s0.astype(BF16), NT)
                    for oi, v, gc, s0 in zip(o_intra, vs, gcs, sts)]
            for n, r, s0, o in zip(ns, rows, sts, outs):
                st_ref[0, 0, n] = s0
                oraw_ref[0, r, :] = o
                normed = o * _rms_scale(o) * og_ref[...]
                g = gate_ref[0, 0, r, :]
                o_ref[0, r, :] = normed
                ogate_ref[0, r, :] = (normed * (g * _sigmoid(g))).astype(BF16)
            return st

        lax.fori_loop(0, nchunk // unroll, group, jnp.zeros((HEAD_DIM, HEAD_DIM), F32))

    seq = jax.ShapeDtypeStruct((b, s, D_MODEL), F32)
    return _pcall_riding(
        body, rider, (proj4, proj4, proj4, proj4, lbl4, o_gain), name="hgrn2_fwd", grid=(b, N_HEADS),
        in_specs=[_head_spec(s, 0), _head_spec(s, 1), _head_spec(s, 2), _head_spec(s, 3),
                  pl.BlockSpec((2, 1, 1, HEAD_DIM), lambda b, h: (0, h, 0, 0)), _GAIN_SPEC],
        out_specs=[_seq_spec(s), _seq_spec(s), _state_spec(nchunk), _seq_spec(s)],
        out_shape=[seq, seq, jax.ShapeDtypeStruct((b, N_HEADS, nchunk, HEAD_DIM, HEAD_DIM), F32),
                   jax.ShapeDtypeStruct((b, s, D_MODEL), BF16)],
        semantics=("parallel", "parallel"))


def outproj_bwd(dh, w_out, o2, proj, name):
    t = dh.shape[0]
    tm = _row_tile(t, 512)

    def body(dh_ref, w_ref, o_ref, gate_ref, do_ref, dproj_ref, dw_ref):
        dhb = dh_ref[...].astype(BF16)
        dog = _dot(dhb, w_ref[...], NT)
        g = gate_ref[0]
        sg = _sigmoid(g)
        silu = g * sg
        o = o_ref[...]
        do_ref[...] = dog * silu
        dproj_ref[0] = dog * o * (sg * (1.0 + g * (1.0 - sg)))
        part = _dot((o * silu).astype(BF16), dhb, TN)

        @pl.when(pl.program_id(0) == 0)
        def _():
            dw_ref[...] = part

        @pl.when(pl.program_id(0) > 0)
        def _():
            dw_ref[...] += part

    row = pl.BlockSpec((tm, D_MODEL), lambda i: (i, 0))
    full = pl.BlockSpec((D_MODEL, D_MODEL), lambda i: (0, 0))
    return _pcall(
        body, name=name, grid=(t // tm,),
        in_specs=[row, full, row, pl.BlockSpec((1, tm, D_MODEL), lambda i: (3, i, 0))],
        out_specs=[row, pl.BlockSpec((1, tm, D_MODEL), lambda i: (3, i, 0)), full],
        out_shape=[jax.ShapeDtypeStruct((t, D_MODEL), F32),
                   jax.ShapeDtypeStruct((4, t, D_MODEL), F32),
                   jax.ShapeDtypeStruct((D_MODEL, D_MODEL), F32)],
        compiler_params=_params("arbitrary"),
    )(dh, w_out, o2, proj)


def inproj_bwd_dx(dproj, wg, x2, gain, dres, name, rider=None):
    t = x2.shape[0]
    tm = _row_tile(t, 512)

    def body(d_ref, w_ref, x_ref, g_ref, r_ref, dx_ref, dg_ref):
        du = jnp.zeros((tm, D_MODEL), F32)
        for p in range(N_DEV):
            cols = slice((p % 2) * W_COLS, (p % 2 + 1) * W_COLS)
            du = du + _dot(d_ref[p // 2, :, cols].astype(BF16), w_ref[p], NT)
        x = x_ref[...]
        r = _rms_scale(x)
        xh = x * r
        a = du * g_ref[...]
        dx_ref[...] = r_ref[...] + r * (a - xh * jnp.mean(a * xh, axis=-1, keepdims=True))
        part = jnp.sum(du * xh, axis=0, keepdims=True)

        @pl.when(pl.program_id(0) == 0)
        def _():
            dg_ref[...] = part

        @pl.when(pl.program_id(0) > 0)
        def _():
            dg_ref[...] += part

    row = pl.BlockSpec((tm, D_MODEL), lambda i: (i, 0))
    vec = pl.BlockSpec((1, D_MODEL), lambda i: (0, 0))
    return _pcall_riding(
        body, rider, (dproj, wg, x2, gain, dres), name=name, grid=(t // tm,),
        in_specs=[pl.BlockSpec((4, tm, D_MODEL), lambda i: (0, i, 0)),
                  pl.BlockSpec((N_DEV, D_MODEL, W_COLS), lambda i: (0, 0, 0)),
                  row, vec, row],
        out_specs=[row, vec],
        out_shape=[jax.ShapeDtypeStruct((t, D_MODEL), F32), jax.ShapeDtypeStruct((1, D_MODEL), F32)],
        semantics=("arbitrary",))


def inproj_bwd_dw(ut, dproj, name, out_dtype):
    t = ut.shape[1]

    def body(ut_ref, d_ref, dw_ref):
        dw_ref[0] = _dot(ut_ref[...], d_ref[0].astype(BF16)).astype(dw_ref.dtype)

    return _pcall(
        body, name=name, grid=(N_DEV,),
        in_specs=[pl.BlockSpec((D_MODEL, t), lambda j: (0, 0)),
                  pl.BlockSpec((1, t, W_COLS), lambda j: (j // 2, 0, j % 2))],
        out_specs=pl.BlockSpec((1, D_MODEL, W_COLS), lambda j: (j, 0, 0)),
        out_shape=jax.ShapeDtypeStruct((N_DEV, D_MODEL, W_COLS), out_dtype),
        compiler_params=_params("parallel"),
    )(ut, dproj)


def _rms_bwd(x, gain, dy):
    r = _rms_scale(x)
    xh = x * r
    a = dy * gain
    return r * (a - xh * jnp.mean(a * xh, axis=-1, keepdims=True)), dy * xh


def sb_attn_bwd(qkv4, proj4, do3, o3, q_gain, k_gain, dproj4, rider=None):
    _, b, s, _ = proj4.shape
    blk, top = ATTN_BLOCK, SB_TOP_ROWS
    nq = s // blk
    ngroups = nq // SB_QBLOCKS_PER_STEP
    assert ngroups * SB_QBLOCKS_PER_STEP * blk == s
    scale = HEAD_DIM ** -0.5

    def body(qn_ref, kn_ref, v_ref, q_ref, k_ref, do_ref, o_ref, qg_ref, kg_ref, _alias, d_ref, dqg_ref, dkg_ref, dob):
        for h in range(SB_HEADS_PER_STEP):
            dob[h] = do_ref[0, :, h * HEAD_DIM:(h + 1) * HEAD_DIM].astype(BF16)
        d_ref[...] = jnp.zeros_like(d_ref)
        row = lax.broadcasted_iota(jnp.int32, (blk, blk), 0)
        col = lax.broadcasted_iota(jnp.int32, (blk, blk), 1)
        tri_lt = col < row
        suffix = (row > col).astype(BF16)
        suffix_incl = (row >= col).astype(BF16)

        def step(items):
            heads = [it[0] for it in items]
            where = [_sb_key_rows(it[1], it[2]) for it in items]
            kjs = [_sb_head(kn_ref, h, rows) for h, (rows, _) in zip(heads, where)]
            zs = [_sb_logits(it[3], kj) for it, kj in zip(items, kjs)]
            dws = [_dot(it[4], _sb_head(v_ref, h, rows), NT) for it, h, (rows, _) in zip(items, heads, where)]
            scored = [_sb_scores(z, it[2] == 0, live, it[6]) for z, it, (_, live) in zip(zs, items, where)]
            afters = [_dot_exact(log_skip, suffix) for log_skip, _, _ in scored]
            wbs = [jnp.where(valid, jnp.exp(log_beta + after + it[7]), 0.0).astype(BF16)
                   for (_, log_beta, valid), after, it in zip(scored, afters, items)]
            gs = [dw * wb.astype(F32) for dw, wb in zip(dws, wbs)]
            befores = [it[5] - (_dot_exact(g, suffix_incl) + it[8]) for g, it in zip(gs, items)]
            dzbs = [jnp.where(valid, g - jnp.exp(log_beta) * (g + before), 0.0).astype(BF16)
                    for (_, log_beta, valid), g, before in zip(scored, gs, befores)]
            dqs = [it[9] + _dot(dzb, kj) for it, dzb, kj in zip(items, dzbs, kjs)]
            for it, h, (rows, _), wb, dzb in zip(items, heads, where, wbs, dzbs):
                cols = slice(h * HEAD_DIM, (h + 1) * HEAD_DIM)
                d_ref[2, 0, rows, cols] += _dot(wb, it[4], TN)
                d_ref[1, 0, rows, cols] += _dot(dzb, it[3], TN)
            cs = [it[7] + jnp.sum(log_skip, axis=1, keepdims=True) for (log_skip, _, _), it in zip(scored, items)]
            cgs = [it[8] + jnp.sum(g, axis=1, keepdims=True) for g, it in zip(gs, items)]
            return cs, cgs, dqs

        def top_items(chains, deltas, done, cs, cgs, dqs):
            return [(h, i, done, _sb_head(qn_ref, h, _sb_rows(i, 0, top)), dob[h, _sb_rows(i, 0, top), :], delta,
                     tri_lt[:top, :], c, cg, dq)
                    for (h, i), delta, c, cg, dq in zip(chains, deltas, cs, cgs, dqs)]

        def finish_tops(chains, deltas, state):
            def k_step(state):
                done, cs, cgs, dqs = state
                cs, cgs, dqs = step(top_items(chains, deltas, done, cs, cgs, dqs))
                return done + 1, tuple(cs), tuple(cgs), tuple(dqs)

            _, _, _, dqs = lax.while_loop(functools.partial(_sb_keys_left, chains, slice(0, top)), k_step, state)
            for (h, i), dq in zip(chains, dqs):
                d_ref[0, 0, _sb_rows(i, 0, top), h * HEAD_DIM:(h + 1) * HEAD_DIM] = dq * scale

        def q_group(m, before):
            chains, chains_before = _sb_chains(m), _sb_chains(m - 1)
            deltas_b, before = before[0], before[1:]
            qis, dois, deltas = [], [], []
            for h, i in chains:
                rows_i = _sb_rows(i, 0, blk)
                qis.append(_sb_head(qn_ref, h, rows_i))
                dois.append(dob[h, rows_i, :])
                deltas.append(jnp.sum(dois[-1].astype(F32) * o_ref[0, rows_i, h * HEAD_DIM:(h + 1) * HEAD_DIM],
                                      axis=1, keepdims=True))
            n = len(chains)

            def items_of(done, cs, cgs, dqs):
                return [(h, i, done, q, do, delta, tri_lt, c, cg, dq)
                        for (h, i), q, do, delta, c, cg, dq in zip(chains, qis, dois, deltas, cs, cgs, dqs)]

            def k_step(state):
                done, cs, cgs, dqs = state
                cs, cgs, dqs = step(items_of(done, cs, cgs, dqs))
                return done + 1, tuple(cs), tuple(cgs), tuple(dqs)

            done_b, cs_b, cgs_b, dqs_b = before
            zero = (jnp.zeros((blk, 1), F32),) * n
            new = step(items_of(jnp.int32(0), zero, zero, (jnp.zeros((blk, HEAD_DIM), F32),) * n)
                       + top_items(chains_before, deltas_b, done_b, cs_b, cgs_b, dqs_b))
            done, cs, cgs, dqs = lax.while_loop(functools.partial(_sb_keys_left, chains, slice(top, blk)), k_step,
                                                (jnp.int32(1),) + tuple(tuple(x[:n]) for x in new))

            @pl.when(m > 0)
            def _():
                finish_tops(chains_before, deltas_b, (done_b + 1,) + tuple(tuple(x[n:]) for x in new))

            for (h, i), dq in zip(chains, dqs):
                d_ref[0, 0, _sb_rows(i, top, blk - top), h * HEAD_DIM:(h + 1) * HEAD_DIM] = dq[top:, :] * scale
            first = lambda xs: tuple(x[:top, :] for x in xs)
            return first(deltas), done, first(cs), first(cgs), first(dqs)

        n = SB_HEADS_PER_STEP * SB_QBLOCKS_PER_STEP
        zero = (jnp.zeros((top, 1), F32),) * n
        nothing = (zero, jnp.int32(0), zero, zero, (jnp.zeros((top, HEAD_DIM), F32),) * n)
        last = lax.fori_loop(0, ngroups, q_group, nothing)
        finish_tops(_sb_chains(ngroups - 1), last[0], last[1:])

        def norm_block(i, carry):
            rows = pl.ds(pl.multiple_of(i * blk, blk), blk)
            out = []
            for h in range(SB_HEADS_PER_STEP):
                cols = slice(h * HEAD_DIM, (h + 1) * HEAD_DIM)
                for part, src_ref, gain_ref in ((0, q_ref, qg_ref), (1, k_ref, kg_ref)):
                    dy = d_ref[part, 0, rows, cols] * (scale if part == 1 else 1.0)
                    dx, pg = _rms_bwd(src_ref[0, 0, rows, cols], gain_ref[...], dy)
                    d_ref[part, 0, rows, cols] = dx
                    out.append(carry[len(out)] + jnp.sum(pg, axis=0, keepdims=True))
            return tuple(out)

        sums = lax.fori_loop(0, nq, norm_block, (jnp.zeros((1, HEAD_DIM), F32),) * (2 * SB_HEADS_PER_STEP))
        for h in range(SB_HEADS_PER_STEP):
            dqg_ref[0, h] = sums[2 * h]
            dkg_ref[0, h] = sums[2 * h + 1]

    head_row = jax.ShapeDtypeStruct((b, N_HEADS, 1, HEAD_DIM), F32)
    return _pcall_riding(
        body, rider, (qkv4, qkv4, qkv4, proj4, proj4, do3, o3, q_gain, k_gain, dproj4),
        name="sb_attn_bwd", grid=(b, N_HEADS // SB_HEADS_PER_STEP),
        in_specs=[_sb_group_spec(s, 0), _sb_group_spec(s, 1), _sb_group_spec(s, 2),
                  _sb_group_spec(s, 0), _sb_group_spec(s, 1),
                  _sb_seq_group_spec(s), _sb_seq_group_spec(s), _GAIN_SPEC, _GAIN_SPEC,
                  pl.BlockSpec(memory_space=pl.ANY)],
        out_specs=[pl.BlockSpec((3, 1, s, SB_GROUP_COLS), lambda b, g: (0, b, 0, g)),
                   _SB_GROUP_ROW_SPEC, _SB_GROUP_ROW_SPEC],
        out_shape=[jax.ShapeDtypeStruct(dproj4.shape, F32), head_row, head_row],
        scratch_shapes=[pltpu.VMEM((SB_HEADS_PER_STEP, s, HEAD_DIM), BF16)],
        input_output_aliases={9: 0}, semantics=("parallel", "parallel"))


def hgrn2_bwd(proj4, don3, oraw3, states, lbl4, o_gain, dproj4, rider=None):
    _, b, s, _ = proj4.shape
    nchunk = s // HG_CHUNK
    c = HG_CHUNK
    subs = range(HG_CHUNK // HG_SUB)
    unroll = math.gcd(nchunk, HG_UNROLL_BWD)
    ngroup = nchunk // unroll

    def body(q_ref, f_ref, i_ref, don_ref, oraw_ref, st_ref, lbl_ref, og_ref, _alias, d_ref, dog_ref, dlb_ref):
        incl, lower, before_sub, upper = _hg_masks()
        lb, oml = _hg_lower_bound(lbl_ref)
        last_row = lax.broadcasted_iota(jnp.int32, (c, HEAD_DIM), 0) == c - 1

        def group(m, carry):
            dst, dog_acc, dlb_acc = carry
            ns, rows = _hg_group_rows(ngroup - 1 - m, unroll)
            ns, rows = ns[::-1], rows[::-1]
            vs = [_hg_gates(q_ref[0, 0, r, :], f_ref[0, 0, r, :], lb, oml) for r in rows]
            inps = [i_ref[0, 0, r, :].astype(BF16) for r in rows]
            sts = [st_ref[0, 0, n] for n in ns]
            gcs = [_dot_exact(v["logf"], lower, left=True) for v in vs]
            grs = [_dot_exact(v["logf"], before_sub, left=True) for v in vs]
            e_qs = [jnp.exp(gc - gr) for gc, gr in zip(gcs, grs)]
            a_s, qdbs, ksbs, ess = _hg_intra([v["q"] * e for v, e in zip(vs, e_qs)], [v["k"] for v in vs],
                                             gcs, grs, incl)
            e_gcs = [jnp.exp(gc) for gc in gcs]
            gls = [gc[c - 1:c, :] for gc in gcs]
            e_gls = [jnp.exp(gl) for gl in gls]
            e_ks = [jnp.exp(gl - gc) for gl, gc in zip(gls, gcs)]
            normed = [_rms_bwd(oraw_ref[0, r, :], og_ref[...], don_ref[0, r, :]) for r in rows]
            dobs = [do.astype(BF16) for do, _ in normed]
            dabs = [jnp.where(incl, _dot(dob, inp, NT), 0.0).astype(BF16) for dob, inp in zip(dobs, inps)]
            adds = [_dot(dob, (v["q"] * e).astype(BF16), TN) for dob, v, e in zip(dobs, vs, e_gcs)]
            dq_inters = [_dot(dob, st.astype(BF16)) * e for dob, st, e in zip(dobs, sts, e_gcs)]
            dqds = [jnp.concatenate([_dot(dab[sub * HG_SUB:(sub + 1) * HG_SUB, :], ksb[sub]) for sub in subs], axis=0)
                    for dab, ksb in zip(dabs, ksbs)]
            dkss = [[_dot(dab[sub * HG_SUB:(sub + 1) * HG_SUB, :], qdb[sub * HG_SUB:(sub + 1) * HG_SUB, :], TN)
                     for sub in subs] for dab, qdb in zip(dabs, qdbs)]
            dsts = []
            for e_gl, add in zip(e_gls, adds):
                dsts.append(dst)
                dst = dst * e_gl + add
            dstbs = [d.astype(BF16) for d in dsts]
            dis = [_dot(a.astype(BF16), dob, TN) + _dot((v["k"] * e_k).astype(BF16), dstb, NT)
                   for a, dob, v, e_k, dstb in zip(a_s, dobs, vs, e_ks, dstbs)]
            dk_inters = [_dot(inp, dstb) * e_k for inp, dstb, e_k in zip(inps, dstbs, e_ks)]
            dks, dgcs = [], []
            for u in range(len(rows)):
                q, k = vs[u]["q"], vs[u]["k"]
                dk, dgc_k = dk_inters[u], jnp.zeros((c, HEAD_DIM), F32)
                for sub in subs:
                    dk = dk + dkss[u][sub] * ess[u][sub]
                    dgc_k = dgc_k + dkss[u][sub] * ksbs[u][sub].astype(F32)
                at_last = (jnp.sum(k * dk_inters[u], axis=0, keepdims=True)
                           + e_gls[u] * jnp.sum(sts[u] * dsts[u], axis=0, keepdims=True))
                dks.append(dk)
                dgcs.append((qdbs[u].astype(F32) * dqds[u] - dgc_k) + (q * dq_inters[u] - k * dk_inters[u])
                            + jnp.where(last_row, at_last, 0.0))
            dlf_fs = [_dot_exact(dgc, upper, left=True) / v["f"] for dgc, v in zip(dgcs, vs)]
            for u, r in enumerate(rows):
                v = vs[u]
                dq = dqds[u] * e_qs[u] + dq_inters[u]
                d_ref[0, 0, r, :] = dq * (v["sq"] * (1.0 + v["qp"] * (1.0 - v["sq"])))
                d_ref[1, 0, r, :] = (dlf_fs[u] - dks[u]) * (oml * v["sf"] * v["sfn"])
                d_ref[2, 0, r, :] = dis[u]
                dlb_acc = dlb_acc + jnp.sum((dlf_fs[u] - dks[u]) * v["sfn"], axis=0, keepdims=True)
                dog_acc = dog_acc + jnp.sum(normed[u][1], axis=0, keepdims=True)
            return dst, dog_acc, dlb_acc

        zero = jnp.zeros((1, HEAD_DIM), F32)
        _, dog, dlb = lax.fori_loop(0, ngroup, group, (jnp.zeros((HEAD_DIM, HEAD_DIM), F32), zero, zero))
        dog_ref[0, 0] = dog
        dlb_ref[0, 0] = dlb

    head_row = jax.ShapeDtypeStruct((b, N_HEADS, 1, HEAD_DIM), F32)
    return _pcall_riding(
        body, rider, (proj4, proj4, proj4, don3, oraw3, states, lbl4, o_gain, dproj4),
        name="hgrn2_bwd", grid=(b, N_HEADS),
        in_specs=[_head_spec(s, 0), _head_spec(s, 1), _head_spec(s, 2), _seq_spec(s), _seq_spec(s),
                  _state_spec(nchunk), pl.BlockSpec((2, 1, 1, HEAD_DIM), lambda b, h: (0, h, 0, 0)), _GAIN_SPEC,
                  pl.BlockSpec(memory_space=pl.ANY)],
        out_specs=[pl.BlockSpec((3, 1, s, HEAD_DIM), lambda b, h: (0, b, 0, h)), _HEAD_ROW_SPEC, _HEAD_ROW_SPEC],
        out_shape=[jax.ShapeDtypeStruct(dproj4.shape, F32), head_row, head_row],
        input_output_aliases={8: 0}, semantics=("parallel", "parallel"))


def local_step(x, target, sb_norm, wsi, sb_q_gain, sb_k_gain, hg_o_gain, hg_lb_logits, wso_mine, whi_mine, who_mine,
               hg_norm_mine):
    b, s, _ = x.shape
    t = b * s
    x2 = x.reshape(t, D_MODEL)
    tg2 = target.reshape(t, D_MODEL)
    lbl4 = hg_lb_logits.reshape(2, N_HEADS, 1, HEAD_DIM)
    four = (4, b, s, D_MODEL)
    three = (b, s, D_MODEL)
    rows8 = (N_DEV, W_ROWS, D_MODEL)

    (proj0, u0, qkv0), (wso, hgn) = rms_inproj(x2, sb_norm, wsi, "sb_inproj",
                                               _Rider([wso_mine, hg_norm_mine], scatter=False),
                                               qk_gains=(sb_q_gain, sb_k_gain))
    qkv0 = qkv0.reshape(3, b, s, D_MODEL)
    wso = wso.reshape(D_MODEL, D_MODEL)
    hg_norm_full = hgn[:, 0, :].reshape(1, D_MODEL)
    (o0, og0), (whi,) = sb_attn_fwd(qkv0, proj0.reshape(four), _Rider([whi_mine], scatter=False))
    o0 = o0.reshape(t, D_MODEL)
    h1 = outproj_residual(og0.reshape(t, D_MODEL), wso, x2, None, "sb_outproj")
    (proj1, u1), _ = rms_inproj(h1, hg_norm_full, whi, "hg_inproj")
    (o1, o1_raw, states, og1), (who,) = hgrn2_fwd(proj1.reshape(four), lbl4, hg_o_gain,
                                                  _Rider([who_mine], scatter=False))
    who = who.reshape(D_MODEL, D_MODEL)
    o1 = o1.reshape(t, D_MODEL)
    dh2, loss_parts = outproj_residual(og1.reshape(t, D_MODEL), who, h1, tg2, "hg_outproj_loss")

    do1, dproj1, g_who = outproj_bwd(dh2, who, o1, proj1, "hg_outproj_bwd")
    (dproj1, g_og, g_lb), (p_who,) = hgrn2_bwd(proj1.reshape(four), do1.reshape(three), o1_raw, states, lbl4,
                                               hg_o_gain, dproj1.reshape(four),
                                               _Rider([g_who.reshape(rows8)], scatter=True))
    dproj1 = dproj1.reshape(4, t, D_MODEL)
    (dh1, g_hgn), _ = inproj_bwd_dx(dproj1, whi, h1, hg_norm_full, dh2, "hg_inproj_bwd_dx")
    g_whi = inproj_bwd_dw(u1, dproj1, "hg_inproj_bwd_dw", out_dtype=BF16)

    do0, dproj0, g_wso = outproj_bwd(dh1, wso, o0, proj0, "sb_outproj_bwd")
    (dproj0, g_qg, g_kg), (p_whi, p_wso) = sb_attn_bwd(qkv0, proj0.reshape(four), do0.reshape(three), o0.reshape(three),
                                                       sb_q_gain, sb_k_gain, dproj0.reshape(four),
                                                       _Rider([g_whi, g_wso.reshape(rows8)], scatter=True))
    dproj0 = dproj0.reshape(4, t, D_MODEL)
    g_wsi = inproj_bwd_dw(u0, dproj0, "sb_inproj_bwd_dw", out_dtype=BF16)
    (gx, g_sbn), (p_wsi,) = inproj_bwd_dx(dproj0, wsi, x2, sb_norm, dh1, "sb_inproj_bwd_dx",
                                          _Rider([sum_within_chip(g_wsi)], scatter=True, chips=True))
    return dict(loss_parts=loss_parts, gx=gx.reshape(three), p_wsi=p_wsi, p_wso=p_wso, p_whi=p_whi, p_who=p_who,
                g_sbn=g_sbn, g_hgn=g_hgn, g_qg=g_qg, g_kg=g_kg, g_og=g_og, g_lb=g_lb)


def sum_within_chip(g):
    _, r, c_ = g.shape
    chips = N_DEV // 2

    def swap(g_ref, got_ref, send_sems, recv_sems):
        x, y, c = _mesh_pos()
        copies = [pltpu.make_async_remote_copy(
            src_ref=g_ref.at[2 * q + 1 - c], dst_ref=got_ref.at[q], send_sem=send_sems.at[q], recv_sem=recv_sems.at[q],
            device_id=(x, y, 1 - c), device_id_type=MESH) for q in range(chips)]
        for cp in copies:
            cp.start()
        for cp in copies:
            cp.wait_recv()
            cp.wait_send()

    got = _pcall(swap, name="swap_with_sibling", in_specs=[_ANY], out_specs=_ANY,
                 out_shape=jax.ShapeDtypeStruct((chips, r, c_), g.dtype),
                 scratch_shapes=[pltpu.SemaphoreType.DMA((chips,)), pltpu.SemaphoreType.DMA((chips,))])(g)

    def add(g_ref, got_ref, out_ref):
        mine = g_ref[0, lax.axis_index("c")]
        out_ref[0] = (mine.astype(F32) + got_ref[0].astype(F32)).astype(out_ref.dtype)

    return _pcall(
        add, name="add_sibling_partials", grid=(chips,),
        in_specs=[pl.BlockSpec((1, 2, r, c_), lambda q: (q, 0, 0, 0)), pl.BlockSpec((1, r, c_), lambda q: (q, 0, 0))],
        out_specs=pl.BlockSpec((1, r, c_), lambda q: (q, 0, 0)),
        out_shape=jax.ShapeDtypeStruct((chips, r, c_), g.dtype),
        compiler_params=_params("parallel"),
    )(g.reshape(chips, 2, r, c_), got)


def _two_level_gather(src, out, send_sems, recv_sems, local_sem, pos, meanwhile):
    x, y, c = pos
    me, sibling = (x, y, c), (x, y, 1 - c)
    chips = [(1 - x, y), (x, 1 - y), (1 - x, 1 - y)]

    def copy(k, block, to, source=None):
        slot = out.at[_linear(block)]
        return pltpu.make_async_remote_copy(
            src_ref=slot if source is None else source, dst_ref=slot, send_sem=send_sems.at[k],
            recv_sem=recv_sems.at[k], device_id=to, device_id_type=MESH)

    mine = pltpu.make_async_copy(src, out.at[_linear(me)], local_sem)
    mine.start()
    first = [copy(0, me, sibling, src)] + [copy(1 + j, me, (*chip, c), src) for j, chip in enumerate(chips)]
    for cp in first:
        cp.start()
    meanwhile()
    passed = [copy(4 + j, (*chip, c), sibling) for j, chip in enumerate(chips)]
    for j, chip in enumerate(chips):
        copy(1 + j, (*chip, c), me).wait_recv()
        passed[j].start()
    copy(0, sibling, me).wait_recv()
    for j, chip in enumerate(chips):
        copy(4 + j, (*chip, 1 - c), me).wait_recv()
    for cp in first + passed:
        cp.wait_send()
    mine.wait()


def gather_first_weights(w_si, w_so, w_hi, w_ho, hg_norm):
    def body(si_ref, so_ref, hi_ref, ho_ref, hn_ref, o_si, so_b, hi_b, ho_b, hn_b, si_b, send_sems, recv_sems, local_sem):
        si_b[...] = si_ref[...].astype(BF16)

        def cast_the_rest():
            for src, buf in ((so_ref, so_b), (hi_ref, hi_b), (ho_ref, ho_b)):
                buf[...] = src[...].astype(BF16)
            hn_b[...] = jnp.broadcast_to(hn_ref[...], hn_b.shape)

        _two_level_gather(si_b, o_si, send_sems, recv_sems, local_sem, _mesh_pos(), cast_the_rest)

    return _pcall(
        body, name="gather_first_weights",
        in_specs=[_VMEM] * 5, out_specs=[_ANY] + [_VMEM] * 4,
        out_shape=[jax.ShapeDtypeStruct((N_DEV,) + w_si.shape, BF16), jax.ShapeDtypeStruct(w_so.shape, BF16),
                   jax.ShapeDtypeStruct(w_hi.shape, BF16), jax.ShapeDtypeStruct(w_ho.shape, BF16),
                   jax.ShapeDtypeStruct((8, HEAD_DIM), F32)],
        scratch_shapes=[pltpu.VMEM(w_si.shape, BF16), pltpu.SemaphoreType.DMA((N_PEERS,)),
                        pltpu.SemaphoreType.DMA((N_PEERS,)), pltpu.SemaphoreType.DMA],
        compiler_params=pltpu.CompilerParams(vmem_limit_bytes=VMEM_LIMIT_BYTES),
    )(w_si, w_so, w_hi, w_ho, hg_norm)


def _adamw(w, g, m, v):
    m = ADAM_B1 * m + (1.0 - ADAM_B1) * g
    v = ADAM_B2 * v + (1.0 - ADAM_B2) * (g * g)
    m_hat = m / (1.0 - ADAM_B1 ** ADAM_STEP)
    v_hat = v / (1.0 - ADAM_B2 ** ADAM_STEP)
    delta = -ADAM_LR * (m_hat / (jnp.sqrt(v_hat) + ADAM_EPS) + ADAM_WD * w)
    return delta, m, v


def reduce_adamw(parts, w, m, v, name):
    n, r, c = parts.shape
    tr = _row_tile(r, 256)

    def body(p_ref, w_ref, m_ref, v_ref, g_ref, d_ref, m2_ref, v2_ref):
        g = p_ref[0].astype(F32)
        for sender in range(1, n):
            g = g + p_ref[sender].astype(F32)
        g_ref[0] = g
        d_ref[0], m2_ref[0], v2_ref[0] = _adamw(w_ref[0], g, m_ref[0], v_ref[0])

    tile = pl.BlockSpec((1, tr, c), lambda i: (0, i, 0))
    return _pcall(
        body, name=name, grid=(r // tr,),
        in_specs=[pl.BlockSpec((n, tr, c), lambda i: (0, i, 0)), tile, tile, tile],
        out_specs=[tile] * 4, out_shape=[jax.ShapeDtypeStruct((1, r, c), F32)] * 4,
        compiler_params=_params("parallel"),
    )(parts, w, m, v)


PACK_ROWS = 32
ROW_SBN, ROW_HGN, ROW_LB, ROW_QG, ROW_KG, ROW_OG, ROW_LOSS = 0, 8, 16, 24, 25, 26, 27


def small_update(g_sbn, g_hgn, g_lb, g_qg, g_kg, g_og, loss_parts, small):
    def gather(sbn_ref, hgn_ref, lb_ref, qg_ref, kg_ref, og_ref, loss_ref, gath, pack, send_sems, recv_sems, local_sems):
        pos = _mesh_pos()
        pack[...] = jnp.zeros_like(pack)
        pack[ROW_SBN:ROW_SBN + 8, :] = sbn_ref[...]
        pack[ROW_HGN:ROW_HGN + 8, :] = hgn_ref[...]
        pack[ROW_LB:ROW_LB + 8, :] = jnp.sum(lb_ref[...], axis=0)
        pack[ROW_QG:ROW_QG + 1, :] = jnp.sum(qg_ref[...], axis=0, keepdims=True)
        pack[ROW_KG:ROW_KG + 1, :] = jnp.sum(kg_ref[...], axis=0, keepdims=True)
        pack[ROW_OG:ROW_OG + 1, :] = jnp.sum(og_ref[...], axis=0, keepdims=True)
        pack[ROW_LOSS:ROW_LOSS + 1, :] = jnp.sum(loss_ref[...], axis=0)[0:1, :]
        _exchange_start([((lambda p: pack), gath)], send_sems, recv_sems, local_sems, pos)
        _exchange_wait([((lambda p: pack), gath)], send_sems, recv_sems, local_sems, pos)

    packs = _pcall(
        gather, name="small_gather",
        in_specs=[_VMEM] * 7, out_specs=_VMEM, out_shape=jax.ShapeDtypeStruct((N_DEV, PACK_ROWS, HEAD_DIM), F32),
        scratch_shapes=[pltpu.VMEM((PACK_ROWS, HEAD_DIM), F32)] + _exchange_sems(1),
    )(g_sbn, g_hgn, g_lb, g_qg, g_kg, g_og, loss_parts)

    def apply(gath, *refs):
        wmv, outs, tot = refs[:len(small)], refs[len(small):-1], refs[-1]
        me = _linear(_mesh_pos())
        total = gath[0]
        for dev in range(1, N_DEV):
            total = total + gath[dev]
        tot[...] = total
        outs[0][...] = jnp.broadcast_to(tot[ROW_LOSS:ROW_LOSS + 1, :], (8, HEAD_DIM))
        logits_ref, g_logits_ref = wmv[15], outs[1 + 4 * 5]
        for h in range(N_HEADS):
            lanes = slice(h * HEAD_DIM, (h + 1) * HEAD_DIM)
            p1, p0 = _sigmoid_pair(logits_ref[1:2, lanes] - logits_ref[0:1, lanes])
            d_l1 = p0 * p1 * tot[ROW_LB + h:ROW_LB + h + 1, :]
            g_logits_ref[0:1, lanes] = -d_l1
            g_logits_ref[1:2, lanes] = d_l1
        grads = [tot[ROW_SBN:ROW_SBN + 8, :], tot[ROW_QG:ROW_QG + 1, :], tot[ROW_KG:ROW_KG + 1, :],
                 tot[pl.ds(ROW_HGN + me, 1), :], tot[ROW_OG:ROW_OG + 1, :], g_logits_ref[...]]
        for i, g in enumerate(grads):
            w_ref, m_ref, v_ref = wmv[3 * i:3 * i + 3]
            o = outs[1 + 4 * i:5 + 4 * i]
            if i < 5:
                o[0][...] = g
            o[1][...], o[2][...], o[3][...] = _adamw(w_ref[...], g, m_ref[...], v_ref[...])

    out_shape = [jax.ShapeDtypeStruct((8, HEAD_DIM), F32)]
    for i in range(6):
        out_shape += [jax.ShapeDtypeStruct(small[3 * i].shape, F32)] * 4
    return _pcall(
        apply, name="small_update",
        in_specs=[_VMEM] * (1 + len(small)), out_specs=[_VMEM] * 25, out_shape=out_shape,
        scratch_shapes=[pltpu.VMEM((PACK_ROWS, HEAD_DIM), F32)],
    )(packs, *small)


def kernel(x, sb_norm, sb_w_in, sb_q_gain, sb_k_gain, sb_w_out, hg_norm, hg_w_in, hg_o_gain, hg_w_out, hg_lb_logits, loss_target, m_sb_norm, m_sb_w_in, m_sb_q_gain, m_sb_k_gain, m_sb_w_out, m_hg_norm, m_hg_w_in, m_hg_o_gain, m_hg_w_out, m_hg_lb_logits, v_sb_norm, v_sb_w_in, v_sb_q_gain, v_sb_k_gain, v_sb_w_out, v_hg_norm, v_hg_w_in, v_hg_o_gain, v_hg_w_out, v_hg_lb_logits):
    b = x.shape[0]
    wsi, wso_mine, whi_mine, who_mine, hg_norm_mine = gather_first_weights(
        sb_w_in[0], sb_w_out[0], hg_w_in[0], hg_w_out[0], hg_norm)
    r = local_step(x, loss_target, sb_norm, wsi, sb_q_gain, sb_k_gain, hg_o_gain, hg_lb_logits,
                   wso_mine, whi_mine, who_mine, hg_norm_mine)
    big = {}
    for name, p, w, m, v in (("sb_w_in", r["p_wsi"], sb_w_in, m_sb_w_in, v_sb_w_in),
                             ("sb_w_out", r["p_wso"], sb_w_out, m_sb_w_out, v_sb_w_out),
                             ("hg_w_in", r["p_whi"], hg_w_in, m_hg_w_in, v_hg_w_in),
                             ("hg_w_out", r["p_who"], hg_w_out, m_hg_w_out, v_hg_w_out)):
        big[name] = reduce_adamw(p, w, m, v, "adamw_" + name)

    def rows8(a):
        return a.reshape(8, HEAD_DIM)

    small_in = [rows8(sb_norm), rows8(m_sb_norm), rows8(v_sb_norm),
                sb_q_gain, m_sb_q_gain, v_sb_q_gain,
                sb_k_gain, m_sb_k_gain, v_sb_k_gain,
                hg_norm, m_hg_norm, v_hg_norm,
                hg_o_gain, m_hg_o_gain, v_hg_o_gain,
                hg_lb_logits, m_hg_lb_logits, v_hg_lb_logits]
    so = small_update(rows8(r["g_sbn"]), rows8(r["g_hgn"]), r["g_lb"].reshape(b, N_HEADS, HEAD_DIM),
                      r["g_qg"].reshape(b * N_HEADS, HEAD_DIM), r["g_kg"].reshape(b * N_HEADS, HEAD_DIM),
                      r["g_og"].reshape(b * N_HEADS, HEAD_DIM), r["loss_parts"], small_in)
    loss = so[0][0, 0]
    shapes = {"sb_norm": (1, D_MODEL), "sb_q_gain": (1, HEAD_DIM), "sb_k_gain": (1, HEAD_DIM),
              "hg_norm": (1, HEAD_DIM), "hg_o_gain": (1, HEAD_DIM), "hg_lb_logits": (2, D_MODEL)}
    small = {}
    for i, name in enumerate(("sb_norm", "sb_q_gain", "sb_k_gain", "hg_norm", "hg_o_gain", "hg_lb_logits")):
        small[name] = [o.reshape(shapes[name]) for o in so[1 + 4 * i:5 + 4 * i]]
    order = ("sb_norm", "sb_w_in", "sb_q_gain", "sb_k_gain", "sb_w_out",
             "hg_norm", "hg_w_in", "hg_o_gain", "hg_w_out", "hg_lb_logits")
    res = {**big, **small}
    return (loss, r["gx"]) + tuple(res[n][j] for j in range(4) for n in order)
```

```python
import functools
import math

import jax
import jax.numpy as jnp
from jax import lax
from jax.experimental import pallas as pl
from jax.experimental.pallas import tpu as pltpu

F32 = jnp.float32
BF16 = jnp.bfloat16

N_DEV = 8
D_MODEL = 1024
N_HEADS = 8
HEAD_DIM = 128
RMS_EPS = 1e-6
ATTN_BLOCK = 128
HG_CHUNK = 64
HG_SUB = 16
HG_UNROLL_FWD = 16
HG_UNROLL_BWD = 16
EXP_CLAMP = 80.0
SB_HEADS_PER_STEP = 2
SB_QBLOCKS_PER_STEP = 4
SB_GROUP_COLS = SB_HEADS_PER_STEP * 128
SB_TOP_ROWS = 32
SB_LOG_WEIGHT_FLOOR = -104.0
VMEM_LIMIT_BYTES = 48 * 1024 * 1024
W_COLS = 4 * D_MODEL // N_DEV
W_ROWS = D_MODEL // N_DEV

ADAM_LR = 0.001
ADAM_B1 = 0.9
ADAM_B2 = 0.999
ADAM_EPS = 1e-08
ADAM_WD = 0.01
ADAM_STEP = 10

NT = (((1,), (1,)), ((), ()))
TN = (((0,), (0,)), ((), ()))
NN = (((1,), (0,)), ((), ()))


def _pcall(body, *, name, **kw):
    return pl.pallas_call(body, name=name, **kw)


def _params(*sem):
    return pltpu.CompilerParams(dimension_semantics=sem, vmem_limit_bytes=VMEM_LIMIT_BYTES)


def _dot(a, b, dims=NN):
    return lax.dot_general(a, b, dims, preferred_element_type=F32)


def _dot_exact(a, m, dims=NN, left=False):
    hi = a.astype(BF16)
    lo = (a - hi.astype(F32)).astype(BF16)
    if left:
        return _dot(m, hi, dims) + _dot(m, lo, dims)
    return _dot(hi, m, dims) + _dot(lo, m, dims)


def _sigmoid(x):
    return 1.0 / (1.0 + jnp.exp(-x))


def _sigmoid_pair(x):
    e = jnp.exp(-jnp.abs(x))
    big = 1.0 / (1.0 + e)
    small = e * big
    pos = x >= 0
    return jnp.where(pos, big, small), jnp.where(pos, small, big)


def _rms_scale(x):
    return lax.rsqrt(jnp.mean(x * x, axis=-1, keepdims=True) + RMS_EPS)


def _row_tile(t, want):
    return want if t % want == 0 else t


MESH = pl.DeviceIdType.MESH
N_PEERS = N_DEV - 1
_ANY = pl.BlockSpec(memory_space=pl.ANY)
_VMEM = pl.BlockSpec(memory_space=pltpu.VMEM)


def _mesh_pos():
    return lax.axis_index("x"), lax.axis_index("y"), lax.axis_index("c")


def _linear(pos):
    return 4 * pos[0] + 2 * pos[1] + pos[2]


def _peer(pos, k):
    flips = ((k + 1) >> 2 & 1, (k + 1) >> 1 & 1, (k + 1) & 1)
    return tuple(1 - p if f else p for p, f in zip(pos, flips))


def _chip(pos):
    return 2 * pos[0] + pos[1]


ALL_PEERS = tuple(range(N_PEERS))
SAME_CORE_PEERS = (1, 3, 5)


def _exchange_copies(pairs, send_sems, recv_sems, local_sems, pos, landing, peers, slot):
    me = slot(pos)
    local, remote = [], []
    for a, (src_of, dst) in enumerate(pairs):
        local.append(pltpu.make_async_copy(src_of(pos), dst.at[me], local_sems.at[a]))
        for k in peers:
            peer = _peer(pos, k)
            remote.append(pltpu.make_async_remote_copy(
                src_ref=src_of(peer), dst_ref=dst.at[slot(peer) if landing else me],
                send_sem=send_sems.at[a, k], recv_sem=recv_sems.at[a, k], device_id=peer, device_id_type=MESH))
    return local, remote


def _exchange_start(pairs, send_sems, recv_sems, local_sems, pos, peers=ALL_PEERS, slot=_linear):
    local, sent = _exchange_copies(pairs, send_sems, recv_sems, local_sems, pos, False, peers, slot)
    for copy in local + sent:
        copy.start()


def _exchange_wait(pairs, send_sems, recv_sems, local_sems, pos, peers=ALL_PEERS, slot=_linear):
    local, landed = _exchange_copies(pairs, send_sems, recv_sems, local_sems, pos, True, peers, slot)
    for copy in landed:
        copy.wait_recv()
        copy.wait_send()
    for copy in local:
        copy.wait()


def _exchange_sems(n):
    return [pltpu.SemaphoreType.DMA((n, N_PEERS)), pltpu.SemaphoreType.DMA((n, N_PEERS)),
            pltpu.SemaphoreType.DMA((n,))]


class _Rider:
    def __init__(self, arrays, scatter, chips=False):
        self.arrays = list(arrays)
        self.scatter = scatter
        self.peers = SAME_CORE_PEERS if chips else ALL_PEERS
        self.slot = _chip if chips else _linear
        self.out_shapes = [jax.ShapeDtypeStruct(a.shape if scatter else (N_DEV,) + a.shape, a.dtype)
                           for a in self.arrays]

    def pairs(self, in_refs, out_refs):
        if self.scatter:
            return [((lambda pos, r=r: r.at[self.slot(pos)]), o) for r, o in zip(in_refs, out_refs)]
        return [((lambda pos, r=r: r), o) for r, o in zip(in_refs, out_refs)]


def _pcall_riding(body, rider, args, *, name, grid, in_specs, out_specs, out_shape, semantics, scratch_shapes=(),
                  input_output_aliases=None):
    aliases = input_output_aliases or {}
    if rider is None:
        outs = _pcall(body, name=name, grid=grid, in_specs=list(in_specs), out_specs=list(out_specs),
                      out_shape=list(out_shape), scratch_shapes=list(scratch_shapes), input_output_aliases=aliases,
                      compiler_params=_params(*semantics))(*args)
        return list(outs), []
    n_in, n_out, n_scr, n_r = len(in_specs), len(out_specs), len(scratch_shapes), len(rider.arrays)

    def riding(*refs):
        ins, refs = refs[:n_in], refs[n_in:]
        rider_in, refs = refs[:n_r], refs[n_r:]
        outs, refs = refs[:n_out], refs[n_out:]
        rider_out, refs = refs[:n_r], refs[n_r:]
        scratch, sems = refs[:n_scr], refs[n_scr:]
        pairs = rider.pairs(rider_in, rider_out)
        first = functools.reduce(jnp.logical_and, [pl.program_id(a) == 0 for a in range(len(grid))])
        last = functools.reduce(jnp.logical_and, [pl.program_id(a) == g - 1 for a, g in enumerate(grid)])

        @pl.when(first)
        def _():
            _exchange_start(pairs, *sems, _mesh_pos(), rider.peers, rider.slot)

        body(*ins, *outs, *scratch)

        @pl.when(last)
        def _():
            _exchange_wait(pairs, *sems, _mesh_pos(), rider.peers, rider.slot)

    outs = _pcall(riding, name=name, grid=grid, in_specs=list(in_specs) + [_ANY] * n_r,
                  out_specs=list(out_specs) + [_ANY] * n_r, out_shape=list(out_shape) + rider.out_shapes,
                  scratch_shapes=list(scratch_shapes) + _exchange_sems(n_r), input_output_aliases=aliases,
                  compiler_params=_params(*(("arbitrary",) * len(grid))))(*args, *rider.arrays)
    return list(outs[:n_out]), list(outs[n_out:])


def rms_inproj(x2, gain, wg, name, rider=None, qk_gains=None):
    t = x2.shape[0]
    with_qkv = qk_gains is not None
    tm = _row_tile(t, 256 if with_qkv else 512)

    def body(x_ref, g_ref, w_ref, *rest):
        if with_qkv:
            qg_ref, kg_ref, proj_ref, ut_ref, qkv_ref = rest
            head_gain = (qg_ref, kg_ref)
        else:
            proj_ref, ut_ref = rest
        x = x_ref[...]
        u = x * _rms_scale(x) * g_ref[...]
        ut_ref[...] = u.T.astype(BF16)
        u = u.astype(BF16)
        for p in range(N_DEV):
            part, lo = p // 2, (p % 2) * W_COLS
            res = _dot(u, w_ref[p])
            proj_ref[part, :, lo:lo + W_COLS] = res
            if with_qkv and part < 3:
                for h in range(W_COLS // HEAD_DIM):
                    y = res[:, h * HEAD_DIM:(h + 1) * HEAD_DIM]
                    if part < 2:
                        y = y * _rms_scale(y) * head_gain[part][...]
                    qkv_ref[part, :, lo + h * HEAD_DIM:lo + (h + 1) * HEAD_DIM] = y.astype(BF16)

    vec = pl.BlockSpec((1, D_MODEL), lambda i: (0, 0))
    in_specs = [pl.BlockSpec((tm, D_MODEL), lambda i: (i, 0)), vec,
                pl.BlockSpec((N_DEV, D_MODEL, W_COLS), lambda i: (0, 0, 0))]
    out_specs = [pl.BlockSpec((4, tm, D_MODEL), lambda i: (0, i, 0)), pl.BlockSpec((D_MODEL, tm), lambda i: (0, i))]
    out_shape = [jax.ShapeDtypeStruct((4, t, D_MODEL), F32), jax.ShapeDtypeStruct((D_MODEL, t), BF16)]
    args = (x2, gain, wg)
    if with_qkv:
        in_specs += [pl.BlockSpec((1, HEAD_DIM), lambda i: (0, 0))] * 2
        out_specs.append(pl.BlockSpec((3, tm, D_MODEL), lambda i: (0, i, 0)))
        out_shape.append(jax.ShapeDtypeStruct((3, t, D_MODEL), BF16))
        args += tuple(qk_gains)
    return _pcall_riding(body, rider, args, name=name, grid=(t // tm,), in_specs=in_specs, out_specs=out_specs,
                         out_shape=out_shape, semantics=("parallel",))


def outproj_residual(og2, w_out, resid, target, name):
    t = og2.shape[0]
    tm = _row_tile(t, 512)
    with_loss = target is not None

    def body(og_ref, w_ref, r_ref, *rest):
        h = r_ref[...] + _dot(og_ref[...], w_ref[...])
        if with_loss:
            t_ref, dh_ref, loss_ref = rest
            err = h - t_ref[...]
            dh_ref[...] = err * (1.0 / D_MODEL)
            part = 0.5 * jnp.sum(jnp.mean(err * err, axis=-1, keepdims=True))
            loss_ref[...] = jnp.full(loss_ref.shape, part, F32)
        else:
            (h_ref,) = rest
            h_ref[...] = h

    row = pl.BlockSpec((tm, D_MODEL), lambda i: (i, 0))
    in_specs = [row, pl.BlockSpec((D_MODEL, D_MODEL), lambda i: (0, 0)), row]
    args = [og2, w_out, resid]
    if with_loss:
        in_specs.append(row)
        args.append(target)
        out_specs = [row, pl.BlockSpec((1, 8, 128), lambda i: (i, 0, 0))]
        out_shape = [jax.ShapeDtypeStruct((t, D_MODEL), F32),
                     jax.ShapeDtypeStruct((t // tm, 8, 128), F32)]
    else:
        out_specs = row
        out_shape = jax.ShapeDtypeStruct((t, D_MODEL), F32)
    return _pcall(body, name=name, grid=(t // tm,), in_specs=in_specs, out_specs=out_specs,
                  out_shape=out_shape, compiler_params=_params("parallel"))(*args)


def _head_spec(s, part):
    return pl.BlockSpec((1, 1, s, HEAD_DIM), lambda b, h: (part, b, 0, h))


def _seq_spec(s):
    return pl.BlockSpec((1, s, HEAD_DIM), lambda b, h: (b, 0, h))


_GAIN_SPEC = pl.BlockSpec((1, HEAD_DIM), lambda b, h: (0, 0))
_HEAD_ROW_SPEC = pl.BlockSpec((1, 1, 1, HEAD_DIM), lambda b, h: (b, h, 0, 0))


def _sb_group_spec(s, part):
    return pl.BlockSpec((1, 1, s, SB_GROUP_COLS), lambda b, g: (part, b, 0, g))


def _sb_seq_group_spec(s):
    return pl.BlockSpec((1, s, SB_GROUP_COLS), lambda b, g: (b, 0, g))


_SB_GROUP_ROW_SPEC = pl.BlockSpec((1, SB_HEADS_PER_STEP, 1, HEAD_DIM), lambda b, g: (b, g, 0, 0))


def _sb_chains(m):
    return [(h, m * SB_QBLOCKS_PER_STEP + r) for h in range(SB_HEADS_PER_STEP) for r in range(SB_QBLOCKS_PER_STEP)]


def _sb_logits(qi, kj):
    return _dot(qi, kj, NT) * (HEAD_DIM ** -0.5)


def _sb_scores(z, diag, live, tri_lt):
    soft = jnp.log(1.0 + jnp.exp(-jnp.abs(z)))
    valid = jnp.logical_and(live, jnp.logical_or(jnp.logical_not(diag), tri_lt))
    log_skip = jnp.where(valid, -(jnp.maximum(z, 0.0) + soft), 0.0)
    log_beta = jnp.minimum(z, 0.0) - soft
    return log_skip, log_beta, valid


def _sb_keys_left(chains, watch, state):
    done, carries = state[0], state[1]
    worst = None
    for (_, i), c in zip(chains, carries):
        c = jnp.where(done <= i, c[watch], -jnp.inf)
        worst = c if worst is None else jnp.maximum(worst, c)
    return jnp.logical_and(done <= chains[-1][1],
                           jnp.logical_or(done == 0, jnp.max(worst) > SB_LOG_WEIGHT_FLOOR))


def _sb_key_rows(i, done):
    j = i - done
    return pl.ds(pl.multiple_of(jnp.maximum(j, 0) * ATTN_BLOCK, ATTN_BLOCK), ATTN_BLOCK), j >= 0


def _sb_head(ref, h, rows):
    return ref[0, 0, rows, h * HEAD_DIM:(h + 1) * HEAD_DIM]


def _sb_rows(i, offset, count):
    return pl.ds(pl.multiple_of(jnp.maximum(i, 0) * ATTN_BLOCK + offset, math.gcd(ATTN_BLOCK, SB_TOP_ROWS)), count)


def sb_attn_fwd(qkv4, proj4, rider=None):
    _, b, s, _ = qkv4.shape
    blk, top = ATTN_BLOCK, SB_TOP_ROWS
    ngroups = s // blk // SB_QBLOCKS_PER_STEP
    assert ngroups * SB_QBLOCKS_PER_STEP * blk == s

    def body(q_ref, k_ref, v_ref, gate_ref, o_ref, og_ref):
        def write(h, rows, o):
            cols = slice(h * HEAD_DIM, (h + 1) * HEAD_DIM)
            g = _sb_head(gate_ref, h, rows)
            o_ref[0, rows, cols] = o
            og_ref[0, rows, cols] = (o * (g * _sigmoid(g))).astype(BF16)

        row = lax.broadcasted_iota(jnp.int32, (blk, blk), 0)
        col = lax.broadcasted_iota(jnp.int32, (blk, blk), 1)
        tri_lt = col < row
        suffix = (row > col).astype(BF16)

        def step(items):
            where = [_sb_key_rows(i, done) for _, i, done, _, _, _, _ in items]
            zs = [_sb_logits(q, _sb_head(k_ref, h, rows)) for (h, _, _, q, _, _, _), (rows, _) in zip(items, where)]
            scored = [_sb_scores(z, done == 0, live, mask)
                      for z, (_, _, done, _, mask, _, _), (_, live) in zip(zs, items, where)]
            afters = [_dot_exact(log_skip, suffix) for log_skip, _, _ in scored]
            ws = [jnp.where(valid, jnp.exp(log_beta + after + c), 0.0).astype(BF16)
                  for (_, log_beta, valid), after, (_, _, _, _, _, c, _) in zip(scored, afters, items)]
            accs = [acc + _dot(w, _sb_head(v_ref, h, rows))
                    for (h, _, _, _, _, _, acc), (rows, _), w in zip(items, where, ws)]
            cs = [c + jnp.sum(log_skip, axis=1, keepdims=True)
                  for (log_skip, _, _), (_, _, _, _, _, c, _) in zip(scored, items)]
            return cs, accs

        def top_items(chains, done, cs, accs):
            return [(h, i, done, _sb_head(q_ref, h, _sb_rows(i, 0, top)), tri_lt[:top, :], c, acc)
                    for (h, i), c, acc in zip(chains, cs, accs)]

        def finish_tops(chains, state):
            def k_step(state):
                done, cs, accs = state
                cs, accs = step(top_items(chains, done, cs, accs))
                return done + 1, tuple(cs), tuple(accs)

            _, _, accs = lax.while_loop(functools.partial(_sb_keys_left, chains, slice(0, top)), k_step, state)
            for (h, i), acc in zip(chains, accs):
                write(h, _sb_rows(i, 0, top), acc)

        def q_group(m, before):
            chains, chains_before = _sb_chains(m), _sb_chains(m - 1)
            qis = [_sb_head(q_ref, h, _sb_rows(i, 0, blk)) for h, i in chains]
            n = len(chains)

            def items_of(done, cs, accs):
                return [(h, i, done, q, tri_lt, c, acc) for (h, i), q, c, acc in zip(chains, qis, cs, accs)]

            def k_step(state):
                done, cs, accs = state
                cs, accs = step(items_of(done, cs, accs))
                return done + 1, tuple(cs), tuple(accs)

            done_b, cs_b, accs_b = before
            cs0, accs0 = step(items_of(jnp.int32(0), (jnp.zeros((blk, 1), F32),) * n,
                                       (jnp.zeros((blk, HEAD_DIM), F32),) * n)
                              + top_items(chains_before, done_b, cs_b, accs_b))
            done, cs, accs = lax.while_loop(functools.partial(_sb_keys_left, chains, slice(top, blk)), k_step,
                                            (jnp.int32(1), tuple(cs0[:n]), tuple(accs0[:n])))

            @pl.when(m > 0)
            def _():
                finish_tops(chains_before, (done_b + 1, tuple(cs0[n:]), tuple(accs0[n:])))

            for (h, i), acc in zip(chains, accs):
                write(h, _sb_rows(i, top, blk - top), acc[top:, :])
            return done, tuple(c[:top, :] for c in cs), tuple(acc[:top, :] for acc in accs)

        n = SB_HEADS_PER_STEP * SB_QBLOCKS_PER_STEP
        nothing = (jnp.int32(0), (jnp.zeros((top, 1), F32),) * n, (jnp.zeros((top, HEAD_DIM), F32),) * n)
        last = lax.fori_loop(0, ngroups, q_group, nothing)
        finish_tops(_sb_chains(ngroups - 1), last)

    return _pcall_riding(
        body, rider, (qkv4, qkv4, qkv4, proj4),
        name="sb_attn_fwd", grid=(b, N_HEADS // SB_HEADS_PER_STEP),
        in_specs=[_sb_group_spec(s, 0), _sb_group_spec(s, 1), _sb_group_spec(s, 2), _sb_group_spec(s, 3)],
        out_specs=[_sb_seq_group_spec(s)] * 2,
        out_shape=[jax.ShapeDtypeStruct((b, s, D_MODEL), F32), jax.ShapeDtypeStruct((b, s, D_MODEL), BF16)],
        semantics=("parallel", "parallel"))


def _hg_masks():
    c = HG_CHUNK
    row = lax.broadcasted_iota(jnp.int32, (c, c), 0)
    col = lax.broadcasted_iota(jnp.int32, (c, c), 1)
    incl = (col <= row)
    lower = incl.astype(BF16)
    before_sub = (col < (row // HG_SUB) * HG_SUB).astype(BF16)
    upper = (col >= row).astype(BF16)
    return incl, lower, before_sub, upper


def _hg_lower_bound(lbl_ref):
    l0 = lbl_ref[0, 0]
    l1 = lbl_ref[1, 0]
    d = l1 - l0
    return _sigmoid_pair(d)


def _hg_gates(qp, fp, lb, oml):
    sq = _sigmoid(qp)
    sf, sfn = _sigmoid_pair(fp)
    f = lb + oml * sf
    return dict(qp=qp, sq=sq, q=qp * sq, sf=sf, sfn=sfn, f=f, k=oml * sfn, logf=jnp.log(f))


def _hg_intra(qds, ks, gcs, grs, incl):
    subs = range(HG_CHUNK // HG_SUB)
    qdbs = [qd.astype(BF16) for qd in qds]
    ess = [[jnp.exp(jnp.minimum(gr[sub * HG_SUB:sub * HG_SUB + 1, :] - gc, EXP_CLAMP)) for sub in subs]
           for gc, gr in zip(gcs, grs)]
    ksbs = [[(k * e).astype(BF16) for e in es] for k, es in zip(ks, ess)]
    rows = [[_dot(qdb[sub * HG_SUB:(sub + 1) * HG_SUB, :], ksb[sub], NT) for sub in subs]
            for qdb, ksb in zip(qdbs, ksbs)]
    a_s = [jnp.where(incl, jnp.concatenate(r, axis=0), 0.0) for r in rows]
    return a_s, qdbs, ksbs, ess


def _hg_group_rows(outer, unroll):
    ns = [outer * unroll + u for u in range(unroll)]
    return ns, [pl.ds(pl.multiple_of(n * HG_CHUNK, HG_CHUNK), HG_CHUNK) for n in ns]


def _state_spec(nchunk):
    return pl.BlockSpec((1, 1, nchunk, HEAD_DIM, HEAD_DIM), lambda b, h: (b, h, 0, 0, 0))


def hgrn2_fwd(proj4, lbl4, o_gain, rider=None):
    _, b, s, _ = proj4.shape
    nchunk = s // HG_CHUNK
    c = HG_CHUNK
    unroll = math.gcd(nchunk, HG_UNROLL_FWD)

    def body(q_ref, f_ref, i_ref, gate_ref, lbl_ref, og_ref, o_ref, oraw_ref, st_ref, ogate_ref):
        incl, lower, before_sub, _ = _hg_masks()
        lb, oml = _hg_lower_bound(lbl_ref)

        def group(outer, st):
            ns, rows = _hg_group_rows(outer, unroll)
            vs = [_hg_gates(q_ref[0, 0, r, :], f_ref[0, 0, r, :], lb, oml) for r in rows]
            inps = [i_ref[0, 0, r, :].astype(BF16) for r in rows]
            gcs = [_dot_exact(v["logf"], lower, left=True) for v in vs]
            grs = [_dot_exact(v["logf"], before_sub, left=True) for v in vs]
            a_s, _, _, _ = _hg_intra([v["q"] * jnp.exp(gc - gr) for v, gc, gr in zip(vs, gcs, grs)],
                                     [v["k"] for v in vs], gcs, grs, incl)
            gls = [gc[c - 1:c, :] for gc in gcs]
            adds = [_dot(inp, (v["k"] * jnp.exp(gl - gc)).astype(BF16), TN)
                    for inp, v, gl, gc in zip(inps, vs, gls, gcs)]
            o_intra = [_dot(a.astype(BF16), inp) for a, inp in zip(a_s, inps)]
            sts = []
            for gl, add in zip(gls, adds):
                sts.append(st)
                st = st * jnp.exp(gl) + add
            outs = [oi + _dot((v["q"] * jnp.exp(gc)).astype(BF16), s0.astype(BF16), NT)
                    for oi, v, gc, s0 in zip(o_intra, vs, gcs, sts)]
            for n, r, s0, o in zip(ns, rows, sts, outs):
                st_ref[0, 0, n] = s0
                oraw_ref[0, r, :] = o
                normed = o * _rms_scale(o) * og_ref[...]
                g = gate_ref[0, 0, r, :]
                o_ref[0, r, :] = normed
                ogate_ref[0, r, :] = (normed * (g * _sigmoid(g))).astype(BF16)
            return st

        lax.fori_loop(0, nchunk // unroll, group, jnp.zeros((HEAD_DIM, HEAD_DIM), F32))

    seq = jax.ShapeDtypeStruct((b, s, D_MODEL), F32)
    return _pcall_riding(
        body, rider, (proj4, proj4, proj4, proj4, lbl4, o_gain), name="hgrn2_fwd", grid=(b, N_HEADS),
        in_specs=[_head_spec(s, 0), _head_spec(s, 1), _head_spec(s, 2), _head_spec(s, 3),
                  pl.BlockSpec((2, 1, 1, HEAD_DIM), lambda b, h: (0, h, 0, 0)), _GAIN_SPEC],
        out_specs=[_seq_spec(s), _seq_spec(s), _state_spec(nchunk), _seq_spec(s)],
        out_shape=[seq, seq, jax.ShapeDtypeStruct((b, N_HEADS, nchunk, HEAD_DIM, HEAD_DIM), F32),
                   jax.ShapeDtypeStruct((b, s, D_MODEL), BF16)],
        semantics=("parallel", "parallel"))


def outproj_bwd(dh, w_out, o2, proj, name, do_dtype=F32):
    t = dh.shape[0]
    tm = _row_tile(t, 512)

    def body(dh_ref, w_ref, o_ref, gate_ref, do_ref, dproj_ref, dw_ref):
        dhb = dh_ref[...].astype(BF16)
        dog = _dot(dhb, w_ref[...], NT)
        g = gate_ref[0]
        sg = _sigmoid(g)
        silu = g * sg
        o = o_ref[...]
        do_ref[...] = (dog * silu).astype(do_dtype)
        dproj_ref[0] = dog * o * (sg * (1.0 + g * (1.0 - sg)))
        part = _dot((o * silu).astype(BF16), dhb, TN)

        @pl.when(pl.program_id(0) == 0)
        def _():
            dw_ref[...] = part

        @pl.when(pl.program_id(0) > 0)
        def _():
            dw_ref[...] += part

    row = pl.BlockSpec((tm, D_MODEL), lambda i: (i, 0))
    full = pl.BlockSpec((D_MODEL, D_MODEL), lambda i: (0, 0))
    return _pcall(
        body, name=name, grid=(t // tm,),
        in_specs=[row, full, row, pl.BlockSpec((1, tm, D_MODEL), lambda i: (3, i, 0))],
        out_specs=[row, pl.BlockSpec((1, tm, D_MODEL), lambda i: (3, i, 0)), full],
        out_shape=[jax.ShapeDtypeStruct((t, D_MODEL), do_dtype),
                   jax.ShapeDtypeStruct((4, t, D_MODEL), F32),
                   jax.ShapeDtypeStruct((D_MODEL, D_MODEL), F32)],
        compiler_params=_params("arbitrary"),
    )(dh, w_out, o2, proj)


def inproj_bwd_dx(dproj, wg, x2, gain, dres, name, rider=None):
    t = x2.shape[0]
    tm = _row_tile(t, 512)

    def body(d_ref, w_ref, x_ref, g_ref, r_ref, dx_ref, dg_ref):
        du = jnp.zeros((tm, D_MODEL), F32)
        for p in range(N_DEV):
            cols = slice((p % 2) * W_COLS, (p % 2 + 1) * W_COLS)
            du = du + _dot(d_ref[p // 2, :, cols].astype(BF16), w_ref[p], NT)
        x = x_ref[...]
        r = _rms_scale(x)
        xh = x * r
        a = du * g_ref[...]
        dx_ref[...] = r_ref[...] + r * (a - xh * jnp.mean(a * xh, axis=-1, keepdims=True))
        part = jnp.sum(du * xh, axis=0, keepdims=True)

        @pl.when(pl.program_id(0) == 0)
        def _():
            dg_ref[...] = part

        @pl.when(pl.program_id(0) > 0)
        def _():
            dg_ref[...] += part

    row = pl.BlockSpec((tm, D_MODEL), lambda i: (i, 0))
    vec = pl.BlockSpec((1, D_MODEL), lambda i: (0, 0))
    return _pcall_riding(
        body, rider, (dproj, wg, x2, gain, dres), name=name, grid=(t // tm,),
        in_specs=[pl.BlockSpec((4, tm, D_MODEL), lambda i: (0, i, 0)),
                  pl.BlockSpec((N_DEV, D_MODEL, W_COLS), lambda i: (0, 0, 0)),
                  row, vec, row],
        out_specs=[row, vec],
        out_shape=[jax.ShapeDtypeStruct((t, D_MODEL), F32), jax.ShapeDtypeStruct((1, D_MODEL), F32)],
        semantics=("arbitrary",))


def inproj_bwd_dw(ut, dproj, name, out_dtype):
    t = ut.shape[1]

    def body(ut_ref, d_ref, dw_ref):
        dw_ref[0] = _dot(ut_ref[...], d_ref[0].astype(BF16)).astype(dw_ref.dtype)

    return _pcall(
        body, name=name, grid=(N_DEV,),
        in_specs=[pl.BlockSpec((D_MODEL, t), lambda j: (0, 0)),
                  pl.BlockSpec((1, t, W_COLS), lambda j: (j // 2, 0, j % 2))],
        out_specs=pl.BlockSpec((1, D_MODEL, W_COLS), lambda j: (j, 0, 0)),
        out_shape=jax.ShapeDtypeStruct((N_DEV, D_MODEL, W_COLS), out_dtype),
        compiler_params=_params("parallel"),
    )(ut, dproj)


def _rms_bwd(x, gain, dy):
    r = _rms_scale(x)
    xh = x * r
    a = dy * gain
    return r * (a - xh * jnp.mean(a * xh, axis=-1, keepdims=True)), dy * xh


def sb_attn_bwd(qkv4, proj4, do3, o3, q_gain, k_gain, dproj4, rider=None):
    _, b, s, _ = proj4.shape
    blk, top = ATTN_BLOCK, SB_TOP_ROWS
    nq = s // blk
    ngroups = nq // SB_QBLOCKS_PER_STEP
    assert ngroups * SB_QBLOCKS_PER_STEP * blk == s
    scale = HEAD_DIM ** -0.5

    def body(qn_ref, kn_ref, v_ref, q_ref, k_ref, do_ref, o_ref, qg_ref, kg_ref, _alias, d_ref, dqg_ref, dkg_ref, dob):
        for h in range(SB_HEADS_PER_STEP):
            dob[h] = do_ref[0, :, h * HEAD_DIM:(h + 1) * HEAD_DIM].astype(BF16)
        d_ref[...] = jnp.zeros_like(d_ref)
        row = lax.broadcasted_iota(jnp.int32, (blk, blk), 0)
        col = lax.broadcasted_iota(jnp.int32, (blk, blk), 1)
        tri_lt = col < row
        suffix = (row > col).astype(BF16)
        suffix_incl = (row >= col).astype(BF16)

        def step(items):
            heads = [it[0] for it in items]
            where = [_sb_key_rows(it[1], it[2]) for it in items]
            kjs = [_sb_head(kn_ref, h, rows) for h, (rows, _) in zip(heads, where)]
            zs = [_sb_logits(it[3], kj) for it, kj in zip(items, kjs)]
            dws = [_dot(it[4], _sb_head(v_ref, h, rows), NT) for it, h, (rows, _) in zip(items, heads, where)]
            scored = [_sb_scores(z, it[2] == 0, live, it[6]) for z, it, (_, live) in zip(zs, items, where)]
            afters = [_dot_exact(log_skip, suffix) for log_skip, _, _ in scored]
            wbs = [jnp.where(valid, jnp.exp(log_beta + after + it[7]), 0.0).astype(BF16)
                   for (_, log_beta, valid), after, it in zip(scored, afters, items)]
            gs = [dw * wb.astype(F32) for dw, wb in zip(dws, wbs)]
            befores = [it[5] - (_dot_exact(g, suffix_incl) + it[8]) for g, it in zip(gs, items)]
            dzbs = [jnp.where(valid, g - jnp.exp(log_beta) * (g + before), 0.0).astype(BF16)
                    for (_, log_beta, valid), g, before in zip(scored, gs, befores)]
            dqs = [it[9] + _dot(dzb, kj) for it, dzb, kj in zip(items, dzbs, kjs)]
            for it, h, (rows, _), wb, dzb in zip(items, heads, where, wbs, dzbs):
                cols = slice(h * HEAD_DIM, (h + 1) * HEAD_DIM)
                d_ref[2, 0, rows, cols] += _dot(wb, it[4], TN)
                d_ref[1, 0, rows, cols] += _dot(dzb, it[3], TN)
            cs = [it[7] + jnp.sum(log_skip, axis=1, keepdims=True) for (log_skip, _, _), it in zip(scored, items)]
            cgs = [it[8] + jnp.sum(g, axis=1, keepdims=True) for g, it in zip(gs, items)]
            return cs, cgs, dqs

        def top_items(chains, deltas, done, cs, cgs, dqs):
            return [(h, i, done, _sb_head(qn_ref, h, _sb_rows(i, 0, top)), dob[h, _sb_rows(i, 0, top), :], delta,
                     tri_lt[:top, :], c, cg, dq)
                    for (h, i), delta, c, cg, dq in zip(chains, deltas, cs, cgs, dqs)]

        def finish_tops(chains, deltas, state):
            def k_step(state):
                done, cs, cgs, dqs = state
                cs, cgs, dqs = step(top_items(chains, deltas, done, cs, cgs, dqs))
                return done + 1, tuple(cs), tuple(cgs), tuple(dqs)

            _, _, _, dqs = lax.while_loop(functools.partial(_sb_keys_left, chains, slice(0, top)), k_step, state)
            for (h, i), dq in zip(chains, dqs):
                d_ref[0, 0, _sb_rows(i, 0, top), h * HEAD_DIM:(h + 1) * HEAD_DIM] = dq * scale

        def q_group(m, before):
            chains, chains_before = _sb_chains(m), _sb_chains(m - 1)
            deltas_b, before = before[0], before[1:]
            qis, dois, deltas = [], [], []
            for h, i in chains:
                rows_i = _sb_rows(i, 0, blk)
                qis.append(_sb_head(qn_ref, h, rows_i))
                dois.append(dob[h, rows_i, :])
                deltas.append(jnp.sum(dois[-1].astype(F32) * o_ref[0, rows_i, h * HEAD_DIM:(h + 1) * HEAD_DIM],
                                      axis=1, keepdims=True))
            n = len(chains)

            def items_of(done, cs, cgs, dqs):
                return [(h, i, done, q, do, delta, tri_lt, c, cg, dq)
                        for (h, i), q, do, delta, c, cg, dq in zip(chains, qis, dois, deltas, cs, cgs, dqs)]

            def k_step(state):
                done, cs, cgs, dqs = state
                cs, cgs, dqs = step(items_of(done, cs, cgs, dqs))
                return done + 1, tuple(cs), tuple(cgs), tuple(dqs)

            done_b, cs_b, cgs_b, dqs_b = before
            zero = (jnp.zeros((blk, 1), F32),) * n
            new = step(items_of(jnp.int32(0), zero, zero, (jnp.zeros((blk, HEAD_DIM), F32),) * n)
                       + top_items(chains_before, deltas_b, done_b, cs_b, cgs_b, dqs_b))
            done, cs, cgs, dqs = lax.while_loop(functools.partial(_sb_keys_left, chains, slice(top, blk)), k_step,
                                                (jnp.int32(1),) + tuple(tuple(x[:n]) for x in new))

            @pl.when(m > 0)
            def _():
                finish_tops(chains_before, deltas_b, (done_b + 1,) + tuple(tuple(x[n:]) for x in new))

            for (h, i), dq in zip(chains, dqs):
                d_ref[0, 0, _sb_rows(i, top, blk - top), h * HEAD_DIM:(h + 1) * HEAD_DIM] = dq[top:, :] * scale
            first = lambda xs: tuple(x[:top, :] for x in xs)
            return first(deltas), done, first(cs), first(cgs), first(dqs)

        n = SB_HEADS_PER_STEP * SB_QBLOCKS_PER_STEP
        zero = (jnp.zeros((top, 1), F32),) * n
        nothing = (zero, jnp.int32(0), zero, zero, (jnp.zeros((top, HEAD_DIM), F32),) * n)
        last = lax.fori_loop(0, ngroups, q_group, nothing)
        finish_tops(_sb_chains(ngroups - 1), last[0], last[1:])

        def norm_block(i, carry):
            rows = pl.ds(pl.multiple_of(i * blk, blk), blk)
            out = []
            for h in range(SB_HEADS_PER_STEP):
                cols = slice(h * HEAD_DIM, (h + 1) * HEAD_DIM)
                for part, src_ref, gain_ref in ((0, q_ref, qg_ref), (1, k_ref, kg_ref)):
                    dy = d_ref[part, 0, rows, cols] * (scale if part == 1 else 1.0)
                    dx, pg = _rms_bwd(src_ref[0, 0, rows, cols], gain_ref[...], dy)
                    d_ref[part, 0, rows, cols] = dx
                    out.append(carry[len(out)] + jnp.sum(pg, axis=0, keepdims=True))
            return tuple(out)

        sums = lax.fori_loop(0, nq, norm_block, (jnp.zeros((1, HEAD_DIM), F32),) * (2 * SB_HEADS_PER_STEP))
        for h in range(SB_HEADS_PER_STEP):
            dqg_ref[0, h] = sums[2 * h]
            dkg_ref[0, h] = sums[2 * h + 1]

    head_row = jax.ShapeDtypeStruct((b, N_HEADS, 1, HEAD_DIM), F32)
    return _pcall_riding(
        body, rider, (qkv4, qkv4, qkv4, proj4, proj4, do3, o3, q_gain, k_gain, dproj4),
        name="sb_attn_bwd", grid=(b, N_HEADS // SB_HEADS_PER_STEP),
        in_specs=[_sb_group_spec(s, 0), _sb_group_spec(s, 1), _sb_group_spec(s, 2),
                  _sb_group_spec(s, 0), _sb_group_spec(s, 1),
                  _sb_seq_group_spec(s), _sb_seq_group_spec(s), _GAIN_SPEC, _GAIN_SPEC,
                  pl.BlockSpec(memory_space=pl.ANY)],
        out_specs=[pl.BlockSpec((3, 1, s, SB_GROUP_COLS), lambda b, g: (0, b, 0, g)),
                   _SB_GROUP_ROW_SPEC, _SB_GROUP_ROW_SPEC],
        out_shape=[jax.ShapeDtypeStruct(dproj4.shape, F32), head_row, head_row],
        scratch_shapes=[pltpu.VMEM((SB_HEADS_PER_STEP, s, HEAD_DIM), BF16)],
        input_output_aliases={9: 0}, semantics=("parallel", "parallel"))


def hgrn2_bwd(proj4, don3, oraw3, states, lbl4, o_gain, dproj4, rider=None):
    _, b, s, _ = proj4.shape
    nchunk = s // HG_CHUNK
    c = HG_CHUNK
    subs = range(HG_CHUNK // HG_SUB)
    unroll = math.gcd(nchunk, HG_UNROLL_BWD)
    ngroup = nchunk // unroll

    def body(q_ref, f_ref, i_ref, don_ref, oraw_ref, st_ref, lbl_ref, og_ref, _alias, d_ref, dog_ref, dlb_ref):
        incl, lower, before_sub, upper = _hg_masks()
        lb, oml = _hg_lower_bound(lbl_ref)
        last_row = lax.broadcasted_iota(jnp.int32, (c, HEAD_DIM), 0) == c - 1

        def group(m, carry):
            dst, dog_acc, dlb_acc = carry
            ns, rows = _hg_group_rows(ngroup - 1 - m, unroll)
            ns, rows = ns[::-1], rows[::-1]
            vs = [_hg_gates(q_ref[0, 0, r, :], f_ref[0, 0, r, :], lb, oml) for r in rows]
            inps = [i_ref[0, 0, r, :].astype(BF16) for r in rows]
            sts = [st_ref[0, 0, n] for n in ns]
            gcs = [_dot_exact(v["logf"], lower, left=True) for v in vs]
            grs = [_dot_exact(v["logf"], before_sub, left=True) for v in vs]
            e_qs = [jnp.exp(gc - gr) for gc, gr in zip(gcs, grs)]
            a_s, qdbs, ksbs, ess = _hg_intra([v["q"] * e for v, e in zip(vs, e_qs)], [v["k"] for v in vs],
                                             gcs, grs, incl)
            e_gcs = [jnp.exp(gc) for gc in gcs]
            gls = [gc[c - 1:c, :] for gc in gcs]
            e_gls = [jnp.exp(gl) for gl in gls]
            e_ks = [jnp.exp(gl - gc) for gl, gc in zip(gls, gcs)]
            normed = [_rms_bwd(oraw_ref[0, r, :], og_ref[...], don_ref[0, r, :]) for r in rows]
            dobs = [do.astype(BF16) for do, _ in normed]
            dabs = [jnp.where(incl, _dot(dob, inp, NT), 0.0).astype(BF16) for dob, inp in zip(dobs, inps)]
            adds = [_dot(dob, (v["q"] * e).astype(BF16), TN) for dob, v, e in zip(dobs, vs, e_gcs)]
            dq_inters = [_dot(dob, st.astype(BF16)) * e for dob, st, e in zip(dobs, sts, e_gcs)]
            dqds = [jnp.concatenate([_dot(dab[sub * HG_SUB:(sub + 1) * HG_SUB, :], ksb[sub]) for sub in subs], axis=0)
                    for dab, ksb in zip(dabs, ksbs)]
            dkss = [[_dot(dab[sub * HG_SUB:(sub + 1) * HG_SUB, :], qdb[sub * HG_SUB:(sub + 1) * HG_SUB, :], TN)
                     for sub in subs] for dab, qdb in zip(dabs, qdbs)]
            dsts = []
            for e_gl, add in zip(e_gls, adds):
                dsts.append(dst)
                dst = dst * e_gl + add
            dstbs = [d.astype(BF16) for d in dsts]
            dis = [_dot(a.astype(BF16), dob, TN) + _dot((v["k"] * e_k).astype(BF16), dstb, NT)
                   for a, dob, v, e_k, dstb in zip(a_s, dobs, vs, e_ks, dstbs)]
            dk_inters = [_dot(inp, dstb) * e_k for inp, dstb, e_k in zip(inps, dstbs, e_ks)]
            dks, dgcs = [], []
            for u in range(len(rows)):
                q, k = vs[u]["q"], vs[u]["k"]
                dk, dgc_k = dk_inters[u], jnp.zeros((c, HEAD_DIM), F32)
                for sub in subs:
                    dk = dk + dkss[u][sub] * ess[u][sub]
                    dgc_k = dgc_k + dkss[u][sub] * ksbs[u][sub].astype(F32)
                at_last = (jnp.sum(k * dk_inters[u], axis=0, keepdims=True)
                           + e_gls[u] * jnp.sum(sts[u] * dsts[u], axis=0, keepdims=True))
                dks.append(dk)
                dgcs.append((qdbs[u].astype(F32) * dqds[u] - dgc_k) + (q * dq_inters[u] - k * dk_inters[u])
                            + jnp.where(last_row, at_last, 0.0))
            dlf_fs = [_dot_exact(dgc, upper, left=True) / v["f"] for dgc, v in zip(dgcs, vs)]
            for u, r in enumerate(rows):
                v = vs[u]
                dq = dqds[u] * e_qs[u] + dq_inters[u]
                d_ref[0, 0, r, :] = dq * (v["sq"] * (1.0 + v["qp"] * (1.0 - v["sq"])))
                d_ref[1, 0, r, :] = (dlf_fs[u] - dks[u]) * (oml * v["sf"] * v["sfn"])
                d_ref[2, 0, r, :] = dis[u]
                dlb_acc = dlb_acc + jnp.sum((dlf_fs[u] - dks[u]) * v["sfn"], axis=0, keepdims=True)
                dog_acc = dog_acc + jnp.sum(normed[u][1], axis=0, keepdims=True)
            return dst, dog_acc, dlb_acc

        zero = jnp.zeros((1, HEAD_DIM), F32)
        _, dog, dlb = lax.fori_loop(0, ngroup, group, (jnp.zeros((HEAD_DIM, HEAD_DIM), F32), zero, zero))
        dog_ref[0, 0] = dog
        dlb_ref[0, 0] = dlb

    head_row = jax.ShapeDtypeStruct((b, N_HEADS, 1, HEAD_DIM), F32)
    return _pcall_riding(
        body, rider, (proj4, proj4, proj4, don3, oraw3, states, lbl4, o_gain, dproj4),
        name="hgrn2_bwd", grid=(b, N_HEADS),
        in_specs=[_head_spec(s, 0), _head_spec(s, 1), _head_spec(s, 2), _seq_spec(s), _seq_spec(s),
                  _state_spec(nchunk), pl.BlockSpec((2, 1, 1, HEAD_DIM), lambda b, h: (0, h, 0, 0)), _GAIN_SPEC,
                  pl.BlockSpec(memory_space=pl.ANY)],
        out_specs=[pl.BlockSpec((3, 1, s, HEAD_DIM), lambda b, h: (0, b, 0, h)), _HEAD_ROW_SPEC, _HEAD_ROW_SPEC],
        out_shape=[jax.ShapeDtypeStruct(dproj4.shape, F32), head_row, head_row],
        input_output_aliases={8: 0}, semantics=("parallel", "parallel"))


def local_step(x, target, sb_norm, wsi, sb_q_gain, sb_k_gain, hg_o_gain, hg_lb_logits, wso_mine, whi_mine, who_mine,
               hg_norm_mine):
    b, s, _ = x.shape
    t = b * s
    x2 = x.reshape(t, D_MODEL)
    tg2 = target.reshape(t, D_MODEL)
    lbl4 = hg_lb_logits.reshape(2, N_HEADS, 1, HEAD_DIM)
    four = (4, b, s, D_MODEL)
    three = (b, s, D_MODEL)
    rows8 = (N_DEV, W_ROWS, D_MODEL)

    (proj0, u0, qkv0), (wso, hgn) = rms_inproj(x2, sb_norm, wsi, "sb_inproj",
                                               _Rider([wso_mine, hg_norm_mine], scatter=False),
                                               qk_gains=(sb_q_gain, sb_k_gain))
    qkv0 = qkv0.reshape(3, b, s, D_MODEL)
    wso = wso.reshape(D_MODEL, D_MODEL)
    hg_norm_full = hgn[:, 0, :].reshape(1, D_MODEL)
    (o0, og0), (whi,) = sb_attn_fwd(qkv0, proj0.reshape(four), _Rider([whi_mine], scatter=False))
    o0 = o0.reshape(t, D_MODEL)
    h1 = outproj_residual(og0.reshape(t, D_MODEL), wso, x2, None, "sb_outproj")
    (proj1, u1), _ = rms_inproj(h1, hg_norm_full, whi, "hg_inproj")
    (o1, o1_raw, states, og1), (who,) = hgrn2_fwd(proj1.reshape(four), lbl4, hg_o_gain,
                                                  _Rider([who_mine], scatter=False))
    who = who.reshape(D_MODEL, D_MODEL)
    o1 = o1.reshape(t, D_MODEL)
    dh2, loss_parts = outproj_residual(og1.reshape(t, D_MODEL), who, h1, tg2, "hg_outproj_loss")

    do1, dproj1, g_who = outproj_bwd(dh2, who, o1, proj1, "hg_outproj_bwd")
    (dproj1, g_og, g_lb), (p_who,) = hgrn2_bwd(proj1.reshape(four), do1.reshape(three), o1_raw, states, lbl4,
                                               hg_o_gain, dproj1.reshape(four),
                                               _Rider([g_who.reshape(rows8)], scatter=True))
    dproj1 = dproj1.reshape(4, t, D_MODEL)
    (dh1, g_hgn), _ = inproj_bwd_dx(dproj1, whi, h1, hg_norm_full, dh2, "hg_inproj_bwd_dx")
    g_whi = inproj_bwd_dw(u1, dproj1, "hg_inproj_bwd_dw", out_dtype=BF16)

    do0, dproj0, g_wso = outproj_bwd(dh1, wso, o0, proj0, "sb_outproj_bwd", do_dtype=BF16)
    (dproj0, g_qg, g_kg), (p_whi, p_wso) = sb_attn_bwd(qkv0, proj0.reshape(four), do0.reshape(three), o0.reshape(three),
                                                       sb_q_gain, sb_k_gain, dproj0.reshape(four),
                                                       _Rider([g_whi, g_wso.reshape(rows8)], scatter=True))
    dproj0 = dproj0.reshape(4, t, D_MODEL)
    g_wsi = inproj_bwd_dw(u0, dproj0, "sb_inproj_bwd_dw", out_dtype=BF16)
    (gx, g_sbn), (p_wsi,) = inproj_bwd_dx(dproj0, wsi, x2, sb_norm, dh1, "sb_inproj_bwd_dx",
                                          _Rider([sum_within_chip(g_wsi)], scatter=True, chips=True))
    return dict(loss_parts=loss_parts, gx=gx.reshape(three), p_wsi=p_wsi, p_wso=p_wso, p_whi=p_whi, p_who=p_who,
                g_sbn=g_sbn, g_hgn=g_hgn, g_qg=g_qg, g_kg=g_kg, g_og=g_og, g_lb=g_lb)


def sum_within_chip(g):
    _, r, c_ = g.shape
    chips = N_DEV // 2

    def swap(g_ref, got_ref, send_sems, recv_sems):
        x, y, c = _mesh_pos()
        copies = [pltpu.make_async_remote_copy(
            src_ref=g_ref.at[2 * q + 1 - c], dst_ref=got_ref.at[q], send_sem=send_sems.at[q], recv_sem=recv_sems.at[q],
            device_id=(x, y, 1 - c), device_id_type=MESH) for q in range(chips)]
        for cp in copies:
            cp.start()
        for cp in copies:
            cp.wait_recv()
            cp.wait_send()

    got = _pcall(swap, name="swap_with_sibling", in_specs=[_ANY], out_specs=_ANY,
                 out_shape=jax.ShapeDtypeStruct((chips, r, c_), g.dtype),
                 scratch_shapes=[pltpu.SemaphoreType.DMA((chips,)), pltpu.SemaphoreType.DMA((chips,))])(g)

    def add(g_ref, got_ref, out_ref):
        mine = g_ref[0, lax.axis_index("c")]
        out_ref[0] = (mine.astype(F32) + got_ref[0].astype(F32)).astype(out_ref.dtype)

    return _pcall(
        add, name="add_sibling_partials", grid=(chips,),
        in_specs=[pl.BlockSpec((1, 2, r, c_), lambda q: (q, 0, 0, 0)), pl.BlockSpec((1, r, c_), lambda q: (q, 0, 0))],
        out_specs=pl.BlockSpec((1, r, c_), lambda q: (q, 0, 0)),
        out_shape=jax.ShapeDtypeStruct((chips, r, c_), g.dtype),
        compiler_params=_params("parallel"),
    )(g.reshape(chips, 2, r, c_), got)


def _two_level_gather(src, out, send_sems, recv_sems, local_sem, pos, meanwhile):
    x, y, c = pos
    me, sibling = (x, y, c), (x, y, 1 - c)
    chips = [(1 - x, y), (x, 1 - y), (1 - x, 1 - y)]

    def copy(k, block, to, source=None):
        slot = out.at[_linear(block)]
        return pltpu.make_async_remote_copy(
            src_ref=slot if source is None else source, dst_ref=slot, send_sem=send_sems.at[k],
            recv_sem=recv_sems.at[k], device_id=to, device_id_type=MESH)

    mine = pltpu.make_async_copy(src, out.at[_linear(me)], local_sem)
    mine.start()
    first = [copy(0, me, sibling, src)] + [copy(1 + j, me, (*chip, c), src) for j, chip in enumerate(chips)]
    for cp in first:
        cp.start()
    meanwhile()
    passed = [copy(4 + j, (*chip, c), sibling) for j, chip in enumerate(chips)]
    for j, chip in enumerate(chips):
        copy(1 + j, (*chip, c), me).wait_recv()
        passed[j].start()
    copy(0, sibling, me).wait_recv()
    for j, chip in enumerate(chips):
        copy(4 + j, (*chip, 1 - c), me).wait_recv()
    for cp in first + passed:
        cp.wait_send()
    mine.wait()


def gather_first_weights(w_si, w_so, w_hi, w_ho, hg_norm):
    def body(si_ref, so_ref, hi_ref, ho_ref, hn_ref, o_si, so_b, hi_b, ho_b, hn_b, si_b, send_sems, recv_sems, local_sem):
        si_b[...] = si_ref[...].astype(BF16)

        def cast_the_rest():
            for src, buf in ((so_ref, so_b), (hi_ref, hi_b), (ho_ref, ho_b)):
                buf[...] = src[...].astype(BF16)
            hn_b[...] = jnp.broadcast_to(hn_ref[...], hn_b.shape)

        _two_level_gather(si_b, o_si, send_sems, recv_sems, local_sem, _mesh_pos(), cast_the_rest)

    return _pcall(
        body, name="gather_first_weights",
        in_specs=[_VMEM] * 5, out_specs=[_ANY] + [_VMEM] * 4,
        out_shape=[jax.ShapeDtypeStruct((N_DEV,) + w_si.shape, BF16), jax.ShapeDtypeStruct(w_so.shape, BF16),
                   jax.ShapeDtypeStruct(w_hi.shape, BF16), jax.ShapeDtypeStruct(w_ho.shape, BF16),
                   jax.ShapeDtypeStruct((8, HEAD_DIM), F32)],
        scratch_shapes=[pltpu.VMEM(w_si.shape, BF16), pltpu.SemaphoreType.DMA((N_PEERS,)),
                        pltpu.SemaphoreType.DMA((N_PEERS,)), pltpu.SemaphoreType.DMA],
        compiler_params=pltpu.CompilerParams(vmem_limit_bytes=VMEM_LIMIT_BYTES),
    )(w_si, w_so, w_hi, w_ho, hg_norm)


def _adamw(w, g, m, v):
    m = ADAM_B1 * m + (1.0 - ADAM_B1) * g
    v = ADAM_B2 * v + (1.0 - ADAM_B2) * (g * g)
    m_hat = m / (1.0 - ADAM_B1 ** ADAM_STEP)
    v_hat = v / (1.0 - ADAM_B2 ** ADAM_STEP)
    delta = -ADAM_LR * (m_hat / (jnp.sqrt(v_hat) + ADAM_EPS) + ADAM_WD * w)
    return delta, m, v


def reduce_adamw(parts, w, m, v, name):
    n, r, c = parts.shape
    tr = _row_tile(r, 256)

    def body(p_ref, w_ref, m_ref, v_ref, g_ref, d_ref, m2_ref, v2_ref):
        g = p_ref[0].astype(F32)
        for sender in range(1, n):
            g = g + p_ref[sender].astype(F32)
        g_ref[0] = g
        d_ref[0], m2_ref[0], v2_ref[0] = _adamw(w_ref[0], g, m_ref[0], v_ref[0])

    tile = pl.BlockSpec((1, tr, c), lambda i: (0, i, 0))
    return _pcall(
        body, name=name, grid=(r // tr,),
        in_specs=[pl.BlockSpec((n, tr, c), lambda i: (0, i, 0)), tile, tile, tile],
        out_specs=[tile] * 4, out_shape=[jax.ShapeDtypeStruct((1, r, c), F32)] * 4,
        compiler_params=_params("parallel"),
    )(parts, w, m, v)


PACK_ROWS = 32
ROW_SBN, ROW_HGN, ROW_LB, ROW_QG, ROW_KG, ROW_OG, ROW_LOSS = 0, 8, 16, 24, 25, 26, 27


def small_update(g_sbn, g_hgn, g_lb, g_qg, g_kg, g_og, loss_parts, small):
    def gather(sbn_ref, hgn_ref, lb_ref, qg_ref, kg_ref, og_ref, loss_ref, gath, pack, send_sems, recv_sems, local_sems):
        pos = _mesh_pos()
        pack[...] = jnp.zeros_like(pack)
        pack[ROW_SBN:ROW_SBN + 8, :] = sbn_ref[...]
        pack[ROW_HGN:ROW_HGN + 8, :] = hgn_ref[...]
        pack[ROW_LB:ROW_LB + 8, :] = jnp.sum(lb_ref[...], axis=0)
        pack[ROW_QG:ROW_QG + 1, :] = jnp.sum(qg_ref[...], axis=0, keepdims=True)
        pack[ROW_KG:ROW_KG + 1, :] = jnp.sum(kg_ref[...], axis=0, keepdims=True)
        pack[ROW_OG:ROW_OG + 1, :] = jnp.sum(og_ref[...], axis=0, keepdims=True)
        pack[ROW_LOSS:ROW_LOSS + 1, :] = jnp.sum(loss_ref[...], axis=0)[0:1, :]
        _exchange_start([((lambda p: pack), gath)], send_sems, recv_sems, local_sems, pos)
        _exchange_wait([((lambda p: pack), gath)], send_sems, recv_sems, local_sems, pos)

    packs = _pcall(
        gather, name="small_gather",
        in_specs=[_VMEM] * 7, out_specs=_VMEM, out_shape=jax.ShapeDtypeStruct((N_DEV, PACK_ROWS, HEAD_DIM), F32),
        scratch_shapes=[pltpu.VMEM((PACK_ROWS, HEAD_DIM), F32)] + _exchange_sems(1),
    )(g_sbn, g_hgn, g_lb, g_qg, g_kg, g_og, loss_parts)

    def apply(gath, *refs):
        wmv, outs, tot = refs[:len(small)], refs[len(small):-1], refs[-1]
        me = _linear(_mesh_pos())
        total = gath[0]
        for dev in range(1, N_DEV):
            total = total + gath[dev]
        tot[...] = total
        outs[0][...] = jnp.broadcast_to(tot[ROW_LOSS:ROW_LOSS + 1, :], (8, HEAD_DIM))
        logits_ref, g_logits_ref = wmv[15], outs[1 + 4 * 5]
        for h in range(N_HEADS):
            lanes = slice(h * HEAD_DIM, (h + 1) * HEAD_DIM)
            p1, p0 = _sigmoid_pair(logits_ref[1:2, lanes] - logits_ref[0:1, lanes])
            d_l1 = p0 * p1 * tot[ROW_LB + h:ROW_LB + h + 1, :]
            g_logits_ref[0:1, lanes] = -d_l1
            g_logits_ref[1:2, lanes] = d_l1
        grads = [tot[ROW_SBN:ROW_SBN + 8, :], tot[ROW_QG:ROW_QG + 1, :], tot[ROW_KG:ROW_KG + 1, :],
                 tot[pl.ds(ROW_HGN + me, 1), :], tot[ROW_OG:ROW_OG + 1, :], g_logits_ref[...]]
        for i, g in enumerate(grads):
            w_ref, m_ref, v_ref = wmv[3 * i:3 * i + 3]
            o = outs[1 + 4 * i:5 + 4 * i]
            if i < 5:
                o[0][...] = g
            o[1][...], o[2][...], o[3][...] = _adamw(w_ref[...], g, m_ref[...], v_ref[...])

    out_shape = [jax.ShapeDtypeStruct((8, HEAD_DIM), F32)]
    for i in range(6):
        out_shape += [jax.ShapeDtypeStruct(small[3 * i].shape, F32)] * 4
    return _pcall(
        apply, name="small_update",
        in_specs=[_VMEM] * (1 + len(small)), out_specs=[_VMEM] * 25, out_shape=out_shape,
        scratch_shapes=[pltpu.VMEM((PACK_ROWS, HEAD_DIM), F32)],
    )(packs, *small)


def kernel(x, sb_norm, sb_w_in, sb_q_gain, sb_k_gain, sb_w_out, hg_norm, hg_w_in, hg_o_gain, hg_w_out, hg_lb_logits, loss_target, m_sb_norm, m_sb_w_in, m_sb_q_gain, m_sb_k_gain, m_sb_w_out, m_hg_norm, m_hg_w_in, m_hg_o_gain, m_hg_w_out, m_hg_lb_logits, v_sb_norm, v_sb_w_in, v_sb_q_gain, v_sb_k_gain, v_sb_w_out, v_hg_norm, v_hg_w_in, v_hg_o_gain, v_hg_w_out, v_hg_lb_logits):
    b = x.shape[0]
    wsi, wso_mine, whi_mine, who_mine, hg_norm_mine = gather_first_weights(
        sb_w_in[0], sb_w_out[0], hg_w_in[0], hg_w_out[0], hg_norm)
    r = local_step(x, loss_target, sb_norm, wsi, sb_q_gain, sb_k_gain, hg_o_gain, hg_lb_logits,
                   wso_mine, whi_mine, who_mine, hg_norm_mine)
    big = {}
    for name, p, w, m, v in (("sb_w_in", r["p_wsi"], sb_w_in, m_sb_w_in, v_sb_w_in),
                             ("sb_w_out", r["p_wso"], sb_w_out, m_sb_w_out, v_sb_w_out),
                             ("hg_w_in", r["p_whi"], hg_w_in, m_hg_w_in, v_hg_w_in),
                             ("hg_w_out", r["p_who"], hg_w_out, m_hg_w_out, v_hg_w_out)):
        big[name] = reduce_adamw(p, w, m, v, "adamw_" + name)

    def rows8(a):
        return a.reshape(8, HEAD_DIM)

    small_in = [rows8(sb_norm), rows8(m_sb_norm), rows8(v_sb_norm),
                sb_q_gain, m_sb_q_gain, v_sb_q_gain,
                sb_k_gain, m_sb_k_gain, v_sb_k_gain,
                hg_norm, m_hg_norm, v_hg_norm,
                hg_o_gain, m_hg_o_gain, v_hg_o_gain,
                hg_lb_logits, m_hg_lb_logits, v_hg_lb_logits]
    so = small_update(rows8(r["g_sbn"]), rows8(r["g_hgn"]), r["g_lb"].reshape(b, N_HEADS, HEAD_DIM),
                      r["g_qg"].reshape(b * N_HEADS, HEAD_DIM), r["g_kg"].reshape(b * N_HEADS, HEAD_DIM),
                      r["g_og"].reshape(b * N_HEADS, HEAD_DIM), r["loss_parts"], small_in)
    loss = so[0][0, 0]
    shapes = {"sb_norm": (1, D_MODEL), "sb_q_gain": (1, HEAD_DIM), "sb_k_gain": (1, HEAD_DIM),
              "hg_norm": (1, HEAD_DIM), "hg_o_gain": (1, HEAD_DIM), "hg_lb_logits": (2, D_MODEL)}
    small = {}
    for i, name in enumerate(("sb_norm", "sb_q_gain", "sb_k_gain", "hg_norm", "hg_o_gain", "hg_lb_logits")):
        small[name] = [o.reshape(shapes[name]) for o in so[1 + 4 * i:5 + 4 * i]]
    order = ("sb_norm", "sb_w_in", "sb_q_gain", "sb_k_gain", "sb_w_out",
             "hg_norm", "hg_w_in", "hg_o_gain", "hg_w_out", "hg_lb_logits")
    res = {**big, **small}
    return (loss, r["gx"]) + tuple(res[n][j] for j in range(4) for n in order)
```
